```python
import jax
import jax.numpy as jnp
from jax import lax
import numpy as np

D_MODEL = 2048
BATCH = 16
SEQ = 2048
DEPTH = 1

HGRN_HEAD_DIM = 128
HGRN_HEADS = D_MODEL // HGRN_HEAD_DIM
HGRN_WIDTH = HGRN_HEADS * HGRN_HEAD_DIM
HGRN_CHUNK = 32
ATTN_GROUPS = ((128, 1), (512, 4), (2048, 16))
ATTN_HEADS_PER_GROUP = 4
HEAD_DIM = 128
ATTN_QKV_WIDTH = len(ATTN_GROUPS) * 3 * ATTN_HEADS_PER_GROUP * HEAD_DIM
ATTN_OUT_WIDTH = ATTN_HEADS_PER_GROUP * HEAD_DIM
ROPE_THETA = 500000.0
ROPE_DIM = HEAD_DIM // 4
N_BRANCHES = 2
IN_COLS = 5 * HGRN_WIDTH + ATTN_QKV_WIDTH + N_BRANCHES * D_MODEL
D_FF = ((8 * D_MODEL // 3 + 255) // 256) * 256
DEEPNORM_ALPHA = (2.0 * DEPTH) ** 0.25
DEEPNORM_BETA = (8.0 * DEPTH) ** -0.25
LN_EPS = 1e-5
NEG_INF = -1e30

kernel_name = 'hybrid_hgrn2_dilated_attn_macaron_deepnorm'


def layer_norm(x, g, b):
    xf = x.astype(jnp.float32)
    mu = jnp.mean(xf, axis=-1, keepdims=True)
    var = jnp.mean(jnp.square(xf - mu), axis=-1, keepdims=True)
    return ((xf - mu) * lax.rsqrt(var + LN_EPS) * g + b).astype(x.dtype)


def swiglu(x, w_in, w_out):
    gate, up = jnp.split(x @ w_in, 2, axis=-1)
    return (jax.nn.silu(gate) * up) @ w_out


def partial_rope(t, pos):
    t = t.astype(jnp.float32)
    inv_freq = ROPE_THETA ** (-jnp.arange(0, ROPE_DIM, 2, dtype=jnp.float32) / ROPE_DIM)
    ang = pos.astype(jnp.float32)[:, None] * inv_freq
    cos = jnp.cos(ang)[None, :, None, None, :]
    sin = jnp.sin(ang)[None, :, None, None, :]
    t1, t2, rest = jnp.split(t, [ROPE_DIM // 2, ROPE_DIM], axis=-1)
    return jnp.concatenate([t1 * cos - t2 * sin, t2 * cos + t1 * sin, rest], axis=-1)


def dilated_window_attention(q, k, v, window, dilation):
    b_, s_, h_, dh = q.shape
    half = window // (2 * dilation)
    seg = s_ // dilation
    blk = half
    n_blk = -(-seg // blk)
    seg_p = n_blk * blk

    def to_residue(t):
        t = t.reshape(b_, seg, dilation, h_, dh).transpose(0, 2, 3, 1, 4)
        return jnp.pad(t, ((0, 0), (0, 0), (0, 0), (0, seg_p - seg), (0, 0)))

    def neighbours(t):
        t = jnp.pad(t, ((0, 0), (0, 0), (0, 0), (blk, blk), (0, 0)))
        t = t.reshape(b_, dilation, h_, n_blk + 2, blk, dh)
        return jnp.concatenate([t[:, :, :, :-2], t[:, :, :, 1:-1], t[:, :, :, 2:]], axis=4)

    qr = to_residue(q).reshape(b_, dilation, h_, n_blk, blk, dh)
    kr = neighbours(to_residue(k))
    vr = neighbours(to_residue(v))
    qi = jnp.arange(seg_p).reshape(n_blk, blk, 1)
    kj = (jnp.arange(n_blk)[:, None, None] - 1) * blk + jnp.arange(3 * blk)[None, None, :]
    valid = (jnp.abs(qi - kj) <= half) & (kj >= 0) & (kj < seg)
    s = jnp.einsum('brhnqe,brhnke->brhnqk', qr, kr).astype(jnp.float32) * (HEAD_DIM ** -0.5)
    s = jnp.where(valid, s, NEG_INF)
    m = jnp.max(s, axis=-1, keepdims=True)
    p = jnp.exp(s - m)
    denom = jnp.sum(p, axis=-1, keepdims=True)
    o = jnp.einsum('brhnqk,brhnke->brhnqe', p, vr.astype(jnp.float32)) / denom
    lse = (m + jnp.log(denom))[..., 0]
    o = o.reshape(b_, dilation, h_, seg_p, dh)[:, :, :, :seg]
    o = o.transpose(0, 3, 1, 2, 4).reshape(b_, s_, h_, dh)
    lse = lse.reshape(b_, dilation, h_, seg_p)[:, :, :, :seg]
    lse = lse.transpose(0, 3, 1, 2).reshape(b_, s_, h_)
    return o, lse


def dilated_attention_mixer(h_qkv):
    b_, s_, _ = h_qkv.shape
    qkv = h_qkv.reshape(b_, s_, len(ATTN_GROUPS), 3, ATTN_HEADS_PER_GROUP, HEAD_DIM)
    pos = jnp.arange(s_)
    q = partial_rope(qkv[:, :, :, 0], pos)
    k = partial_rope(qkv[:, :, :, 1], pos)
    v = qkv[:, :, :, 2]
    outs, lses = [], []
    for g, (window, dilation) in enumerate(ATTN_GROUPS):
        o_g, lse_g = dilated_window_attention(q[:, :, g], k[:, :, g], v[:, :, g], window, dilation)
        outs.append(o_g)
        lses.append(lse_g)
    w = jax.nn.softmax(jnp.stack(lses, axis=0), axis=0)
    o = jnp.sum(w[..., None] * jnp.stack(outs, axis=0), axis=0)
    return o.reshape(b_, s_, ATTN_OUT_WIDTH)


def hgrn2_chunk_scan(q, f, v):
    b_, s_, h_, dk = q.shape
    dv = v.shape[-1]
    n_chunks = s_ // HGRN_CHUNK

    def chunks(t):
        return t.reshape(b_, n_chunks, HGRN_CHUNK, h_, t.shape[-1]).transpose(0, 3, 1, 2, 4)

    qc, fc, vc = chunks(q), chunks(f), chunks(v)
    kc = 1.0 - fc
    cum = jnp.cumsum(jnp.log(fc), axis=3)
    cum_last = cum[:, :, :, -1:]
    q_dec = qc * jnp.exp(cum)
    k_dec = kc * jnp.exp(-cum)
    k_end = kc * jnp.exp(cum_last - cum)
    tril = jnp.tril(jnp.ones((HGRN_CHUNK, HGRN_CHUNK), dtype=bool))
    a = jnp.where(tril, jnp.einsum('bhncd,bhnsd->bhncs', q_dec, k_dec), 0.0)
    o_intra = jnp.einsum('bhncs,bhnsv->bhncv', a, vc)
    decay = jnp.exp(cum_last[:, :, :, 0])

    def step(state, inp):
        q_n, k_n, v_n, dec_n = inp
        o_n = jnp.einsum('bhcd,bhdv->bhcv', q_n, state)
        state = dec_n[..., None] * state + jnp.einsum('bhcd,bhcv->bhdv', k_n, v_n)
        return state, o_n

    xs = (jnp.moveaxis(q_dec, 2, 0), jnp.moveaxis(k_end, 2, 0),
          jnp.moveaxis(vc, 2, 0), jnp.moveaxis(decay, 2, 0))
    _, o_inter = lax.scan(step, jnp.zeros((b_, h_, dk, dv), jnp.float32), xs)
    o = o_intra + jnp.moveaxis(o_inter, 0, 2)
    return o.transpose(0, 2, 3, 1, 4).reshape(b_, s_, h_, dv)


def bidirectional_hgrn2(hq, hf_fwd, hf_bwd, hi, hog, lb_fwd, lb_bwd, layer, norm_g):
    b_, s_, _ = hq.shape

    def heads(t):
        return t.astype(jnp.float32).reshape(b_, s_, HGRN_HEADS, HGRN_HEAD_DIM)

    def forget(hf, lb_table):
        lb = jnp.cumsum(jax.nn.softmax(lb_table.astype(jnp.float32), axis=0), axis=0)[layer]
        return heads(lb + (1.0 - lb) * jax.nn.sigmoid(hf.astype(jnp.float32)))

    q = heads(jax.nn.silu(hq.astype(jnp.float32)))
    i = heads(hi)
    f_f = forget(hf_fwd, lb_fwd)
    f_b = forget(hf_bwd, lb_bwd)
    rev = lambda t: jnp.flip(t, axis=1)
    o = hgrn2_chunk_scan(q, f_f, i) + rev(hgrn2_chunk_scan(rev(q), rev(f_b), rev(i)))
    o = o * lax.rsqrt(jnp.mean(jnp.square(o), axis=-1, keepdims=True) + LN_EPS)
    return o.reshape(b_, s_, HGRN_WIDTH) * norm_g * jax.nn.silu(hog.astype(jnp.float32))


def hybrid_mixer(h, w_in, lb_fwd, lb_bwd, layer, hgrn_norm_g, w_a, w_b, w_out):
    proj = h @ w_in
    splits = np.cumsum([HGRN_WIDTH] * 5 + [ATTN_QKV_WIDTH]).tolist()
    hq, hf_fwd, hf_bwd, hi, hog, h_qkv, h_gate = jnp.split(proj, splits, axis=-1)
    y_a = bidirectional_hgrn2(hq, hf_fwd, hf_bwd, hi, hog, lb_fwd, lb_bwd, layer,
                              hgrn_norm_g).astype(h.dtype) @ w_a
    y_b = dilated_attention_mixer(h_qkv).astype(h.dtype) @ w_b
    g_a, g_b = jnp.split(jax.nn.sigmoid(h_gate), N_BRANCHES, axis=-1)
    return (g_a * y_a + g_b * y_b) @ w_out


def _fwd_setup_inputs(seed: int = 0) -> dict:
    key = jax.random.key(seed)
    ks = jax.random.split(key, 18)

    def normal(k, shape):
        return jax.random.normal(k, shape, jnp.float32)

    def dense(k, shape, scale=1.0):
        return normal(k, shape) * (shape[-2] ** -0.5) * scale

    def gain(k, shape):
        return 1.0 + 0.02 * normal(k, shape)

    def bias(k, shape):
        return 0.02 * normal(k, shape)

    return {
        'x': normal(ks[0], (BATCH, SEQ, D_MODEL)),
        'ffn1_w_in': dense(ks[1], (DEPTH, D_MODEL, 2 * D_FF)),
        'ffn1_w_out': dense(ks[2], (DEPTH, D_FF, D_MODEL), DEEPNORM_BETA),
        'ln1_g': gain(ks[3], (DEPTH, D_MODEL)),
        'ln1_b': bias(ks[4], (DEPTH, D_MODEL)),
        'mix_w_in': dense(ks[5], (DEPTH, D_MODEL, IN_COLS)),
        'hgrn_lb_fwd': 0.1 * normal(ks[6], (DEPTH + 1, HGRN_WIDTH)),
        'hgrn_lb_bwd': 0.1 * normal(ks[7], (DEPTH + 1, HGRN_WIDTH)),
        'hgrn_norm_g': gain(ks[8], (DEPTH, HGRN_WIDTH)),
        'w_branch_a': dense(ks[9], (DEPTH, HGRN_WIDTH, D_MODEL), DEEPNORM_BETA),
        'w_branch_b': dense(ks[10], (DEPTH, ATTN_OUT_WIDTH, D_MODEL), DEEPNORM_BETA),
        'mix_w_out': dense(ks[11], (DEPTH, D_MODEL, D_MODEL), DEEPNORM_BETA),
        'ln2_g': gain(ks[12], (DEPTH, D_MODEL)),
        'ln2_b': bias(ks[13], (DEPTH, D_MODEL)),
        'ffn2_w_in': dense(ks[14], (DEPTH, D_MODEL, 2 * D_FF)),
        'ffn2_w_out': dense(ks[15], (DEPTH, D_FF, D_MODEL), DEEPNORM_BETA),
        'ln3_g': gain(ks[16], (DEPTH, D_MODEL)),
        'ln3_b': bias(ks[17], (DEPTH, D_MODEL)),
    }


def _fwd_reference(x, ffn1_w_in, ffn1_w_out, ln1_g, ln1_b, mix_w_in, hgrn_lb_fwd, hgrn_lb_bwd,
              hgrn_norm_g, w_branch_a, w_branch_b, mix_w_out, ln2_g, ln2_b,
              ffn2_w_in, ffn2_w_out, ln3_g, ln3_b):
    h = x
    for layer in range(DEPTH):
        h = layer_norm(DEEPNORM_ALPHA * h + 0.5 * swiglu(h, ffn1_w_in[layer], ffn1_w_out[layer]),
                       ln1_g[layer], ln1_b[layer])
        mix = hybrid_mixer(h, mix_w_in[layer], hgrn_lb_fwd, hgrn_lb_bwd, layer, hgrn_norm_g[layer],
                           w_branch_a[layer], w_branch_b[layer], mix_w_out[layer])
        h = layer_norm(DEEPNORM_ALPHA * h + mix, ln2_g[layer], ln2_b[layer])
        h = layer_norm(DEEPNORM_ALPHA * h + 0.5 * swiglu(h, ffn2_w_in[layer], ffn2_w_out[layer]),
                       ln3_g[layer], ln3_b[layer])
    return h


import jax as _jax
import jax.numpy as _jnp

TWIN_FORMAT = 'train_step'
FWD_PARAMS = ['x', 'ffn1_w_in', 'ffn1_w_out', 'ln1_g', 'ln1_b', 'mix_w_in', 'hgrn_lb_fwd', 'hgrn_lb_bwd', 'hgrn_norm_g', 'w_branch_a', 'w_branch_b', 'mix_w_out', 'ln2_g', 'ln2_b', 'ffn2_w_in', 'ffn2_w_out', 'ln3_g', 'ln3_b']
TWIN_WEIGHTS = ['ffn1_w_in', 'ffn1_w_out', 'ln1_g', 'ln1_b', 'mix_w_in', 'hgrn_lb_fwd', 'hgrn_lb_bwd', 'hgrn_norm_g', 'w_branch_a', 'w_branch_b', 'mix_w_out', 'ln2_g', 'ln2_b', 'ffn2_w_in', 'ffn2_w_out', 'ln3_g', 'ln3_b']
TWIN_DIFF_INPUT = 'x'
TWIN_INPUTS = ['x', 'ffn1_w_in', 'ffn1_w_out', 'ln1_g', 'ln1_b', 'mix_w_in', 'hgrn_lb_fwd', 'hgrn_lb_bwd', 'hgrn_norm_g', 'w_branch_a', 'w_branch_b', 'mix_w_out', 'ln2_g', 'ln2_b', 'ffn2_w_in', 'ffn2_w_out', 'ln3_g', 'ln3_b', 'loss_target', 'm_ffn1_w_in', 'm_ffn1_w_out', 'm_ln1_g', 'm_ln1_b', 'm_mix_w_in', 'm_hgrn_lb_fwd', 'm_hgrn_lb_bwd', 'm_hgrn_norm_g', 'm_w_branch_a', 'm_w_branch_b', 'm_mix_w_out', 'm_ln2_g', 'm_ln2_b', 'm_ffn2_w_in', 'm_ffn2_w_out', 'm_ln3_g', 'm_ln3_b', 'v_ffn1_w_in', 'v_ffn1_w_out', 'v_ln1_g', 'v_ln1_b', 'v_mix_w_in', 'v_hgrn_lb_fwd', 'v_hgrn_lb_bwd', 'v_hgrn_norm_g', 'v_w_branch_a', 'v_w_branch_b', 'v_mix_w_out', 'v_ln2_g', 'v_ln2_b', 'v_ffn2_w_in', 'v_ffn2_w_out', 'v_ln3_g', 'v_ln3_b']
TWIN_OUTPUTS = ['loss', 'grad_x', 'grad_ffn1_w_in', 'grad_ffn1_w_out', 'grad_ln1_g', 'grad_ln1_b', 'grad_mix_w_in', 'grad_hgrn_lb_fwd', 'grad_hgrn_lb_bwd', 'grad_hgrn_norm_g', 'grad_w_branch_a', 'grad_w_branch_b', 'grad_mix_w_out', 'grad_ln2_g', 'grad_ln2_b', 'grad_ffn2_w_in', 'grad_ffn2_w_out', 'grad_ln3_g', 'grad_ln3_b', 'delta_ffn1_w_in', 'delta_ffn1_w_out', 'delta_ln1_g', 'delta_ln1_b', 'delta_mix_w_in', 'delta_hgrn_lb_fwd', 'delta_hgrn_lb_bwd', 'delta_hgrn_norm_g', 'delta_w_branch_a', 'delta_w_branch_b', 'delta_mix_w_out', 'delta_ln2_g', 'delta_ln2_b', 'delta_ffn2_w_in', 'delta_ffn2_w_out', 'delta_ln3_g', 'delta_ln3_b', 'new_m_ffn1_w_in', 'new_m_ffn1_w_out', 'new_m_ln1_g', 'new_m_ln1_b', 'new_m_mix_w_in', 'new_m_hgrn_lb_fwd', 'new_m_hgrn_lb_bwd', 'new_m_hgrn_norm_g', 'new_m_w_branch_a', 'new_m_w_branch_b', 'new_m_mix_w_out', 'new_m_ln2_g', 'new_m_ln2_b', 'new_m_ffn2_w_in', 'new_m_ffn2_w_out', 'new_m_ln3_g', 'new_m_ln3_b', 'new_v_ffn1_w_in', 'new_v_ffn1_w_out', 'new_v_ln1_g', 'new_v_ln1_b', 'new_v_mix_w_in', 'new_v_hgrn_lb_fwd', 'new_v_hgrn_lb_bwd', 'new_v_hgrn_norm_g', 'new_v_w_branch_a', 'new_v_w_branch_b', 'new_v_mix_w_out', 'new_v_ln2_g', 'new_v_ln2_b', 'new_v_ffn2_w_in', 'new_v_ffn2_w_out', 'new_v_ln3_g', 'new_v_ln3_b']
TWIN_LEAF_KINDS = {'loss': 'loss', 'grad_x': 'grad_x', 'grad_ffn1_w_in': 'grad_w', 'grad_ffn1_w_out': 'grad_w', 'grad_ln1_g': 'grad_w', 'grad_ln1_b': 'grad_w', 'grad_mix_w_in': 'grad_w', 'grad_hgrn_lb_fwd': 'grad_w', 'grad_hgrn_lb_bwd': 'grad_w', 'grad_hgrn_norm_g': 'grad_w', 'grad_w_branch_a': 'grad_w', 'grad_w_branch_b': 'grad_w', 'grad_mix_w_out': 'grad_w', 'grad_ln2_g': 'grad_w', 'grad_ln2_b': 'grad_w', 'grad_ffn2_w_in': 'grad_w', 'grad_ffn2_w_out': 'grad_w', 'grad_ln3_g': 'grad_w', 'grad_ln3_b': 'grad_w', 'delta_ffn1_w_in': 'delta_w', 'delta_ffn1_w_out': 'delta_w', 'delta_ln1_g': 'delta_w', 'delta_ln1_b': 'delta_w', 'delta_mix_w_in': 'delta_w', 'delta_hgrn_lb_fwd': 'delta_w', 'delta_hgrn_lb_bwd': 'delta_w', 'delta_hgrn_norm_g': 'delta_w', 'delta_w_branch_a': 'delta_w', 'delta_w_branch_b': 'delta_w', 'delta_mix_w_out': 'delta_w', 'delta_ln2_g': 'delta_w', 'delta_ln2_b': 'delta_w', 'delta_ffn2_w_in': 'delta_w', 'delta_ffn2_w_out': 'delta_w', 'delta_ln3_g': 'delta_w', 'delta_ln3_b': 'delta_w', 'new_m_ffn1_w_in': 'new_m', 'new_m_ffn1_w_out': 'new_m', 'new_m_ln1_g': 'new_m', 'new_m_ln1_b': 'new_m', 'new_m_mix_w_in': 'new_m', 'new_m_hgrn_lb_fwd': 'new_m', 'new_m_hgrn_lb_bwd': 'new_m', 'new_m_hgrn_norm_g': 'new_m', 'new_m_w_branch_a': 'new_m', 'new_m_w_branch_b': 'new_m', 'new_m_mix_w_out': 'new_m', 'new_m_ln2_g': 'new_m', 'new_m_ln2_b': 'new_m', 'new_m_ffn2_w_in': 'new_m', 'new_m_ffn2_w_out': 'new_m', 'new_m_ln3_g': 'new_m', 'new_m_ln3_b': 'new_m', 'new_v_ffn1_w_in': 'new_v', 'new_v_ffn1_w_out': 'new_v', 'new_v_ln1_g': 'new_v', 'new_v_ln1_b': 'new_v', 'new_v_mix_w_in': 'new_v', 'new_v_hgrn_lb_fwd': 'new_v', 'new_v_hgrn_lb_bwd': 'new_v', 'new_v_hgrn_norm_g': 'new_v', 'new_v_w_branch_a': 'new_v', 'new_v_w_branch_b': 'new_v', 'new_v_mix_w_out': 'new_v', 'new_v_ln2_g': 'new_v', 'new_v_ln2_b': 'new_v', 'new_v_ffn2_w_in': 'new_v', 'new_v_ffn2_w_out': 'new_v', 'new_v_ln3_g': 'new_v', 'new_v_ln3_b': 'new_v'}


def _forward(args):
    return _fwd_reference(*[args[k] for k in FWD_PARAMS])


def _output_shape():
    out = _jax.eval_shape(lambda: _forward(_fwd_setup_inputs(0)))
    return out.shape, out.dtype

N_MICROBATCH = 1
ADAM_LR = 0.001
ADAM_B1 = 0.9
ADAM_B2 = 0.999
ADAM_EPS = 1e-08
ADAM_WD = 0.01
ADAM_STEP = 10
PER_EXAMPLE_BATCH_AXIS = {'x': 0, 'loss_target': 0}
SHARED_INPUTS = []
_WEIGHT_DTYPES = {'ffn1_w_in': _jnp.float32, 'ffn1_w_out': _jnp.float32, 'ln1_g': _jnp.float32, 'ln1_b': _jnp.float32, 'mix_w_in': _jnp.float32, 'hgrn_lb_fwd': _jnp.float32, 'hgrn_lb_bwd': _jnp.float32, 'hgrn_norm_g': _jnp.float32, 'w_branch_a': _jnp.float32, 'w_branch_b': _jnp.float32, 'mix_w_out': _jnp.float32, 'ln2_g': _jnp.float32, 'ln2_b': _jnp.float32, 'ffn2_w_in': _jnp.float32, 'ffn2_w_out': _jnp.float32, 'ln3_g': _jnp.float32, 'ln3_b': _jnp.float32}
MOMENT_SCALE = {'ffn1_w_in': 8.150883e-03, 'ffn1_w_out': 2.235123e-02, 'ln1_g': 5.112703e-01, 'ln1_b': 2.394977e-01, 'mix_w_in': 4.335725e-03, 'hgrn_lb_fwd': 4.681414e-04, 'hgrn_lb_bwd': 4.712476e-04, 'hgrn_norm_g': 8.678691e-03, 'w_branch_a': 1.443126e-02, 'w_branch_b': 2.895948e-03, 'mix_w_out': 1.472170e-02, 'ln2_g': 5.148720e-01, 'ln2_b': 2.398328e-01, 'ffn2_w_in': 8.031736e-03, 'ffn2_w_out': 2.198688e-02, 'ln3_g': 1.601078e+01, 'ln3_b': 4.215977e-01}


def _to_microbatches(a, axis):
    t = _jnp.moveaxis(a, axis, 0)
    t = t.reshape((N_MICROBATCH, t.shape[0] // N_MICROBATCH) + t.shape[1:])
    return _jnp.moveaxis(t, 1, axis + 1)


def setup_inputs(seed: int = 0) -> dict:
    inp = _fwd_setup_inputs(seed)
    key = _jax.random.fold_in(_jax.random.key(seed), 7919)
    shape, _ = _output_shape()
    out = dict(inp)
    out["loss_target"] = _jax.random.normal(_jax.random.fold_in(key, 0), shape, _jnp.float32)
    for i, name in enumerate(TWIN_WEIGHTS):
        w = inp[name].astype(_jnp.float32)
        if MOMENT_SCALE is None:
            s = _jnp.sqrt(_jnp.mean(_jnp.square(w)) + 1e-30)
        else:
            s = MOMENT_SCALE[name]
        km, kv = _jax.random.split(_jax.random.fold_in(key, i + 1))
        out[name] = w
        out["m_" + name] = s * _jax.random.normal(km, w.shape, _jnp.float32)
        out["v_" + name] = (s * s) * _jax.random.uniform(kv, w.shape, _jnp.float32, 0.5, 1.5)
    if N_MICROBATCH > 1:
        for name, axis in PER_EXAMPLE_BATCH_AXIS.items():
            out[name] = _to_microbatches(out[name], axis)
    return {'x': out['x'], 'ffn1_w_in': out['ffn1_w_in'], 'ffn1_w_out': out['ffn1_w_out'], 'ln1_g': out['ln1_g'], 'ln1_b': out['ln1_b'], 'mix_w_in': out['mix_w_in'], 'hgrn_lb_fwd': out['hgrn_lb_fwd'], 'hgrn_lb_bwd': out['hgrn_lb_bwd'], 'hgrn_norm_g': out['hgrn_norm_g'], 'w_branch_a': out['w_branch_a'], 'w_branch_b': out['w_branch_b'], 'mix_w_out': out['mix_w_out'], 'ln2_g': out['ln2_g'], 'ln2_b': out['ln2_b'], 'ffn2_w_in': out['ffn2_w_in'], 'ffn2_w_out': out['ffn2_w_out'], 'ln3_g': out['ln3_g'], 'ln3_b': out['ln3_b'], 'loss_target': out['loss_target'], 'm_ffn1_w_in': out['m_ffn1_w_in'], 'm_ffn1_w_out': out['m_ffn1_w_out'], 'm_ln1_g': out['m_ln1_g'], 'm_ln1_b': out['m_ln1_b'], 'm_mix_w_in': out['m_mix_w_in'], 'm_hgrn_lb_fwd': out['m_hgrn_lb_fwd'], 'm_hgrn_lb_bwd': out['m_hgrn_lb_bwd'], 'm_hgrn_norm_g': out['m_hgrn_norm_g'], 'm_w_branch_a': out['m_w_branch_a'], 'm_w_branch_b': out['m_w_branch_b'], 'm_mix_w_out': out['m_mix_w_out'], 'm_ln2_g': out['m_ln2_g'], 'm_ln2_b': out['m_ln2_b'], 'm_ffn2_w_in': out['m_ffn2_w_in'], 'm_ffn2_w_out': out['m_ffn2_w_out'], 'm_ln3_g': out['m_ln3_g'], 'm_ln3_b': out['m_ln3_b'], 'v_ffn1_w_in': out['v_ffn1_w_in'], 'v_ffn1_w_out': out['v_ffn1_w_out'], 'v_ln1_g': out['v_ln1_g'], 'v_ln1_b': out['v_ln1_b'], 'v_mix_w_in': out['v_mix_w_in'], 'v_hgrn_lb_fwd': out['v_hgrn_lb_fwd'], 'v_hgrn_lb_bwd': out['v_hgrn_lb_bwd'], 'v_hgrn_norm_g': out['v_hgrn_norm_g'], 'v_w_branch_a': out['v_w_branch_a'], 'v_w_branch_b': out['v_w_branch_b'], 'v_mix_w_out': out['v_mix_w_out'], 'v_ln2_g': out['v_ln2_g'], 'v_ln2_b': out['v_ln2_b'], 'v_ffn2_w_in': out['v_ffn2_w_in'], 'v_ffn2_w_out': out['v_ffn2_w_out'], 'v_ln3_g': out['v_ln3_g'], 'v_ln3_b': out['v_ln3_b']}


def _loss(weights, diff, rest, loss_target):
    with _jax.named_scope("forward"):
        args = {**rest, TWIN_DIFF_INPUT: diff, **{k: w.astype(_WEIGHT_DTYPES[k]) for k, w in weights.items()}}
        y = _forward(args)
    with _jax.named_scope("loss_head"):
        err = _jnp.square(y.astype(_jnp.float32) - loss_target)
        return 0.5 * _jnp.sum(_jnp.mean(err, axis=-1)) if err.ndim else 0.5 * err


def _adamw(w, g, m, v):
    m = ADAM_B1 * m + (1.0 - ADAM_B1) * g
    v = ADAM_B2 * v + (1.0 - ADAM_B2) * _jnp.square(g)
    m_hat = m / (1.0 - ADAM_B1 ** ADAM_STEP)
    v_hat = v / (1.0 - ADAM_B2 ** ADAM_STEP)
    delta = -ADAM_LR * (m_hat / (_jnp.sqrt(v_hat) + ADAM_EPS) + ADAM_WD * w)
    return delta, m, v


def reference(x, ffn1_w_in, ffn1_w_out, ln1_g, ln1_b, mix_w_in, hgrn_lb_fwd, hgrn_lb_bwd, hgrn_norm_g, w_branch_a, w_branch_b, mix_w_out, ln2_g, ln2_b, ffn2_w_in, ffn2_w_out, ln3_g, ln3_b, loss_target, m_ffn1_w_in, m_ffn1_w_out, m_ln1_g, m_ln1_b, m_mix_w_in, m_hgrn_lb_fwd, m_hgrn_lb_bwd, m_hgrn_norm_g, m_w_branch_a, m_w_branch_b, m_mix_w_out, m_ln2_g, m_ln2_b, m_ffn2_w_in, m_ffn2_w_out, m_ln3_g, m_ln3_b, v_ffn1_w_in, v_ffn1_w_out, v_ln1_g, v_ln1_b, v_mix_w_in, v_hgrn_lb_fwd, v_hgrn_lb_bwd, v_hgrn_norm_g, v_w_branch_a, v_w_branch_b, v_mix_w_out, v_ln2_g, v_ln2_b, v_ffn2_w_in, v_ffn2_w_out, v_ln3_g, v_ln3_b):
    given = dict(x=x, ffn1_w_in=ffn1_w_in, ffn1_w_out=ffn1_w_out, ln1_g=ln1_g, ln1_b=ln1_b, mix_w_in=mix_w_in, hgrn_lb_fwd=hgrn_lb_fwd, hgrn_lb_bwd=hgrn_lb_bwd, hgrn_norm_g=hgrn_norm_g, w_branch_a=w_branch_a, w_branch_b=w_branch_b, mix_w_out=mix_w_out, ln2_g=ln2_g, ln2_b=ln2_b, ffn2_w_in=ffn2_w_in, ffn2_w_out=ffn2_w_out, ln3_g=ln3_g, ln3_b=ln3_b, loss_target=loss_target, m_ffn1_w_in=m_ffn1_w_in, m_ffn1_w_out=m_ffn1_w_out, m_ln1_g=m_ln1_g, m_ln1_b=m_ln1_b, m_mix_w_in=m_mix_w_in, m_hgrn_lb_fwd=m_hgrn_lb_fwd, m_hgrn_lb_bwd=m_hgrn_lb_bwd, m_hgrn_norm_g=m_hgrn_norm_g, m_w_branch_a=m_w_branch_a, m_w_branch_b=m_w_branch_b, m_mix_w_out=m_mix_w_out, m_ln2_g=m_ln2_g, m_ln2_b=m_ln2_b, m_ffn2_w_in=m_ffn2_w_in, m_ffn2_w_out=m_ffn2_w_out, m_ln3_g=m_ln3_g, m_ln3_b=m_ln3_b, v_ffn1_w_in=v_ffn1_w_in, v_ffn1_w_out=v_ffn1_w_out, v_ln1_g=v_ln1_g, v_ln1_b=v_ln1_b, v_mix_w_in=v_mix_w_in, v_hgrn_lb_fwd=v_hgrn_lb_fwd, v_hgrn_lb_bwd=v_hgrn_lb_bwd, v_hgrn_norm_g=v_hgrn_norm_g, v_w_branch_a=v_w_branch_a, v_w_branch_b=v_w_branch_b, v_mix_w_out=v_mix_w_out, v_ln2_g=v_ln2_g, v_ln2_b=v_ln2_b, v_ffn2_w_in=v_ffn2_w_in, v_ffn2_w_out=v_ffn2_w_out, v_ln3_g=v_ln3_g, v_ln3_b=v_ln3_b)
    weights = {n: given[n] for n in TWIN_WEIGHTS}
    shared = {n: given[n] for n in SHARED_INPUTS}
    per_example = {n: given[n] for n in ['x']}
    grad_fn = _jax.value_and_grad(_loss, argnums=(0, 1))

    def one_microbatch(ex, loss_target):
        ex = dict(ex)
        diff = ex.pop(TWIN_DIFF_INPUT)
        return grad_fn(weights, diff, {**shared, **ex}, loss_target)

    if N_MICROBATCH == 1:
        loss, (grad_w, grad_x) = one_microbatch(per_example, given["loss_target"])
    else:
        def body(carry, xs):
            loss_sum, grad_sum = carry
            l_k, (gw_k, gx_k) = one_microbatch(xs[0], xs[1])
            with _jax.named_scope("update"):
                return (loss_sum + l_k, _jax.tree.map(_jnp.add, grad_sum, gw_k)), gx_k

        init = (_jnp.zeros((), _jnp.float32), _jax.tree.map(_jnp.zeros_like, weights))
        (loss, grad_w), grad_x = _jax.lax.scan(body, init, (per_example, given["loss_target"]))
    with _jax.named_scope("update"):
        delta_w, new_m, new_v = {}, {}, {}
        for n in TWIN_WEIGHTS:
            delta_w[n], new_m[n], new_v[n] = _adamw(weights[n], grad_w[n], given["m_" + n], given["v_" + n])
    return (loss, grad_x, *[grad_w[n] for n in TWIN_WEIGHTS], *[delta_w[n] for n in TWIN_WEIGHTS],
            *[new_m[n] for n in TWIN_WEIGHTS], *[new_v[n] for n in TWIN_WEIGHTS])
```

```python
import functools

import jax
import jax.numpy as jnp
from jax import lax
from jax.experimental import pallas as pl
from jax.experimental.pallas import tpu as pltpu

F32 = jnp.float32
BF16 = jnp.bfloat16

N_DEV = 8
HEAD = 128
CHUNK = 32
ATTN_GROUPS = ((128, 1), (512, 4), (2048, 16))
ATTN_HEADS = 4
ATTN_HALF = 64
QKV_GROUP = 3 * ATTN_HEADS * HEAD
QKV_WIDTH = len(ATTN_GROUPS) * QKV_GROUP
ATTN_OUT = ATTN_HEADS * HEAD
ROPE_THETA = 500000.0
ROPE_DIM = HEAD // 4
ALPHA = 2.0 ** 0.25
LN_EPS = 1e-5
NEG_INF = -1e30
ADAM_LR, ADAM_B1, ADAM_B2, ADAM_EPS, ADAM_WD, ADAM_STEP = 0.001, 0.9, 0.999, 1e-08, 0.01, 10
VMEM_LIMIT = 56 * 1024 * 1024

NT = (((1,), (1,)), ((), ()))
NN = (((1,), (0,)), ((), ()))
TN = (((0,), (0,)), ((), ()))
MESH = pl.DeviceIdType.MESH


def _dot(a, b, dims):
    return lax.dot_general(a, b, dims, preferred_element_type=F32)


def _tile(n, pref, mult=128):
    if n <= pref:
        return n
    t = (pref // mult) * mult
    while t >= mult:
        if n % t == 0:
            return t
        t -= mult
    return n


def _tile_multi(ns, pref, mult=128):
    t = (pref // mult) * mult
    while t >= mult:
        if all(n % t == 0 for n in ns):
            return t
        t -= mult
    raise ValueError(f"no common tile for {ns}")


def _params(**kw):
    return pltpu.CompilerParams(vmem_limit_bytes=VMEM_LIMIT, **kw)


def _sigmoid(x):
    return jax.nn.sigmoid(x)


def _dsilu(x, s):
    return s * (1.0 + x * (1.0 - s))


def _ln_stats(r):
    mu = jnp.mean(r, axis=-1, keepdims=True)
    xc = r - mu
    var = jnp.mean(xc * xc, axis=-1, keepdims=True)
    rstd = lax.rsqrt(var + LN_EPS)
    return xc * rstd, rstd


def _ln_bwd(dy, xhat, rstd, g):
    dyg = dy * g
    m1 = jnp.mean(dyg, axis=-1, keepdims=True)
    m2 = jnp.mean(dyg * xhat, axis=-1, keepdims=True)
    return rstd * (dyg - m1 - xhat * m2)


def _ffn_in_fwd(xb, w_t, name):
    t_, d_ = xb.shape
    f_ = w_t.shape[0] // 2
    tm, tn = _tile(t_, 512, 8), _tile(f_, 512)
    nj = f_ // tn

    def body(x_ref, wg_ref, wu_ref, g_ref, u_ref, a_ref):
        x = x_ref[...]
        g = _dot(x, wg_ref[...], NT)
        u = _dot(x, wu_ref[...], NT)
        g_ref[...] = g
        u_ref[...] = u
        a_ref[...] = (g * _sigmoid(g) * u).astype(BF16)

    return pl.pallas_call(
        body, name=name, grid=(t_ // tm, nj),
        in_specs=[pl.BlockSpec((tm, d_), lambda i, j: (i, 0)),
                  pl.BlockSpec((tn, d_), lambda i, j: (j, 0)),
                  pl.BlockSpec((tn, d_), lambda i, j: (j + nj, 0))],
        out_specs=[pl.BlockSpec((tm, tn), lambda i, j: (i, j))] * 3,
        out_shape=[jax.ShapeDtypeStruct((t_, f_), F32), jax.ShapeDtypeStruct((t_, f_), F32),
                   jax.ShapeDtypeStruct((t_, f_), BF16)],
        compiler_params=_params(),
    )(xb, w_t, w_t)


def _mm_res_ln_fwd(a, w, res, g, b, scale, name):
    t_, k_ = a.shape
    d_ = w.shape[1]
    tm, tk = _tile(t_, 512, 8), _tile(k_, 512)
    nk = k_ // tk

    def body(a_ref, w_ref, res_ref, g_ref, b_ref, r_ref, h_ref, hb_ref, acc):
        k = pl.program_id(1)

        @pl.when(k == 0)
        def _():
            acc[...] = jnp.zeros_like(acc)

        acc[...] += _dot(a_ref[...], w_ref[...], NN)

        @pl.when(k == nk - 1)
        def _():
            r = ALPHA * res_ref[...] + scale * acc[...]
            xhat, _ = _ln_stats(r)
            h = xhat * g_ref[...] + b_ref[...]
            r_ref[...] = r
            h_ref[...] = h
            hb_ref[...] = h.astype(BF16)

    row = pl.BlockSpec((tm, d_), lambda i, k: (i, 0))
    vec = pl.BlockSpec((1, d_), lambda i, k: (0, 0))
    return pl.pallas_call(
        body, name=name, grid=(t_ // tm, nk),
        in_specs=[pl.BlockSpec((tm, tk), lambda i, k: (i, k)),
                  pl.BlockSpec((tk, d_), lambda i, k: (k, 0)), row, vec, vec],
        out_specs=[row, row, row],
        out_shape=[jax.ShapeDtypeStruct((t_, d_), F32), jax.ShapeDtypeStruct((t_, d_), F32),
                   jax.ShapeDtypeStruct((t_, d_), BF16)],
        scratch_shapes=[pltpu.VMEM((tm, d_), F32)],
        compiler_params=_params(),
    )(a, w, res, g, b)


def _mm_nt(a, w_t, out_dtype, name):
    t_, k_ = a.shape
    n_ = w_t.shape[0]
    tm, tn = _tile(t_, 512, 8), _tile(n_, 512)

    def body(a_ref, w_ref, o_ref):
        o_ref[...] = _dot(a_ref[...], w_ref[...], NT).astype(out_dtype)

    return pl.pallas_call(
        body, name=name, grid=(t_ // tm, n_ // tn),
        in_specs=[pl.BlockSpec((tm, k_), lambda i, j: (i, 0)),
                  pl.BlockSpec((tn, k_), lambda i, j: (j, 0))],
        out_specs=pl.BlockSpec((tm, tn), lambda i, j: (i, j)),
        out_shape=jax.ShapeDtypeStruct((t_, n_), out_dtype),
        compiler_params=_params(),
    )(a, w_t)


def _mm_nn(a, w, out_dtype, name):
    t_, k_ = a.shape
    n_ = w.shape[1]
    tm, tn = _tile(t_, 512, 8), _tile(n_, 512)

    def body(a_ref, w_ref, o_ref):
        o_ref[...] = _dot(a_ref[...], w_ref[...], NN).astype(out_dtype)

    return pl.pallas_call(
        body, name=name, grid=(t_ // tm, n_ // tn),
        in_specs=[pl.BlockSpec((tm, k_), lambda i, j: (i, 0)),
                  pl.BlockSpec((k_, tn), lambda i, j: (0, j))],
        out_specs=pl.BlockSpec((tm, tn), lambda i, j: (i, j)),
        out_shape=jax.ShapeDtypeStruct((t_, n_), out_dtype),
        compiler_params=_params(),
    )(a, w)


def _mm_tn(a, b, scale, name):
    t_, m_ = a.shape
    n_ = b.shape[1]
    tm, tn, tk = _tile(m_, 512), _tile(n_, 2048), _tile(t_, 512, 8)
    nk = t_ // tk

    def body(a_ref, b_ref, o_ref, acc):
        k = pl.program_id(2)

        @pl.when(k == 0)
        def _():
            acc[...] = jnp.zeros_like(acc)

        acc[...] += _dot(a_ref[...], b_ref[...], TN)

        @pl.when(k == nk - 1)
        def _():
            o_ref[...] = (scale * acc[...]).astype(BF16)

    return pl.pallas_call(
        body, name=name, grid=(m_ // tm, n_ // tn, nk),
        in_specs=[pl.BlockSpec((tk, tm), lambda i, j, k: (k, i)),
                  pl.BlockSpec((tk, tn), lambda i, j, k: (k, j))],
        out_specs=pl.BlockSpec((tm, tn), lambda i, j, k: (i, j)),
        out_shape=jax.ShapeDtypeStruct((m_, n_), BF16),
        scratch_shapes=[pltpu.VMEM((tm, tn), F32)],
        compiler_params=_params(),
    )(a, b)


def _gate_out_fwd(ya_in, ob, wa, wb_t, proj, d_, name):
    t_ = ya_in.shape[0]
    goff = 5 * d_ + QKV_WIDTH
    tm, tn = _tile(t_, 512, 8), _tile_multi([d_, goff], 512)
    ja, jb = goff // tn, (goff + d_) // tn

    def body(ya_ref, ob_ref, wa_ref, wb_ref, ga_ref, gb_ref, yao_ref, ybo_ref, z_ref):
        y_a = _dot(ya_ref[...], wa_ref[...], NN)
        y_b = _dot(ob_ref[...], wb_ref[...], NT)
        yao_ref[...] = y_a
        ybo_ref[...] = y_b
        z_ref[...] = (_sigmoid(ga_ref[...]) * y_a + _sigmoid(gb_ref[...]) * y_b).astype(BF16)

    tile = pl.BlockSpec((tm, tn), lambda i, j: (i, j))
    return pl.pallas_call(
        body, name=name, grid=(t_ // tm, d_ // tn),
        in_specs=[pl.BlockSpec((tm, d_), lambda i, j: (i, 0)),
                  pl.BlockSpec((tm, ATTN_OUT), lambda i, j: (i, 0)),
                  pl.BlockSpec((d_, tn), lambda i, j: (0, j)),
                  pl.BlockSpec((tn, ATTN_OUT), lambda i, j: (j, 0)),
                  pl.BlockSpec((tm, tn), lambda i, j: (i, ja + j)),
                  pl.BlockSpec((tm, tn), lambda i, j: (i, jb + j))],
        out_specs=[tile, tile, tile],
        out_shape=[jax.ShapeDtypeStruct((t_, d_), F32), jax.ShapeDtypeStruct((t_, d_), F32),
                   jax.ShapeDtypeStruct((t_, d_), BF16)],
        compiler_params=_params(),
    )(ya_in, ob, wa, wb_t, proj, proj)


def _ffn_mid_bwd(drb, w_out, gate, up, scale, name):
    t_, d_ = drb.shape
    f_ = w_out.shape[0]
    tm, tn = _tile(t_, 512, 8), _tile(f_, 512)

    def body(dr_ref, w_ref, g_ref, u_ref, dg_ref, du_ref):
        da = scale * _dot(dr_ref[...], w_ref[...], NT)
        g = g_ref[...]
        s = _sigmoid(g)
        dg_ref[...] = (da * u_ref[...] * _dsilu(g, s)).astype(BF16)
        du_ref[...] = (da * g * s).astype(BF16)

    tile = pl.BlockSpec((tm, tn), lambda i, j: (i, j))
    return pl.pallas_call(
        body, name=name, grid=(t_ // tm, f_ // tn),
        in_specs=[pl.BlockSpec((tm, d_), lambda i, j: (i, 0)),
                  pl.BlockSpec((tn, d_), lambda i, j: (j, 0)), tile, tile],
        out_specs=[tile, tile],
        out_shape=[jax.ShapeDtypeStruct((t_, f_), BF16)] * 2,
        compiler_params=_params(),
    )(drb, w_out, gate, up)


def _mm_nn_res_lnbwd(a, w, dres, r, g, name):
    t_, k_ = a.shape
    d_ = w.shape[1]
    tm, tk = _tile(t_, 512, 8), _tile(k_, 512)
    nk = k_ // tk

    def body(a_ref, w_ref, dres_ref, r_ref, g_ref, dr_ref, drb_ref, dg_ref, db_ref, acc):
        i, k = pl.program_id(0), pl.program_id(1)

        @pl.when(k == 0)
        def _():
            acc[...] = jnp.zeros_like(acc)

        @pl.when((i == 0) & (k == 0))
        def _():
            dg_ref[...] = jnp.zeros_like(dg_ref)
            db_ref[...] = jnp.zeros_like(db_ref)

        acc[...] += _dot(a_ref[...], w_ref[...], NN)

        @pl.when(k == nk - 1)
        def _():
            dy = acc[...] + ALPHA * dres_ref[...]
            xhat, rstd = _ln_stats(r_ref[...])
            dr = _ln_bwd(dy, xhat, rstd, g_ref[...])
            dr_ref[...] = dr
            drb_ref[...] = dr.astype(BF16)
            dg_ref[...] += jnp.sum(dy * xhat, axis=0, keepdims=True)
            db_ref[...] += jnp.sum(dy, axis=0, keepdims=True)

    row = pl.BlockSpec((tm, d_), lambda i, k: (i, 0))
    vec = pl.BlockSpec((1, d_), lambda i, k: (0, 0))
    return pl.pallas_call(
        body, name=name, grid=(t_ // tm, nk),
        in_specs=[pl.BlockSpec((tm, tk), lambda i, k: (i, k)),
                  pl.BlockSpec((tk, d_), lambda i, k: (k, 0)), row, row, vec],
        out_specs=[row, row, vec, vec],
        out_shape=[jax.ShapeDtypeStruct((t_, d_), F32), jax.ShapeDtypeStruct((t_, d_), BF16),
                   jax.ShapeDtypeStruct((1, d_), F32), jax.ShapeDtypeStruct((1, d_), F32)],
        scratch_shapes=[pltpu.VMEM((tm, d_), F32)],
        compiler_params=_params(),
    )(a, w, dres, r, g)


def _mm_nn_res(a, w, dres, name):
    t_, k_ = a.shape
    d_ = w.shape[1]
    tm, tk = _tile(t_, 512, 8), _tile(k_, 512)
    nk = k_ // tk

    def body(a_ref, w_ref, dres_ref, o_ref, acc):
        k = pl.program_id(1)

        @pl.when(k == 0)
        def _():
            acc[...] = jnp.zeros_like(acc)

        acc[...] += _dot(a_ref[...], w_ref[...], NN)

        @pl.when(k == nk - 1)
        def _():
            o_ref[...] = acc[...] + ALPHA * dres_ref[...]

    row = pl.BlockSpec((tm, d_), lambda i, k: (i, 0))
    return pl.pallas_call(
        body, name=name, grid=(t_ // tm, nk),
        in_specs=[pl.BlockSpec((tm, tk), lambda i, k: (i, k)),
                  pl.BlockSpec((tk, d_), lambda i, k: (k, 0)), row],
        out_specs=row,
        out_shape=jax.ShapeDtypeStruct((t_, d_), F32),
        scratch_shapes=[pltpu.VMEM((tm, d_), F32)],
        compiler_params=_params(),
    )(a, w, dres)


def _dz_gate_bwd(drb, w_out, proj, ya, yb, d_, name):
    t_ = drb.shape[0]
    goff = 5 * d_ + QKV_WIDTH
    tm, tn = _tile(t_, 512, 8), _tile_multi([d_, goff], 512)
    ja, jb = goff // tn, (goff + d_) // tn

    def body(dr_ref, w_ref, ga_ref, gb_ref, ya_ref, yb_ref, dya_ref, dyb_ref, dga_ref, dgb_ref):
        dz = _dot(dr_ref[...], w_ref[...], NT)
        sa, sb = _sigmoid(ga_ref[...]), _sigmoid(gb_ref[...])
        dya_ref[...] = (dz * sa).astype(BF16)
        dyb_ref[...] = (dz * sb).astype(BF16)
        dga_ref[...] = (dz * ya_ref[...] * sa * (1.0 - sa)).astype(BF16)
        dgb_ref[...] = (dz * yb_ref[...] * sb * (1.0 - sb)).astype(BF16)

    tile = pl.BlockSpec((tm, tn), lambda i, j: (i, j))
    return pl.pallas_call(
        body, name=name, grid=(t_ // tm, d_ // tn),
        in_specs=[pl.BlockSpec((tm, d_), lambda i, j: (i, 0)),
                  pl.BlockSpec((tn, d_), lambda i, j: (j, 0)),
                  pl.BlockSpec((tm, tn), lambda i, j: (i, ja + j)),
                  pl.BlockSpec((tm, tn), lambda i, j: (i, jb + j)), tile, tile],
        out_specs=[tile] * 4,
        out_shape=[jax.ShapeDtypeStruct((t_, d_), BF16)] * 4,
        compiler_params=_params(),
    )(drb, w_out, proj, proj, ya, yb)


def _ln_loss_bwd(r, target, g, b, name):
    t_, d_ = r.shape
    tm = _tile(t_, 256, 8)

    def body(r_ref, t_ref, g_ref, b_ref, dr_ref, drb_ref, dg_ref, db_ref, loss_ref):
        i = pl.program_id(0)

        @pl.when(i == 0)
        def _():
            dg_ref[...] = jnp.zeros_like(dg_ref)
            db_ref[...] = jnp.zeros_like(db_ref)
            loss_ref[...] = jnp.zeros_like(loss_ref)

        xhat, rstd = _ln_stats(r_ref[...])
        gain = g_ref[...]
        err = xhat * gain + b_ref[...] - t_ref[...]
        loss_ref[...] += (0.5 / d_) * jnp.sum(err * err)
        dy = err * (1.0 / d_)
        dr = _ln_bwd(dy, xhat, rstd, gain)
        dr_ref[...] = dr
        drb_ref[...] = dr.astype(BF16)
        dg_ref[...] += jnp.sum(dy * xhat, axis=0, keepdims=True)
        db_ref[...] += jnp.sum(dy, axis=0, keepdims=True)

    row = pl.BlockSpec((tm, d_), lambda i: (i, 0))
    vec = pl.BlockSpec((1, d_), lambda i: (0, 0))
    return pl.pallas_call(
        body, name=name, grid=(t_ // tm,),
        in_specs=[row, row, vec, vec],
        out_specs=[row, row, vec, vec, pl.BlockSpec((1, HEAD), lambda i: (0, 0))],
        out_shape=[jax.ShapeDtypeStruct((t_, d_), F32), jax.ShapeDtypeStruct((t_, d_), BF16),
                   jax.ShapeDtypeStruct((1, d_), F32), jax.ShapeDtypeStruct((1, d_), F32),
                   jax.ShapeDtypeStruct((1, HEAD), F32)],
        compiler_params=_params(),
    )(r, target, g, b)


def _chunk_scan(x, row, reverse, size):
    s = 1
    while s < CHUNK:
        if reverse:
            x = x + jnp.where(row < CHUNK - s, pltpu.roll(x, size - s, 0), 0.0)
        else:
            x = x + jnp.where(row >= s, pltpu.roll(x, s, 0), 0.0)
        s *= 2
    return x


def _lower_bound(tab):
    return _sigmoid(tab[0:1, :] - tab[1:2, :])


def _tri_mask(reverse):
    r = lax.broadcasted_iota(jnp.int32, (CHUNK, CHUNK), 0)
    c = lax.broadcasted_iota(jnp.int32, (CHUNK, CHUNK), 1)
    return (c >= r) if reverse else (r >= c)


def _hgrn_fwd(proj, lbf, lbb, ng, b_, s_, d_, name):
    h_ = d_ // HEAD
    nc = s_ // CHUNK

    def body(hq_ref, hff_ref, hfb_ref, hi_ref, hog_ref, lbf_ref, lbb_ref, ng_ref, ya_ref, o_ref,
             q_s, k_s, cum_s, o_s):
        row = lax.broadcasted_iota(jnp.int32, (s_, HEAD), 0) % CHUNK
        hq = hq_ref[...]
        q_s[...] = hq * _sigmoid(hq)
        o_s[...] = jnp.zeros_like(o_s)
        for reverse, hf_ref, lb_ref in ((False, hff_ref, lbf_ref), (True, hfb_ref, lbb_ref)):
            lb = _lower_bound(lb_ref[...])
            f = lb + (1.0 - lb) * _sigmoid(hf_ref[...])
            k_s[...] = 1.0 - f
            cum_s[...] = _chunk_scan(jnp.log(f), row, reverse, s_)
            mask = _tri_mask(reverse)

            def step(n, st, reverse=reverse, mask=mask):
                idx = (nc - 1 - n) if reverse else n
                sl = pl.ds(pl.multiple_of(idx * CHUNK, CHUNK), CHUNK)
                cm = cum_s[sl, :]
                tot = cm[0:1, :] if reverse else cm[CHUNK - 1:CHUNK, :]
                qc, kc = q_s[sl, :], k_s[sl, :]
                vb = hi_ref[sl, :].astype(BF16)
                qd = (qc * jnp.exp(cm)).astype(BF16)
                kd = (kc * jnp.exp(-cm)).astype(BF16)
                ke = (kc * jnp.exp(tot - cm)).astype(BF16)
                a = jnp.where(mask, _dot(qd, kd, NT), 0.0)
                o_s[sl, :] += _dot(a.astype(BF16), vb, NN) + _dot(qd, st.astype(BF16), NT)
                return st * jnp.exp(tot) + _dot(vb, ke, TN)

            lax.fori_loop(0, nc, step, jnp.zeros((HEAD, HEAD), F32))
        o = o_s[...]
        o_ref[...] = o
        nrm = o * lax.rsqrt(jnp.mean(o * o, axis=-1, keepdims=True) + LN_EPS)
        hog = hog_ref[...]
        ya_ref[...] = (nrm * ng_ref[...] * hog * _sigmoid(hog)).astype(BF16)

    def col(part):
        return pl.BlockSpec((s_, HEAD), lambda h, b, part=part: (b, part * h_ + h))

    tab = pl.BlockSpec((2, HEAD), lambda h, b: (0, h))
    out = pl.BlockSpec((s_, HEAD), lambda h, b: (b, h))
    return pl.pallas_call(
        body, name=name, grid=(h_, b_),
        in_specs=[col(0), col(1), col(2), col(3), col(4), tab, tab,
                  pl.BlockSpec((1, HEAD), lambda h, b: (0, h))],
        out_specs=[out, out],
        out_shape=[jax.ShapeDtypeStruct((b_ * s_, d_), BF16), jax.ShapeDtypeStruct((b_ * s_, d_), F32)],
        scratch_shapes=[pltpu.VMEM((s_, HEAD), F32)] * 4,
        compiler_params=_params(),
    )(proj, proj, proj, proj, proj, lbf, lbb, ng)


def _hgrn_bwd(proj, lbf, lbb, ng, o_sum, dya, b_, s_, d_, name):
    h_ = d_ // HEAD
    nc = s_ // CHUNK

    def body(hq_ref, hff_ref, hfb_ref, hi_ref, hog_ref, lbf_ref, lbb_ref, ng_ref, o_ref, dya_ref,
             dhq_ref, dhff_ref, dhfb_ref, dhi_ref, dhog_ref, dng_ref, dlbf_ref, dlbb_ref,
             q_s, k_s, cum_s, do_s, dq_s, dv_s, db_s, dk_s, st_s):
        b = pl.program_id(1)

        @pl.when(b == 0)
        def _():
            dng_ref[...] = jnp.zeros_like(dng_ref)
            dlbf_ref[...] = jnp.zeros_like(dlbf_ref)
            dlbb_ref[...] = jnp.zeros_like(dlbb_ref)

        row = lax.broadcasted_iota(jnp.int32, (s_, HEAD), 0) % CHUNK
        crow = lax.broadcasted_iota(jnp.int32, (CHUNK, HEAD), 0)
        hq = hq_ref[...]
        sq = _sigmoid(hq)
        q_s[...] = hq * sq
        o = o_ref[...]
        rinv = lax.rsqrt(jnp.mean(o * o, axis=-1, keepdims=True) + LN_EPS)
        nrm = o * rinv
        hog = hog_ref[...]
        so = _sigmoid(hog)
        gain = ng_ref[...]
        dy = dya_ref[...]
        dhog_ref[...] = (dy * nrm * gain * _dsilu(hog, so)).astype(BF16)
        dng_ref[...] += jnp.sum(dy * nrm * hog * so, axis=0, keepdims=True)
        dn = dy * gain * hog * so
        do_s[...] = rinv * (dn - nrm * jnp.mean(dn * nrm, axis=-1, keepdims=True))
        dq_s[...] = jnp.zeros_like(dq_s)
        dv_s[...] = jnp.zeros_like(dv_s)

        for reverse, hf_ref, lb_ref, dhf_ref, dlb_ref in (
                (False, hff_ref, lbf_ref, dhff_ref, dlbf_ref), (True, hfb_ref, lbb_ref, dhfb_ref, dlbb_ref)):
            tab = lb_ref[...]
            lb = _lower_bound(tab)
            sf = _sigmoid(hf_ref[...])
            f = lb + (1.0 - lb) * sf
            k_s[...] = 1.0 - f
            cum_s[...] = _chunk_scan(jnp.log(f), row, reverse, s_)
            mask = _tri_mask(reverse)
            last = 0 if reverse else CHUNK - 1

            def chunk(idx, reverse=reverse):
                sl = pl.ds(pl.multiple_of(idx * CHUNK, CHUNK), CHUNK)
                cm = cum_s[sl, :]
                tot = cm[0:1, :] if reverse else cm[CHUNK - 1:CHUNK, :]
                return sl, cm, tot

            def fstep(n, st, reverse=reverse, chunk=chunk):
                idx = (nc - 1 - n) if reverse else n
                sl, cm, tot = chunk(idx)
                st_s[idx] = st
                ke = (k_s[sl, :] * jnp.exp(tot - cm)).astype(BF16)
                return st * jnp.exp(tot) + _dot(hi_ref[sl, :].astype(BF16), ke, TN)

            lax.fori_loop(0, nc, fstep, jnp.zeros((HEAD, HEAD), F32))

            def bstep(n, dst, reverse=reverse, chunk=chunk, mask=mask, last=last):
                idx = n if reverse else (nc - 1 - n)
                sl, cm, tot = chunk(idx)
                eb, enb, ee, dec = jnp.exp(cm), jnp.exp(-cm), jnp.exp(tot - cm), jnp.exp(tot)
                qc, kc = q_s[sl, :], k_s[sl, :]
                qd, kd, ke = qc * eb, kc * enb, kc * ee
                qdb, kdb, keb = qd.astype(BF16), kd.astype(BF16), ke.astype(BF16)
                vb = hi_ref[sl, :].astype(BF16)
                dob = do_s[sl, :].astype(BF16)
                st0 = st_s[idx]
                dstb = dst.astype(BF16)
                a = jnp.where(mask, _dot(qdb, kdb, NT), 0.0).astype(BF16)
                da = jnp.where(mask, _dot(dob, vb, NT), 0.0).astype(BF16)
                dqd = _dot(da, kdb, NN) + _dot(dob, st0.astype(BF16), NN)
                dkd = _dot(da, qdb, TN)
                dv = _dot(a, dob, TN) + _dot(keb, dstb, NT)
                dke = _dot(vb, dstb, NN)
                ddec = jnp.sum(dst * st0, axis=0, keepdims=True)
                dtot = jnp.sum(dke * ke, axis=0, keepdims=True) + ddec * dec
                db = dqd * qd - dkd * kd - dke * ke
                db_s[sl, :] = db + jnp.where(crow == last, dtot, 0.0)
                dk_s[sl, :] = dkd * enb + dke * ee
                dq_s[sl, :] += dqd * eb
                dv_s[sl, :] += dv
                return dst * dec + _dot(dob, qdb, TN)

            lax.fori_loop(0, nc, bstep, jnp.zeros((HEAD, HEAD), F32))
            dlogf = _chunk_scan(db_s[...], row, not reverse, s_)
            df = dlogf / f - dk_s[...]
            dhf_ref[...] = (df * (1.0 - lb) * sf * (1.0 - sf)).astype(BF16)
            dlb = jnp.sum(df * (1.0 - sf), axis=0, keepdims=True) * lb * (1.0 - lb)
            dlb_ref[0:1, :] += dlb
            dlb_ref[1:2, :] -= dlb

        dhq_ref[...] = (dq_s[...] * _dsilu(hq, sq)).astype(BF16)
        dhi_ref[...] = dv_s[...].astype(BF16)

    def col(part):
        return pl.BlockSpec((s_, HEAD), lambda h, b, part=part: (b, part * h_ + h))

    tab = pl.BlockSpec((2, HEAD), lambda h, b: (0, h))
    vec = pl.BlockSpec((1, HEAD), lambda h, b: (0, h))
    blk = pl.BlockSpec((s_, HEAD), lambda h, b: (b, h))
    act = jax.ShapeDtypeStruct((b_ * s_, d_), BF16)
    return pl.pallas_call(
        body, name=name, grid=(h_, b_),
        in_specs=[col(0), col(1), col(2), col(3), col(4), tab, tab, vec, blk, blk],
        out_specs=[blk] * 5 + [vec, tab, tab],
        out_shape=[act] * 5 + [jax.ShapeDtypeStruct((1, d_), F32), jax.ShapeDtypeStruct((2, d_), F32),
                               jax.ShapeDtypeStruct((2, d_), F32)],
        scratch_shapes=[pltpu.VMEM((s_, HEAD), F32)] * 8 + [pltpu.VMEM((nc, HEAD, HEAD), F32)],
        compiler_params=_params(),
    )(proj, proj, proj, proj, proj, lbf, lbb, ng, o_sum, dya)


def _rope_tables(s_):
    half = ROPE_DIM // 2
    inv_freq = ROPE_THETA ** (-jnp.arange(0, ROPE_DIM, 2, dtype=F32) / ROPE_DIM)
    ang = jnp.arange(s_, dtype=F32)[:, None] * inv_freq
    cos, sin = jnp.cos(ang), jnp.sin(ang)
    zeros = jnp.zeros((s_, HEAD - ROPE_DIM), F32)
    zh = jnp.zeros((s_, half), F32)
    c = jnp.concatenate([cos, cos, jnp.ones((s_, HEAD - ROPE_DIM), F32)], axis=1)
    s1 = jnp.concatenate([-sin, zh, zeros], axis=1)
    s2 = jnp.concatenate([zh, sin, zeros], axis=1)
    return c, s1, s2


def _rope(t, c, s1, s2):
    half = ROPE_DIM // 2
    return t * c + pltpu.roll(t, HEAD - half, 1) * s1 + pltpu.roll(t, half, 1) * s2


def _rope_bwd(dt, c, s1, s2):
    half = ROPE_DIM // 2
    return dt * c + pltpu.roll(dt * s1, half, 1) + pltpu.roll(dt * s2, HEAD - half, 1)


def _window_mask(r0, qb, wk, seg):
    row = lax.broadcasted_iota(jnp.int32, (qb, wk), 0)
    col = lax.broadcasted_iota(jnp.int32, (qb, wk), 1)
    kj = r0 - ATTN_HALF + col
    return (col - row >= 0) & (col - row <= 2 * ATTN_HALF) & (kj >= 0) & (kj < seg)


def _attn_fwd(qkv, tabs, b_, s_, dil, name):
    seg = s_ // dil
    qb = min(128, seg)
    nq, wk = seg // qb, qb + 2 * ATTN_HALF
    scale = HEAD ** -0.5
    ncol = QKV_GROUP // HEAD

    def body(q_ref, k_ref, v_ref, c_ref, s1_ref, s2_ref, o_ref, lse_ref, q_s, k_s, v_s):
        c, s1, s2 = c_ref[...], s1_ref[...], s2_ref[...]
        q_s[...] = _rope(q_ref[...], c, s1, s2).astype(BF16)
        k_s[...] = jnp.zeros_like(k_s)
        v_s[...] = jnp.zeros_like(v_s)
        k_s[ATTN_HALF:ATTN_HALF + seg, :] = _rope(k_ref[...], c, s1, s2).astype(BF16)
        v_s[ATTN_HALF:ATTN_HALF + seg, :] = v_ref[...].astype(BF16)

        def step(i, carry):
            r0 = pl.multiple_of(i * qb, qb)
            sc = _dot(q_s[pl.ds(r0, qb), :], k_s[pl.ds(r0, wk), :], NT) * scale
            sc = jnp.where(_window_mask(r0, qb, wk, seg), sc, NEG_INF)
            m = jnp.max(sc, axis=-1, keepdims=True)
            p = jnp.exp(sc - m)
            den = jnp.sum(p, axis=-1, keepdims=True)
            o_ref[pl.ds(r0, qb), :] = _dot(p.astype(BF16), v_s[pl.ds(r0, wk), :], NN) / den
            lse_ref[pl.ds(r0, qb), :] = jnp.broadcast_to(m + jnp.log(den), (qb, HEAD))
            return carry

        lax.fori_loop(0, nq, step, 0)

    def col(part):
        return pl.BlockSpec((seg, HEAD), lambda b, r, h, part=part: (b, r * ncol + part * ATTN_HEADS + h))

    tab = pl.BlockSpec((seg, HEAD), lambda b, r, h: (0, r))
    out = pl.BlockSpec((seg, HEAD), lambda b, r, h: (b, r * ATTN_HEADS + h))
    shape = jax.ShapeDtypeStruct((b_ * seg, dil * ATTN_OUT), F32)
    return pl.pallas_call(
        body, name=name, grid=(b_, dil, ATTN_HEADS),
        in_specs=[col(0), col(1), col(2), tab, tab, tab],
        out_specs=[out, out],
        out_shape=[shape, shape],
        scratch_shapes=[pltpu.VMEM((seg, HEAD), BF16), pltpu.VMEM((seg + 2 * ATTN_HALF, HEAD), BF16),
                        pltpu.VMEM((seg + 2 * ATTN_HALF, HEAD), BF16)],
        compiler_params=_params(),
    )(qkv, qkv, qkv, *tabs)


def _attn_bwd(qkv, tabs, dog, cg, lse, b_, s_, dil, name):
    seg = s_ // dil
    qb = min(128, seg)
    nq, wk = seg // qb, qb + 2 * ATTN_HALF
    scale = HEAD ** -0.5
    ncol = QKV_GROUP // HEAD

    def body(q_ref, k_ref, v_ref, c_ref, s1_ref, s2_ref, do_ref, cg_ref, lse_ref, dq_ref, dk_ref, dv_ref,
             q_s, k_s, v_s, dk_s, dv_s):
        c, s1, s2 = c_ref[...], s1_ref[...], s2_ref[...]
        q_s[...] = _rope(q_ref[...], c, s1, s2).astype(BF16)
        k_s[...] = jnp.zeros_like(k_s)
        v_s[...] = jnp.zeros_like(v_s)
        k_s[ATTN_HALF:ATTN_HALF + seg, :] = _rope(k_ref[...], c, s1, s2).astype(BF16)
        v_s[ATTN_HALF:ATTN_HALF + seg, :] = v_ref[...].astype(BF16)
        dk_s[...] = jnp.zeros_like(dk_s)
        dv_s[...] = jnp.zeros_like(dv_s)

        def step(i, carry):
            r0 = pl.multiple_of(i * qb, qb)
            rows, win = pl.ds(r0, qb), pl.ds(r0, wk)
            qc, kw, vw = q_s[rows, :], k_s[win, :], v_s[win, :]
            sc = _dot(qc, kw, NT) * scale
            p = jnp.where(_window_mask(r0, qb, wk, seg), jnp.exp(sc - lse_ref[rows, 0:1]), 0.0)
            dob = do_ref[rows, :].astype(BF16)
            dp = _dot(dob, vw, NT)
            ds = (p * (dp + cg_ref[rows, 0:1]) * scale).astype(BF16)
            dq = _dot(ds, kw, NN)
            dq_ref[rows, :] = _rope_bwd(dq, c_ref[rows, :], s1_ref[rows, :], s2_ref[rows, :])
            dk_s[win, :] += _dot(ds, qc, TN)
            dv_s[win, :] += _dot(p.astype(BF16), dob, TN)
            return carry

        lax.fori_loop(0, nq, step, 0)
        dk_ref[...] = _rope_bwd(dk_s[ATTN_HALF:ATTN_HALF + seg, :], c, s1, s2)
        dv_ref[...] = dv_s[ATTN_HALF:ATTN_HALF + seg, :]

    def col(part):
        return pl.BlockSpec((seg, HEAD), lambda b, r, h, part=part: (b, r * ncol + part * ATTN_HEADS + h))

    tab = pl.BlockSpec((seg, HEAD), lambda b, r, h: (0, r))
    out = pl.BlockSpec((seg, HEAD), lambda b, r, h: (b, r * ATTN_HEADS + h))
    shape = jax.ShapeDtypeStruct((b_ * seg, dil * ATTN_OUT), F32)
    return pl.pallas_call(
        body, name=name, grid=(b_, dil, ATTN_HEADS),
        in_specs=[col(0), col(1), col(2), tab, tab, tab, out, out, out],
        out_specs=[out, out, out],
        out_shape=[shape, shape, shape],
        scratch_shapes=[pltpu.VMEM((seg, HEAD), BF16), pltpu.VMEM((seg + 2 * ATTN_HALF, HEAD), BF16),
                        pltpu.VMEM((seg + 2 * ATTN_HALF, HEAD), BF16),
                        pltpu.VMEM((seg + 2 * ATTN_HALF, HEAD), F32), pltpu.VMEM((seg + 2 * ATTN_HALF, HEAD), F32)],
        compiler_params=_params(),
    )(qkv, qkv, qkv, *tabs, dog, cg, lse)


def _group_weights(lses):
    m = jnp.maximum(jnp.maximum(lses[0], lses[1]), lses[2])
    es = [jnp.exp(l - m) for l in lses]
    den = es[0] + es[1] + es[2]
    return [e / den for e in es]


def _combine_fwd(outs, lses, name):
    t_, w_ = outs[0].shape
    tm = _tile(t_, 512, 8)
    ng = len(outs)

    def body(*refs):
        ws = _group_weights([r[...] for r in refs[ng:2 * ng]])
        acc = ws[0] * refs[0][...]
        for g in range(1, ng):
            acc = acc + ws[g] * refs[g][...]
        refs[2 * ng][...] = acc.astype(BF16)

    row = pl.BlockSpec((tm, w_), lambda i: (i, 0))
    return pl.pallas_call(
        body, name=name, grid=(t_ // tm,), in_specs=[row] * (2 * ng), out_specs=row,
        out_shape=jax.ShapeDtypeStruct((t_, w_), BF16), compiler_params=_params(),
    )(*outs, *lses)


def _combine_bwd(dob, outs, lses, name):
    t_, w_ = outs[0].shape
    tm = _tile(t_, 512, 8)
    ng = len(outs)

    def body(*refs):
        do = refs[0][...]
        os_ = [r[...] for r in refs[1:1 + ng]]
        ws = _group_weights([r[...] for r in refs[1 + ng:1 + 2 * ng]])
        o = ws[0] * os_[0]
        for g in range(1, ng):
            o = o + ws[g] * os_[g]
        prod = do * o
        heads = [jnp.broadcast_to(jnp.sum(prod[:, h * HEAD:(h + 1) * HEAD], axis=-1, keepdims=True), (tm, HEAD))
                 for h in range(w_ // HEAD)]
        tot = jnp.concatenate(heads, axis=1)
        for g in range(ng):
            refs[1 + 2 * ng + g][...] = ws[g] * do
            refs[1 + 3 * ng + g][...] = -ws[g] * tot

    row = pl.BlockSpec((tm, w_), lambda i: (i, 0))
    shape = jax.ShapeDtypeStruct((t_, w_), F32)
    res = pl.pallas_call(
        body, name=name, grid=(t_ // tm,), in_specs=[row] * (1 + 2 * ng), out_specs=[row] * (2 * ng),
        out_shape=[shape] * (2 * ng), compiler_params=_params(),
    )(dob, *outs, *lses)
    return res[:ng], res[ng:]


def _adam_update(w, g, m, v):
    m = ADAM_B1 * m + (1.0 - ADAM_B1) * g
    v = ADAM_B2 * v + (1.0 - ADAM_B2) * (g * g)
    m_hat = m / (1.0 - ADAM_B1 ** ADAM_STEP)
    v_hat = v / (1.0 - ADAM_B2 ** ADAM_STEP)
    return -ADAM_LR * (m_hat / (jnp.sqrt(v_hat) + ADAM_EPS) + ADAM_WD * w), m, v


def _adam(w, g, m, v, name):
    r_, c_ = w.shape
    tr = _tile(r_, 256, 8)

    def body(w_ref, g_ref, m_ref, v_ref, d_ref, mo_ref, vo_ref):
        d_ref[...], mo_ref[...], vo_ref[...] = _adam_update(w_ref[...], g_ref[...], m_ref[...], v_ref[...])

    blk = pl.BlockSpec((tr, c_), lambda i: (i, 0))
    shape = jax.ShapeDtypeStruct((r_, c_), F32)
    return pl.pallas_call(
        body, name=name, grid=(r_ // tr,), in_specs=[blk] * 4, out_specs=[blk] * 3,
        out_shape=[shape] * 3, compiler_params=_params(),
    )(w, g, m, v)


def _sum_partials(recv, name):
    n_, r_, c_ = recv.shape
    tr = _tile(r_, 128, 16)

    def body(p_ref, o_ref):
        acc = p_ref[0].astype(F32)
        for i in range(1, n_):
            acc = acc + p_ref[i].astype(F32)
        o_ref[...] = acc

    return pl.pallas_call(
        body, name=name, grid=(r_ // tr,),
        in_specs=[pl.BlockSpec((n_, tr, c_), lambda i: (0, i, 0))],
        out_specs=pl.BlockSpec((tr, c_), lambda i: (i, 0)),
        out_shape=jax.ShapeDtypeStruct((r_, c_), F32), compiler_params=_params(),
    )(recv)


def _small_sum_adam(parts, w, m, v, name):
    n_, r_, c_ = parts.shape

    def body(p_ref, w_ref, m_ref, v_ref, g_ref, d_ref, mo_ref, vo_ref):
        g = p_ref[0]
        for i in range(1, n_):
            g = g + p_ref[i]
        g_ref[...] = g
        d_ref[...], mo_ref[...], vo_ref[...] = _adam_update(w_ref[...], g, m_ref[...], v_ref[...])

    shape = jax.ShapeDtypeStruct((r_, c_), F32)
    return pl.pallas_call(body, name=name, out_shape=[shape] * 4, compiler_params=_params())(parts, w, m, v)


def _my_place():
    x, y, c = lax.axis_index("x"), lax.axis_index("y"), lax.axis_index("c")
    return x, y, c


def _peer(x, y, c, d):
    px = 1 - x if d & 4 else x
    py = 1 - y if d & 2 else y
    pc = 1 - c if d & 1 else c
    return (px, py, pc), 4 * px + 2 * py + pc


def _all_gather(shards, name):
    nw = len(shards)

    def body(*refs):
        ins, outs = refs[:nw], refs[nw:2 * nw]
        send_sems, recv_sems, local_sems = refs[2 * nw:]
        x, y, c = _my_place()
        me = 4 * x + 2 * y + c
        copies = []
        for k in range(nw):
            rows = shards[k].shape[0]
            mine = outs[k].at[pl.ds(pl.multiple_of(me * rows, 16), rows), :]
            local = pltpu.make_async_copy(ins[k], mine, local_sems.at[k])
            local.start()
            copies.append(local)
            for d in range(1, N_DEV):
                place, _ = _peer(x, y, c, d)
                remote = pltpu.make_async_remote_copy(
                    src_ref=ins[k], dst_ref=mine, send_sem=send_sems.at[d - 1, k], recv_sem=recv_sems.at[d - 1, k],
                    device_id=place, device_id_type=MESH)
                remote.start()
                copies.append(remote)
        for cp in copies:
            cp.wait()

    hbm = pl.BlockSpec(memory_space=pl.ANY)
    return pl.pallas_call(
        body, name=name, in_specs=[hbm] * nw, out_specs=[hbm] * nw,
        out_shape=[jax.ShapeDtypeStruct((N_DEV * s.shape[0], s.shape[1]), s.dtype) for s in shards],
        scratch_shapes=[pltpu.SemaphoreType.DMA((N_DEV - 1, nw)), pltpu.SemaphoreType.DMA((N_DEV - 1, nw)),
                        pltpu.SemaphoreType.DMA((nw,))],
    )(*shards)


def _scatter_partials(fulls, name):
    nw = len(fulls)

    def body(*refs):
        ins, outs = refs[:nw], refs[nw:2 * nw]
        send_sems, recv_sems, local_sems = refs[2 * nw:]
        x, y, c = _my_place()
        me = 4 * x + 2 * y + c
        copies = []
        for k in range(nw):
            rows = fulls[k].shape[0] // N_DEV
            slab = outs[k].at[me]
            local = pltpu.make_async_copy(ins[k].at[pl.ds(pl.multiple_of(me * rows, 16), rows), :], slab,
                                          local_sems.at[k])
            local.start()
            copies.append(local)
            for d in range(1, N_DEV):
                place, num = _peer(x, y, c, d)
                remote = pltpu.make_async_remote_copy(
                    src_ref=ins[k].at[pl.ds(pl.multiple_of(num * rows, 16), rows), :], dst_ref=slab,
                    send_sem=send_sems.at[d - 1, k], recv_sem=recv_sems.at[d - 1, k],
                    device_id=place, device_id_type=MESH)
                remote.start()
                copies.append(remote)
        for cp in copies:
            cp.wait()

    hbm = pl.BlockSpec(memory_space=pl.ANY)
    return pl.pallas_call(
        body, name=name, in_specs=[hbm] * nw, out_specs=[hbm] * nw,
        out_shape=[jax.ShapeDtypeStruct((N_DEV, f.shape[0] // N_DEV, f.shape[1]), f.dtype) for f in fulls],
        scratch_shapes=[pltpu.SemaphoreType.DMA((N_DEV - 1, nw)), pltpu.SemaphoreType.DMA((N_DEV - 1, nw)),
                        pltpu.SemaphoreType.DMA((nw,))],
    )(*fulls)


BIG = ("ffn1_w_in", "ffn1_w_out", "mix_w_in", "w_branch_a", "w_branch_b", "mix_w_out", "ffn2_w_in", "ffn2_w_out")
TRANSPOSED = ("ffn1_w_in", "mix_w_in", "w_branch_b", "ffn2_w_in")
SMALL = ("ln1_g", "ln1_b", "ln2_g", "ln2_b", "ln3_g", "ln3_b", "hgrn_norm_g", "hgrn_lb_fwd", "hgrn_lb_bwd")
SMALL_ROWS = 16


def _local_step(x, target, wts, sp):
    b_, s_, d_ = x.shape
    t_ = b_ * s_
    x2, tgt = x.reshape(t_, d_), target.reshape(t_, d_)
    xb = x2.astype(BF16)
    g1, u1, a1 = _ffn_in_fwd(xb, wts["ffn1_w_in"], "ffn1_in")
    r1, h1, h1b = _mm_res_ln_fwd(a1, wts["ffn1_w_out"], x2, sp["ln1_g"], sp["ln1_b"], 0.5, "ffn1_out_ln1")
    proj = _mm_nt(h1b, wts["mix_w_in"], F32, "mix_in")
    ya_in, o_sum = _hgrn_fwd(proj, sp["hgrn_lb_fwd"], sp["hgrn_lb_bwd"], sp["hgrn_norm_g"], b_, s_, d_, "hgrn_fwd")
    tabs = _rope_tables(s_)
    qkvs, gtabs, outs, lses = [], [], [], []
    for gi, (_, dil) in enumerate(ATTN_GROUPS):
        off = 5 * d_ + gi * QKV_GROUP
        qkv = proj[:, off:off + QKV_GROUP].reshape(t_ // dil, dil * QKV_GROUP)
        gt = [tb.reshape(s_ // dil, dil * HEAD) for tb in tabs]
        o_g, lse_g = _attn_fwd(qkv, gt, b_, s_, dil, f"attn_fwd_{gi}")
        qkvs.append(qkv)
        gtabs.append(gt)
        outs.append(o_g.reshape(t_, ATTN_OUT))
        lses.append(lse_g.reshape(t_, ATTN_OUT))
    ob = _combine_fwd(outs, lses, "attn_combine")
    ya, yb, z = _gate_out_fwd(ya_in, ob, wts["w_branch_a"], wts["w_branch_b"], proj, d_, "branch_gate")
    r2, h2, h2b = _mm_res_ln_fwd(z, wts["mix_w_out"], h1, sp["ln2_g"], sp["ln2_b"], 1.0, "mix_out_ln2")
    g2, u2, a2 = _ffn_in_fwd(h2b, wts["ffn2_w_in"], "ffn2_in")
    r3, _, _ = _mm_res_ln_fwd(a2, wts["ffn2_w_out"], h2, sp["ln3_g"], sp["ln3_b"], 0.5, "ffn2_out_ln3")
    dr3, dr3b, dg3, db3, loss = _ln_loss_bwd(r3, tgt, sp["ln3_g"], sp["ln3_b"], "loss_ln3_bwd")
    grads = {}
    grads["ffn2_w_out"] = _mm_tn(a2, dr3b, 0.5, "d_ffn2_w_out")
    dgate2, dup2 = _ffn_mid_bwd(dr3b, wts["ffn2_w_out"], g2, u2, 0.5, "ffn2_mid_bwd")
    du2 = jnp.concatenate([dgate2, dup2], axis=1)
    grads["ffn2_w_in"] = _mm_tn(du2, h2b, 1.0, "d_ffn2_w_in")
    dr2, dr2b, dg2, db2 = _mm_nn_res_lnbwd(du2, wts["ffn2_w_in"], dr3, r2, sp["ln2_g"], "ffn2_in_bwd_ln2")
    grads["mix_w_out"] = _mm_tn(z, dr2b, 1.0, "d_mix_w_out")
    dya, dyb, dga, dgb = _dz_gate_bwd(dr2b, wts["mix_w_out"], proj, ya, yb, d_, "branch_gate_bwd")
    grads["w_branch_a"] = _mm_tn(ya_in, dya, 1.0, "d_w_branch_a")
    grads["w_branch_b"] = _mm_tn(dyb, ob, 1.0, "d_w_branch_b")
    dya_in = _mm_nt(dya, wts["w_branch_a"], F32, "branch_a_bwd")
    dob = _mm_nn(dyb, wts["w_branch_b"], F32, "branch_b_bwd")
    dhq, dhff, dhfb, dhi, dhog, dng, dlbf, dlbb = _hgrn_bwd(
        proj, sp["hgrn_lb_fwd"], sp["hgrn_lb_bwd"], sp["hgrn_norm_g"], o_sum, dya_in, b_, s_, d_, "hgrn_bwd")
    dogs, cgs = _combine_bwd(dob, outs, lses, "attn_combine_bwd")
    dqkv = []
    for gi, (_, dil) in enumerate(ATTN_GROUPS):
        seg_rows = t_ // dil
        dq, dk, dv = _attn_bwd(qkvs[gi], gtabs[gi], dogs[gi].reshape(seg_rows, dil * ATTN_OUT),
                               cgs[gi].reshape(seg_rows, dil * ATTN_OUT), lses[gi].reshape(seg_rows, dil * ATTN_OUT),
                               b_, s_, dil, f"attn_bwd_{gi}")
        dqkv += [t.reshape(t_, ATTN_OUT).astype(BF16) for t in (dq, dk, dv)]
    dproj = jnp.concatenate([dhq, dhff, dhfb, dhi, dhog] + dqkv + [dga, dgb], axis=1)
    grads["mix_w_in"] = _mm_tn(dproj, h1b, 1.0, "d_mix_w_in")
    dr1, dr1b, dg1, db1 = _mm_nn_res_lnbwd(dproj, wts["mix_w_in"], dr2, r1, sp["ln1_g"], "mix_in_bwd_ln1")
    grads["ffn1_w_out"] = _mm_tn(a1, dr1b, 0.5, "d_ffn1_w_out")
    dgate1, dup1 = _ffn_mid_bwd(dr1b, wts["ffn1_w_out"], g1, u1, 0.5, "ffn1_mid_bwd")
    du1 = jnp.concatenate([dgate1, dup1], axis=1)
    grads["ffn1_w_in"] = _mm_tn(du1, xb, 1.0, "d_ffn1_w_in")
    grad_x = _mm_nn_res(du1, wts["ffn1_w_in"], dr1, "ffn1_in_bwd")
    small = {"ln1_g": dg1, "ln1_b": db1, "ln2_g": dg2, "ln2_b": db2, "ln3_g": dg3, "ln3_b": db3,
             "hgrn_norm_g": dng, "hgrn_lb_fwd": dlbf, "hgrn_lb_bwd": dlbb}
    return loss, grad_x.reshape(b_, s_, d_), grads, small


def _pack_small(vals):
    rows = jnp.concatenate([vals[n] for n in SMALL], axis=0)
    return jnp.pad(rows, ((0, SMALL_ROWS - rows.shape[0]), (0, 0)))


def _unpack_small(packed):
    out, r = {}, 0
    for n in SMALL:
        k = 2 if n.startswith("hgrn_lb") else 1
        out[n] = packed[r:r + k]
        r += k
    return out


def kernel(x, ffn1_w_in, ffn1_w_out, ln1_g, ln1_b, mix_w_in, hgrn_lb_fwd, hgrn_lb_bwd, hgrn_norm_g, w_branch_a, w_branch_b, mix_w_out, ln2_g, ln2_b, ffn2_w_in, ffn2_w_out, ln3_g, ln3_b, loss_target, m_ffn1_w_in, m_ffn1_w_out, m_ln1_g, m_ln1_b, m_mix_w_in, m_hgrn_lb_fwd, m_hgrn_lb_bwd, m_hgrn_norm_g, m_w_branch_a, m_w_branch_b, m_mix_w_out, m_ln2_g, m_ln2_b, m_ffn2_w_in, m_ffn2_w_out, m_ln3_g, m_ln3_b, v_ffn1_w_in, v_ffn1_w_out, v_ln1_g, v_ln1_b, v_mix_w_in, v_hgrn_lb_fwd, v_hgrn_lb_bwd, v_hgrn_norm_g, v_w_branch_a, v_w_branch_b, v_mix_w_out, v_ln2_g, v_ln2_b, v_ffn2_w_in, v_ffn2_w_out, v_ln3_g, v_ln3_b):
    args = dict(locals())
    big_w = {n: args[n][0] for n in BIG}
    sp = {n: args[n] for n in SMALL}
    shards = [(big_w[n].T if n in TRANSPOSED else big_w[n]).astype(BF16) for n in BIG]
    wts = dict(zip(BIG, _all_gather(shards, "gather_weights")))
    loss_part, grad_x, grads, small = _local_step(x, loss_target, wts, sp)
    loss = lax.psum(loss_part[0, 0], ("x", "y", "c"))
    slabs = _scatter_partials([grads[n] for n in BIG], "scatter_grads")
    out_g, out_d, out_m, out_v = {}, {}, {}, {}
    for n, slab in zip(BIG, slabs):
        g = _sum_partials(slab, f"sum_{n}")
        if n in TRANSPOSED:
            g = g.T
        d_w, m_w, v_w = _adam(big_w[n], g, args["m_" + n][0], args["v_" + n][0], f"adam_{n}")
        out_g[n], out_d[n], out_m[n], out_v[n] = g[None], d_w[None], m_w[None], v_w[None]
    (parts,) = _all_gather([_pack_small(small)], "gather_small_grads")
    res = _small_sum_adam(parts.reshape(N_DEV, SMALL_ROWS, parts.shape[1]), _pack_small(sp),
                          _pack_small({n: args["m_" + n] for n in SMALL}),
                          _pack_small({n: args["v_" + n] for n in SMALL}), "small_adam")
    sg, sd, sm, sv = (_unpack_small(r) for r in res)
    out_g.update(sg), out_d.update(sd), out_m.update(sm), out_v.update(sv)
    order = ("ffn1_w_in", "ffn1_w_out", "ln1_g", "ln1_b", "mix_w_in", "hgrn_lb_fwd", "hgrn_lb_bwd", "hgrn_norm_g",
             "w_branch_a", "w_branch_b", "mix_w_out", "ln2_g", "ln2_b", "ffn2_w_in", "ffn2_w_out", "ln3_g", "ln3_b")
    return (loss, grad_x, *[out_g[n] for n in order], *[out_d[n] for n in order],
            *[out_m[n] for n in order], *[out_v[n] for n in order])
```

```python
import functools

import jax
import jax.numpy as jnp
from jax import lax
from jax.experimental import pallas as pl
from jax.experimental.pallas import tpu as pltpu

F32 = jnp.float32
BF16 = jnp.bfloat16

N_DEV = 8
HEAD = 128
CHUNK = 32
ATTN_GROUPS = ((128, 1), (512, 4), (2048, 16))
ATTN_HEADS = 4
ATTN_HALF = 64
QKV_GROUP = 3 * ATTN_HEADS * HEAD
QKV_WIDTH = len(ATTN_GROUPS) * QKV_GROUP
ATTN_OUT = ATTN_HEADS * HEAD
ROPE_THETA = 500000.0
ROPE_DIM = HEAD // 4
ALPHA = 2.0 ** 0.25
LN_EPS = 1e-5
NEG_INF = -1e30
ADAM_LR, ADAM_B1, ADAM_B2, ADAM_EPS, ADAM_WD, ADAM_STEP = 0.001, 0.9, 0.999, 1e-08, 0.01, 10
VMEM_LIMIT = 56 * 1024 * 1024

NT = (((1,), (1,)), ((), ()))
NN = (((1,), (0,)), ((), ()))
TN = (((0,), (0,)), ((), ()))
MESH = pl.DeviceIdType.MESH


def _dot(a, b, dims):
    return lax.dot_general(a, b, dims, preferred_element_type=F32)


def _tile(n, pref, mult=128):
    if n <= pref:
        return n
    t = (pref // mult) * mult
    while t >= mult:
        if n % t == 0:
            return t
        t -= mult
    return n


def _tile_multi(ns, pref, mult=128):
    t = (pref // mult) * mult
    while t >= mult:
        if all(n % t == 0 for n in ns):
            return t
        t -= mult
    raise ValueError(f"no common tile for {ns}")


def _params(**kw):
    return pltpu.CompilerParams(vmem_limit_bytes=VMEM_LIMIT, **kw)


def _sigmoid(x):
    return jax.nn.sigmoid(x)


def _dsilu(x, s):
    return s * (1.0 + x * (1.0 - s))


def _ln_stats(r):
    mu = jnp.mean(r, axis=-1, keepdims=True)
    xc = r - mu
    var = jnp.mean(xc * xc, axis=-1, keepdims=True)
    rstd = lax.rsqrt(var + LN_EPS)
    return xc * rstd, rstd


def _ln_bwd(dy, xhat, rstd, g):
    dyg = dy * g
    m1 = jnp.mean(dyg, axis=-1, keepdims=True)
    m2 = jnp.mean(dyg * xhat, axis=-1, keepdims=True)
    return rstd * (dyg - m1 - xhat * m2)


def _ffn_in_fwd(xb, w_t, name):
    t_, d_ = xb.shape
    f_ = w_t.shape[0] // 2
    tm, tn = _tile(t_, 512, 8), _tile(f_, 512)
    nj = f_ // tn

    def body(x_ref, wg_ref, wu_ref, g_ref, u_ref, a_ref):
        x = x_ref[...]
        g = _dot(x, wg_ref[...], NT)
        u = _dot(x, wu_ref[...], NT)
        g_ref[...] = g
        u_ref[...] = u
        a_ref[...] = (g * _sigmoid(g) * u).astype(BF16)

    return pl.pallas_call(
        body, name=name, grid=(t_ // tm, nj),
        in_specs=[pl.BlockSpec((tm, d_), lambda i, j: (i, 0)),
                  pl.BlockSpec((tn, d_), lambda i, j: (j, 0)),
                  pl.BlockSpec((tn, d_), lambda i, j: (j + nj, 0))],
        out_specs=[pl.BlockSpec((tm, tn), lambda i, j: (i, j))] * 3,
        out_shape=[jax.ShapeDtypeStruct((t_, f_), F32), jax.ShapeDtypeStruct((t_, f_), F32),
                   jax.ShapeDtypeStruct((t_, f_), BF16)],
        compiler_params=_params(),
    )(xb, w_t, w_t)


def _mm_res_ln_fwd(a, w, res, g, b, scale, name):
    t_, k_ = a.shape
    d_ = w.shape[1]
    tm, tk = _tile(t_, 512, 8), _tile(k_, 512)
    nk = k_ // tk

    def body(a_ref, w_ref, res_ref, g_ref, b_ref, r_ref, h_ref, hb_ref, acc):
        k = pl.program_id(1)

        @pl.when(k == 0)
        def _():
            acc[...] = jnp.zeros_like(acc)

        acc[...] += _dot(a_ref[...], w_ref[...], NN)

        @pl.when(k == nk - 1)
        def _():
            r = ALPHA * res_ref[...] + scale * acc[...]
            xhat, _ = _ln_stats(r)
            h = xhat * g_ref[...] + b_ref[...]
            r_ref[...] = r
            h_ref[...] = h
            hb_ref[...] = h.astype(BF16)

    row = pl.BlockSpec((tm, d_), lambda i, k: (i, 0))
    vec = pl.BlockSpec((1, d_), lambda i, k: (0, 0))
    return pl.pallas_call(
        body, name=name, grid=(t_ // tm, nk),
        in_specs=[pl.BlockSpec((tm, tk), lambda i, k: (i, k)),
                  pl.BlockSpec((tk, d_), lambda i, k: (k, 0)), row, vec, vec],
        out_specs=[row, row, row],
        out_shape=[jax.ShapeDtypeStruct((t_, d_), F32), jax.ShapeDtypeStruct((t_, d_), F32),
                   jax.ShapeDtypeStruct((t_, d_), BF16)],
        scratch_shapes=[pltpu.VMEM((tm, d_), F32)],
        compiler_params=_params(),
    )(a, w, res, g, b)


def _mm_nt(a, w_t, out_dtype, name):
    t_, k_ = a.shape
    n_ = w_t.shape[0]
    tm, tn = _tile(t_, 512, 8), _tile(n_, 512)

    def body(a_ref, w_ref, o_ref):
        o_ref[...] = _dot(a_ref[...], w_ref[...], NT).astype(out_dtype)

    return pl.pallas_call(
        body, name=name, grid=(t_ // tm, n_ // tn),
        in_specs=[pl.BlockSpec((tm, k_), lambda i, j: (i, 0)),
                  pl.BlockSpec((tn, k_), lambda i, j: (j, 0))],
        out_specs=pl.BlockSpec((tm, tn), lambda i, j: (i, j)),
        out_shape=jax.ShapeDtypeStruct((t_, n_), out_dtype),
        compiler_params=_params(),
    )(a, w_t)


def _mm_nn(a, w, out_dtype, name):
    t_, k_ = a.shape
    n_ = w.shape[1]
    tm, tn = _tile(t_, 512, 8), _tile(n_, 512)

    def body(a_ref, w_ref, o_ref):
        o_ref[...] = _dot(a_ref[...], w_ref[...], NN).astype(out_dtype)

    return pl.pallas_call(
        body, name=name, grid=(t_ // tm, n_ // tn),
        in_specs=[pl.BlockSpec((tm, k_), lambda i, j: (i, 0)),
                  pl.BlockSpec((k_, tn), lambda i, j: (0, j))],
        out_specs=pl.BlockSpec((tm, tn), lambda i, j: (i, j)),
        out_shape=jax.ShapeDtypeStruct((t_, n_), out_dtype),
        compiler_params=_params(),
    )(a, w)


def _mm_tn(a, b, scale, name):
    t_, m_ = a.shape
    n_ = b.shape[1]
    tm, tn, tk = _tile(m_, 512), _tile(n_, 2048), _tile(t_, 512, 8)
    nk = t_ // tk

    def body(a_ref, b_ref, o_ref, acc):
        k = pl.program_id(2)

        @pl.when(k == 0)
        def _():
            acc[...] = jnp.zeros_like(acc)

        acc[...] += _dot(a_ref[...], b_ref[...], TN)

        @pl.when(k == nk - 1)
        def _():
            o_ref[...] = (scale * acc[...]).astype(BF16)

    return pl.pallas_call(
        body, name=name, grid=(m_ // tm, n_ // tn, nk),
        in_specs=[pl.BlockSpec((tk, tm), lambda i, j, k: (k, i)),
                  pl.BlockSpec((tk, tn), lambda i, j, k: (k, j))],
        out_specs=pl.BlockSpec((tm, tn), lambda i, j, k: (i, j)),
        out_shape=jax.ShapeDtypeStruct((m_, n_), BF16),
        scratch_shapes=[pltpu.VMEM((tm, tn), F32)],
        compiler_params=_params(),
    )(a, b)


def _gate_out_fwd(ya_in, ob, wa, wb_t, proj, d_, name):
    t_ = ya_in.shape[0]
    goff = 5 * d_ + QKV_WIDTH
    tm, tn = _tile(t_, 512, 8), _tile_multi([d_, goff], 512)
    ja, jb = goff // tn, (goff + d_) // tn

    def body(ya_ref, ob_ref, wa_ref, wb_ref, ga_ref, gb_ref, yao_ref, ybo_ref, z_ref):
        y_a = _dot(ya_ref[...], wa_ref[...], NN)
        y_b = _dot(ob_ref[...], wb_ref[...], NT)
        yao_ref[...] = y_a
        ybo_ref[...] = y_b
        z_ref[...] = (_sigmoid(ga_ref[...]) * y_a + _sigmoid(gb_ref[...]) * y_b).astype(BF16)

    tile = pl.BlockSpec((tm, tn), lambda i, j: (i, j))
    return pl.pallas_call(
        body, name=name, grid=(t_ // tm, d_ // tn),
        in_specs=[pl.BlockSpec((tm, d_), lambda i, j: (i, 0)),
                  pl.BlockSpec((tm, ATTN_OUT), lambda i, j: (i, 0)),
                  pl.BlockSpec((d_, tn), lambda i, j: (0, j)),
                  pl.BlockSpec((tn, ATTN_OUT), lambda i, j: (j, 0)),
                  pl.BlockSpec((tm, tn), lambda i, j: (i, ja + j)),
                  pl.BlockSpec((tm, tn), lambda i, j: (i, jb + j))],
        out_specs=[tile, tile, tile],
        out_shape=[jax.ShapeDtypeStruct((t_, d_), F32), jax.ShapeDtypeStruct((t_, d_), F32),
                   jax.ShapeDtypeStruct((t_, d_), BF16)],
        compiler_params=_params(),
    )(ya_in, ob, wa, wb_t, proj, proj)


def _ffn_mid_bwd(drb, w_out, gate, up, scale, name):
    t_, d_ = drb.shape
    f_ = w_out.shape[0]
    tm, tn = _tile(t_, 512, 8), _tile(f_, 512)

    def body(dr_ref, w_ref, g_ref, u_ref, dg_ref, du_ref):
        da = scale * _dot(dr_ref[...], w_ref[...], NT)
        g = g_ref[...]
        s = _sigmoid(g)
        dg_ref[...] = (da * u_ref[...] * _dsilu(g, s)).astype(BF16)
        du_ref[...] = (da * g * s).astype(BF16)

    tile = pl.BlockSpec((tm, tn), lambda i, j: (i, j))
    return pl.pallas_call(
        body, name=name, grid=(t_ // tm, f_ // tn),
        in_specs=[pl.BlockSpec((tm, d_), lambda i, j: (i, 0)),
                  pl.BlockSpec((tn, d_), lambda i, j: (j, 0)), tile, tile],
        out_specs=[tile, tile],
        out_shape=[jax.ShapeDtypeStruct((t_, f_), BF16)] * 2,
        compiler_params=_params(),
    )(drb, w_out, gate, up)


def _mm_nn_res_lnbwd(a, w, dres, r, g, name):
    t_, k_ = a.shape
    d_ = w.shape[1]
    tm, tk = _tile(t_, 512, 8), _tile(k_, 512)
    nk = k_ // tk

    def body(a_ref, w_ref, dres_ref, r_ref, g_ref, dr_ref, drb_ref, dg_ref, db_ref, acc):
        i, k = pl.program_id(0), pl.program_id(1)

        @pl.when(k == 0)
        def _():
            acc[...] = jnp.zeros_like(acc)

        @pl.when((i == 0) & (k == 0))
        def _():
            dg_ref[...] = jnp.zeros_like(dg_ref)
            db_ref[...] = jnp.zeros_like(db_ref)

        acc[...] += _dot(a_ref[...], w_ref[...], NN)

        @pl.when(k == nk - 1)
        def _():
            dy = acc[...] + ALPHA * dres_ref[...]
            xhat, rstd = _ln_stats(r_ref[...])
            dr = _ln_bwd(dy, xhat, rstd, g_ref[...])
            dr_ref[...] = dr
            drb_ref[...] = dr.astype(BF16)
            dg_ref[...] += jnp.sum(dy * xhat, axis=0, keepdims=True)
            db_ref[...] += jnp.sum(dy, axis=0, keepdims=True)

    row = pl.BlockSpec((tm, d_), lambda i, k: (i, 0))
    vec = pl.BlockSpec((1, d_), lambda i, k: (0, 0))
    return pl.pallas_call(
        body, name=name, grid=(t_ // tm, nk),
        in_specs=[pl.BlockSpec((tm, tk), lambda i, k: (i, k)),
                  pl.BlockSpec((tk, d_), lambda i, k: (k, 0)), row, row, vec],
        out_specs=[row, row, vec, vec],
        out_shape=[jax.ShapeDtypeStruct((t_, d_), F32), jax.ShapeDtypeStruct((t_, d_), BF16),
                   jax.ShapeDtypeStruct((1, d_), F32), jax.ShapeDtypeStruct((1, d_), F32)],
        scratch_shapes=[pltpu.VMEM((tm, d_), F32)],
        compiler_params=_params(),
    )(a, w, dres, r, g)


def _mm_nn_res(a, w, dres, name):
    t_, k_ = a.shape
    d_ = w.shape[1]
    tm, tk = _tile(t_, 512, 8), _tile(k_, 512)
    nk = k_ // tk

    def body(a_ref, w_ref, dres_ref, o_ref, acc):
        k = pl.program_id(1)

        @pl.when(k == 0)
        def _():
            acc[...] = jnp.zeros_like(acc)

        acc[...] += _dot(a_ref[...], w_ref[...], NN)

        @pl.when(k == nk - 1)
        def _():
            o_ref[...] = acc[...] + ALPHA * dres_ref[...]

    row = pl.BlockSpec((tm, d_), lambda i, k: (i, 0))
    return pl.pallas_call(
        body, name=name, grid=(t_ // tm, nk),
        in_specs=[pl.BlockSpec((tm, tk), lambda i, k: (i, k)),
                  pl.BlockSpec((tk, d_), lambda i, k: (k, 0)), row],
        out_specs=row,
        out_shape=jax.ShapeDtypeStruct((t_, d_), F32),
        scratch_shapes=[pltpu.VMEM((tm, d_), F32)],
        compiler_params=_params(),
    )(a, w, dres)


def _dz_gate_bwd(drb, w_out, proj, ya, yb, d_, name):
    t_ = drb.shape[0]
    goff = 5 * d_ + QKV_WIDTH
    tm, tn = _tile(t_, 512, 8), _tile_multi([d_, goff], 512)
    ja, jb = goff // tn, (goff + d_) // tn

    def body(dr_ref, w_ref, ga_ref, gb_ref, ya_ref, yb_ref, dya_ref, dyb_ref, dga_ref, dgb_ref):
        dz = _dot(dr_ref[...], w_ref[...], NT)
        sa, sb = _sigmoid(ga_ref[...]), _sigmoid(gb_ref[...])
        dya_ref[...] = (dz * sa).astype(BF16)
        dyb_ref[...] = (dz * sb).astype(BF16)
        dga_ref[...] = (dz * ya_ref[...] * sa * (1.0 - sa)).astype(BF16)
        dgb_ref[...] = (dz * yb_ref[...] * sb * (1.0 - sb)).astype(BF16)

    tile = pl.BlockSpec((tm, tn), lambda i, j: (i, j))
    return pl.pallas_call(
        body, name=name, grid=(t_ // tm, d_ // tn),
        in_specs=[pl.BlockSpec((tm, d_), lambda i, j: (i, 0)),
                  pl.BlockSpec((tn, d_), lambda i, j: (j, 0)),
                  pl.BlockSpec((tm, tn), lambda i, j: (i, ja + j)),
                  pl.BlockSpec((tm, tn), lambda i, j: (i, jb + j)), tile, tile],
        out_specs=[tile] * 4,
        out_shape=[jax.ShapeDtypeStruct((t_, d_), BF16)] * 4,
        compiler_params=_params(),
    )(drb, w_out, proj, proj, ya, yb)


def _ln_loss_bwd(r, target, g, b, name):
    t_, d_ = r.shape
    tm = _tile(t_, 256, 8)

    def body(r_ref, t_ref, g_ref, b_ref, dr_ref, drb_ref, dg_ref, db_ref, loss_ref):
        i = pl.program_id(0)

        @pl.when(i == 0)
        def _():
            dg_ref[...] = jnp.zeros_like(dg_ref)
            db_ref[...] = jnp.zeros_like(db_ref)
            loss_ref[...] = jnp.zeros_like(loss_ref)

        xhat, rstd = _ln_stats(r_ref[...])
        gain = g_ref[...]
        err = xhat * gain + b_ref[...] - t_ref[...]
        loss_ref[...] += (0.5 / d_) * jnp.sum(err * err)
        dy = err * (1.0 / d_)
        dr = _ln_bwd(dy, xhat, rstd, gain)
        dr_ref[...] = dr
        drb_ref[...] = dr.astype(BF16)
        dg_ref[...] += jnp.sum(dy * xhat, axis=0, keepdims=True)
        db_ref[...] += jnp.sum(dy, axis=0, keepdims=True)

    row = pl.BlockSpec((tm, d_), lambda i: (i, 0))
    vec = pl.BlockSpec((1, d_), lambda i: (0, 0))
    return pl.pallas_call(
        body, name=name, grid=(t_ // tm,),
        in_specs=[row, row, vec, vec],
        out_specs=[row, row, vec, vec, pl.BlockSpec((1, HEAD), lambda i: (0, 0))],
        out_shape=[jax.ShapeDtypeStruct((t_, d_), F32), jax.ShapeDtypeStruct((t_, d_), BF16),
                   jax.ShapeDtypeStruct((1, d_), F32), jax.ShapeDtypeStruct((1, d_), F32),
                   jax.ShapeDtypeStruct((1, HEAD), F32)],
        compiler_params=_params(),
    )(r, target, g, b)


def _chunk_scan(x, row, reverse, size):
    s = 1
    while s < CHUNK:
        if reverse:
            x = x + jnp.where(row < CHUNK - s, pltpu.roll(x, size - s, 0), 0.0)
        else:
            x = x + jnp.where(row >= s, pltpu.roll(x, s, 0), 0.0)
        s *= 2
    return x


def _lower_bound(tab):
    return _sigmoid(tab[0:1, :] - tab[1:2, :])


def _tri_mask(reverse):
    r = lax.broadcasted_iota(jnp.int32, (CHUNK, CHUNK), 0)
    c = lax.broadcasted_iota(jnp.int32, (CHUNK, CHUNK), 1)
    return (c >= r) if reverse else (r >= c)


def _hgrn_fwd(proj, lbf, lbb, ng, b_, s_, d_, name):
    h_ = d_ // HEAD
    nc = s_ // CHUNK

    def body(hq_ref, hff_ref, hfb_ref, hi_ref, hog_ref, lbf_ref, lbb_ref, ng_ref, ya_ref, o_ref,
             q_s, k_s, cum_s, o_s):
        row = lax.broadcasted_iota(jnp.int32, (s_, HEAD), 0) % CHUNK
        hq = hq_ref[...]
        q_s[...] = hq * _sigmoid(hq)
        o_s[...] = jnp.zeros_like(o_s)
        for reverse, hf_ref, lb_ref in ((False, hff_ref, lbf_ref), (True, hfb_ref, lbb_ref)):
            lb = _lower_bound(lb_ref[...])
            f = lb + (1.0 - lb) * _sigmoid(hf_ref[...])
            k_s[...] = 1.0 - f
            cum_s[...] = _chunk_scan(jnp.log(f), row, reverse, s_)
            mask = _tri_mask(reverse)

            def step(n, st, reverse=reverse, mask=mask):
                idx = (nc - 1 - n) if reverse else n
                sl = pl.ds(pl.multiple_of(idx * CHUNK, CHUNK), CHUNK)
                cm = cum_s[sl, :]
                tot = cm[0:1, :] if reverse else cm[CHUNK - 1:CHUNK, :]
                qc, kc = q_s[sl, :], k_s[sl, :]
                vb = hi_ref[sl, :].astype(BF16)
                qd = (qc * jnp.exp(cm)).astype(BF16)
                kd = (kc * jnp.exp(-cm)).astype(BF16)
                ke = (kc * jnp.exp(tot - cm)).astype(BF16)
                a = jnp.where(mask, _dot(qd, kd, NT), 0.0)
                o_s[sl, :] += _dot(a.astype(BF16), vb, NN) + _dot(qd, st.astype(BF16), NT)
                return st * jnp.exp(tot) + _dot(vb, ke, TN)

            lax.fori_loop(0, nc, step, jnp.zeros((HEAD, HEAD), F32))
        o = o_s[...]
        o_ref[...] = o
        nrm = o * lax.rsqrt(jnp.mean(o * o, axis=-1, keepdims=True) + LN_EPS)
        hog = hog_ref[...]
        ya_ref[...] = (nrm * ng_ref[...] * hog * _sigmoid(hog)).astype(BF16)

    def col(part):
        return pl.BlockSpec((s_, HEAD), lambda h, b, part=part: (b, part * h_ + h))

    tab = pl.BlockSpec((2, HEAD), lambda h, b: (0, h))
    out = pl.BlockSpec((s_, HEAD), lambda h, b: (b, h))
    return pl.pallas_call(
        body, name=name, grid=(h_, b_),
        in_specs=[col(0), col(1), col(2), col(3), col(4), tab, tab,
                  pl.BlockSpec((1, HEAD), lambda h, b: (0, h))],
        out_specs=[out, out],
        out_shape=[jax.ShapeDtypeStruct((b_ * s_, d_), BF16), jax.ShapeDtypeStruct((b_ * s_, d_), F32)],
        scratch_shapes=[pltpu.VMEM((s_, HEAD), F32)] * 4,
        compiler_params=_params(),
    )(proj, proj, proj, proj, proj, lbf, lbb, ng)


def _hgrn_bwd(proj, lbf, lbb, ng, o_sum, dya, b_, s_, d_, name):
    h_ = d_ // HEAD
    nc = s_ // CHUNK

    def body(hq_ref, hff_ref, hfb_ref, hi_ref, hog_ref, lbf_ref, lbb_ref, ng_ref, o_ref, dya_ref,
             dhq_ref, dhff_ref, dhfb_ref, dhi_ref, dhog_ref, dng_ref, dlbf_ref, dlbb_ref,
             q_s, k_s, cum_s, do_s, dq_s, dv_s, db_s, dk_s, st_s):
        b = pl.program_id(1)

        @pl.when(b == 0)
        def _():
            dng_ref[...] = jnp.zeros_like(dng_ref)
            dlbf_ref[...] = jnp.zeros_like(dlbf_ref)
            dlbb_ref[...] = jnp.zeros_like(dlbb_ref)

        row = lax.broadcasted_iota(jnp.int32, (s_, HEAD), 0) % CHUNK
        crow = lax.broadcasted_iota(jnp.int32, (CHUNK, HEAD), 0)
        hq = hq_ref[...]
        sq = _sigmoid(hq)
        q_s[...] = hq * sq
        o = o_ref[...]
        rinv = lax.rsqrt(jnp.mean(o * o, axis=-1, keepdims=True) + LN_EPS)
        nrm = o * rinv
        hog = hog_ref[...]
        so = _sigmoid(hog)
        gain = ng_ref[...]
        dy = dya_ref[...]
        dhog_ref[...] = (dy * nrm * gain * _dsilu(hog, so)).astype(BF16)
        dng_ref[...] += jnp.sum(dy * nrm * hog * so, axis=0, keepdims=True)
        dn = dy * gain * hog * so
        do_s[...] = rinv * (dn - nrm * jnp.mean(dn * nrm, axis=-1, keepdims=True))
        dq_s[...] = jnp.zeros_like(dq_s)
        dv_s[...] = jnp.zeros_like(dv_s)

        for reverse, hf_ref, lb_ref, dhf_ref, dlb_ref in (
                (False, hff_ref, lbf_ref, dhff_ref, dlbf_ref), (True, hfb_ref, lbb_ref, dhfb_ref, dlbb_ref)):
            tab = lb_ref[...]
            lb = _lower_bound(tab)
            sf = _sigmoid(hf_ref[...])
            f = lb + (1.0 - lb) * sf
            k_s[...] = 1.0 - f
            cum_s[...] = _chunk_scan(jnp.log(f), row, reverse, s_)
            mask = _tri_mask(reverse)
            last = 0 if reverse else CHUNK - 1

            def chunk(idx, reverse=reverse):
                sl = pl.ds(pl.multiple_of(idx * CHUNK, CHUNK), CHUNK)
                cm = cum_s[sl, :]
                tot = cm[0:1, :] if reverse else cm[CHUNK - 1:CHUNK, :]
                return sl, cm, tot

            def fstep(n, st, reverse=reverse, chunk=chunk):
                idx = (nc - 1 - n) if reverse else n
                sl, cm, tot = chunk(idx)
                st_s[idx] = st
                ke = (k_s[sl, :] * jnp.exp(tot - cm)).astype(BF16)
                return st * jnp.exp(tot) + _dot(hi_ref[sl, :].astype(BF16), ke, TN)

            lax.fori_loop(0, nc, fstep, jnp.zeros((HEAD, HEAD), F32))

            def bstep(n, dst, reverse=reverse, chunk=chunk, mask=mask, last=last):
                idx = n if reverse else (nc - 1 - n)
                sl, cm, tot = chunk(idx)
                eb, enb, ee, dec = jnp.exp(cm), jnp.exp(-cm), jnp.exp(tot - cm), jnp.exp(tot)
                qc, kc = q_s[sl, :], k_s[sl, :]
                qd, kd, ke = qc * eb, kc * enb, kc * ee
                qdb, kdb, keb = qd.astype(BF16), kd.astype(BF16), ke.astype(BF16)
                vb = hi_ref[sl, :].astype(BF16)
                dob = do_s[sl, :].astype(BF16)
                st0 = st_s[idx]
                dstb = dst.astype(BF16)
                a = jnp.where(mask, _dot(qdb, kdb, NT), 0.0).astype(BF16)
                da = jnp.where(mask, _dot(dob, vb, NT), 0.0).astype(BF16)
                dqd = _dot(da, kdb, NN) + _dot(dob, st0.astype(BF16), NN)
                dkd = _dot(da, qdb, TN)
                dv = _dot(a, dob, TN) + _dot(keb, dstb, NT)
                dke = _dot(vb, dstb, NN)
                ddec = jnp.sum(dst * st0, axis=0, keepdims=True)
                dtot = jnp.sum(dke * ke, axis=0, keepdims=True) + ddec * dec
                db = dqd * qd - dkd * kd - dke * ke
                db_s[sl, :] = db + jnp.where(crow == last, dtot, 0.0)
                dk_s[sl, :] = dkd * enb + dke * ee
                dq_s[sl, :] += dqd * eb
                dv_s[sl, :] += dv
                return dst * dec + _dot(dob, qdb, TN)

            lax.fori_loop(0, nc, bstep, jnp.zeros((HEAD, HEAD), F32))
            dlogf = _chunk_scan(db_s[...], row, not reverse, s_)
            df = dlogf / f - dk_s[...]
            dhf_ref[...] = (df * (1.0 - lb) * sf * (1.0 - sf)).astype(BF16)
            dlb = jnp.sum(df * (1.0 - sf), axis=0, keepdims=True) * lb * (1.0 - lb)
            dlb_ref[0:1, :] += dlb
            dlb_ref[1:2, :] -= dlb

        dhq_ref[...] = (dq_s[...] * _dsilu(hq, sq)).astype(BF16)
        dhi_ref[...] = dv_s[...].astype(BF16)

    def col(part):
        return pl.BlockSpec((s_, HEAD), lambda h, b, part=part: (b, part * h_ + h))

    tab = pl.BlockSpec((2, HEAD), lambda h, b: (0, h))
    vec = pl.BlockSpec((1, HEAD), lambda h, b: (0, h))
    blk = pl.BlockSpec((s_, HEAD), lambda h, b: (b, h))
    act = jax.ShapeDtypeStruct((b_ * s_, d_), BF16)
    return pl.pallas_call(
        body, name=name, grid=(h_, b_),
        in_specs=[col(0), col(1), col(2), col(3), col(4), tab, tab, vec, blk, blk],
        out_specs=[blk] * 5 + [vec, tab, tab],
        out_shape=[act] * 5 + [jax.ShapeDtypeStruct((1, d_), F32), jax.ShapeDtypeStruct((2, d_), F32),
                               jax.ShapeDtypeStruct((2, d_), F32)],
        scratch_shapes=[pltpu.VMEM((s_, HEAD), F32)] * 8 + [pltpu.VMEM((nc, HEAD, HEAD), F32)],
        compiler_params=_params(),
    )(proj, proj, proj, proj, proj, lbf, lbb, ng, o_sum, dya)


def _rope_tables(s_):
    half = ROPE_DIM // 2
    inv_freq = ROPE_THETA ** (-jnp.arange(0, ROPE_DIM, 2, dtype=F32) / ROPE_DIM)
    ang = jnp.arange(s_, dtype=F32)[:, None] * inv_freq
    cos, sin = jnp.cos(ang), jnp.sin(ang)
    zeros = jnp.zeros((s_, HEAD - ROPE_DIM), F32)
    zh = jnp.zeros((s_, half), F32)
    c = jnp.concatenate([cos, cos, jnp.ones((s_, HEAD - ROPE_DIM), F32)], axis=1)
    s1 = jnp.concatenate([-sin, zh, zeros], axis=1)
    s2 = jnp.concatenate([zh, sin, zeros], axis=1)
    return c, s1, s2


def _rope(t, c, s1, s2):
    half = ROPE_DIM // 2
    return t * c + pltpu.roll(t, HEAD - half, 1) * s1 + pltpu.roll(t, half, 1) * s2


def _rope_bwd(dt, c, s1, s2):
    half = ROPE_DIM // 2
    return dt * c + pltpu.roll(dt * s1, half, 1) + pltpu.roll(dt * s2, HEAD - half, 1)


def _window_mask(r0, qb, wk, seg):
    row = lax.broadcasted_iota(jnp.int32, (qb, wk), 0)
    col = lax.broadcasted_iota(jnp.int32, (qb, wk), 1)
    kj = r0 - ATTN_HALF + col
    return (col - row >= 0) & (col - row <= 2 * ATTN_HALF) & (kj >= 0) & (kj < seg)


def _attn_fwd(qkv, tabs, b_, s_, dil, name):
    seg = s_ // dil
    qb = min(128, seg)
    nq, wk = seg // qb, qb + 2 * ATTN_HALF
    scale = HEAD ** -0.5
    ncol = QKV_GROUP // HEAD

    def body(q_ref, k_ref, v_ref, c_ref, s1_ref, s2_ref, o_ref, lse_ref, q_s, k_s, v_s):
        c, s1, s2 = c_ref[...], s1_ref[...], s2_ref[...]
        q_s[...] = _rope(q_ref[...], c, s1, s2).astype(BF16)
        k_s[...] = jnp.zeros_like(k_s)
        v_s[...] = jnp.zeros_like(v_s)
        k_s[ATTN_HALF:ATTN_HALF + seg, :] = _rope(k_ref[...], c, s1, s2).astype(BF16)
        v_s[ATTN_HALF:ATTN_HALF + seg, :] = v_ref[...].astype(BF16)

        def step(i, carry):
            r0 = pl.multiple_of(i * qb, qb)
            sc = _dot(q_s[pl.ds(r0, qb), :], k_s[pl.ds(r0, wk), :], NT) * scale
            sc = jnp.where(_window_mask(r0, qb, wk, seg), sc, NEG_INF)
            m = jnp.max(sc, axis=-1, keepdims=True)
            p = jnp.exp(sc - m)
            den = jnp.sum(p, axis=-1, keepdims=True)
            o_ref[pl.ds(r0, qb), :] = _dot(p.astype(BF16), v_s[pl.ds(r0, wk), :], NN) / den
            lse_ref[pl.ds(r0, qb), :] = jnp.broadcast_to(m + jnp.log(den), (qb, HEAD))
            return carry

        lax.fori_loop(0, nq, step, 0)

    def col(part):
        return pl.BlockSpec((seg, HEAD), lambda b, r, h, part=part: (b, r * ncol + part * ATTN_HEADS + h))

    tab = pl.BlockSpec((seg, HEAD), lambda b, r, h: (0, r))
    out = pl.BlockSpec((seg, HEAD), lambda b, r, h: (b, r * ATTN_HEADS + h))
    shape = jax.ShapeDtypeStruct((b_ * seg, dil * ATTN_OUT), F32)
    return pl.pallas_call(
        body, name=name, grid=(b_, dil, ATTN_HEADS),
        in_specs=[col(0), col(1), col(2), tab, tab, tab],
        out_specs=[out, out],
        out_shape=[shape, shape],
        scratch_shapes=[pltpu.VMEM((seg, HEAD), BF16), pltpu.VMEM((seg + 2 * ATTN_HALF, HEAD), BF16),
                        pltpu.VMEM((seg + 2 * ATTN_HALF, HEAD), BF16)],
        compiler_params=_params(),
    )(qkv, qkv, qkv, *tabs)


def _attn_bwd(qkv, tabs, dog, cg, lse, b_, s_, dil, name):
    seg = s_ // dil
    qb = min(128, seg)
    nq, wk = seg // qb, qb + 2 * ATTN_HALF
    scale = HEAD ** -0.5
    ncol = QKV_GROUP // HEAD

    def body(q_ref, k_ref, v_ref, c_ref, s1_ref, s2_ref, do_ref, cg_ref, lse_ref, dq_ref, dk_ref, dv_ref,
             q_s, k_s, v_s, dk_s, dv_s):
        c, s1, s2 = c_ref[...], s1_ref[...], s2_ref[...]
        q_s[...] = _rope(q_ref[...], c, s1, s2).astype(BF16)
        k_s[...] = jnp.zeros_like(k_s)
        v_s[...] = jnp.zeros_like(v_s)
        k_s[ATTN_HALF:ATTN_HALF + seg, :] = _rope(k_ref[...], c, s1, s2).astype(BF16)
        v_s[ATTN_HALF:ATTN_HALF + seg, :] = v_ref[...].astype(BF16)
        dk_s[...] = jnp.zeros_like(dk_s)
        dv_s[...] = jnp.zeros_like(dv_s)

        def step(i, carry):
            r0 = pl.multiple_of(i * qb, qb)
            rows, win = pl.ds(r0, qb), pl.ds(r0, wk)
            qc, kw, vw = q_s[rows, :], k_s[win, :], v_s[win, :]
            sc = _dot(qc, kw, NT) * scale
            p = jnp.where(_window_mask(r0, qb, wk, seg), jnp.exp(sc - lse_ref[rows, 0:1]), 0.0)
            dob = do_ref[rows, :].astype(BF16)
            dp = _dot(dob, vw, NT)
            ds = (p * (dp + cg_ref[rows, 0:1]) * scale).astype(BF16)
            dq = _dot(ds, kw, NN)
            dq_ref[rows, :] = _rope_bwd(dq, c_ref[rows, :], s1_ref[rows, :], s2_ref[rows, :])
            dk_s[win, :] += _dot(ds, qc, TN)
            dv_s[win, :] += _dot(p.astype(BF16), dob, TN)
            return carry

        lax.fori_loop(0, nq, step, 0)
        dk_ref[...] = _rope_bwd(dk_s[ATTN_HALF:ATTN_HALF + seg, :], c, s1, s2)
        dv_ref[...] = dv_s[ATTN_HALF:ATTN_HALF + seg, :]

    def col(part):
        return pl.BlockSpec((seg, HEAD), lambda b, r, h, part=part: (b, r * ncol + part * ATTN_HEADS + h))

    tab = pl.BlockSpec((seg, HEAD), lambda b, r, h: (0, r))
    out = pl.BlockSpec((seg, HEAD), lambda b, r, h: (b, r * ATTN_HEADS + h))
    shape = jax.ShapeDtypeStruct((b_ * seg, dil * ATTN_OUT), F32)
    return pl.pallas_call(
        body, name=name, grid=(b_, dil, ATTN_HEADS),
        in_specs=[col(0), col(1), col(2), tab, tab, tab, out, out, out],
        out_specs=[out, out, out],
        out_shape=[shape, shape, shape],
        scratch_shapes=[pltpu.VMEM((seg, HEAD), BF16), pltpu.VMEM((seg + 2 * ATTN_HALF, HEAD), BF16),
                        pltpu.VMEM((seg + 2 * ATTN_HALF, HEAD), BF16),
                        pltpu.VMEM((seg + 2 * ATTN_HALF, HEAD), F32), pltpu.VMEM((seg + 2 * ATTN_HALF, HEAD), F32)],
        compiler_params=_params(),
    )(qkv, qkv, qkv, *tabs, dog, cg, lse)


def _group_weights(lses):
    m = jnp.maximum(jnp.maximum(lses[0], lses[1]), lses[2])
    es = [jnp.exp(l - m) for l in lses]
    den = es[0] + es[1] + es[2]
    return [e / den for e in es]


def _combine_fwd(outs, lses, name):
    t_, w_ = outs[0].shape
    tm = _tile(t_, 512, 8)
    ng = len(outs)

    def body(*refs):
        ws = _group_weights([r[...] for r in refs[ng:2 * ng]])
        acc = ws[0] * refs[0][...]
        for g in range(1, ng):
            acc = acc + ws[g] * refs[g][...]
        refs[2 * ng][...] = acc.astype(BF16)

    row = pl.BlockSpec((tm, w_), lambda i: (i, 0))
    return pl.pallas_call(
        body, name=name, grid=(t_ // tm,), in_specs=[row] * (2 * ng), out_specs=row,
        out_shape=jax.ShapeDtypeStruct((t_, w_), BF16), compiler_params=_params(),
    )(*outs, *lses)


def _combine_bwd(dob, outs, lses, name):
    t_, w_ = outs[0].shape
    tm = _tile(t_, 512, 8)
    ng = len(outs)

    def body(*refs):
        do = refs[0][...]
        os_ = [r[...] for r in refs[1:1 + ng]]
        ws = _group_weights([r[...] for r in refs[1 + ng:1 + 2 * ng]])
        o = ws[0] * os_[0]
        for g in range(1, ng):
            o = o + ws[g] * os_[g]
        prod = do * o
        heads = [jnp.broadcast_to(jnp.sum(prod[:, h * HEAD:(h + 1) * HEAD], axis=-1, keepdims=True), (tm, HEAD))
                 for h in range(w_ // HEAD)]
        tot = jnp.concatenate(heads, axis=1)
        for g in range(ng):
            refs[1 + 2 * ng + g][...] = ws[g] * do
            refs[1 + 3 * ng + g][...] = -ws[g] * tot

    row = pl.BlockSpec((tm, w_), lambda i: (i, 0))
    shape = jax.ShapeDtypeStruct((t_, w_), F32)
    res = pl.pallas_call(
        body, name=name, grid=(t_ // tm,), in_specs=[row] * (1 + 2 * ng), out_specs=[row] * (2 * ng),
        out_shape=[shape] * (2 * ng), compiler_params=_params(),
    )(dob, *outs, *lses)
    return res[:ng], res[ng:]


def _adam_update(w, g, m, v):
    m = ADAM_B1 * m + (1.0 - ADAM_B1) * g
    v = ADAM_B2 * v + (1.0 - ADAM_B2) * (g * g)
    m_hat = m / (1.0 - ADAM_B1 ** ADAM_STEP)
    v_hat = v / (1.0 - ADAM_B2 ** ADAM_STEP)
    return -ADAM_LR * (m_hat / (jnp.sqrt(v_hat) + ADAM_EPS) + ADAM_WD * w), m, v


def _adam(w, g, m, v, name):
    r_, c_ = w.shape
    tr = _tile(r_, 256, 8)

    def body(w_ref, g_ref, m_ref, v_ref, d_ref, mo_ref, vo_ref):
        d_ref[...], mo_ref[...], vo_ref[...] = _adam_update(w_ref[...], g_ref[...], m_ref[...], v_ref[...])

    blk = pl.BlockSpec((tr, c_), lambda i: (i, 0))
    shape = jax.ShapeDtypeStruct((r_, c_), F32)
    return pl.pallas_call(
        body, name=name, grid=(r_ // tr,), in_specs=[blk] * 4, out_specs=[blk] * 3,
        out_shape=[shape] * 3, compiler_params=_params(),
    )(w, g, m, v)


def _sum_partials(recv, name):
    n_, r_, c_ = recv.shape
    tr = _tile(r_, 128, 16)

    def body(p_ref, o_ref):
        acc = p_ref[0].astype(F32)
        for i in range(1, n_):
            acc = acc + p_ref[i].astype(F32)
        o_ref[...] = acc

    return pl.pallas_call(
        body, name=name, grid=(r_ // tr,),
        in_specs=[pl.BlockSpec((n_, tr, c_), lambda i: (0, i, 0))],
        out_specs=pl.BlockSpec((tr, c_), lambda i: (i, 0)),
        out_shape=jax.ShapeDtypeStruct((r_, c_), F32), compiler_params=_params(),
    )(recv)


def _small_sum_adam(parts, w, m, v, name):
    n_, r_, c_ = parts.shape

    def body(p_ref, w_ref, m_ref, v_ref, g_ref, d_ref, mo_ref, vo_ref):
        g = p_ref[0]
        for i in range(1, n_):
            g = g + p_ref[i]
        g_ref[...] = g
        d_ref[...], mo_ref[...], vo_ref[...] = _adam_update(w_ref[...], g, m_ref[...], v_ref[...])

    shape = jax.ShapeDtypeStruct((r_, c_), F32)
    return pl.pallas_call(body, name=name, out_shape=[shape] * 4, compiler_params=_params())(parts, w, m, v)


def _my_place():
    x, y, c = lax.axis_index("x"), lax.axis_index("y"), lax.axis_index("c")
    return x, y, c


def _peer(x, y, c, d):
    px = 1 - x if d & 4 else x
    py = 1 - y if d & 2 else y
    pc = 1 - c if d & 1 else c
    return (px, py, pc), 4 * px + 2 * py + pc


def _all_gather(shards, name):
    nw = len(shards)

    def body(*refs):
        ins, outs = refs[:nw], refs[nw:2 * nw]
        send_sems, recv_sems, local_sems = refs[2 * nw:]
        x, y, c = _my_place()
        me = 4 * x + 2 * y + c
        copies = []
        for k in range(nw):
            rows = shards[k].shape[0]
            mine = outs[k].at[pl.ds(pl.multiple_of(me * rows, 16), rows), :]
            local = pltpu.make_async_copy(ins[k], mine, local_sems.at[k])
            local.start()
            copies.append(local)
            for d in range(1, N_DEV):
                place, _ = _peer(x, y, c, d)
                remote = pltpu.make_async_remote_copy(
                    src_ref=ins[k], dst_ref=mine, send_sem=send_sems.at[d - 1, k], recv_sem=recv_sems.at[d - 1, k],
                    device_id=place, device_id_type=MESH)
                remote.start()
                copies.append(remote)
        for cp in copies:
            cp.wait()

    hbm = pl.BlockSpec(memory_space=pl.ANY)
    return pl.pallas_call(
        body, name=name, in_specs=[hbm] * nw, out_specs=[hbm] * nw,
        out_shape=[jax.ShapeDtypeStruct((N_DEV * s.shape[0], s.shape[1]), s.dtype) for s in shards],
        scratch_shapes=[pltpu.SemaphoreType.DMA((N_DEV - 1, nw)), pltpu.SemaphoreType.DMA((N_DEV - 1, nw)),
                        pltpu.SemaphoreType.DMA((nw,))],
    )(*shards)


HBM_SPEC = pl.BlockSpec(memory_space=pltpu.HBM)
SEM_SPEC = pl.BlockSpec(memory_space=pltpu.SEMAPHORE)
EFFECT = pltpu.SideEffectType.DATAFLOW_SIDE_EFFECTING


def _in_hbm(a):
    return pltpu.with_memory_space_constraint(a, pltpu.HBM)


def _token_shape():
    return jax.ShapeDtypeStruct((8, HEAD), F32)


def _gather_start(shards, name):
    nw = len(shards)
    lands = [lax.empty((N_DEV * s.shape[0], s.shape[1]), s.dtype) for s in shards]

    def body(*refs):
        ins, lnd = refs[:nw], refs[nw:2 * nw]
        send, recv = refs[2 * nw:3 * nw], refs[3 * nw:4 * nw]
        token, local_sems = refs[6 * nw], refs[6 * nw + 1]
        x, y, c = _my_place()
        me = 4 * x + 2 * y + c
        locals_ = []
        for k in range(nw):
            rows = shards[k].shape[0]
            mine = lnd[k].at[pl.ds(pl.multiple_of(me * rows, 16), rows), :]
            local = pltpu.make_async_copy(ins[k], mine, local_sems.at[k])
            local.start()
            locals_.append(local)
            for d in range(1, N_DEV):
                place, _ = _peer(x, y, c, d)
                pltpu.make_async_remote_copy(
                    src_ref=ins[k], dst_ref=mine, send_sem=send[k].at[d - 1], recv_sem=recv[k].at[d - 1],
                    device_id=place, device_id_type=MESH).start()
        for local in locals_:
            local.wait()
        token[...] = jnp.zeros_like(token)

    sems = [pltpu.SemaphoreType.DMA((N_DEV - 1,))] * (2 * nw)
    thru = [pltpu.HBM(a.shape, a.dtype) for a in list(shards) + lands]
    res = pl.pallas_call(
        body, name=name, out_shape=(*sems, *thru, _token_shape()),
        in_specs=[HBM_SPEC] * (2 * nw),
        out_specs=(*([SEM_SPEC] * (2 * nw)), *([HBM_SPEC] * (2 * nw)), pl.BlockSpec(memory_space=pltpu.VMEM)),
        input_output_aliases={i: 2 * nw + i for i in range(2 * nw)},
        scratch_shapes=[pltpu.SemaphoreType.DMA((nw,))],
        compiler_params=pltpu.CompilerParams(has_side_effects=EFFECT),
    )(*[_in_hbm(s) for s in shards], *[_in_hbm(l) for l in lands])
    return [dict(send=res[k], recv=res[nw + k], src=res[2 * nw + k], land=res[3 * nw + k]) for k in range(nw)]


def _gather_wait(pending, after, name):
    rows = pending["src"].shape[0]

    def body(src_ref, land_ref, send, recv, after_ref, src_dead, got):
        x, y, c = _my_place()
        me = 4 * x + 2 * y + c
        mine = land_ref.at[pl.ds(pl.multiple_of(me * rows, 16), rows), :]
        for d in range(1, N_DEV):
            place, _ = _peer(x, y, c, d)
            cp = pltpu.make_async_remote_copy(
                src_ref=src_ref, dst_ref=mine, send_sem=send.at[d - 1], recv_sem=recv.at[d - 1],
                device_id=place, device_id_type=MESH)
            cp.wait_send()
            cp.wait_recv()

    src, land = pending["src"], pending["land"]
    return pl.pallas_call(
        body, name=name, out_shape=(pltpu.HBM(src.shape, src.dtype), pltpu.HBM(land.shape, land.dtype)),
        in_specs=(HBM_SPEC, HBM_SPEC, SEM_SPEC, SEM_SPEC, pl.BlockSpec(memory_space=pl.ANY)),
        out_specs=(HBM_SPEC, HBM_SPEC), input_output_aliases={0: 0, 1: 1},
        compiler_params=pltpu.CompilerParams(has_side_effects=EFFECT),
    )(src, land, pending["send"], pending["recv"], after)[1]


def _scatter_start(full, name):
    rows, cols = full.shape[0] // N_DEV, full.shape[1]
    land = lax.empty((N_DEV, rows, cols), full.dtype)

    def body(full_ref, land_ref, send, recv, full_thru, land_thru, token, local_sem):
        x, y, c = _my_place()
        me = 4 * x + 2 * y + c
        slab = land_ref.at[me]
        local = pltpu.make_async_copy(full_ref.at[pl.ds(pl.multiple_of(me * rows, 16), rows), :], slab, local_sem)
        local.start()
        for d in range(1, N_DEV):
            place, num = _peer(x, y, c, d)
            pltpu.make_async_remote_copy(
                src_ref=full_ref.at[pl.ds(pl.multiple_of(num * rows, 16), rows), :], dst_ref=slab,
                send_sem=send.at[d - 1], recv_sem=recv.at[d - 1], device_id=place, device_id_type=MESH).start()
        local.wait()
        token[...] = jnp.zeros_like(token)

    res = pl.pallas_call(
        body, name=name,
        out_shape=(pltpu.SemaphoreType.DMA((N_DEV - 1,)), pltpu.SemaphoreType.DMA((N_DEV - 1,)),
                   pltpu.HBM(full.shape, full.dtype), pltpu.HBM(land.shape, land.dtype), _token_shape()),
        in_specs=(HBM_SPEC, HBM_SPEC),
        out_specs=(SEM_SPEC, SEM_SPEC, HBM_SPEC, HBM_SPEC, pl.BlockSpec(memory_space=pltpu.VMEM)),
        input_output_aliases={0: 2, 1: 3},
        scratch_shapes=[pltpu.SemaphoreType.DMA(())],
        compiler_params=pltpu.CompilerParams(has_side_effects=EFFECT),
    )(_in_hbm(full), _in_hbm(land))
    return dict(send=res[0], recv=res[1], src=res[2], land=res[3]), res[4]


def _scatter_wait(pending, after, name):
    rows = pending["land"].shape[1]

    def body(src_ref, land_ref, send, recv, after_ref, src_dead, got):
        x, y, c = _my_place()
        me = 4 * x + 2 * y + c
        for d in range(1, N_DEV):
            place, num = _peer(x, y, c, d)
            cp = pltpu.make_async_remote_copy(
                src_ref=src_ref.at[pl.ds(pl.multiple_of(num * rows, 16), rows), :], dst_ref=land_ref.at[me],
                send_sem=send.at[d - 1], recv_sem=recv.at[d - 1], device_id=place, device_id_type=MESH)
            cp.wait_send()
            cp.wait_recv()

    src, land = pending["src"], pending["land"]
    return pl.pallas_call(
        body, name=name, out_shape=(pltpu.HBM(src.shape, src.dtype), pltpu.HBM(land.shape, land.dtype)),
        in_specs=(HBM_SPEC, HBM_SPEC, SEM_SPEC, SEM_SPEC, pl.BlockSpec(memory_space=pl.ANY)),
        out_specs=(HBM_SPEC, HBM_SPEC), input_output_aliases={0: 0, 1: 1},
        compiler_params=pltpu.CompilerParams(has_side_effects=EFFECT),
    )(src, land, pending["send"], pending["recv"], after)[1]


BIG = ("ffn1_w_in", "ffn1_w_out", "mix_w_in", "w_branch_a", "w_branch_b", "mix_w_out", "ffn2_w_in", "ffn2_w_out")
TRANSPOSED = ("ffn1_w_in", "mix_w_in", "w_branch_b", "ffn2_w_in")
SMALL = ("ln1_g", "ln1_b", "ln2_g", "ln2_b", "ln3_g", "ln3_b", "hgrn_norm_g", "hgrn_lb_fwd", "hgrn_lb_bwd")
SMALL_ROWS = 16


def _local_step(x, target, weight, emit, sp):
    b_, s_, d_ = x.shape
    t_ = b_ * s_
    x2, tgt = x.reshape(t_, d_), target.reshape(t_, d_)
    xb = x2.astype(BF16)
    w1i = weight("ffn1_w_in", xb)
    g1, u1, a1 = _ffn_in_fwd(xb, w1i, "ffn1_in")
    w1o = weight("ffn1_w_out", a1)
    r1, h1, h1b = _mm_res_ln_fwd(a1, w1o, x2, sp["ln1_g"], sp["ln1_b"], 0.5, "ffn1_out_ln1")
    wmx = weight("mix_w_in", h1b)
    proj = _mm_nt(h1b, wmx, F32, "mix_in")
    ya_in, o_sum = _hgrn_fwd(proj, sp["hgrn_lb_fwd"], sp["hgrn_lb_bwd"], sp["hgrn_norm_g"], b_, s_, d_, "hgrn_fwd")
    tabs = _rope_tables(s_)
    qkvs, gtabs, outs, lses = [], [], [], []
    for gi, (_, dil) in enumerate(ATTN_GROUPS):
        off = 5 * d_ + gi * QKV_GROUP
        qkv = proj[:, off:off + QKV_GROUP].reshape(t_ // dil, dil * QKV_GROUP)
        gt = [tb.reshape(s_ // dil, dil * HEAD) for tb in tabs]
        o_g, lse_g = _attn_fwd(qkv, gt, b_, s_, dil, f"attn_fwd_{gi}")
        qkvs.append(qkv)
        gtabs.append(gt)
        outs.append(o_g.reshape(t_, ATTN_OUT))
        lses.append(lse_g.reshape(t_, ATTN_OUT))
    ob = _combine_fwd(outs, lses, "attn_combine")
    wa, wb = weight("w_branch_a", ya_in), weight("w_branch_b", ob)
    ya, yb, z = _gate_out_fwd(ya_in, ob, wa, wb, proj, d_, "branch_gate")
    wo = weight("mix_w_out", z)
    r2, h2, h2b = _mm_res_ln_fwd(z, wo, h1, sp["ln2_g"], sp["ln2_b"], 1.0, "mix_out_ln2")
    w2i = weight("ffn2_w_in", h2b)
    g2, u2, a2 = _ffn_in_fwd(h2b, w2i, "ffn2_in")
    w2o = weight("ffn2_w_out", a2)
    r3, _, _ = _mm_res_ln_fwd(a2, w2o, h2, sp["ln3_g"], sp["ln3_b"], 0.5, "ffn2_out_ln3")
    dr3, dr3b, dg3, db3, loss = _ln_loss_bwd(r3, tgt, sp["ln3_g"], sp["ln3_b"], "loss_ln3_bwd")
    dr3b = emit("ffn2_w_out", _mm_tn(a2, dr3b, 0.5, "d_ffn2_w_out"), dr3b)
    dgate2, dup2 = _ffn_mid_bwd(dr3b, w2o, g2, u2, 0.5, "ffn2_mid_bwd")
    du2 = jnp.concatenate([dgate2, dup2], axis=1)
    du2 = emit("ffn2_w_in", _mm_tn(du2, h2b, 1.0, "d_ffn2_w_in"), du2)
    dr2, dr2b, dg2, db2 = _mm_nn_res_lnbwd(du2, w2i, dr3, r2, sp["ln2_g"], "ffn2_in_bwd_ln2")
    dr2b = emit("mix_w_out", _mm_tn(z, dr2b, 1.0, "d_mix_w_out"), dr2b)
    dya, dyb, dga, dgb = _dz_gate_bwd(dr2b, wo, proj, ya, yb, d_, "branch_gate_bwd")
    dya = emit("w_branch_a", _mm_tn(ya_in, dya, 1.0, "d_w_branch_a"), dya)
    dyb = emit("w_branch_b", _mm_tn(dyb, ob, 1.0, "d_w_branch_b"), dyb)
    dya_in = _mm_nt(dya, wa, F32, "branch_a_bwd")
    dob = _mm_nn(dyb, wb, F32, "branch_b_bwd")
    dhq, dhff, dhfb, dhi, dhog, dng, dlbf, dlbb = _hgrn_bwd(
        proj, sp["hgrn_lb_fwd"], sp["hgrn_lb_bwd"], sp["hgrn_norm_g"], o_sum, dya_in, b_, s_, d_, "hgrn_bwd")
    dogs, cgs = _combine_bwd(dob, outs, lses, "attn_combine_bwd")
    dqkv = []
    for gi, (_, dil) in enumerate(ATTN_GROUPS):
        seg_rows = t_ // dil
        dq, dk, dv = _attn_bwd(qkvs[gi], gtabs[gi], dogs[gi].reshape(seg_rows, dil * ATTN_OUT),
                               cgs[gi].reshape(seg_rows, dil * ATTN_OUT), lses[gi].reshape(seg_rows, dil * ATTN_OUT),
                               b_, s_, dil, f"attn_bwd_{gi}")
        dqkv += [t.reshape(t_, ATTN_OUT).astype(BF16) for t in (dq, dk, dv)]
    dproj = jnp.concatenate([dhq, dhff, dhfb, dhi, dhog] + dqkv + [dga, dgb], axis=1)
    dproj = emit("mix_w_in", _mm_tn(dproj, h1b, 1.0, "d_mix_w_in"), dproj)
    dr1, dr1b, dg1, db1 = _mm_nn_res_lnbwd(dproj, wmx, dr2, r1, sp["ln1_g"], "mix_in_bwd_ln1")
    dr1b = emit("ffn1_w_out", _mm_tn(a1, dr1b, 0.5, "d_ffn1_w_out"), dr1b)
    dgate1, dup1 = _ffn_mid_bwd(dr1b, w1o, g1, u1, 0.5, "ffn1_mid_bwd")
    du1 = jnp.concatenate([dgate1, dup1], axis=1)
    du1 = emit("ffn1_w_in", _mm_tn(du1, xb, 1.0, "d_ffn1_w_in"), du1)
    grad_x = _mm_nn_res(du1, w1i, dr1, "ffn1_in_bwd")
    small = {"ln1_g": dg1, "ln1_b": db1, "ln2_g": dg2, "ln2_b": db2, "ln3_g": dg3, "ln3_b": db3,
             "hgrn_norm_g": dng, "hgrn_lb_fwd": dlbf, "hgrn_lb_bwd": dlbb}
    return loss, grad_x.reshape(b_, s_, d_), small


def _pack_small(vals):
    rows = jnp.concatenate([vals[n] for n in SMALL], axis=0)
    return jnp.pad(rows, ((0, SMALL_ROWS - rows.shape[0]), (0, 0)))


def _unpack_small(packed):
    out, r = {}, 0
    for n in SMALL:
        k = 2 if n.startswith("hgrn_lb") else 1
        out[n] = packed[r:r + k]
        r += k
    return out


def kernel(x, ffn1_w_in, ffn1_w_out, ln1_g, ln1_b, mix_w_in, hgrn_lb_fwd, hgrn_lb_bwd, hgrn_norm_g, w_branch_a, w_branch_b, mix_w_out, ln2_g, ln2_b, ffn2_w_in, ffn2_w_out, ln3_g, ln3_b, loss_target, m_ffn1_w_in, m_ffn1_w_out, m_ln1_g, m_ln1_b, m_mix_w_in, m_hgrn_lb_fwd, m_hgrn_lb_bwd, m_hgrn_norm_g, m_w_branch_a, m_w_branch_b, m_mix_w_out, m_ln2_g, m_ln2_b, m_ffn2_w_in, m_ffn2_w_out, m_ln3_g, m_ln3_b, v_ffn1_w_in, v_ffn1_w_out, v_ln1_g, v_ln1_b, v_mix_w_in, v_hgrn_lb_fwd, v_hgrn_lb_bwd, v_hgrn_norm_g, v_w_branch_a, v_w_branch_b, v_mix_w_out, v_ln2_g, v_ln2_b, v_ffn2_w_in, v_ffn2_w_out, v_ln3_g, v_ln3_b):
    args = dict(locals())
    big_w = {n: args[n][0] for n in BIG}
    sp = {n: args[n] for n in SMALL}
    shards = [(big_w[n].T if n in TRANSPOSED else big_w[n]).astype(BF16) for n in BIG]
    gathering = dict(zip(BIG, _gather_start(shards, "gather_start")))
    scattering = {}

    def weight(n, after):
        return _gather_wait(gathering[n], after, f"gather_wait_{n}")

    def emit(n, grad, nxt):
        scattering[n], token = _scatter_start(grad, f"scatter_start_{n}")
        return lax.optimization_barrier((nxt, token))[0]

    loss_part, grad_x, small = _local_step(x, loss_target, weight, emit, sp)
    loss = lax.psum(loss_part[0, 0], ("x", "y", "c"))
    out_g, out_d, out_m, out_v = {}, {}, {}, {}
    for n in BIG:
        g = _sum_partials(_scatter_wait(scattering[n], grad_x, f"scatter_wait_{n}"), f"sum_{n}")
        if n in TRANSPOSED:
            g = g.T
        d_w, m_w, v_w = _adam(big_w[n], g, args["m_" + n][0], args["v_" + n][0], f"adam_{n}")
        out_g[n], out_d[n], out_m[n], out_v[n] = g[None], d_w[None], m_w[None], v_w[None]
    (parts,) = _all_gather([_pack_small(small)], "gather_small_grads")
    res = _small_sum_adam(parts.reshape(N_DEV, SMALL_ROWS, parts.shape[1]), _pack_small(sp),
                          _pack_small({n: args["m_" + n] for n in SMALL}),
                          _pack_small({n: args["v_" + n] for n in SMALL}), "small_adam")
    sg, sd, sm, sv = (_unpack_small(r) for r in res)
    out_g.update(sg), out_d.update(sd), out_m.update(sm), out_v.update(sv)
    order = ("ffn1_w_in", "ffn1_w_out", "ln1_g", "ln1_b", "mix_w_in", "hgrn_lb_fwd", "hgrn_lb_bwd", "hgrn_norm_g",
             "w_branch_a", "w_branch_b", "mix_w_out", "ln2_g", "ln2_b", "ffn2_w_in", "ffn2_w_out", "ln3_g", "ln3_b")
    return (loss, grad_x, *[out_g[n] for n in order], *[out_d[n] for n in order],
            *[out_m[n] for n in order], *[out_v[n] for n in order])
```

```python
import functools

import jax
import jax.numpy as jnp
from jax import lax
from jax.experimental import pallas as pl
from jax.experimental.pallas import tpu as pltpu

F32 = jnp.float32
BF16 = jnp.bfloat16

N_DEV = 8
HEAD = 128
CHUNK = 32
ATTN_GROUPS = ((128, 1), (512, 4), (2048, 16))
ATTN_HEADS = 4
ATTN_HALF = 64
QKV_GROUP = 3 * ATTN_HEADS * HEAD
QKV_WIDTH = len(ATTN_GROUPS) * QKV_GROUP
ATTN_OUT = ATTN_HEADS * HEAD
ROPE_THETA = 500000.0
ROPE_DIM = HEAD // 4
ALPHA = 2.0 ** 0.25
LN_EPS = 1e-5
NEG_INF = -1e30
ADAM_LR, ADAM_B1, ADAM_B2, ADAM_EPS, ADAM_WD, ADAM_STEP = 0.001, 0.9, 0.999, 1e-08, 0.01, 10
VMEM_LIMIT = 56 * 1024 * 1024

NT = (((1,), (1,)), ((), ()))
NN = (((1,), (0,)), ((), ()))
TN = (((0,), (0,)), ((), ()))
MESH = pl.DeviceIdType.MESH


def _dot(a, b, dims):
    return lax.dot_general(a, b, dims, preferred_element_type=F32)


def _tile(n, pref, mult=128):
    if n <= pref:
        return n
    t = (pref // mult) * mult
    while t >= mult:
        if n % t == 0:
            return t
        t -= mult
    return n


def _tile_multi(ns, pref, mult=128):
    t = (pref // mult) * mult
    while t >= mult:
        if all(n % t == 0 for n in ns):
            return t
        t -= mult
    raise ValueError(f"no common tile for {ns}")


def _params(**kw):
    return pltpu.CompilerParams(vmem_limit_bytes=VMEM_LIMIT, **kw)


def _after(body, n_in, dep):
    if dep is None:
        return body, [], []

    def wrapped(*refs):
        body(*refs[:n_in], *refs[n_in + 1:])

    return wrapped, [pl.BlockSpec(dep.shape, lambda *_: (0,) * dep.ndim)], [dep]


def _sigmoid(x):
    return jax.nn.sigmoid(x)


def _dsilu(x, s):
    return s * (1.0 + x * (1.0 - s))


def _ln_stats(r):
    mu = jnp.mean(r, axis=-1, keepdims=True)
    xc = r - mu
    var = jnp.mean(xc * xc, axis=-1, keepdims=True)
    rstd = lax.rsqrt(var + LN_EPS)
    return xc * rstd, rstd


def _ln_bwd(dy, xhat, rstd, g):
    dyg = dy * g
    m1 = jnp.mean(dyg, axis=-1, keepdims=True)
    m2 = jnp.mean(dyg * xhat, axis=-1, keepdims=True)
    return rstd * (dyg - m1 - xhat * m2)


def _ffn_in_fwd(xb, w_t, name):
    t_, d_ = xb.shape
    f_ = w_t.shape[0] // 2
    tm, tn = _tile(t_, 512, 8), _tile(f_, 512)
    nj = f_ // tn

    def body(x_ref, wg_ref, wu_ref, g_ref, u_ref, a_ref):
        x = x_ref[...]
        g = _dot(x, wg_ref[...], NT)
        u = _dot(x, wu_ref[...], NT)
        g_ref[...] = g
        u_ref[...] = u
        a_ref[...] = (g * _sigmoid(g) * u).astype(BF16)

    return pl.pallas_call(
        body, name=name, grid=(t_ // tm, nj),
        in_specs=[pl.BlockSpec((tm, d_), lambda i, j: (i, 0)),
                  pl.BlockSpec((tn, d_), lambda i, j: (j, 0)),
                  pl.BlockSpec((tn, d_), lambda i, j: (j + nj, 0))],
        out_specs=[pl.BlockSpec((tm, tn), lambda i, j: (i, j))] * 3,
        out_shape=[jax.ShapeDtypeStruct((t_, f_), F32), jax.ShapeDtypeStruct((t_, f_), F32),
                   jax.ShapeDtypeStruct((t_, f_), BF16)],
        compiler_params=_params(),
    )(xb, w_t, w_t)


def _mm_res_ln_fwd(a, w, res, g, b, scale, name):
    t_, k_ = a.shape
    d_ = w.shape[1]
    tm, tk = _tile(t_, 512, 8), _tile(k_, 512)
    nk = k_ // tk

    def body(a_ref, w_ref, res_ref, g_ref, b_ref, r_ref, h_ref, hb_ref, acc):
        k = pl.program_id(1)

        @pl.when(k == 0)
        def _():
            acc[...] = jnp.zeros_like(acc)

        acc[...] += _dot(a_ref[...], w_ref[...], NN)

        @pl.when(k == nk - 1)
        def _():
            r = ALPHA * res_ref[...] + scale * acc[...]
            xhat, _ = _ln_stats(r)
            h = xhat * g_ref[...] + b_ref[...]
            r_ref[...] = r
            h_ref[...] = h
            hb_ref[...] = h.astype(BF16)

    row = pl.BlockSpec((tm, d_), lambda i, k: (i, 0))
    vec = pl.BlockSpec((1, d_), lambda i, k: (0, 0))
    return pl.pallas_call(
        body, name=name, grid=(t_ // tm, nk),
        in_specs=[pl.BlockSpec((tm, tk), lambda i, k: (i, k)),
                  pl.BlockSpec((tk, d_), lambda i, k: (k, 0)), row, vec, vec],
        out_specs=[row, row, row],
        out_shape=[jax.ShapeDtypeStruct((t_, d_), F32), jax.ShapeDtypeStruct((t_, d_), F32),
                   jax.ShapeDtypeStruct((t_, d_), BF16)],
        scratch_shapes=[pltpu.VMEM((tm, d_), F32)],
        compiler_params=_params(),
    )(a, w, res, g, b)


def _mm_nt(a, w_t, out_dtype, name, dep=None):
    t_, k_ = a.shape
    n_ = w_t.shape[0]
    tm, tn = _tile(t_, 512, 8), _tile(n_, 512)

    def body(a_ref, w_ref, o_ref):
        o_ref[...] = _dot(a_ref[...], w_ref[...], NT).astype(out_dtype)

    body, dep_specs, deps = _after(body, 2, dep)
    return pl.pallas_call(
        body, name=name, grid=(t_ // tm, n_ // tn),
        in_specs=[pl.BlockSpec((tm, k_), lambda i, j: (i, 0)),
                  pl.BlockSpec((tn, k_), lambda i, j: (j, 0)), *dep_specs],
        out_specs=pl.BlockSpec((tm, tn), lambda i, j: (i, j)),
        out_shape=jax.ShapeDtypeStruct((t_, n_), out_dtype),
        compiler_params=_params(),
    )(a, w_t, *deps)


def _mm_nn(a, w, out_dtype, name, dep=None):
    t_, k_ = a.shape
    n_ = w.shape[1]
    tm, tn = _tile(t_, 512, 8), _tile(n_, 512)

    def body(a_ref, w_ref, o_ref):
        o_ref[...] = _dot(a_ref[...], w_ref[...], NN).astype(out_dtype)

    body, dep_specs, deps = _after(body, 2, dep)
    return pl.pallas_call(
        body, name=name, grid=(t_ // tm, n_ // tn),
        in_specs=[pl.BlockSpec((tm, k_), lambda i, j: (i, 0)),
                  pl.BlockSpec((k_, tn), lambda i, j: (0, j)), *dep_specs],
        out_specs=pl.BlockSpec((tm, tn), lambda i, j: (i, j)),
        out_shape=jax.ShapeDtypeStruct((t_, n_), out_dtype),
        compiler_params=_params(),
    )(a, w, *deps)


def _mm_tn(a, b, scale, name):
    t_, m_ = a.shape
    n_ = b.shape[1]
    tm, tn, tk = _tile(m_, 512), _tile(n_, 2048), _tile(t_, 512, 8)
    nk = t_ // tk

    def body(a_ref, b_ref, o_ref, acc):
        k = pl.program_id(2)

        @pl.when(k == 0)
        def _():
            acc[...] = jnp.zeros_like(acc)

        acc[...] += _dot(a_ref[...], b_ref[...], TN)

        @pl.when(k == nk - 1)
        def _():
            o_ref[...] = (scale * acc[...]).astype(BF16)

    return pl.pallas_call(
        body, name=name, grid=(m_ // tm, n_ // tn, nk),
        in_specs=[pl.BlockSpec((tk, tm), lambda i, j, k: (k, i)),
                  pl.BlockSpec((tk, tn), lambda i, j, k: (k, j))],
        out_specs=pl.BlockSpec((tm, tn), lambda i, j, k: (i, j)),
        out_shape=jax.ShapeDtypeStruct((m_, n_), BF16),
        scratch_shapes=[pltpu.VMEM((tm, tn), F32)],
        compiler_params=_params(),
    )(a, b)


def _gate_out_fwd(ya_in, ob, wa, wb_t, proj, d_, name):
    t_ = ya_in.shape[0]
    goff = 5 * d_ + QKV_WIDTH
    tm, tn = _tile(t_, 512, 8), _tile_multi([d_, goff], 512)
    ja, jb = goff // tn, (goff + d_) // tn

    def body(ya_ref, ob_ref, wa_ref, wb_ref, ga_ref, gb_ref, yao_ref, ybo_ref, z_ref):
        y_a = _dot(ya_ref[...], wa_ref[...], NN)
        y_b = _dot(ob_ref[...], wb_ref[...], NT)
        yao_ref[...] = y_a
        ybo_ref[...] = y_b
        z_ref[...] = (_sigmoid(ga_ref[...]) * y_a + _sigmoid(gb_ref[...]) * y_b).astype(BF16)

    tile = pl.BlockSpec((tm, tn), lambda i, j: (i, j))
    return pl.pallas_call(
        body, name=name, grid=(t_ // tm, d_ // tn),
        in_specs=[pl.BlockSpec((tm, d_), lambda i, j: (i, 0)),
                  pl.BlockSpec((tm, ATTN_OUT), lambda i, j: (i, 0)),
                  pl.BlockSpec((d_, tn), lambda i, j: (0, j)),
                  pl.BlockSpec((tn, ATTN_OUT), lambda i, j: (j, 0)),
                  pl.BlockSpec((tm, tn), lambda i, j: (i, ja + j)),
                  pl.BlockSpec((tm, tn), lambda i, j: (i, jb + j))],
        out_specs=[tile, tile, tile],
        out_shape=[jax.ShapeDtypeStruct((t_, d_), F32), jax.ShapeDtypeStruct((t_, d_), F32),
                   jax.ShapeDtypeStruct((t_, d_), BF16)],
        compiler_params=_params(),
    )(ya_in, ob, wa, wb_t, proj, proj)


def _ffn_mid_bwd(drb, w_out, gate, up, scale, name, dep=None):
    t_, d_ = drb.shape
    f_ = w_out.shape[0]
    tm, tn = _tile(t_, 512, 8), _tile(f_, 512)

    def body(dr_ref, w_ref, g_ref, u_ref, dg_ref, du_ref):
        da = scale * _dot(dr_ref[...], w_ref[...], NT)
        g = g_ref[...]
        s = _sigmoid(g)
        dg_ref[...] = (da * u_ref[...] * _dsilu(g, s)).astype(BF16)
        du_ref[...] = (da * g * s).astype(BF16)

    body, dep_specs, deps = _after(body, 4, dep)
    tile = pl.BlockSpec((tm, tn), lambda i, j: (i, j))
    return pl.pallas_call(
        body, name=name, grid=(t_ // tm, f_ // tn),
        in_specs=[pl.BlockSpec((tm, d_), lambda i, j: (i, 0)),
                  pl.BlockSpec((tn, d_), lambda i, j: (j, 0)), tile, tile, *dep_specs],
        out_specs=[tile, tile],
        out_shape=[jax.ShapeDtypeStruct((t_, f_), BF16)] * 2,
        compiler_params=_params(),
    )(drb, w_out, gate, up, *deps)


def _mm_nn_res_lnbwd(a, w, dres, r, g, name, dep=None):
    t_, k_ = a.shape
    d_ = w.shape[1]
    tm, tk = _tile(t_, 512, 8), _tile(k_, 512)
    nk = k_ // tk

    def body(a_ref, w_ref, dres_ref, r_ref, g_ref, dr_ref, drb_ref, dg_ref, db_ref, acc):
        i, k = pl.program_id(0), pl.program_id(1)

        @pl.when(k == 0)
        def _():
            acc[...] = jnp.zeros_like(acc)

        @pl.when((i == 0) & (k == 0))
        def _():
            dg_ref[...] = jnp.zeros_like(dg_ref)
            db_ref[...] = jnp.zeros_like(db_ref)

        acc[...] += _dot(a_ref[...], w_ref[...], NN)

        @pl.when(k == nk - 1)
        def _():
            dy = acc[...] + ALPHA * dres_ref[...]
            xhat, rstd = _ln_stats(r_ref[...])
            dr = _ln_bwd(dy, xhat, rstd, g_ref[...])
            dr_ref[...] = dr
            drb_ref[...] = dr.astype(BF16)
            dg_ref[...] += jnp.sum(dy * xhat, axis=0, keepdims=True)
            db_ref[...] += jnp.sum(dy, axis=0, keepdims=True)

    body, dep_specs, deps = _after(body, 5, dep)
    row = pl.BlockSpec((tm, d_), lambda i, k: (i, 0))
    vec = pl.BlockSpec((1, d_), lambda i, k: (0, 0))
    return pl.pallas_call(
        body, name=name, grid=(t_ // tm, nk),
        in_specs=[pl.BlockSpec((tm, tk), lambda i, k: (i, k)),
                  pl.BlockSpec((tk, d_), lambda i, k: (k, 0)), row, row, vec, *dep_specs],
        out_specs=[row, row, vec, vec],
        out_shape=[jax.ShapeDtypeStruct((t_, d_), F32), jax.ShapeDtypeStruct((t_, d_), BF16),
                   jax.ShapeDtypeStruct((1, d_), F32), jax.ShapeDtypeStruct((1, d_), F32)],
        scratch_shapes=[pltpu.VMEM((tm, d_), F32)],
        compiler_params=_params(),
    )(a, w, dres, r, g, *deps)


def _mm_nn_res(a, w, dres, name, dep=None):
    t_, k_ = a.shape
    d_ = w.shape[1]
    tm, tk = _tile(t_, 512, 8), _tile(k_, 512)
    nk = k_ // tk

    def body(a_ref, w_ref, dres_ref, o_ref, acc):
        k = pl.program_id(1)

        @pl.when(k == 0)
        def _():
            acc[...] = jnp.zeros_like(acc)

        acc[...] += _dot(a_ref[...], w_ref[...], NN)

        @pl.when(k == nk - 1)
        def _():
            o_ref[...] = acc[...] + ALPHA * dres_ref[...]

    body, dep_specs, deps = _after(body, 3, dep)
    row = pl.BlockSpec((tm, d_), lambda i, k: (i, 0))
    return pl.pallas_call(
        body, name=name, grid=(t_ // tm, nk),
        in_specs=[pl.BlockSpec((tm, tk), lambda i, k: (i, k)),
                  pl.BlockSpec((tk, d_), lambda i, k: (k, 0)), row, *dep_specs],
        out_specs=row,
        out_shape=jax.ShapeDtypeStruct((t_, d_), F32),
        scratch_shapes=[pltpu.VMEM((tm, d_), F32)],
        compiler_params=_params(),
    )(a, w, dres, *deps)


def _dz_gate_bwd(drb, w_out, proj, ya, yb, d_, name, dep=None):
    t_ = drb.shape[0]
    goff = 5 * d_ + QKV_WIDTH
    tm, tn = _tile(t_, 512, 8), _tile_multi([d_, goff], 512)
    ja, jb = goff // tn, (goff + d_) // tn

    def body(dr_ref, w_ref, ga_ref, gb_ref, ya_ref, yb_ref, dya_ref, dyb_ref, dga_ref, dgb_ref):
        dz = _dot(dr_ref[...], w_ref[...], NT)
        sa, sb = _sigmoid(ga_ref[...]), _sigmoid(gb_ref[...])
        dya_ref[...] = (dz * sa).astype(BF16)
        dyb_ref[...] = (dz * sb).astype(BF16)
        dga_ref[...] = (dz * ya_ref[...] * sa * (1.0 - sa)).astype(BF16)
        dgb_ref[...] = (dz * yb_ref[...] * sb * (1.0 - sb)).astype(BF16)

    body, dep_specs, deps = _after(body, 6, dep)
    tile = pl.BlockSpec((tm, tn), lambda i, j: (i, j))
    return pl.pallas_call(
        body, name=name, grid=(t_ // tm, d_ // tn),
        in_specs=[pl.BlockSpec((tm, d_), lambda i, j: (i, 0)),
                  pl.BlockSpec((tn, d_), lambda i, j: (j, 0)),
                  pl.BlockSpec((tm, tn), lambda i, j: (i, ja + j)),
                  pl.BlockSpec((tm, tn), lambda i, j: (i, jb + j)), tile, tile, *dep_specs],
        out_specs=[tile] * 4,
        out_shape=[jax.ShapeDtypeStruct((t_, d_), BF16)] * 4,
        compiler_params=_params(),
    )(drb, w_out, proj, proj, ya, yb, *deps)


def _ln_loss_bwd(r, target, g, b, name):
    t_, d_ = r.shape
    tm = _tile(t_, 256, 8)

    def body(r_ref, t_ref, g_ref, b_ref, dr_ref, drb_ref, dg_ref, db_ref, loss_ref):
        i = pl.program_id(0)

        @pl.when(i == 0)
        def _():
            dg_ref[...] = jnp.zeros_like(dg_ref)
            db_ref[...] = jnp.zeros_like(db_ref)
            loss_ref[...] = jnp.zeros_like(loss_ref)

        xhat, rstd = _ln_stats(r_ref[...])
        gain = g_ref[...]
        err = xhat * gain + b_ref[...] - t_ref[...]
        loss_ref[...] += (0.5 / d_) * jnp.sum(err * err)
        dy = err * (1.0 / d_)
        dr = _ln_bwd(dy, xhat, rstd, gain)
        dr_ref[...] = dr
        drb_ref[...] = dr.astype(BF16)
        dg_ref[...] += jnp.sum(dy * xhat, axis=0, keepdims=True)
        db_ref[...] += jnp.sum(dy, axis=0, keepdims=True)

    row = pl.BlockSpec((tm, d_), lambda i: (i, 0))
    vec = pl.BlockSpec((1, d_), lambda i: (0, 0))
    return pl.pallas_call(
        body, name=name, grid=(t_ // tm,),
        in_specs=[row, row, vec, vec],
        out_specs=[row, row, vec, vec, pl.BlockSpec((1, HEAD), lambda i: (0, 0))],
        out_shape=[jax.ShapeDtypeStruct((t_, d_), F32), jax.ShapeDtypeStruct((t_, d_), BF16),
                   jax.ShapeDtypeStruct((1, d_), F32), jax.ShapeDtypeStruct((1, d_), F32),
                   jax.ShapeDtypeStruct((1, HEAD), F32)],
        compiler_params=_params(),
    )(r, target, g, b)


def _chunk_scan(x, row, reverse, size):
    s = 1
    while s < CHUNK:
        if reverse:
            x = x + jnp.where(row < CHUNK - s, pltpu.roll(x, size - s, 0), 0.0)
        else:
            x = x + jnp.where(row >= s, pltpu.roll(x, s, 0), 0.0)
        s *= 2
    return x


def _lower_bound(tab):
    return _sigmoid(tab[0:1, :] - tab[1:2, :])


def _tri_mask(reverse):
    r = lax.broadcasted_iota(jnp.int32, (CHUNK, CHUNK), 0)
    c = lax.broadcasted_iota(jnp.int32, (CHUNK, CHUNK), 1)
    return (c >= r) if reverse else (r >= c)


def _hgrn_fwd(proj, lbf, lbb, ng, b_, s_, d_, name):
    h_ = d_ // HEAD
    nc = s_ // CHUNK

    def body(hq_ref, hff_ref, hfb_ref, hi_ref, hog_ref, lbf_ref, lbb_ref, ng_ref, ya_ref, o_ref,
             q_s, k_s, cum_s, o_s):
        row = lax.broadcasted_iota(jnp.int32, (s_, HEAD), 0) % CHUNK
        hq = hq_ref[...]
        q_s[...] = hq * _sigmoid(hq)
        o_s[...] = jnp.zeros_like(o_s)
        for reverse, hf_ref, lb_ref in ((False, hff_ref, lbf_ref), (True, hfb_ref, lbb_ref)):
            lb = _lower_bound(lb_ref[...])
            f = lb + (1.0 - lb) * _sigmoid(hf_ref[...])
            k_s[...] = 1.0 - f
            cum_s[...] = _chunk_scan(jnp.log(f), row, reverse, s_)
            mask = _tri_mask(reverse)

            def step(n, st, reverse=reverse, mask=mask):
                idx = (nc - 1 - n) if reverse else n
                sl = pl.ds(pl.multiple_of(idx * CHUNK, CHUNK), CHUNK)
                cm = cum_s[sl, :]
                tot = cm[0:1, :] if reverse else cm[CHUNK - 1:CHUNK, :]
                qc, kc = q_s[sl, :], k_s[sl, :]
                vb = hi_ref[sl, :].astype(BF16)
                qd = (qc * jnp.exp(cm)).astype(BF16)
                kd = (kc * jnp.exp(-cm)).astype(BF16)
                ke = (kc * jnp.exp(tot - cm)).astype(BF16)
                a = jnp.where(mask, _dot(qd, kd, NT), 0.0)
                o_s[sl, :] += _dot(a.astype(BF16), vb, NN) + _dot(qd, st.astype(BF16), NT)
                return st * jnp.exp(tot) + _dot(vb, ke, TN)

            lax.fori_loop(0, nc, step, jnp.zeros((HEAD, HEAD), F32))
        o = o_s[...]
        o_ref[...] = o
        nrm = o * lax.rsqrt(jnp.mean(o * o, axis=-1, keepdims=True) + LN_EPS)
        hog = hog_ref[...]
        ya_ref[...] = (nrm * ng_ref[...] * hog * _sigmoid(hog)).astype(BF16)

    def col(part):
        return pl.BlockSpec((s_, HEAD), lambda h, b, part=part: (b, part * h_ + h))

    tab = pl.BlockSpec((2, HEAD), lambda h, b: (0, h))
    out = pl.BlockSpec((s_, HEAD), lambda h, b: (b, h))
    return pl.pallas_call(
        body, name=name, grid=(h_, b_),
        in_specs=[col(0), col(1), col(2), col(3), col(4), tab, tab,
                  pl.BlockSpec((1, HEAD), lambda h, b: (0, h))],
        out_specs=[out, out],
        out_shape=[jax.ShapeDtypeStruct((b_ * s_, d_), BF16), jax.ShapeDtypeStruct((b_ * s_, d_), F32)],
        scratch_shapes=[pltpu.VMEM((s_, HEAD), F32)] * 4,
        compiler_params=_params(),
    )(proj, proj, proj, proj, proj, lbf, lbb, ng)


def _hgrn_bwd(proj, lbf, lbb, ng, o_sum, dya, b_, s_, d_, name):
    h_ = d_ // HEAD
    nc = s_ // CHUNK

    def body(hq_ref, hff_ref, hfb_ref, hi_ref, hog_ref, lbf_ref, lbb_ref, ng_ref, o_ref, dya_ref,
             dhq_ref, dhff_ref, dhfb_ref, dhi_ref, dhog_ref, dng_ref, dlbf_ref, dlbb_ref,
             q_s, k_s, cum_s, do_s, dq_s, dv_s, db_s, dk_s, st_s):
        b = pl.program_id(1)

        @pl.when(b == 0)
        def _():
            dng_ref[...] = jnp.zeros_like(dng_ref)
            dlbf_ref[...] = jnp.zeros_like(dlbf_ref)
            dlbb_ref[...] = jnp.zeros_like(dlbb_ref)

        row = lax.broadcasted_iota(jnp.int32, (s_, HEAD), 0) % CHUNK
        crow = lax.broadcasted_iota(jnp.int32, (CHUNK, HEAD), 0)
        hq = hq_ref[...]
        sq = _sigmoid(hq)
        q_s[...] = hq * sq
        o = o_ref[...]
        rinv = lax.rsqrt(jnp.mean(o * o, axis=-1, keepdims=True) + LN_EPS)
        nrm = o * rinv
        hog = hog_ref[...]
        so = _sigmoid(hog)
        gain = ng_ref[...]
        dy = dya_ref[...]
        dhog_ref[...] = (dy * nrm * gain * _dsilu(hog, so)).astype(BF16)
        dng_ref[...] += jnp.sum(dy * nrm * hog * so, axis=0, keepdims=True)
        dn = dy * gain * hog * so
        do_s[...] = rinv * (dn - nrm * jnp.mean(dn * nrm, axis=-1, keepdims=True))
        dq_s[...] = jnp.zeros_like(dq_s)
        dv_s[...] = jnp.zeros_like(dv_s)

        for reverse, hf_ref, lb_ref, dhf_ref, dlb_ref in (
                (False, hff_ref, lbf_ref, dhff_ref, dlbf_ref), (True, hfb_ref, lbb_ref, dhfb_ref, dlbb_ref)):
            tab = lb_ref[...]
            lb = _lower_bound(tab)
            sf = _sigmoid(hf_ref[...])
            f = lb + (1.0 - lb) * sf
            k_s[...] = 1.0 - f
            cum_s[...] = _chunk_scan(jnp.log(f), row, reverse, s_)
            mask = _tri_mask(reverse)
            last = 0 if reverse else CHUNK - 1

            def chunk(idx, reverse=reverse):
                sl = pl.ds(pl.multiple_of(idx * CHUNK, CHUNK), CHUNK)
                cm = cum_s[sl, :]
                tot = cm[0:1, :] if reverse else cm[CHUNK - 1:CHUNK, :]
                return sl, cm, tot

            def fstep(n, st, reverse=reverse, chunk=chunk):
                idx = (nc - 1 - n) if reverse else n
                sl, cm, tot = chunk(idx)
                st_s[idx] = st
                ke = (k_s[sl, :] * jnp.exp(tot - cm)).astype(BF16)
                return st * jnp.exp(tot) + _dot(hi_ref[sl, :].astype(BF16), ke, TN)

            lax.fori_loop(0, nc, fstep, jnp.zeros((HEAD, HEAD), F32))

            def bstep(n, dst, reverse=reverse, chunk=chunk, mask=mask, last=last):
                idx = n if reverse else (nc - 1 - n)
                sl, cm, tot = chunk(idx)
                eb, enb, ee, dec = jnp.exp(cm), jnp.exp(-cm), jnp.exp(tot - cm), jnp.exp(tot)
                qc, kc = q_s[sl, :], k_s[sl, :]
                qd, kd, ke = qc * eb, kc * enb, kc * ee
                qdb, kdb, keb = qd.astype(BF16), kd.astype(BF16), ke.astype(BF16)
                vb = hi_ref[sl, :].astype(BF16)
                dob = do_s[sl, :].astype(BF16)
                st0 = st_s[idx]
                dstb = dst.astype(BF16)
                a = jnp.where(mask, _dot(qdb, kdb, NT), 0.0).astype(BF16)
                da = jnp.where(mask, _dot(dob, vb, NT), 0.0).astype(BF16)
                dqd = _dot(da, kdb, NN) + _dot(dob, st0.astype(BF16), NN)
                dkd = _dot(da, qdb, TN)
                dv = _dot(a, dob, TN) + _dot(keb, dstb, NT)
                dke = _dot(vb, dstb, NN)
                ddec = jnp.sum(dst * st0, axis=0, keepdims=True)
                dtot = jnp.sum(dke * ke, axis=0, keepdims=True) + ddec * dec
                db = dqd * qd - dkd * kd - dke * ke
                db_s[sl, :] = db + jnp.where(crow == last, dtot, 0.0)
                dk_s[sl, :] = dkd * enb + dke * ee
                dq_s[sl, :] += dqd * eb
                dv_s[sl, :] += dv
                return dst * dec + _dot(dob, qdb, TN)

            lax.fori_loop(0, nc, bstep, jnp.zeros((HEAD, HEAD), F32))
            dlogf = _chunk_scan(db_s[...], row, not reverse, s_)
            df = dlogf / f - dk_s[...]
            dhf_ref[...] = (df * (1.0 - lb) * sf * (1.0 - sf)).astype(BF16)
            dlb = jnp.sum(df * (1.0 - sf), axis=0, keepdims=True) * lb * (1.0 - lb)
            dlb_ref[0:1, :] += dlb
            dlb_ref[1:2, :] -= dlb

        dhq_ref[...] = (dq_s[...] * _dsilu(hq, sq)).astype(BF16)
        dhi_ref[...] = dv_s[...].astype(BF16)

    def col(part):
        return pl.BlockSpec((s_, HEAD), lambda h, b, part=part: (b, part * h_ + h))

    tab = pl.BlockSpec((2, HEAD), lambda h, b: (0, h))
    vec = pl.BlockSpec((1, HEAD), lambda h, b: (0, h))
    blk = pl.BlockSpec((s_, HEAD), lambda h, b: (b, h))
    act = jax.ShapeDtypeStruct((b_ * s_, d_), BF16)
    return pl.pallas_call(
        body, name=name, grid=(h_, b_),
        in_specs=[col(0), col(1), col(2), col(3), col(4), tab, tab, vec, blk, blk],
        out_specs=[blk] * 5 + [vec, tab, tab],
        out_shape=[act] * 5 + [jax.ShapeDtypeStruct((1, d_), F32), jax.ShapeDtypeStruct((2, d_), F32),
                               jax.ShapeDtypeStruct((2, d_), F32)],
        scratch_shapes=[pltpu.VMEM((s_, HEAD), F32)] * 8 + [pltpu.VMEM((nc, HEAD, HEAD), F32)],
        compiler_params=_params(),
    )(proj, proj, proj, proj, proj, lbf, lbb, ng, o_sum, dya)


def _rope_tables(s_):
    half = ROPE_DIM // 2
    inv_freq = ROPE_THETA ** (-jnp.arange(0, ROPE_DIM, 2, dtype=F32) / ROPE_DIM)
    ang = jnp.arange(s_, dtype=F32)[:, None] * inv_freq
    cos, sin = jnp.cos(ang), jnp.sin(ang)
    zeros = jnp.zeros((s_, HEAD - ROPE_DIM), F32)
    zh = jnp.zeros((s_, half), F32)
    c = jnp.concatenate([cos, cos, jnp.ones((s_, HEAD - ROPE_DIM), F32)], axis=1)
    s1 = jnp.concatenate([-sin, zh, zeros], axis=1)
    s2 = jnp.concatenate([zh, sin, zeros], axis=1)
    return c, s1, s2


def _rope(t, c, s1, s2):
    half = ROPE_DIM // 2
    return t * c + pltpu.roll(t, HEAD - half, 1) * s1 + pltpu.roll(t, half, 1) * s2


def _rope_bwd(dt, c, s1, s2):
    half = ROPE_DIM // 2
    return dt * c + pltpu.roll(dt * s1, half, 1) + pltpu.roll(dt * s2, HEAD - half, 1)


def _window_mask(r0, qb, wk, seg):
    row = lax.broadcasted_iota(jnp.int32, (qb, wk), 0)
    col = lax.broadcasted_iota(jnp.int32, (qb, wk), 1)
    kj = r0 - ATTN_HALF + col
    return (col - row >= 0) & (col - row <= 2 * ATTN_HALF) & (kj >= 0) & (kj < seg)


def _attn_fwd(qkv, tabs, b_, s_, dil, name):
    seg = s_ // dil
    qb = min(128, seg)
    nq, wk = seg // qb, qb + 2 * ATTN_HALF
    scale = HEAD ** -0.5
    ncol = QKV_GROUP // HEAD

    def body(q_ref, k_ref, v_ref, c_ref, s1_ref, s2_ref, o_ref, lse_ref, q_s, k_s, v_s):
        c, s1, s2 = c_ref[...], s1_ref[...], s2_ref[...]
        q_s[...] = _rope(q_ref[...], c, s1, s2).astype(BF16)
        k_s[...] = jnp.zeros_like(k_s)
        v_s[...] = jnp.zeros_like(v_s)
        k_s[ATTN_HALF:ATTN_HALF + seg, :] = _rope(k_ref[...], c, s1, s2).astype(BF16)
        v_s[ATTN_HALF:ATTN_HALF + seg, :] = v_ref[...].astype(BF16)

        def step(i, carry):
            r0 = pl.multiple_of(i * qb, qb)
            sc = _dot(q_s[pl.ds(r0, qb), :], k_s[pl.ds(r0, wk), :], NT) * scale
            sc = jnp.where(_window_mask(r0, qb, wk, seg), sc, NEG_INF)
            m = jnp.max(sc, axis=-1, keepdims=True)
            p = jnp.exp(sc - m)
            den = jnp.sum(p, axis=-1, keepdims=True)
            o_ref[pl.ds(r0, qb), :] = _dot(p.astype(BF16), v_s[pl.ds(r0, wk), :], NN) / den
            lse_ref[pl.ds(r0, qb), :] = jnp.broadcast_to(m + jnp.log(den), (qb, HEAD))
            return carry

        lax.fori_loop(0, nq, step, 0)

    def col(part):
        return pl.BlockSpec((seg, HEAD), lambda b, r, h, part=part: (b, r * ncol + part * ATTN_HEADS + h))

    tab = pl.BlockSpec((seg, HEAD), lambda b, r, h: (0, r))
    out = pl.BlockSpec((seg, HEAD), lambda b, r, h: (b, r * ATTN_HEADS + h))
    shape = jax.ShapeDtypeStruct((b_ * seg, dil * ATTN_OUT), F32)
    return pl.pallas_call(
        body, name=name, grid=(b_, dil, ATTN_HEADS),
        in_specs=[col(0), col(1), col(2), tab, tab, tab],
        out_specs=[out, out],
        out_shape=[shape, shape],
        scratch_shapes=[pltpu.VMEM((seg, HEAD), BF16), pltpu.VMEM((seg + 2 * ATTN_HALF, HEAD), BF16),
                        pltpu.VMEM((seg + 2 * ATTN_HALF, HEAD), BF16)],
        compiler_params=_params(),
    )(qkv, qkv, qkv, *tabs)


def _attn_bwd(qkv, tabs, dog, cg, lse, b_, s_, dil, name):
    seg = s_ // dil
    qb = min(128, seg)
    nq, wk = seg // qb, qb + 2 * ATTN_HALF
    scale = HEAD ** -0.5
    ncol = QKV_GROUP // HEAD

    def body(q_ref, k_ref, v_ref, c_ref, s1_ref, s2_ref, do_ref, cg_ref, lse_ref, dq_ref, dk_ref, dv_ref,
             q_s, k_s, v_s, dk_s, dv_s):
        c, s1, s2 = c_ref[...], s1_ref[...], s2_ref[...]
        q_s[...] = _rope(q_ref[...], c, s1, s2).astype(BF16)
        k_s[...] = jnp.zeros_like(k_s)
        v_s[...] = jnp.zeros_like(v_s)
        k_s[ATTN_HALF:ATTN_HALF + seg, :] = _rope(k_ref[...], c, s1, s2).astype(BF16)
        v_s[ATTN_HALF:ATTN_HALF + seg, :] = v_ref[...].astype(BF16)
        dk_s[...] = jnp.zeros_like(dk_s)
        dv_s[...] = jnp.zeros_like(dv_s)

        def step(i, carry):
            r0 = pl.multiple_of(i * qb, qb)
            rows, win = pl.ds(r0, qb), pl.ds(r0, wk)
            qc, kw, vw = q_s[rows, :], k_s[win, :], v_s[win, :]
            sc = _dot(qc, kw, NT) * scale
            p = jnp.where(_window_mask(r0, qb, wk, seg), jnp.exp(sc - lse_ref[rows, 0:1]), 0.0)
            dob = do_ref[rows, :].astype(BF16)
            dp = _dot(dob, vw, NT)
            ds = (p * (dp + cg_ref[rows, 0:1]) * scale).astype(BF16)
            dq = _dot(ds, kw, NN)
            dq_ref[rows, :] = _rope_bwd(dq, c_ref[rows, :], s1_ref[rows, :], s2_ref[rows, :])
            dk_s[win, :] += _dot(ds, qc, TN)
            dv_s[win, :] += _dot(p.astype(BF16), dob, TN)
            return carry

        lax.fori_loop(0, nq, step, 0)
        dk_ref[...] = _rope_bwd(dk_s[ATTN_HALF:ATTN_HALF + seg, :], c, s1, s2)
        dv_ref[...] = dv_s[ATTN_HALF:ATTN_HALF + seg, :]

    def col(part):
        return pl.BlockSpec((seg, HEAD), lambda b, r, h, part=part: (b, r * ncol + part * ATTN_HEADS + h))

    tab = pl.BlockSpec((seg, HEAD), lambda b, r, h: (0, r))
    out = pl.BlockSpec((seg, HEAD), lambda b, r, h: (b, r * ATTN_HEADS + h))
    shape = jax.ShapeDtypeStruct((b_ * seg, dil * ATTN_OUT), F32)
    return pl.pallas_call(
        body, name=name, grid=(b_, dil, ATTN_HEADS),
        in_specs=[col(0), col(1), col(2), tab, tab, tab, out, out, out],
        out_specs=[out, out, out],
        out_shape=[shape, shape, shape],
        scratch_shapes=[pltpu.VMEM((seg, HEAD), BF16), pltpu.VMEM((seg + 2 * ATTN_HALF, HEAD), BF16),
                        pltpu.VMEM((seg + 2 * ATTN_HALF, HEAD), BF16),
                        pltpu.VMEM((seg + 2 * ATTN_HALF, HEAD), F32), pltpu.VMEM((seg + 2 * ATTN_HALF, HEAD), F32)],
        compiler_params=_params(),
    )(qkv, qkv, qkv, *tabs, dog, cg, lse)


def _group_weights(lses):
    m = jnp.maximum(jnp.maximum(lses[0], lses[1]), lses[2])
    es = [jnp.exp(l - m) for l in lses]
    den = es[0] + es[1] + es[2]
    return [e / den for e in es]


def _combine_fwd(outs, lses, name):
    t_, w_ = outs[0].shape
    tm = _tile(t_, 512, 8)
    ng = len(outs)

    def body(*refs):
        ws = _group_weights([r[...] for r in refs[ng:2 * ng]])
        acc = ws[0] * refs[0][...]
        for g in range(1, ng):
            acc = acc + ws[g] * refs[g][...]
        refs[2 * ng][...] = acc.astype(BF16)

    row = pl.BlockSpec((tm, w_), lambda i: (i, 0))
    return pl.pallas_call(
        body, name=name, grid=(t_ // tm,), in_specs=[row] * (2 * ng), out_specs=row,
        out_shape=jax.ShapeDtypeStruct((t_, w_), BF16), compiler_params=_params(),
    )(*outs, *lses)


def _combine_bwd(dob, outs, lses, name):
    t_, w_ = outs[0].shape
    tm = _tile(t_, 512, 8)
    ng = len(outs)

    def body(*refs):
        do = refs[0][...]
        os_ = [r[...] for r in refs[1:1 + ng]]
        ws = _group_weights([r[...] for r in refs[1 + ng:1 + 2 * ng]])
        o = ws[0] * os_[0]
        for g in range(1, ng):
            o = o + ws[g] * os_[g]
        prod = do * o
        heads = [jnp.broadcast_to(jnp.sum(prod[:, h * HEAD:(h + 1) * HEAD], axis=-1, keepdims=True), (tm, HEAD))
                 for h in range(w_ // HEAD)]
        tot = jnp.concatenate(heads, axis=1)
        for g in range(ng):
            refs[1 + 2 * ng + g][...] = ws[g] * do
            refs[1 + 3 * ng + g][...] = -ws[g] * tot

    row = pl.BlockSpec((tm, w_), lambda i: (i, 0))
    shape = jax.ShapeDtypeStruct((t_, w_), F32)
    res = pl.pallas_call(
        body, name=name, grid=(t_ // tm,), in_specs=[row] * (1 + 2 * ng), out_specs=[row] * (2 * ng),
        out_shape=[shape] * (2 * ng), compiler_params=_params(),
    )(dob, *outs, *lses)
    return res[:ng], res[ng:]


def _adam_update(w, g, m, v):
    m = ADAM_B1 * m + (1.0 - ADAM_B1) * g
    v = ADAM_B2 * v + (1.0 - ADAM_B2) * (g * g)
    m_hat = m / (1.0 - ADAM_B1 ** ADAM_STEP)
    v_hat = v / (1.0 - ADAM_B2 ** ADAM_STEP)
    return -ADAM_LR * (m_hat / (jnp.sqrt(v_hat) + ADAM_EPS) + ADAM_WD * w), m, v


def _adam(w, g, m, v, name):
    r_, c_ = w.shape
    tr = _tile(r_, 256, 8)

    def body(w_ref, g_ref, m_ref, v_ref, d_ref, mo_ref, vo_ref):
        d_ref[...], mo_ref[...], vo_ref[...] = _adam_update(w_ref[...], g_ref[...], m_ref[...], v_ref[...])

    blk = pl.BlockSpec((tr, c_), lambda i: (i, 0))
    shape = jax.ShapeDtypeStruct((r_, c_), F32)
    return pl.pallas_call(
        body, name=name, grid=(r_ // tr,), in_specs=[blk] * 4, out_specs=[blk] * 3,
        out_shape=[shape] * 3, compiler_params=_params(),
    )(w, g, m, v)


def _sum_partials(recv, name):
    n_, r_, c_ = recv.shape
    tr = _tile(r_, 128, 16)

    def body(p_ref, o_ref):
        acc = p_ref[0].astype(F32)
        for i in range(1, n_):
            acc = acc + p_ref[i].astype(F32)
        o_ref[...] = acc

    return pl.pallas_call(
        body, name=name, grid=(r_ // tr,),
        in_specs=[pl.BlockSpec((n_, tr, c_), lambda i: (0, i, 0))],
        out_specs=pl.BlockSpec((tr, c_), lambda i: (i, 0)),
        out_shape=jax.ShapeDtypeStruct((r_, c_), F32), compiler_params=_params(),
    )(recv)


def _small_sum_adam(parts, w, m, v, name):
    n_, r_, c_ = parts.shape

    def body(p_ref, w_ref, m_ref, v_ref, g_ref, d_ref, mo_ref, vo_ref):
        g = p_ref[0]
        for i in range(1, n_):
            g = g + p_ref[i]
        g_ref[...] = g
        d_ref[...], mo_ref[...], vo_ref[...] = _adam_update(w_ref[...], g, m_ref[...], v_ref[...])

    shape = jax.ShapeDtypeStruct((r_, c_), F32)
    return pl.pallas_call(body, name=name, out_shape=[shape] * 4, compiler_params=_params())(parts, w, m, v)


def _my_place():
    x, y, c = lax.axis_index("x"), lax.axis_index("y"), lax.axis_index("c")
    return x, y, c


def _peer(x, y, c, d):
    px = 1 - x if d & 4 else x
    py = 1 - y if d & 2 else y
    pc = 1 - c if d & 1 else c
    return (px, py, pc), 4 * px + 2 * py + pc


def _all_gather(shards, name):
    nw = len(shards)

    def body(*refs):
        ins, outs = refs[:nw], refs[nw:2 * nw]
        send_sems, recv_sems, local_sems = refs[2 * nw:]
        x, y, c = _my_place()
        me = 4 * x + 2 * y + c
        copies = []
        for k in range(nw):
            rows = shards[k].shape[0]
            mine = outs[k].at[pl.ds(pl.multiple_of(me * rows, 16), rows), :]
            local = pltpu.make_async_copy(ins[k], mine, local_sems.at[k])
            local.start()
            copies.append(local)
            for d in range(1, N_DEV):
                place, _ = _peer(x, y, c, d)
                remote = pltpu.make_async_remote_copy(
                    src_ref=ins[k], dst_ref=mine, send_sem=send_sems.at[d - 1, k], recv_sem=recv_sems.at[d - 1, k],
                    device_id=place, device_id_type=MESH)
                remote.start()
                copies.append(remote)
        for cp in copies:
            cp.wait()

    hbm = pl.BlockSpec(memory_space=pl.ANY)
    return pl.pallas_call(
        body, name=name, in_specs=[hbm] * nw, out_specs=[hbm] * nw,
        out_shape=[jax.ShapeDtypeStruct((N_DEV * s.shape[0], s.shape[1]), s.dtype) for s in shards],
        scratch_shapes=[pltpu.SemaphoreType.DMA((N_DEV - 1, nw)), pltpu.SemaphoreType.DMA((N_DEV - 1, nw)),
                        pltpu.SemaphoreType.DMA((nw,))],
    )(*shards)


HBM_SPEC = pl.BlockSpec(memory_space=pltpu.HBM)
SEM_SPEC = pl.BlockSpec(memory_space=pltpu.SEMAPHORE)
EFFECT = pltpu.SideEffectType.DATAFLOW_SIDE_EFFECTING


def _in_hbm(a):
    return pltpu.with_memory_space_constraint(a, pltpu.HBM)


def _token_shape():
    return jax.ShapeDtypeStruct((8, HEAD), F32)


def _gather_start(shards, name):
    nw = len(shards)
    lands = [lax.empty((N_DEV * s.shape[0], s.shape[1]), s.dtype) for s in shards]

    def body(*refs):
        ins, lnd = refs[:nw], refs[nw:2 * nw]
        send, recv, own = refs[2 * nw:3 * nw], refs[3 * nw:4 * nw], refs[4 * nw:5 * nw]
        token = refs[7 * nw]
        x, y, c = _my_place()
        me = 4 * x + 2 * y + c
        for k in range(nw):
            rows = shards[k].shape[0]
            mine = lnd[k].at[pl.ds(pl.multiple_of(me * rows, 16), rows), :]
            pltpu.make_async_copy(ins[k], mine, own[k]).start()
            for d in range(1, N_DEV):
                place, _ = _peer(x, y, c, d)
                pltpu.make_async_remote_copy(
                    src_ref=ins[k], dst_ref=mine, send_sem=send[k].at[d - 1], recv_sem=recv[k].at[d - 1],
                    device_id=place, device_id_type=MESH).start()
        token[...] = jnp.zeros_like(token)

    sems = [pltpu.SemaphoreType.DMA((N_DEV - 1,))] * (2 * nw) + [pltpu.SemaphoreType.DMA(())] * nw
    thru = [pltpu.HBM(a.shape, a.dtype) for a in list(shards) + lands]
    res = pl.pallas_call(
        body, name=name, out_shape=(*sems, *thru, _token_shape()),
        in_specs=[HBM_SPEC] * (2 * nw),
        out_specs=(*([SEM_SPEC] * (3 * nw)), *([HBM_SPEC] * (2 * nw)), pl.BlockSpec(memory_space=pltpu.VMEM)),
        input_output_aliases={i: 3 * nw + i for i in range(2 * nw)},
        compiler_params=pltpu.CompilerParams(has_side_effects=EFFECT),
    )(*[_in_hbm(s) for s in shards], *[_in_hbm(l) for l in lands])
    return [dict(send=res[k], recv=res[nw + k], own=res[2 * nw + k], src=res[3 * nw + k], land=res[4 * nw + k])
            for k in range(nw)]


def _gather_wait(pending, after, name):
    rows = pending["src"].shape[0]

    def body(src_ref, land_ref, send, recv, own, after_ref, src_dead, got):
        x, y, c = _my_place()
        me = 4 * x + 2 * y + c
        mine = land_ref.at[pl.ds(pl.multiple_of(me * rows, 16), rows), :]
        pltpu.make_async_copy(src_ref, mine, own).wait()
        for d in range(1, N_DEV):
            place, _ = _peer(x, y, c, d)
            cp = pltpu.make_async_remote_copy(
                src_ref=src_ref, dst_ref=mine, send_sem=send.at[d - 1], recv_sem=recv.at[d - 1],
                device_id=place, device_id_type=MESH)
            cp.wait_send()
            cp.wait_recv()

    src, land = pending["src"], pending["land"]
    return pl.pallas_call(
        body, name=name, out_shape=(pltpu.HBM(src.shape, src.dtype), pltpu.HBM(land.shape, land.dtype)),
        in_specs=(HBM_SPEC, HBM_SPEC, SEM_SPEC, SEM_SPEC, SEM_SPEC, pl.BlockSpec(memory_space=pl.ANY)),
        out_specs=(HBM_SPEC, HBM_SPEC), input_output_aliases={0: 0, 1: 1},
        compiler_params=pltpu.CompilerParams(has_side_effects=EFFECT),
    )(src, land, pending["send"], pending["recv"], pending["own"], after)[1]


def _scatter_start(full, name):
    rows, cols = full.shape[0] // N_DEV, full.shape[1]
    land = lax.empty((N_DEV, rows, cols), full.dtype)

    def body(full_ref, land_ref, send, recv, own, full_thru, land_thru, token):
        x, y, c = _my_place()
        me = 4 * x + 2 * y + c
        slab = land_ref.at[me]
        pltpu.make_async_copy(full_ref.at[pl.ds(pl.multiple_of(me * rows, 16), rows), :], slab, own).start()
        for d in range(1, N_DEV):
            place, num = _peer(x, y, c, d)
            pltpu.make_async_remote_copy(
                src_ref=full_ref.at[pl.ds(pl.multiple_of(num * rows, 16), rows), :], dst_ref=slab,
                send_sem=send.at[d - 1], recv_sem=recv.at[d - 1], device_id=place, device_id_type=MESH).start()
        token[...] = jnp.zeros_like(token)

    res = pl.pallas_call(
        body, name=name,
        out_shape=(pltpu.SemaphoreType.DMA((N_DEV - 1,)), pltpu.SemaphoreType.DMA((N_DEV - 1,)),
                   pltpu.SemaphoreType.DMA(()),
                   pltpu.HBM(full.shape, full.dtype), pltpu.HBM(land.shape, land.dtype), _token_shape()),
        in_specs=(HBM_SPEC, HBM_SPEC),
        out_specs=(SEM_SPEC, SEM_SPEC, SEM_SPEC, HBM_SPEC, HBM_SPEC, pl.BlockSpec(memory_space=pltpu.VMEM)),
        input_output_aliases={0: 3, 1: 4},
        compiler_params=pltpu.CompilerParams(has_side_effects=EFFECT),
    )(_in_hbm(full), _in_hbm(land))
    return dict(send=res[0], recv=res[1], own=res[2], src=res[3], land=res[4]), res[5]


def _scatter_wait(pending, after, name):
    rows = pending["land"].shape[1]

    def body(src_ref, land_ref, send, recv, own, after_ref, src_dead, got):
        x, y, c = _my_place()
        me = 4 * x + 2 * y + c
        pltpu.make_async_copy(src_ref.at[pl.ds(pl.multiple_of(me * rows, 16), rows), :], land_ref.at[me], own).wait()
        for d in range(1, N_DEV):
            place, num = _peer(x, y, c, d)
            cp = pltpu.make_async_remote_copy(
                src_ref=src_ref.at[pl.ds(pl.multiple_of(num * rows, 16), rows), :], dst_ref=land_ref.at[me],
                send_sem=send.at[d - 1], recv_sem=recv.at[d - 1], device_id=place, device_id_type=MESH)
            cp.wait_send()
            cp.wait_recv()

    src, land = pending["src"], pending["land"]
    return pl.pallas_call(
        body, name=name, out_shape=(pltpu.HBM(src.shape, src.dtype), pltpu.HBM(land.shape, land.dtype)),
        in_specs=(HBM_SPEC, HBM_SPEC, SEM_SPEC, SEM_SPEC, SEM_SPEC, pl.BlockSpec(memory_space=pl.ANY)),
        out_specs=(HBM_SPEC, HBM_SPEC), input_output_aliases={0: 0, 1: 1},
        compiler_params=pltpu.CompilerParams(has_side_effects=EFFECT),
    )(src, land, pending["send"], pending["recv"], pending["own"], after)[1]


BIG = ("ffn1_w_in", "ffn1_w_out", "mix_w_in", "w_branch_a", "w_branch_b", "mix_w_out", "ffn2_w_in", "ffn2_w_out")
TRANSPOSED = ("ffn1_w_in", "mix_w_in", "w_branch_b", "ffn2_w_in")
SMALL = ("ln1_g", "ln1_b", "ln2_g", "ln2_b", "ln3_g", "ln3_b", "hgrn_norm_g", "hgrn_lb_fwd", "hgrn_lb_bwd")
SMALL_ROWS = 16


def _local_step(x, target, weight, emit, sp):
    b_, s_, d_ = x.shape
    t_ = b_ * s_
    x2, tgt = x.reshape(t_, d_), target.reshape(t_, d_)
    xb = x2.astype(BF16)
    w1i = weight("ffn1_w_in", xb)
    g1, u1, a1 = _ffn_in_fwd(xb, w1i, "ffn1_in")
    w1o = weight("ffn1_w_out", a1)
    r1, h1, h1b = _mm_res_ln_fwd(a1, w1o, x2, sp["ln1_g"], sp["ln1_b"], 0.5, "ffn1_out_ln1")
    wmx = weight("mix_w_in", h1b)
    proj = _mm_nt(h1b, wmx, F32, "mix_in")
    ya_in, o_sum = _hgrn_fwd(proj, sp["hgrn_lb_fwd"], sp["hgrn_lb_bwd"], sp["hgrn_norm_g"], b_, s_, d_, "hgrn_fwd")
    tabs = _rope_tables(s_)
    qkvs, gtabs, outs, lses = [], [], [], []
    for gi, (_, dil) in enumerate(ATTN_GROUPS):
        off = 5 * d_ + gi * QKV_GROUP
        qkv = proj[:, off:off + QKV_GROUP].reshape(t_ // dil, dil * QKV_GROUP)
        gt = [tb.reshape(s_ // dil, dil * HEAD) for tb in tabs]
        o_g, lse_g = _attn_fwd(qkv, gt, b_, s_, dil, f"attn_fwd_{gi}")
        qkvs.append(qkv)
        gtabs.append(gt)
        outs.append(o_g.reshape(t_, ATTN_OUT))
        lses.append(lse_g.reshape(t_, ATTN_OUT))
    ob = _combine_fwd(outs, lses, "attn_combine")
    wa, wb = weight("w_branch_a", ya_in), weight("w_branch_b", ob)
    ya, yb, z = _gate_out_fwd(ya_in, ob, wa, wb, proj, d_, "branch_gate")
    wo = weight("mix_w_out", z)
    r2, h2, h2b = _mm_res_ln_fwd(z, wo, h1, sp["ln2_g"], sp["ln2_b"], 1.0, "mix_out_ln2")
    w2i = weight("ffn2_w_in", h2b)
    g2, u2, a2 = _ffn_in_fwd(h2b, w2i, "ffn2_in")
    w2o = weight("ffn2_w_out", a2)
    r3, _, _ = _mm_res_ln_fwd(a2, w2o, h2, sp["ln3_g"], sp["ln3_b"], 0.5, "ffn2_out_ln3")
    dr3, dr3b, dg3, db3, loss = _ln_loss_bwd(r3, tgt, sp["ln3_g"], sp["ln3_b"], "loss_ln3_bwd")
    dep = emit("ffn2_w_out", _mm_tn(a2, dr3b, 0.5, "d_ffn2_w_out"))
    dgate2, dup2 = _ffn_mid_bwd(dr3b, w2o, g2, u2, 0.5, "ffn2_mid_bwd", dep)
    du2 = jnp.concatenate([dgate2, dup2], axis=1)
    dep = emit("ffn2_w_in", _mm_tn(du2, h2b, 1.0, "d_ffn2_w_in"))
    dr2, dr2b, dg2, db2 = _mm_nn_res_lnbwd(du2, w2i, dr3, r2, sp["ln2_g"], "ffn2_in_bwd_ln2", dep)
    dep = emit("mix_w_out", _mm_tn(z, dr2b, 1.0, "d_mix_w_out"))
    dya, dyb, dga, dgb = _dz_gate_bwd(dr2b, wo, proj, ya, yb, d_, "branch_gate_bwd", dep)
    dep = emit("w_branch_a", _mm_tn(ya_in, dya, 1.0, "d_w_branch_a"))
    dya_in = _mm_nt(dya, wa, F32, "branch_a_bwd", dep)
    dep = emit("w_branch_b", _mm_tn(dyb, ob, 1.0, "d_w_branch_b"))
    dob = _mm_nn(dyb, wb, F32, "branch_b_bwd", dep)
    dhq, dhff, dhfb, dhi, dhog, dng, dlbf, dlbb = _hgrn_bwd(
        proj, sp["hgrn_lb_fwd"], sp["hgrn_lb_bwd"], sp["hgrn_norm_g"], o_sum, dya_in, b_, s_, d_, "hgrn_bwd")
    dogs, cgs = _combine_bwd(dob, outs, lses, "attn_combine_bwd")
    dqkv = []
    for gi, (_, dil) in enumerate(ATTN_GROUPS):
        seg_rows = t_ // dil
        dq, dk, dv = _attn_bwd(qkvs[gi], gtabs[gi], dogs[gi].reshape(seg_rows, dil * ATTN_OUT),
                               cgs[gi].reshape(seg_rows, dil * ATTN_OUT), lses[gi].reshape(seg_rows, dil * ATTN_OUT),
                               b_, s_, dil, f"attn_bwd_{gi}")
        dqkv += [t.reshape(t_, ATTN_OUT).astype(BF16) for t in (dq, dk, dv)]
    dproj = jnp.concatenate([dhq, dhff, dhfb, dhi, dhog] + dqkv + [dga, dgb], axis=1)
    dep = emit("mix_w_in", _mm_tn(dproj, h1b, 1.0, "d_mix_w_in"))
    dr1, dr1b, dg1, db1 = _mm_nn_res_lnbwd(dproj, wmx, dr2, r1, sp["ln1_g"], "mix_in_bwd_ln1", dep)
    dep = emit("ffn1_w_out", _mm_tn(a1, dr1b, 0.5, "d_ffn1_w_out"))
    dgate1, dup1 = _ffn_mid_bwd(dr1b, w1o, g1, u1, 0.5, "ffn1_mid_bwd", dep)
    du1 = jnp.concatenate([dgate1, dup1], axis=1)
    dep = emit("ffn1_w_in", _mm_tn(du1, xb, 1.0, "d_ffn1_w_in"))
    grad_x = _mm_nn_res(du1, w1i, dr1, "ffn1_in_bwd", dep)
    small = {"ln1_g": dg1, "ln1_b": db1, "ln2_g": dg2, "ln2_b": db2, "ln3_g": dg3, "ln3_b": db3,
             "hgrn_norm_g": dng, "hgrn_lb_fwd": dlbf, "hgrn_lb_bwd": dlbb}
    return loss, grad_x.reshape(b_, s_, d_), small


def _pack_small(vals):
    rows = jnp.concatenate([vals[n] for n in SMALL], axis=0)
    return jnp.pad(rows, ((0, SMALL_ROWS - rows.shape[0]), (0, 0)))


def _unpack_small(packed):
    out, r = {}, 0
    for n in SMALL:
        k = 2 if n.startswith("hgrn_lb") else 1
        out[n] = packed[r:r + k]
        r += k
    return out


def kernel(x, ffn1_w_in, ffn1_w_out, ln1_g, ln1_b, mix_w_in, hgrn_lb_fwd, hgrn_lb_bwd, hgrn_norm_g, w_branch_a, w_branch_b, mix_w_out, ln2_g, ln2_b, ffn2_w_in, ffn2_w_out, ln3_g, ln3_b, loss_target, m_ffn1_w_in, m_ffn1_w_out, m_ln1_g, m_ln1_b, m_mix_w_in, m_hgrn_lb_fwd, m_hgrn_lb_bwd, m_hgrn_norm_g, m_w_branch_a, m_w_branch_b, m_mix_w_out, m_ln2_g, m_ln2_b, m_ffn2_w_in, m_ffn2_w_out, m_ln3_g, m_ln3_b, v_ffn1_w_in, v_ffn1_w_out, v_ln1_g, v_ln1_b, v_mix_w_in, v_hgrn_lb_fwd, v_hgrn_lb_bwd, v_hgrn_norm_g, v_w_branch_a, v_w_branch_b, v_mix_w_out, v_ln2_g, v_ln2_b, v_ffn2_w_in, v_ffn2_w_out, v_ln3_g, v_ln3_b):
    args = dict(locals())
    big_w = {n: args[n][0] for n in BIG}
    sp = {n: args[n] for n in SMALL}
    shards = [(big_w[n].T if n in TRANSPOSED else big_w[n]).astype(BF16) for n in BIG]
    gathering = dict(zip(BIG, _gather_start(shards, "gather_start")))
    scattering = {}

    def weight(n, after):
        return _gather_wait(gathering[n], after, f"gather_wait_{n}")

    def emit(n, grad):
        scattering[n], token = _scatter_start(grad, f"scatter_start_{n}")
        return token

    loss_part, grad_x, small = _local_step(x, loss_target, weight, emit, sp)
    loss = lax.psum(loss_part[0, 0], ("x", "y", "c"))
    out_g, out_d, out_m, out_v = {}, {}, {}, {}
    for n in BIG:
        g = _sum_partials(_scatter_wait(scattering[n], grad_x, f"scatter_wait_{n}"), f"sum_{n}")
        if n in TRANSPOSED:
            g = g.T
        d_w, m_w, v_w = _adam(big_w[n], g, args["m_" + n][0], args["v_" + n][0], f"adam_{n}")
        out_g[n], out_d[n], out_m[n], out_v[n] = g[None], d_w[None], m_w[None], v_w[None]
    (parts,) = _all_gather([_pack_small(small)], "gather_small_grads")
    res = _small_sum_adam(parts.reshape(N_DEV, SMALL_ROWS, parts.shape[1]), _pack_small(sp),
                          _pack_small({n: args["m_" + n] for n in SMALL}),
                          _pack_small({n: args["v_" + n] for n in SMALL}), "small_adam")
    sg, sd, sm, sv = (_unpack_small(r) for r in res)
    out_g.update(sg), out_d.update(sd), out_m.update(sm), out_v.update(sv)
    order = ("ffn1_w_in", "ffn1_w_out", "ln1_g", "ln1_b", "mix_w_in", "hgrn_lb_fwd", "hgrn_lb_bwd", "hgrn_norm_g",
             "w_branch_a", "w_branch_b", "mix_w_out", "ln2_g", "ln2_b", "ffn2_w_in", "ffn2_w_out", "ln3_g", "ln3_b")
    return (loss, grad_x, *[out_g[n] for n in order], *[out_d[n] for n in order],
            *[out_m[n] for n in order], *[out_v[n] for n in order])
```

```python
import functools

import jax
import jax.numpy as jnp
from jax import lax
from jax.experimental import pallas as pl
from jax.experimental.pallas import tpu as pltpu

F32 = jnp.float32
BF16 = jnp.bfloat16

N_DEV = 8
HEAD = 128
CHUNK = 32
ATTN_GROUPS = ((128, 1), (512, 4), (2048, 16))
ATTN_HEADS = 4
ATTN_HALF = 64
QKV_GROUP = 3 * ATTN_HEADS * HEAD
QKV_WIDTH = len(ATTN_GROUPS) * QKV_GROUP
ATTN_OUT = ATTN_HEADS * HEAD
ROPE_THETA = 500000.0
ROPE_DIM = HEAD // 4
ALPHA = 2.0 ** 0.25
LN_EPS = 1e-5
NEG_INF = -1e30
ADAM_LR, ADAM_B1, ADAM_B2, ADAM_EPS, ADAM_WD, ADAM_STEP = 0.001, 0.9, 0.999, 1e-08, 0.01, 10
VMEM_LIMIT = 56 * 1024 * 1024

NT = (((1,), (1,)), ((), ()))
NN = (((1,), (0,)), ((), ()))
TN = (((0,), (0,)), ((), ()))
MESH = pl.DeviceIdType.MESH


def _dot(a, b, dims):
    return lax.dot_general(a, b, dims, preferred_element_type=F32)


def _tile(n, pref, mult=128):
    if n <= pref:
        return n
    t = (pref // mult) * mult
    while t >= mult:
        if n % t == 0:
            return t
        t -= mult
    return n


def _tile_multi(ns, pref, mult=128):
    t = (pref // mult) * mult
    while t >= mult:
        if all(n % t == 0 for n in ns):
            return t
        t -= mult
    raise ValueError(f"no common tile for {ns}")


def _params(**kw):
    return pltpu.CompilerParams(vmem_limit_bytes=VMEM_LIMIT, **kw)


def _after(body, n_in, dep):
    if dep is None:
        return body, [], []

    def wrapped(*refs):
        body(*refs[:n_in], *refs[n_in + 1:])

    return wrapped, [pl.BlockSpec(dep.shape, lambda *_: (0,) * dep.ndim)], [dep]


def _sigmoid(x):
    return jax.nn.sigmoid(x)


def _dsilu(x, s):
    return s * (1.0 + x * (1.0 - s))


def _ln_stats(r):
    mu = jnp.mean(r, axis=-1, keepdims=True)
    xc = r - mu
    var = jnp.mean(xc * xc, axis=-1, keepdims=True)
    rstd = lax.rsqrt(var + LN_EPS)
    return xc * rstd, rstd


def _ln_bwd(dy, xhat, rstd, g):
    dyg = dy * g
    m1 = jnp.mean(dyg, axis=-1, keepdims=True)
    m2 = jnp.mean(dyg * xhat, axis=-1, keepdims=True)
    return rstd * (dyg - m1 - xhat * m2)


def _ffn_in_fwd(xb, w_t, name):
    t_, d_ = xb.shape
    f_ = w_t.shape[0] // 2
    tm, tn = _tile(t_, 512, 8), _tile(f_, 512)
    nj = f_ // tn

    def body(x_ref, wg_ref, wu_ref, g_ref, u_ref, a_ref):
        x = x_ref[...]
        g = _dot(x, wg_ref[...], NT)
        u = _dot(x, wu_ref[...], NT)
        g_ref[...] = g
        u_ref[...] = u
        a_ref[...] = (g * _sigmoid(g) * u).astype(BF16)

    return pl.pallas_call(
        body, name=name, grid=(t_ // tm, nj),
        in_specs=[pl.BlockSpec((tm, d_), lambda i, j: (i, 0)),
                  pl.BlockSpec((tn, d_), lambda i, j: (j, 0)),
                  pl.BlockSpec((tn, d_), lambda i, j: (j + nj, 0))],
        out_specs=[pl.BlockSpec((tm, tn), lambda i, j: (i, j))] * 3,
        out_shape=[jax.ShapeDtypeStruct((t_, f_), F32), jax.ShapeDtypeStruct((t_, f_), F32),
                   jax.ShapeDtypeStruct((t_, f_), BF16)],
        compiler_params=_params(),
    )(xb, w_t, w_t)


def _mm_res_ln_fwd(a, w, res, g, b, scale, name):
    t_, k_ = a.shape
    d_ = w.shape[1]
    tm, tk = _tile(t_, 512, 8), _tile(k_, 512)
    nk = k_ // tk

    def body(a_ref, w_ref, res_ref, g_ref, b_ref, r_ref, h_ref, hb_ref, acc):
        k = pl.program_id(1)

        @pl.when(k == 0)
        def _():
            acc[...] = jnp.zeros_like(acc)

        acc[...] += _dot(a_ref[...], w_ref[...], NN)

        @pl.when(k == nk - 1)
        def _():
            r = ALPHA * res_ref[...] + scale * acc[...]
            xhat, _ = _ln_stats(r)
            h = xhat * g_ref[...] + b_ref[...]
            r_ref[...] = r
            h_ref[...] = h
            hb_ref[...] = h.astype(BF16)

    row = pl.BlockSpec((tm, d_), lambda i, k: (i, 0))
    vec = pl.BlockSpec((1, d_), lambda i, k: (0, 0))
    return pl.pallas_call(
        body, name=name, grid=(t_ // tm, nk),
        in_specs=[pl.BlockSpec((tm, tk), lambda i, k: (i, k)),
                  pl.BlockSpec((tk, d_), lambda i, k: (k, 0)), row, vec, vec],
        out_specs=[row, row, row],
        out_shape=[jax.ShapeDtypeStruct((t_, d_), F32), jax.ShapeDtypeStruct((t_, d_), F32),
                   jax.ShapeDtypeStruct((t_, d_), BF16)],
        scratch_shapes=[pltpu.VMEM((tm, d_), F32)],
        compiler_params=_params(),
    )(a, w, res, g, b)


def _mm_nt(a, w_t, out_dtype, name, dep=None):
    t_, k_ = a.shape
    n_ = w_t.shape[0]
    tm, tn = _tile(t_, 512, 8), _tile(n_, 512)

    def body(a_ref, w_ref, o_ref):
        o_ref[...] = _dot(a_ref[...], w_ref[...], NT).astype(out_dtype)

    body, dep_specs, deps = _after(body, 2, dep)
    return pl.pallas_call(
        body, name=name, grid=(t_ // tm, n_ // tn),
        in_specs=[pl.BlockSpec((tm, k_), lambda i, j: (i, 0)),
                  pl.BlockSpec((tn, k_), lambda i, j: (j, 0)), *dep_specs],
        out_specs=pl.BlockSpec((tm, tn), lambda i, j: (i, j)),
        out_shape=jax.ShapeDtypeStruct((t_, n_), out_dtype),
        compiler_params=_params(),
    )(a, w_t, *deps)


def _mm_nn(a, w, out_dtype, name, dep=None):
    t_, k_ = a.shape
    n_ = w.shape[1]
    tm, tn = _tile(t_, 512, 8), _tile(n_, 512)

    def body(a_ref, w_ref, o_ref):
        o_ref[...] = _dot(a_ref[...], w_ref[...], NN).astype(out_dtype)

    body, dep_specs, deps = _after(body, 2, dep)
    return pl.pallas_call(
        body, name=name, grid=(t_ // tm, n_ // tn),
        in_specs=[pl.BlockSpec((tm, k_), lambda i, j: (i, 0)),
                  pl.BlockSpec((k_, tn), lambda i, j: (0, j)), *dep_specs],
        out_specs=pl.BlockSpec((tm, tn), lambda i, j: (i, j)),
        out_shape=jax.ShapeDtypeStruct((t_, n_), out_dtype),
        compiler_params=_params(),
    )(a, w, *deps)


def _mm_tn(a, b, scale, name):
    t_, m_ = a.shape
    n_ = b.shape[1]
    tm, tn, tk = _tile(m_, 512), _tile(n_, 2048), _tile(t_, 512, 8)
    nk = t_ // tk

    def body(a_ref, b_ref, o_ref, acc):
        k = pl.program_id(2)

        @pl.when(k == 0)
        def _():
            acc[...] = jnp.zeros_like(acc)

        acc[...] += _dot(a_ref[...], b_ref[...], TN)

        @pl.when(k == nk - 1)
        def _():
            o_ref[...] = (scale * acc[...]).astype(BF16)

    return pl.pallas_call(
        body, name=name, grid=(m_ // tm, n_ // tn, nk),
        in_specs=[pl.BlockSpec((tk, tm), lambda i, j, k: (k, i)),
                  pl.BlockSpec((tk, tn), lambda i, j, k: (k, j))],
        out_specs=pl.BlockSpec((tm, tn), lambda i, j, k: (i, j)),
        out_shape=jax.ShapeDtypeStruct((m_, n_), BF16),
        scratch_shapes=[pltpu.VMEM((tm, tn), F32)],
        compiler_params=_params(),
    )(a, b)


def _gate_out_fwd(ya_in, ob, wa, wb_t, proj, d_, name):
    t_ = ya_in.shape[0]
    goff = 5 * d_ + QKV_WIDTH
    tm, tn = _tile(t_, 512, 8), _tile_multi([d_, goff], 512)
    ja, jb = goff // tn, (goff + d_) // tn

    def body(ya_ref, ob_ref, wa_ref, wb_ref, ga_ref, gb_ref, yao_ref, ybo_ref, z_ref):
        y_a = _dot(ya_ref[...], wa_ref[...], NN)
        y_b = _dot(ob_ref[...], wb_ref[...], NT)
        yao_ref[...] = y_a
        ybo_ref[...] = y_b
        z_ref[...] = (_sigmoid(ga_ref[...]) * y_a + _sigmoid(gb_ref[...]) * y_b).astype(BF16)

    tile = pl.BlockSpec((tm, tn), lambda i, j: (i, j))
    return pl.pallas_call(
        body, name=name, grid=(t_ // tm, d_ // tn),
        in_specs=[pl.BlockSpec((tm, d_), lambda i, j: (i, 0)),
                  pl.BlockSpec((tm, ATTN_OUT), lambda i, j: (i, 0)),
                  pl.BlockSpec((d_, tn), lambda i, j: (0, j)),
                  pl.BlockSpec((tn, ATTN_OUT), lambda i, j: (j, 0)),
                  pl.BlockSpec((tm, tn), lambda i, j: (i, ja + j)),
                  pl.BlockSpec((tm, tn), lambda i, j: (i, jb + j))],
        out_specs=[tile, tile, tile],
        out_shape=[jax.ShapeDtypeStruct((t_, d_), F32), jax.ShapeDtypeStruct((t_, d_), F32),
                   jax.ShapeDtypeStruct((t_, d_), BF16)],
        compiler_params=_params(),
    )(ya_in, ob, wa, wb_t, proj, proj)


def _ffn_mid_bwd(drb, w_out, gate, up, scale, name, dep=None):
    t_, d_ = drb.shape
    f_ = w_out.shape[0]
    tm, tn = _tile(t_, 512, 8), _tile(f_, 512)

    def body(dr_ref, w_ref, g_ref, u_ref, dg_ref, du_ref):
        da = scale * _dot(dr_ref[...], w_ref[...], NT)
        g = g_ref[...]
        s = _sigmoid(g)
        dg_ref[...] = (da * u_ref[...] * _dsilu(g, s)).astype(BF16)
        du_ref[...] = (da * g * s).astype(BF16)

    body, dep_specs, deps = _after(body, 4, dep)
    tile = pl.BlockSpec((tm, tn), lambda i, j: (i, j))
    return pl.pallas_call(
        body, name=name, grid=(t_ // tm, f_ // tn),
        in_specs=[pl.BlockSpec((tm, d_), lambda i, j: (i, 0)),
                  pl.BlockSpec((tn, d_), lambda i, j: (j, 0)), tile, tile, *dep_specs],
        out_specs=[tile, tile],
        out_shape=[jax.ShapeDtypeStruct((t_, f_), BF16)] * 2,
        compiler_params=_params(),
    )(drb, w_out, gate, up, *deps)


def _mm_nn_res_lnbwd(a, w, dres, r, g, name, dep=None):
    t_, k_ = a.shape
    d_ = w.shape[1]
    tm, tk = _tile(t_, 512, 8), _tile(k_, 512)
    nk = k_ // tk

    def body(a_ref, w_ref, dres_ref, r_ref, g_ref, dr_ref, drb_ref, dg_ref, db_ref, acc):
        i, k = pl.program_id(0), pl.program_id(1)

        @pl.when(k == 0)
        def _():
            acc[...] = jnp.zeros_like(acc)

        @pl.when((i == 0) & (k == 0))
        def _():
            dg_ref[...] = jnp.zeros_like(dg_ref)
            db_ref[...] = jnp.zeros_like(db_ref)

        acc[...] += _dot(a_ref[...], w_ref[...], NN)

        @pl.when(k == nk - 1)
        def _():
            dy = acc[...] + ALPHA * dres_ref[...]
            xhat, rstd = _ln_stats(r_ref[...])
            dr = _ln_bwd(dy, xhat, rstd, g_ref[...])
            dr_ref[...] = dr
            drb_ref[...] = dr.astype(BF16)
            dg_ref[...] += jnp.sum(dy * xhat, axis=0, keepdims=True)
            db_ref[...] += jnp.sum(dy, axis=0, keepdims=True)

    body, dep_specs, deps = _after(body, 5, dep)
    row = pl.BlockSpec((tm, d_), lambda i, k: (i, 0))
    vec = pl.BlockSpec((1, d_), lambda i, k: (0, 0))
    return pl.pallas_call(
        body, name=name, grid=(t_ // tm, nk),
        in_specs=[pl.BlockSpec((tm, tk), lambda i, k: (i, k)),
                  pl.BlockSpec((tk, d_), lambda i, k: (k, 0)), row, row, vec, *dep_specs],
        out_specs=[row, row, vec, vec],
        out_shape=[jax.ShapeDtypeStruct((t_, d_), F32), jax.ShapeDtypeStruct((t_, d_), BF16),
                   jax.ShapeDtypeStruct((1, d_), F32), jax.ShapeDtypeStruct((1, d_), F32)],
        scratch_shapes=[pltpu.VMEM((tm, d_), F32)],
        compiler_params=_params(),
    )(a, w, dres, r, g, *deps)


def _mm_nn_res(a, w, dres, name, dep=None):
    t_, k_ = a.shape
    d_ = w.shape[1]
    tm, tk = _tile(t_, 512, 8), _tile(k_, 512)
    nk = k_ // tk

    def body(a_ref, w_ref, dres_ref, o_ref, acc):
        k = pl.program_id(1)

        @pl.when(k == 0)
        def _():
            acc[...] = jnp.zeros_like(acc)

        acc[...] += _dot(a_ref[...], w_ref[...], NN)

        @pl.when(k == nk - 1)
        def _():
            o_ref[...] = acc[...] + ALPHA * dres_ref[...]

    body, dep_specs, deps = _after(body, 3, dep)
    row = pl.BlockSpec((tm, d_), lambda i, k: (i, 0))
    return pl.pallas_call(
        body, name=name, grid=(t_ // tm, nk),
        in_specs=[pl.BlockSpec((tm, tk), lambda i, k: (i, k)),
                  pl.BlockSpec((tk, d_), lambda i, k: (k, 0)), row, *dep_specs],
        out_specs=row,
        out_shape=jax.ShapeDtypeStruct((t_, d_), F32),
        scratch_shapes=[pltpu.VMEM((tm, d_), F32)],
        compiler_params=_params(),
    )(a, w, dres, *deps)


def _dz_gate_bwd(drb, w_out, proj, ya, yb, d_, name, dep=None):
    t_ = drb.shape[0]
    goff = 5 * d_ + QKV_WIDTH
    tm, tn = _tile(t_, 512, 8), _tile_multi([d_, goff], 512)
    ja, jb = goff // tn, (goff + d_) // tn

    def body(dr_ref, w_ref, ga_ref, gb_ref, ya_ref, yb_ref, dya_ref, dyb_ref, dga_ref, dgb_ref):
        dz = _dot(dr_ref[...], w_ref[...], NT)
        sa, sb = _sigmoid(ga_ref[...]), _sigmoid(gb_ref[...])
        dya_ref[...] = (dz * sa).astype(BF16)
        dyb_ref[...] = (dz * sb).astype(BF16)
        dga_ref[...] = (dz * ya_ref[...] * sa * (1.0 - sa)).astype(BF16)
        dgb_ref[...] = (dz * yb_ref[...] * sb * (1.0 - sb)).astype(BF16)

    body, dep_specs, deps = _after(body, 6, dep)
    tile = pl.BlockSpec((tm, tn), lambda i, j: (i, j))
    return pl.pallas_call(
        body, name=name, grid=(t_ // tm, d_ // tn),
        in_specs=[pl.BlockSpec((tm, d_), lambda i, j: (i, 0)),
                  pl.BlockSpec((tn, d_), lambda i, j: (j, 0)),
                  pl.BlockSpec((tm, tn), lambda i, j: (i, ja + j)),
                  pl.BlockSpec((tm, tn), lambda i, j: (i, jb + j)), tile, tile, *dep_specs],
        out_specs=[tile] * 4,
        out_shape=[jax.ShapeDtypeStruct((t_, d_), BF16)] * 4,
        compiler_params=_params(),
    )(drb, w_out, proj, proj, ya, yb, *deps)


def _ln_loss_bwd(r, target, g, b, name):
    t_, d_ = r.shape
    tm = _tile(t_, 256, 8)

    def body(r_ref, t_ref, g_ref, b_ref, dr_ref, drb_ref, dg_ref, db_ref, loss_ref):
        i = pl.program_id(0)

        @pl.when(i == 0)
        def _():
            dg_ref[...] = jnp.zeros_like(dg_ref)
            db_ref[...] = jnp.zeros_like(db_ref)
            loss_ref[...] = jnp.zeros_like(loss_ref)

        xhat, rstd = _ln_stats(r_ref[...])
        gain = g_ref[...]
        err = xhat * gain + b_ref[...] - t_ref[...]
        loss_ref[...] += (0.5 / d_) * jnp.sum(err * err)
        dy = err * (1.0 / d_)
        dr = _ln_bwd(dy, xhat, rstd, gain)
        dr_ref[...] = dr
        drb_ref[...] = dr.astype(BF16)
        dg_ref[...] += jnp.sum(dy * xhat, axis=0, keepdims=True)
        db_ref[...] += jnp.sum(dy, axis=0, keepdims=True)

    row = pl.BlockSpec((tm, d_), lambda i: (i, 0))
    vec = pl.BlockSpec((1, d_), lambda i: (0, 0))
    return pl.pallas_call(
        body, name=name, grid=(t_ // tm,),
        in_specs=[row, row, vec, vec],
        out_specs=[row, row, vec, vec, pl.BlockSpec((1, HEAD), lambda i: (0, 0))],
        out_shape=[jax.ShapeDtypeStruct((t_, d_), F32), jax.ShapeDtypeStruct((t_, d_), BF16),
                   jax.ShapeDtypeStruct((1, d_), F32), jax.ShapeDtypeStruct((1, d_), F32),
                   jax.ShapeDtypeStruct((1, HEAD), F32)],
        compiler_params=_params(),
    )(r, target, g, b)


def _chunk_scan(x, row, reverse, size):
    s = 1
    while s < CHUNK:
        if reverse:
            x = x + jnp.where(row < CHUNK - s, pltpu.roll(x, size - s, 0), 0.0)
        else:
            x = x + jnp.where(row >= s, pltpu.roll(x, s, 0), 0.0)
        s *= 2
    return x


def _lower_bound(tab):
    return _sigmoid(tab[0:1, :] - tab[1:2, :])


def _tri_mask(reverse):
    r = lax.broadcasted_iota(jnp.int32, (CHUNK, CHUNK), 0)
    c = lax.broadcasted_iota(jnp.int32, (CHUNK, CHUNK), 1)
    return (c >= r) if reverse else (r >= c)


def _hgrn_fwd(proj, lbf, lbb, ng, b_, s_, d_, name):
    h_ = d_ // HEAD
    nc = s_ // CHUNK

    def body(hq_ref, hff_ref, hfb_ref, hi_ref, hog_ref, lbf_ref, lbb_ref, ng_ref, ya_ref, o_ref,
             q_s, k_s, cum_s, o_s):
        row = lax.broadcasted_iota(jnp.int32, (s_, HEAD), 0) % CHUNK
        hq = hq_ref[...]
        q_s[...] = hq * _sigmoid(hq)
        o_s[...] = jnp.zeros_like(o_s)
        for reverse, hf_ref, lb_ref in ((False, hff_ref, lbf_ref), (True, hfb_ref, lbb_ref)):
            lb = _lower_bound(lb_ref[...])
            f = lb + (1.0 - lb) * _sigmoid(hf_ref[...])
            k_s[...] = 1.0 - f
            cum_s[...] = _chunk_scan(jnp.log(f), row, reverse, s_)
            mask = _tri_mask(reverse)

            def step(n, st, reverse=reverse, mask=mask):
                idx = (nc - 1 - n) if reverse else n
                sl = pl.ds(pl.multiple_of(idx * CHUNK, CHUNK), CHUNK)
                cm = cum_s[sl, :]
                tot = cm[0:1, :] if reverse else cm[CHUNK - 1:CHUNK, :]
                qc, kc = q_s[sl, :], k_s[sl, :]
                vb = hi_ref[sl, :].astype(BF16)
                qd = (qc * jnp.exp(cm)).astype(BF16)
                kd = (kc * jnp.exp(-cm)).astype(BF16)
                ke = (kc * jnp.exp(tot - cm)).astype(BF16)
                a = jnp.where(mask, _dot(qd, kd, NT), 0.0)
                o_s[sl, :] += _dot(a.astype(BF16), vb, NN) + _dot(qd, st.astype(BF16), NT)
                return st * jnp.exp(tot) + _dot(vb, ke, TN)

            lax.fori_loop(0, nc, step, jnp.zeros((HEAD, HEAD), F32))
        o = o_s[...]
        o_ref[...] = o
        nrm = o * lax.rsqrt(jnp.mean(o * o, axis=-1, keepdims=True) + LN_EPS)
        hog = hog_ref[...]
        ya_ref[...] = (nrm * ng_ref[...] * hog * _sigmoid(hog)).astype(BF16)

    def col(part):
        return pl.BlockSpec((s_, HEAD), lambda h, b, part=part: (b, part * h_ + h))

    tab = pl.BlockSpec((2, HEAD), lambda h, b: (0, h))
    out = pl.BlockSpec((s_, HEAD), lambda h, b: (b, h))
    return pl.pallas_call(
        body, name=name, grid=(h_, b_),
        in_specs=[col(0), col(1), col(2), col(3), col(4), tab, tab,
                  pl.BlockSpec((1, HEAD), lambda h, b: (0, h))],
        out_specs=[out, out],
        out_shape=[jax.ShapeDtypeStruct((b_ * s_, d_), BF16), jax.ShapeDtypeStruct((b_ * s_, d_), F32)],
        scratch_shapes=[pltpu.VMEM((s_, HEAD), F32)] * 4,
        compiler_params=_params(),
    )(proj, proj, proj, proj, proj, lbf, lbb, ng)


def _hgrn_bwd(proj, lbf, lbb, ng, o_sum, dya, b_, s_, d_, name):
    h_ = d_ // HEAD
    nc = s_ // CHUNK

    def body(hq_ref, hff_ref, hfb_ref, hi_ref, hog_ref, lbf_ref, lbb_ref, ng_ref, o_ref, dya_ref,
             dhq_ref, dhff_ref, dhfb_ref, dhi_ref, dhog_ref, dng_ref, dlbf_ref, dlbb_ref,
             q_s, k_s, cum_s, do_s, dq_s, dv_s, db_s, dk_s, st_s):
        b = pl.program_id(1)

        @pl.when(b == 0)
        def _():
            dng_ref[...] = jnp.zeros_like(dng_ref)
            dlbf_ref[...] = jnp.zeros_like(dlbf_ref)
            dlbb_ref[...] = jnp.zeros_like(dlbb_ref)

        row = lax.broadcasted_iota(jnp.int32, (s_, HEAD), 0) % CHUNK
        crow = lax.broadcasted_iota(jnp.int32, (CHUNK, HEAD), 0)
        hq = hq_ref[...]
        sq = _sigmoid(hq)
        q_s[...] = hq * sq
        o = o_ref[...]
        rinv = lax.rsqrt(jnp.mean(o * o, axis=-1, keepdims=True) + LN_EPS)
        nrm = o * rinv
        hog = hog_ref[...]
        so = _sigmoid(hog)
        gain = ng_ref[...]
        dy = dya_ref[...]
        dhog_ref[...] = (dy * nrm * gain * _dsilu(hog, so)).astype(BF16)
        dng_ref[...] += jnp.sum(dy * nrm * hog * so, axis=0, keepdims=True)
        dn = dy * gain * hog * so
        do_s[...] = rinv * (dn - nrm * jnp.mean(dn * nrm, axis=-1, keepdims=True))
        dq_s[...] = jnp.zeros_like(dq_s)
        dv_s[...] = jnp.zeros_like(dv_s)

        for reverse, hf_ref, lb_ref, dhf_ref, dlb_ref in (
                (False, hff_ref, lbf_ref, dhff_ref, dlbf_ref), (True, hfb_ref, lbb_ref, dhfb_ref, dlbb_ref)):
            tab = lb_ref[...]
            lb = _lower_bound(tab)
            sf = _sigmoid(hf_ref[...])
            f = lb + (1.0 - lb) * sf
            k_s[...] = 1.0 - f
            cum_s[...] = _chunk_scan(jnp.log(f), row, reverse, s_)
            mask = _tri_mask(reverse)
            last = 0 if reverse else CHUNK - 1

            def chunk(idx, reverse=reverse):
                sl = pl.ds(pl.multiple_of(idx * CHUNK, CHUNK), CHUNK)
                cm = cum_s[sl, :]
                tot = cm[0:1, :] if reverse else cm[CHUNK - 1:CHUNK, :]
                return sl, cm, tot

            def fstep(n, st, reverse=reverse, chunk=chunk):
                idx = (nc - 1 - n) if reverse else n
                sl, cm, tot = chunk(idx)
                st_s[idx] = st
                ke = (k_s[sl, :] * jnp.exp(tot - cm)).astype(BF16)
                return st * jnp.exp(tot) + _dot(hi_ref[sl, :].astype(BF16), ke, TN)

            lax.fori_loop(0, nc, fstep, jnp.zeros((HEAD, HEAD), F32))

            def bstep(n, dst, reverse=reverse, chunk=chunk, mask=mask, last=last):
                idx = n if reverse else (nc - 1 - n)
                sl, cm, tot = chunk(idx)
                eb, enb, ee, dec = jnp.exp(cm), jnp.exp(-cm), jnp.exp(tot - cm), jnp.exp(tot)
                qc, kc = q_s[sl, :], k_s[sl, :]
                qd, kd, ke = qc * eb, kc * enb, kc * ee
                qdb, kdb, keb = qd.astype(BF16), kd.astype(BF16), ke.astype(BF16)
                vb = hi_ref[sl, :].astype(BF16)
                dob = do_s[sl, :].astype(BF16)
                st0 = st_s[idx]
                dstb = dst.astype(BF16)
                a = jnp.where(mask, _dot(qdb, kdb, NT), 0.0).astype(BF16)
                da = jnp.where(mask, _dot(dob, vb, NT), 0.0).astype(BF16)
                dqd = _dot(da, kdb, NN) + _dot(dob, st0.astype(BF16), NN)
                dkd = _dot(da, qdb, TN)
                dv = _dot(a, dob, TN) + _dot(keb, dstb, NT)
                dke = _dot(vb, dstb, NN)
                ddec = jnp.sum(dst * st0, axis=0, keepdims=True)
                dtot = jnp.sum(dke * ke, axis=0, keepdims=True) + ddec * dec
                db = dqd * qd - dkd * kd - dke * ke
                db_s[sl, :] = db + jnp.where(crow == last, dtot, 0.0)
                dk_s[sl, :] = dkd * enb + dke * ee
                dq_s[sl, :] += dqd * eb
                dv_s[sl, :] += dv
                return dst * dec + _dot(dob, qdb, TN)

            lax.fori_loop(0, nc, bstep, jnp.zeros((HEAD, HEAD), F32))
            dlogf = _chunk_scan(db_s[...], row, not reverse, s_)
            df = dlogf / f - dk_s[...]
            dhf_ref[...] = (df * (1.0 - lb) * sf * (1.0 - sf)).astype(BF16)
            dlb = jnp.sum(df * (1.0 - sf), axis=0, keepdims=True) * lb * (1.0 - lb)
            dlb_ref[0:1, :] += dlb
            dlb_ref[1:2, :] -= dlb

        dhq_ref[...] = (dq_s[...] * _dsilu(hq, sq)).astype(BF16)
        dhi_ref[...] = dv_s[...].astype(BF16)

    def col(part):
        return pl.BlockSpec((s_, HEAD), lambda h, b, part=part: (b, part * h_ + h))

    tab = pl.BlockSpec((2, HEAD), lambda h, b: (0, h))
    vec = pl.BlockSpec((1, HEAD), lambda h, b: (0, h))
    blk = pl.BlockSpec((s_, HEAD), lambda h, b: (b, h))
    act = jax.ShapeDtypeStruct((b_ * s_, d_), BF16)
    return pl.pallas_call(
        body, name=name, grid=(h_, b_),
        in_specs=[col(0), col(1), col(2), col(3), col(4), tab, tab, vec, blk, blk],
        out_specs=[blk] * 5 + [vec, tab, tab],
        out_shape=[act] * 5 + [jax.ShapeDtypeStruct((1, d_), F32), jax.ShapeDtypeStruct((2, d_), F32),
                               jax.ShapeDtypeStruct((2, d_), F32)],
        scratch_shapes=[pltpu.VMEM((s_, HEAD), F32)] * 8 + [pltpu.VMEM((nc, HEAD, HEAD), F32)],
        compiler_params=_params(),
    )(proj, proj, proj, proj, proj, lbf, lbb, ng, o_sum, dya)


BLOCK = 64
HALF = BLOCK // 2


def _block_scan(x, row, reverse, size):
    s = 1
    while s < BLOCK:
        if reverse:
            x = x + jnp.where(row < BLOCK - s, pltpu.roll(x, size - s, 0), 0.0)
        else:
            x = x + jnp.where(row >= s, pltpu.roll(x, s, 0), 0.0)
        s *= 2
    return x


def _block_exps(l, reverse):
    first = lax.broadcasted_iota(jnp.int32, (BLOCK, HEAD), 0) < HALF
    q1, q3 = HALF // 2, HALF + HALF // 2
    if reverse:
        rho1, rho2, lh, ltot = l[q1:q1 + 1], l[q3:q3 + 1], l[HALF:HALF + 1], l[0:1]
    else:
        rho1, rho2, lh, ltot = l[q1 - 1:q1], l[q3 - 1:q3], l[HALF - 1:HALF], l[BLOCK - 1:BLOCK]
    ref = jnp.where(first, rho1, rho2)
    query_half = first if reverse else jnp.logical_not(first)
    e2 = jnp.where(query_half, jnp.exp(jnp.minimum(l - lh, 0.0)), 0.0)
    e1 = jnp.where(query_half, 0.0, jnp.exp(jnp.minimum(lh - l, 0.0)))
    return jnp.exp(l - ref), jnp.exp(ref - l), e2, e1, jnp.exp(l), jnp.exp(ltot - l), jnp.exp(ltot)


def _half_mask(reverse):
    r = lax.broadcasted_iota(jnp.int32, (BLOCK, BLOCK), 0)
    c = lax.broadcasted_iota(jnp.int32, (BLOCK, BLOCK), 1)
    same = (r < HALF) == (c < HALF)
    return same & ((c >= r) if reverse else (r >= c))


def _hgrn_fwd(proj, lbf, lbb, ng, b_, s_, d_, name):
    h_ = d_ // HEAD
    nb = s_ // BLOCK

    def body(hq_ref, hff_ref, hfb_ref, hi_ref, hog_ref, lbf_ref, lbb_ref, ng_ref, ya_ref, o_ref,
             q_s, k_s, l_s, of_s, oi_s, qd_s, u_s, st_s, dec_s):
        row = lax.broadcasted_iota(jnp.int32, (s_, HEAD), 0) % BLOCK
        hq = hq_ref[...]
        q_s[...] = hq * _sigmoid(hq)
        for reverse, hf_ref, lb_ref in ((False, hff_ref, lbf_ref), (True, hfb_ref, lbb_ref)):
            lb = _lower_bound(lb_ref[...])
            f = lb + (1.0 - lb) * _sigmoid(hf_ref[...])
            k_s[...] = 1.0 - f
            l_s[...] = _block_scan(jnp.log(f), row, reverse, s_)
            mask = _half_mask(reverse)

            def inside(n, carry, reverse=reverse, mask=mask):
                sl = pl.ds(pl.multiple_of(n * BLOCK, BLOCK), BLOCK)
                eq, ek, e2, e1, el, ee, dec = _block_exps(l_s[sl, :], reverse)
                qc, kc = q_s[sl, :], k_s[sl, :]
                vb = hi_ref[sl, :].astype(BF16)
                a = jnp.where(mask, _dot((qc * eq).astype(BF16), (kc * ek).astype(BF16), NT), 0.0)
                a = a + _dot((qc * e2).astype(BF16), (kc * e1).astype(BF16), NT)
                oi_s[sl, :] = _dot(a.astype(BF16), vb, NN)
                qd_s[sl, :] = (qc * el).astype(BF16)
                u_s[n] = _dot(vb, (kc * ee).astype(BF16), TN)
                dec_s[n] = jnp.broadcast_to(dec, (8, HEAD))
                return carry

            lax.fori_loop(0, nb, inside, 0, unroll=4)

            def carry_state(n, st, reverse=reverse):
                idx = (nb - 1 - n) if reverse else n
                st_s[idx] = st.astype(BF16)
                return st * dec_s[idx][0:1, :] + u_s[idx]

            lax.fori_loop(0, nb, carry_state, jnp.zeros((HEAD, HEAD), F32))

            def across(n, carry, reverse=reverse):
                sl = pl.ds(pl.multiple_of(n * BLOCK, BLOCK), BLOCK)
                o_dir = oi_s[sl, :] + _dot(qd_s[sl, :], st_s[n], NT)
                if not reverse:
                    of_s[sl, :] = o_dir
                else:
                    o = of_s[sl, :] + o_dir
                    o_ref[sl, :] = o
                    nrm = o * lax.rsqrt(jnp.mean(o * o, axis=-1, keepdims=True) + LN_EPS)
                    hog = hog_ref[sl, :]
                    ya_ref[sl, :] = (nrm * ng_ref[...] * hog * _sigmoid(hog)).astype(BF16)
                return carry

            lax.fori_loop(0, nb, across, 0, unroll=8)

    def col(part):
        return pl.BlockSpec((s_, HEAD), lambda h, b, part=part: (b, part * h_ + h))

    tab = pl.BlockSpec((2, HEAD), lambda h, b: (0, h))
    out = pl.BlockSpec((s_, HEAD), lambda h, b: (b, h))
    return pl.pallas_call(
        body, name=name, grid=(h_, b_),
        in_specs=[col(0), col(1), col(2), col(3), col(4), tab, tab,
                  pl.BlockSpec((1, HEAD), lambda h, b: (0, h))],
        out_specs=[out, out],
        out_shape=[jax.ShapeDtypeStruct((b_ * s_, d_), BF16), jax.ShapeDtypeStruct((b_ * s_, d_), F32)],
        scratch_shapes=[pltpu.VMEM((s_, HEAD), F32)] * 5 + [
            pltpu.VMEM((s_, HEAD), BF16), pltpu.VMEM((nb, HEAD, HEAD), F32), pltpu.VMEM((nb, HEAD, HEAD), BF16),
            pltpu.VMEM((nb, 8, HEAD), F32)],
        compiler_params=_params(),
    )(proj, proj, proj, proj, proj, lbf, lbb, ng)


def _hgrn_bwd(proj, lbf, lbb, ng, o_sum, dya, b_, s_, d_, name):
    h_ = d_ // HEAD
    nb = s_ // BLOCK

    def body(hq_ref, hff_ref, hfb_ref, hi_ref, hog_ref, lbf_ref, lbb_ref, ng_ref, o_ref, dya_ref,
             dhq_ref, dhff_ref, dhfb_ref, dhi_ref, dhog_ref, dng_ref, dlbf_ref, dlbb_ref,
             q_s, k_s, l_s, do_s, dq_s, dv_s, dl_s, dk_s, u_s, w_s, st_s, dst_s, dec_s):
        b = pl.program_id(1)

        @pl.when(b == 0)
        def _():
            dng_ref[...] = jnp.zeros_like(dng_ref)
            dlbf_ref[...] = jnp.zeros_like(dlbf_ref)
            dlbb_ref[...] = jnp.zeros_like(dlbb_ref)

        row = lax.broadcasted_iota(jnp.int32, (s_, HEAD), 0) % BLOCK
        brow = lax.broadcasted_iota(jnp.int32, (BLOCK, HEAD), 0)
        hq = hq_ref[...]
        sq = _sigmoid(hq)
        q_s[...] = hq * sq
        o = o_ref[...]
        rinv = lax.rsqrt(jnp.mean(o * o, axis=-1, keepdims=True) + LN_EPS)
        nrm = o * rinv
        hog = hog_ref[...]
        so = _sigmoid(hog)
        gain = ng_ref[...]
        dy = dya_ref[...]
        dhog_ref[...] = (dy * nrm * gain * _dsilu(hog, so)).astype(BF16)
        dng_ref[...] += jnp.sum(dy * nrm * hog * so, axis=0, keepdims=True)
        dn = dy * gain * hog * so
        do_s[...] = rinv * (dn - nrm * jnp.mean(dn * nrm, axis=-1, keepdims=True))

        for reverse, hf_ref, lb_ref, dhf_ref, dlb_ref in (
                (False, hff_ref, lbf_ref, dhff_ref, dlbf_ref), (True, hfb_ref, lbb_ref, dhfb_ref, dlbb_ref)):
            lb = _lower_bound(lb_ref[...])
            sf = _sigmoid(hf_ref[...])
            f = lb + (1.0 - lb) * sf
            k_s[...] = 1.0 - f
            l_s[...] = _block_scan(jnp.log(f), row, reverse, s_)
            mask = _half_mask(reverse)
            total_row = 0 if reverse else BLOCK - 1

            def prepare(n, carry, reverse=reverse):
                sl = pl.ds(pl.multiple_of(n * BLOCK, BLOCK), BLOCK)
                _, _, _, _, el, ee, dec = _block_exps(l_s[sl, :], reverse)
                vb = hi_ref[sl, :].astype(BF16)
                u_s[n] = _dot(vb, (k_s[sl, :] * ee).astype(BF16), TN)
                w_s[n] = _dot(do_s[sl, :].astype(BF16), (q_s[sl, :] * el).astype(BF16), TN)
                dec_s[n] = jnp.broadcast_to(dec, (8, HEAD))
                return carry

            lax.fori_loop(0, nb, prepare, 0, unroll=4)

            def carry_state(n, st, reverse=reverse):
                idx = (nb - 1 - n) if reverse else n
                st_s[idx] = st
                return st * dec_s[idx][0:1, :] + u_s[idx]

            lax.fori_loop(0, nb, carry_state, jnp.zeros((HEAD, HEAD), F32))

            def carry_grad(n, dst, reverse=reverse):
                idx = n if reverse else (nb - 1 - n)
                dst_s[idx] = dst
                return dst * dec_s[idx][0:1, :] + w_s[idx]

            lax.fori_loop(0, nb, carry_grad, jnp.zeros((HEAD, HEAD), F32))

            def inside(n, carry, reverse=reverse, mask=mask, total_row=total_row):
                sl = pl.ds(pl.multiple_of(n * BLOCK, BLOCK), BLOCK)
                eq, ek, e2, e1, el, ee, dec = _block_exps(l_s[sl, :], reverse)
                qc, kc = q_s[sl, :], k_s[sl, :]
                vb = hi_ref[sl, :].astype(BF16)
                dob = do_s[sl, :].astype(BF16)
                qt, kt, q2, k1 = ((qc * eq).astype(BF16), (kc * ek).astype(BF16),
                                  (qc * e2).astype(BF16), (kc * e1).astype(BF16))
                kend = kc * ee
                st0, dst1 = st_s[n], dst_s[n]
                dstb = dst1.astype(BF16)
                a = jnp.where(mask, _dot(qt, kt, NT), 0.0) + _dot(q2, k1, NT)
                da = _dot(dob, vb, NT)
                dab = da.astype(BF16)
                dad = jnp.where(mask, da, 0.0).astype(BF16)
                dqt, dkt = _dot(dad, kt, NN), _dot(dad, qt, TN)
                dq2, dk1 = _dot(dab, k1, NN), _dot(dab, q2, TN)
                dqd = _dot(dob, st0.astype(BF16), NN)
                dke = _dot(vb, dstb, NN)
                dv = _dot(a.astype(BF16), dob, TN) + _dot(kend.astype(BF16), dstb, NT)
                dq = dqt * eq + dq2 * e2 + dqd * el
                dk = dkt * ek + dk1 * e1 + dke * ee
                dtot = jnp.sum(dke * kend, axis=0, keepdims=True) + jnp.sum(dst1 * st0, axis=0, keepdims=True) * dec
                dl_s[sl, :] = qc * dq - kc * dk + jnp.where(brow == total_row, dtot, 0.0)
                dk_s[sl, :] = dk
                if not reverse:
                    dq_s[sl, :] = dq
                    dv_s[sl, :] = dv
                else:
                    hqc = hq_ref[sl, :]
                    dhq_ref[sl, :] = ((dq_s[sl, :] + dq) * _dsilu(hqc, _sigmoid(hqc))).astype(BF16)
                    dhi_ref[sl, :] = (dv_s[sl, :] + dv).astype(BF16)
                return carry

            lax.fori_loop(0, nb, inside, 0, unroll=2)
            dlogf = _block_scan(dl_s[...], row, not reverse, s_)
            df = dlogf / f - dk_s[...]
            dhf_ref[...] = (df * (1.0 - lb) * sf * (1.0 - sf)).astype(BF16)
            dlb = jnp.sum(df * (1.0 - sf), axis=0, keepdims=True) * lb * (1.0 - lb)
            dlb_ref[0:1, :] += dlb
            dlb_ref[1:2, :] -= dlb

    def col(part):
        return pl.BlockSpec((s_, HEAD), lambda h, b, part=part: (b, part * h_ + h))

    tab = pl.BlockSpec((2, HEAD), lambda h, b: (0, h))
    vec = pl.BlockSpec((1, HEAD), lambda h, b: (0, h))
    blk = pl.BlockSpec((s_, HEAD), lambda h, b: (b, h))
    act = jax.ShapeDtypeStruct((b_ * s_, d_), BF16)
    state = pltpu.VMEM((nb, HEAD, HEAD), F32)
    return pl.pallas_call(
        body, name=name, grid=(h_, b_),
        in_specs=[col(0), col(1), col(2), col(3), col(4), tab, tab, vec, blk, blk],
        out_specs=[blk] * 5 + [vec, tab, tab],
        out_shape=[act] * 5 + [jax.ShapeDtypeStruct((1, d_), F32), jax.ShapeDtypeStruct((2, d_), F32),
                               jax.ShapeDtypeStruct((2, d_), F32)],
        scratch_shapes=[pltpu.VMEM((s_, HEAD), F32)] * 8 + [state] * 4 + [pltpu.VMEM((nb, 8, HEAD), F32)],
        compiler_params=_params(),
    )(proj, proj, proj, proj, proj, lbf, lbb, ng, o_sum, dya)


def _rope_tables(s_):
    half = ROPE_DIM // 2
    inv_freq = ROPE_THETA ** (-jnp.arange(0, ROPE_DIM, 2, dtype=F32) / ROPE_DIM)
    ang = jnp.arange(s_, dtype=F32)[:, None] * inv_freq
    cos, sin = jnp.cos(ang), jnp.sin(ang)
    zeros = jnp.zeros((s_, HEAD - ROPE_DIM), F32)
    zh = jnp.zeros((s_, half), F32)
    c = jnp.concatenate([cos, cos, jnp.ones((s_, HEAD - ROPE_DIM), F32)], axis=1)
    s1 = jnp.concatenate([-sin, zh, zeros], axis=1)
    s2 = jnp.concatenate([zh, sin, zeros], axis=1)
    return c, s1, s2


def _rope(t, c, s1, s2):
    half = ROPE_DIM // 2
    return t * c + pltpu.roll(t, HEAD - half, 1) * s1 + pltpu.roll(t, half, 1) * s2


def _rope_bwd(dt, c, s1, s2):
    half = ROPE_DIM // 2
    return dt * c + pltpu.roll(dt * s1, half, 1) + pltpu.roll(dt * s2, HEAD - half, 1)


def _window_mask(r0, qb, wk, seg):
    row = lax.broadcasted_iota(jnp.int32, (qb, wk), 0)
    col = lax.broadcasted_iota(jnp.int32, (qb, wk), 1)
    kj = r0 - ATTN_HALF + col
    return (col - row >= 0) & (col - row <= 2 * ATTN_HALF) & (kj >= 0) & (kj < seg)


def _attn_fwd(qkv, tabs, b_, s_, dil, name):
    seg = s_ // dil
    qb = min(128, seg)
    nq, wk = seg // qb, qb + 2 * ATTN_HALF
    scale = HEAD ** -0.5
    ncol = QKV_GROUP // HEAD

    def body(q_ref, k_ref, v_ref, c_ref, s1_ref, s2_ref, o_ref, lse_ref, q_s, k_s, v_s):
        c, s1, s2 = c_ref[...], s1_ref[...], s2_ref[...]
        q_s[...] = _rope(q_ref[...], c, s1, s2).astype(BF16)
        k_s[...] = jnp.zeros_like(k_s)
        v_s[...] = jnp.zeros_like(v_s)
        k_s[ATTN_HALF:ATTN_HALF + seg, :] = _rope(k_ref[...], c, s1, s2).astype(BF16)
        v_s[ATTN_HALF:ATTN_HALF + seg, :] = v_ref[...].astype(BF16)

        def step(i, carry):
            r0 = pl.multiple_of(i * qb, qb)
            sc = _dot(q_s[pl.ds(r0, qb), :], k_s[pl.ds(r0, wk), :], NT) * scale
            sc = jnp.where(_window_mask(r0, qb, wk, seg), sc, NEG_INF)
            m = jnp.max(sc, axis=-1, keepdims=True)
            p = jnp.exp(sc - m)
            den = jnp.sum(p, axis=-1, keepdims=True)
            o_ref[pl.ds(r0, qb), :] = _dot(p.astype(BF16), v_s[pl.ds(r0, wk), :], NN) / den
            lse_ref[pl.ds(r0, qb), :] = jnp.broadcast_to(m + jnp.log(den), (qb, HEAD))
            return carry

        lax.fori_loop(0, nq, step, 0)

    def col(part):
        return pl.BlockSpec((seg, HEAD), lambda b, r, h, part=part: (b, r * ncol + part * ATTN_HEADS + h))

    tab = pl.BlockSpec((seg, HEAD), lambda b, r, h: (0, r))
    out = pl.BlockSpec((seg, HEAD), lambda b, r, h: (b, r * ATTN_HEADS + h))
    shape = jax.ShapeDtypeStruct((b_ * seg, dil * ATTN_OUT), F32)
    return pl.pallas_call(
        body, name=name, grid=(b_, dil, ATTN_HEADS),
        in_specs=[col(0), col(1), col(2), tab, tab, tab],
        out_specs=[out, out],
        out_shape=[shape, shape],
        scratch_shapes=[pltpu.VMEM((seg, HEAD), BF16), pltpu.VMEM((seg + 2 * ATTN_HALF, HEAD), BF16),
                        pltpu.VMEM((seg + 2 * ATTN_HALF, HEAD), BF16)],
        compiler_params=_params(),
    )(qkv, qkv, qkv, *tabs)


def _attn_bwd(qkv, tabs, dog, cg, lse, b_, s_, dil, name):
    seg = s_ // dil
    qb = min(128, seg)
    nq, wk = seg // qb, qb + 2 * ATTN_HALF
    scale = HEAD ** -0.5
    ncol = QKV_GROUP // HEAD

    def body(q_ref, k_ref, v_ref, c_ref, s1_ref, s2_ref, do_ref, cg_ref, lse_ref, dq_ref, dk_ref, dv_ref,
             q_s, k_s, v_s, dk_s, dv_s):
        c, s1, s2 = c_ref[...], s1_ref[...], s2_ref[...]
        q_s[...] = _rope(q_ref[...], c, s1, s2).astype(BF16)
        k_s[...] = jnp.zeros_like(k_s)
        v_s[...] = jnp.zeros_like(v_s)
        k_s[ATTN_HALF:ATTN_HALF + seg, :] = _rope(k_ref[...], c, s1, s2).astype(BF16)
        v_s[ATTN_HALF:ATTN_HALF + seg, :] = v_ref[...].astype(BF16)
        dk_s[...] = jnp.zeros_like(dk_s)
        dv_s[...] = jnp.zeros_like(dv_s)

        def step(i, carry):
            r0 = pl.multiple_of(i * qb, qb)
            rows, win = pl.ds(r0, qb), pl.ds(r0, wk)
            qc, kw, vw = q_s[rows, :], k_s[win, :], v_s[win, :]
            sc = _dot(qc, kw, NT) * scale
            p = jnp.where(_window_mask(r0, qb, wk, seg), jnp.exp(sc - lse_ref[rows, 0:1]), 0.0)
            dob = do_ref[rows, :].astype(BF16)
            dp = _dot(dob, vw, NT)
            ds = (p * (dp + cg_ref[rows, 0:1]) * scale).astype(BF16)
            dq = _dot(ds, kw, NN)
            dq_ref[rows, :] = _rope_bwd(dq, c_ref[rows, :], s1_ref[rows, :], s2_ref[rows, :])
            dk_s[win, :] += _dot(ds, qc, TN)
            dv_s[win, :] += _dot(p.astype(BF16), dob, TN)
            return carry

        lax.fori_loop(0, nq, step, 0)
        dk_ref[...] = _rope_bwd(dk_s[ATTN_HALF:ATTN_HALF + seg, :], c, s1, s2)
        dv_ref[...] = dv_s[ATTN_HALF:ATTN_HALF + seg, :]

    def col(part):
        return pl.BlockSpec((seg, HEAD), lambda b, r, h, part=part: (b, r * ncol + part * ATTN_HEADS + h))

    tab = pl.BlockSpec((seg, HEAD), lambda b, r, h: (0, r))
    out = pl.BlockSpec((seg, HEAD), lambda b, r, h: (b, r * ATTN_HEADS + h))
    shape = jax.ShapeDtypeStruct((b_ * seg, dil * ATTN_OUT), F32)
    return pl.pallas_call(
        body, name=name, grid=(b_, dil, ATTN_HEADS),
        in_specs=[col(0), col(1), col(2), tab, tab, tab, out, out, out],
        out_specs=[out, out, out],
        out_shape=[shape, shape, shape],
        scratch_shapes=[pltpu.VMEM((seg, HEAD), BF16), pltpu.VMEM((seg + 2 * ATTN_HALF, HEAD), BF16),
                        pltpu.VMEM((seg + 2 * ATTN_HALF, HEAD), BF16),
                        pltpu.VMEM((seg + 2 * ATTN_HALF, HEAD), F32), pltpu.VMEM((seg + 2 * ATTN_HALF, HEAD), F32)],
        compiler_params=_params(),
    )(qkv, qkv, qkv, *tabs, dog, cg, lse)


def _group_weights(lses):
    m = jnp.maximum(jnp.maximum(lses[0], lses[1]), lses[2])
    es = [jnp.exp(l - m) for l in lses]
    den = es[0] + es[1] + es[2]
    return [e / den for e in es]


def _combine_fwd(outs, lses, name):
    t_, w_ = outs[0].shape
    tm = _tile(t_, 512, 8)
    ng = len(outs)

    def body(*refs):
        ws = _group_weights([r[...] for r in refs[ng:2 * ng]])
        acc = ws[0] * refs[0][...]
        for g in range(1, ng):
            acc = acc + ws[g] * refs[g][...]
        refs[2 * ng][...] = acc.astype(BF16)

    row = pl.BlockSpec((tm, w_), lambda i: (i, 0))
    return pl.pallas_call(
        body, name=name, grid=(t_ // tm,), in_specs=[row] * (2 * ng), out_specs=row,
        out_shape=jax.ShapeDtypeStruct((t_, w_), BF16), compiler_params=_params(),
    )(*outs, *lses)


def _combine_bwd(dob, outs, lses, name):
    t_, w_ = outs[0].shape
    tm = _tile(t_, 512, 8)
    ng = len(outs)

    def body(*refs):
        do = refs[0][...]
        os_ = [r[...] for r in refs[1:1 + ng]]
        ws = _group_weights([r[...] for r in refs[1 + ng:1 + 2 * ng]])
        o = ws[0] * os_[0]
        for g in range(1, ng):
            o = o + ws[g] * os_[g]
        prod = do * o
        heads = [jnp.broadcast_to(jnp.sum(prod[:, h * HEAD:(h + 1) * HEAD], axis=-1, keepdims=True), (tm, HEAD))
                 for h in range(w_ // HEAD)]
        tot = jnp.concatenate(heads, axis=1)
        for g in range(ng):
            refs[1 + 2 * ng + g][...] = ws[g] * do
            refs[1 + 3 * ng + g][...] = -ws[g] * tot

    row = pl.BlockSpec((tm, w_), lambda i: (i, 0))
    shape = jax.ShapeDtypeStruct((t_, w_), F32)
    res = pl.pallas_call(
        body, name=name, grid=(t_ // tm,), in_specs=[row] * (1 + 2 * ng), out_specs=[row] * (2 * ng),
        out_shape=[shape] * (2 * ng), compiler_params=_params(),
    )(dob, *outs, *lses)
    return res[:ng], res[ng:]


def _adam_update(w, g, m, v):
    m = ADAM_B1 * m + (1.0 - ADAM_B1) * g
    v = ADAM_B2 * v + (1.0 - ADAM_B2) * (g * g)
    m_hat = m / (1.0 - ADAM_B1 ** ADAM_STEP)
    v_hat = v / (1.0 - ADAM_B2 ** ADAM_STEP)
    return -ADAM_LR * (m_hat / (jnp.sqrt(v_hat) + ADAM_EPS) + ADAM_WD * w), m, v


def _adam(w, g, m, v, name):
    r_, c_ = w.shape
    tr = _tile(r_, 256, 8)

    def body(w_ref, g_ref, m_ref, v_ref, d_ref, mo_ref, vo_ref):
        d_ref[...], mo_ref[...], vo_ref[...] = _adam_update(w_ref[...], g_ref[...], m_ref[...], v_ref[...])

    blk = pl.BlockSpec((tr, c_), lambda i: (i, 0))
    shape = jax.ShapeDtypeStruct((r_, c_), F32)
    return pl.pallas_call(
        body, name=name, grid=(r_ // tr,), in_specs=[blk] * 4, out_specs=[blk] * 3,
        out_shape=[shape] * 3, compiler_params=_params(),
    )(w, g, m, v)


def _sum_partials(recv, name):
    n_, r_, c_ = recv.shape
    tr = _tile(r_, 128, 16)

    def body(p_ref, o_ref):
        acc = p_ref[0].astype(F32)
        for i in range(1, n_):
            acc = acc + p_ref[i].astype(F32)
        o_ref[...] = acc

    return pl.pallas_call(
        body, name=name, grid=(r_ // tr,),
        in_specs=[pl.BlockSpec((n_, tr, c_), lambda i: (0, i, 0))],
        out_specs=pl.BlockSpec((tr, c_), lambda i: (i, 0)),
        out_shape=jax.ShapeDtypeStruct((r_, c_), F32), compiler_params=_params(),
    )(recv)


def _small_sum_adam(parts, w, m, v, name):
    n_, r_, c_ = parts.shape

    def body(p_ref, w_ref, m_ref, v_ref, g_ref, d_ref, mo_ref, vo_ref):
        g = p_ref[0]
        for i in range(1, n_):
            g = g + p_ref[i]
        g_ref[...] = g
        d_ref[...], mo_ref[...], vo_ref[...] = _adam_update(w_ref[...], g, m_ref[...], v_ref[...])

    shape = jax.ShapeDtypeStruct((r_, c_), F32)
    return pl.pallas_call(body, name=name, out_shape=[shape] * 4, compiler_params=_params())(parts, w, m, v)


def _my_place():
    x, y, c = lax.axis_index("x"), lax.axis_index("y"), lax.axis_index("c")
    return x, y, c


def _peer(x, y, c, d):
    px = 1 - x if d & 4 else x
    py = 1 - y if d & 2 else y
    pc = 1 - c if d & 1 else c
    return (px, py, pc), 4 * px + 2 * py + pc


def _all_gather(shards, name):
    nw = len(shards)

    def body(*refs):
        ins, outs = refs[:nw], refs[nw:2 * nw]
        send_sems, recv_sems, local_sems = refs[2 * nw:]
        x, y, c = _my_place()
        me = 4 * x + 2 * y + c
        copies = []
        for k in range(nw):
            rows = shards[k].shape[0]
            mine = outs[k].at[pl.ds(pl.multiple_of(me * rows, 16), rows), :]
            local = pltpu.make_async_copy(ins[k], mine, local_sems.at[k])
            local.start()
            copies.append(local)
            for d in range(1, N_DEV):
                place, _ = _peer(x, y, c, d)
                remote = pltpu.make_async_remote_copy(
                    src_ref=ins[k], dst_ref=mine, send_sem=send_sems.at[d - 1, k], recv_sem=recv_sems.at[d - 1, k],
                    device_id=place, device_id_type=MESH)
                remote.start()
                copies.append(remote)
        for cp in copies:
            cp.wait()

    hbm = pl.BlockSpec(memory_space=pl.ANY)
    return pl.pallas_call(
        body, name=name, in_specs=[hbm] * nw, out_specs=[hbm] * nw,
        out_shape=[jax.ShapeDtypeStruct((N_DEV * s.shape[0], s.shape[1]), s.dtype) for s in shards],
        scratch_shapes=[pltpu.SemaphoreType.DMA((N_DEV - 1, nw)), pltpu.SemaphoreType.DMA((N_DEV - 1, nw)),
                        pltpu.SemaphoreType.DMA((nw,))],
    )(*shards)


HBM_SPEC = pl.BlockSpec(memory_space=pltpu.HBM)
SEM_SPEC = pl.BlockSpec(memory_space=pltpu.SEMAPHORE)
EFFECT = pltpu.SideEffectType.DATAFLOW_SIDE_EFFECTING


def _in_hbm(a):
    return pltpu.with_memory_space_constraint(a, pltpu.HBM)


def _token_shape():
    return jax.ShapeDtypeStruct((8, HEAD), F32)


def _gather_start(shards, name):
    nw = len(shards)
    lands = [lax.empty((N_DEV * s.shape[0], s.shape[1]), s.dtype) for s in shards]

    def body(*refs):
        ins, lnd = refs[:nw], refs[nw:2 * nw]
        send, recv, own = refs[2 * nw:3 * nw], refs[3 * nw:4 * nw], refs[4 * nw:5 * nw]
        token = refs[7 * nw]
        x, y, c = _my_place()
        me = 4 * x + 2 * y + c
        for k in range(nw):
            rows = shards[k].shape[0]
            mine = lnd[k].at[pl.ds(pl.multiple_of(me * rows, 16), rows), :]
            pltpu.make_async_copy(ins[k], mine, own[k]).start()
            for d in range(1, N_DEV):
                place, _ = _peer(x, y, c, d)
                pltpu.make_async_remote_copy(
                    src_ref=ins[k], dst_ref=mine, send_sem=send[k].at[d - 1], recv_sem=recv[k].at[d - 1],
                    device_id=place, device_id_type=MESH).start()
        token[...] = jnp.zeros_like(token)

    sems = [pltpu.SemaphoreType.DMA((N_DEV - 1,))] * (2 * nw) + [pltpu.SemaphoreType.DMA(())] * nw
    thru = [pltpu.HBM(a.shape, a.dtype) for a in list(shards) + lands]
    res = pl.pallas_call(
        body, name=name, out_shape=(*sems, *thru, _token_shape()),
        in_specs=[HBM_SPEC] * (2 * nw),
        out_specs=(*([SEM_SPEC] * (3 * nw)), *([HBM_SPEC] * (2 * nw)), pl.BlockSpec(memory_space=pltpu.VMEM)),
        input_output_aliases={i: 3 * nw + i for i in range(2 * nw)},
        compiler_params=pltpu.CompilerParams(has_side_effects=EFFECT),
    )(*[_in_hbm(s) for s in shards], *[_in_hbm(l) for l in lands])
    return [dict(send=res[k], recv=res[nw + k], own=res[2 * nw + k], src=res[3 * nw + k], land=res[4 * nw + k])
            for k in range(nw)]


def _gather_wait(pending, after, name):
    rows = pending["src"].shape[0]

    def body(src_ref, land_ref, send, recv, own, after_ref, src_dead, got):
        x, y, c = _my_place()
        me = 4 * x + 2 * y + c
        mine = land_ref.at[pl.ds(pl.multiple_of(me * rows, 16), rows), :]
        pltpu.make_async_copy(src_ref, mine, own).wait()
        for d in range(1, N_DEV):
            place, _ = _peer(x, y, c, d)
            cp = pltpu.make_async_remote_copy(
                src_ref=src_ref, dst_ref=mine, send_sem=send.at[d - 1], recv_sem=recv.at[d - 1],
                device_id=place, device_id_type=MESH)
            cp.wait_send()
            cp.wait_recv()

    src, land = pending["src"], pending["land"]
    return pl.pallas_call(
        body, name=name, out_shape=(pltpu.HBM(src.shape, src.dtype), pltpu.HBM(land.shape, land.dtype)),
        in_specs=(HBM_SPEC, HBM_SPEC, SEM_SPEC, SEM_SPEC, SEM_SPEC, pl.BlockSpec(memory_space=pl.ANY)),
        out_specs=(HBM_SPEC, HBM_SPEC), input_output_aliases={0: 0, 1: 1},
        compiler_params=pltpu.CompilerParams(has_side_effects=EFFECT),
    )(src, land, pending["send"], pending["recv"], pending["own"], after)[1]


def _scatter_start(full, name):
    rows, cols = full.shape[0] // N_DEV, full.shape[1]
    land = lax.empty((N_DEV, rows, cols), full.dtype)

    def body(full_ref, land_ref, send, recv, own, full_thru, land_thru, token):
        x, y, c = _my_place()
        me = 4 * x + 2 * y + c
        slab = land_ref.at[me]
        pltpu.make_async_copy(full_ref.at[pl.ds(pl.multiple_of(me * rows, 16), rows), :], slab, own).start()
        for d in range(1, N_DEV):
            place, num = _peer(x, y, c, d)
            pltpu.make_async_remote_copy(
                src_ref=full_ref.at[pl.ds(pl.multiple_of(num * rows, 16), rows), :], dst_ref=slab,
                send_sem=send.at[d - 1], recv_sem=recv.at[d - 1], device_id=place, device_id_type=MESH).start()
        token[...] = jnp.zeros_like(token)

    res = pl.pallas_call(
        body, name=name,
        out_shape=(pltpu.SemaphoreType.DMA((N_DEV - 1,)), pltpu.SemaphoreType.DMA((N_DEV - 1,)),
                   pltpu.SemaphoreType.DMA(()),
                   pltpu.HBM(full.shape, full.dtype), pltpu.HBM(land.shape, land.dtype), _token_shape()),
        in_specs=(HBM_SPEC, HBM_SPEC),
        out_specs=(SEM_SPEC, SEM_SPEC, SEM_SPEC, HBM_SPEC, HBM_SPEC, pl.BlockSpec(memory_space=pltpu.VMEM)),
        input_output_aliases={0: 3, 1: 4},
        compiler_params=pltpu.CompilerParams(has_side_effects=EFFECT),
    )(_in_hbm(full), _in_hbm(land))
    return dict(send=res[0], recv=res[1], own=res[2], src=res[3], land=res[4]), res[5]


def _scatter_wait(pending, after, name):
    rows = pending["land"].shape[1]

    def body(src_ref, land_ref, send, recv, own, after_ref, src_dead, got):
        x, y, c = _my_place()
        me = 4 * x + 2 * y + c
        pltpu.make_async_copy(src_ref.at[pl.ds(pl.multiple_of(me * rows, 16), rows), :], land_ref.at[me], own).wait()
        for d in range(1, N_DEV):
            place, num = _peer(x, y, c, d)
            cp = pltpu.make_async_remote_copy(
                src_ref=src_ref.at[pl.ds(pl.multiple_of(num * rows, 16), rows), :], dst_ref=land_ref.at[me],
                send_sem=send.at[d - 1], recv_sem=recv.at[d - 1], device_id=place, device_id_type=MESH)
            cp.wait_send()
            cp.wait_recv()

    src, land = pending["src"], pending["land"]
    return pl.pallas_call(
        body, name=name, out_shape=(pltpu.HBM(src.shape, src.dtype), pltpu.HBM(land.shape, land.dtype)),
        in_specs=(HBM_SPEC, HBM_SPEC, SEM_SPEC, SEM_SPEC, SEM_SPEC, pl.BlockSpec(memory_space=pl.ANY)),
        out_specs=(HBM_SPEC, HBM_SPEC), input_output_aliases={0: 0, 1: 1},
        compiler_params=pltpu.CompilerParams(has_side_effects=EFFECT),
    )(src, land, pending["send"], pending["recv"], pending["own"], after)[1]


BIG = ("ffn1_w_in", "ffn1_w_out", "mix_w_in", "w_branch_a", "w_branch_b", "mix_w_out", "ffn2_w_in", "ffn2_w_out")
TRANSPOSED = ("ffn1_w_in", "mix_w_in", "w_branch_b", "ffn2_w_in")
SMALL = ("ln1_g", "ln1_b", "ln2_g", "ln2_b", "ln3_g", "ln3_b", "hgrn_norm_g", "hgrn_lb_fwd", "hgrn_lb_bwd")
SMALL_ROWS = 16


def _local_step(x, target, weight, emit, sp):
    b_, s_, d_ = x.shape
    t_ = b_ * s_
    x2, tgt = x.reshape(t_, d_), target.reshape(t_, d_)
    xb = x2.astype(BF16)
    w1i = weight("ffn1_w_in", xb)
    g1, u1, a1 = _ffn_in_fwd(xb, w1i, "ffn1_in")
    w1o = weight("ffn1_w_out", a1)
    r1, h1, h1b = _mm_res_ln_fwd(a1, w1o, x2, sp["ln1_g"], sp["ln1_b"], 0.5, "ffn1_out_ln1")
    wmx = weight("mix_w_in", h1b)
    proj = _mm_nt(h1b, wmx, F32, "mix_in")
    ya_in, o_sum = _hgrn_fwd(proj, sp["hgrn_lb_fwd"], sp["hgrn_lb_bwd"], sp["hgrn_norm_g"], b_, s_, d_, "hgrn_fwd")
    tabs = _rope_tables(s_)
    qkvs, gtabs, outs, lses = [], [], [], []
    for gi, (_, dil) in enumerate(ATTN_GROUPS):
        off = 5 * d_ + gi * QKV_GROUP
        qkv = proj[:, off:off + QKV_GROUP].reshape(t_ // dil, dil * QKV_GROUP)
        gt = [tb.reshape(s_ // dil, dil * HEAD) for tb in tabs]
        o_g, lse_g = _attn_fwd(qkv, gt, b_, s_, dil, f"attn_fwd_{gi}")
        qkvs.append(qkv)
        gtabs.append(gt)
        outs.append(o_g.reshape(t_, ATTN_OUT))
        lses.append(lse_g.reshape(t_, ATTN_OUT))
    ob = _combine_fwd(outs, lses, "attn_combine")
    wa, wb = weight("w_branch_a", ya_in), weight("w_branch_b", ob)
    ya, yb, z = _gate_out_fwd(ya_in, ob, wa, wb, proj, d_, "branch_gate")
    wo = weight("mix_w_out", z)
    r2, h2, h2b = _mm_res_ln_fwd(z, wo, h1, sp["ln2_g"], sp["ln2_b"], 1.0, "mix_out_ln2")
    w2i = weight("ffn2_w_in", h2b)
    g2, u2, a2 = _ffn_in_fwd(h2b, w2i, "ffn2_in")
    w2o = weight("ffn2_w_out", a2)
    r3, _, _ = _mm_res_ln_fwd(a2, w2o, h2, sp["ln3_g"], sp["ln3_b"], 0.5, "ffn2_out_ln3")
    dr3, dr3b, dg3, db3, loss = _ln_loss_bwd(r3, tgt, sp["ln3_g"], sp["ln3_b"], "loss_ln3_bwd")
    dep = emit("ffn2_w_out", _mm_tn(a2, dr3b, 0.5, "d_ffn2_w_out"))
    dgate2, dup2 = _ffn_mid_bwd(dr3b, w2o, g2, u2, 0.5, "ffn2_mid_bwd", dep)
    du2 = jnp.concatenate([dgate2, dup2], axis=1)
    dep = emit("ffn2_w_in", _mm_tn(du2, h2b, 1.0, "d_ffn2_w_in"))
    dr2, dr2b, dg2, db2 = _mm_nn_res_lnbwd(du2, w2i, dr3, r2, sp["ln2_g"], "ffn2_in_bwd_ln2", dep)
    dep = emit("mix_w_out", _mm_tn(z, dr2b, 1.0, "d_mix_w_out"))
    dya, dyb, dga, dgb = _dz_gate_bwd(dr2b, wo, proj, ya, yb, d_, "branch_gate_bwd", dep)
    dep = emit("w_branch_a", _mm_tn(ya_in, dya, 1.0, "d_w_branch_a"))
    dya_in = _mm_nt(dya, wa, F32, "branch_a_bwd", dep)
    dep = emit("w_branch_b", _mm_tn(dyb, ob, 1.0, "d_w_branch_b"))
    dob = _mm_nn(dyb, wb, F32, "branch_b_bwd", dep)
    dhq, dhff, dhfb, dhi, dhog, dng, dlbf, dlbb = _hgrn_bwd(
        proj, sp["hgrn_lb_fwd"], sp["hgrn_lb_bwd"], sp["hgrn_norm_g"], o_sum, dya_in, b_, s_, d_, "hgrn_bwd")
    dogs, cgs = _combine_bwd(dob, outs, lses, "attn_combine_bwd")
    dqkv = []
    for gi, (_, dil) in enumerate(ATTN_GROUPS):
        seg_rows = t_ // dil
        dq, dk, dv = _attn_bwd(qkvs[gi], gtabs[gi], dogs[gi].reshape(seg_rows, dil * ATTN_OUT),
                               cgs[gi].reshape(seg_rows, dil * ATTN_OUT), lses[gi].reshape(seg_rows, dil * ATTN_OUT),
                               b_, s_, dil, f"attn_bwd_{gi}")
        dqkv += [t.reshape(t_, ATTN_OUT).astype(BF16) for t in (dq, dk, dv)]
    dproj = jnp.concatenate([dhq, dhff, dhfb, dhi, dhog] + dqkv + [dga, dgb], axis=1)
    dep = emit("mix_w_in", _mm_tn(dproj, h1b, 1.0, "d_mix_w_in"))
    dr1, dr1b, dg1, db1 = _mm_nn_res_lnbwd(dproj, wmx, dr2, r1, sp["ln1_g"], "mix_in_bwd_ln1", dep)
    dep = emit("ffn1_w_out", _mm_tn(a1, dr1b, 0.5, "d_ffn1_w_out"))
    dgate1, dup1 = _ffn_mid_bwd(dr1b, w1o, g1, u1, 0.5, "ffn1_mid_bwd", dep)
    du1 = jnp.concatenate([dgate1, dup1], axis=1)
    dep = emit("ffn1_w_in", _mm_tn(du1, xb, 1.0, "d_ffn1_w_in"))
    grad_x = _mm_nn_res(du1, w1i, dr1, "ffn1_in_bwd", dep)
    small = {"ln1_g": dg1, "ln1_b": db1, "ln2_g": dg2, "ln2_b": db2, "ln3_g": dg3, "ln3_b": db3,
             "hgrn_norm_g": dng, "hgrn_lb_fwd": dlbf, "hgrn_lb_bwd": dlbb}
    return loss, grad_x.reshape(b_, s_, d_), small


def _pack_small(vals):
    rows = jnp.concatenate([vals[n] for n in SMALL], axis=0)
    return jnp.pad(rows, ((0, SMALL_ROWS - rows.shape[0]), (0, 0)))


def _unpack_small(packed):
    out, r = {}, 0
    for n in SMALL:
        k = 2 if n.startswith("hgrn_lb") else 1
        out[n] = packed[r:r + k]
        r += k
    return out


def kernel(x, ffn1_w_in, ffn1_w_out, ln1_g, ln1_b, mix_w_in, hgrn_lb_fwd, hgrn_lb_bwd, hgrn_norm_g, w_branch_a, w_branch_b, mix_w_out, ln2_g, ln2_b, ffn2_w_in, ffn2_w_out, ln3_g, ln3_b, loss_target, m_ffn1_w_in, m_ffn1_w_out, m_ln1_g, m_ln1_b, m_mix_w_in, m_hgrn_lb_fwd, m_hgrn_lb_bwd, m_hgrn_norm_g, m_w_branch_a, m_w_branch_b, m_mix_w_out, m_ln2_g, m_ln2_b, m_ffn2_w_in, m_ffn2_w_out, m_ln3_g, m_ln3_b, v_ffn1_w_in, v_ffn1_w_out, v_ln1_g, v_ln1_b, v_mix_w_in, v_hgrn_lb_fwd, v_hgrn_lb_bwd, v_hgrn_norm_g, v_w_branch_a, v_w_branch_b, v_mix_w_out, v_ln2_g, v_ln2_b, v_ffn2_w_in, v_ffn2_w_out, v_ln3_g, v_ln3_b):
    args = dict(locals())
    big_w = {n: args[n][0] for n in BIG}
    sp = {n: args[n] for n in SMALL}
    shards = [(big_w[n].T if n in TRANSPOSED else big_w[n]).astype(BF16) for n in BIG]
    gathering = dict(zip(BIG, _gather_start(shards, "gather_start")))
    scattering = {}

    def weight(n, after):
        return _gather_wait(gathering[n], after, f"gather_wait_{n}")

    def emit(n, grad):
        scattering[n], token = _scatter_start(grad, f"scatter_start_{n}")
        return token

    loss_part, grad_x, small = _local_step(x, loss_target, weight, emit, sp)
    loss = lax.psum(loss_part[0, 0], ("x", "y", "c"))
    out_g, out_d, out_m, out_v = {}, {}, {}, {}
    for n in BIG:
        g = _sum_partials(_scatter_wait(scattering[n], grad_x, f"scatter_wait_{n}"), f"sum_{n}")
        if n in TRANSPOSED:
            g = g.T
        d_w, m_w, v_w = _adam(big_w[n], g, args["m_" + n][0], args["v_" + n][0], f"adam_{n}")
        out_g[n], out_d[n], out_m[n], out_v[n] = g[None], d_w[None], m_w[None], v_w[None]
    (parts,) = _all_gather([_pack_small(small)], "gather_small_grads")
    res = _small_sum_adam(parts.reshape(N_DEV, SMALL_ROWS, parts.shape[1]), _pack_small(sp),
                          _pack_small({n: args["m_" + n] for n in SMALL}),
                          _pack_small({n: args["v_" + n] for n in SMALL}), "small_adam")
    sg, sd, sm, sv = (_unpack_small(r) for r in res)
    out_g.update(sg), out_d.update(sd), out_m.update(sm), out_v.update(sv)
    order = ("ffn1_w_in", "ffn1_w_out", "ln1_g", "ln1_b", "mix_w_in", "hgrn_lb_fwd", "hgrn_lb_bwd", "hgrn_norm_g",
             "w_branch_a", "w_branch_b", "mix_w_out", "ln2_g", "ln2_b", "ffn2_w_in", "ffn2_w_out", "ln3_g", "ln3_b")
    return (loss, grad_x, *[out_g[n] for n in order], *[out_d[n] for n in order],
            *[out_m[n] for n in order], *[out_v[n] for n in order])
```

```python
import functools

import jax
import jax.numpy as jnp
from jax import lax
from jax.experimental import pallas as pl
from jax.experimental.pallas import tpu as pltpu

F32 = jnp.float32
BF16 = jnp.bfloat16

N_DEV = 8
HEAD = 128
CHUNK = 32
ATTN_GROUPS = ((128, 1), (512, 4), (2048, 16))
ATTN_HEADS = 4
ATTN_HALF = 64
QKV_GROUP = 3 * ATTN_HEADS * HEAD
QKV_WIDTH = len(ATTN_GROUPS) * QKV_GROUP
ATTN_OUT = ATTN_HEADS * HEAD
ROPE_THETA = 500000.0
ROPE_DIM = HEAD // 4
ALPHA = 2.0 ** 0.25
LN_EPS = 1e-5
NEG_INF = -1e30
ADAM_LR, ADAM_B1, ADAM_B2, ADAM_EPS, ADAM_WD, ADAM_STEP = 0.001, 0.9, 0.999, 1e-08, 0.01, 10
VMEM_LIMIT = 56 * 1024 * 1024

NT = (((1,), (1,)), ((), ()))
NN = (((1,), (0,)), ((), ()))
TN = (((0,), (0,)), ((), ()))
MESH = pl.DeviceIdType.MESH


def _dot(a, b, dims):
    return lax.dot_general(a, b, dims, preferred_element_type=F32)


def _tile(n, pref, mult=128):
    if n <= pref:
        return n
    t = (pref // mult) * mult
    while t >= mult:
        if n % t == 0:
            return t
        t -= mult
    return n


def _tile_multi(ns, pref, mult=128):
    t = (pref // mult) * mult
    while t >= mult:
        if all(n % t == 0 for n in ns):
            return t
        t -= mult
    raise ValueError(f"no common tile for {ns}")


def _params(**kw):
    return pltpu.CompilerParams(vmem_limit_bytes=VMEM_LIMIT, **kw)


def _after(body, n_in, dep):
    if dep is None:
        return body, [], []

    def wrapped(*refs):
        body(*refs[:n_in], *refs[n_in + 1:])

    return wrapped, [pl.BlockSpec(dep.shape, lambda *_: (0,) * dep.ndim)], [dep]


def _sigmoid(x):
    return jax.nn.sigmoid(x)


def _dsilu(x, s):
    return s * (1.0 + x * (1.0 - s))


def _ln_stats(r):
    mu = jnp.mean(r, axis=-1, keepdims=True)
    xc = r - mu
    var = jnp.mean(xc * xc, axis=-1, keepdims=True)
    rstd = lax.rsqrt(var + LN_EPS)
    return xc * rstd, rstd


def _ln_bwd(dy, xhat, rstd, g):
    dyg = dy * g
    m1 = jnp.mean(dyg, axis=-1, keepdims=True)
    m2 = jnp.mean(dyg * xhat, axis=-1, keepdims=True)
    return rstd * (dyg - m1 - xhat * m2)


ROW_TILE = 1024
SUB_ROWS = 256


def _once(shape, index_map):
    return pl.BlockSpec(shape, index_map, pipeline_mode=pl.Buffered(1))


def _for_row_blocks(tm, fn):
    sub = SUB_ROWS if tm % SUB_ROWS == 0 else tm

    def step(s, carry):
        fn(pl.ds(pl.multiple_of(s * sub, sub), sub))
        return carry

    lax.fori_loop(0, tm // sub, step, 0)


def _ffn_in_fwd(xb, w_t, name):
    t_, d_ = xb.shape
    f_ = w_t.shape[0] // 2
    tm, tn = _tile(t_, ROW_TILE, 8), _tile(f_, 512)
    nj = f_ // tn

    def body(x_ref, wg_ref, wu_ref, g_ref, u_ref, a_ref):
        x = x_ref[...]
        g = _dot(x, wg_ref[...], NT)
        u = _dot(x, wu_ref[...], NT)
        g_ref[...] = g
        u_ref[...] = u
        a_ref[...] = (g * _sigmoid(g) * u).astype(BF16)

    return pl.pallas_call(
        body, name=name, grid=(t_ // tm, nj),
        in_specs=[pl.BlockSpec((tm, d_), lambda i, j: (i, 0)),
                  pl.BlockSpec((tn, d_), lambda i, j: (j, 0)),
                  pl.BlockSpec((tn, d_), lambda i, j: (j + nj, 0))],
        out_specs=[pl.BlockSpec((tm, tn), lambda i, j: (i, j))] * 3,
        out_shape=[jax.ShapeDtypeStruct((t_, f_), F32), jax.ShapeDtypeStruct((t_, f_), F32),
                   jax.ShapeDtypeStruct((t_, f_), BF16)],
        compiler_params=_params(),
    )(xb, w_t, w_t)


def _mm_res_ln_fwd(a, w, res, g, b, scale, name):
    t_, k_ = a.shape
    d_ = w.shape[1]
    tm, tk = _tile(t_, ROW_TILE, 8), _tile(k_, 512)
    nk = k_ // tk

    def body(a_ref, w_ref, res_ref, g_ref, b_ref, r_ref, h_ref, hb_ref, acc):
        k = pl.program_id(1)

        @pl.when(k == 0)
        def _():
            acc[...] = jnp.zeros_like(acc)

        acc[...] += _dot(a_ref[...], w_ref[...], NN)

        @pl.when(k == nk - 1)
        def _():
            def rows_out(rows):
                r = ALPHA * res_ref[rows, :] + scale * acc[rows, :]
                xhat, _ = _ln_stats(r)
                h = xhat * g_ref[...] + b_ref[...]
                r_ref[rows, :] = r
                h_ref[rows, :] = h
                hb_ref[rows, :] = h.astype(BF16)

            _for_row_blocks(tm, rows_out)

    row = _once((tm, d_), lambda i, k: (i, 0))
    vec = pl.BlockSpec((1, d_), lambda i, k: (0, 0))
    return pl.pallas_call(
        body, name=name, grid=(t_ // tm, nk),
        in_specs=[pl.BlockSpec((tm, tk), lambda i, k: (i, k)),
                  pl.BlockSpec((tk, d_), lambda i, k: (k, 0)), row, vec, vec],
        out_specs=[row, row, row],
        out_shape=[jax.ShapeDtypeStruct((t_, d_), F32), jax.ShapeDtypeStruct((t_, d_), F32),
                   jax.ShapeDtypeStruct((t_, d_), BF16)],
        scratch_shapes=[pltpu.VMEM((tm, d_), F32)],
        compiler_params=_params(),
    )(a, w, res, g, b)


def _mm_nt(a, w_t, out_dtype, name, dep=None):
    t_, k_ = a.shape
    n_ = w_t.shape[0]
    tm, tn = _tile(t_, ROW_TILE, 8), _tile(n_, 512)

    def body(a_ref, w_ref, o_ref):
        o_ref[...] = _dot(a_ref[...], w_ref[...], NT).astype(out_dtype)

    body, dep_specs, deps = _after(body, 2, dep)
    return pl.pallas_call(
        body, name=name, grid=(t_ // tm, n_ // tn),
        in_specs=[pl.BlockSpec((tm, k_), lambda i, j: (i, 0)),
                  pl.BlockSpec((tn, k_), lambda i, j: (j, 0)), *dep_specs],
        out_specs=pl.BlockSpec((tm, tn), lambda i, j: (i, j)),
        out_shape=jax.ShapeDtypeStruct((t_, n_), out_dtype),
        compiler_params=_params(),
    )(a, w_t, *deps)


def _mm_nn(a, w, out_dtype, name, dep=None):
    t_, k_ = a.shape
    n_ = w.shape[1]
    tm, tn = _tile(t_, ROW_TILE, 8), _tile(n_, 512)

    def body(a_ref, w_ref, o_ref):
        o_ref[...] = _dot(a_ref[...], w_ref[...], NN).astype(out_dtype)

    body, dep_specs, deps = _after(body, 2, dep)
    return pl.pallas_call(
        body, name=name, grid=(t_ // tm, n_ // tn),
        in_specs=[pl.BlockSpec((tm, k_), lambda i, j: (i, 0)),
                  pl.BlockSpec((k_, tn), lambda i, j: (0, j)), *dep_specs],
        out_specs=pl.BlockSpec((tm, tn), lambda i, j: (i, j)),
        out_shape=jax.ShapeDtypeStruct((t_, n_), out_dtype),
        compiler_params=_params(),
    )(a, w, *deps)


def _mm_tn(a, b, scale, name):
    t_, m_ = a.shape
    n_ = b.shape[1]
    tm = _tile(m_, 512)

    def body(a_ref, b_ref, o_ref):
        o_ref[...] = (scale * _dot(a_ref[...], b_ref[...], TN)).astype(BF16)

    return pl.pallas_call(
        body, name=name, grid=(m_ // tm,),
        in_specs=[pl.BlockSpec((t_, tm), lambda i: (0, i)), _once((t_, n_), lambda i: (0, 0))],
        out_specs=pl.BlockSpec((tm, n_), lambda i: (i, 0)),
        out_shape=jax.ShapeDtypeStruct((m_, n_), BF16),
        compiler_params=_params(),
    )(a, b)


def _gate_out_fwd(ya_in, ob, wa, wb_t, proj, d_, name):
    t_ = ya_in.shape[0]
    goff = 5 * d_ + QKV_WIDTH
    tm, tn = _tile(t_, ROW_TILE, 8), _tile_multi([d_, goff], 512)
    ja, jb = goff // tn, (goff + d_) // tn

    def body(ya_ref, ob_ref, wa_ref, wb_ref, ga_ref, gb_ref, yao_ref, ybo_ref, z_ref):
        y_a = _dot(ya_ref[...], wa_ref[...], NN)
        y_b = _dot(ob_ref[...], wb_ref[...], NT)
        yao_ref[...] = y_a
        ybo_ref[...] = y_b
        z_ref[...] = (_sigmoid(ga_ref[...]) * y_a + _sigmoid(gb_ref[...]) * y_b).astype(BF16)

    tile = pl.BlockSpec((tm, tn), lambda i, j: (i, j))
    return pl.pallas_call(
        body, name=name, grid=(t_ // tm, d_ // tn),
        in_specs=[pl.BlockSpec((tm, d_), lambda i, j: (i, 0)),
                  pl.BlockSpec((tm, ATTN_OUT), lambda i, j: (i, 0)),
                  pl.BlockSpec((d_, tn), lambda i, j: (0, j)),
                  pl.BlockSpec((tn, ATTN_OUT), lambda i, j: (j, 0)),
                  pl.BlockSpec((tm, tn), lambda i, j: (i, ja + j)),
                  pl.BlockSpec((tm, tn), lambda i, j: (i, jb + j))],
        out_specs=[tile, tile, tile],
        out_shape=[jax.ShapeDtypeStruct((t_, d_), F32), jax.ShapeDtypeStruct((t_, d_), F32),
                   jax.ShapeDtypeStruct((t_, d_), BF16)],
        compiler_params=_params(),
    )(ya_in, ob, wa, wb_t, proj, proj)


def _ffn_mid_bwd(drb, w_out, gate, up, scale, name, dep=None):
    t_, d_ = drb.shape
    f_ = w_out.shape[0]
    tm, tn = _tile(t_, ROW_TILE, 8), _tile(f_, 512)

    def body(dr_ref, w_ref, g_ref, u_ref, dg_ref, du_ref):
        da = scale * _dot(dr_ref[...], w_ref[...], NT)
        g = g_ref[...]
        s = _sigmoid(g)
        dg_ref[...] = (da * u_ref[...] * _dsilu(g, s)).astype(BF16)
        du_ref[...] = (da * g * s).astype(BF16)

    body, dep_specs, deps = _after(body, 4, dep)
    tile = pl.BlockSpec((tm, tn), lambda i, j: (i, j))
    return pl.pallas_call(
        body, name=name, grid=(t_ // tm, f_ // tn),
        in_specs=[pl.BlockSpec((tm, d_), lambda i, j: (i, 0)),
                  pl.BlockSpec((tn, d_), lambda i, j: (j, 0)), tile, tile, *dep_specs],
        out_specs=[tile, tile],
        out_shape=[jax.ShapeDtypeStruct((t_, f_), BF16)] * 2,
        compiler_params=_params(),
    )(drb, w_out, gate, up, *deps)


def _mm_nn_res_lnbwd(a, w, dres, r, g, name, dep=None):
    t_, k_ = a.shape
    d_ = w.shape[1]
    tm, tk = _tile(t_, ROW_TILE, 8), _tile(k_, 512)
    nk = k_ // tk

    def body(a_ref, w_ref, dres_ref, r_ref, g_ref, dr_ref, drb_ref, dg_ref, db_ref, acc):
        i, k = pl.program_id(0), pl.program_id(1)

        @pl.when(k == 0)
        def _():
            acc[...] = jnp.zeros_like(acc)

        @pl.when((i == 0) & (k == 0))
        def _():
            dg_ref[...] = jnp.zeros_like(dg_ref)
            db_ref[...] = jnp.zeros_like(db_ref)

        acc[...] += _dot(a_ref[...], w_ref[...], NN)

        @pl.when(k == nk - 1)
        def _():
            def rows_out(rows):
                dy = acc[rows, :] + ALPHA * dres_ref[rows, :]
                xhat, rstd = _ln_stats(r_ref[rows, :])
                dr = _ln_bwd(dy, xhat, rstd, g_ref[...])
                dr_ref[rows, :] = dr
                drb_ref[rows, :] = dr.astype(BF16)
                dg_ref[...] += jnp.sum(dy * xhat, axis=0, keepdims=True)
                db_ref[...] += jnp.sum(dy, axis=0, keepdims=True)

            _for_row_blocks(tm, rows_out)

    body, dep_specs, deps = _after(body, 5, dep)
    row = _once((tm, d_), lambda i, k: (i, 0))
    vec = pl.BlockSpec((1, d_), lambda i, k: (0, 0))
    return pl.pallas_call(
        body, name=name, grid=(t_ // tm, nk),
        in_specs=[pl.BlockSpec((tm, tk), lambda i, k: (i, k)),
                  pl.BlockSpec((tk, d_), lambda i, k: (k, 0)), row, row, vec, *dep_specs],
        out_specs=[row, row, vec, vec],
        out_shape=[jax.ShapeDtypeStruct((t_, d_), F32), jax.ShapeDtypeStruct((t_, d_), BF16),
                   jax.ShapeDtypeStruct((1, d_), F32), jax.ShapeDtypeStruct((1, d_), F32)],
        scratch_shapes=[pltpu.VMEM((tm, d_), F32)],
        compiler_params=_params(),
    )(a, w, dres, r, g, *deps)


def _mm_nn_res(a, w, dres, name, dep=None):
    t_, k_ = a.shape
    d_ = w.shape[1]
    tm, tk = _tile(t_, ROW_TILE, 8), _tile(k_, 512)
    nk = k_ // tk

    def body(a_ref, w_ref, dres_ref, o_ref, acc):
        k = pl.program_id(1)

        @pl.when(k == 0)
        def _():
            acc[...] = jnp.zeros_like(acc)

        acc[...] += _dot(a_ref[...], w_ref[...], NN)

        @pl.when(k == nk - 1)
        def _():
            def rows_out(rows):
                o_ref[rows, :] = acc[rows, :] + ALPHA * dres_ref[rows, :]

            _for_row_blocks(tm, rows_out)

    body, dep_specs, deps = _after(body, 3, dep)
    row = _once((tm, d_), lambda i, k: (i, 0))
    return pl.pallas_call(
        body, name=name, grid=(t_ // tm, nk),
        in_specs=[pl.BlockSpec((tm, tk), lambda i, k: (i, k)),
                  pl.BlockSpec((tk, d_), lambda i, k: (k, 0)), row, *dep_specs],
        out_specs=row,
        out_shape=jax.ShapeDtypeStruct((t_, d_), F32),
        scratch_shapes=[pltpu.VMEM((tm, d_), F32)],
        compiler_params=_params(),
    )(a, w, dres, *deps)


def _dz_gate_bwd(drb, w_out, proj, ya, yb, d_, name, dep=None):
    t_ = drb.shape[0]
    goff = 5 * d_ + QKV_WIDTH
    tm, tn = _tile(t_, ROW_TILE, 8), _tile_multi([d_, goff], 512)
    ja, jb = goff // tn, (goff + d_) // tn

    def body(dr_ref, w_ref, ga_ref, gb_ref, ya_ref, yb_ref, dya_ref, dyb_ref, dga_ref, dgb_ref):
        dz = _dot(dr_ref[...], w_ref[...], NT)
        sa, sb = _sigmoid(ga_ref[...]), _sigmoid(gb_ref[...])
        dya_ref[...] = (dz * sa).astype(BF16)
        dyb_ref[...] = (dz * sb).astype(BF16)
        dga_ref[...] = (dz * ya_ref[...] * sa * (1.0 - sa)).astype(BF16)
        dgb_ref[...] = (dz * yb_ref[...] * sb * (1.0 - sb)).astype(BF16)

    body, dep_specs, deps = _after(body, 6, dep)
    tile = pl.BlockSpec((tm, tn), lambda i, j: (i, j))
    return pl.pallas_call(
        body, name=name, grid=(t_ // tm, d_ // tn),
        in_specs=[pl.BlockSpec((tm, d_), lambda i, j: (i, 0)),
                  pl.BlockSpec((tn, d_), lambda i, j: (j, 0)),
                  pl.BlockSpec((tm, tn), lambda i, j: (i, ja + j)),
                  pl.BlockSpec((tm, tn), lambda i, j: (i, jb + j)), tile, tile, *dep_specs],
        out_specs=[tile] * 4,
        out_shape=[jax.ShapeDtypeStruct((t_, d_), BF16)] * 4,
        compiler_params=_params(),
    )(drb, w_out, proj, proj, ya, yb, *deps)


def _ln_loss_bwd(r, target, g, b, name):
    t_, d_ = r.shape
    tm = _tile(t_, 256, 8)

    def body(r_ref, t_ref, g_ref, b_ref, dr_ref, drb_ref, dg_ref, db_ref, loss_ref):
        i = pl.program_id(0)

        @pl.when(i == 0)
        def _():
            dg_ref[...] = jnp.zeros_like(dg_ref)
            db_ref[...] = jnp.zeros_like(db_ref)
            loss_ref[...] = jnp.zeros_like(loss_ref)

        xhat, rstd = _ln_stats(r_ref[...])
        gain = g_ref[...]
        err = xhat * gain + b_ref[...] - t_ref[...]
        loss_ref[...] += (0.5 / d_) * jnp.sum(err * err)
        dy = err * (1.0 / d_)
        dr = _ln_bwd(dy, xhat, rstd, gain)
        dr_ref[...] = dr
        drb_ref[...] = dr.astype(BF16)
        dg_ref[...] += jnp.sum(dy * xhat, axis=0, keepdims=True)
        db_ref[...] += jnp.sum(dy, axis=0, keepdims=True)

    row = pl.BlockSpec((tm, d_), lambda i: (i, 0))
    vec = pl.BlockSpec((1, d_), lambda i: (0, 0))
    return pl.pallas_call(
        body, name=name, grid=(t_ // tm,),
        in_specs=[row, row, vec, vec],
        out_specs=[row, row, vec, vec, pl.BlockSpec((1, HEAD), lambda i: (0, 0))],
        out_shape=[jax.ShapeDtypeStruct((t_, d_), F32), jax.ShapeDtypeStruct((t_, d_), BF16),
                   jax.ShapeDtypeStruct((1, d_), F32), jax.ShapeDtypeStruct((1, d_), F32),
                   jax.ShapeDtypeStruct((1, HEAD), F32)],
        compiler_params=_params(),
    )(r, target, g, b)


def _chunk_scan(x, row, reverse, size):
    s = 1
    while s < CHUNK:
        if reverse:
            x = x + jnp.where(row < CHUNK - s, pltpu.roll(x, size - s, 0), 0.0)
        else:
            x = x + jnp.where(row >= s, pltpu.roll(x, s, 0), 0.0)
        s *= 2
    return x


def _lower_bound(tab):
    return _sigmoid(tab[0:1, :] - tab[1:2, :])


def _tri_mask(reverse):
    r = lax.broadcasted_iota(jnp.int32, (CHUNK, CHUNK), 0)
    c = lax.broadcasted_iota(jnp.int32, (CHUNK, CHUNK), 1)
    return (c >= r) if reverse else (r >= c)


def _hgrn_fwd(proj, lbf, lbb, ng, b_, s_, d_, name):
    h_ = d_ // HEAD
    nc = s_ // CHUNK

    def body(hq_ref, hff_ref, hfb_ref, hi_ref, hog_ref, lbf_ref, lbb_ref, ng_ref, ya_ref, o_ref,
             q_s, k_s, cum_s, o_s):
        row = lax.broadcasted_iota(jnp.int32, (s_, HEAD), 0) % CHUNK
        hq = hq_ref[...]
        q_s[...] = hq * _sigmoid(hq)
        o_s[...] = jnp.zeros_like(o_s)
        for reverse, hf_ref, lb_ref in ((False, hff_ref, lbf_ref), (True, hfb_ref, lbb_ref)):
            lb = _lower_bound(lb_ref[...])
            f = lb + (1.0 - lb) * _sigmoid(hf_ref[...])
            k_s[...] = 1.0 - f
            cum_s[...] = _chunk_scan(jnp.log(f), row, reverse, s_)
            mask = _tri_mask(reverse)

            def step(n, st, reverse=reverse, mask=mask):
                idx = (nc - 1 - n) if reverse else n
                sl = pl.ds(pl.multiple_of(idx * CHUNK, CHUNK), CHUNK)
                cm = cum_s[sl, :]
                tot = cm[0:1, :] if reverse else cm[CHUNK - 1:CHUNK, :]
                qc, kc = q_s[sl, :], k_s[sl, :]
                vb = hi_ref[sl, :].astype(BF16)
                qd = (qc * jnp.exp(cm)).astype(BF16)
                kd = (kc * jnp.exp(-cm)).astype(BF16)
                ke = (kc * jnp.exp(tot - cm)).astype(BF16)
                a = jnp.where(mask, _dot(qd, kd, NT), 0.0)
                o_s[sl, :] += _dot(a.astype(BF16), vb, NN) + _dot(qd, st.astype(BF16), NT)
                return st * jnp.exp(tot) + _dot(vb, ke, TN)

            lax.fori_loop(0, nc, step, jnp.zeros((HEAD, HEAD), F32))
        o = o_s[...]
        o_ref[...] = o
        nrm = o * lax.rsqrt(jnp.mean(o * o, axis=-1, keepdims=True) + LN_EPS)
        hog = hog_ref[...]
        ya_ref[...] = (nrm * ng_ref[...] * hog * _sigmoid(hog)).astype(BF16)

    def col(part):
        return pl.BlockSpec((s_, HEAD), lambda h, b, part=part: (b, part * h_ + h))

    tab = pl.BlockSpec((2, HEAD), lambda h, b: (0, h))
    out = pl.BlockSpec((s_, HEAD), lambda h, b: (b, h))
    return pl.pallas_call(
        body, name=name, grid=(h_, b_),
        in_specs=[col(0), col(1), col(2), col(3), col(4), tab, tab,
                  pl.BlockSpec((1, HEAD), lambda h, b: (0, h))],
        out_specs=[out, out],
        out_shape=[jax.ShapeDtypeStruct((b_ * s_, d_), BF16), jax.ShapeDtypeStruct((b_ * s_, d_), F32)],
        scratch_shapes=[pltpu.VMEM((s_, HEAD), F32)] * 4,
        compiler_params=_params(),
    )(proj, proj, proj, proj, proj, lbf, lbb, ng)


def _hgrn_bwd(proj, lbf, lbb, ng, o_sum, dya, b_, s_, d_, name):
    h_ = d_ // HEAD
    nc = s_ // CHUNK

    def body(hq_ref, hff_ref, hfb_ref, hi_ref, hog_ref, lbf_ref, lbb_ref, ng_ref, o_ref, dya_ref,
             dhq_ref, dhff_ref, dhfb_ref, dhi_ref, dhog_ref, dng_ref, dlbf_ref, dlbb_ref,
             q_s, k_s, cum_s, do_s, dq_s, dv_s, db_s, dk_s, st_s):
        b = pl.program_id(1)

        @pl.when(b == 0)
        def _():
            dng_ref[...] = jnp.zeros_like(dng_ref)
            dlbf_ref[...] = jnp.zeros_like(dlbf_ref)
            dlbb_ref[...] = jnp.zeros_like(dlbb_ref)

        row = lax.broadcasted_iota(jnp.int32, (s_, HEAD), 0) % CHUNK
        crow = lax.broadcasted_iota(jnp.int32, (CHUNK, HEAD), 0)
        hq = hq_ref[...]
        sq = _sigmoid(hq)
        q_s[...] = hq * sq
        o = o_ref[...]
        rinv = lax.rsqrt(jnp.mean(o * o, axis=-1, keepdims=True) + LN_EPS)
        nrm = o * rinv
        hog = hog_ref[...]
        so = _sigmoid(hog)
        gain = ng_ref[...]
        dy = dya_ref[...]
        dhog_ref[...] = (dy * nrm * gain * _dsilu(hog, so)).astype(BF16)
        dng_ref[...] += jnp.sum(dy * nrm * hog * so, axis=0, keepdims=True)
        dn = dy * gain * hog * so
        do_s[...] = rinv * (dn - nrm * jnp.mean(dn * nrm, axis=-1, keepdims=True))
        dq_s[...] = jnp.zeros_like(dq_s)
        dv_s[...] = jnp.zeros_like(dv_s)

        for reverse, hf_ref, lb_ref, dhf_ref, dlb_ref in (
                (False, hff_ref, lbf_ref, dhff_ref, dlbf_ref), (True, hfb_ref, lbb_ref, dhfb_ref, dlbb_ref)):
            tab = lb_ref[...]
            lb = _lower_bound(tab)
            sf = _sigmoid(hf_ref[...])
            f = lb + (1.0 - lb) * sf
            k_s[...] = 1.0 - f
            cum_s[...] = _chunk_scan(jnp.log(f), row, reverse, s_)
            mask = _tri_mask(reverse)
            last = 0 if reverse else CHUNK - 1

            def chunk(idx, reverse=reverse):
                sl = pl.ds(pl.multiple_of(idx * CHUNK, CHUNK), CHUNK)
                cm = cum_s[sl, :]
                tot = cm[0:1, :] if reverse else cm[CHUNK - 1:CHUNK, :]
                return sl, cm, tot

            def fstep(n, st, reverse=reverse, chunk=chunk):
                idx = (nc - 1 - n) if reverse else n
                sl, cm, tot = chunk(idx)
                st_s[idx] = st
                ke = (k_s[sl, :] * jnp.exp(tot - cm)).astype(BF16)
                return st * jnp.exp(tot) + _dot(hi_ref[sl, :].astype(BF16), ke, TN)

            lax.fori_loop(0, nc, fstep, jnp.zeros((HEAD, HEAD), F32))

            def bstep(n, dst, reverse=reverse, chunk=chunk, mask=mask, last=last):
                idx = n if reverse else (nc - 1 - n)
                sl, cm, tot = chunk(idx)
                eb, enb, ee, dec = jnp.exp(cm), jnp.exp(-cm), jnp.exp(tot - cm), jnp.exp(tot)
                qc, kc = q_s[sl, :], k_s[sl, :]
                qd, kd, ke = qc * eb, kc * enb, kc * ee
                qdb, kdb, keb = qd.astype(BF16), kd.astype(BF16), ke.astype(BF16)
                vb = hi_ref[sl, :].astype(BF16)
                dob = do_s[sl, :].astype(BF16)
                st0 = st_s[idx]
                dstb = dst.astype(BF16)
                a = jnp.where(mask, _dot(qdb, kdb, NT), 0.0).astype(BF16)
                da = jnp.where(mask, _dot(dob, vb, NT), 0.0).astype(BF16)
                dqd = _dot(da, kdb, NN) + _dot(dob, st0.astype(BF16), NN)
                dkd = _dot(da, qdb, TN)
                dv = _dot(a, dob, TN) + _dot(keb, dstb, NT)
                dke = _dot(vb, dstb, NN)
                ddec = jnp.sum(dst * st0, axis=0, keepdims=True)
                dtot = jnp.sum(dke * ke, axis=0, keepdims=True) + ddec * dec
                db = dqd * qd - dkd * kd - dke * ke
                db_s[sl, :] = db + jnp.where(crow == last, dtot, 0.0)
                dk_s[sl, :] = dkd * enb + dke * ee
                dq_s[sl, :] += dqd * eb
                dv_s[sl, :] += dv
                return dst * dec + _dot(dob, qdb, TN)

            lax.fori_loop(0, nc, bstep, jnp.zeros((HEAD, HEAD), F32))
            dlogf = _chunk_scan(db_s[...], row, not reverse, s_)
            df = dlogf / f - dk_s[...]
            dhf_ref[...] = (df * (1.0 - lb) * sf * (1.0 - sf)).astype(BF16)
            dlb = jnp.sum(df * (1.0 - sf), axis=0, keepdims=True) * lb * (1.0 - lb)
            dlb_ref[0:1, :] += dlb
            dlb_ref[1:2, :] -= dlb

        dhq_ref[...] = (dq_s[...] * _dsilu(hq, sq)).astype(BF16)
        dhi_ref[...] = dv_s[...].astype(BF16)

    def col(part):
        return pl.BlockSpec((s_, HEAD), lambda h, b, part=part: (b, part * h_ + h))

    tab = pl.BlockSpec((2, HEAD), lambda h, b: (0, h))
    vec = pl.BlockSpec((1, HEAD), lambda h, b: (0, h))
    blk = pl.BlockSpec((s_, HEAD), lambda h, b: (b, h))
    act = jax.ShapeDtypeStruct((b_ * s_, d_), BF16)
    return pl.pallas_call(
        body, name=name, grid=(h_, b_),
        in_specs=[col(0), col(1), col(2), col(3), col(4), tab, tab, vec, blk, blk],
        out_specs=[blk] * 5 + [vec, tab, tab],
        out_shape=[act] * 5 + [jax.ShapeDtypeStruct((1, d_), F32), jax.ShapeDtypeStruct((2, d_), F32),
                               jax.ShapeDtypeStruct((2, d_), F32)],
        scratch_shapes=[pltpu.VMEM((s_, HEAD), F32)] * 8 + [pltpu.VMEM((nc, HEAD, HEAD), F32)],
        compiler_params=_params(),
    )(proj, proj, proj, proj, proj, lbf, lbb, ng, o_sum, dya)


BLOCK = 64
HALF = BLOCK // 2


def _block_scan(x, row, reverse, size):
    s = 1
    while s < BLOCK:
        if reverse:
            x = x + jnp.where(row < BLOCK - s, pltpu.roll(x, size - s, 0), 0.0)
        else:
            x = x + jnp.where(row >= s, pltpu.roll(x, s, 0), 0.0)
        s *= 2
    return x


def _block_exps(l, reverse):
    first = lax.broadcasted_iota(jnp.int32, (BLOCK, HEAD), 0) < HALF
    q1, q3 = HALF // 2, HALF + HALF // 2
    if reverse:
        rho1, rho2, lh, ltot = l[q1:q1 + 1], l[q3:q3 + 1], l[HALF:HALF + 1], l[0:1]
    else:
        rho1, rho2, lh, ltot = l[q1 - 1:q1], l[q3 - 1:q3], l[HALF - 1:HALF], l[BLOCK - 1:BLOCK]
    ref = jnp.where(first, rho1, rho2)
    query_half = first if reverse else jnp.logical_not(first)
    e2 = jnp.where(query_half, jnp.exp(jnp.minimum(l - lh, 0.0)), 0.0)
    e1 = jnp.where(query_half, 0.0, jnp.exp(jnp.minimum(lh - l, 0.0)))
    return jnp.exp(l - ref), jnp.exp(ref - l), e2, e1, jnp.exp(l), jnp.exp(ltot - l), jnp.exp(ltot)


def _half_mask(reverse):
    r = lax.broadcasted_iota(jnp.int32, (BLOCK, BLOCK), 0)
    c = lax.broadcasted_iota(jnp.int32, (BLOCK, BLOCK), 1)
    same = (r < HALF) == (c < HALF)
    return same & ((c >= r) if reverse else (r >= c))


def _hgrn_fwd(proj, lbf, lbb, ng, b_, s_, d_, name):
    h_ = d_ // HEAD
    nb = s_ // BLOCK

    def body(hq_ref, hff_ref, hfb_ref, hi_ref, hog_ref, lbf_ref, lbb_ref, ng_ref, ya_ref, o_ref,
             q_s, k_s, l_s, of_s, oi_s, qd_s, u_s, st_s, dec_s):
        row = lax.broadcasted_iota(jnp.int32, (s_, HEAD), 0) % BLOCK
        hq = hq_ref[...]
        q_s[...] = hq * _sigmoid(hq)
        for reverse, hf_ref, lb_ref in ((False, hff_ref, lbf_ref), (True, hfb_ref, lbb_ref)):
            lb = _lower_bound(lb_ref[...])
            f = lb + (1.0 - lb) * _sigmoid(hf_ref[...])
            k_s[...] = 1.0 - f
            l_s[...] = _block_scan(jnp.log(f), row, reverse, s_)
            mask = _half_mask(reverse)

            def inside(n, carry, reverse=reverse, mask=mask):
                sl = pl.ds(pl.multiple_of(n * BLOCK, BLOCK), BLOCK)
                eq, ek, e2, e1, el, ee, dec = _block_exps(l_s[sl, :], reverse)
                qc, kc = q_s[sl, :], k_s[sl, :]
                vb = hi_ref[sl, :].astype(BF16)
                a = jnp.where(mask, _dot((qc * eq).astype(BF16), (kc * ek).astype(BF16), NT), 0.0)
                a = a + _dot((qc * e2).astype(BF16), (kc * e1).astype(BF16), NT)
                oi_s[sl, :] = _dot(a.astype(BF16), vb, NN)
                qd_s[sl, :] = (qc * el).astype(BF16)
                u_s[n] = _dot(vb, (kc * ee).astype(BF16), TN)
                dec_s[n] = jnp.broadcast_to(dec, (8, HEAD))
                return carry

            lax.fori_loop(0, nb, inside, 0, unroll=4)

            def carry_state(n, st, reverse=reverse):
                idx = (nb - 1 - n) if reverse else n
                st_s[idx] = st.astype(BF16)
                return st * dec_s[idx][0:1, :] + u_s[idx]

            lax.fori_loop(0, nb, carry_state, jnp.zeros((HEAD, HEAD), F32))

            def across(n, carry, reverse=reverse):
                sl = pl.ds(pl.multiple_of(n * BLOCK, BLOCK), BLOCK)
                o_dir = oi_s[sl, :] + _dot(qd_s[sl, :], st_s[n], NT)
                if not reverse:
                    of_s[sl, :] = o_dir
                else:
                    o = of_s[sl, :] + o_dir
                    o_ref[sl, :] = o
                    nrm = o * lax.rsqrt(jnp.mean(o * o, axis=-1, keepdims=True) + LN_EPS)
                    hog = hog_ref[sl, :]
                    ya_ref[sl, :] = (nrm * ng_ref[...] * hog * _sigmoid(hog)).astype(BF16)
                return carry

            lax.fori_loop(0, nb, across, 0, unroll=8)

    def col(part):
        return pl.BlockSpec((s_, HEAD), lambda h, b, part=part: (b, part * h_ + h))

    tab = pl.BlockSpec((2, HEAD), lambda h, b: (0, h))
    out = pl.BlockSpec((s_, HEAD), lambda h, b: (b, h))
    return pl.pallas_call(
        body, name=name, grid=(h_, b_),
        in_specs=[col(0), col(1), col(2), col(3), col(4), tab, tab,
                  pl.BlockSpec((1, HEAD), lambda h, b: (0, h))],
        out_specs=[out, out],
        out_shape=[jax.ShapeDtypeStruct((b_ * s_, d_), BF16), jax.ShapeDtypeStruct((b_ * s_, d_), F32)],
        scratch_shapes=[pltpu.VMEM((s_, HEAD), F32)] * 5 + [
            pltpu.VMEM((s_, HEAD), BF16), pltpu.VMEM((nb, HEAD, HEAD), F32), pltpu.VMEM((nb, HEAD, HEAD), BF16),
            pltpu.VMEM((nb, 8, HEAD), F32)],
        compiler_params=_params(),
    )(proj, proj, proj, proj, proj, lbf, lbb, ng)


def _hgrn_bwd(proj, lbf, lbb, ng, o_sum, dya, b_, s_, d_, name):
    h_ = d_ // HEAD
    nb = s_ // BLOCK

    def body(hq_ref, hff_ref, hfb_ref, hi_ref, hog_ref, lbf_ref, lbb_ref, ng_ref, o_ref, dya_ref,
             dhq_ref, dhff_ref, dhfb_ref, dhi_ref, dhog_ref, dng_ref, dlbf_ref, dlbb_ref,
             q_s, k_s, l_s, do_s, dq_s, dv_s, dl_s, dk_s, u_s, w_s, st_s, dst_s, dec_s):
        b = pl.program_id(1)

        @pl.when(b == 0)
        def _():
            dng_ref[...] = jnp.zeros_like(dng_ref)
            dlbf_ref[...] = jnp.zeros_like(dlbf_ref)
            dlbb_ref[...] = jnp.zeros_like(dlbb_ref)

        row = lax.broadcasted_iota(jnp.int32, (s_, HEAD), 0) % BLOCK
        brow = lax.broadcasted_iota(jnp.int32, (BLOCK, HEAD), 0)
        hq = hq_ref[...]
        sq = _sigmoid(hq)
        q_s[...] = hq * sq
        o = o_ref[...]
        rinv = lax.rsqrt(jnp.mean(o * o, axis=-1, keepdims=True) + LN_EPS)
        nrm = o * rinv
        hog = hog_ref[...]
        so = _sigmoid(hog)
        gain = ng_ref[...]
        dy = dya_ref[...]
        dhog_ref[...] = (dy * nrm * gain * _dsilu(hog, so)).astype(BF16)
        dng_ref[...] += jnp.sum(dy * nrm * hog * so, axis=0, keepdims=True)
        dn = dy * gain * hog * so
        do_s[...] = rinv * (dn - nrm * jnp.mean(dn * nrm, axis=-1, keepdims=True))

        for reverse, hf_ref, lb_ref, dhf_ref, dlb_ref in (
                (False, hff_ref, lbf_ref, dhff_ref, dlbf_ref), (True, hfb_ref, lbb_ref, dhfb_ref, dlbb_ref)):
            lb = _lower_bound(lb_ref[...])
            sf = _sigmoid(hf_ref[...])
            f = lb + (1.0 - lb) * sf
            k_s[...] = 1.0 - f
            l_s[...] = _block_scan(jnp.log(f), row, reverse, s_)
            mask = _half_mask(reverse)
            total_row = 0 if reverse else BLOCK - 1

            def prepare(n, carry, reverse=reverse):
                sl = pl.ds(pl.multiple_of(n * BLOCK, BLOCK), BLOCK)
                _, _, _, _, el, ee, dec = _block_exps(l_s[sl, :], reverse)
                vb = hi_ref[sl, :].astype(BF16)
                u_s[n] = _dot(vb, (k_s[sl, :] * ee).astype(BF16), TN)
                w_s[n] = _dot(do_s[sl, :].astype(BF16), (q_s[sl, :] * el).astype(BF16), TN)
                dec_s[n] = jnp.broadcast_to(dec, (8, HEAD))
                return carry

            lax.fori_loop(0, nb, prepare, 0, unroll=4)

            def carry_state(n, st, reverse=reverse):
                idx = (nb - 1 - n) if reverse else n
                st_s[idx] = st
                return st * dec_s[idx][0:1, :] + u_s[idx]

            lax.fori_loop(0, nb, carry_state, jnp.zeros((HEAD, HEAD), F32))

            def carry_grad(n, dst, reverse=reverse):
                idx = n if reverse else (nb - 1 - n)
                dst_s[idx] = dst
                return dst * dec_s[idx][0:1, :] + w_s[idx]

            lax.fori_loop(0, nb, carry_grad, jnp.zeros((HEAD, HEAD), F32))

            def inside(n, carry, reverse=reverse, mask=mask, total_row=total_row):
                sl = pl.ds(pl.multiple_of(n * BLOCK, BLOCK), BLOCK)
                eq, ek, e2, e1, el, ee, dec = _block_exps(l_s[sl, :], reverse)
                qc, kc = q_s[sl, :], k_s[sl, :]
                vb = hi_ref[sl, :].astype(BF16)
                dob = do_s[sl, :].astype(BF16)
                qt, kt, q2, k1 = ((qc * eq).astype(BF16), (kc * ek).astype(BF16),
                                  (qc * e2).astype(BF16), (kc * e1).astype(BF16))
                kend = kc * ee
                st0, dst1 = st_s[n], dst_s[n]
                dstb = dst1.astype(BF16)
                a = jnp.where(mask, _dot(qt, kt, NT), 0.0) + _dot(q2, k1, NT)
                da = _dot(dob, vb, NT)
                dab = da.astype(BF16)
                dad = jnp.where(mask, da, 0.0).astype(BF16)
                dqt, dkt = _dot(dad, kt, NN), _dot(dad, qt, TN)
                dq2, dk1 = _dot(dab, k1, NN), _dot(dab, q2, TN)
                dqd = _dot(dob, st0.astype(BF16), NN)
                dke = _dot(vb, dstb, NN)
                dv = _dot(a.astype(BF16), dob, TN) + _dot(kend.astype(BF16), dstb, NT)
                dq = dqt * eq + dq2 * e2 + dqd * el
                dk = dkt * ek + dk1 * e1 + dke * ee
                dtot = jnp.sum(dke * kend, axis=0, keepdims=True) + jnp.sum(dst1 * st0, axis=0, keepdims=True) * dec
                dl_s[sl, :] = qc * dq - kc * dk + jnp.where(brow == total_row, dtot, 0.0)
                dk_s[sl, :] = dk
                if not reverse:
                    dq_s[sl, :] = dq
                    dv_s[sl, :] = dv
                else:
                    hqc = hq_ref[sl, :]
                    dhq_ref[sl, :] = ((dq_s[sl, :] + dq) * _dsilu(hqc, _sigmoid(hqc))).astype(BF16)
                    dhi_ref[sl, :] = (dv_s[sl, :] + dv).astype(BF16)
                return carry

            lax.fori_loop(0, nb, inside, 0, unroll=2)
            dlogf = _block_scan(dl_s[...], row, not reverse, s_)
            df = dlogf / f - dk_s[...]
            dhf_ref[...] = (df * (1.0 - lb) * sf * (1.0 - sf)).astype(BF16)
            dlb = jnp.sum(df * (1.0 - sf), axis=0, keepdims=True) * lb * (1.0 - lb)
            dlb_ref[0:1, :] += dlb
            dlb_ref[1:2, :] -= dlb

    def col(part):
        return pl.BlockSpec((s_, HEAD), lambda h, b, part=part: (b, part * h_ + h))

    tab = pl.BlockSpec((2, HEAD), lambda h, b: (0, h))
    vec = pl.BlockSpec((1, HEAD), lambda h, b: (0, h))
    blk = pl.BlockSpec((s_, HEAD), lambda h, b: (b, h))
    act = jax.ShapeDtypeStruct((b_ * s_, d_), BF16)
    state = pltpu.VMEM((nb, HEAD, HEAD), F32)
    return pl.pallas_call(
        body, name=name, grid=(h_, b_),
        in_specs=[col(0), col(1), col(2), col(3), col(4), tab, tab, vec, blk, blk],
        out_specs=[blk] * 5 + [vec, tab, tab],
        out_shape=[act] * 5 + [jax.ShapeDtypeStruct((1, d_), F32), jax.ShapeDtypeStruct((2, d_), F32),
                               jax.ShapeDtypeStruct((2, d_), F32)],
        scratch_shapes=[pltpu.VMEM((s_, HEAD), F32)] * 8 + [state] * 4 + [pltpu.VMEM((nb, 8, HEAD), F32)],
        compiler_params=_params(),
    )(proj, proj, proj, proj, proj, lbf, lbb, ng, o_sum, dya)


def _rope_tables(s_):
    half = ROPE_DIM // 2
    inv_freq = ROPE_THETA ** (-jnp.arange(0, ROPE_DIM, 2, dtype=F32) / ROPE_DIM)
    ang = jnp.arange(s_, dtype=F32)[:, None] * inv_freq
    cos, sin = jnp.cos(ang), jnp.sin(ang)
    zeros = jnp.zeros((s_, HEAD - ROPE_DIM), F32)
    zh = jnp.zeros((s_, half), F32)
    c = jnp.concatenate([cos, cos, jnp.ones((s_, HEAD - ROPE_DIM), F32)], axis=1)
    s1 = jnp.concatenate([-sin, zh, zeros], axis=1)
    s2 = jnp.concatenate([zh, sin, zeros], axis=1)
    return c, s1, s2


def _rope(t, c, s1, s2):
    half = ROPE_DIM // 2
    return t * c + pltpu.roll(t, HEAD - half, 1) * s1 + pltpu.roll(t, half, 1) * s2


def _rope_bwd(dt, c, s1, s2):
    half = ROPE_DIM // 2
    return dt * c + pltpu.roll(dt * s1, half, 1) + pltpu.roll(dt * s2, HEAD - half, 1)


def _window_mask(r0, qb, wk, seg):
    row = lax.broadcasted_iota(jnp.int32, (qb, wk), 0)
    col = lax.broadcasted_iota(jnp.int32, (qb, wk), 1)
    kj = r0 - ATTN_HALF + col
    return (col - row >= 0) & (col - row <= 2 * ATTN_HALF) & (kj >= 0) & (kj < seg)


def _attn_fwd(qkv, tabs, b_, s_, dil, name):
    seg = s_ // dil
    qb = min(128, seg)
    nq, wk = seg // qb, qb + 2 * ATTN_HALF
    scale = HEAD ** -0.5
    ncol = QKV_GROUP // HEAD

    def body(q_ref, k_ref, v_ref, c_ref, s1_ref, s2_ref, o_ref, lse_ref, q_s, k_s, v_s):
        c, s1, s2 = c_ref[...], s1_ref[...], s2_ref[...]
        q_s[...] = _rope(q_ref[...], c, s1, s2).astype(BF16)
        k_s[...] = jnp.zeros_like(k_s)
        v_s[...] = jnp.zeros_like(v_s)
        k_s[ATTN_HALF:ATTN_HALF + seg, :] = _rope(k_ref[...], c, s1, s2).astype(BF16)
        v_s[ATTN_HALF:ATTN_HALF + seg, :] = v_ref[...].astype(BF16)

        def step(i, carry):
            r0 = pl.multiple_of(i * qb, qb)
            sc = _dot(q_s[pl.ds(r0, qb), :], k_s[pl.ds(r0, wk), :], NT) * scale
            sc = jnp.where(_window_mask(r0, qb, wk, seg), sc, NEG_INF)
            m = jnp.max(sc, axis=-1, keepdims=True)
            p = jnp.exp(sc - m)
            den = jnp.sum(p, axis=-1, keepdims=True)
            o_ref[pl.ds(r0, qb), :] = _dot(p.astype(BF16), v_s[pl.ds(r0, wk), :], NN) / den
            lse_ref[pl.ds(r0, qb), :] = jnp.broadcast_to(m + jnp.log(den), (qb, HEAD))
            return carry

        lax.fori_loop(0, nq, step, 0)

    def col(part):
        return pl.BlockSpec((seg, HEAD), lambda b, r, h, part=part: (b, r * ncol + part * ATTN_HEADS + h))

    tab = pl.BlockSpec((seg, HEAD), lambda b, r, h: (0, r))
    out = pl.BlockSpec((seg, HEAD), lambda b, r, h: (b, r * ATTN_HEADS + h))
    shape = jax.ShapeDtypeStruct((b_ * seg, dil * ATTN_OUT), F32)
    return pl.pallas_call(
        body, name=name, grid=(b_, dil, ATTN_HEADS),
        in_specs=[col(0), col(1), col(2), tab, tab, tab],
        out_specs=[out, out],
        out_shape=[shape, shape],
        scratch_shapes=[pltpu.VMEM((seg, HEAD), BF16), pltpu.VMEM((seg + 2 * ATTN_HALF, HEAD), BF16),
                        pltpu.VMEM((seg + 2 * ATTN_HALF, HEAD), BF16)],
        compiler_params=_params(),
    )(qkv, qkv, qkv, *tabs)


def _attn_bwd(qkv, tabs, dog, cg, lse, b_, s_, dil, name):
    seg = s_ // dil
    qb = min(128, seg)
    nq, wk = seg // qb, qb + 2 * ATTN_HALF
    scale = HEAD ** -0.5
    ncol = QKV_GROUP // HEAD

    def body(q_ref, k_ref, v_ref, c_ref, s1_ref, s2_ref, do_ref, cg_ref, lse_ref, dq_ref, dk_ref, dv_ref,
             q_s, k_s, v_s, dk_s, dv_s):
        c, s1, s2 = c_ref[...], s1_ref[...], s2_ref[...]
        q_s[...] = _rope(q_ref[...], c, s1, s2).astype(BF16)
        k_s[...] = jnp.zeros_like(k_s)
        v_s[...] = jnp.zeros_like(v_s)
        k_s[ATTN_HALF:ATTN_HALF + seg, :] = _rope(k_ref[...], c, s1, s2).astype(BF16)
        v_s[ATTN_HALF:ATTN_HALF + seg, :] = v_ref[...].astype(BF16)
        dk_s[...] = jnp.zeros_like(dk_s)
        dv_s[...] = jnp.zeros_like(dv_s)

        def step(i, carry):
            r0 = pl.multiple_of(i * qb, qb)
            rows, win = pl.ds(r0, qb), pl.ds(r0, wk)
            qc, kw, vw = q_s[rows, :], k_s[win, :], v_s[win, :]
            sc = _dot(qc, kw, NT) * scale
            p = jnp.where(_window_mask(r0, qb, wk, seg), jnp.exp(sc - lse_ref[rows, 0:1]), 0.0)
            dob = do_ref[rows, :].astype(BF16)
            dp = _dot(dob, vw, NT)
            ds = (p * (dp + cg_ref[rows, 0:1]) * scale).astype(BF16)
            dq = _dot(ds, kw, NN)
            dq_ref[rows, :] = _rope_bwd(dq, c_ref[rows, :], s1_ref[rows, :], s2_ref[rows, :])
            dk_s[win, :] += _dot(ds, qc, TN)
            dv_s[win, :] += _dot(p.astype(BF16), dob, TN)
            return carry

        lax.fori_loop(0, nq, step, 0)
        dk_ref[...] = _rope_bwd(dk_s[ATTN_HALF:ATTN_HALF + seg, :], c, s1, s2)
        dv_ref[...] = dv_s[ATTN_HALF:ATTN_HALF + seg, :]

    def col(part):
        return pl.BlockSpec((seg, HEAD), lambda b, r, h, part=part: (b, r * ncol + part * ATTN_HEADS + h))

    tab = pl.BlockSpec((seg, HEAD), lambda b, r, h: (0, r))
    out = pl.BlockSpec((seg, HEAD), lambda b, r, h: (b, r * ATTN_HEADS + h))
    shape = jax.ShapeDtypeStruct((b_ * seg, dil * ATTN_OUT), F32)
    return pl.pallas_call(
        body, name=name, grid=(b_, dil, ATTN_HEADS),
        in_specs=[col(0), col(1), col(2), tab, tab, tab, out, out, out],
        out_specs=[out, out, out],
        out_shape=[shape, shape, shape],
        scratch_shapes=[pltpu.VMEM((seg, HEAD), BF16), pltpu.VMEM((seg + 2 * ATTN_HALF, HEAD), BF16),
                        pltpu.VMEM((seg + 2 * ATTN_HALF, HEAD), BF16),
                        pltpu.VMEM((seg + 2 * ATTN_HALF, HEAD), F32), pltpu.VMEM((seg + 2 * ATTN_HALF, HEAD), F32)],
        compiler_params=_params(),
    )(qkv, qkv, qkv, *tabs, dog, cg, lse)


def _group_weights(lses):
    m = jnp.maximum(jnp.maximum(lses[0], lses[1]), lses[2])
    es = [jnp.exp(l - m) for l in lses]
    den = es[0] + es[1] + es[2]
    return [e / den for e in es]


def _combine_fwd(outs, lses, name):
    t_, w_ = outs[0].shape
    tm = _tile(t_, 512, 8)
    ng = len(outs)

    def body(*refs):
        ws = _group_weights([r[...] for r in refs[ng:2 * ng]])
        acc = ws[0] * refs[0][...]
        for g in range(1, ng):
            acc = acc + ws[g] * refs[g][...]
        refs[2 * ng][...] = acc.astype(BF16)

    row = pl.BlockSpec((tm, w_), lambda i: (i, 0))
    return pl.pallas_call(
        body, name=name, grid=(t_ // tm,), in_specs=[row] * (2 * ng), out_specs=row,
        out_shape=jax.ShapeDtypeStruct((t_, w_), BF16), compiler_params=_params(),
    )(*outs, *lses)


def _combine_bwd(dob, outs, lses, name):
    t_, w_ = outs[0].shape
    tm = _tile(t_, 512, 8)
    ng = len(outs)

    def body(*refs):
        do = refs[0][...]
        os_ = [r[...] for r in refs[1:1 + ng]]
        ws = _group_weights([r[...] for r in refs[1 + ng:1 + 2 * ng]])
        o = ws[0] * os_[0]
        for g in range(1, ng):
            o = o + ws[g] * os_[g]
        prod = do * o
        heads = [jnp.broadcast_to(jnp.sum(prod[:, h * HEAD:(h + 1) * HEAD], axis=-1, keepdims=True), (tm, HEAD))
                 for h in range(w_ // HEAD)]
        tot = jnp.concatenate(heads, axis=1)
        for g in range(ng):
            refs[1 + 2 * ng + g][...] = ws[g] * do
            refs[1 + 3 * ng + g][...] = -ws[g] * tot

    row = pl.BlockSpec((tm, w_), lambda i: (i, 0))
    shape = jax.ShapeDtypeStruct((t_, w_), F32)
    res = pl.pallas_call(
        body, name=name, grid=(t_ // tm,), in_specs=[row] * (1 + 2 * ng), out_specs=[row] * (2 * ng),
        out_shape=[shape] * (2 * ng), compiler_params=_params(),
    )(dob, *outs, *lses)
    return res[:ng], res[ng:]


def _adam_update(w, g, m, v):
    m = ADAM_B1 * m + (1.0 - ADAM_B1) * g
    v = ADAM_B2 * v + (1.0 - ADAM_B2) * (g * g)
    m_hat = m / (1.0 - ADAM_B1 ** ADAM_STEP)
    v_hat = v / (1.0 - ADAM_B2 ** ADAM_STEP)
    return -ADAM_LR * (m_hat / (jnp.sqrt(v_hat) + ADAM_EPS) + ADAM_WD * w), m, v


def _adam(w, g, m, v, name):
    r_, c_ = w.shape
    tr = _tile(r_, 256, 8)

    def body(w_ref, g_ref, m_ref, v_ref, d_ref, mo_ref, vo_ref):
        d_ref[...], mo_ref[...], vo_ref[...] = _adam_update(w_ref[...], g_ref[...], m_ref[...], v_ref[...])

    blk = pl.BlockSpec((tr, c_), lambda i: (i, 0))
    shape = jax.ShapeDtypeStruct((r_, c_), F32)
    return pl.pallas_call(
        body, name=name, grid=(r_ // tr,), in_specs=[blk] * 4, out_specs=[blk] * 3,
        out_shape=[shape] * 3, compiler_params=_params(),
    )(w, g, m, v)


def _sum_partials(recv, name):
    n_, r_, c_ = recv.shape
    tr = _tile(r_, 128, 16)

    def body(p_ref, o_ref):
        acc = p_ref[0].astype(F32)
        for i in range(1, n_):
            acc = acc + p_ref[i].astype(F32)
        o_ref[...] = acc

    return pl.pallas_call(
        body, name=name, grid=(r_ // tr,),
        in_specs=[pl.BlockSpec((n_, tr, c_), lambda i: (0, i, 0))],
        out_specs=pl.BlockSpec((tr, c_), lambda i: (i, 0)),
        out_shape=jax.ShapeDtypeStruct((r_, c_), F32), compiler_params=_params(),
    )(recv)


def _small_sum_adam(parts, w, m, v, name):
    n_, r_, c_ = parts.shape

    def body(p_ref, w_ref, m_ref, v_ref, g_ref, d_ref, mo_ref, vo_ref):
        g = p_ref[0]
        for i in range(1, n_):
            g = g + p_ref[i]
        g_ref[...] = g
        d_ref[...], mo_ref[...], vo_ref[...] = _adam_update(w_ref[...], g, m_ref[...], v_ref[...])

    shape = jax.ShapeDtypeStruct((r_, c_), F32)
    return pl.pallas_call(body, name=name, out_shape=[shape] * 4, compiler_params=_params())(parts, w, m, v)


def _my_place():
    x, y, c = lax.axis_index("x"), lax.axis_index("y"), lax.axis_index("c")
    return x, y, c


def _peer(x, y, c, d):
    px = 1 - x if d & 4 else x
    py = 1 - y if d & 2 else y
    pc = 1 - c if d & 1 else c
    return (px, py, pc), 4 * px + 2 * py + pc


def _all_gather(shards, name):
    nw = len(shards)

    def body(*refs):
        ins, outs = refs[:nw], refs[nw:2 * nw]
        send_sems, recv_sems, local_sems = refs[2 * nw:]
        x, y, c = _my_place()
        me = 4 * x + 2 * y + c
        copies = []
        for k in range(nw):
            rows = shards[k].shape[0]
            mine = outs[k].at[pl.ds(pl.multiple_of(me * rows, 16), rows), :]
            local = pltpu.make_async_copy(ins[k], mine, local_sems.at[k])
            local.start()
            copies.append(local)
            for d in range(1, N_DEV):
                place, _ = _peer(x, y, c, d)
                remote = pltpu.make_async_remote_copy(
                    src_ref=ins[k], dst_ref=mine, send_sem=send_sems.at[d - 1, k], recv_sem=recv_sems.at[d - 1, k],
                    device_id=place, device_id_type=MESH)
                remote.start()
                copies.append(remote)
        for cp in copies:
            cp.wait()

    hbm = pl.BlockSpec(memory_space=pl.ANY)
    return pl.pallas_call(
        body, name=name, in_specs=[hbm] * nw, out_specs=[hbm] * nw,
        out_shape=[jax.ShapeDtypeStruct((N_DEV * s.shape[0], s.shape[1]), s.dtype) for s in shards],
        scratch_shapes=[pltpu.SemaphoreType.DMA((N_DEV - 1, nw)), pltpu.SemaphoreType.DMA((N_DEV - 1, nw)),
                        pltpu.SemaphoreType.DMA((nw,))],
    )(*shards)


HBM_SPEC = pl.BlockSpec(memory_space=pltpu.HBM)
SEM_SPEC = pl.BlockSpec(memory_space=pltpu.SEMAPHORE)
EFFECT = pltpu.SideEffectType.DATAFLOW_SIDE_EFFECTING


def _in_hbm(a):
    return pltpu.with_memory_space_constraint(a, pltpu.HBM)


def _token_shape():
    return jax.ShapeDtypeStruct((8, HEAD), F32)


def _gather_start(shards, name):
    nw = len(shards)
    lands = [lax.empty((N_DEV * s.shape[0], s.shape[1]), s.dtype) for s in shards]

    def body(*refs):
        ins, lnd = refs[:nw], refs[nw:2 * nw]
        send, recv, own = refs[2 * nw:3 * nw], refs[3 * nw:4 * nw], refs[4 * nw:5 * nw]
        token = refs[7 * nw]
        x, y, c = _my_place()
        me = 4 * x + 2 * y + c
        for k in range(nw):
            rows = shards[k].shape[0]
            mine = lnd[k].at[pl.ds(pl.multiple_of(me * rows, 16), rows), :]
            pltpu.make_async_copy(ins[k], mine, own[k]).start()
            for d in range(1, N_DEV):
                place, _ = _peer(x, y, c, d)
                pltpu.make_async_remote_copy(
                    src_ref=ins[k], dst_ref=mine, send_sem=send[k].at[d - 1], recv_sem=recv[k].at[d - 1],
                    device_id=place, device_id_type=MESH).start()
        token[...] = jnp.zeros_like(token)

    sems = [pltpu.SemaphoreType.DMA((N_DEV - 1,))] * (2 * nw) + [pltpu.SemaphoreType.DMA(())] * nw
    thru = [pltpu.HBM(a.shape, a.dtype) for a in list(shards) + lands]
    res = pl.pallas_call(
        body, name=name, out_shape=(*sems, *thru, _token_shape()),
        in_specs=[HBM_SPEC] * (2 * nw),
        out_specs=(*([SEM_SPEC] * (3 * nw)), *([HBM_SPEC] * (2 * nw)), pl.BlockSpec(memory_space=pltpu.VMEM)),
        input_output_aliases={i: 3 * nw + i for i in range(2 * nw)},
        compiler_params=pltpu.CompilerParams(has_side_effects=EFFECT),
    )(*[_in_hbm(s) for s in shards], *[_in_hbm(l) for l in lands])
    return [dict(send=res[k], recv=res[nw + k], own=res[2 * nw + k], src=res[3 * nw + k], land=res[4 * nw + k])
            for k in range(nw)]


def _gather_wait(pending, after, name):
    rows = pending["src"].shape[0]

    def body(src_ref, land_ref, send, recv, own, after_ref, src_dead, got):
        x, y, c = _my_place()
        me = 4 * x + 2 * y + c
        mine = land_ref.at[pl.ds(pl.multiple_of(me * rows, 16), rows), :]
        pltpu.make_async_copy(src_ref, mine, own).wait()
        for d in range(1, N_DEV):
            place, _ = _peer(x, y, c, d)
            cp = pltpu.make_async_remote_copy(
                src_ref=src_ref, dst_ref=mine, send_sem=send.at[d - 1], recv_sem=recv.at[d - 1],
                device_id=place, device_id_type=MESH)
            cp.wait_send()
            cp.wait_recv()

    src, land = pending["src"], pending["land"]
    return pl.pallas_call(
        body, name=name, out_shape=(pltpu.HBM(src.shape, src.dtype), pltpu.HBM(land.shape, land.dtype)),
        in_specs=(HBM_SPEC, HBM_SPEC, SEM_SPEC, SEM_SPEC, SEM_SPEC, pl.BlockSpec(memory_space=pl.ANY)),
        out_specs=(HBM_SPEC, HBM_SPEC), input_output_aliases={0: 0, 1: 1},
        compiler_params=pltpu.CompilerParams(has_side_effects=EFFECT),
    )(src, land, pending["send"], pending["recv"], pending["own"], after)[1]


def _scatter_start(full, name):
    rows, cols = full.shape[0] // N_DEV, full.shape[1]
    land = lax.empty((N_DEV, rows, cols), full.dtype)

    def body(full_ref, land_ref, send, recv, own, full_thru, land_thru, token):
        x, y, c = _my_place()
        me = 4 * x + 2 * y + c
        slab = land_ref.at[me]
        pltpu.make_async_copy(full_ref.at[pl.ds(pl.multiple_of(me * rows, 16), rows), :], slab, own).start()
        for d in range(1, N_DEV):
            place, num = _peer(x, y, c, d)
            pltpu.make_async_remote_copy(
                src_ref=full_ref.at[pl.ds(pl.multiple_of(num * rows, 16), rows), :], dst_ref=slab,
                send_sem=send.at[d - 1], recv_sem=recv.at[d - 1], device_id=place, device_id_type=MESH).start()
        token[...] = jnp.zeros_like(token)

    res = pl.pallas_call(
        body, name=name,
        out_shape=(pltpu.SemaphoreType.DMA((N_DEV - 1,)), pltpu.SemaphoreType.DMA((N_DEV - 1,)),
                   pltpu.SemaphoreType.DMA(()),
                   pltpu.HBM(full.shape, full.dtype), pltpu.HBM(land.shape, land.dtype), _token_shape()),
        in_specs=(HBM_SPEC, HBM_SPEC),
        out_specs=(SEM_SPEC, SEM_SPEC, SEM_SPEC, HBM_SPEC, HBM_SPEC, pl.BlockSpec(memory_space=pltpu.VMEM)),
        input_output_aliases={0: 3, 1: 4},
        compiler_params=pltpu.CompilerParams(has_side_effects=EFFECT),
    )(_in_hbm(full), _in_hbm(land))
    return dict(send=res[0], recv=res[1], own=res[2], src=res[3], land=res[4]), res[5]


def _scatter_wait(pending, after, name):
    rows = pending["land"].shape[1]

    def body(src_ref, land_ref, send, recv, own, after_ref, src_dead, got):
        x, y, c = _my_place()
        me = 4 * x + 2 * y + c
        pltpu.make_async_copy(src_ref.at[pl.ds(pl.multiple_of(me * rows, 16), rows), :], land_ref.at[me], own).wait()
        for d in range(1, N_DEV):
            place, num = _peer(x, y, c, d)
            cp = pltpu.make_async_remote_copy(
                src_ref=src_ref.at[pl.ds(pl.multiple_of(num * rows, 16), rows), :], dst_ref=land_ref.at[me],
                send_sem=send.at[d - 1], recv_sem=recv.at[d - 1], device_id=place, device_id_type=MESH)
            cp.wait_send()
            cp.wait_recv()

    src, land = pending["src"], pending["land"]
    return pl.pallas_call(
        body, name=name, out_shape=(pltpu.HBM(src.shape, src.dtype), pltpu.HBM(land.shape, land.dtype)),
        in_specs=(HBM_SPEC, HBM_SPEC, SEM_SPEC, SEM_SPEC, SEM_SPEC, pl.BlockSpec(memory_space=pl.ANY)),
        out_specs=(HBM_SPEC, HBM_SPEC), input_output_aliases={0: 0, 1: 1},
        compiler_params=pltpu.CompilerParams(has_side_effects=EFFECT),
    )(src, land, pending["send"], pending["recv"], pending["own"], after)[1]


BIG = ("ffn1_w_in", "ffn1_w_out", "mix_w_in", "w_branch_a", "w_branch_b", "mix_w_out", "ffn2_w_in", "ffn2_w_out")
TRANSPOSED = ("ffn1_w_in", "mix_w_in", "w_branch_b", "ffn2_w_in")
SMALL = ("ln1_g", "ln1_b", "ln2_g", "ln2_b", "ln3_g", "ln3_b", "hgrn_norm_g", "hgrn_lb_fwd", "hgrn_lb_bwd")
SMALL_ROWS = 16


def _local_step(x, target, weight, emit, sp):
    b_, s_, d_ = x.shape
    t_ = b_ * s_
    x2, tgt = x.reshape(t_, d_), target.reshape(t_, d_)
    xb = x2.astype(BF16)
    w1i = weight("ffn1_w_in", xb)
    g1, u1, a1 = _ffn_in_fwd(xb, w1i, "ffn1_in")
    w1o = weight("ffn1_w_out", a1)
    r1, h1, h1b = _mm_res_ln_fwd(a1, w1o, x2, sp["ln1_g"], sp["ln1_b"], 0.5, "ffn1_out_ln1")
    wmx = weight("mix_w_in", h1b)
    proj = _mm_nt(h1b, wmx, F32, "mix_in")
    ya_in, o_sum = _hgrn_fwd(proj, sp["hgrn_lb_fwd"], sp["hgrn_lb_bwd"], sp["hgrn_norm_g"], b_, s_, d_, "hgrn_fwd")
    tabs = _rope_tables(s_)
    qkvs, gtabs, outs, lses = [], [], [], []
    for gi, (_, dil) in enumerate(ATTN_GROUPS):
        off = 5 * d_ + gi * QKV_GROUP
        qkv = proj[:, off:off + QKV_GROUP].reshape(t_ // dil, dil * QKV_GROUP)
        gt = [tb.reshape(s_ // dil, dil * HEAD) for tb in tabs]
        o_g, lse_g = _attn_fwd(qkv, gt, b_, s_, dil, f"attn_fwd_{gi}")
        qkvs.append(qkv)
        gtabs.append(gt)
        outs.append(o_g.reshape(t_, ATTN_OUT))
        lses.append(lse_g.reshape(t_, ATTN_OUT))
    ob = _combine_fwd(outs, lses, "attn_combine")
    wa, wb = weight("w_branch_a", ya_in), weight("w_branch_b", ob)
    ya, yb, z = _gate_out_fwd(ya_in, ob, wa, wb, proj, d_, "branch_gate")
    wo = weight("mix_w_out", z)
    r2, h2, h2b = _mm_res_ln_fwd(z, wo, h1, sp["ln2_g"], sp["ln2_b"], 1.0, "mix_out_ln2")
    w2i = weight("ffn2_w_in", h2b)
    g2, u2, a2 = _ffn_in_fwd(h2b, w2i, "ffn2_in")
    w2o = weight("ffn2_w_out", a2)
    r3, _, _ = _mm_res_ln_fwd(a2, w2o, h2, sp["ln3_g"], sp["ln3_b"], 0.5, "ffn2_out_ln3")
    dr3, dr3b, dg3, db3, loss = _ln_loss_bwd(r3, tgt, sp["ln3_g"], sp["ln3_b"], "loss_ln3_bwd")
    dep = emit("ffn2_w_out", _mm_tn(a2, dr3b, 0.5, "d_ffn2_w_out"))
    dgate2, dup2 = _ffn_mid_bwd(dr3b, w2o, g2, u2, 0.5, "ffn2_mid_bwd", dep)
    du2 = jnp.concatenate([dgate2, dup2], axis=1)
    dep = emit("ffn2_w_in", _mm_tn(du2, h2b, 1.0, "d_ffn2_w_in"))
    dr2, dr2b, dg2, db2 = _mm_nn_res_lnbwd(du2, w2i, dr3, r2, sp["ln2_g"], "ffn2_in_bwd_ln2", dep)
    dep = emit("mix_w_out", _mm_tn(z, dr2b, 1.0, "d_mix_w_out"))
    dya, dyb, dga, dgb = _dz_gate_bwd(dr2b, wo, proj, ya, yb, d_, "branch_gate_bwd", dep)
    dep = emit("w_branch_a", _mm_tn(ya_in, dya, 1.0, "d_w_branch_a"))
    dya_in = _mm_nt(dya, wa, F32, "branch_a_bwd", dep)
    dep = emit("w_branch_b", _mm_tn(dyb, ob, 1.0, "d_w_branch_b"))
    dob = _mm_nn(dyb, wb, F32, "branch_b_bwd", dep)
    dhq, dhff, dhfb, dhi, dhog, dng, dlbf, dlbb = _hgrn_bwd(
        proj, sp["hgrn_lb_fwd"], sp["hgrn_lb_bwd"], sp["hgrn_norm_g"], o_sum, dya_in, b_, s_, d_, "hgrn_bwd")
    dogs, cgs = _combine_bwd(dob, outs, lses, "attn_combine_bwd")
    dqkv = []
    for gi, (_, dil) in enumerate(ATTN_GROUPS):
        seg_rows = t_ // dil
        dq, dk, dv = _attn_bwd(qkvs[gi], gtabs[gi], dogs[gi].reshape(seg_rows, dil * ATTN_OUT),
                               cgs[gi].reshape(seg_rows, dil * ATTN_OUT), lses[gi].reshape(seg_rows, dil * ATTN_OUT),
                               b_, s_, dil, f"attn_bwd_{gi}")
        dqkv += [t.reshape(t_, ATTN_OUT).astype(BF16) for t in (dq, dk, dv)]
    dproj = jnp.concatenate([dhq, dhff, dhfb, dhi, dhog] + dqkv + [dga, dgb], axis=1)
    dep = emit("mix_w_in", _mm_tn(dproj, h1b, 1.0, "d_mix_w_in"))
    dr1, dr1b, dg1, db1 = _mm_nn_res_lnbwd(dproj, wmx, dr2, r1, sp["ln1_g"], "mix_in_bwd_ln1", dep)
    dep = emit("ffn1_w_out", _mm_tn(a1, dr1b, 0.5, "d_ffn1_w_out"))
    dgate1, dup1 = _ffn_mid_bwd(dr1b, w1o, g1, u1, 0.5, "ffn1_mid_bwd", dep)
    du1 = jnp.concatenate([dgate1, dup1], axis=1)
    dep = emit("ffn1_w_in", _mm_tn(du1, xb, 1.0, "d_ffn1_w_in"))
    grad_x = _mm_nn_res(du1, w1i, dr1, "ffn1_in_bwd", dep)
    small = {"ln1_g": dg1, "ln1_b": db1, "ln2_g": dg2, "ln2_b": db2, "ln3_g": dg3, "ln3_b": db3,
             "hgrn_norm_g": dng, "hgrn_lb_fwd": dlbf, "hgrn_lb_bwd": dlbb}
    return loss, grad_x.reshape(b_, s_, d_), small


def _pack_small(vals):
    rows = jnp.concatenate([vals[n] for n in SMALL], axis=0)
    return jnp.pad(rows, ((0, SMALL_ROWS - rows.shape[0]), (0, 0)))


def _unpack_small(packed):
    out, r = {}, 0
    for n in SMALL:
        k = 2 if n.startswith("hgrn_lb") else 1
        out[n] = packed[r:r + k]
        r += k
    return out


def kernel(x, ffn1_w_in, ffn1_w_out, ln1_g, ln1_b, mix_w_in, hgrn_lb_fwd, hgrn_lb_bwd, hgrn_norm_g, w_branch_a, w_branch_b, mix_w_out, ln2_g, ln2_b, ffn2_w_in, ffn2_w_out, ln3_g, ln3_b, loss_target, m_ffn1_w_in, m_ffn1_w_out, m_ln1_g, m_ln1_b, m_mix_w_in, m_hgrn_lb_fwd, m_hgrn_lb_bwd, m_hgrn_norm_g, m_w_branch_a, m_w_branch_b, m_mix_w_out, m_ln2_g, m_ln2_b, m_ffn2_w_in, m_ffn2_w_out, m_ln3_g, m_ln3_b, v_ffn1_w_in, v_ffn1_w_out, v_ln1_g, v_ln1_b, v_mix_w_in, v_hgrn_lb_fwd, v_hgrn_lb_bwd, v_hgrn_norm_g, v_w_branch_a, v_w_branch_b, v_mix_w_out, v_ln2_g, v_ln2_b, v_ffn2_w_in, v_ffn2_w_out, v_ln3_g, v_ln3_b):
    args = dict(locals())
    big_w = {n: args[n][0] for n in BIG}
    sp = {n: args[n] for n in SMALL}
    shards = [(big_w[n].T if n in TRANSPOSED else big_w[n]).astype(BF16) for n in BIG]
    gathering = dict(zip(BIG, _gather_start(shards, "gather_start")))
    scattering = {}

    def weight(n, after):
        return _gather_wait(gathering[n], after, f"gather_wait_{n}")

    def emit(n, grad):
        scattering[n], token = _scatter_start(grad, f"scatter_start_{n}")
        return token

    loss_part, grad_x, small = _local_step(x, loss_target, weight, emit, sp)
    loss = lax.psum(loss_part[0, 0], ("x", "y", "c"))
    out_g, out_d, out_m, out_v = {}, {}, {}, {}
    for n in BIG:
        g = _sum_partials(_scatter_wait(scattering[n], grad_x, f"scatter_wait_{n}"), f"sum_{n}")
        if n in TRANSPOSED:
            g = g.T
        d_w, m_w, v_w = _adam(big_w[n], g, args["m_" + n][0], args["v_" + n][0], f"adam_{n}")
        out_g[n], out_d[n], out_m[n], out_v[n] = g[None], d_w[None], m_w[None], v_w[None]
    (parts,) = _all_gather([_pack_small(small)], "gather_small_grads")
    res = _small_sum_adam(parts.reshape(N_DEV, SMALL_ROWS, parts.shape[1]), _pack_small(sp),
                          _pack_small({n: args["m_" + n] for n in SMALL}),
                          _pack_small({n: args["v_" + n] for n in SMALL}), "small_adam")
    sg, sd, sm, sv = (_unpack_small(r) for r in res)
    out_g.update(sg), out_d.update(sd), out_m.update(sm), out_v.update(sv)
    order = ("ffn1_w_in", "ffn1_w_out", "ln1_g", "ln1_b", "mix_w_in", "hgrn_lb_fwd", "hgrn_lb_bwd", "hgrn_norm_g",
             "w_branch_a", "w_branch_b", "mix_w_out", "ln2_g", "ln2_b", "ffn2_w_in", "ffn2_w_out", "ln3_g", "ln3_b")
    return (loss, grad_x, *[out_g[n] for n in order], *[out_d[n] for n in order],
            *[out_m[n] for n in order], *[out_v[n] for n in order])
```

```python
import functools

import jax
import jax.numpy as jnp
from jax import lax
from jax.experimental import pallas as pl
from jax.experimental.pallas import tpu as pltpu

F32 = jnp.float32
BF16 = jnp.bfloat16

N_DEV = 8
HEAD = 128
CHUNK = 32
ATTN_GROUPS = ((128, 1), (512, 4), (2048, 16))
ATTN_HEADS = 4
ATTN_HALF = 64
QKV_GROUP = 3 * ATTN_HEADS * HEAD
QKV_WIDTH = len(ATTN_GROUPS) * QKV_GROUP
ATTN_OUT = ATTN_HEADS * HEAD
ROPE_THETA = 500000.0
ROPE_DIM = HEAD // 4
ALPHA = 2.0 ** 0.25
LN_EPS = 1e-5
NEG_INF = -1e30
ADAM_LR, ADAM_B1, ADAM_B2, ADAM_EPS, ADAM_WD, ADAM_STEP = 0.001, 0.9, 0.999, 1e-08, 0.01, 10
VMEM_LIMIT = 56 * 1024 * 1024

NT = (((1,), (1,)), ((), ()))
NN = (((1,), (0,)), ((), ()))
TN = (((0,), (0,)), ((), ()))
MESH = pl.DeviceIdType.MESH


def _dot(a, b, dims):
    return lax.dot_general(a, b, dims, preferred_element_type=F32)


def _tile(n, pref, mult=128):
    if n <= pref:
        return n
    t = (pref // mult) * mult
    while t >= mult:
        if n % t == 0:
            return t
        t -= mult
    return n


def _tile_multi(ns, pref, mult=128):
    t = (pref // mult) * mult
    while t >= mult:
        if all(n % t == 0 for n in ns):
            return t
        t -= mult
    raise ValueError(f"no common tile for {ns}")


def _params(**kw):
    return pltpu.CompilerParams(vmem_limit_bytes=VMEM_LIMIT, **kw)


def _after(body, n_in, dep):
    if dep is None:
        return body, [], []

    def wrapped(*refs):
        body(*refs[:n_in], *refs[n_in + 1:])

    return wrapped, [pl.BlockSpec(dep.shape, lambda *_: (0,) * dep.ndim)], [dep]


def _sigmoid(x):
    return jax.nn.sigmoid(x)


def _dsilu(x, s):
    return s * (1.0 + x * (1.0 - s))


def _ln_stats(r):
    mu = jnp.mean(r, axis=-1, keepdims=True)
    xc = r - mu
    var = jnp.mean(xc * xc, axis=-1, keepdims=True)
    rstd = lax.rsqrt(var + LN_EPS)
    return xc * rstd, rstd


def _ln_bwd(dy, xhat, rstd, g):
    dyg = dy * g
    m1 = jnp.mean(dyg, axis=-1, keepdims=True)
    m2 = jnp.mean(dyg * xhat, axis=-1, keepdims=True)
    return rstd * (dyg - m1 - xhat * m2)


ROW_TILE = 1024
SUB_ROWS = 256


def _once(shape, index_map):
    return pl.BlockSpec(shape, index_map, pipeline_mode=pl.Buffered(1))


def _for_row_blocks(tm, fn):
    sub = SUB_ROWS if tm % SUB_ROWS == 0 else tm

    def step(s, carry):
        fn(pl.ds(pl.multiple_of(s * sub, sub), sub))
        return carry

    lax.fori_loop(0, tm // sub, step, 0)


def _ffn_in_fwd(xb, w_t, name):
    t_, d_ = xb.shape
    f_ = w_t.shape[0] // 2
    tm, tn = _tile(t_, ROW_TILE, 8), _tile(f_, 512)
    nj = f_ // tn

    def body(x_ref, wg_ref, wu_ref, g_ref, u_ref, a_ref):
        x = x_ref[...]
        g = _dot(x, wg_ref[...], NT)
        u = _dot(x, wu_ref[...], NT)
        g_ref[...] = g
        u_ref[...] = u
        a_ref[...] = (g * _sigmoid(g) * u).astype(BF16)

    return pl.pallas_call(
        body, name=name, grid=(t_ // tm, nj),
        in_specs=[pl.BlockSpec((tm, d_), lambda i, j: (i, 0)),
                  pl.BlockSpec((tn, d_), lambda i, j: (j, 0)),
                  pl.BlockSpec((tn, d_), lambda i, j: (j + nj, 0))],
        out_specs=[pl.BlockSpec((tm, tn), lambda i, j: (i, j))] * 3,
        out_shape=[jax.ShapeDtypeStruct((t_, f_), F32), jax.ShapeDtypeStruct((t_, f_), F32),
                   jax.ShapeDtypeStruct((t_, f_), BF16)],
        compiler_params=_params(),
    )(xb, w_t, w_t)


def _mm_res_ln_fwd(a, w, res, g, b, scale, name):
    t_, k_ = a.shape
    d_ = w.shape[1]
    tm, tk = _tile(t_, ROW_TILE, 8), _tile(k_, 512)
    nk = k_ // tk

    def body(a_ref, w_ref, res_ref, g_ref, b_ref, r_ref, h_ref, hb_ref, acc):
        k = pl.program_id(1)

        @pl.when(k == 0)
        def _():
            acc[...] = jnp.zeros_like(acc)

        acc[...] += _dot(a_ref[...], w_ref[...], NN)

        @pl.when(k == nk - 1)
        def _():
            def rows_out(rows):
                r = ALPHA * res_ref[rows, :] + scale * acc[rows, :]
                xhat, _ = _ln_stats(r)
                h = xhat * g_ref[...] + b_ref[...]
                r_ref[rows, :] = r
                h_ref[rows, :] = h
                hb_ref[rows, :] = h.astype(BF16)

            _for_row_blocks(tm, rows_out)

    row = _once((tm, d_), lambda i, k: (i, 0))
    vec = pl.BlockSpec((1, d_), lambda i, k: (0, 0))
    return pl.pallas_call(
        body, name=name, grid=(t_ // tm, nk),
        in_specs=[pl.BlockSpec((tm, tk), lambda i, k: (i, k)),
                  pl.BlockSpec((tk, d_), lambda i, k: (k, 0)), row, vec, vec],
        out_specs=[row, row, row],
        out_shape=[jax.ShapeDtypeStruct((t_, d_), F32), jax.ShapeDtypeStruct((t_, d_), F32),
                   jax.ShapeDtypeStruct((t_, d_), BF16)],
        scratch_shapes=[pltpu.VMEM((tm, d_), F32)],
        compiler_params=_params(),
    )(a, w, res, g, b)


def _mm_nt(a, w_t, out_dtype, name, dep=None):
    t_, k_ = a.shape
    n_ = w_t.shape[0]
    tm, tn = _tile(t_, ROW_TILE, 8), _tile(n_, 512)

    def body(a_ref, w_ref, o_ref):
        o_ref[...] = _dot(a_ref[...], w_ref[...], NT).astype(out_dtype)

    body, dep_specs, deps = _after(body, 2, dep)
    return pl.pallas_call(
        body, name=name, grid=(t_ // tm, n_ // tn),
        in_specs=[pl.BlockSpec((tm, k_), lambda i, j: (i, 0)),
                  pl.BlockSpec((tn, k_), lambda i, j: (j, 0)), *dep_specs],
        out_specs=pl.BlockSpec((tm, tn), lambda i, j: (i, j)),
        out_shape=jax.ShapeDtypeStruct((t_, n_), out_dtype),
        compiler_params=_params(),
    )(a, w_t, *deps)


def _mm_nn(a, w, out_dtype, name, dep=None):
    t_, k_ = a.shape
    n_ = w.shape[1]
    tm, tn = _tile(t_, ROW_TILE, 8), _tile(n_, 512)

    def body(a_ref, w_ref, o_ref):
        o_ref[...] = _dot(a_ref[...], w_ref[...], NN).astype(out_dtype)

    body, dep_specs, deps = _after(body, 2, dep)
    return pl.pallas_call(
        body, name=name, grid=(t_ // tm, n_ // tn),
        in_specs=[pl.BlockSpec((tm, k_), lambda i, j: (i, 0)),
                  pl.BlockSpec((k_, tn), lambda i, j: (0, j)), *dep_specs],
        out_specs=pl.BlockSpec((tm, tn), lambda i, j: (i, j)),
        out_shape=jax.ShapeDtypeStruct((t_, n_), out_dtype),
        compiler_params=_params(),
    )(a, w, *deps)


def _mm_tn(a, b, scale, name):
    t_, m_ = a.shape
    n_ = b.shape[1]
    tm = _tile(m_, 512)

    def body(a_ref, b_ref, o_ref):
        o_ref[...] = (scale * _dot(a_ref[...], b_ref[...], TN)).astype(BF16)

    return pl.pallas_call(
        body, name=name, grid=(m_ // tm,),
        in_specs=[pl.BlockSpec((t_, tm), lambda i: (0, i)), _once((t_, n_), lambda i: (0, 0))],
        out_specs=pl.BlockSpec((tm, n_), lambda i: (i, 0)),
        out_shape=jax.ShapeDtypeStruct((m_, n_), BF16),
        compiler_params=_params(),
    )(a, b)


def _gate_out_fwd(ya_in, ob, wa, wb_t, proj, d_, name):
    t_ = ya_in.shape[0]
    goff = 5 * d_ + QKV_WIDTH
    tm, tn = _tile(t_, ROW_TILE, 8), _tile_multi([d_, goff], 512)
    ja, jb = goff // tn, (goff + d_) // tn

    def body(ya_ref, ob_ref, wa_ref, wb_ref, ga_ref, gb_ref, yao_ref, ybo_ref, z_ref):
        y_a = _dot(ya_ref[...], wa_ref[...], NN)
        y_b = _dot(ob_ref[...], wb_ref[...], NT)
        yao_ref[...] = y_a
        ybo_ref[...] = y_b
        z_ref[...] = (_sigmoid(ga_ref[...]) * y_a + _sigmoid(gb_ref[...]) * y_b).astype(BF16)

    tile = pl.BlockSpec((tm, tn), lambda i, j: (i, j))
    return pl.pallas_call(
        body, name=name, grid=(t_ // tm, d_ // tn),
        in_specs=[pl.BlockSpec((tm, d_), lambda i, j: (i, 0)),
                  pl.BlockSpec((tm, ATTN_OUT), lambda i, j: (i, 0)),
                  pl.BlockSpec((d_, tn), lambda i, j: (0, j)),
                  pl.BlockSpec((tn, ATTN_OUT), lambda i, j: (j, 0)),
                  pl.BlockSpec((tm, tn), lambda i, j: (i, ja + j)),
                  pl.BlockSpec((tm, tn), lambda i, j: (i, jb + j))],
        out_specs=[tile, tile, tile],
        out_shape=[jax.ShapeDtypeStruct((t_, d_), F32), jax.ShapeDtypeStruct((t_, d_), F32),
                   jax.ShapeDtypeStruct((t_, d_), BF16)],
        compiler_params=_params(),
    )(ya_in, ob, wa, wb_t, proj, proj)


def _ffn_mid_bwd(drb, w_out, gate, up, scale, name, dep=None):
    t_, d_ = drb.shape
    f_ = w_out.shape[0]
    tm, tn = _tile(t_, ROW_TILE, 8), _tile(f_, 512)

    def body(dr_ref, w_ref, g_ref, u_ref, dg_ref, du_ref):
        da = scale * _dot(dr_ref[...], w_ref[...], NT)
        g = g_ref[...]
        s = _sigmoid(g)
        dg_ref[...] = (da * u_ref[...] * _dsilu(g, s)).astype(BF16)
        du_ref[...] = (da * g * s).astype(BF16)

    body, dep_specs, deps = _after(body, 4, dep)
    tile = pl.BlockSpec((tm, tn), lambda i, j: (i, j))
    return pl.pallas_call(
        body, name=name, grid=(t_ // tm, f_ // tn),
        in_specs=[pl.BlockSpec((tm, d_), lambda i, j: (i, 0)),
                  pl.BlockSpec((tn, d_), lambda i, j: (j, 0)), tile, tile, *dep_specs],
        out_specs=[tile, tile],
        out_shape=[jax.ShapeDtypeStruct((t_, f_), BF16)] * 2,
        compiler_params=_params(),
    )(drb, w_out, gate, up, *deps)


def _mm_nn_res_lnbwd(a, w, dres, r, g, name, dep=None):
    t_, k_ = a.shape
    d_ = w.shape[1]
    tm, tk = _tile(t_, ROW_TILE, 8), _tile(k_, 512)
    nk = k_ // tk

    def body(a_ref, w_ref, dres_ref, r_ref, g_ref, dr_ref, drb_ref, dg_ref, db_ref, acc):
        i, k = pl.program_id(0), pl.program_id(1)

        @pl.when(k == 0)
        def _():
            acc[...] = jnp.zeros_like(acc)

        @pl.when((i == 0) & (k == 0))
        def _():
            dg_ref[...] = jnp.zeros_like(dg_ref)
            db_ref[...] = jnp.zeros_like(db_ref)

        acc[...] += _dot(a_ref[...], w_ref[...], NN)

        @pl.when(k == nk - 1)
        def _():
            def rows_out(rows):
                dy = acc[rows, :] + ALPHA * dres_ref[rows, :]
                xhat, rstd = _ln_stats(r_ref[rows, :])
                dr = _ln_bwd(dy, xhat, rstd, g_ref[...])
                dr_ref[rows, :] = dr
                drb_ref[rows, :] = dr.astype(BF16)
                dg_ref[...] += jnp.sum(dy * xhat, axis=0, keepdims=True)
                db_ref[...] += jnp.sum(dy, axis=0, keepdims=True)

            _for_row_blocks(tm, rows_out)

    body, dep_specs, deps = _after(body, 5, dep)
    row = _once((tm, d_), lambda i, k: (i, 0))
    vec = pl.BlockSpec((1, d_), lambda i, k: (0, 0))
    return pl.pallas_call(
        body, name=name, grid=(t_ // tm, nk),
        in_specs=[pl.BlockSpec((tm, tk), lambda i, k: (i, k)),
                  pl.BlockSpec((tk, d_), lambda i, k: (k, 0)), row, row, vec, *dep_specs],
        out_specs=[row, row, vec, vec],
        out_shape=[jax.ShapeDtypeStruct((t_, d_), F32), jax.ShapeDtypeStruct((t_, d_), BF16),
                   jax.ShapeDtypeStruct((1, d_), F32), jax.ShapeDtypeStruct((1, d_), F32)],
        scratch_shapes=[pltpu.VMEM((tm, d_), F32)],
        compiler_params=_params(),
    )(a, w, dres, r, g, *deps)


def _mm_nn_res(a, w, dres, name, dep=None):
    t_, k_ = a.shape
    d_ = w.shape[1]
    tm, tk = _tile(t_, ROW_TILE, 8), _tile(k_, 512)
    nk = k_ // tk

    def body(a_ref, w_ref, dres_ref, o_ref, acc):
        k = pl.program_id(1)

        @pl.when(k == 0)
        def _():
            acc[...] = jnp.zeros_like(acc)

        acc[...] += _dot(a_ref[...], w_ref[...], NN)

        @pl.when(k == nk - 1)
        def _():
            def rows_out(rows):
                o_ref[rows, :] = acc[rows, :] + ALPHA * dres_ref[rows, :]

            _for_row_blocks(tm, rows_out)

    body, dep_specs, deps = _after(body, 3, dep)
    row = _once((tm, d_), lambda i, k: (i, 0))
    return pl.pallas_call(
        body, name=name, grid=(t_ // tm, nk),
        in_specs=[pl.BlockSpec((tm, tk), lambda i, k: (i, k)),
                  pl.BlockSpec((tk, d_), lambda i, k: (k, 0)), row, *dep_specs],
        out_specs=row,
        out_shape=jax.ShapeDtypeStruct((t_, d_), F32),
        scratch_shapes=[pltpu.VMEM((tm, d_), F32)],
        compiler_params=_params(),
    )(a, w, dres, *deps)


def _dz_gate_bwd(drb, w_out, proj, ya, yb, d_, name, dep=None):
    t_ = drb.shape[0]
    goff = 5 * d_ + QKV_WIDTH
    tm, tn = _tile(t_, ROW_TILE, 8), _tile_multi([d_, goff], 512)
    ja, jb = goff // tn, (goff + d_) // tn

    def body(dr_ref, w_ref, ga_ref, gb_ref, ya_ref, yb_ref, dya_ref, dyb_ref, dga_ref, dgb_ref):
        dz = _dot(dr_ref[...], w_ref[...], NT)
        sa, sb = _sigmoid(ga_ref[...]), _sigmoid(gb_ref[...])
        dya_ref[...] = (dz * sa).astype(BF16)
        dyb_ref[...] = (dz * sb).astype(BF16)
        dga_ref[...] = (dz * ya_ref[...] * sa * (1.0 - sa)).astype(BF16)
        dgb_ref[...] = (dz * yb_ref[...] * sb * (1.0 - sb)).astype(BF16)

    body, dep_specs, deps = _after(body, 6, dep)
    tile = pl.BlockSpec((tm, tn), lambda i, j: (i, j))
    return pl.pallas_call(
        body, name=name, grid=(t_ // tm, d_ // tn),
        in_specs=[pl.BlockSpec((tm, d_), lambda i, j: (i, 0)),
                  pl.BlockSpec((tn, d_), lambda i, j: (j, 0)),
                  pl.BlockSpec((tm, tn), lambda i, j: (i, ja + j)),
                  pl.BlockSpec((tm, tn), lambda i, j: (i, jb + j)), tile, tile, *dep_specs],
        out_specs=[tile] * 4,
        out_shape=[jax.ShapeDtypeStruct((t_, d_), BF16)] * 4,
        compiler_params=_params(),
    )(drb, w_out, proj, proj, ya, yb, *deps)


def _ln_loss_bwd(r, target, g, b, name):
    t_, d_ = r.shape
    tm = _tile(t_, 256, 8)

    def body(r_ref, t_ref, g_ref, b_ref, dr_ref, drb_ref, dg_ref, db_ref, loss_ref):
        i = pl.program_id(0)

        @pl.when(i == 0)
        def _():
            dg_ref[...] = jnp.zeros_like(dg_ref)
            db_ref[...] = jnp.zeros_like(db_ref)
            loss_ref[...] = jnp.zeros_like(loss_ref)

        xhat, rstd = _ln_stats(r_ref[...])
        gain = g_ref[...]
        err = xhat * gain + b_ref[...] - t_ref[...]
        loss_ref[...] += (0.5 / d_) * jnp.sum(err * err)
        dy = err * (1.0 / d_)
        dr = _ln_bwd(dy, xhat, rstd, gain)
        dr_ref[...] = dr
        drb_ref[...] = dr.astype(BF16)
        dg_ref[...] += jnp.sum(dy * xhat, axis=0, keepdims=True)
        db_ref[...] += jnp.sum(dy, axis=0, keepdims=True)

    row = pl.BlockSpec((tm, d_), lambda i: (i, 0))
    vec = pl.BlockSpec((1, d_), lambda i: (0, 0))
    return pl.pallas_call(
        body, name=name, grid=(t_ // tm,),
        in_specs=[row, row, vec, vec],
        out_specs=[row, row, vec, vec, pl.BlockSpec((1, HEAD), lambda i: (0, 0))],
        out_shape=[jax.ShapeDtypeStruct((t_, d_), F32), jax.ShapeDtypeStruct((t_, d_), BF16),
                   jax.ShapeDtypeStruct((1, d_), F32), jax.ShapeDtypeStruct((1, d_), F32),
                   jax.ShapeDtypeStruct((1, HEAD), F32)],
        compiler_params=_params(),
    )(r, target, g, b)


def _chunk_scan(x, row, reverse, size):
    s = 1
    while s < CHUNK:
        if reverse:
            x = x + jnp.where(row < CHUNK - s, pltpu.roll(x, size - s, 0), 0.0)
        else:
            x = x + jnp.where(row >= s, pltpu.roll(x, s, 0), 0.0)
        s *= 2
    return x


def _lower_bound(tab):
    return _sigmoid(tab[0:1, :] - tab[1:2, :])


def _tri_mask(reverse):
    r = lax.broadcasted_iota(jnp.int32, (CHUNK, CHUNK), 0)
    c = lax.broadcasted_iota(jnp.int32, (CHUNK, CHUNK), 1)
    return (c >= r) if reverse else (r >= c)


def _hgrn_fwd(proj, lbf, lbb, ng, b_, s_, d_, name):
    h_ = d_ // HEAD
    nc = s_ // CHUNK

    def body(hq_ref, hff_ref, hfb_ref, hi_ref, hog_ref, lbf_ref, lbb_ref, ng_ref, ya_ref, o_ref,
             q_s, k_s, cum_s, o_s):
        row = lax.broadcasted_iota(jnp.int32, (s_, HEAD), 0) % CHUNK
        hq = hq_ref[...]
        q_s[...] = hq * _sigmoid(hq)
        o_s[...] = jnp.zeros_like(o_s)
        for reverse, hf_ref, lb_ref in ((False, hff_ref, lbf_ref), (True, hfb_ref, lbb_ref)):
            lb = _lower_bound(lb_ref[...])
            f = lb + (1.0 - lb) * _sigmoid(hf_ref[...])
            k_s[...] = 1.0 - f
            cum_s[...] = _chunk_scan(jnp.log(f), row, reverse, s_)
            mask = _tri_mask(reverse)

            def step(n, st, reverse=reverse, mask=mask):
                idx = (nc - 1 - n) if reverse else n
                sl = pl.ds(pl.multiple_of(idx * CHUNK, CHUNK), CHUNK)
                cm = cum_s[sl, :]
                tot = cm[0:1, :] if reverse else cm[CHUNK - 1:CHUNK, :]
                qc, kc = q_s[sl, :], k_s[sl, :]
                vb = hi_ref[sl, :].astype(BF16)
                qd = (qc * jnp.exp(cm)).astype(BF16)
                kd = (kc * jnp.exp(-cm)).astype(BF16)
                ke = (kc * jnp.exp(tot - cm)).astype(BF16)
                a = jnp.where(mask, _dot(qd, kd, NT), 0.0)
                o_s[sl, :] += _dot(a.astype(BF16), vb, NN) + _dot(qd, st.astype(BF16), NT)
                return st * jnp.exp(tot) + _dot(vb, ke, TN)

            lax.fori_loop(0, nc, step, jnp.zeros((HEAD, HEAD), F32))
        o = o_s[...]
        o_ref[...] = o
        nrm = o * lax.rsqrt(jnp.mean(o * o, axis=-1, keepdims=True) + LN_EPS)
        hog = hog_ref[...]
        ya_ref[...] = (nrm * ng_ref[...] * hog * _sigmoid(hog)).astype(BF16)

    def col(part):
        return pl.BlockSpec((s_, HEAD), lambda h, b, part=part: (b, part * h_ + h))

    tab = pl.BlockSpec((2, HEAD), lambda h, b: (0, h))
    out = pl.BlockSpec((s_, HEAD), lambda h, b: (b, h))
    return pl.pallas_call(
        body, name=name, grid=(h_, b_),
        in_specs=[col(0), col(1), col(2), col(3), col(4), tab, tab,
                  pl.BlockSpec((1, HEAD), lambda h, b: (0, h))],
        out_specs=[out, out],
        out_shape=[jax.ShapeDtypeStruct((b_ * s_, d_), BF16), jax.ShapeDtypeStruct((b_ * s_, d_), F32)],
        scratch_shapes=[pltpu.VMEM((s_, HEAD), F32)] * 4,
        compiler_params=_params(),
    )(proj, proj, proj, proj, proj, lbf, lbb, ng)


def _hgrn_bwd(proj, lbf, lbb, ng, o_sum, dya, b_, s_, d_, name):
    h_ = d_ // HEAD
    nc = s_ // CHUNK

    def body(hq_ref, hff_ref, hfb_ref, hi_ref, hog_ref, lbf_ref, lbb_ref, ng_ref, o_ref, dya_ref,
             dhq_ref, dhff_ref, dhfb_ref, dhi_ref, dhog_ref, dng_ref, dlbf_ref, dlbb_ref,
             q_s, k_s, cum_s, do_s, dq_s, dv_s, db_s, dk_s, st_s):
        b = pl.program_id(1)

        @pl.when(b == 0)
        def _():
            dng_ref[...] = jnp.zeros_like(dng_ref)
            dlbf_ref[...] = jnp.zeros_like(dlbf_ref)
            dlbb_ref[...] = jnp.zeros_like(dlbb_ref)

        row = lax.broadcasted_iota(jnp.int32, (s_, HEAD), 0) % CHUNK
        crow = lax.broadcasted_iota(jnp.int32, (CHUNK, HEAD), 0)
        hq = hq_ref[...]
        sq = _sigmoid(hq)
        q_s[...] = hq * sq
        o = o_ref[...]
        rinv = lax.rsqrt(jnp.mean(o * o, axis=-1, keepdims=True) + LN_EPS)
        nrm = o * rinv
        hog = hog_ref[...]
        so = _sigmoid(hog)
        gain = ng_ref[...]
        dy = dya_ref[...]
        dhog_ref[...] = (dy * nrm * gain * _dsilu(hog, so)).astype(BF16)
        dng_ref[...] += jnp.sum(dy * nrm * hog * so, axis=0, keepdims=True)
        dn = dy * gain * hog * so
        do_s[...] = rinv * (dn - nrm * jnp.mean(dn * nrm, axis=-1, keepdims=True))
        dq_s[...] = jnp.zeros_like(dq_s)
        dv_s[...] = jnp.zeros_like(dv_s)

        for reverse, hf_ref, lb_ref, dhf_ref, dlb_ref in (
                (False, hff_ref, lbf_ref, dhff_ref, dlbf_ref), (True, hfb_ref, lbb_ref, dhfb_ref, dlbb_ref)):
            tab = lb_ref[...]
            lb = _lower_bound(tab)
            sf = _sigmoid(hf_ref[...])
            f = lb + (1.0 - lb) * sf
            k_s[...] = 1.0 - f
            cum_s[...] = _chunk_scan(jnp.log(f), row, reverse, s_)
            mask = _tri_mask(reverse)
            last = 0 if reverse else CHUNK - 1

            def chunk(idx, reverse=reverse):
                sl = pl.ds(pl.multiple_of(idx * CHUNK, CHUNK), CHUNK)
                cm = cum_s[sl, :]
                tot = cm[0:1, :] if reverse else cm[CHUNK - 1:CHUNK, :]
                return sl, cm, tot

            def fstep(n, st, reverse=reverse, chunk=chunk):
                idx = (nc - 1 - n) if reverse else n
                sl, cm, tot = chunk(idx)
                st_s[idx] = st
                ke = (k_s[sl, :] * jnp.exp(tot - cm)).astype(BF16)
                return st * jnp.exp(tot) + _dot(hi_ref[sl, :].astype(BF16), ke, TN)

            lax.fori_loop(0, nc, fstep, jnp.zeros((HEAD, HEAD), F32))

            def bstep(n, dst, reverse=reverse, chunk=chunk, mask=mask, last=last):
                idx = n if reverse else (nc - 1 - n)
                sl, cm, tot = chunk(idx)
                eb, enb, ee, dec = jnp.exp(cm), jnp.exp(-cm), jnp.exp(tot - cm), jnp.exp(tot)
                qc, kc = q_s[sl, :], k_s[sl, :]
                qd, kd, ke = qc * eb, kc * enb, kc * ee
                qdb, kdb, keb = qd.astype(BF16), kd.astype(BF16), ke.astype(BF16)
                vb = hi_ref[sl, :].astype(BF16)
                dob = do_s[sl, :].astype(BF16)
                st0 = st_s[idx]
                dstb = dst.astype(BF16)
                a = jnp.where(mask, _dot(qdb, kdb, NT), 0.0).astype(BF16)
                da = jnp.where(mask, _dot(dob, vb, NT), 0.0).astype(BF16)
                dqd = _dot(da, kdb, NN) + _dot(dob, st0.astype(BF16), NN)
                dkd = _dot(da, qdb, TN)
                dv = _dot(a, dob, TN) + _dot(keb, dstb, NT)
                dke = _dot(vb, dstb, NN)
                ddec = jnp.sum(dst * st0, axis=0, keepdims=True)
                dtot = jnp.sum(dke * ke, axis=0, keepdims=True) + ddec * dec
                db = dqd * qd - dkd * kd - dke * ke
                db_s[sl, :] = db + jnp.where(crow == last, dtot, 0.0)
                dk_s[sl, :] = dkd * enb + dke * ee
                dq_s[sl, :] += dqd * eb
                dv_s[sl, :] += dv
                return dst * dec + _dot(dob, qdb, TN)

            lax.fori_loop(0, nc, bstep, jnp.zeros((HEAD, HEAD), F32))
            dlogf = _chunk_scan(db_s[...], row, not reverse, s_)
            df = dlogf / f - dk_s[...]
            dhf_ref[...] = (df * (1.0 - lb) * sf * (1.0 - sf)).astype(BF16)
            dlb = jnp.sum(df * (1.0 - sf), axis=0, keepdims=True) * lb * (1.0 - lb)
            dlb_ref[0:1, :] += dlb
            dlb_ref[1:2, :] -= dlb

        dhq_ref[...] = (dq_s[...] * _dsilu(hq, sq)).astype(BF16)
        dhi_ref[...] = dv_s[...].astype(BF16)

    def col(part):
        return pl.BlockSpec((s_, HEAD), lambda h, b, part=part: (b, part * h_ + h))

    tab = pl.BlockSpec((2, HEAD), lambda h, b: (0, h))
    vec = pl.BlockSpec((1, HEAD), lambda h, b: (0, h))
    blk = pl.BlockSpec((s_, HEAD), lambda h, b: (b, h))
    act = jax.ShapeDtypeStruct((b_ * s_, d_), BF16)
    return pl.pallas_call(
        body, name=name, grid=(h_, b_),
        in_specs=[col(0), col(1), col(2), col(3), col(4), tab, tab, vec, blk, blk],
        out_specs=[blk] * 5 + [vec, tab, tab],
        out_shape=[act] * 5 + [jax.ShapeDtypeStruct((1, d_), F32), jax.ShapeDtypeStruct((2, d_), F32),
                               jax.ShapeDtypeStruct((2, d_), F32)],
        scratch_shapes=[pltpu.VMEM((s_, HEAD), F32)] * 8 + [pltpu.VMEM((nc, HEAD, HEAD), F32)],
        compiler_params=_params(),
    )(proj, proj, proj, proj, proj, lbf, lbb, ng, o_sum, dya)


BLOCK = 64
HALF = BLOCK // 2


def _block_scan(x, row, reverse, size):
    s = 1
    while s < BLOCK:
        if reverse:
            x = x + jnp.where(row < BLOCK - s, pltpu.roll(x, size - s, 0), 0.0)
        else:
            x = x + jnp.where(row >= s, pltpu.roll(x, s, 0), 0.0)
        s *= 2
    return x


def _block_exps(l, reverse):
    first = lax.broadcasted_iota(jnp.int32, (BLOCK, HEAD), 0) < HALF
    q1, q3 = HALF // 2, HALF + HALF // 2
    if reverse:
        rho1, rho2, lh, ltot = l[q1:q1 + 1], l[q3:q3 + 1], l[HALF:HALF + 1], l[0:1]
    else:
        rho1, rho2, lh, ltot = l[q1 - 1:q1], l[q3 - 1:q3], l[HALF - 1:HALF], l[BLOCK - 1:BLOCK]
    ref = jnp.where(first, rho1, rho2)
    query_half = first if reverse else jnp.logical_not(first)
    e2 = jnp.where(query_half, jnp.exp(jnp.minimum(l - lh, 0.0)), 0.0)
    e1 = jnp.where(query_half, 0.0, jnp.exp(jnp.minimum(lh - l, 0.0)))
    return jnp.exp(l - ref), jnp.exp(ref - l), e2, e1, jnp.exp(l), jnp.exp(ltot - l), jnp.exp(ltot)


def _half_mask(reverse):
    r = lax.broadcasted_iota(jnp.int32, (BLOCK, BLOCK), 0)
    c = lax.broadcasted_iota(jnp.int32, (BLOCK, BLOCK), 1)
    same = (r < HALF) == (c < HALF)
    return same & ((c >= r) if reverse else (r >= c))


def _hgrn_fwd(proj, lbf, lbb, ng, b_, s_, d_, name):
    h_ = d_ // HEAD
    nb = s_ // BLOCK

    def body(hq_ref, hff_ref, hfb_ref, hi_ref, hog_ref, lbf_ref, lbb_ref, ng_ref, ya_ref, o_ref,
             q_s, k_s, l_s, of_s, oi_s, qd_s, u_s, st_s, dec_s):
        row = lax.broadcasted_iota(jnp.int32, (s_, HEAD), 0) % BLOCK
        hq = hq_ref[...]
        q_s[...] = hq * _sigmoid(hq)
        for reverse, hf_ref, lb_ref in ((False, hff_ref, lbf_ref), (True, hfb_ref, lbb_ref)):
            lb = _lower_bound(lb_ref[...])
            f = lb + (1.0 - lb) * _sigmoid(hf_ref[...])
            k_s[...] = 1.0 - f
            l_s[...] = _block_scan(jnp.log(f), row, reverse, s_)
            mask = _half_mask(reverse)

            def inside(n, carry, reverse=reverse, mask=mask):
                sl = pl.ds(pl.multiple_of(n * BLOCK, BLOCK), BLOCK)
                eq, ek, e2, e1, el, ee, dec = _block_exps(l_s[sl, :], reverse)
                qc, kc = q_s[sl, :], k_s[sl, :]
                vb = hi_ref[sl, :].astype(BF16)
                a = jnp.where(mask, _dot((qc * eq).astype(BF16), (kc * ek).astype(BF16), NT), 0.0)
                a = a + _dot((qc * e2).astype(BF16), (kc * e1).astype(BF16), NT)
                oi_s[sl, :] = _dot(a.astype(BF16), vb, NN)
                qd_s[sl, :] = (qc * el).astype(BF16)
                u_s[n] = _dot(vb, (kc * ee).astype(BF16), TN)
                dec_s[n] = jnp.broadcast_to(dec, (8, HEAD))
                return carry

            lax.fori_loop(0, nb, inside, 0, unroll=4)

            def carry_state(n, st, reverse=reverse):
                idx = (nb - 1 - n) if reverse else n
                st_s[idx] = st.astype(BF16)
                return st * dec_s[idx][0:1, :] + u_s[idx]

            lax.fori_loop(0, nb, carry_state, jnp.zeros((HEAD, HEAD), F32))

            def across(n, carry, reverse=reverse):
                sl = pl.ds(pl.multiple_of(n * BLOCK, BLOCK), BLOCK)
                o_dir = oi_s[sl, :] + _dot(qd_s[sl, :], st_s[n], NT)
                if not reverse:
                    of_s[sl, :] = o_dir
                else:
                    o = of_s[sl, :] + o_dir
                    o_ref[sl, :] = o
                    nrm = o * lax.rsqrt(jnp.mean(o * o, axis=-1, keepdims=True) + LN_EPS)
                    hog = hog_ref[sl, :]
                    ya_ref[sl, :] = (nrm * ng_ref[...] * hog * _sigmoid(hog)).astype(BF16)
                return carry

            lax.fori_loop(0, nb, across, 0, unroll=8)

    def col(part):
        return pl.BlockSpec((s_, HEAD), lambda h, b, part=part: (b, part * h_ + h))

    tab = pl.BlockSpec((2, HEAD), lambda h, b: (0, h))
    out = pl.BlockSpec((s_, HEAD), lambda h, b: (b, h))
    return pl.pallas_call(
        body, name=name, grid=(h_, b_),
        in_specs=[col(0), col(1), col(2), col(3), col(4), tab, tab,
                  pl.BlockSpec((1, HEAD), lambda h, b: (0, h))],
        out_specs=[out, out],
        out_shape=[jax.ShapeDtypeStruct((b_ * s_, d_), BF16), jax.ShapeDtypeStruct((b_ * s_, d_), F32)],
        scratch_shapes=[pltpu.VMEM((s_, HEAD), F32)] * 5 + [
            pltpu.VMEM((s_, HEAD), BF16), pltpu.VMEM((nb, HEAD, HEAD), F32), pltpu.VMEM((nb, HEAD, HEAD), BF16),
            pltpu.VMEM((nb, 8, HEAD), F32)],
        compiler_params=_params(),
    )(proj, proj, proj, proj, proj, lbf, lbb, ng)


def _hgrn_bwd(proj, lbf, lbb, ng, o_sum, dya, b_, s_, d_, name):
    h_ = d_ // HEAD
    nb = s_ // BLOCK

    def body(hq_ref, hff_ref, hfb_ref, hi_ref, hog_ref, lbf_ref, lbb_ref, ng_ref, o_ref, dya_ref,
             dhq_ref, dhff_ref, dhfb_ref, dhi_ref, dhog_ref, dng_ref, dlbf_ref, dlbb_ref,
             q_s, k_s, l_s, do_s, dq_s, dv_s, dl_s, dk_s, u_s, w_s, st_s, dst_s, dec_s):
        b = pl.program_id(1)

        @pl.when(b == 0)
        def _():
            dng_ref[...] = jnp.zeros_like(dng_ref)
            dlbf_ref[...] = jnp.zeros_like(dlbf_ref)
            dlbb_ref[...] = jnp.zeros_like(dlbb_ref)

        row = lax.broadcasted_iota(jnp.int32, (s_, HEAD), 0) % BLOCK
        brow = lax.broadcasted_iota(jnp.int32, (BLOCK, HEAD), 0)
        hq = hq_ref[...]
        sq = _sigmoid(hq)
        q_s[...] = hq * sq
        o = o_ref[...]
        rinv = lax.rsqrt(jnp.mean(o * o, axis=-1, keepdims=True) + LN_EPS)
        nrm = o * rinv
        hog = hog_ref[...]
        so = _sigmoid(hog)
        gain = ng_ref[...]
        dy = dya_ref[...]
        dhog_ref[...] = (dy * nrm * gain * _dsilu(hog, so)).astype(BF16)
        dng_ref[...] += jnp.sum(dy * nrm * hog * so, axis=0, keepdims=True)
        dn = dy * gain * hog * so
        do_s[...] = rinv * (dn - nrm * jnp.mean(dn * nrm, axis=-1, keepdims=True))

        for reverse, hf_ref, lb_ref, dhf_ref, dlb_ref in (
                (False, hff_ref, lbf_ref, dhff_ref, dlbf_ref), (True, hfb_ref, lbb_ref, dhfb_ref, dlbb_ref)):
            lb = _lower_bound(lb_ref[...])
            sf = _sigmoid(hf_ref[...])
            f = lb + (1.0 - lb) * sf
            k_s[...] = 1.0 - f
            l_s[...] = _block_scan(jnp.log(f), row, reverse, s_)
            mask = _half_mask(reverse)
            total_row = 0 if reverse else BLOCK - 1

            def prepare(n, carry, reverse=reverse):
                sl = pl.ds(pl.multiple_of(n * BLOCK, BLOCK), BLOCK)
                _, _, _, _, el, ee, dec = _block_exps(l_s[sl, :], reverse)
                vb = hi_ref[sl, :].astype(BF16)
                u_s[n] = _dot(vb, (k_s[sl, :] * ee).astype(BF16), TN)
                w_s[n] = _dot(do_s[sl, :].astype(BF16), (q_s[sl, :] * el).astype(BF16), TN)
                dec_s[n] = jnp.broadcast_to(dec, (8, HEAD))
                return carry

            lax.fori_loop(0, nb, prepare, 0, unroll=4)

            def carry_state(n, st, reverse=reverse):
                idx = (nb - 1 - n) if reverse else n
                st_s[idx] = st
                return st * dec_s[idx][0:1, :] + u_s[idx]

            lax.fori_loop(0, nb, carry_state, jnp.zeros((HEAD, HEAD), F32))

            def carry_grad(n, dst, reverse=reverse):
                idx = n if reverse else (nb - 1 - n)
                dst_s[idx] = dst
                return dst * dec_s[idx][0:1, :] + w_s[idx]

            lax.fori_loop(0, nb, carry_grad, jnp.zeros((HEAD, HEAD), F32))

            def inside(n, carry, reverse=reverse, mask=mask, total_row=total_row):
                sl = pl.ds(pl.multiple_of(n * BLOCK, BLOCK), BLOCK)
                eq, ek, e2, e1, el, ee, dec = _block_exps(l_s[sl, :], reverse)
                qc, kc = q_s[sl, :], k_s[sl, :]
                vb = hi_ref[sl, :].astype(BF16)
                dob = do_s[sl, :].astype(BF16)
                qt, kt, q2, k1 = ((qc * eq).astype(BF16), (kc * ek).astype(BF16),
                                  (qc * e2).astype(BF16), (kc * e1).astype(BF16))
                kend = kc * ee
                st0, dst1 = st_s[n], dst_s[n]
                dstb = dst1.astype(BF16)
                a = jnp.where(mask, _dot(qt, kt, NT), 0.0) + _dot(q2, k1, NT)
                da = _dot(dob, vb, NT)
                dab = da.astype(BF16)
                dad = jnp.where(mask, da, 0.0).astype(BF16)
                dqt, dkt = _dot(dad, kt, NN), _dot(dad, qt, TN)
                dq2, dk1 = _dot(dab, k1, NN), _dot(dab, q2, TN)
                dqd = _dot(dob, st0.astype(BF16), NN)
                dke = _dot(vb, dstb, NN)
                dv = _dot(a.astype(BF16), dob, TN) + _dot(kend.astype(BF16), dstb, NT)
                dq = dqt * eq + dq2 * e2 + dqd * el
                dk = dkt * ek + dk1 * e1 + dke * ee
                dtot = jnp.sum(dke * kend, axis=0, keepdims=True) + jnp.sum(dst1 * st0, axis=0, keepdims=True) * dec
                dl_s[sl, :] = qc * dq - kc * dk + jnp.where(brow == total_row, dtot, 0.0)
                dk_s[sl, :] = dk
                if not reverse:
                    dq_s[sl, :] = dq
                    dv_s[sl, :] = dv
                else:
                    hqc = hq_ref[sl, :]
                    dhq_ref[sl, :] = ((dq_s[sl, :] + dq) * _dsilu(hqc, _sigmoid(hqc))).astype(BF16)
                    dhi_ref[sl, :] = (dv_s[sl, :] + dv).astype(BF16)
                return carry

            lax.fori_loop(0, nb, inside, 0, unroll=2)
            dlogf = _block_scan(dl_s[...], row, not reverse, s_)
            df = dlogf / f - dk_s[...]
            dhf_ref[...] = (df * (1.0 - lb) * sf * (1.0 - sf)).astype(BF16)
            dlb = jnp.sum(df * (1.0 - sf), axis=0, keepdims=True) * lb * (1.0 - lb)
            dlb_ref[0:1, :] += dlb
            dlb_ref[1:2, :] -= dlb

    def col(part):
        return pl.BlockSpec((s_, HEAD), lambda h, b, part=part: (b, part * h_ + h))

    tab = pl.BlockSpec((2, HEAD), lambda h, b: (0, h))
    vec = pl.BlockSpec((1, HEAD), lambda h, b: (0, h))
    blk = pl.BlockSpec((s_, HEAD), lambda h, b: (b, h))
    act = jax.ShapeDtypeStruct((b_ * s_, d_), BF16)
    state = pltpu.VMEM((nb, HEAD, HEAD), F32)
    return pl.pallas_call(
        body, name=name, grid=(h_, b_),
        in_specs=[col(0), col(1), col(2), col(3), col(4), tab, tab, vec, blk, blk],
        out_specs=[blk] * 5 + [vec, tab, tab],
        out_shape=[act] * 5 + [jax.ShapeDtypeStruct((1, d_), F32), jax.ShapeDtypeStruct((2, d_), F32),
                               jax.ShapeDtypeStruct((2, d_), F32)],
        scratch_shapes=[pltpu.VMEM((s_, HEAD), F32)] * 8 + [state] * 4 + [pltpu.VMEM((nb, 8, HEAD), F32)],
        compiler_params=_params(),
    )(proj, proj, proj, proj, proj, lbf, lbb, ng, o_sum, dya)


def _rope_tables(s_):
    half = ROPE_DIM // 2
    inv_freq = ROPE_THETA ** (-jnp.arange(0, ROPE_DIM, 2, dtype=F32) / ROPE_DIM)
    ang = jnp.arange(s_, dtype=F32)[:, None] * inv_freq
    cos, sin = jnp.cos(ang), jnp.sin(ang)
    zeros = jnp.zeros((s_, HEAD - ROPE_DIM), F32)
    zh = jnp.zeros((s_, half), F32)
    c = jnp.concatenate([cos, cos, jnp.ones((s_, HEAD - ROPE_DIM), F32)], axis=1)
    s1 = jnp.concatenate([-sin, zh, zeros], axis=1)
    s2 = jnp.concatenate([zh, sin, zeros], axis=1)
    return c, s1, s2


def _rope(t, c, s1, s2):
    half = ROPE_DIM // 2
    return t * c + pltpu.roll(t, HEAD - half, 1) * s1 + pltpu.roll(t, half, 1) * s2


def _rope_bwd(dt, c, s1, s2):
    half = ROPE_DIM // 2
    return dt * c + pltpu.roll(dt * s1, half, 1) + pltpu.roll(dt * s2, HEAD - half, 1)


def _window_mask(r0, qb, wk, seg):
    row = lax.broadcasted_iota(jnp.int32, (qb, wk), 0)
    col = lax.broadcasted_iota(jnp.int32, (qb, wk), 1)
    kj = r0 - ATTN_HALF + col
    return (col - row >= 0) & (col - row <= 2 * ATTN_HALF) & (kj >= 0) & (kj < seg)


def _attn_fwd(qkv, tabs, b_, s_, dil, name):
    seg = s_ // dil
    qb = min(128, seg)
    nq, wk = seg // qb, qb + 2 * ATTN_HALF
    scale = HEAD ** -0.5
    ncol = QKV_GROUP // HEAD

    def body(q_ref, k_ref, v_ref, c_ref, s1_ref, s2_ref, o_ref, lse_ref, q_s, k_s, v_s):
        c, s1, s2 = c_ref[...], s1_ref[...], s2_ref[...]
        q_s[...] = _rope(q_ref[...], c, s1, s2).astype(BF16)
        k_s[...] = jnp.zeros_like(k_s)
        v_s[...] = jnp.zeros_like(v_s)
        k_s[ATTN_HALF:ATTN_HALF + seg, :] = _rope(k_ref[...], c, s1, s2).astype(BF16)
        v_s[ATTN_HALF:ATTN_HALF + seg, :] = v_ref[...].astype(BF16)

        def step(i, carry):
            r0 = pl.multiple_of(i * qb, qb)
            sc = _dot(q_s[pl.ds(r0, qb), :], k_s[pl.ds(r0, wk), :], NT) * scale
            sc = jnp.where(_window_mask(r0, qb, wk, seg), sc, NEG_INF)
            m = jnp.max(sc, axis=-1, keepdims=True)
            p = jnp.exp(sc - m)
            den = jnp.sum(p, axis=-1, keepdims=True)
            o_ref[pl.ds(r0, qb), :] = _dot(p.astype(BF16), v_s[pl.ds(r0, wk), :], NN) / den
            lse_ref[pl.ds(r0, qb), :] = jnp.broadcast_to(m + jnp.log(den), (qb, HEAD))
            return carry

        lax.fori_loop(0, nq, step, 0)

    def col(part):
        return pl.BlockSpec((seg, HEAD), lambda b, r, h, part=part: (b, r * ncol + part * ATTN_HEADS + h))

    tab = pl.BlockSpec((seg, HEAD), lambda b, r, h: (0, r))
    out = pl.BlockSpec((seg, HEAD), lambda b, r, h: (b, r * ATTN_HEADS + h))
    shape = jax.ShapeDtypeStruct((b_ * seg, dil * ATTN_OUT), F32)
    return pl.pallas_call(
        body, name=name, grid=(b_, dil, ATTN_HEADS),
        in_specs=[col(0), col(1), col(2), tab, tab, tab],
        out_specs=[out, out],
        out_shape=[shape, shape],
        scratch_shapes=[pltpu.VMEM((seg, HEAD), BF16), pltpu.VMEM((seg + 2 * ATTN_HALF, HEAD), BF16),
                        pltpu.VMEM((seg + 2 * ATTN_HALF, HEAD), BF16)],
        compiler_params=_params(),
    )(qkv, qkv, qkv, *tabs)


def _attn_bwd(qkv, tabs, dog, cg, lse, b_, s_, dil, name):
    seg = s_ // dil
    qb = min(128, seg)
    nq, wk = seg // qb, qb + 2 * ATTN_HALF
    scale = HEAD ** -0.5
    ncol = QKV_GROUP // HEAD

    def body(q_ref, k_ref, v_ref, c_ref, s1_ref, s2_ref, do_ref, cg_ref, lse_ref, dq_ref, dk_ref, dv_ref,
             q_s, k_s, v_s, dk_s, dv_s):
        c, s1, s2 = c_ref[...], s1_ref[...], s2_ref[...]
        q_s[...] = _rope(q_ref[...], c, s1, s2).astype(BF16)
        k_s[...] = jnp.zeros_like(k_s)
        v_s[...] = jnp.zeros_like(v_s)
        k_s[ATTN_HALF:ATTN_HALF + seg, :] = _rope(k_ref[...], c, s1, s2).astype(BF16)
        v_s[ATTN_HALF:ATTN_HALF + seg, :] = v_ref[...].astype(BF16)
        dk_s[...] = jnp.zeros_like(dk_s)
        dv_s[...] = jnp.zeros_like(dv_s)

        def step(i, carry):
            r0 = pl.multiple_of(i * qb, qb)
            rows, win = pl.ds(r0, qb), pl.ds(r0, wk)
            qc, kw, vw = q_s[rows, :], k_s[win, :], v_s[win, :]
            sc = _dot(qc, kw, NT) * scale
            p = jnp.where(_window_mask(r0, qb, wk, seg), jnp.exp(sc - lse_ref[rows, 0:1]), 0.0)
            dob = do_ref[rows, :].astype(BF16)
            dp = _dot(dob, vw, NT)
            ds = (p * (dp + cg_ref[rows, 0:1]) * scale).astype(BF16)
            dq = _dot(ds, kw, NN)
            dq_ref[rows, :] = _rope_bwd(dq, c_ref[rows, :], s1_ref[rows, :], s2_ref[rows, :])
            dk_s[win, :] += _dot(ds, qc, TN)
            dv_s[win, :] += _dot(p.astype(BF16), dob, TN)
            return carry

        lax.fori_loop(0, nq, step, 0)
        dk_ref[...] = _rope_bwd(dk_s[ATTN_HALF:ATTN_HALF + seg, :], c, s1, s2)
        dv_ref[...] = dv_s[ATTN_HALF:ATTN_HALF + seg, :]

    def col(part):
        return pl.BlockSpec((seg, HEAD), lambda b, r, h, part=part: (b, r * ncol + part * ATTN_HEADS + h))

    tab = pl.BlockSpec((seg, HEAD), lambda b, r, h: (0, r))
    out = pl.BlockSpec((seg, HEAD), lambda b, r, h: (b, r * ATTN_HEADS + h))
    shape = jax.ShapeDtypeStruct((b_ * seg, dil * ATTN_OUT), F32)
    return pl.pallas_call(
        body, name=name, grid=(b_, dil, ATTN_HEADS),
        in_specs=[col(0), col(1), col(2), tab, tab, tab, out, out, out],
        out_specs=[out, out, out],
        out_shape=[shape, shape, shape],
        scratch_shapes=[pltpu.VMEM((seg, HEAD), BF16), pltpu.VMEM((seg + 2 * ATTN_HALF, HEAD), BF16),
                        pltpu.VMEM((seg + 2 * ATTN_HALF, HEAD), BF16),
                        pltpu.VMEM((seg + 2 * ATTN_HALF, HEAD), F32), pltpu.VMEM((seg + 2 * ATTN_HALF, HEAD), F32)],
        compiler_params=_params(),
    )(qkv, qkv, qkv, *tabs, dog, cg, lse)


def _group_weights(lses):
    m = jnp.maximum(jnp.maximum(lses[0], lses[1]), lses[2])
    es = [jnp.exp(l - m) for l in lses]
    den = es[0] + es[1] + es[2]
    return [e / den for e in es]


def _combine_fwd(outs, lses, name):
    t_, w_ = outs[0].shape
    tm = _tile(t_, 512, 8)
    ng = len(outs)

    def body(*refs):
        ws = _group_weights([r[...] for r in refs[ng:2 * ng]])
        acc = ws[0] * refs[0][...]
        for g in range(1, ng):
            acc = acc + ws[g] * refs[g][...]
        refs[2 * ng][...] = acc.astype(BF16)

    row = pl.BlockSpec((tm, w_), lambda i: (i, 0))
    return pl.pallas_call(
        body, name=name, grid=(t_ // tm,), in_specs=[row] * (2 * ng), out_specs=row,
        out_shape=jax.ShapeDtypeStruct((t_, w_), BF16), compiler_params=_params(),
    )(*outs, *lses)


def _combine_bwd(dob, outs, lses, name):
    t_, w_ = outs[0].shape
    tm = _tile(t_, 512, 8)
    ng = len(outs)

    def body(*refs):
        do = refs[0][...]
        os_ = [r[...] for r in refs[1:1 + ng]]
        ws = _group_weights([r[...] for r in refs[1 + ng:1 + 2 * ng]])
        o = ws[0] * os_[0]
        for g in range(1, ng):
            o = o + ws[g] * os_[g]
        prod = do * o
        heads = [jnp.broadcast_to(jnp.sum(prod[:, h * HEAD:(h + 1) * HEAD], axis=-1, keepdims=True), (tm, HEAD))
                 for h in range(w_ // HEAD)]
        tot = jnp.concatenate(heads, axis=1)
        for g in range(ng):
            refs[1 + 2 * ng + g][...] = ws[g] * do
            refs[1 + 3 * ng + g][...] = -ws[g] * tot

    row = pl.BlockSpec((tm, w_), lambda i: (i, 0))
    shape = jax.ShapeDtypeStruct((t_, w_), F32)
    res = pl.pallas_call(
        body, name=name, grid=(t_ // tm,), in_specs=[row] * (1 + 2 * ng), out_specs=[row] * (2 * ng),
        out_shape=[shape] * (2 * ng), compiler_params=_params(),
    )(dob, *outs, *lses)
    return res[:ng], res[ng:]


def _adam_update(w, g, m, v):
    m = ADAM_B1 * m + (1.0 - ADAM_B1) * g
    v = ADAM_B2 * v + (1.0 - ADAM_B2) * (g * g)
    m_hat = m / (1.0 - ADAM_B1 ** ADAM_STEP)
    v_hat = v / (1.0 - ADAM_B2 ** ADAM_STEP)
    return -ADAM_LR * (m_hat / (jnp.sqrt(v_hat) + ADAM_EPS) + ADAM_WD * w), m, v


def _adam(w, g, m, v, name):
    r_, c_ = w.shape
    tr = _tile(r_, 256, 8)

    def body(w_ref, g_ref, m_ref, v_ref, d_ref, mo_ref, vo_ref):
        d_ref[...], mo_ref[...], vo_ref[...] = _adam_update(w_ref[...], g_ref[...], m_ref[...], v_ref[...])

    blk = pl.BlockSpec((tr, c_), lambda i: (i, 0))
    shape = jax.ShapeDtypeStruct((r_, c_), F32)
    return pl.pallas_call(
        body, name=name, grid=(r_ // tr,), in_specs=[blk] * 4, out_specs=[blk] * 3,
        out_shape=[shape] * 3, compiler_params=_params(),
    )(w, g, m, v)


def _sum_partials(recv, name):
    n_, r_, c_ = recv.shape
    tr = _tile(r_, 128, 16)

    def body(p_ref, o_ref):
        acc = p_ref[0].astype(F32)
        for i in range(1, n_):
            acc = acc + p_ref[i].astype(F32)
        o_ref[...] = acc

    return pl.pallas_call(
        body, name=name, grid=(r_ // tr,),
        in_specs=[pl.BlockSpec((n_, tr, c_), lambda i: (0, i, 0))],
        out_specs=pl.BlockSpec((tr, c_), lambda i: (i, 0)),
        out_shape=jax.ShapeDtypeStruct((r_, c_), F32), compiler_params=_params(),
    )(recv)


def _small_sum_adam(parts, w, m, v, name):
    n_, r_, c_ = parts.shape

    def body(p_ref, w_ref, m_ref, v_ref, g_ref, d_ref, mo_ref, vo_ref):
        g = p_ref[0]
        for i in range(1, n_):
            g = g + p_ref[i]
        g_ref[...] = g
        d_ref[...], mo_ref[...], vo_ref[...] = _adam_update(w_ref[...], g, m_ref[...], v_ref[...])

    shape = jax.ShapeDtypeStruct((r_, c_), F32)
    return pl.pallas_call(body, name=name, out_shape=[shape] * 4, compiler_params=_params())(parts, w, m, v)


def _my_place():
    x, y, c = lax.axis_index("x"), lax.axis_index("y"), lax.axis_index("c")
    return x, y, c


def _peer(x, y, c, d):
    px = 1 - x if d & 4 else x
    py = 1 - y if d & 2 else y
    pc = 1 - c if d & 1 else c
    return (px, py, pc), 4 * px + 2 * py + pc


def _all_gather(shards, name):
    nw = len(shards)

    def body(*refs):
        ins, outs = refs[:nw], refs[nw:2 * nw]
        send_sems, recv_sems, local_sems = refs[2 * nw:]
        x, y, c = _my_place()
        me = 4 * x + 2 * y + c
        copies = []
        for k in range(nw):
            rows = shards[k].shape[0]
            mine = outs[k].at[pl.ds(pl.multiple_of(me * rows, 16), rows), :]
            local = pltpu.make_async_copy(ins[k], mine, local_sems.at[k])
            local.start()
            copies.append(local)
            for d in range(1, N_DEV):
                place, _ = _peer(x, y, c, d)
                remote = pltpu.make_async_remote_copy(
                    src_ref=ins[k], dst_ref=mine, send_sem=send_sems.at[d - 1, k], recv_sem=recv_sems.at[d - 1, k],
                    device_id=place, device_id_type=MESH)
                remote.start()
                copies.append(remote)
        for cp in copies:
            cp.wait()

    hbm = pl.BlockSpec(memory_space=pl.ANY)
    return pl.pallas_call(
        body, name=name, in_specs=[hbm] * nw, out_specs=[hbm] * nw,
        out_shape=[jax.ShapeDtypeStruct((N_DEV * s.shape[0], s.shape[1]), s.dtype) for s in shards],
        scratch_shapes=[pltpu.SemaphoreType.DMA((N_DEV - 1, nw)), pltpu.SemaphoreType.DMA((N_DEV - 1, nw)),
                        pltpu.SemaphoreType.DMA((nw,))],
    )(*shards)


HBM_SPEC = pl.BlockSpec(memory_space=pltpu.HBM)
SEM_SPEC = pl.BlockSpec(memory_space=pltpu.SEMAPHORE)
EFFECT = pltpu.SideEffectType.DATAFLOW_SIDE_EFFECTING


def _in_hbm(a):
    return pltpu.with_memory_space_constraint(a, pltpu.HBM)


def _token_shape():
    return jax.ShapeDtypeStruct((8, HEAD), F32)


SIBLING = 1
OTHER_CHIPS = (4, 2, 6)


def _rows_of(ref, num, rows):
    return ref.at[pl.ds(pl.multiple_of(num * rows, 16), rows), :]


def _gather_start(shards, name):
    nw = len(shards)
    lands = [lax.empty((N_DEV * s.shape[0], s.shape[1]), s.dtype) for s in shards]
    n_to = 1 + len(OTHER_CHIPS)

    def body(*refs):
        ins, lnd = refs[:nw], refs[nw:2 * nw]
        send, from_sib, from_chips, own = (refs[(2 + i) * nw:(3 + i) * nw] for i in range(4))
        token = refs[8 * nw]
        x, y, c = _my_place()
        me = 4 * x + 2 * y + c
        for k in range(nw):
            mine = _rows_of(lnd[k], me, shards[k].shape[0])
            pltpu.make_async_copy(ins[k], mine, own[k]).start()
            for i, d in enumerate((SIBLING,) + OTHER_CHIPS):
                place, _ = _peer(x, y, c, d)
                pltpu.make_async_remote_copy(
                    src_ref=ins[k], dst_ref=mine, send_sem=send[k].at[i],
                    recv_sem=from_sib[k] if i == 0 else from_chips[k].at[i - 1],
                    device_id=place, device_id_type=MESH).start()
        token[...] = jnp.zeros_like(token)

    dma = pltpu.SemaphoreType.DMA
    sems = [dma((n_to,))] * nw + [dma(())] * nw + [dma((len(OTHER_CHIPS),))] * nw + [dma(())] * nw
    thru = [pltpu.HBM(a.shape, a.dtype) for a in list(shards) + lands]
    res = pl.pallas_call(
        body, name=name, out_shape=(*sems, *thru, _token_shape()),
        in_specs=[HBM_SPEC] * (2 * nw),
        out_specs=(*([SEM_SPEC] * (4 * nw)), *([HBM_SPEC] * (2 * nw)), pl.BlockSpec(memory_space=pltpu.VMEM)),
        input_output_aliases={i: 4 * nw + i for i in range(2 * nw)},
        compiler_params=pltpu.CompilerParams(has_side_effects=EFFECT),
    )(*[_in_hbm(s) for s in shards], *[_in_hbm(l) for l in lands])
    return [dict(send=res[k], from_sib=res[nw + k], from_chips=res[2 * nw + k], own=res[3 * nw + k],
                 src=res[4 * nw + k], land=res[5 * nw + k]) for k in range(nw)]


def _gather_forward(pending, after, name):
    rows = pending["src"].shape[0]
    n_fw = len(OTHER_CHIPS)

    def body(land_ref, from_chips, after_ref, fw_send, fw_recv, land_thru):
        x, y, c = _my_place()
        sibling, _ = _peer(x, y, c, SIBLING)
        for j, d in enumerate(OTHER_CHIPS):
            _, num = _peer(x, y, c, d)
            block = _rows_of(land_ref, num, rows)
            pltpu.make_async_remote_copy(
                src_ref=block, dst_ref=block, send_sem=fw_send.at[j], recv_sem=from_chips.at[j],
                device_id=sibling, device_id_type=MESH).wait_recv()
            pltpu.make_async_remote_copy(
                src_ref=block, dst_ref=block, send_sem=fw_send.at[j], recv_sem=fw_recv.at[j],
                device_id=sibling, device_id_type=MESH).start()

    land = pending["land"]
    dma = pltpu.SemaphoreType.DMA
    fw_send, fw_recv, land = pl.pallas_call(
        body, name=name, out_shape=(dma((n_fw,)), dma((n_fw,)), pltpu.HBM(land.shape, land.dtype)),
        in_specs=(HBM_SPEC, SEM_SPEC, pl.BlockSpec(memory_space=pl.ANY)),
        out_specs=(SEM_SPEC, SEM_SPEC, HBM_SPEC), input_output_aliases={0: 2},
        compiler_params=pltpu.CompilerParams(has_side_effects=EFFECT),
    )(land, pending["from_chips"], after)
    return dict(pending, land=land, fw_send=fw_send, fw_recv=fw_recv)


def _gather_wait(pending, name):
    rows = pending["src"].shape[0]

    def body(src_ref, land_ref, send, from_sib, own, fw_send, fw_recv, src_dead, got):
        x, y, c = _my_place()
        me = 4 * x + 2 * y + c
        sibling, sib_num = _peer(x, y, c, SIBLING)
        mine = _rows_of(land_ref, me, rows)
        pltpu.make_async_copy(src_ref, mine, own).wait()
        for i in range(1 + len(OTHER_CHIPS)):
            pltpu.make_async_remote_copy(
                src_ref=src_ref, dst_ref=mine, send_sem=send.at[i], recv_sem=from_sib,
                device_id=sibling, device_id_type=MESH).wait_send()
        theirs = _rows_of(land_ref, sib_num, rows)
        pltpu.make_async_remote_copy(
            src_ref=src_ref, dst_ref=theirs, send_sem=send.at[0], recv_sem=from_sib,
            device_id=sibling, device_id_type=MESH).wait_recv()
        for j, d in enumerate(OTHER_CHIPS):
            _, num = _peer(x, y, c, d)
            sent = _rows_of(land_ref, num, rows)
            _, got_num = _peer(x, y, c, d | SIBLING)
            arrived = _rows_of(land_ref, got_num, rows)
            cp = pltpu.make_async_remote_copy(
                src_ref=sent, dst_ref=arrived, send_sem=fw_send.at[j], recv_sem=fw_recv.at[j],
                device_id=sibling, device_id_type=MESH)
            cp.wait_send()
            cp.wait_recv()

    src, land = pending["src"], pending["land"]
    return pl.pallas_call(
        body, name=name, out_shape=(pltpu.HBM(src.shape, src.dtype), pltpu.HBM(land.shape, land.dtype)),
        in_specs=(HBM_SPEC, HBM_SPEC) + (SEM_SPEC,) * 5,
        out_specs=(HBM_SPEC, HBM_SPEC), input_output_aliases={0: 0, 1: 1},
        compiler_params=pltpu.CompilerParams(has_side_effects=EFFECT),
    )(src, land, pending["send"], pending["from_sib"], pending["own"], pending["fw_send"], pending["fw_recv"])[1]


def _scatter_start(full, name):
    rows, cols = full.shape[0] // N_DEV, full.shape[1]
    land = lax.empty((N_DEV, rows, cols), full.dtype)

    def body(full_ref, land_ref, send, recv, own, full_thru, land_thru, token):
        x, y, c = _my_place()
        me = 4 * x + 2 * y + c
        slab = land_ref.at[me]
        pltpu.make_async_copy(full_ref.at[pl.ds(pl.multiple_of(me * rows, 16), rows), :], slab, own).start()
        for d in range(1, N_DEV):
            place, num = _peer(x, y, c, d)
            pltpu.make_async_remote_copy(
                src_ref=full_ref.at[pl.ds(pl.multiple_of(num * rows, 16), rows), :], dst_ref=slab,
                send_sem=send.at[d - 1], recv_sem=recv.at[d - 1], device_id=place, device_id_type=MESH).start()
        token[...] = jnp.zeros_like(token)

    res = pl.pallas_call(
        body, name=name,
        out_shape=(pltpu.SemaphoreType.DMA((N_DEV - 1,)), pltpu.SemaphoreType.DMA((N_DEV - 1,)),
                   pltpu.SemaphoreType.DMA(()),
                   pltpu.HBM(full.shape, full.dtype), pltpu.HBM(land.shape, land.dtype), _token_shape()),
        in_specs=(HBM_SPEC, HBM_SPEC),
        out_specs=(SEM_SPEC, SEM_SPEC, SEM_SPEC, HBM_SPEC, HBM_SPEC, pl.BlockSpec(memory_space=pltpu.VMEM)),
        input_output_aliases={0: 3, 1: 4},
        compiler_params=pltpu.CompilerParams(has_side_effects=EFFECT),
    )(_in_hbm(full), _in_hbm(land))
    return dict(send=res[0], recv=res[1], own=res[2], src=res[3], land=res[4]), res[5]


def _scatter_wait(pending, after, name):
    rows = pending["land"].shape[1]

    def body(src_ref, land_ref, send, recv, own, after_ref, src_dead, got):
        x, y, c = _my_place()
        me = 4 * x + 2 * y + c
        pltpu.make_async_copy(src_ref.at[pl.ds(pl.multiple_of(me * rows, 16), rows), :], land_ref.at[me], own).wait()
        for d in range(1, N_DEV):
            place, num = _peer(x, y, c, d)
            cp = pltpu.make_async_remote_copy(
                src_ref=src_ref.at[pl.ds(pl.multiple_of(num * rows, 16), rows), :], dst_ref=land_ref.at[me],
                send_sem=send.at[d - 1], recv_sem=recv.at[d - 1], device_id=place, device_id_type=MESH)
            cp.wait_send()
            cp.wait_recv()

    src, land = pending["src"], pending["land"]
    return pl.pallas_call(
        body, name=name, out_shape=(pltpu.HBM(src.shape, src.dtype), pltpu.HBM(land.shape, land.dtype)),
        in_specs=(HBM_SPEC, HBM_SPEC, SEM_SPEC, SEM_SPEC, SEM_SPEC, pl.BlockSpec(memory_space=pl.ANY)),
        out_specs=(HBM_SPEC, HBM_SPEC), input_output_aliases={0: 0, 1: 1},
        compiler_params=pltpu.CompilerParams(has_side_effects=EFFECT),
    )(src, land, pending["send"], pending["recv"], pending["own"], after)[1]


BIG = ("ffn1_w_in", "ffn1_w_out", "mix_w_in", "w_branch_a", "w_branch_b", "mix_w_out", "ffn2_w_in", "ffn2_w_out")
TRANSPOSED = ("ffn1_w_in", "mix_w_in", "w_branch_b", "ffn2_w_in")
SMALL = ("ln1_g", "ln1_b", "ln2_g", "ln2_b", "ln3_g", "ln3_b", "hgrn_norm_g", "hgrn_lb_fwd", "hgrn_lb_bwd")
SMALL_ROWS = 16


def _local_step(x, target, weight, emit, sp):
    b_, s_, d_ = x.shape
    t_ = b_ * s_
    x2, tgt = x.reshape(t_, d_), target.reshape(t_, d_)
    xb = x2.astype(BF16)
    w1i = weight("ffn1_w_in", xb)
    g1, u1, a1 = _ffn_in_fwd(xb, w1i, "ffn1_in")
    w1o = weight("ffn1_w_out", a1)
    r1, h1, h1b = _mm_res_ln_fwd(a1, w1o, x2, sp["ln1_g"], sp["ln1_b"], 0.5, "ffn1_out_ln1")
    wmx = weight("mix_w_in", h1b)
    proj = _mm_nt(h1b, wmx, F32, "mix_in")
    ya_in, o_sum = _hgrn_fwd(proj, sp["hgrn_lb_fwd"], sp["hgrn_lb_bwd"], sp["hgrn_norm_g"], b_, s_, d_, "hgrn_fwd")
    tabs = _rope_tables(s_)
    qkvs, gtabs, outs, lses = [], [], [], []
    for gi, (_, dil) in enumerate(ATTN_GROUPS):
        off = 5 * d_ + gi * QKV_GROUP
        qkv = proj[:, off:off + QKV_GROUP].reshape(t_ // dil, dil * QKV_GROUP)
        gt = [tb.reshape(s_ // dil, dil * HEAD) for tb in tabs]
        o_g, lse_g = _attn_fwd(qkv, gt, b_, s_, dil, f"attn_fwd_{gi}")
        qkvs.append(qkv)
        gtabs.append(gt)
        outs.append(o_g.reshape(t_, ATTN_OUT))
        lses.append(lse_g.reshape(t_, ATTN_OUT))
    ob = _combine_fwd(outs, lses, "attn_combine")
    wa, wb = weight("w_branch_a", ya_in), weight("w_branch_b", ob)
    ya, yb, z = _gate_out_fwd(ya_in, ob, wa, wb, proj, d_, "branch_gate")
    wo = weight("mix_w_out", z)
    r2, h2, h2b = _mm_res_ln_fwd(z, wo, h1, sp["ln2_g"], sp["ln2_b"], 1.0, "mix_out_ln2")
    w2i = weight("ffn2_w_in", h2b)
    g2, u2, a2 = _ffn_in_fwd(h2b, w2i, "ffn2_in")
    w2o = weight("ffn2_w_out", a2)
    r3, _, _ = _mm_res_ln_fwd(a2, w2o, h2, sp["ln3_g"], sp["ln3_b"], 0.5, "ffn2_out_ln3")
    dr3, dr3b, dg3, db3, loss = _ln_loss_bwd(r3, tgt, sp["ln3_g"], sp["ln3_b"], "loss_ln3_bwd")
    dep = emit("ffn2_w_out", _mm_tn(a2, dr3b, 0.5, "d_ffn2_w_out"))
    dgate2, dup2 = _ffn_mid_bwd(dr3b, w2o, g2, u2, 0.5, "ffn2_mid_bwd", dep)
    du2 = jnp.concatenate([dgate2, dup2], axis=1)
    dep = emit("ffn2_w_in", _mm_tn(du2, h2b, 1.0, "d_ffn2_w_in"))
    dr2, dr2b, dg2, db2 = _mm_nn_res_lnbwd(du2, w2i, dr3, r2, sp["ln2_g"], "ffn2_in_bwd_ln2", dep)
    dep = emit("mix_w_out", _mm_tn(z, dr2b, 1.0, "d_mix_w_out"))
    dya, dyb, dga, dgb = _dz_gate_bwd(dr2b, wo, proj, ya, yb, d_, "branch_gate_bwd", dep)
    dep = emit("w_branch_a", _mm_tn(ya_in, dya, 1.0, "d_w_branch_a"))
    dya_in = _mm_nt(dya, wa, F32, "branch_a_bwd", dep)
    dep = emit("w_branch_b", _mm_tn(dyb, ob, 1.0, "d_w_branch_b"))
    dob = _mm_nn(dyb, wb, F32, "branch_b_bwd", dep)
    dhq, dhff, dhfb, dhi, dhog, dng, dlbf, dlbb = _hgrn_bwd(
        proj, sp["hgrn_lb_fwd"], sp["hgrn_lb_bwd"], sp["hgrn_norm_g"], o_sum, dya_in, b_, s_, d_, "hgrn_bwd")
    dogs, cgs = _combine_bwd(dob, outs, lses, "attn_combine_bwd")
    dqkv = []
    for gi, (_, dil) in enumerate(ATTN_GROUPS):
        seg_rows = t_ // dil
        dq, dk, dv = _attn_bwd(qkvs[gi], gtabs[gi], dogs[gi].reshape(seg_rows, dil * ATTN_OUT),
                               cgs[gi].reshape(seg_rows, dil * ATTN_OUT), lses[gi].reshape(seg_rows, dil * ATTN_OUT),
                               b_, s_, dil, f"attn_bwd_{gi}")
        dqkv += [t.reshape(t_, ATTN_OUT).astype(BF16) for t in (dq, dk, dv)]
    dproj = jnp.concatenate([dhq, dhff, dhfb, dhi, dhog] + dqkv + [dga, dgb], axis=1)
    dep = emit("mix_w_in", _mm_tn(dproj, h1b, 1.0, "d_mix_w_in"))
    dr1, dr1b, dg1, db1 = _mm_nn_res_lnbwd(dproj, wmx, dr2, r1, sp["ln1_g"], "mix_in_bwd_ln1", dep)
    dep = emit("ffn1_w_out", _mm_tn(a1, dr1b, 0.5, "d_ffn1_w_out"))
    dgate1, dup1 = _ffn_mid_bwd(dr1b, w1o, g1, u1, 0.5, "ffn1_mid_bwd", dep)
    du1 = jnp.concatenate([dgate1, dup1], axis=1)
    dep = emit("ffn1_w_in", _mm_tn(du1, xb, 1.0, "d_ffn1_w_in"))
    grad_x = _mm_nn_res(du1, w1i, dr1, "ffn1_in_bwd", dep)
    small = {"ln1_g": dg1, "ln1_b": db1, "ln2_g": dg2, "ln2_b": db2, "ln3_g": dg3, "ln3_b": db3,
             "hgrn_norm_g": dng, "hgrn_lb_fwd": dlbf, "hgrn_lb_bwd": dlbb}
    return loss, grad_x.reshape(b_, s_, d_), small


def _pack_small(vals):
    rows = jnp.concatenate([vals[n] for n in SMALL], axis=0)
    return jnp.pad(rows, ((0, SMALL_ROWS - rows.shape[0]), (0, 0)))


def _unpack_small(packed):
    out, r = {}, 0
    for n in SMALL:
        k = 2 if n.startswith("hgrn_lb") else 1
        out[n] = packed[r:r + k]
        r += k
    return out


def kernel(x, ffn1_w_in, ffn1_w_out, ln1_g, ln1_b, mix_w_in, hgrn_lb_fwd, hgrn_lb_bwd, hgrn_norm_g, w_branch_a, w_branch_b, mix_w_out, ln2_g, ln2_b, ffn2_w_in, ffn2_w_out, ln3_g, ln3_b, loss_target, m_ffn1_w_in, m_ffn1_w_out, m_ln1_g, m_ln1_b, m_mix_w_in, m_hgrn_lb_fwd, m_hgrn_lb_bwd, m_hgrn_norm_g, m_w_branch_a, m_w_branch_b, m_mix_w_out, m_ln2_g, m_ln2_b, m_ffn2_w_in, m_ffn2_w_out, m_ln3_g, m_ln3_b, v_ffn1_w_in, v_ffn1_w_out, v_ln1_g, v_ln1_b, v_mix_w_in, v_hgrn_lb_fwd, v_hgrn_lb_bwd, v_hgrn_norm_g, v_w_branch_a, v_w_branch_b, v_mix_w_out, v_ln2_g, v_ln2_b, v_ffn2_w_in, v_ffn2_w_out, v_ln3_g, v_ln3_b):
    args = dict(locals())
    big_w = {n: args[n][0] for n in BIG}
    sp = {n: args[n] for n in SMALL}
    shards = [(big_w[n].T if n in TRANSPOSED else big_w[n]).astype(BF16) for n in BIG]
    gathering = dict(zip(BIG, _gather_start(shards, "gather_start")))
    scattering = {}

    def weight(n, after):
        return _gather_wait(_gather_forward(gathering[n], after, f"gather_forward_{n}"), f"gather_wait_{n}")

    def emit(n, grad):
        scattering[n], token = _scatter_start(grad, f"scatter_start_{n}")
        return token

    loss_part, grad_x, small = _local_step(x, loss_target, weight, emit, sp)
    loss = lax.psum(loss_part[0, 0], ("x", "y", "c"))
    out_g, out_d, out_m, out_v = {}, {}, {}, {}
    for n in BIG:
        g = _sum_partials(_scatter_wait(scattering[n], grad_x, f"scatter_wait_{n}"), f"sum_{n}")
        if n in TRANSPOSED:
            g = g.T
        d_w, m_w, v_w = _adam(big_w[n], g, args["m_" + n][0], args["v_" + n][0], f"adam_{n}")
        out_g[n], out_d[n], out_m[n], out_v[n] = g[None], d_w[None], m_w[None], v_w[None]
    (parts,) = _all_gather([_pack_small(small)], "gather_small_grads")
    res = _small_sum_adam(parts.reshape(N_DEV, SMALL_ROWS, parts.shape[1]), _pack_small(sp),
                          _pack_small({n: args["m_" + n] for n in SMALL}),
                          _pack_small({n: args["v_" + n] for n in SMALL}), "small_adam")
    sg, sd, sm, sv = (_unpack_small(r) for r in res)
    out_g.update(sg), out_d.update(sd), out_m.update(sm), out_v.update(sv)
    order = ("ffn1_w_in", "ffn1_w_out", "ln1_g", "ln1_b", "mix_w_in", "hgrn_lb_fwd", "hgrn_lb_bwd", "hgrn_norm_g",
             "w_branch_a", "w_branch_b", "mix_w_out", "ln2_g", "ln2_b", "ffn2_w_in", "ffn2_w_out", "ln3_g", "ln3_b")
    return (loss, grad_x, *[out_g[n] for n in order], *[out_d[n] for n in order],
            *[out_m[n] for n in order], *[out_v[n] for n in order])
```

```python
import functools

import jax
import jax.numpy as jnp
from jax import lax
from jax.experimental import pallas as pl
from jax.experimental.pallas import tpu as pltpu

F32 = jnp.float32
BF16 = jnp.bfloat16

N_DEV = 8
HEAD = 128
CHUNK = 32
ATTN_GROUPS = ((128, 1), (512, 4), (2048, 16))
ATTN_HEADS = 4
ATTN_HALF = 64
QKV_GROUP = 3 * ATTN_HEADS * HEAD
QKV_WIDTH = len(ATTN_GROUPS) * QKV_GROUP
ATTN_OUT = ATTN_HEADS * HEAD
ROPE_THETA = 500000.0
ROPE_DIM = HEAD // 4
ALPHA = 2.0 ** 0.25
LN_EPS = 1e-5
NEG_INF = -1e30
ADAM_LR, ADAM_B1, ADAM_B2, ADAM_EPS, ADAM_WD, ADAM_STEP = 0.001, 0.9, 0.999, 1e-08, 0.01, 10
VMEM_LIMIT = 56 * 1024 * 1024

NT = (((1,), (1,)), ((), ()))
NN = (((1,), (0,)), ((), ()))
TN = (((0,), (0,)), ((), ()))
MESH = pl.DeviceIdType.MESH


def _dot(a, b, dims):
    return lax.dot_general(a, b, dims, preferred_element_type=F32)


def _tile(n, pref, mult=128):
    if n <= pref:
        return n
    t = (pref // mult) * mult
    while t >= mult:
        if n % t == 0:
            return t
        t -= mult
    return n


def _tile_multi(ns, pref, mult=128):
    t = (pref // mult) * mult
    while t >= mult:
        if all(n % t == 0 for n in ns):
            return t
        t -= mult
    raise ValueError(f"no common tile for {ns}")


def _params(**kw):
    return pltpu.CompilerParams(vmem_limit_bytes=VMEM_LIMIT, **kw)


def _after(body, n_in, dep):
    if dep is None:
        return body, [], []

    def wrapped(*refs):
        body(*refs[:n_in], *refs[n_in + 1:])

    return wrapped, [pl.BlockSpec(dep.shape, lambda *_: (0,) * dep.ndim)], [dep]


def _pieces(a):
    pieces = tuple(a) if isinstance(a, (tuple, list)) else (a,)
    assert all(p.shape == pieces[0].shape for p in pieces)
    return pieces, pieces[0].shape[0], pieces[0].shape[1], len(pieces)


def _for_piece(step, p, per, npc, fn):
    if npc == 1:
        fn()
    else:
        pl.when((step >= p * per) & (step < (p + 1) * per))(fn)


def _sigmoid(x):
    return jax.nn.sigmoid(x)


def _dsilu(x, s):
    return s * (1.0 + x * (1.0 - s))


def _ln_stats(r):
    mu = jnp.mean(r, axis=-1, keepdims=True)
    xc = r - mu
    var = jnp.mean(xc * xc, axis=-1, keepdims=True)
    rstd = lax.rsqrt(var + LN_EPS)
    return xc * rstd, rstd


def _ln_bwd(dy, xhat, rstd, g):
    dyg = dy * g
    m1 = jnp.mean(dyg, axis=-1, keepdims=True)
    m2 = jnp.mean(dyg * xhat, axis=-1, keepdims=True)
    return rstd * (dyg - m1 - xhat * m2)


ROW_TILE = 1024
SUB_ROWS = 256


def _once(shape, index_map):
    return pl.BlockSpec(shape, index_map, pipeline_mode=pl.Buffered(1))


def _for_row_blocks(tm, fn):
    sub = SUB_ROWS if tm % SUB_ROWS == 0 else tm

    def step(s, carry):
        fn(pl.ds(pl.multiple_of(s * sub, sub), sub))
        return carry

    lax.fori_loop(0, tm // sub, step, 0)


def _ffn_in_fwd(xb, w_t, name):
    t_, d_ = xb.shape
    f_ = w_t.shape[0] // 2
    tm, tn = _tile(t_, ROW_TILE, 8), _tile(f_, 512)
    nj = f_ // tn

    def body(x_ref, wg_ref, wu_ref, g_ref, u_ref, a_ref):
        x = x_ref[...]
        g = _dot(x, wg_ref[...], NT)
        u = _dot(x, wu_ref[...], NT)
        g_ref[...] = g
        u_ref[...] = u
        a_ref[...] = (g * _sigmoid(g) * u).astype(BF16)

    return pl.pallas_call(
        body, name=name, grid=(t_ // tm, nj),
        in_specs=[pl.BlockSpec((tm, d_), lambda i, j: (i, 0)),
                  pl.BlockSpec((tn, d_), lambda i, j: (j, 0)),
                  pl.BlockSpec((tn, d_), lambda i, j: (j + nj, 0))],
        out_specs=[pl.BlockSpec((tm, tn), lambda i, j: (i, j))] * 3,
        out_shape=[jax.ShapeDtypeStruct((t_, f_), F32), jax.ShapeDtypeStruct((t_, f_), F32),
                   jax.ShapeDtypeStruct((t_, f_), BF16)],
        compiler_params=_params(),
    )(xb, w_t, w_t)


def _mm_res_ln_fwd(a, w, res, g, b, scale, name):
    t_, k_ = a.shape
    d_ = w.shape[1]
    tm, tk = _tile(t_, ROW_TILE, 8), _tile(k_, 512)
    nk = k_ // tk

    def body(a_ref, w_ref, res_ref, g_ref, b_ref, r_ref, h_ref, hb_ref, acc):
        k = pl.program_id(1)

        @pl.when(k == 0)
        def _():
            acc[...] = jnp.zeros_like(acc)

        acc[...] += _dot(a_ref[...], w_ref[...], NN)

        @pl.when(k == nk - 1)
        def _():
            def rows_out(rows):
                r = ALPHA * res_ref[rows, :] + scale * acc[rows, :]
                xhat, _ = _ln_stats(r)
                h = xhat * g_ref[...] + b_ref[...]
                r_ref[rows, :] = r
                h_ref[rows, :] = h
                hb_ref[rows, :] = h.astype(BF16)

            _for_row_blocks(tm, rows_out)

    row = _once((tm, d_), lambda i, k: (i, 0))
    vec = pl.BlockSpec((1, d_), lambda i, k: (0, 0))
    return pl.pallas_call(
        body, name=name, grid=(t_ // tm, nk),
        in_specs=[pl.BlockSpec((tm, tk), lambda i, k: (i, k)),
                  pl.BlockSpec((tk, d_), lambda i, k: (k, 0)), row, vec, vec],
        out_specs=[row, row, row],
        out_shape=[jax.ShapeDtypeStruct((t_, d_), F32), jax.ShapeDtypeStruct((t_, d_), F32),
                   jax.ShapeDtypeStruct((t_, d_), BF16)],
        scratch_shapes=[pltpu.VMEM((tm, d_), F32)],
        compiler_params=_params(),
    )(a, w, res, g, b)


def _mm_nt(a, w_t, out_dtype, name, dep=None):
    t_, k_ = a.shape
    n_ = w_t.shape[0]
    tm, tn = _tile(t_, ROW_TILE, 8), _tile(n_, 512)

    def body(a_ref, w_ref, o_ref):
        o_ref[...] = _dot(a_ref[...], w_ref[...], NT).astype(out_dtype)

    body, dep_specs, deps = _after(body, 2, dep)
    return pl.pallas_call(
        body, name=name, grid=(t_ // tm, n_ // tn),
        in_specs=[pl.BlockSpec((tm, k_), lambda i, j: (i, 0)),
                  pl.BlockSpec((tn, k_), lambda i, j: (j, 0)), *dep_specs],
        out_specs=pl.BlockSpec((tm, tn), lambda i, j: (i, j)),
        out_shape=jax.ShapeDtypeStruct((t_, n_), out_dtype),
        compiler_params=_params(),
    )(a, w_t, *deps)


def _mm_nn(a, w, out_dtype, name, dep=None):
    t_, k_ = a.shape
    n_ = w.shape[1]
    tm, tn = _tile(t_, ROW_TILE, 8), _tile(n_, 512)

    def body(a_ref, w_ref, o_ref):
        o_ref[...] = _dot(a_ref[...], w_ref[...], NN).astype(out_dtype)

    body, dep_specs, deps = _after(body, 2, dep)
    return pl.pallas_call(
        body, name=name, grid=(t_ // tm, n_ // tn),
        in_specs=[pl.BlockSpec((tm, k_), lambda i, j: (i, 0)),
                  pl.BlockSpec((k_, tn), lambda i, j: (0, j)), *dep_specs],
        out_specs=pl.BlockSpec((tm, tn), lambda i, j: (i, j)),
        out_shape=jax.ShapeDtypeStruct((t_, n_), out_dtype),
        compiler_params=_params(),
    )(a, w, *deps)


def _mm_tn(a, b, scale, name):
    pieces, t_, mp, npc = _pieces(a)
    n_ = b.shape[1]
    tm = _tile(mp, 512)
    per = mp // tm

    def body(*refs):
        b_ref, o_ref = refs[npc], refs[npc + 1]
        for p in range(npc):
            def piece_out(p=p):
                o_ref[...] = (scale * _dot(refs[p][...], b_ref[...], TN)).astype(BF16)

            _for_piece(pl.program_id(0), p, per, npc, piece_out)

    return pl.pallas_call(
        body, name=name, grid=(npc * per,),
        in_specs=[pl.BlockSpec((t_, tm), lambda i, p=p: (0, jnp.clip(i - p * per, 0, per - 1))) for p in range(npc)]
        + [_once((t_, n_), lambda i: (0, 0))],
        out_specs=pl.BlockSpec((tm, n_), lambda i: (i, 0)),
        out_shape=jax.ShapeDtypeStruct((npc * mp, n_), BF16),
        compiler_params=_params(),
    )(*pieces, b)


def _gate_out_fwd(ya_in, ob, wa, wb_t, proj, d_, name):
    t_ = ya_in.shape[0]
    goff = 5 * d_ + QKV_WIDTH
    tm, tn = _tile(t_, ROW_TILE, 8), _tile_multi([d_, goff], 512)
    ja, jb = goff // tn, (goff + d_) // tn

    def body(ya_ref, ob_ref, wa_ref, wb_ref, ga_ref, gb_ref, yao_ref, ybo_ref, z_ref):
        y_a = _dot(ya_ref[...], wa_ref[...], NN)
        y_b = _dot(ob_ref[...], wb_ref[...], NT)
        yao_ref[...] = y_a
        ybo_ref[...] = y_b
        z_ref[...] = (_sigmoid(ga_ref[...]) * y_a + _sigmoid(gb_ref[...]) * y_b).astype(BF16)

    tile = pl.BlockSpec((tm, tn), lambda i, j: (i, j))
    return pl.pallas_call(
        body, name=name, grid=(t_ // tm, d_ // tn),
        in_specs=[pl.BlockSpec((tm, d_), lambda i, j: (i, 0)),
                  pl.BlockSpec((tm, ATTN_OUT), lambda i, j: (i, 0)),
                  pl.BlockSpec((d_, tn), lambda i, j: (0, j)),
                  pl.BlockSpec((tn, ATTN_OUT), lambda i, j: (j, 0)),
                  pl.BlockSpec((tm, tn), lambda i, j: (i, ja + j)),
                  pl.BlockSpec((tm, tn), lambda i, j: (i, jb + j))],
        out_specs=[tile, tile, tile],
        out_shape=[jax.ShapeDtypeStruct((t_, d_), F32), jax.ShapeDtypeStruct((t_, d_), F32),
                   jax.ShapeDtypeStruct((t_, d_), BF16)],
        compiler_params=_params(),
    )(ya_in, ob, wa, wb_t, proj, proj)


def _ffn_mid_bwd(drb, w_out, gate, up, scale, name, dep=None):
    t_, d_ = drb.shape
    f_ = w_out.shape[0]
    tm, tn = _tile(t_, ROW_TILE, 8), _tile(f_, 512)

    def body(dr_ref, w_ref, g_ref, u_ref, dg_ref, du_ref):
        da = scale * _dot(dr_ref[...], w_ref[...], NT)
        g = g_ref[...]
        s = _sigmoid(g)
        dg_ref[...] = (da * u_ref[...] * _dsilu(g, s)).astype(BF16)
        du_ref[...] = (da * g * s).astype(BF16)

    body, dep_specs, deps = _after(body, 4, dep)
    tile = pl.BlockSpec((tm, tn), lambda i, j: (i, j))
    return pl.pallas_call(
        body, name=name, grid=(t_ // tm, f_ // tn),
        in_specs=[pl.BlockSpec((tm, d_), lambda i, j: (i, 0)),
                  pl.BlockSpec((tn, d_), lambda i, j: (j, 0)), tile, tile, *dep_specs],
        out_specs=[tile, tile],
        out_shape=[jax.ShapeDtypeStruct((t_, f_), BF16)] * 2,
        compiler_params=_params(),
    )(drb, w_out, gate, up, *deps)


def _mm_nn_res_lnbwd(a, w, dres, r, g, name, dep=None):
    pieces, t_, kp, npc = _pieces(a)
    d_ = w.shape[1]
    tm, tk = _tile(t_, ROW_TILE, 8), _tile(kp, 512)
    per = kp // tk
    nk = npc * per

    def body(*refs):
        w_ref, dres_ref, r_ref, g_ref, dr_ref, drb_ref, dg_ref, db_ref, acc = refs[npc:]
        i, k = pl.program_id(0), pl.program_id(1)

        @pl.when(k == 0)
        def _():
            acc[...] = jnp.zeros_like(acc)

        @pl.when((i == 0) & (k == 0))
        def _():
            dg_ref[...] = jnp.zeros_like(dg_ref)
            db_ref[...] = jnp.zeros_like(db_ref)

        for p in range(npc):
            def piece_in(p=p):
                acc[...] += _dot(refs[p][...], w_ref[...], NN)

            _for_piece(k, p, per, npc, piece_in)

        @pl.when(k == nk - 1)
        def _():
            def rows_out(rows):
                dy = acc[rows, :] + ALPHA * dres_ref[rows, :]
                xhat, rstd = _ln_stats(r_ref[rows, :])
                dr = _ln_bwd(dy, xhat, rstd, g_ref[...])
                dr_ref[rows, :] = dr
                drb_ref[rows, :] = dr.astype(BF16)
                dg_ref[...] += jnp.sum(dy * xhat, axis=0, keepdims=True)
                db_ref[...] += jnp.sum(dy, axis=0, keepdims=True)

            _for_row_blocks(tm, rows_out)

    body, dep_specs, deps = _after(body, npc + 4, dep)
    row = _once((tm, d_), lambda i, k: (i, 0))
    vec = pl.BlockSpec((1, d_), lambda i, k: (0, 0))
    return pl.pallas_call(
        body, name=name, grid=(t_ // tm, nk),
        in_specs=[pl.BlockSpec((tm, tk), lambda i, k, p=p: (i, jnp.clip(k - p * per, 0, per - 1))) for p in range(npc)]
        + [pl.BlockSpec((tk, d_), lambda i, k: (k, 0)), row, row, vec, *dep_specs],
        out_specs=[row, row, vec, vec],
        out_shape=[jax.ShapeDtypeStruct((t_, d_), F32), jax.ShapeDtypeStruct((t_, d_), BF16),
                   jax.ShapeDtypeStruct((1, d_), F32), jax.ShapeDtypeStruct((1, d_), F32)],
        scratch_shapes=[pltpu.VMEM((tm, d_), F32)],
        compiler_params=_params(),
    )(*pieces, w, dres, r, g, *deps)


def _mm_nn_res(a, w, dres, name, dep=None):
    pieces, t_, kp, npc = _pieces(a)
    d_ = w.shape[1]
    tm, tk = _tile(t_, ROW_TILE, 8), _tile(kp, 512)
    per = kp // tk
    nk = npc * per

    def body(*refs):
        w_ref, dres_ref, o_ref, acc = refs[npc:]
        k = pl.program_id(1)

        @pl.when(k == 0)
        def _():
            acc[...] = jnp.zeros_like(acc)

        for p in range(npc):
            def piece_in(p=p):
                acc[...] += _dot(refs[p][...], w_ref[...], NN)

            _for_piece(k, p, per, npc, piece_in)

        @pl.when(k == nk - 1)
        def _():
            def rows_out(rows):
                o_ref[rows, :] = acc[rows, :] + ALPHA * dres_ref[rows, :]

            _for_row_blocks(tm, rows_out)

    body, dep_specs, deps = _after(body, npc + 2, dep)
    row = _once((tm, d_), lambda i, k: (i, 0))
    return pl.pallas_call(
        body, name=name, grid=(t_ // tm, nk),
        in_specs=[pl.BlockSpec((tm, tk), lambda i, k, p=p: (i, jnp.clip(k - p * per, 0, per - 1))) for p in range(npc)]
        + [pl.BlockSpec((tk, d_), lambda i, k: (k, 0)), row, *dep_specs],
        out_specs=row,
        out_shape=jax.ShapeDtypeStruct((t_, d_), F32),
        scratch_shapes=[pltpu.VMEM((tm, d_), F32)],
        compiler_params=_params(),
    )(*pieces, w, dres, *deps)


def _dz_gate_bwd(drb, w_out, proj, ya, yb, d_, name, dep=None):
    t_ = drb.shape[0]
    goff = 5 * d_ + QKV_WIDTH
    tm, tn = _tile(t_, ROW_TILE, 8), _tile_multi([d_, goff], 512)
    ja, jb = goff // tn, (goff + d_) // tn

    def body(dr_ref, w_ref, ga_ref, gb_ref, ya_ref, yb_ref, dya_ref, dyb_ref, dga_ref, dgb_ref):
        dz = _dot(dr_ref[...], w_ref[...], NT)
        sa, sb = _sigmoid(ga_ref[...]), _sigmoid(gb_ref[...])
        dya_ref[...] = (dz * sa).astype(BF16)
        dyb_ref[...] = (dz * sb).astype(BF16)
        dga_ref[...] = (dz * ya_ref[...] * sa * (1.0 - sa)).astype(BF16)
        dgb_ref[...] = (dz * yb_ref[...] * sb * (1.0 - sb)).astype(BF16)

    body, dep_specs, deps = _after(body, 6, dep)
    tile = pl.BlockSpec((tm, tn), lambda i, j: (i, j))
    return pl.pallas_call(
        body, name=name, grid=(t_ // tm, d_ // tn),
        in_specs=[pl.BlockSpec((tm, d_), lambda i, j: (i, 0)),
                  pl.BlockSpec((tn, d_), lambda i, j: (j, 0)),
                  pl.BlockSpec((tm, tn), lambda i, j: (i, ja + j)),
                  pl.BlockSpec((tm, tn), lambda i, j: (i, jb + j)), tile, tile, *dep_specs],
        out_specs=[tile] * 4,
        out_shape=[jax.ShapeDtypeStruct((t_, d_), BF16)] * 4,
        compiler_params=_params(),
    )(drb, w_out, proj, proj, ya, yb, *deps)


def _ln_loss_bwd(r, target, g, b, name):
    t_, d_ = r.shape
    tm = _tile(t_, 256, 8)

    def body(r_ref, t_ref, g_ref, b_ref, dr_ref, drb_ref, dg_ref, db_ref, loss_ref):
        i = pl.program_id(0)

        @pl.when(i == 0)
        def _():
            dg_ref[...] = jnp.zeros_like(dg_ref)
            db_ref[...] = jnp.zeros_like(db_ref)
            loss_ref[...] = jnp.zeros_like(loss_ref)

        xhat, rstd = _ln_stats(r_ref[...])
        gain = g_ref[...]
        err = xhat * gain + b_ref[...] - t_ref[...]
        loss_ref[...] += (0.5 / d_) * jnp.sum(err * err)
        dy = err * (1.0 / d_)
        dr = _ln_bwd(dy, xhat, rstd, gain)
        dr_ref[...] = dr
        drb_ref[...] = dr.astype(BF16)
        dg_ref[...] += jnp.sum(dy * xhat, axis=0, keepdims=True)
        db_ref[...] += jnp.sum(dy, axis=0, keepdims=True)

    row = pl.BlockSpec((tm, d_), lambda i: (i, 0))
    vec = pl.BlockSpec((1, d_), lambda i: (0, 0))
    return pl.pallas_call(
        body, name=name, grid=(t_ // tm,),
        in_specs=[row, row, vec, vec],
        out_specs=[row, row, vec, vec, pl.BlockSpec((1, HEAD), lambda i: (0, 0))],
        out_shape=[jax.ShapeDtypeStruct((t_, d_), F32), jax.ShapeDtypeStruct((t_, d_), BF16),
                   jax.ShapeDtypeStruct((1, d_), F32), jax.ShapeDtypeStruct((1, d_), F32),
                   jax.ShapeDtypeStruct((1, HEAD), F32)],
        compiler_params=_params(),
    )(r, target, g, b)


def _chunk_scan(x, row, reverse, size):
    s = 1
    while s < CHUNK:
        if reverse:
            x = x + jnp.where(row < CHUNK - s, pltpu.roll(x, size - s, 0), 0.0)
        else:
            x = x + jnp.where(row >= s, pltpu.roll(x, s, 0), 0.0)
        s *= 2
    return x


def _lower_bound(tab):
    return _sigmoid(tab[0:1, :] - tab[1:2, :])


def _tri_mask(reverse):
    r = lax.broadcasted_iota(jnp.int32, (CHUNK, CHUNK), 0)
    c = lax.broadcasted_iota(jnp.int32, (CHUNK, CHUNK), 1)
    return (c >= r) if reverse else (r >= c)


def _hgrn_fwd(proj, lbf, lbb, ng, b_, s_, d_, name):
    h_ = d_ // HEAD
    nc = s_ // CHUNK

    def body(hq_ref, hff_ref, hfb_ref, hi_ref, hog_ref, lbf_ref, lbb_ref, ng_ref, ya_ref, o_ref,
             q_s, k_s, cum_s, o_s):
        row = lax.broadcasted_iota(jnp.int32, (s_, HEAD), 0) % CHUNK
        hq = hq_ref[...]
        q_s[...] = hq * _sigmoid(hq)
        o_s[...] = jnp.zeros_like(o_s)
        for reverse, hf_ref, lb_ref in ((False, hff_ref, lbf_ref), (True, hfb_ref, lbb_ref)):
            lb = _lower_bound(lb_ref[...])
            f = lb + (1.0 - lb) * _sigmoid(hf_ref[...])
            k_s[...] = 1.0 - f
            cum_s[...] = _chunk_scan(jnp.log(f), row, reverse, s_)
            mask = _tri_mask(reverse)

            def step(n, st, reverse=reverse, mask=mask):
                idx = (nc - 1 - n) if reverse else n
                sl = pl.ds(pl.multiple_of(idx * CHUNK, CHUNK), CHUNK)
                cm = cum_s[sl, :]
                tot = cm[0:1, :] if reverse else cm[CHUNK - 1:CHUNK, :]
                qc, kc = q_s[sl, :], k_s[sl, :]
                vb = hi_ref[sl, :].astype(BF16)
                qd = (qc * jnp.exp(cm)).astype(BF16)
                kd = (kc * jnp.exp(-cm)).astype(BF16)
                ke = (kc * jnp.exp(tot - cm)).astype(BF16)
                a = jnp.where(mask, _dot(qd, kd, NT), 0.0)
                o_s[sl, :] += _dot(a.astype(BF16), vb, NN) + _dot(qd, st.astype(BF16), NT)
                return st * jnp.exp(tot) + _dot(vb, ke, TN)

            lax.fori_loop(0, nc, step, jnp.zeros((HEAD, HEAD), F32))
        o = o_s[...]
        o_ref[...] = o
        nrm = o * lax.rsqrt(jnp.mean(o * o, axis=-1, keepdims=True) + LN_EPS)
        hog = hog_ref[...]
        ya_ref[...] = (nrm * ng_ref[...] * hog * _sigmoid(hog)).astype(BF16)

    def col(part):
        return pl.BlockSpec((s_, HEAD), lambda h, b, part=part: (b, part * h_ + h))

    tab = pl.BlockSpec((2, HEAD), lambda h, b: (0, h))
    out = pl.BlockSpec((s_, HEAD), lambda h, b: (b, h))
    return pl.pallas_call(
        body, name=name, grid=(h_, b_),
        in_specs=[col(0), col(1), col(2), col(3), col(4), tab, tab,
                  pl.BlockSpec((1, HEAD), lambda h, b: (0, h))],
        out_specs=[out, out],
        out_shape=[jax.ShapeDtypeStruct((b_ * s_, d_), BF16), jax.ShapeDtypeStruct((b_ * s_, d_), F32)],
        scratch_shapes=[pltpu.VMEM((s_, HEAD), F32)] * 4,
        compiler_params=_params(),
    )(proj, proj, proj, proj, proj, lbf, lbb, ng)


def _hgrn_bwd(proj, lbf, lbb, ng, o_sum, dya, b_, s_, d_, name):
    h_ = d_ // HEAD
    nc = s_ // CHUNK

    def body(hq_ref, hff_ref, hfb_ref, hi_ref, hog_ref, lbf_ref, lbb_ref, ng_ref, o_ref, dya_ref,
             dhq_ref, dhff_ref, dhfb_ref, dhi_ref, dhog_ref, dng_ref, dlbf_ref, dlbb_ref,
             q_s, k_s, cum_s, do_s, dq_s, dv_s, db_s, dk_s, st_s):
        b = pl.program_id(1)

        @pl.when(b == 0)
        def _():
            dng_ref[...] = jnp.zeros_like(dng_ref)
            dlbf_ref[...] = jnp.zeros_like(dlbf_ref)
            dlbb_ref[...] = jnp.zeros_like(dlbb_ref)

        row = lax.broadcasted_iota(jnp.int32, (s_, HEAD), 0) % CHUNK
        crow = lax.broadcasted_iota(jnp.int32, (CHUNK, HEAD), 0)
        hq = hq_ref[...]
        sq = _sigmoid(hq)
        q_s[...] = hq * sq
        o = o_ref[...]
        rinv = lax.rsqrt(jnp.mean(o * o, axis=-1, keepdims=True) + LN_EPS)
        nrm = o * rinv
        hog = hog_ref[...]
        so = _sigmoid(hog)
        gain = ng_ref[...]
        dy = dya_ref[...]
        dhog_ref[...] = (dy * nrm * gain * _dsilu(hog, so)).astype(BF16)
        dng_ref[...] += jnp.sum(dy * nrm * hog * so, axis=0, keepdims=True)
        dn = dy * gain * hog * so
        do_s[...] = rinv * (dn - nrm * jnp.mean(dn * nrm, axis=-1, keepdims=True))
        dq_s[...] = jnp.zeros_like(dq_s)
        dv_s[...] = jnp.zeros_like(dv_s)

        for reverse, hf_ref, lb_ref, dhf_ref, dlb_ref in (
                (False, hff_ref, lbf_ref, dhff_ref, dlbf_ref), (True, hfb_ref, lbb_ref, dhfb_ref, dlbb_ref)):
            tab = lb_ref[...]
            lb = _lower_bound(tab)
            sf = _sigmoid(hf_ref[...])
            f = lb + (1.0 - lb) * sf
            k_s[...] = 1.0 - f
            cum_s[...] = _chunk_scan(jnp.log(f), row, reverse, s_)
            mask = _tri_mask(reverse)
            last = 0 if reverse else CHUNK - 1

            def chunk(idx, reverse=reverse):
                sl = pl.ds(pl.multiple_of(idx * CHUNK, CHUNK), CHUNK)
                cm = cum_s[sl, :]
                tot = cm[0:1, :] if reverse else cm[CHUNK - 1:CHUNK, :]
                return sl, cm, tot

            def fstep(n, st, reverse=reverse, chunk=chunk):
                idx = (nc - 1 - n) if reverse else n
                sl, cm, tot = chunk(idx)
                st_s[idx] = st
                ke = (k_s[sl, :] * jnp.exp(tot - cm)).astype(BF16)
                return st * jnp.exp(tot) + _dot(hi_ref[sl, :].astype(BF16), ke, TN)

            lax.fori_loop(0, nc, fstep, jnp.zeros((HEAD, HEAD), F32))

            def bstep(n, dst, reverse=reverse, chunk=chunk, mask=mask, last=last):
                idx = n if reverse else (nc - 1 - n)
                sl, cm, tot = chunk(idx)
                eb, enb, ee, dec = jnp.exp(cm), jnp.exp(-cm), jnp.exp(tot - cm), jnp.exp(tot)
                qc, kc = q_s[sl, :], k_s[sl, :]
                qd, kd, ke = qc * eb, kc * enb, kc * ee
                qdb, kdb, keb = qd.astype(BF16), kd.astype(BF16), ke.astype(BF16)
                vb = hi_ref[sl, :].astype(BF16)
                dob = do_s[sl, :].astype(BF16)
                st0 = st_s[idx]
                dstb = dst.astype(BF16)
                a = jnp.where(mask, _dot(qdb, kdb, NT), 0.0).astype(BF16)
                da = jnp.where(mask, _dot(dob, vb, NT), 0.0).astype(BF16)
                dqd = _dot(da, kdb, NN) + _dot(dob, st0.astype(BF16), NN)
                dkd = _dot(da, qdb, TN)
                dv = _dot(a, dob, TN) + _dot(keb, dstb, NT)
                dke = _dot(vb, dstb, NN)
                ddec = jnp.sum(dst * st0, axis=0, keepdims=True)
                dtot = jnp.sum(dke * ke, axis=0, keepdims=True) + ddec * dec
                db = dqd * qd - dkd * kd - dke * ke
                db_s[sl, :] = db + jnp.where(crow == last, dtot, 0.0)
                dk_s[sl, :] = dkd * enb + dke * ee
                dq_s[sl, :] += dqd * eb
                dv_s[sl, :] += dv
                return dst * dec + _dot(dob, qdb, TN)

            lax.fori_loop(0, nc, bstep, jnp.zeros((HEAD, HEAD), F32))
            dlogf = _chunk_scan(db_s[...], row, not reverse, s_)
            df = dlogf / f - dk_s[...]
            dhf_ref[...] = (df * (1.0 - lb) * sf * (1.0 - sf)).astype(BF16)
            dlb = jnp.sum(df * (1.0 - sf), axis=0, keepdims=True) * lb * (1.0 - lb)
            dlb_ref[0:1, :] += dlb
            dlb_ref[1:2, :] -= dlb

        dhq_ref[...] = (dq_s[...] * _dsilu(hq, sq)).astype(BF16)
        dhi_ref[...] = dv_s[...].astype(BF16)

    def col(part):
        return pl.BlockSpec((s_, HEAD), lambda h, b, part=part: (b, part * h_ + h))

    tab = pl.BlockSpec((2, HEAD), lambda h, b: (0, h))
    vec = pl.BlockSpec((1, HEAD), lambda h, b: (0, h))
    blk = pl.BlockSpec((s_, HEAD), lambda h, b: (b, h))
    act = jax.ShapeDtypeStruct((b_ * s_, d_), BF16)
    return pl.pallas_call(
        body, name=name, grid=(h_, b_),
        in_specs=[col(0), col(1), col(2), col(3), col(4), tab, tab, vec, blk, blk],
        out_specs=[blk] * 5 + [vec, tab, tab],
        out_shape=[act] * 5 + [jax.ShapeDtypeStruct((1, d_), F32), jax.ShapeDtypeStruct((2, d_), F32),
                               jax.ShapeDtypeStruct((2, d_), F32)],
        scratch_shapes=[pltpu.VMEM((s_, HEAD), F32)] * 8 + [pltpu.VMEM((nc, HEAD, HEAD), F32)],
        compiler_params=_params(),
    )(proj, proj, proj, proj, proj, lbf, lbb, ng, o_sum, dya)


FWD_BLOCK = 128
BWD_BLOCK = 64


def _block_scan(x, row, reverse, size, blk):
    s = 1
    while s < blk:
        if reverse:
            x = x + jnp.where(row < blk - s, pltpu.roll(x, size - s, 0), 0.0)
        else:
            x = x + jnp.where(row >= s, pltpu.roll(x, s, 0), 0.0)
        s *= 2
    return x


def _block_exps(l, reverse):
    blk = l.shape[0]
    half = blk // 2
    first = lax.broadcasted_iota(jnp.int32, (blk, HEAD), 0) < half
    q1, q3 = half // 2, half + half // 2
    if reverse:
        rho1, rho2, lh, ltot = l[q1:q1 + 1], l[q3:q3 + 1], l[half:half + 1], l[0:1]
    else:
        rho1, rho2, lh, ltot = l[q1 - 1:q1], l[q3 - 1:q3], l[half - 1:half], l[blk - 1:blk]
    ref = jnp.where(first, rho1, rho2)
    query_half = first if reverse else jnp.logical_not(first)
    e2 = jnp.where(query_half, jnp.exp(jnp.minimum(l - lh, 0.0)), 0.0)
    e1 = jnp.where(query_half, 0.0, jnp.exp(jnp.minimum(lh - l, 0.0)))
    return jnp.exp(l - ref), jnp.exp(ref - l), e2, e1, jnp.exp(l), jnp.exp(ltot - l), jnp.exp(ltot)


def _half_mask(reverse, blk):
    r = lax.broadcasted_iota(jnp.int32, (blk, blk), 0)
    c = lax.broadcasted_iota(jnp.int32, (blk, blk), 1)
    same = (r < blk // 2) == (c < blk // 2)
    return same & ((c >= r) if reverse else (r >= c))


def _hgrn_fwd(proj, lbf, lbb, ng, b_, s_, d_, name):
    h_ = d_ // HEAD
    BLOCK = min(FWD_BLOCK, s_)
    nb = s_ // BLOCK

    def body(hq_ref, hff_ref, hfb_ref, hi_ref, hog_ref, lbf_ref, lbb_ref, ng_ref, ya_ref, o_ref,
             q_s, k_s, l_s, of_s, oi_s, qd_s, u_s, st_s, dec_s):
        row = lax.broadcasted_iota(jnp.int32, (s_, HEAD), 0) % BLOCK
        hq = hq_ref[...]
        q_s[...] = hq * _sigmoid(hq)
        for reverse, hf_ref, lb_ref in ((False, hff_ref, lbf_ref), (True, hfb_ref, lbb_ref)):
            lb = _lower_bound(lb_ref[...])
            f = lb + (1.0 - lb) * _sigmoid(hf_ref[...])
            k_s[...] = 1.0 - f
            l_s[...] = _block_scan(jnp.log(f), row, reverse, s_, BLOCK)
            mask = _half_mask(reverse, BLOCK)

            def inside(n, carry, reverse=reverse, mask=mask):
                sl = pl.ds(pl.multiple_of(n * BLOCK, BLOCK), BLOCK)
                eq, ek, e2, e1, el, ee, dec = _block_exps(l_s[sl, :], reverse)
                qc, kc = q_s[sl, :], k_s[sl, :]
                vb = hi_ref[sl, :].astype(BF16)
                a = jnp.where(mask, _dot((qc * eq).astype(BF16), (kc * ek).astype(BF16), NT), 0.0)
                a = a + _dot((qc * e2).astype(BF16), (kc * e1).astype(BF16), NT)
                oi_s[sl, :] = _dot(a.astype(BF16), vb, NN)
                qd_s[sl, :] = (qc * el).astype(BF16)
                u_s[n] = _dot(vb, (kc * ee).astype(BF16), TN)
                dec_s[n] = jnp.broadcast_to(dec, (8, HEAD))
                return carry

            lax.fori_loop(0, nb, inside, 0, unroll=4)

            def carry_state(n, st, reverse=reverse):
                idx = (nb - 1 - n) if reverse else n
                st_s[idx] = st.astype(BF16)
                return st * dec_s[idx][0:1, :] + u_s[idx]

            lax.fori_loop(0, nb, carry_state, jnp.zeros((HEAD, HEAD), F32))

            def across(n, carry, reverse=reverse):
                sl = pl.ds(pl.multiple_of(n * BLOCK, BLOCK), BLOCK)
                o_dir = oi_s[sl, :] + _dot(qd_s[sl, :], st_s[n], NT)
                if not reverse:
                    of_s[sl, :] = o_dir
                else:
                    o = of_s[sl, :] + o_dir
                    o_ref[sl, :] = o
                    nrm = o * lax.rsqrt(jnp.mean(o * o, axis=-1, keepdims=True) + LN_EPS)
                    hog = hog_ref[sl, :]
                    ya_ref[sl, :] = (nrm * ng_ref[...] * hog * _sigmoid(hog)).astype(BF16)
                return carry

            lax.fori_loop(0, nb, across, 0, unroll=8)

    def col(part):
        return pl.BlockSpec((s_, HEAD), lambda h, b, part=part: (b, part * h_ + h))

    tab = pl.BlockSpec((2, HEAD), lambda h, b: (0, h))
    out = pl.BlockSpec((s_, HEAD), lambda h, b: (b, h))
    return pl.pallas_call(
        body, name=name, grid=(h_, b_),
        in_specs=[col(0), col(1), col(2), col(3), col(4), tab, tab,
                  pl.BlockSpec((1, HEAD), lambda h, b: (0, h))],
        out_specs=[out, out],
        out_shape=[jax.ShapeDtypeStruct((b_ * s_, d_), BF16), jax.ShapeDtypeStruct((b_ * s_, d_), F32)],
        scratch_shapes=[pltpu.VMEM((s_, HEAD), F32)] * 5 + [
            pltpu.VMEM((s_, HEAD), BF16), pltpu.VMEM((nb, HEAD, HEAD), F32), pltpu.VMEM((nb, HEAD, HEAD), BF16),
            pltpu.VMEM((nb, 8, HEAD), F32)],
        compiler_params=_params(),
    )(proj, proj, proj, proj, proj, lbf, lbb, ng)


def _hgrn_bwd(proj, lbf, lbb, ng, o_sum, dya, b_, s_, d_, name):
    h_ = d_ // HEAD
    BLOCK = min(BWD_BLOCK, s_)
    nb = s_ // BLOCK

    def body(hq_ref, hff_ref, hfb_ref, hi_ref, hog_ref, lbf_ref, lbb_ref, ng_ref, o_ref, dya_ref,
             dhq_ref, dhff_ref, dhfb_ref, dhi_ref, dhog_ref, dng_ref, dlbf_ref, dlbb_ref,
             q_s, k_s, l_s, do_s, dq_s, dv_s, dl_s, dk_s, u_s, w_s, st_s, dst_s, dec_s):
        b = pl.program_id(1)

        @pl.when(b == 0)
        def _():
            dng_ref[...] = jnp.zeros_like(dng_ref)
            dlbf_ref[...] = jnp.zeros_like(dlbf_ref)
            dlbb_ref[...] = jnp.zeros_like(dlbb_ref)

        row = lax.broadcasted_iota(jnp.int32, (s_, HEAD), 0) % BLOCK
        brow = lax.broadcasted_iota(jnp.int32, (BLOCK, HEAD), 0)
        hq = hq_ref[...]
        sq = _sigmoid(hq)
        q_s[...] = hq * sq
        o = o_ref[...]
        rinv = lax.rsqrt(jnp.mean(o * o, axis=-1, keepdims=True) + LN_EPS)
        nrm = o * rinv
        hog = hog_ref[...]
        so = _sigmoid(hog)
        gain = ng_ref[...]
        dy = dya_ref[...]
        dhog_ref[...] = (dy * nrm * gain * _dsilu(hog, so)).astype(BF16)
        dng_ref[...] += jnp.sum(dy * nrm * hog * so, axis=0, keepdims=True)
        dn = dy * gain * hog * so
        do_s[...] = rinv * (dn - nrm * jnp.mean(dn * nrm, axis=-1, keepdims=True))

        for reverse, hf_ref, lb_ref, dhf_ref, dlb_ref in (
                (False, hff_ref, lbf_ref, dhff_ref, dlbf_ref), (True, hfb_ref, lbb_ref, dhfb_ref, dlbb_ref)):
            lb = _lower_bound(lb_ref[...])
            sf = _sigmoid(hf_ref[...])
            f = lb + (1.0 - lb) * sf
            k_s[...] = 1.0 - f
            l_s[...] = _block_scan(jnp.log(f), row, reverse, s_, BLOCK)
            mask = _half_mask(reverse, BLOCK)
            total_row = 0 if reverse else BLOCK - 1

            def prepare(n, carry, reverse=reverse):
                sl = pl.ds(pl.multiple_of(n * BLOCK, BLOCK), BLOCK)
                _, _, _, _, el, ee, dec = _block_exps(l_s[sl, :], reverse)
                vb = hi_ref[sl, :].astype(BF16)
                u_s[n] = _dot(vb, (k_s[sl, :] * ee).astype(BF16), TN)
                w_s[n] = _dot(do_s[sl, :].astype(BF16), (q_s[sl, :] * el).astype(BF16), TN)
                dec_s[n] = jnp.broadcast_to(dec, (8, HEAD))
                return carry

            lax.fori_loop(0, nb, prepare, 0, unroll=4)

            def carry_state(n, st, reverse=reverse):
                idx = (nb - 1 - n) if reverse else n
                st_s[idx] = st
                return st * dec_s[idx][0:1, :] + u_s[idx]

            lax.fori_loop(0, nb, carry_state, jnp.zeros((HEAD, HEAD), F32))

            def carry_grad(n, dst, reverse=reverse):
                idx = n if reverse else (nb - 1 - n)
                dst_s[idx] = dst
                return dst * dec_s[idx][0:1, :] + w_s[idx]

            lax.fori_loop(0, nb, carry_grad, jnp.zeros((HEAD, HEAD), F32))

            def inside(n, carry, reverse=reverse, mask=mask, total_row=total_row):
                sl = pl.ds(pl.multiple_of(n * BLOCK, BLOCK), BLOCK)
                eq, ek, e2, e1, el, ee, dec = _block_exps(l_s[sl, :], reverse)
                qc, kc = q_s[sl, :], k_s[sl, :]
                vb = hi_ref[sl, :].astype(BF16)
                dob = do_s[sl, :].astype(BF16)
                qt, kt, q2, k1 = ((qc * eq).astype(BF16), (kc * ek).astype(BF16),
                                  (qc * e2).astype(BF16), (kc * e1).astype(BF16))
                kend = kc * ee
                st0, dst1 = st_s[n], dst_s[n]
                dstb = dst1.astype(BF16)
                a = jnp.where(mask, _dot(qt, kt, NT), 0.0) + _dot(q2, k1, NT)
                da = _dot(dob, vb, NT)
                dab = da.astype(BF16)
                dad = jnp.where(mask, da, 0.0).astype(BF16)
                dqt, dkt = _dot(dad, kt, NN), _dot(dad, qt, TN)
                dq2, dk1 = _dot(dab, k1, NN), _dot(dab, q2, TN)
                dqd = _dot(dob, st0.astype(BF16), NN)
                dke = _dot(vb, dstb, NN)
                dv = _dot(a.astype(BF16), dob, TN) + _dot(kend.astype(BF16), dstb, NT)
                dq = dqt * eq + dq2 * e2 + dqd * el
                dk = dkt * ek + dk1 * e1 + dke * ee
                dtot = jnp.sum(dke * kend, axis=0, keepdims=True) + jnp.sum(dst1 * st0, axis=0, keepdims=True) * dec
                dl_s[sl, :] = qc * dq - kc * dk + jnp.where(brow == total_row, dtot, 0.0)
                dk_s[sl, :] = dk
                if not reverse:
                    dq_s[sl, :] = dq
                    dv_s[sl, :] = dv
                else:
                    hqc = hq_ref[sl, :]
                    dhq_ref[sl, :] = ((dq_s[sl, :] + dq) * _dsilu(hqc, _sigmoid(hqc))).astype(BF16)
                    dhi_ref[sl, :] = (dv_s[sl, :] + dv).astype(BF16)
                return carry

            lax.fori_loop(0, nb, inside, 0, unroll=2)
            dlogf = _block_scan(dl_s[...], row, not reverse, s_, BLOCK)
            df = dlogf / f - dk_s[...]
            dhf_ref[...] = (df * (1.0 - lb) * sf * (1.0 - sf)).astype(BF16)
            dlb = jnp.sum(df * (1.0 - sf), axis=0, keepdims=True) * lb * (1.0 - lb)
            dlb_ref[0:1, :] += dlb
            dlb_ref[1:2, :] -= dlb

    def col(part):
        return pl.BlockSpec((s_, HEAD), lambda h, b, part=part: (b, part * h_ + h))

    tab = pl.BlockSpec((2, HEAD), lambda h, b: (0, h))
    vec = pl.BlockSpec((1, HEAD), lambda h, b: (0, h))
    blk = pl.BlockSpec((s_, HEAD), lambda h, b: (b, h))
    act = jax.ShapeDtypeStruct((b_ * s_, d_), BF16)
    state = pltpu.VMEM((nb, HEAD, HEAD), F32)
    return pl.pallas_call(
        body, name=name, grid=(h_, b_),
        in_specs=[col(0), col(1), col(2), col(3), col(4), tab, tab, vec, blk, blk],
        out_specs=[blk] * 5 + [vec, tab, tab],
        out_shape=[act] * 5 + [jax.ShapeDtypeStruct((1, d_), F32), jax.ShapeDtypeStruct((2, d_), F32),
                               jax.ShapeDtypeStruct((2, d_), F32)],
        scratch_shapes=[pltpu.VMEM((s_, HEAD), F32)] * 8 + [state] * 4 + [pltpu.VMEM((nb, 8, HEAD), F32)],
        compiler_params=_params(),
    )(proj, proj, proj, proj, proj, lbf, lbb, ng, o_sum, dya)


def _rope_tables(s_):
    half = ROPE_DIM // 2
    inv_freq = ROPE_THETA ** (-jnp.arange(0, ROPE_DIM, 2, dtype=F32) / ROPE_DIM)
    ang = jnp.arange(s_, dtype=F32)[:, None] * inv_freq
    cos, sin = jnp.cos(ang), jnp.sin(ang)
    zeros = jnp.zeros((s_, HEAD - ROPE_DIM), F32)
    zh = jnp.zeros((s_, half), F32)
    c = jnp.concatenate([cos, cos, jnp.ones((s_, HEAD - ROPE_DIM), F32)], axis=1)
    s1 = jnp.concatenate([-sin, zh, zeros], axis=1)
    s2 = jnp.concatenate([zh, sin, zeros], axis=1)
    return c, s1, s2


def _rope(t, c, s1, s2):
    half = ROPE_DIM // 2
    return t * c + pltpu.roll(t, HEAD - half, 1) * s1 + pltpu.roll(t, half, 1) * s2


def _rope_bwd(dt, c, s1, s2):
    half = ROPE_DIM // 2
    return dt * c + pltpu.roll(dt * s1, half, 1) + pltpu.roll(dt * s2, HEAD - half, 1)


def _window_mask(r0, qb, wk, seg):
    row = lax.broadcasted_iota(jnp.int32, (qb, wk), 0)
    col = lax.broadcasted_iota(jnp.int32, (qb, wk), 1)
    kj = r0 - ATTN_HALF + col
    return (col - row >= 0) & (col - row <= 2 * ATTN_HALF) & (kj >= 0) & (kj < seg)


def _attn_fwd(qkv, tabs, b_, s_, dil, name):
    seg = s_ // dil
    qb = min(128, seg)
    nq, wk = seg // qb, qb + 2 * ATTN_HALF
    scale = HEAD ** -0.5
    ncol = QKV_GROUP // HEAD

    def body(q_ref, k_ref, v_ref, c_ref, s1_ref, s2_ref, o_ref, lse_ref, q_s, k_s, v_s):
        c, s1, s2 = c_ref[...], s1_ref[...], s2_ref[...]
        q_s[...] = _rope(q_ref[...], c, s1, s2).astype(BF16)
        k_s[...] = jnp.zeros_like(k_s)
        v_s[...] = jnp.zeros_like(v_s)
        k_s[ATTN_HALF:ATTN_HALF + seg, :] = _rope(k_ref[...], c, s1, s2).astype(BF16)
        v_s[ATTN_HALF:ATTN_HALF + seg, :] = v_ref[...].astype(BF16)

        def step(i, carry):
            r0 = pl.multiple_of(i * qb, qb)
            sc = _dot(q_s[pl.ds(r0, qb), :], k_s[pl.ds(r0, wk), :], NT) * scale
            sc = jnp.where(_window_mask(r0, qb, wk, seg), sc, NEG_INF)
            m = jnp.max(sc, axis=-1, keepdims=True)
            p = jnp.exp(sc - m)
            den = jnp.sum(p, axis=-1, keepdims=True)
            o_ref[pl.ds(r0, qb), :] = _dot(p.astype(BF16), v_s[pl.ds(r0, wk), :], NN) / den
            lse_ref[pl.ds(r0, qb), :] = jnp.broadcast_to(m + jnp.log(den), (qb, HEAD))
            return carry

        lax.fori_loop(0, nq, step, 0)

    def col(part):
        return pl.BlockSpec((seg, HEAD), lambda b, r, h, part=part: (b, r * ncol + part * ATTN_HEADS + h))

    tab = pl.BlockSpec((seg, HEAD), lambda b, r, h: (0, r))
    out = pl.BlockSpec((seg, HEAD), lambda b, r, h: (b, r * ATTN_HEADS + h))
    shape = jax.ShapeDtypeStruct((b_ * seg, dil * ATTN_OUT), F32)
    return pl.pallas_call(
        body, name=name, grid=(b_, dil, ATTN_HEADS),
        in_specs=[col(0), col(1), col(2), tab, tab, tab],
        out_specs=[out, out],
        out_shape=[shape, shape],
        scratch_shapes=[pltpu.VMEM((seg, HEAD), BF16), pltpu.VMEM((seg + 2 * ATTN_HALF, HEAD), BF16),
                        pltpu.VMEM((seg + 2 * ATTN_HALF, HEAD), BF16)],
        compiler_params=_params(),
    )(qkv, qkv, qkv, *tabs)


def _attn_bwd(qkv, tabs, dog, cg, lse, b_, s_, dil, name):
    seg = s_ // dil
    qb = min(128, seg)
    nq, wk = seg // qb, qb + 2 * ATTN_HALF
    scale = HEAD ** -0.5
    ncol = QKV_GROUP // HEAD

    def body(q_ref, k_ref, v_ref, c_ref, s1_ref, s2_ref, do_ref, cg_ref, lse_ref, dq_ref, dk_ref, dv_ref,
             q_s, k_s, v_s, dk_s, dv_s):
        c, s1, s2 = c_ref[...], s1_ref[...], s2_ref[...]
        q_s[...] = _rope(q_ref[...], c, s1, s2).astype(BF16)
        k_s[...] = jnp.zeros_like(k_s)
        v_s[...] = jnp.zeros_like(v_s)
        k_s[ATTN_HALF:ATTN_HALF + seg, :] = _rope(k_ref[...], c, s1, s2).astype(BF16)
        v_s[ATTN_HALF:ATTN_HALF + seg, :] = v_ref[...].astype(BF16)
        dk_s[...] = jnp.zeros_like(dk_s)
        dv_s[...] = jnp.zeros_like(dv_s)

        def step(i, carry):
            r0 = pl.multiple_of(i * qb, qb)
            rows, win = pl.ds(r0, qb), pl.ds(r0, wk)
            qc, kw, vw = q_s[rows, :], k_s[win, :], v_s[win, :]
            sc = _dot(qc, kw, NT) * scale
            p = jnp.where(_window_mask(r0, qb, wk, seg), jnp.exp(sc - lse_ref[rows, 0:1]), 0.0)
            dob = do_ref[rows, :].astype(BF16)
            dp = _dot(dob, vw, NT)
            ds = (p * (dp + cg_ref[rows, 0:1]) * scale).astype(BF16)
            dq = _dot(ds, kw, NN)
            dq_ref[rows, :] = _rope_bwd(dq, c_ref[rows, :], s1_ref[rows, :], s2_ref[rows, :]).astype(BF16)
            dk_s[win, :] += _dot(ds, qc, TN)
            dv_s[win, :] += _dot(p.astype(BF16), dob, TN)
            return carry

        lax.fori_loop(0, nq, step, 0)
        dk_ref[...] = _rope_bwd(dk_s[ATTN_HALF:ATTN_HALF + seg, :], c, s1, s2).astype(BF16)
        dv_ref[...] = dv_s[ATTN_HALF:ATTN_HALF + seg, :].astype(BF16)

    def col(part):
        return pl.BlockSpec((seg, HEAD), lambda b, r, h, part=part: (b, r * ncol + part * ATTN_HEADS + h))

    tab = pl.BlockSpec((seg, HEAD), lambda b, r, h: (0, r))
    out = pl.BlockSpec((seg, HEAD), lambda b, r, h: (b, r * ATTN_HEADS + h))
    shape = jax.ShapeDtypeStruct((b_ * seg, dil * ATTN_OUT), BF16)
    return pl.pallas_call(
        body, name=name, grid=(b_, dil, ATTN_HEADS),
        in_specs=[col(0), col(1), col(2), tab, tab, tab, out, out, out],
        out_specs=[out, out, out],
        out_shape=[shape, shape, shape],
        scratch_shapes=[pltpu.VMEM((seg, HEAD), BF16), pltpu.VMEM((seg + 2 * ATTN_HALF, HEAD), BF16),
                        pltpu.VMEM((seg + 2 * ATTN_HALF, HEAD), BF16),
                        pltpu.VMEM((seg + 2 * ATTN_HALF, HEAD), F32), pltpu.VMEM((seg + 2 * ATTN_HALF, HEAD), F32)],
        compiler_params=_params(),
    )(qkv, qkv, qkv, *tabs, dog, cg, lse)


def _group_weights(lses):
    m = jnp.maximum(jnp.maximum(lses[0], lses[1]), lses[2])
    es = [jnp.exp(l - m) for l in lses]
    den = es[0] + es[1] + es[2]
    return [e / den for e in es]


def _combine_fwd(outs, lses, name):
    t_, w_ = outs[0].shape
    tm = _tile(t_, 512, 8)
    ng = len(outs)

    def body(*refs):
        ws = _group_weights([r[...] for r in refs[ng:2 * ng]])
        acc = ws[0] * refs[0][...]
        for g in range(1, ng):
            acc = acc + ws[g] * refs[g][...]
        refs[2 * ng][...] = acc.astype(BF16)

    row = pl.BlockSpec((tm, w_), lambda i: (i, 0))
    return pl.pallas_call(
        body, name=name, grid=(t_ // tm,), in_specs=[row] * (2 * ng), out_specs=row,
        out_shape=jax.ShapeDtypeStruct((t_, w_), BF16), compiler_params=_params(),
    )(*outs, *lses)


def _combine_bwd(dob, outs, lses, name):
    t_, w_ = outs[0].shape
    tm = _tile(t_, 512, 8)
    ng = len(outs)

    def body(*refs):
        do = refs[0][...]
        os_ = [r[...] for r in refs[1:1 + ng]]
        ws = _group_weights([r[...] for r in refs[1 + ng:1 + 2 * ng]])
        o = ws[0] * os_[0]
        for g in range(1, ng):
            o = o + ws[g] * os_[g]
        prod = do * o
        heads = [jnp.broadcast_to(jnp.sum(prod[:, h * HEAD:(h + 1) * HEAD], axis=-1, keepdims=True), (tm, HEAD))
                 for h in range(w_ // HEAD)]
        tot = jnp.concatenate(heads, axis=1)
        for g in range(ng):
            refs[1 + 2 * ng + g][...] = ws[g] * do
            refs[1 + 3 * ng + g][...] = -ws[g] * tot

    row = pl.BlockSpec((tm, w_), lambda i: (i, 0))
    shape = jax.ShapeDtypeStruct((t_, w_), F32)
    res = pl.pallas_call(
        body, name=name, grid=(t_ // tm,), in_specs=[row] * (1 + 2 * ng), out_specs=[row] * (2 * ng),
        out_shape=[shape] * (2 * ng), compiler_params=_params(),
    )(dob, *outs, *lses)
    return res[:ng], res[ng:]


def _adam_update(w, g, m, v):
    m = ADAM_B1 * m + (1.0 - ADAM_B1) * g
    v = ADAM_B2 * v + (1.0 - ADAM_B2) * (g * g)
    m_hat = m / (1.0 - ADAM_B1 ** ADAM_STEP)
    v_hat = v / (1.0 - ADAM_B2 ** ADAM_STEP)
    return -ADAM_LR * (m_hat / (jnp.sqrt(v_hat) + ADAM_EPS) + ADAM_WD * w), m, v


def _adam(w, g, m, v, name):
    r_, c_ = w.shape
    tr = _tile(r_, 256, 8)

    def body(w_ref, g_ref, m_ref, v_ref, d_ref, mo_ref, vo_ref):
        d_ref[...], mo_ref[...], vo_ref[...] = _adam_update(w_ref[...], g_ref[...], m_ref[...], v_ref[...])

    blk = pl.BlockSpec((tr, c_), lambda i: (i, 0))
    shape = jax.ShapeDtypeStruct((r_, c_), F32)
    return pl.pallas_call(
        body, name=name, grid=(r_ // tr,), in_specs=[blk] * 4, out_specs=[blk] * 3,
        out_shape=[shape] * 3, compiler_params=_params(),
    )(w, g, m, v)


def _sum_partials(recv, name):
    n_, r_, c_ = recv.shape
    tr = _tile(r_, 128, 16)

    def body(p_ref, o_ref):
        acc = p_ref[0].astype(F32)
        for i in range(1, n_):
            acc = acc + p_ref[i].astype(F32)
        o_ref[...] = acc

    return pl.pallas_call(
        body, name=name, grid=(r_ // tr,),
        in_specs=[pl.BlockSpec((n_, tr, c_), lambda i: (0, i, 0))],
        out_specs=pl.BlockSpec((tr, c_), lambda i: (i, 0)),
        out_shape=jax.ShapeDtypeStruct((r_, c_), F32), compiler_params=_params(),
    )(recv)


def _small_sum_adam(parts, w, m, v, name):
    n_, r_, c_ = parts.shape

    def body(p_ref, w_ref, m_ref, v_ref, g_ref, d_ref, mo_ref, vo_ref):
        g = p_ref[0]
        for i in range(1, n_):
            g = g + p_ref[i]
        g_ref[...] = g
        d_ref[...], mo_ref[...], vo_ref[...] = _adam_update(w_ref[...], g, m_ref[...], v_ref[...])

    shape = jax.ShapeDtypeStruct((r_, c_), F32)
    return pl.pallas_call(body, name=name, out_shape=[shape] * 4, compiler_params=_params())(parts, w, m, v)


def _my_place():
    x, y, c = lax.axis_index("x"), lax.axis_index("y"), lax.axis_index("c")
    return x, y, c


def _peer(x, y, c, d):
    px = 1 - x if d & 4 else x
    py = 1 - y if d & 2 else y
    pc = 1 - c if d & 1 else c
    return (px, py, pc), 4 * px + 2 * py + pc


def _all_gather(shards, name):
    nw = len(shards)

    def body(*refs):
        ins, outs = refs[:nw], refs[nw:2 * nw]
        send_sems, recv_sems, local_sems = refs[2 * nw:]
        x, y, c = _my_place()
        me = 4 * x + 2 * y + c
        copies = []
        for k in range(nw):
            rows = shards[k].shape[0]
            mine = outs[k].at[pl.ds(pl.multiple_of(me * rows, 16), rows), :]
            local = pltpu.make_async_copy(ins[k], mine, local_sems.at[k])
            local.start()
            copies.append(local)
            for d in range(1, N_DEV):
                place, _ = _peer(x, y, c, d)
                remote = pltpu.make_async_remote_copy(
                    src_ref=ins[k], dst_ref=mine, send_sem=send_sems.at[d - 1, k], recv_sem=recv_sems.at[d - 1, k],
                    device_id=place, device_id_type=MESH)
                remote.start()
                copies.append(remote)
        for cp in copies:
            cp.wait()

    hbm = pl.BlockSpec(memory_space=pl.ANY)
    return pl.pallas_call(
        body, name=name, in_specs=[hbm] * nw, out_specs=[hbm] * nw,
        out_shape=[jax.ShapeDtypeStruct((N_DEV * s.shape[0], s.shape[1]), s.dtype) for s in shards],
        scratch_shapes=[pltpu.SemaphoreType.DMA((N_DEV - 1, nw)), pltpu.SemaphoreType.DMA((N_DEV - 1, nw)),
                        pltpu.SemaphoreType.DMA((nw,))],
    )(*shards)


HBM_SPEC = pl.BlockSpec(memory_space=pltpu.HBM)
SEM_SPEC = pl.BlockSpec(memory_space=pltpu.SEMAPHORE)
EFFECT = pltpu.SideEffectType.DATAFLOW_SIDE_EFFECTING


def _in_hbm(a):
    return pltpu.with_memory_space_constraint(a, pltpu.HBM)


def _token_shape():
    return jax.ShapeDtypeStruct((8, HEAD), F32)


SIBLING = 1
OTHER_CHIPS = (4, 2, 6)


def _rows_of(ref, num, rows):
    return ref.at[pl.ds(pl.multiple_of(num * rows, 16), rows), :]


def _gather_start(shards, name):
    nw = len(shards)
    lands = [lax.empty((N_DEV * s.shape[0], s.shape[1]), s.dtype) for s in shards]
    n_to = 1 + len(OTHER_CHIPS)

    def body(*refs):
        ins, lnd = refs[:nw], refs[nw:2 * nw]
        send, from_sib, from_chips, own = (refs[(2 + i) * nw:(3 + i) * nw] for i in range(4))
        token = refs[8 * nw]
        x, y, c = _my_place()
        me = 4 * x + 2 * y + c
        for k in range(nw):
            mine = _rows_of(lnd[k], me, shards[k].shape[0])
            pltpu.make_async_copy(ins[k], mine, own[k]).start()
            for i, d in enumerate((SIBLING,) + OTHER_CHIPS):
                place, _ = _peer(x, y, c, d)
                pltpu.make_async_remote_copy(
                    src_ref=ins[k], dst_ref=mine, send_sem=send[k].at[i],
                    recv_sem=from_sib[k] if i == 0 else from_chips[k].at[i - 1],
                    device_id=place, device_id_type=MESH).start()
        token[...] = jnp.zeros_like(token)

    dma = pltpu.SemaphoreType.DMA
    sems = [dma((n_to,))] * nw + [dma(())] * nw + [dma((len(OTHER_CHIPS),))] * nw + [dma(())] * nw
    thru = [pltpu.HBM(a.shape, a.dtype) for a in list(shards) + lands]
    res = pl.pallas_call(
        body, name=name, out_shape=(*sems, *thru, _token_shape()),
        in_specs=[HBM_SPEC] * (2 * nw),
        out_specs=(*([SEM_SPEC] * (4 * nw)), *([HBM_SPEC] * (2 * nw)), pl.BlockSpec(memory_space=pltpu.VMEM)),
        input_output_aliases={i: 4 * nw + i for i in range(2 * nw)},
        compiler_params=pltpu.CompilerParams(has_side_effects=EFFECT),
    )(*[_in_hbm(s) for s in shards], *[_in_hbm(l) for l in lands])
    return [dict(send=res[k], from_sib=res[nw + k], from_chips=res[2 * nw + k], own=res[3 * nw + k],
                 src=res[4 * nw + k], land=res[5 * nw + k]) for k in range(nw)], res[6 * nw]


def _gather_forward(pending, after, name):
    rows = pending["src"].shape[0]
    n_fw = len(OTHER_CHIPS)

    def body(land_ref, from_chips, after_ref, fw_send, fw_recv, land_thru):
        x, y, c = _my_place()
        sibling, _ = _peer(x, y, c, SIBLING)
        for j, d in enumerate(OTHER_CHIPS):
            _, num = _peer(x, y, c, d)
            block = _rows_of(land_ref, num, rows)
            pltpu.make_async_remote_copy(
                src_ref=block, dst_ref=block, send_sem=fw_send.at[j], recv_sem=from_chips.at[j],
                device_id=sibling, device_id_type=MESH).wait_recv()
            pltpu.make_async_remote_copy(
                src_ref=block, dst_ref=block, send_sem=fw_send.at[j], recv_sem=fw_recv.at[j],
                device_id=sibling, device_id_type=MESH).start()

    land = pending["land"]
    dma = pltpu.SemaphoreType.DMA
    fw_send, fw_recv, land = pl.pallas_call(
        body, name=name, out_shape=(dma((n_fw,)), dma((n_fw,)), pltpu.HBM(land.shape, land.dtype)),
        in_specs=(HBM_SPEC, SEM_SPEC, pl.BlockSpec(memory_space=pl.ANY)),
        out_specs=(SEM_SPEC, SEM_SPEC, HBM_SPEC), input_output_aliases={0: 2},
        compiler_params=pltpu.CompilerParams(has_side_effects=EFFECT),
    )(land, pending["from_chips"], after)
    return dict(pending, land=land, fw_send=fw_send, fw_recv=fw_recv)


def _gather_wait(pending, name):
    rows = pending["src"].shape[0]

    def body(src_ref, land_ref, send, from_sib, own, fw_send, fw_recv, src_dead, got):
        x, y, c = _my_place()
        me = 4 * x + 2 * y + c
        sibling, sib_num = _peer(x, y, c, SIBLING)
        mine = _rows_of(land_ref, me, rows)
        pltpu.make_async_copy(src_ref, mine, own).wait()
        for i in range(1 + len(OTHER_CHIPS)):
            pltpu.make_async_remote_copy(
                src_ref=src_ref, dst_ref=mine, send_sem=send.at[i], recv_sem=from_sib,
                device_id=sibling, device_id_type=MESH).wait_send()
        theirs = _rows_of(land_ref, sib_num, rows)
        pltpu.make_async_remote_copy(
            src_ref=src_ref, dst_ref=theirs, send_sem=send.at[0], recv_sem=from_sib,
            device_id=sibling, device_id_type=MESH).wait_recv()
        for j, d in enumerate(OTHER_CHIPS):
            _, num = _peer(x, y, c, d)
            sent = _rows_of(land_ref, num, rows)
            _, got_num = _peer(x, y, c, d | SIBLING)
            arrived = _rows_of(land_ref, got_num, rows)
            cp = pltpu.make_async_remote_copy(
                src_ref=sent, dst_ref=arrived, send_sem=fw_send.at[j], recv_sem=fw_recv.at[j],
                device_id=sibling, device_id_type=MESH)
            cp.wait_send()
            cp.wait_recv()

    src, land = pending["src"], pending["land"]
    return pl.pallas_call(
        body, name=name, out_shape=(pltpu.HBM(src.shape, src.dtype), pltpu.HBM(land.shape, land.dtype)),
        in_specs=(HBM_SPEC, HBM_SPEC) + (SEM_SPEC,) * 5,
        out_specs=(HBM_SPEC, HBM_SPEC), input_output_aliases={0: 0, 1: 1},
        compiler_params=pltpu.CompilerParams(has_side_effects=EFFECT),
    )(src, land, pending["send"], pending["from_sib"], pending["own"], pending["fw_send"], pending["fw_recv"])[1]


def _scatter_start(full, name):
    rows, cols = full.shape[0] // N_DEV, full.shape[1]
    land = lax.empty((N_DEV, rows, cols), full.dtype)

    def body(full_ref, land_ref, send, recv, own, full_thru, land_thru, token):
        x, y, c = _my_place()
        me = 4 * x + 2 * y + c
        slab = land_ref.at[me]
        pltpu.make_async_copy(full_ref.at[pl.ds(pl.multiple_of(me * rows, 16), rows), :], slab, own).start()
        for d in range(1, N_DEV):
            place, num = _peer(x, y, c, d)
            pltpu.make_async_remote_copy(
                src_ref=full_ref.at[pl.ds(pl.multiple_of(num * rows, 16), rows), :], dst_ref=slab,
                send_sem=send.at[d - 1], recv_sem=recv.at[d - 1], device_id=place, device_id_type=MESH).start()
        token[...] = jnp.zeros_like(token)

    res = pl.pallas_call(
        body, name=name,
        out_shape=(pltpu.SemaphoreType.DMA((N_DEV - 1,)), pltpu.SemaphoreType.DMA((N_DEV - 1,)),
                   pltpu.SemaphoreType.DMA(()),
                   pltpu.HBM(full.shape, full.dtype), pltpu.HBM(land.shape, land.dtype), _token_shape()),
        in_specs=(HBM_SPEC, HBM_SPEC),
        out_specs=(SEM_SPEC, SEM_SPEC, SEM_SPEC, HBM_SPEC, HBM_SPEC, pl.BlockSpec(memory_space=pltpu.VMEM)),
        input_output_aliases={0: 3, 1: 4},
        compiler_params=pltpu.CompilerParams(has_side_effects=EFFECT),
    )(_in_hbm(full), _in_hbm(land))
    return dict(send=res[0], recv=res[1], own=res[2], src=res[3], land=res[4]), res[5]


def _scatter_wait(pending, after, name):
    rows = pending["land"].shape[1]

    def body(src_ref, land_ref, send, recv, own, after_ref, src_dead, got):
        x, y, c = _my_place()
        me = 4 * x + 2 * y + c
        pltpu.make_async_copy(src_ref.at[pl.ds(pl.multiple_of(me * rows, 16), rows), :], land_ref.at[me], own).wait()
        for d in range(1, N_DEV):
            place, num = _peer(x, y, c, d)
            cp = pltpu.make_async_remote_copy(
                src_ref=src_ref.at[pl.ds(pl.multiple_of(num * rows, 16), rows), :], dst_ref=land_ref.at[me],
                send_sem=send.at[d - 1], recv_sem=recv.at[d - 1], device_id=place, device_id_type=MESH)
            cp.wait_send()
            cp.wait_recv()

    src, land = pending["src"], pending["land"]
    return pl.pallas_call(
        body, name=name, out_shape=(pltpu.HBM(src.shape, src.dtype), pltpu.HBM(land.shape, land.dtype)),
        in_specs=(HBM_SPEC, HBM_SPEC, SEM_SPEC, SEM_SPEC, SEM_SPEC, pl.BlockSpec(memory_space=pl.ANY)),
        out_specs=(HBM_SPEC, HBM_SPEC), input_output_aliases={0: 0, 1: 1},
        compiler_params=pltpu.CompilerParams(has_side_effects=EFFECT),
    )(src, land, pending["send"], pending["recv"], pending["own"], after)[1]


BIG = ("ffn1_w_in", "ffn1_w_out", "mix_w_in", "w_branch_a", "w_branch_b", "mix_w_out", "ffn2_w_in", "ffn2_w_out")
TRANSPOSED = ("ffn1_w_in", "mix_w_in", "w_branch_b", "ffn2_w_in")
SMALL = ("ln1_g", "ln1_b", "ln2_g", "ln2_b", "ln3_g", "ln3_b", "hgrn_norm_g", "hgrn_lb_fwd", "hgrn_lb_bwd")
SMALL_ROWS = 16


def _local_step(x, target, weight, emit, sp):
    b_, s_, d_ = x.shape
    t_ = b_ * s_
    x2, tgt = x.reshape(t_, d_), target.reshape(t_, d_)
    xb = x2.astype(BF16)
    w1i = weight("ffn1_w_in", xb)
    g1, u1, a1 = _ffn_in_fwd(xb, w1i, "ffn1_in")
    w1o = weight("ffn1_w_out", a1)
    r1, h1, h1b = _mm_res_ln_fwd(a1, w1o, x2, sp["ln1_g"], sp["ln1_b"], 0.5, "ffn1_out_ln1")
    wmx = weight("mix_w_in", h1b)
    proj = _mm_nt(h1b, wmx, F32, "mix_in")
    ya_in, o_sum = _hgrn_fwd(proj, sp["hgrn_lb_fwd"], sp["hgrn_lb_bwd"], sp["hgrn_norm_g"], b_, s_, d_, "hgrn_fwd")
    tabs = _rope_tables(s_)
    qkvs, gtabs, outs, lses = [], [], [], []
    for gi, (_, dil) in enumerate(ATTN_GROUPS):
        off = 5 * d_ + gi * QKV_GROUP
        qkv = proj[:, off:off + QKV_GROUP].reshape(t_ // dil, dil * QKV_GROUP)
        gt = [tb.reshape(s_ // dil, dil * HEAD) for tb in tabs]
        o_g, lse_g = _attn_fwd(qkv, gt, b_, s_, dil, f"attn_fwd_{gi}")
        qkvs.append(qkv)
        gtabs.append(gt)
        outs.append(o_g.reshape(t_, ATTN_OUT))
        lses.append(lse_g.reshape(t_, ATTN_OUT))
    ob = _combine_fwd(outs, lses, "attn_combine")
    wa, wb = weight("w_branch_a", ya_in), weight("w_branch_b", ob)
    ya, yb, z = _gate_out_fwd(ya_in, ob, wa, wb, proj, d_, "branch_gate")
    wo = weight("mix_w_out", z)
    r2, h2, h2b = _mm_res_ln_fwd(z, wo, h1, sp["ln2_g"], sp["ln2_b"], 1.0, "mix_out_ln2")
    w2i = weight("ffn2_w_in", h2b)
    g2, u2, a2 = _ffn_in_fwd(h2b, w2i, "ffn2_in")
    w2o = weight("ffn2_w_out", a2)
    r3, _, _ = _mm_res_ln_fwd(a2, w2o, h2, sp["ln3_g"], sp["ln3_b"], 0.5, "ffn2_out_ln3")
    dr3, dr3b, dg3, db3, loss = _ln_loss_bwd(r3, tgt, sp["ln3_g"], sp["ln3_b"], "loss_ln3_bwd")
    dep = emit("ffn2_w_out", _mm_tn(a2, dr3b, 0.5, "d_ffn2_w_out"))
    dgate2, dup2 = _ffn_mid_bwd(dr3b, w2o, g2, u2, 0.5, "ffn2_mid_bwd", dep)
    du2 = (dgate2, dup2)
    dep = emit("ffn2_w_in", _mm_tn(du2, h2b, 1.0, "d_ffn2_w_in"))
    dr2, dr2b, dg2, db2 = _mm_nn_res_lnbwd(du2, w2i, dr3, r2, sp["ln2_g"], "ffn2_in_bwd_ln2", dep)
    dep = emit("mix_w_out", _mm_tn(z, dr2b, 1.0, "d_mix_w_out"))
    dya, dyb, dga, dgb = _dz_gate_bwd(dr2b, wo, proj, ya, yb, d_, "branch_gate_bwd", dep)
    dep = emit("w_branch_a", _mm_tn(ya_in, dya, 1.0, "d_w_branch_a"))
    dya_in = _mm_nt(dya, wa, F32, "branch_a_bwd", dep)
    dep = emit("w_branch_b", _mm_tn(dyb, ob, 1.0, "d_w_branch_b"))
    dob = _mm_nn(dyb, wb, F32, "branch_b_bwd", dep)
    dhq, dhff, dhfb, dhi, dhog, dng, dlbf, dlbb = _hgrn_bwd(
        proj, sp["hgrn_lb_fwd"], sp["hgrn_lb_bwd"], sp["hgrn_norm_g"], o_sum, dya_in, b_, s_, d_, "hgrn_bwd")
    dogs, cgs = _combine_bwd(dob, outs, lses, "attn_combine_bwd")
    dqkv = []
    for gi, (_, dil) in enumerate(ATTN_GROUPS):
        seg_rows = t_ // dil
        dq, dk, dv = _attn_bwd(qkvs[gi], gtabs[gi], dogs[gi].reshape(seg_rows, dil * ATTN_OUT),
                               cgs[gi].reshape(seg_rows, dil * ATTN_OUT), lses[gi].reshape(seg_rows, dil * ATTN_OUT),
                               b_, s_, dil, f"attn_bwd_{gi}")
        dqkv += [t.reshape(t_, ATTN_OUT) for t in (dq, dk, dv)]
    dproj = jnp.concatenate([dhq, dhff, dhfb, dhi, dhog] + dqkv + [dga, dgb], axis=1)
    dep = emit("mix_w_in", _mm_tn(dproj, h1b, 1.0, "d_mix_w_in"))
    dr1, dr1b, dg1, db1 = _mm_nn_res_lnbwd(dproj, wmx, dr2, r1, sp["ln1_g"], "mix_in_bwd_ln1", dep)
    dep = emit("ffn1_w_out", _mm_tn(a1, dr1b, 0.5, "d_ffn1_w_out"))
    dgate1, dup1 = _ffn_mid_bwd(dr1b, w1o, g1, u1, 0.5, "ffn1_mid_bwd", dep)
    du1 = (dgate1, dup1)
    dep = emit("ffn1_w_in", _mm_tn(du1, xb, 1.0, "d_ffn1_w_in"))
    grad_x = _mm_nn_res(du1, w1i, dr1, "ffn1_in_bwd", dep)
    small = {"ln1_g": dg1, "ln1_b": db1, "ln2_g": dg2, "ln2_b": db2, "ln3_g": dg3, "ln3_b": db3,
             "hgrn_norm_g": dng, "hgrn_lb_fwd": dlbf, "hgrn_lb_bwd": dlbb}
    return loss, grad_x.reshape(b_, s_, d_), small


def _pack_small(vals):
    rows = jnp.concatenate([vals[n] for n in SMALL], axis=0)
    return jnp.pad(rows, ((0, SMALL_ROWS - rows.shape[0]), (0, 0)))


def _unpack_small(packed):
    out, r = {}, 0
    for n in SMALL:
        k = 2 if n.startswith("hgrn_lb") else 1
        out[n] = packed[r:r + k]
        r += k
    return out


def kernel(x, ffn1_w_in, ffn1_w_out, ln1_g, ln1_b, mix_w_in, hgrn_lb_fwd, hgrn_lb_bwd, hgrn_norm_g, w_branch_a, w_branch_b, mix_w_out, ln2_g, ln2_b, ffn2_w_in, ffn2_w_out, ln3_g, ln3_b, loss_target, m_ffn1_w_in, m_ffn1_w_out, m_ln1_g, m_ln1_b, m_mix_w_in, m_hgrn_lb_fwd, m_hgrn_lb_bwd, m_hgrn_norm_g, m_w_branch_a, m_w_branch_b, m_mix_w_out, m_ln2_g, m_ln2_b, m_ffn2_w_in, m_ffn2_w_out, m_ln3_g, m_ln3_b, v_ffn1_w_in, v_ffn1_w_out, v_ln1_g, v_ln1_b, v_mix_w_in, v_hgrn_lb_fwd, v_hgrn_lb_bwd, v_hgrn_norm_g, v_w_branch_a, v_w_branch_b, v_mix_w_out, v_ln2_g, v_ln2_b, v_ffn2_w_in, v_ffn2_w_out, v_ln3_g, v_ln3_b):
    args = dict(locals())
    big_w = {n: args[n][0] for n in BIG}
    sp = {n: args[n] for n in SMALL}
    def rows_bf16(n, zero=0.0):
        w = big_w[n] + zero
        return (w.T if n in TRANSPOSED else w).astype(BF16)

    first, rest = BIG[:2], BIG[2:]
    pending, token = _gather_start([rows_bf16(n) for n in first], "gather_start_ffn1")
    gathering = dict(zip(first, pending))
    pending, all_started = _gather_start([rows_bf16(n, token[0, 0]) for n in rest], "gather_start_rest")
    gathering.update(zip(rest, pending))
    scattering = {}

    def weight(n, after):
        if n == first[0]:
            after = all_started
        return _gather_wait(_gather_forward(gathering[n], after, f"gather_forward_{n}"), f"gather_wait_{n}")

    def emit(n, grad):
        scattering[n], token = _scatter_start(grad, f"scatter_start_{n}")
        return token

    loss_part, grad_x, small = _local_step(x, loss_target, weight, emit, sp)
    loss = lax.psum(loss_part[0, 0], ("x", "y", "c"))
    out_g, out_d, out_m, out_v = {}, {}, {}, {}
    for n in BIG:
        g = _sum_partials(_scatter_wait(scattering[n], grad_x, f"scatter_wait_{n}"), f"sum_{n}")
        if n in TRANSPOSED:
            g = g.T
        d_w, m_w, v_w = _adam(big_w[n], g, args["m_" + n][0], args["v_" + n][0], f"adam_{n}")
        out_g[n], out_d[n], out_m[n], out_v[n] = g[None], d_w[None], m_w[None], v_w[None]
    (parts,) = _all_gather([_pack_small(small)], "gather_small_grads")
    res = _small_sum_adam(parts.reshape(N_DEV, SMALL_ROWS, parts.shape[1]), _pack_small(sp),
                          _pack_small({n: args["m_" + n] for n in SMALL}),
                          _pack_small({n: args["v_" + n] for n in SMALL}), "small_adam")
    sg, sd, sm, sv = (_unpack_small(r) for r in res)
    out_g.update(sg), out_d.update(sd), out_m.update(sm), out_v.update(sv)
    order = ("ffn1_w_in", "ffn1_w_out", "ln1_g", "ln1_b", "mix_w_in", "hgrn_lb_fwd", "hgrn_lb_bwd", "hgrn_norm_g",
             "w_branch_a", "w_branch_b", "mix_w_out", "ln2_g", "ln2_b", "ffn2_w_in", "ffn2_w_out", "ln3_g", "ln3_b")
    return (loss, grad_x, *[out_g[n] for n in order], *[out_d[n] for n in order],
            *[out_m[n] for n in order], *[out_v[n] for n in order])
```

```python
import functools

import jax
import jax.numpy as jnp
from jax import lax
from jax.experimental import pallas as pl
from jax.experimental.pallas import tpu as pltpu

F32 = jnp.float32
BF16 = jnp.bfloat16

N_DEV = 8
HEAD = 128
CHUNK = 32
ATTN_GROUPS = ((128, 1), (512, 4), (2048, 16))
ATTN_HEADS = 4
ATTN_HALF = 64
QKV_GROUP = 3 * ATTN_HEADS * HEAD
QKV_WIDTH = len(ATTN_GROUPS) * QKV_GROUP
ATTN_OUT = ATTN_HEADS * HEAD
ROPE_THETA = 500000.0
ROPE_DIM = HEAD // 4
ALPHA = 2.0 ** 0.25
LN_EPS = 1e-5
NEG_INF = -1e30
ADAM_LR, ADAM_B1, ADAM_B2, ADAM_EPS, ADAM_WD, ADAM_STEP = 0.001, 0.9, 0.999, 1e-08, 0.01, 10
VMEM_LIMIT = 56 * 1024 * 1024

NT = (((1,), (1,)), ((), ()))
NN = (((1,), (0,)), ((), ()))
TN = (((0,), (0,)), ((), ()))
MESH = pl.DeviceIdType.MESH


def _dot(a, b, dims):
    return lax.dot_general(a, b, dims, preferred_element_type=F32)


def _tile(n, pref, mult=128):
    if n <= pref:
        return n
    t = (pref // mult) * mult
    while t >= mult:
        if n % t == 0:
            return t
        t -= mult
    return n


def _tile_multi(ns, pref, mult=128):
    t = (pref // mult) * mult
    while t >= mult:
        if all(n % t == 0 for n in ns):
            return t
        t -= mult
    raise ValueError(f"no common tile for {ns}")


def _params(**kw):
    return pltpu.CompilerParams(vmem_limit_bytes=VMEM_LIMIT, **kw)


def _after(body, n_in, dep):
    if dep is None:
        return body, [], []

    def wrapped(*refs):
        body(*refs[:n_in], *refs[n_in + 1:])

    return wrapped, [pl.BlockSpec(dep.shape, lambda *_: (0,) * dep.ndim)], [dep]


def _pieces(a):
    pieces = tuple(a) if isinstance(a, (tuple, list)) else (a,)
    assert all(p.shape == pieces[0].shape for p in pieces)
    return pieces, pieces[0].shape[0], pieces[0].shape[1], len(pieces)


def _for_piece(step, p, per, npc, fn):
    if npc == 1:
        fn()
    else:
        pl.when((step >= p * per) & (step < (p + 1) * per))(fn)


def _sigmoid(x):
    return jax.nn.sigmoid(x)


def _dsilu(x, s):
    return s * (1.0 + x * (1.0 - s))


def _ln_stats(r):
    mu = jnp.mean(r, axis=-1, keepdims=True)
    xc = r - mu
    var = jnp.mean(xc * xc, axis=-1, keepdims=True)
    rstd = lax.rsqrt(var + LN_EPS)
    return xc * rstd, rstd


def _ln_bwd(dy, xhat, rstd, g):
    dyg = dy * g
    m1 = jnp.mean(dyg, axis=-1, keepdims=True)
    m2 = jnp.mean(dyg * xhat, axis=-1, keepdims=True)
    return rstd * (dyg - m1 - xhat * m2)


ROW_TILE = 1024
SUB_ROWS = 256


def _once(shape, index_map):
    return pl.BlockSpec(shape, index_map, pipeline_mode=pl.Buffered(1))


def _for_row_blocks(tm, fn):
    sub = SUB_ROWS if tm % SUB_ROWS == 0 else tm

    def step(s, carry):
        fn(pl.ds(pl.multiple_of(s * sub, sub), sub))
        return carry

    lax.fori_loop(0, tm // sub, step, 0)


def _ffn_in_fwd(xb, w_t, name):
    t_, d_ = xb.shape
    f_ = w_t.shape[0] // 2
    tm, tn = _tile(t_, ROW_TILE, 8), _tile(f_, 512)
    nj = f_ // tn

    def body(x_ref, wg_ref, wu_ref, g_ref, u_ref, a_ref):
        x = x_ref[...]
        g = _dot(x, wg_ref[...], NT)
        u = _dot(x, wu_ref[...], NT)
        g_ref[...] = g.astype(BF16)
        u_ref[...] = u.astype(BF16)
        a_ref[...] = (g * _sigmoid(g) * u).astype(BF16)

    return pl.pallas_call(
        body, name=name, grid=(t_ // tm, nj),
        in_specs=[pl.BlockSpec((tm, d_), lambda i, j: (i, 0)),
                  pl.BlockSpec((tn, d_), lambda i, j: (j, 0)),
                  pl.BlockSpec((tn, d_), lambda i, j: (j + nj, 0))],
        out_specs=[pl.BlockSpec((tm, tn), lambda i, j: (i, j))] * 3,
        out_shape=[jax.ShapeDtypeStruct((t_, f_), BF16)] * 3,
        compiler_params=_params(),
    )(xb, w_t, w_t)


def _mm_res_ln_fwd(a, w, res, g, b, scale, name):
    t_, k_ = a.shape
    d_ = w.shape[1]
    tm, tk = _tile(t_, ROW_TILE, 8), _tile(k_, 512)
    nk = k_ // tk

    def body(a_ref, w_ref, res_ref, g_ref, b_ref, r_ref, h_ref, hb_ref, acc):
        k = pl.program_id(1)

        @pl.when(k == 0)
        def _():
            acc[...] = jnp.zeros_like(acc)

        acc[...] += _dot(a_ref[...], w_ref[...], NN)

        @pl.when(k == nk - 1)
        def _():
            def rows_out(rows):
                r = ALPHA * res_ref[rows, :] + scale * acc[rows, :]
                xhat, _ = _ln_stats(r)
                h = xhat * g_ref[...] + b_ref[...]
                r_ref[rows, :] = r
                h_ref[rows, :] = h
                hb_ref[rows, :] = h.astype(BF16)

            _for_row_blocks(tm, rows_out)

    row = _once((tm, d_), lambda i, k: (i, 0))
    vec = pl.BlockSpec((1, d_), lambda i, k: (0, 0))
    return pl.pallas_call(
        body, name=name, grid=(t_ // tm, nk),
        in_specs=[pl.BlockSpec((tm, tk), lambda i, k: (i, k)),
                  pl.BlockSpec((tk, d_), lambda i, k: (k, 0)), row, vec, vec],
        out_specs=[row, row, row],
        out_shape=[jax.ShapeDtypeStruct((t_, d_), F32), jax.ShapeDtypeStruct((t_, d_), F32),
                   jax.ShapeDtypeStruct((t_, d_), BF16)],
        scratch_shapes=[pltpu.VMEM((tm, d_), F32)],
        compiler_params=_params(),
    )(a, w, res, g, b)


def _mm_nt(a, w_t, out_dtype, name, dep=None):
    t_, k_ = a.shape
    n_ = w_t.shape[0]
    tm, tn = _tile(t_, ROW_TILE, 8), _tile(n_, 512)

    def body(a_ref, w_ref, o_ref):
        o_ref[...] = _dot(a_ref[...], w_ref[...], NT).astype(out_dtype)

    body, dep_specs, deps = _after(body, 2, dep)
    return pl.pallas_call(
        body, name=name, grid=(t_ // tm, n_ // tn),
        in_specs=[pl.BlockSpec((tm, k_), lambda i, j: (i, 0)),
                  pl.BlockSpec((tn, k_), lambda i, j: (j, 0)), *dep_specs],
        out_specs=pl.BlockSpec((tm, tn), lambda i, j: (i, j)),
        out_shape=jax.ShapeDtypeStruct((t_, n_), out_dtype),
        compiler_params=_params(),
    )(a, w_t, *deps)


def _mm_nn(a, w, out_dtype, name, dep=None):
    t_, k_ = a.shape
    n_ = w.shape[1]
    tm, tn = _tile(t_, ROW_TILE, 8), _tile(n_, 512)

    def body(a_ref, w_ref, o_ref):
        o_ref[...] = _dot(a_ref[...], w_ref[...], NN).astype(out_dtype)

    body, dep_specs, deps = _after(body, 2, dep)
    return pl.pallas_call(
        body, name=name, grid=(t_ // tm, n_ // tn),
        in_specs=[pl.BlockSpec((tm, k_), lambda i, j: (i, 0)),
                  pl.BlockSpec((k_, tn), lambda i, j: (0, j)), *dep_specs],
        out_specs=pl.BlockSpec((tm, tn), lambda i, j: (i, j)),
        out_shape=jax.ShapeDtypeStruct((t_, n_), out_dtype),
        compiler_params=_params(),
    )(a, w, *deps)


def _mm_tn(a, b, scale, name):
    pieces, t_, mp, npc = _pieces(a)
    n_ = b.shape[1]
    tm = _tile(mp, 512)
    per = mp // tm

    def body(*refs):
        b_ref, o_ref = refs[npc], refs[npc + 1]
        for p in range(npc):
            def piece_out(p=p):
                o_ref[...] = (scale * _dot(refs[p][...], b_ref[...], TN)).astype(BF16)

            _for_piece(pl.program_id(0), p, per, npc, piece_out)

    return pl.pallas_call(
        body, name=name, grid=(npc * per,),
        in_specs=[pl.BlockSpec((t_, tm), lambda i, p=p: (0, jnp.clip(i - p * per, 0, per - 1))) for p in range(npc)]
        + [_once((t_, n_), lambda i: (0, 0))],
        out_specs=pl.BlockSpec((tm, n_), lambda i: (i, 0)),
        out_shape=jax.ShapeDtypeStruct((npc * mp, n_), BF16),
        compiler_params=_params(),
    )(*pieces, b)


def _gate_out_fwd(ya_in, ob, wa, wb_t, proj, d_, name):
    t_ = ya_in.shape[0]
    goff = 5 * d_ + QKV_WIDTH
    tm, tn = _tile(t_, ROW_TILE, 8), _tile_multi([d_, goff], 512)
    ja, jb = goff // tn, (goff + d_) // tn

    def body(ya_ref, ob_ref, wa_ref, wb_ref, ga_ref, gb_ref, yao_ref, ybo_ref, z_ref):
        y_a = _dot(ya_ref[...], wa_ref[...], NN)
        y_b = _dot(ob_ref[...], wb_ref[...], NT)
        yao_ref[...] = y_a.astype(BF16)
        ybo_ref[...] = y_b.astype(BF16)
        z_ref[...] = (_sigmoid(ga_ref[...]) * y_a + _sigmoid(gb_ref[...]) * y_b).astype(BF16)

    tile = pl.BlockSpec((tm, tn), lambda i, j: (i, j))
    return pl.pallas_call(
        body, name=name, grid=(t_ // tm, d_ // tn),
        in_specs=[pl.BlockSpec((tm, d_), lambda i, j: (i, 0)),
                  pl.BlockSpec((tm, ATTN_OUT), lambda i, j: (i, 0)),
                  pl.BlockSpec((d_, tn), lambda i, j: (0, j)),
                  pl.BlockSpec((tn, ATTN_OUT), lambda i, j: (j, 0)),
                  pl.BlockSpec((tm, tn), lambda i, j: (i, ja + j)),
                  pl.BlockSpec((tm, tn), lambda i, j: (i, jb + j))],
        out_specs=[tile, tile, tile],
        out_shape=[jax.ShapeDtypeStruct((t_, d_), BF16)] * 3,
        compiler_params=_params(),
    )(ya_in, ob, wa, wb_t, proj, proj)


def _ffn_mid_bwd(drb, w_out, gate, up, scale, name, dep=None):
    t_, d_ = drb.shape
    f_ = w_out.shape[0]
    tm, tn = _tile(t_, ROW_TILE, 8), _tile(f_, 512)

    def body(dr_ref, w_ref, g_ref, u_ref, dg_ref, du_ref):
        da = scale * _dot(dr_ref[...], w_ref[...], NT)
        g = g_ref[...].astype(F32)
        s = _sigmoid(g)
        dg_ref[...] = (da * u_ref[...].astype(F32) * _dsilu(g, s)).astype(BF16)
        du_ref[...] = (da * g * s).astype(BF16)

    body, dep_specs, deps = _after(body, 4, dep)
    tile = pl.BlockSpec((tm, tn), lambda i, j: (i, j))
    return pl.pallas_call(
        body, name=name, grid=(t_ // tm, f_ // tn),
        in_specs=[pl.BlockSpec((tm, d_), lambda i, j: (i, 0)),
                  pl.BlockSpec((tn, d_), lambda i, j: (j, 0)), tile, tile, *dep_specs],
        out_specs=[tile, tile],
        out_shape=[jax.ShapeDtypeStruct((t_, f_), BF16)] * 2,
        compiler_params=_params(),
    )(drb, w_out, gate, up, *deps)


def _mm_nn_res_lnbwd(a, w, dres, r, g, name, dep=None):
    pieces, t_, kp, npc = _pieces(a)
    d_ = w.shape[1]
    tm, tk = _tile(t_, ROW_TILE, 8), _tile(kp, 512)
    per = kp // tk
    nk = npc * per

    def body(*refs):
        w_ref, dres_ref, r_ref, g_ref, dr_ref, drb_ref, dg_ref, db_ref, acc = refs[npc:]
        i, k = pl.program_id(0), pl.program_id(1)

        @pl.when(k == 0)
        def _():
            acc[...] = jnp.zeros_like(acc)

        @pl.when((i == 0) & (k == 0))
        def _():
            dg_ref[...] = jnp.zeros_like(dg_ref)
            db_ref[...] = jnp.zeros_like(db_ref)

        for p in range(npc):
            def piece_in(p=p):
                acc[...] += _dot(refs[p][...], w_ref[...], NN)

            _for_piece(k, p, per, npc, piece_in)

        @pl.when(k == nk - 1)
        def _():
            def rows_out(rows):
                dy = acc[rows, :] + ALPHA * dres_ref[rows, :]
                xhat, rstd = _ln_stats(r_ref[rows, :])
                dr = _ln_bwd(dy, xhat, rstd, g_ref[...])
                dr_ref[rows, :] = dr
                drb_ref[rows, :] = dr.astype(BF16)
                dg_ref[...] += jnp.sum(dy * xhat, axis=0, keepdims=True)
                db_ref[...] += jnp.sum(dy, axis=0, keepdims=True)

            _for_row_blocks(tm, rows_out)

    body, dep_specs, deps = _after(body, npc + 4, dep)
    row = _once((tm, d_), lambda i, k: (i, 0))
    vec = pl.BlockSpec((1, d_), lambda i, k: (0, 0))
    return pl.pallas_call(
        body, name=name, grid=(t_ // tm, nk),
        in_specs=[pl.BlockSpec((tm, tk), lambda i, k, p=p: (i, jnp.clip(k - p * per, 0, per - 1))) for p in range(npc)]
        + [pl.BlockSpec((tk, d_), lambda i, k: (k, 0)), row, row, vec, *dep_specs],
        out_specs=[row, row, vec, vec],
        out_shape=[jax.ShapeDtypeStruct((t_, d_), F32), jax.ShapeDtypeStruct((t_, d_), BF16),
                   jax.ShapeDtypeStruct((1, d_), F32), jax.ShapeDtypeStruct((1, d_), F32)],
        scratch_shapes=[pltpu.VMEM((tm, d_), F32)],
        compiler_params=_params(),
    )(*pieces, w, dres, r, g, *deps)


def _mm_nn_res(a, w, dres, name, dep=None):
    pieces, t_, kp, npc = _pieces(a)
    d_ = w.shape[1]
    tm, tk = _tile(t_, ROW_TILE, 8), _tile(kp, 512)
    per = kp // tk
    nk = npc * per

    def body(*refs):
        w_ref, dres_ref, o_ref, acc = refs[npc:]
        k = pl.program_id(1)

        @pl.when(k == 0)
        def _():
            acc[...] = jnp.zeros_like(acc)

        for p in range(npc):
            def piece_in(p=p):
                acc[...] += _dot(refs[p][...], w_ref[...], NN)

            _for_piece(k, p, per, npc, piece_in)

        @pl.when(k == nk - 1)
        def _():
            def rows_out(rows):
                o_ref[rows, :] = acc[rows, :] + ALPHA * dres_ref[rows, :]

            _for_row_blocks(tm, rows_out)

    body, dep_specs, deps = _after(body, npc + 2, dep)
    row = _once((tm, d_), lambda i, k: (i, 0))
    return pl.pallas_call(
        body, name=name, grid=(t_ // tm, nk),
        in_specs=[pl.BlockSpec((tm, tk), lambda i, k, p=p: (i, jnp.clip(k - p * per, 0, per - 1))) for p in range(npc)]
        + [pl.BlockSpec((tk, d_), lambda i, k: (k, 0)), row, *dep_specs],
        out_specs=row,
        out_shape=jax.ShapeDtypeStruct((t_, d_), F32),
        scratch_shapes=[pltpu.VMEM((tm, d_), F32)],
        compiler_params=_params(),
    )(*pieces, w, dres, *deps)


def _dz_gate_bwd(drb, w_out, proj, ya, yb, d_, name, dep=None):
    t_ = drb.shape[0]
    goff = 5 * d_ + QKV_WIDTH
    tm, tn = _tile(t_, ROW_TILE, 8), _tile_multi([d_, goff], 512)
    ja, jb = goff // tn, (goff + d_) // tn

    def body(dr_ref, w_ref, ga_ref, gb_ref, ya_ref, yb_ref, dya_ref, dyb_ref, dga_ref, dgb_ref):
        dz = _dot(dr_ref[...], w_ref[...], NT)
        sa, sb = _sigmoid(ga_ref[...]), _sigmoid(gb_ref[...])
        dya_ref[...] = (dz * sa).astype(BF16)
        dyb_ref[...] = (dz * sb).astype(BF16)
        dga_ref[...] = (dz * ya_ref[...].astype(F32) * sa * (1.0 - sa)).astype(BF16)
        dgb_ref[...] = (dz * yb_ref[...].astype(F32) * sb * (1.0 - sb)).astype(BF16)

    body, dep_specs, deps = _after(body, 6, dep)
    tile = pl.BlockSpec((tm, tn), lambda i, j: (i, j))
    return pl.pallas_call(
        body, name=name, grid=(t_ // tm, d_ // tn),
        in_specs=[pl.BlockSpec((tm, d_), lambda i, j: (i, 0)),
                  pl.BlockSpec((tn, d_), lambda i, j: (j, 0)),
                  pl.BlockSpec((tm, tn), lambda i, j: (i, ja + j)),
                  pl.BlockSpec((tm, tn), lambda i, j: (i, jb + j)), tile, tile, *dep_specs],
        out_specs=[tile] * 4,
        out_shape=[jax.ShapeDtypeStruct((t_, d_), BF16)] * 4,
        compiler_params=_params(),
    )(drb, w_out, proj, proj, ya, yb, *deps)


def _ln_loss_bwd(r, target, g, b, name):
    t_, d_ = r.shape
    tm = _tile(t_, 256, 8)

    def body(r_ref, t_ref, g_ref, b_ref, dr_ref, drb_ref, dg_ref, db_ref, loss_ref):
        i = pl.program_id(0)

        @pl.when(i == 0)
        def _():
            dg_ref[...] = jnp.zeros_like(dg_ref)
            db_ref[...] = jnp.zeros_like(db_ref)
            loss_ref[...] = jnp.zeros_like(loss_ref)

        xhat, rstd = _ln_stats(r_ref[...])
        gain = g_ref[...]
        err = xhat * gain + b_ref[...] - t_ref[...]
        loss_ref[...] += (0.5 / d_) * jnp.sum(err * err)
        dy = err * (1.0 / d_)
        dr = _ln_bwd(dy, xhat, rstd, gain)
        dr_ref[...] = dr
        drb_ref[...] = dr.astype(BF16)
        dg_ref[...] += jnp.sum(dy * xhat, axis=0, keepdims=True)
        db_ref[...] += jnp.sum(dy, axis=0, keepdims=True)

    row = pl.BlockSpec((tm, d_), lambda i: (i, 0))
    vec = pl.BlockSpec((1, d_), lambda i: (0, 0))
    return pl.pallas_call(
        body, name=name, grid=(t_ // tm,),
        in_specs=[row, row, vec, vec],
        out_specs=[row, row, vec, vec, pl.BlockSpec((1, HEAD), lambda i: (0, 0))],
        out_shape=[jax.ShapeDtypeStruct((t_, d_), F32), jax.ShapeDtypeStruct((t_, d_), BF16),
                   jax.ShapeDtypeStruct((1, d_), F32), jax.ShapeDtypeStruct((1, d_), F32),
                   jax.ShapeDtypeStruct((1, HEAD), F32)],
        compiler_params=_params(),
    )(r, target, g, b)


def _chunk_scan(x, row, reverse, size):
    s = 1
    while s < CHUNK:
        if reverse:
            x = x + jnp.where(row < CHUNK - s, pltpu.roll(x, size - s, 0), 0.0)
        else:
            x = x + jnp.where(row >= s, pltpu.roll(x, s, 0), 0.0)
        s *= 2
    return x


def _lower_bound(tab):
    return _sigmoid(tab[0:1, :] - tab[1:2, :])


def _tri_mask(reverse):
    r = lax.broadcasted_iota(jnp.int32, (CHUNK, CHUNK), 0)
    c = lax.broadcasted_iota(jnp.int32, (CHUNK, CHUNK), 1)
    return (c >= r) if reverse else (r >= c)


def _hgrn_fwd(proj, lbf, lbb, ng, b_, s_, d_, name):
    h_ = d_ // HEAD
    nc = s_ // CHUNK

    def body(hq_ref, hff_ref, hfb_ref, hi_ref, hog_ref, lbf_ref, lbb_ref, ng_ref, ya_ref, o_ref,
             q_s, k_s, cum_s, o_s):
        row = lax.broadcasted_iota(jnp.int32, (s_, HEAD), 0) % CHUNK
        hq = hq_ref[...]
        q_s[...] = hq * _sigmoid(hq)
        o_s[...] = jnp.zeros_like(o_s)
        for reverse, hf_ref, lb_ref in ((False, hff_ref, lbf_ref), (True, hfb_ref, lbb_ref)):
            lb = _lower_bound(lb_ref[...])
            f = lb + (1.0 - lb) * _sigmoid(hf_ref[...])
            k_s[...] = 1.0 - f
            cum_s[...] = _chunk_scan(jnp.log(f), row, reverse, s_)
            mask = _tri_mask(reverse)

            def step(n, st, reverse=reverse, mask=mask):
                idx = (nc - 1 - n) if reverse else n
                sl = pl.ds(pl.multiple_of(idx * CHUNK, CHUNK), CHUNK)
                cm = cum_s[sl, :]
                tot = cm[0:1, :] if reverse else cm[CHUNK - 1:CHUNK, :]
                qc, kc = q_s[sl, :], k_s[sl, :]
                vb = hi_ref[sl, :].astype(BF16)
                qd = (qc * jnp.exp(cm)).astype(BF16)
                kd = (kc * jnp.exp(-cm)).astype(BF16)
                ke = (kc * jnp.exp(tot - cm)).astype(BF16)
                a = jnp.where(mask, _dot(qd, kd, NT), 0.0)
                o_s[sl, :] += _dot(a.astype(BF16), vb, NN) + _dot(qd, st.astype(BF16), NT)
                return st * jnp.exp(tot) + _dot(vb, ke, TN)

            lax.fori_loop(0, nc, step, jnp.zeros((HEAD, HEAD), F32))
        o = o_s[...]
        o_ref[...] = o
        nrm = o * lax.rsqrt(jnp.mean(o * o, axis=-1, keepdims=True) + LN_EPS)
        hog = hog_ref[...]
        ya_ref[...] = (nrm * ng_ref[...] * hog * _sigmoid(hog)).astype(BF16)

    def col(part):
        return pl.BlockSpec((s_, HEAD), lambda h, b, part=part: (b, part * h_ + h))

    tab = pl.BlockSpec((2, HEAD), lambda h, b: (0, h))
    out = pl.BlockSpec((s_, HEAD), lambda h, b: (b, h))
    return pl.pallas_call(
        body, name=name, grid=(h_, b_),
        in_specs=[col(0), col(1), col(2), col(3), col(4), tab, tab,
                  pl.BlockSpec((1, HEAD), lambda h, b: (0, h))],
        out_specs=[out, out],
        out_shape=[jax.ShapeDtypeStruct((b_ * s_, d_), BF16), jax.ShapeDtypeStruct((b_ * s_, d_), F32)],
        scratch_shapes=[pltpu.VMEM((s_, HEAD), F32)] * 4,
        compiler_params=_params(),
    )(proj, proj, proj, proj, proj, lbf, lbb, ng)


def _hgrn_bwd(proj, lbf, lbb, ng, o_sum, dya, b_, s_, d_, name):
    h_ = d_ // HEAD
    nc = s_ // CHUNK

    def body(hq_ref, hff_ref, hfb_ref, hi_ref, hog_ref, lbf_ref, lbb_ref, ng_ref, o_ref, dya_ref,
             dhq_ref, dhff_ref, dhfb_ref, dhi_ref, dhog_ref, dng_ref, dlbf_ref, dlbb_ref,
             q_s, k_s, cum_s, do_s, dq_s, dv_s, db_s, dk_s, st_s):
        b = pl.program_id(1)

        @pl.when(b == 0)
        def _():
            dng_ref[...] = jnp.zeros_like(dng_ref)
            dlbf_ref[...] = jnp.zeros_like(dlbf_ref)
            dlbb_ref[...] = jnp.zeros_like(dlbb_ref)

        row = lax.broadcasted_iota(jnp.int32, (s_, HEAD), 0) % CHUNK
        crow = lax.broadcasted_iota(jnp.int32, (CHUNK, HEAD), 0)
        hq = hq_ref[...]
        sq = _sigmoid(hq)
        q_s[...] = hq * sq
        o = o_ref[...]
        rinv = lax.rsqrt(jnp.mean(o * o, axis=-1, keepdims=True) + LN_EPS)
        nrm = o * rinv
        hog = hog_ref[...]
        so = _sigmoid(hog)
        gain = ng_ref[...]
        dy = dya_ref[...]
        dhog_ref[...] = (dy * nrm * gain * _dsilu(hog, so)).astype(BF16)
        dng_ref[...] += jnp.sum(dy * nrm * hog * so, axis=0, keepdims=True)
        dn = dy * gain * hog * so
        do_s[...] = rinv * (dn - nrm * jnp.mean(dn * nrm, axis=-1, keepdims=True))
        dq_s[...] = jnp.zeros_like(dq_s)
        dv_s[...] = jnp.zeros_like(dv_s)

        for reverse, hf_ref, lb_ref, dhf_ref, dlb_ref in (
                (False, hff_ref, lbf_ref, dhff_ref, dlbf_ref), (True, hfb_ref, lbb_ref, dhfb_ref, dlbb_ref)):
            tab = lb_ref[...]
            lb = _lower_bound(tab)
            sf = _sigmoid(hf_ref[...])
            f = lb + (1.0 - lb) * sf
            k_s[...] = 1.0 - f
            cum_s[...] = _chunk_scan(jnp.log(f), row, reverse, s_)
            mask = _tri_mask(reverse)
            last = 0 if reverse else CHUNK - 1

            def chunk(idx, reverse=reverse):
                sl = pl.ds(pl.multiple_of(idx * CHUNK, CHUNK), CHUNK)
                cm = cum_s[sl, :]
                tot = cm[0:1, :] if reverse else cm[CHUNK - 1:CHUNK, :]
                return sl, cm, tot

            def fstep(n, st, reverse=reverse, chunk=chunk):
                idx = (nc - 1 - n) if reverse else n
                sl, cm, tot = chunk(idx)
                st_s[idx] = st
                ke = (k_s[sl, :] * jnp.exp(tot - cm)).astype(BF16)
                return st * jnp.exp(tot) + _dot(hi_ref[sl, :].astype(BF16), ke, TN)

            lax.fori_loop(0, nc, fstep, jnp.zeros((HEAD, HEAD), F32))

            def bstep(n, dst, reverse=reverse, chunk=chunk, mask=mask, last=last):
                idx = n if reverse else (nc - 1 - n)
                sl, cm, tot = chunk(idx)
                eb, enb, ee, dec = jnp.exp(cm), jnp.exp(-cm), jnp.exp(tot - cm), jnp.exp(tot)
                qc, kc = q_s[sl, :], k_s[sl, :]
                qd, kd, ke = qc * eb, kc * enb, kc * ee
                qdb, kdb, keb = qd.astype(BF16), kd.astype(BF16), ke.astype(BF16)
                vb = hi_ref[sl, :].astype(BF16)
                dob = do_s[sl, :].astype(BF16)
                st0 = st_s[idx]
                dstb = dst.astype(BF16)
                a = jnp.where(mask, _dot(qdb, kdb, NT), 0.0).astype(BF16)
                da = jnp.where(mask, _dot(dob, vb, NT), 0.0).astype(BF16)
                dqd = _dot(da, kdb, NN) + _dot(dob, st0.astype(BF16), NN)
                dkd = _dot(da, qdb, TN)
                dv = _dot(a, dob, TN) + _dot(keb, dstb, NT)
                dke = _dot(vb, dstb, NN)
                ddec = jnp.sum(dst * st0, axis=0, keepdims=True)
                dtot = jnp.sum(dke * ke, axis=0, keepdims=True) + ddec * dec
                db = dqd * qd - dkd * kd - dke * ke
                db_s[sl, :] = db + jnp.where(crow == last, dtot, 0.0)
                dk_s[sl, :] = dkd * enb + dke * ee
                dq_s[sl, :] += dqd * eb
                dv_s[sl, :] += dv
                return dst * dec + _dot(dob, qdb, TN)

            lax.fori_loop(0, nc, bstep, jnp.zeros((HEAD, HEAD), F32))
            dlogf = _chunk_scan(db_s[...], row, not reverse, s_)
            df = dlogf / f - dk_s[...]
            dhf_ref[...] = (df * (1.0 - lb) * sf * (1.0 - sf)).astype(BF16)
            dlb = jnp.sum(df * (1.0 - sf), axis=0, keepdims=True) * lb * (1.0 - lb)
            dlb_ref[0:1, :] += dlb
            dlb_ref[1:2, :] -= dlb

        dhq_ref[...] = (dq_s[...] * _dsilu(hq, sq)).astype(BF16)
        dhi_ref[...] = dv_s[...].astype(BF16)

    def col(part):
        return pl.BlockSpec((s_, HEAD), lambda h, b, part=part: (b, part * h_ + h))

    tab = pl.BlockSpec((2, HEAD), lambda h, b: (0, h))
    vec = pl.BlockSpec((1, HEAD), lambda h, b: (0, h))
    blk = pl.BlockSpec((s_, HEAD), lambda h, b: (b, h))
    act = jax.ShapeDtypeStruct((b_ * s_, d_), BF16)
    return pl.pallas_call(
        body, name=name, grid=(h_, b_),
        in_specs=[col(0), col(1), col(2), col(3), col(4), tab, tab, vec, blk, blk],
        out_specs=[blk] * 5 + [vec, tab, tab],
        out_shape=[act] * 5 + [jax.ShapeDtypeStruct((1, d_), F32), jax.ShapeDtypeStruct((2, d_), F32),
                               jax.ShapeDtypeStruct((2, d_), F32)],
        scratch_shapes=[pltpu.VMEM((s_, HEAD), F32)] * 8 + [pltpu.VMEM((nc, HEAD, HEAD), F32)],
        compiler_params=_params(),
    )(proj, proj, proj, proj, proj, lbf, lbb, ng, o_sum, dya)


FWD_BLOCK = 128
BWD_BLOCK = 128


def _block_scan(x, row, reverse, size, blk):
    s = 1
    while s < blk:
        if reverse:
            x = x + jnp.where(row < blk - s, pltpu.roll(x, size - s, 0), 0.0)
        else:
            x = x + jnp.where(row >= s, pltpu.roll(x, s, 0), 0.0)
        s *= 2
    return x


def _block_exps(l, reverse):
    blk = l.shape[0]
    half = blk // 2
    first = lax.broadcasted_iota(jnp.int32, (blk, HEAD), 0) < half
    q1, q3 = half // 2, half + half // 2
    if reverse:
        rho1, rho2, lh, ltot = l[q1:q1 + 1], l[q3:q3 + 1], l[half:half + 1], l[0:1]
    else:
        rho1, rho2, lh, ltot = l[q1 - 1:q1], l[q3 - 1:q3], l[half - 1:half], l[blk - 1:blk]
    ref = jnp.where(first, rho1, rho2)
    query_half = first if reverse else jnp.logical_not(first)
    e2 = jnp.where(query_half, jnp.exp(jnp.minimum(l - lh, 0.0)), 0.0)
    e1 = jnp.where(query_half, 0.0, jnp.exp(jnp.minimum(lh - l, 0.0)))
    return (jnp.exp(l - ref), jnp.exp(ref - l), e2, e1, jnp.exp(l), jnp.exp(ltot - l),
            jnp.exp(ltot), jnp.exp(lh), jnp.exp(ltot - lh))


def _half_mask(reverse, blk):
    r = lax.broadcasted_iota(jnp.int32, (blk, blk), 0)
    c = lax.broadcasted_iota(jnp.int32, (blk, blk), 1)
    same = (r < blk // 2) == (c < blk // 2)
    return same & ((c >= r) if reverse else (r >= c))


def _hgrn_fwd(proj, lbf, lbb, ng, b_, s_, d_, name):
    h_ = d_ // HEAD
    BLOCK = min(FWD_BLOCK, s_)
    nb = s_ // BLOCK

    def body(hq_ref, hff_ref, hfb_ref, hi_ref, hog_ref, lbf_ref, lbb_ref, ng_ref, ya_ref, o_ref,
             q_s, k_s, l_s, of_s, oi_s, qd_s, u_s, st_s, dec_s):
        row = lax.broadcasted_iota(jnp.int32, (s_, HEAD), 0) % BLOCK
        hq = hq_ref[...]
        q_s[...] = hq * _sigmoid(hq)
        for reverse, hf_ref, lb_ref in ((False, hff_ref, lbf_ref), (True, hfb_ref, lbb_ref)):
            lb = _lower_bound(lb_ref[...])
            f = lb + (1.0 - lb) * _sigmoid(hf_ref[...])
            k_s[...] = 1.0 - f
            l_s[...] = _block_scan(jnp.log(f), row, reverse, s_, BLOCK)
            mask = _half_mask(reverse, BLOCK)

            def inside(n, carry, reverse=reverse, mask=mask):
                sl = pl.ds(pl.multiple_of(n * BLOCK, BLOCK), BLOCK)
                eq, ek, e2, e1, el, ee, dec, _, _ = _block_exps(l_s[sl, :], reverse)
                qc, kc = q_s[sl, :], k_s[sl, :]
                vb = hi_ref[sl, :].astype(BF16)
                a = jnp.where(mask, _dot((qc * eq).astype(BF16), (kc * ek).astype(BF16), NT), 0.0)
                a = a + _dot((qc * e2).astype(BF16), (kc * e1).astype(BF16), NT)
                oi_s[sl, :] = _dot(a.astype(BF16), vb, NN)
                qd_s[sl, :] = (qc * el).astype(BF16)
                u_s[n] = _dot(vb, (kc * ee).astype(BF16), TN)
                dec_s[n] = jnp.broadcast_to(dec, (8, HEAD))
                return carry

            lax.fori_loop(0, nb, inside, 0, unroll=4)

            def carry_state(n, st, reverse=reverse):
                idx = (nb - 1 - n) if reverse else n
                st_s[idx] = st.astype(BF16)
                return st * dec_s[idx][0:1, :] + u_s[idx]

            lax.fori_loop(0, nb, carry_state, jnp.zeros((HEAD, HEAD), F32))

            def across(n, carry, reverse=reverse):
                sl = pl.ds(pl.multiple_of(n * BLOCK, BLOCK), BLOCK)
                o_dir = oi_s[sl, :] + _dot(qd_s[sl, :], st_s[n], NT)
                if not reverse:
                    of_s[sl, :] = o_dir
                else:
                    o = of_s[sl, :] + o_dir
                    o_ref[sl, :] = o
                    nrm = o * lax.rsqrt(jnp.mean(o * o, axis=-1, keepdims=True) + LN_EPS)
                    hog = hog_ref[sl, :]
                    ya_ref[sl, :] = (nrm * ng_ref[...] * hog * _sigmoid(hog)).astype(BF16)
                return carry

            lax.fori_loop(0, nb, across, 0, unroll=8)

    def col(part):
        return pl.BlockSpec((s_, HEAD), lambda h, b, part=part: (b, part * h_ + h))

    tab = pl.BlockSpec((2, HEAD), lambda h, b: (0, h))
    out = pl.BlockSpec((s_, HEAD), lambda h, b: (b, h))
    return pl.pallas_call(
        body, name=name, grid=(h_, b_),
        in_specs=[col(0), col(1), col(2), col(3), col(4), tab, tab,
                  pl.BlockSpec((1, HEAD), lambda h, b: (0, h))],
        out_specs=[out, out],
        out_shape=[jax.ShapeDtypeStruct((b_ * s_, d_), BF16), jax.ShapeDtypeStruct((b_ * s_, d_), F32)],
        scratch_shapes=[pltpu.VMEM((s_, HEAD), F32)] * 5 + [
            pltpu.VMEM((s_, HEAD), BF16), pltpu.VMEM((nb, HEAD, HEAD), F32), pltpu.VMEM((nb, HEAD, HEAD), BF16),
            pltpu.VMEM((nb, 8, HEAD), F32)],
        compiler_params=_params(),
    )(proj, proj, proj, proj, proj, lbf, lbb, ng)


def _hgrn_bwd(proj, lbf, lbb, ng, o_sum, dya, b_, s_, d_, name):
    h_ = d_ // HEAD
    BLOCK = min(BWD_BLOCK, s_)
    nb = s_ // BLOCK

    def body(hq_ref, hff_ref, hfb_ref, hi_ref, hog_ref, lbf_ref, lbb_ref, ng_ref, o_ref, dya_ref,
             dhq_ref, dhff_ref, dhfb_ref, dhi_ref, dhog_ref, dng_ref, dlbf_ref, dlbb_ref,
             q_s, k_s, l_s, do_s, dq_s, dv_s, dl_s, dk_s, u_s, w_s, st_s, dst_s, dec_s):
        b = pl.program_id(1)

        @pl.when(b == 0)
        def _():
            dng_ref[...] = jnp.zeros_like(dng_ref)
            dlbf_ref[...] = jnp.zeros_like(dlbf_ref)
            dlbb_ref[...] = jnp.zeros_like(dlbb_ref)

        row = lax.broadcasted_iota(jnp.int32, (s_, HEAD), 0) % BLOCK
        brow = lax.broadcasted_iota(jnp.int32, (BLOCK, HEAD), 0)
        hq = hq_ref[...]
        sq = _sigmoid(hq)
        q_s[...] = hq * sq
        o = o_ref[...]
        rinv = lax.rsqrt(jnp.mean(o * o, axis=-1, keepdims=True) + LN_EPS)
        nrm = o * rinv
        hog = hog_ref[...]
        so = _sigmoid(hog)
        gain = ng_ref[...]
        dy = dya_ref[...]
        dhog_ref[...] = (dy * nrm * gain * _dsilu(hog, so)).astype(BF16)
        dng_ref[...] += jnp.sum(dy * nrm * hog * so, axis=0, keepdims=True)
        dn = dy * gain * hog * so
        do_s[...] = rinv * (dn - nrm * jnp.mean(dn * nrm, axis=-1, keepdims=True))

        for reverse, hf_ref, lb_ref, dhf_ref, dlb_ref in (
                (False, hff_ref, lbf_ref, dhff_ref, dlbf_ref), (True, hfb_ref, lbb_ref, dhfb_ref, dlbb_ref)):
            lb = _lower_bound(lb_ref[...])
            sf = _sigmoid(hf_ref[...])
            f = lb + (1.0 - lb) * sf
            k_s[...] = 1.0 - f
            l_s[...] = _block_scan(jnp.log(f), row, reverse, s_, BLOCK)
            mask = _half_mask(reverse, BLOCK)
            total_row = 0 if reverse else BLOCK - 1
            key_end = BLOCK // 2 if reverse else BLOCK // 2 - 1

            def prepare(n, carry, reverse=reverse):
                sl = pl.ds(pl.multiple_of(n * BLOCK, BLOCK), BLOCK)
                _, _, _, _, el, ee, dec, _, _ = _block_exps(l_s[sl, :], reverse)
                vb = hi_ref[sl, :].astype(BF16)
                u_s[n] = _dot(vb, (k_s[sl, :] * ee).astype(BF16), TN)
                w_s[n] = _dot(do_s[sl, :].astype(BF16), (q_s[sl, :] * el).astype(BF16), TN)
                dec_s[n] = jnp.broadcast_to(dec, (8, HEAD))
                return carry

            lax.fori_loop(0, nb, prepare, 0, unroll=4)

            def carry_state(n, st, reverse=reverse):
                idx = (nb - 1 - n) if reverse else n
                st_s[idx] = st
                return st * dec_s[idx][0:1, :] + u_s[idx]

            lax.fori_loop(0, nb, carry_state, jnp.zeros((HEAD, HEAD), F32))

            def carry_grad(n, dst, reverse=reverse):
                idx = n if reverse else (nb - 1 - n)
                dst_s[idx] = dst
                return dst * dec_s[idx][0:1, :] + w_s[idx]

            lax.fori_loop(0, nb, carry_grad, jnp.zeros((HEAD, HEAD), F32))

            def inside(n, carry, reverse=reverse, mask=mask, total_row=total_row, key_end=key_end):
                sl = pl.ds(pl.multiple_of(n * BLOCK, BLOCK), BLOCK)
                eq, ek, e2, e1, el, ee, dec, dec_key, dec_query = _block_exps(l_s[sl, :], reverse)
                qc, kc = q_s[sl, :], k_s[sl, :]
                vb = hi_ref[sl, :].astype(BF16)
                dob = do_s[sl, :].astype(BF16)
                qt, kt, q2, k1 = ((qc * eq).astype(BF16), (kc * ek).astype(BF16),
                                  (qc * e2).astype(BF16), (kc * e1).astype(BF16))
                kend = kc * ee
                st0, dst1 = st_s[n], dst_s[n]
                dstb = dst1.astype(BF16)
                a = jnp.where(mask, _dot(qt, kt, NT), 0.0) + _dot(q2, k1, NT)
                da = _dot(dob, vb, NT)
                dab = da.astype(BF16)
                dad = jnp.where(mask, da, 0.0).astype(BF16)
                dqt, dkt = _dot(dad, kt, NN), _dot(dad, qt, TN)
                dq2, dk1 = _dot(dab, k1, NN), _dot(dab, q2, TN)
                dqd = _dot(dob, st0.astype(BF16), NN)
                dke = _dot(vb, dstb, NN)
                dv = _dot(a.astype(BF16), dob, TN) + _dot(kend.astype(BF16), dstb, NT)
                dq = dqt * eq + dq2 * e2 + dqd * el
                dk = dkt * ek + dk1 * e1 + dke * ee
                dtot = jnp.sum(dke * kend, axis=0, keepdims=True) + jnp.sum(dst1 * st0, axis=0, keepdims=True) * dec
                st_mid = st0 * dec_key + _dot(vb, k1, TN)
                dst_mid = dst1 * dec_query + _dot(dob, q2, TN)
                dmid = jnp.sum(dst_mid * st_mid, axis=0, keepdims=True)
                dl_s[sl, :] = (qc * dq - kc * dk + jnp.where(brow == total_row, dtot, 0.0)
                               + jnp.where(brow == key_end, dmid, 0.0))
                dk_s[sl, :] = dk
                if not reverse:
                    dq_s[sl, :] = dq
                    dv_s[sl, :] = dv
                else:
                    hqc = hq_ref[sl, :]
                    dhq_ref[sl, :] = ((dq_s[sl, :] + dq) * _dsilu(hqc, _sigmoid(hqc))).astype(BF16)
                    dhi_ref[sl, :] = (dv_s[sl, :] + dv).astype(BF16)
                return carry

            lax.fori_loop(0, nb, inside, 0, unroll=2)
            dlogf = _block_scan(dl_s[...], row % (BLOCK // 2), not reverse, s_, BLOCK // 2)
            df = dlogf / f - dk_s[...]
            dhf_ref[...] = (df * (1.0 - lb) * sf * (1.0 - sf)).astype(BF16)
            dlb = jnp.sum(df * (1.0 - sf), axis=0, keepdims=True) * lb * (1.0 - lb)
            dlb_ref[0:1, :] += dlb
            dlb_ref[1:2, :] -= dlb

    def col(part):
        return pl.BlockSpec((s_, HEAD), lambda h, b, part=part: (b, part * h_ + h))

    tab = pl.BlockSpec((2, HEAD), lambda h, b: (0, h))
    vec = pl.BlockSpec((1, HEAD), lambda h, b: (0, h))
    blk = pl.BlockSpec((s_, HEAD), lambda h, b: (b, h))
    act = jax.ShapeDtypeStruct((b_ * s_, d_), BF16)
    state = pltpu.VMEM((nb, HEAD, HEAD), F32)
    return pl.pallas_call(
        body, name=name, grid=(h_, b_),
        in_specs=[col(0), col(1), col(2), col(3), col(4), tab, tab, vec, blk, blk],
        out_specs=[blk] * 5 + [vec, tab, tab],
        out_shape=[act] * 5 + [jax.ShapeDtypeStruct((1, d_), F32), jax.ShapeDtypeStruct((2, d_), F32),
                               jax.ShapeDtypeStruct((2, d_), F32)],
        scratch_shapes=[pltpu.VMEM((s_, HEAD), F32)] * 8 + [state] * 4 + [pltpu.VMEM((nb, 8, HEAD), F32)],
        compiler_params=_params(),
    )(proj, proj, proj, proj, proj, lbf, lbb, ng, o_sum, dya)


def _rope_tables(s_):
    half = ROPE_DIM // 2
    inv_freq = ROPE_THETA ** (-jnp.arange(0, ROPE_DIM, 2, dtype=F32) / ROPE_DIM)
    ang = jnp.arange(s_, dtype=F32)[:, None] * inv_freq
    cos, sin = jnp.cos(ang), jnp.sin(ang)
    zeros = jnp.zeros((s_, HEAD - ROPE_DIM), F32)
    zh = jnp.zeros((s_, half), F32)
    c = jnp.concatenate([cos, cos, jnp.ones((s_, HEAD - ROPE_DIM), F32)], axis=1)
    s1 = jnp.concatenate([-sin, zh, zeros], axis=1)
    s2 = jnp.concatenate([zh, sin, zeros], axis=1)
    return c, s1, s2


def _rope(t, c, s1, s2):
    half = ROPE_DIM // 2
    return t * c + pltpu.roll(t, HEAD - half, 1) * s1 + pltpu.roll(t, half, 1) * s2


def _rope_bwd(dt, c, s1, s2):
    half = ROPE_DIM // 2
    return dt * c + pltpu.roll(dt * s1, half, 1) + pltpu.roll(dt * s2, HEAD - half, 1)


def _window_mask(r0, qb, wk, seg):
    row = lax.broadcasted_iota(jnp.int32, (qb, wk), 0)
    col = lax.broadcasted_iota(jnp.int32, (qb, wk), 1)
    kj = r0 - ATTN_HALF + col
    return (col - row >= 0) & (col - row <= 2 * ATTN_HALF) & (kj >= 0) & (kj < seg)


def _attn_fwd(qkv, tabs, b_, s_, dil, name):
    seg = s_ // dil
    qb = min(128, seg)
    nq, wk = seg // qb, qb + 2 * ATTN_HALF
    scale = HEAD ** -0.5
    ncol = QKV_GROUP // HEAD

    def body(q_ref, k_ref, v_ref, c_ref, s1_ref, s2_ref, o_ref, lse_ref, q_s, k_s, v_s):
        c, s1, s2 = c_ref[...], s1_ref[...], s2_ref[...]
        q_s[...] = _rope(q_ref[...], c, s1, s2).astype(BF16)
        k_s[...] = jnp.zeros_like(k_s)
        v_s[...] = jnp.zeros_like(v_s)
        k_s[ATTN_HALF:ATTN_HALF + seg, :] = _rope(k_ref[...], c, s1, s2).astype(BF16)
        v_s[ATTN_HALF:ATTN_HALF + seg, :] = v_ref[...].astype(BF16)

        def step(i, carry):
            r0 = pl.multiple_of(i * qb, qb)
            sc = _dot(q_s[pl.ds(r0, qb), :], k_s[pl.ds(r0, wk), :], NT) * scale
            sc = jnp.where(_window_mask(r0, qb, wk, seg), sc, NEG_INF)
            m = jnp.max(sc, axis=-1, keepdims=True)
            p = jnp.exp(sc - m)
            den = jnp.sum(p, axis=-1, keepdims=True)
            o_ref[pl.ds(r0, qb), :] = _dot(p.astype(BF16), v_s[pl.ds(r0, wk), :], NN) / den
            lse_ref[pl.ds(r0, qb), :] = jnp.broadcast_to(m + jnp.log(den), (qb, HEAD))
            return carry

        lax.fori_loop(0, nq, step, 0)

    def col(part):
        return pl.BlockSpec((seg, HEAD), lambda b, r, h, part=part: (b, r * ncol + part * ATTN_HEADS + h))

    tab = pl.BlockSpec((seg, HEAD), lambda b, r, h: (0, r))
    out = pl.BlockSpec((seg, HEAD), lambda b, r, h: (b, r * ATTN_HEADS + h))
    shape = jax.ShapeDtypeStruct((b_ * seg, dil * ATTN_OUT), F32)
    return pl.pallas_call(
        body, name=name, grid=(b_, dil, ATTN_HEADS),
        in_specs=[col(0), col(1), col(2), tab, tab, tab],
        out_specs=[out, out],
        out_shape=[shape, shape],
        scratch_shapes=[pltpu.VMEM((seg, HEAD), BF16), pltpu.VMEM((seg + 2 * ATTN_HALF, HEAD), BF16),
                        pltpu.VMEM((seg + 2 * ATTN_HALF, HEAD), BF16)],
        compiler_params=_params(),
    )(qkv, qkv, qkv, *tabs)


def _attn_bwd(qkv, tabs, dog, cg, lse, b_, s_, dil, name):
    seg = s_ // dil
    qb = min(128, seg)
    nq, wk = seg // qb, qb + 2 * ATTN_HALF
    scale = HEAD ** -0.5
    ncol = QKV_GROUP // HEAD

    def body(q_ref, k_ref, v_ref, c_ref, s1_ref, s2_ref, do_ref, cg_ref, lse_ref, dq_ref, dk_ref, dv_ref,
             q_s, k_s, v_s, dk_s, dv_s):
        c, s1, s2 = c_ref[...], s1_ref[...], s2_ref[...]
        q_s[...] = _rope(q_ref[...], c, s1, s2).astype(BF16)
        k_s[...] = jnp.zeros_like(k_s)
        v_s[...] = jnp.zeros_like(v_s)
        k_s[ATTN_HALF:ATTN_HALF + seg, :] = _rope(k_ref[...], c, s1, s2).astype(BF16)
        v_s[ATTN_HALF:ATTN_HALF + seg, :] = v_ref[...].astype(BF16)
        dk_s[...] = jnp.zeros_like(dk_s)
        dv_s[...] = jnp.zeros_like(dv_s)

        def step(i, carry):
            r0 = pl.multiple_of(i * qb, qb)
            rows, win = pl.ds(r0, qb), pl.ds(r0, wk)
            qc, kw, vw = q_s[rows, :], k_s[win, :], v_s[win, :]
            sc = _dot(qc, kw, NT) * scale
            p = jnp.where(_window_mask(r0, qb, wk, seg), jnp.exp(sc - lse_ref[rows, 0:1]), 0.0)
            dob = do_ref[rows, :].astype(BF16)
            dp = _dot(dob, vw, NT)
            ds = (p * (dp + cg_ref[rows, 0:1]) * scale).astype(BF16)
            dq = _dot(ds, kw, NN)
            dq_ref[rows, :] = _rope_bwd(dq, c_ref[rows, :], s1_ref[rows, :], s2_ref[rows, :]).astype(BF16)
            dk_s[win, :] += _dot(ds, qc, TN)
            dv_s[win, :] += _dot(p.astype(BF16), dob, TN)
            return carry

        lax.fori_loop(0, nq, step, 0)
        dk_ref[...] = _rope_bwd(dk_s[ATTN_HALF:ATTN_HALF + seg, :], c, s1, s2).astype(BF16)
        dv_ref[...] = dv_s[ATTN_HALF:ATTN_HALF + seg, :].astype(BF16)

    def col(part):
        return pl.BlockSpec((seg, HEAD), lambda b, r, h, part=part: (b, r * ncol + part * ATTN_HEADS + h))

    tab = pl.BlockSpec((seg, HEAD), lambda b, r, h: (0, r))
    out = pl.BlockSpec((seg, HEAD), lambda b, r, h: (b, r * ATTN_HEADS + h))
    shape = jax.ShapeDtypeStruct((b_ * seg, dil * ATTN_OUT), BF16)
    return pl.pallas_call(
        body, name=name, grid=(b_, dil, ATTN_HEADS),
        in_specs=[col(0), col(1), col(2), tab, tab, tab, out, out, out],
        out_specs=[out, out, out],
        out_shape=[shape, shape, shape],
        scratch_shapes=[pltpu.VMEM((seg, HEAD), BF16), pltpu.VMEM((seg + 2 * ATTN_HALF, HEAD), BF16),
                        pltpu.VMEM((seg + 2 * ATTN_HALF, HEAD), BF16),
                        pltpu.VMEM((seg + 2 * ATTN_HALF, HEAD), F32), pltpu.VMEM((seg + 2 * ATTN_HALF, HEAD), F32)],
        compiler_params=_params(),
    )(qkv, qkv, qkv, *tabs, dog, cg, lse)


def _group_weights(lses):
    m = jnp.maximum(jnp.maximum(lses[0], lses[1]), lses[2])
    es = [jnp.exp(l - m) for l in lses]
    den = es[0] + es[1] + es[2]
    return [e / den for e in es]


def _combine_fwd(outs, lses, name):
    t_, w_ = outs[0].shape
    tm = _tile(t_, 512, 8)
    ng = len(outs)

    def body(*refs):
        ws = _group_weights([r[...] for r in refs[ng:2 * ng]])
        acc = ws[0] * refs[0][...]
        for g in range(1, ng):
            acc = acc + ws[g] * refs[g][...]
        refs[2 * ng][...] = acc.astype(BF16)

    row = pl.BlockSpec((tm, w_), lambda i: (i, 0))
    return pl.pallas_call(
        body, name=name, grid=(t_ // tm,), in_specs=[row] * (2 * ng), out_specs=row,
        out_shape=jax.ShapeDtypeStruct((t_, w_), BF16), compiler_params=_params(),
    )(*outs, *lses)


def _combine_bwd(dob, outs, lses, name):
    t_, w_ = outs[0].shape
    tm = _tile(t_, 512, 8)
    ng = len(outs)

    def body(*refs):
        do = refs[0][...]
        os_ = [r[...] for r in refs[1:1 + ng]]
        ws = _group_weights([r[...] for r in refs[1 + ng:1 + 2 * ng]])
        o = ws[0] * os_[0]
        for g in range(1, ng):
            o = o + ws[g] * os_[g]
        prod = do * o
        heads = [jnp.broadcast_to(jnp.sum(prod[:, h * HEAD:(h + 1) * HEAD], axis=-1, keepdims=True), (tm, HEAD))
                 for h in range(w_ // HEAD)]
        tot = jnp.concatenate(heads, axis=1)
        for g in range(ng):
            refs[1 + 2 * ng + g][...] = ws[g] * do
            refs[1 + 3 * ng + g][...] = -ws[g] * tot

    row = pl.BlockSpec((tm, w_), lambda i: (i, 0))
    shape = jax.ShapeDtypeStruct((t_, w_), F32)
    res = pl.pallas_call(
        body, name=name, grid=(t_ // tm,), in_specs=[row] * (1 + 2 * ng), out_specs=[row] * (2 * ng),
        out_shape=[shape] * (2 * ng), compiler_params=_params(),
    )(dob, *outs, *lses)
    return res[:ng], res[ng:]


def _adam_update(w, g, m, v):
    m = ADAM_B1 * m + (1.0 - ADAM_B1) * g
    v = ADAM_B2 * v + (1.0 - ADAM_B2) * (g * g)
    m_hat = m / (1.0 - ADAM_B1 ** ADAM_STEP)
    v_hat = v / (1.0 - ADAM_B2 ** ADAM_STEP)
    return -ADAM_LR * (m_hat / (jnp.sqrt(v_hat) + ADAM_EPS) + ADAM_WD * w), m, v


def _adam(w, g, m, v, name):
    r_, c_ = w.shape
    tr = _tile(r_, 256, 8)

    def body(w_ref, g_ref, m_ref, v_ref, d_ref, mo_ref, vo_ref):
        d_ref[...], mo_ref[...], vo_ref[...] = _adam_update(w_ref[...], g_ref[...], m_ref[...], v_ref[...])

    blk = pl.BlockSpec((tr, c_), lambda i: (i, 0))
    shape = jax.ShapeDtypeStruct((r_, c_), F32)
    return pl.pallas_call(
        body, name=name, grid=(r_ // tr,), in_specs=[blk] * 4, out_specs=[blk] * 3,
        out_shape=[shape] * 3, compiler_params=_params(),
    )(w, g, m, v)


def _sum_partials(recv, name):
    n_, r_, c_ = recv.shape
    tr = _tile(r_, 128, 16)

    def body(p_ref, o_ref):
        acc = p_ref[0].astype(F32)
        for i in range(1, n_):
            acc = acc + p_ref[i].astype(F32)
        o_ref[...] = acc

    return pl.pallas_call(
        body, name=name, grid=(r_ // tr,),
        in_specs=[pl.BlockSpec((n_, tr, c_), lambda i: (0, i, 0))],
        out_specs=pl.BlockSpec((tr, c_), lambda i: (i, 0)),
        out_shape=jax.ShapeDtypeStruct((r_, c_), F32), compiler_params=_params(),
    )(recv)


def _small_sum_adam(parts, w, m, v, name):
    n_, r_, c_ = parts.shape

    def body(p_ref, w_ref, m_ref, v_ref, g_ref, d_ref, mo_ref, vo_ref):
        g = p_ref[0]
        for i in range(1, n_):
            g = g + p_ref[i]
        g_ref[...] = g
        d_ref[...], mo_ref[...], vo_ref[...] = _adam_update(w_ref[...], g, m_ref[...], v_ref[...])

    shape = jax.ShapeDtypeStruct((r_, c_), F32)
    return pl.pallas_call(body, name=name, out_shape=[shape] * 4, compiler_params=_params())(parts, w, m, v)


def _my_place():
    x, y, c = lax.axis_index("x"), lax.axis_index("y"), lax.axis_index("c")
    return x, y, c


def _peer(x, y, c, d):
    px = 1 - x if d & 4 else x
    py = 1 - y if d & 2 else y
    pc = 1 - c if d & 1 else c
    return (px, py, pc), 4 * px + 2 * py + pc


def _all_gather(shards, name):
    nw = len(shards)

    def body(*refs):
        ins, outs = refs[:nw], refs[nw:2 * nw]
        send_sems, recv_sems, local_sems = refs[2 * nw:]
        x, y, c = _my_place()
        me = 4 * x + 2 * y + c
        copies = []
        for k in range(nw):
            rows = shards[k].shape[0]
            mine = outs[k].at[pl.ds(pl.multiple_of(me * rows, 16), rows), :]
            local = pltpu.make_async_copy(ins[k], mine, local_sems.at[k])
            local.start()
            copies.append(local)
            for d in range(1, N_DEV):
                place, _ = _peer(x, y, c, d)
                remote = pltpu.make_async_remote_copy(
                    src_ref=ins[k], dst_ref=mine, send_sem=send_sems.at[d - 1, k], recv_sem=recv_sems.at[d - 1, k],
                    device_id=place, device_id_type=MESH)
                remote.start()
                copies.append(remote)
        for cp in copies:
            cp.wait()

    hbm = pl.BlockSpec(memory_space=pl.ANY)
    return pl.pallas_call(
        body, name=name, in_specs=[hbm] * nw, out_specs=[hbm] * nw,
        out_shape=[jax.ShapeDtypeStruct((N_DEV * s.shape[0], s.shape[1]), s.dtype) for s in shards],
        scratch_shapes=[pltpu.SemaphoreType.DMA((N_DEV - 1, nw)), pltpu.SemaphoreType.DMA((N_DEV - 1, nw)),
                        pltpu.SemaphoreType.DMA((nw,))],
    )(*shards)


HBM_SPEC = pl.BlockSpec(memory_space=pltpu.HBM)
SEM_SPEC = pl.BlockSpec(memory_space=pltpu.SEMAPHORE)
EFFECT = pltpu.SideEffectType.DATAFLOW_SIDE_EFFECTING


def _in_hbm(a):
    return pltpu.with_memory_space_constraint(a, pltpu.HBM)


def _token_shape():
    return jax.ShapeDtypeStruct((8, HEAD), F32)


SIBLING = 1
OTHER_CHIPS = (4, 2, 6)


def _rows_of(ref, num, rows):
    return ref.at[pl.ds(pl.multiple_of(num * rows, 16), rows), :]


def _gather_start(shards, name):
    nw = len(shards)
    lands = [lax.empty((N_DEV * s.shape[0], s.shape[1]), s.dtype) for s in shards]
    n_to = 1 + len(OTHER_CHIPS)

    def body(*refs):
        ins, lnd = refs[:nw], refs[nw:2 * nw]
        send, from_sib, from_chips, own = (refs[(2 + i) * nw:(3 + i) * nw] for i in range(4))
        token = refs[8 * nw]
        x, y, c = _my_place()
        me = 4 * x + 2 * y + c
        for k in range(nw):
            mine = _rows_of(lnd[k], me, shards[k].shape[0])
            pltpu.make_async_copy(ins[k], mine, own[k]).start()
            for i, d in enumerate((SIBLING,) + OTHER_CHIPS):
                place, _ = _peer(x, y, c, d)
                pltpu.make_async_remote_copy(
                    src_ref=ins[k], dst_ref=mine, send_sem=send[k].at[i],
                    recv_sem=from_sib[k] if i == 0 else from_chips[k].at[i - 1],
                    device_id=place, device_id_type=MESH).start()
        token[...] = jnp.zeros_like(token)

    dma = pltpu.SemaphoreType.DMA
    sems = [dma((n_to,))] * nw + [dma(())] * nw + [dma((len(OTHER_CHIPS),))] * nw + [dma(())] * nw
    thru = [pltpu.HBM(a.shape, a.dtype) for a in list(shards) + lands]
    res = pl.pallas_call(
        body, name=name, out_shape=(*sems, *thru, _token_shape()),
        in_specs=[HBM_SPEC] * (2 * nw),
        out_specs=(*([SEM_SPEC] * (4 * nw)), *([HBM_SPEC] * (2 * nw)), pl.BlockSpec(memory_space=pltpu.VMEM)),
        input_output_aliases={i: 4 * nw + i for i in range(2 * nw)},
        compiler_params=pltpu.CompilerParams(has_side_effects=EFFECT),
    )(*[_in_hbm(s) for s in shards], *[_in_hbm(l) for l in lands])
    return [dict(send=res[k], from_sib=res[nw + k], from_chips=res[2 * nw + k], own=res[3 * nw + k],
                 src=res[4 * nw + k], land=res[5 * nw + k]) for k in range(nw)], res[6 * nw]


def _gather_forward(pending, after, name):
    rows = pending["src"].shape[0]
    n_fw = len(OTHER_CHIPS)

    def body(land_ref, from_chips, after_ref, fw_send, fw_recv, land_thru):
        x, y, c = _my_place()
        sibling, _ = _peer(x, y, c, SIBLING)
        for j, d in enumerate(OTHER_CHIPS):
            _, num = _peer(x, y, c, d)
            block = _rows_of(land_ref, num, rows)
            pltpu.make_async_remote_copy(
                src_ref=block, dst_ref=block, send_sem=fw_send.at[j], recv_sem=from_chips.at[j],
                device_id=sibling, device_id_type=MESH).wait_recv()
            pltpu.make_async_remote_copy(
                src_ref=block, dst_ref=block, send_sem=fw_send.at[j], recv_sem=fw_recv.at[j],
                device_id=sibling, device_id_type=MESH).start()

    land = pending["land"]
    dma = pltpu.SemaphoreType.DMA
    fw_send, fw_recv, land = pl.pallas_call(
        body, name=name, out_shape=(dma((n_fw,)), dma((n_fw,)), pltpu.HBM(land.shape, land.dtype)),
        in_specs=(HBM_SPEC, SEM_SPEC, pl.BlockSpec(memory_space=pl.ANY)),
        out_specs=(SEM_SPEC, SEM_SPEC, HBM_SPEC), input_output_aliases={0: 2},
        compiler_params=pltpu.CompilerParams(has_side_effects=EFFECT),
    )(land, pending["from_chips"], after)
    return dict(pending, land=land, fw_send=fw_send, fw_recv=fw_recv)


def _gather_wait(pending, name):
    rows = pending["src"].shape[0]

    def body(src_ref, land_ref, send, from_sib, own, fw_send, fw_recv, src_dead, got):
        x, y, c = _my_place()
        me = 4 * x + 2 * y + c
        sibling, sib_num = _peer(x, y, c, SIBLING)
        mine = _rows_of(land_ref, me, rows)
        pltpu.make_async_copy(src_ref, mine, own).wait()
        for i in range(1 + len(OTHER_CHIPS)):
            pltpu.make_async_remote_copy(
                src_ref=src_ref, dst_ref=mine, send_sem=send.at[i], recv_sem=from_sib,
                device_id=sibling, device_id_type=MESH).wait_send()
        theirs = _rows_of(land_ref, sib_num, rows)
        pltpu.make_async_remote_copy(
            src_ref=src_ref, dst_ref=theirs, send_sem=send.at[0], recv_sem=from_sib,
            device_id=sibling, device_id_type=MESH).wait_recv()
        for j, d in enumerate(OTHER_CHIPS):
            _, num = _peer(x, y, c, d)
            sent = _rows_of(land_ref, num, rows)
            _, got_num = _peer(x, y, c, d | SIBLING)
            arrived = _rows_of(land_ref, got_num, rows)
            cp = pltpu.make_async_remote_copy(
                src_ref=sent, dst_ref=arrived, send_sem=fw_send.at[j], recv_sem=fw_recv.at[j],
                device_id=sibling, device_id_type=MESH)
            cp.wait_send()
            cp.wait_recv()

    src, land = pending["src"], pending["land"]
    return pl.pallas_call(
        body, name=name, out_shape=(pltpu.HBM(src.shape, src.dtype), pltpu.HBM(land.shape, land.dtype)),
        in_specs=(HBM_SPEC, HBM_SPEC) + (SEM_SPEC,) * 5,
        out_specs=(HBM_SPEC, HBM_SPEC), input_output_aliases={0: 0, 1: 1},
        compiler_params=pltpu.CompilerParams(has_side_effects=EFFECT),
    )(src, land, pending["send"], pending["from_sib"], pending["own"], pending["fw_send"], pending["fw_recv"])[1]


def _scatter_start(full, name):
    rows, cols = full.shape[0] // N_DEV, full.shape[1]
    land = lax.empty((N_DEV, rows, cols), full.dtype)

    def body(full_ref, land_ref, send, recv, own, full_thru, land_thru, token):
        x, y, c = _my_place()
        me = 4 * x + 2 * y + c
        slab = land_ref.at[me]
        pltpu.make_async_copy(full_ref.at[pl.ds(pl.multiple_of(me * rows, 16), rows), :], slab, own).start()
        for d in range(1, N_DEV):
            place, num = _peer(x, y, c, d)
            pltpu.make_async_remote_copy(
                src_ref=full_ref.at[pl.ds(pl.multiple_of(num * rows, 16), rows), :], dst_ref=slab,
                send_sem=send.at[d - 1], recv_sem=recv.at[d - 1], device_id=place, device_id_type=MESH).start()
        token[...] = jnp.zeros_like(token)

    res = pl.pallas_call(
        body, name=name,
        out_shape=(pltpu.SemaphoreType.DMA((N_DEV - 1,)), pltpu.SemaphoreType.DMA((N_DEV - 1,)),
                   pltpu.SemaphoreType.DMA(()),
                   pltpu.HBM(full.shape, full.dtype), pltpu.HBM(land.shape, land.dtype), _token_shape()),
        in_specs=(HBM_SPEC, HBM_SPEC),
        out_specs=(SEM_SPEC, SEM_SPEC, SEM_SPEC, HBM_SPEC, HBM_SPEC, pl.BlockSpec(memory_space=pltpu.VMEM)),
        input_output_aliases={0: 3, 1: 4},
        compiler_params=pltpu.CompilerParams(has_side_effects=EFFECT),
    )(_in_hbm(full), _in_hbm(land))
    return dict(send=res[0], recv=res[1], own=res[2], src=res[3], land=res[4]), res[5]


def _scatter_wait(pending, after, name):
    rows = pending["land"].shape[1]

    def body(src_ref, land_ref, send, recv, own, after_ref, src_dead, got):
        x, y, c = _my_place()
        me = 4 * x + 2 * y + c
        pltpu.make_async_copy(src_ref.at[pl.ds(pl.multiple_of(me * rows, 16), rows), :], land_ref.at[me], own).wait()
        for d in range(1, N_DEV):
            place, num = _peer(x, y, c, d)
            cp = pltpu.make_async_remote_copy(
                src_ref=src_ref.at[pl.ds(pl.multiple_of(num * rows, 16), rows), :], dst_ref=land_ref.at[me],
                send_sem=send.at[d - 1], recv_sem=recv.at[d - 1], device_id=place, device_id_type=MESH)
            cp.wait_send()
            cp.wait_recv()

    src, land = pending["src"], pending["land"]
    return pl.pallas_call(
        body, name=name, out_shape=(pltpu.HBM(src.shape, src.dtype), pltpu.HBM(land.shape, land.dtype)),
        in_specs=(HBM_SPEC, HBM_SPEC, SEM_SPEC, SEM_SPEC, SEM_SPEC, pl.BlockSpec(memory_space=pl.ANY)),
        out_specs=(HBM_SPEC, HBM_SPEC), input_output_aliases={0: 0, 1: 1},
        compiler_params=pltpu.CompilerParams(has_side_effects=EFFECT),
    )(src, land, pending["send"], pending["recv"], pending["own"], after)[1]


BIG = ("ffn1_w_in", "ffn1_w_out", "mix_w_in", "w_branch_a", "w_branch_b", "mix_w_out", "ffn2_w_in", "ffn2_w_out")
TRANSPOSED = ("ffn1_w_in", "mix_w_in", "w_branch_b", "ffn2_w_in")
SMALL = ("ln1_g", "ln1_b", "ln2_g", "ln2_b", "ln3_g", "ln3_b", "hgrn_norm_g", "hgrn_lb_fwd", "hgrn_lb_bwd")
SMALL_ROWS = 16


def _local_step(x, target, weight, emit, sp):
    b_, s_, d_ = x.shape
    t_ = b_ * s_
    x2, tgt = x.reshape(t_, d_), target.reshape(t_, d_)
    xb = x2.astype(BF16)
    w1i = weight("ffn1_w_in", xb)
    g1, u1, a1 = _ffn_in_fwd(xb, w1i, "ffn1_in")
    w1o = weight("ffn1_w_out", a1)
    r1, h1, h1b = _mm_res_ln_fwd(a1, w1o, x2, sp["ln1_g"], sp["ln1_b"], 0.5, "ffn1_out_ln1")
    wmx = weight("mix_w_in", h1b)
    proj = _mm_nt(h1b, wmx, F32, "mix_in")
    ya_in, o_sum = _hgrn_fwd(proj, sp["hgrn_lb_fwd"], sp["hgrn_lb_bwd"], sp["hgrn_norm_g"], b_, s_, d_, "hgrn_fwd")
    tabs = _rope_tables(s_)
    qkvs, gtabs, outs, lses = [], [], [], []
    for gi, (_, dil) in enumerate(ATTN_GROUPS):
        off = 5 * d_ + gi * QKV_GROUP
        qkv = proj[:, off:off + QKV_GROUP].reshape(t_ // dil, dil * QKV_GROUP)
        gt = [tb.reshape(s_ // dil, dil * HEAD) for tb in tabs]
        o_g, lse_g = _attn_fwd(qkv, gt, b_, s_, dil, f"attn_fwd_{gi}")
        qkvs.append(qkv)
        gtabs.append(gt)
        outs.append(o_g.reshape(t_, ATTN_OUT))
        lses.append(lse_g.reshape(t_, ATTN_OUT))
    ob = _combine_fwd(outs, lses, "attn_combine")
    wa, wb = weight("w_branch_a", ya_in), weight("w_branch_b", ob)
    ya, yb, z = _gate_out_fwd(ya_in, ob, wa, wb, proj, d_, "branch_gate")
    wo = weight("mix_w_out", z)
    r2, h2, h2b = _mm_res_ln_fwd(z, wo, h1, sp["ln2_g"], sp["ln2_b"], 1.0, "mix_out_ln2")
    w2i = weight("ffn2_w_in", h2b)
    g2, u2, a2 = _ffn_in_fwd(h2b, w2i, "ffn2_in")
    w2o = weight("ffn2_w_out", a2)
    r3, _, _ = _mm_res_ln_fwd(a2, w2o, h2, sp["ln3_g"], sp["ln3_b"], 0.5, "ffn2_out_ln3")
    dr3, dr3b, dg3, db3, loss = _ln_loss_bwd(r3, tgt, sp["ln3_g"], sp["ln3_b"], "loss_ln3_bwd")
    dep = emit("ffn2_w_out", _mm_tn(a2, dr3b, 0.5, "d_ffn2_w_out"))
    dgate2, dup2 = _ffn_mid_bwd(dr3b, w2o, g2, u2, 0.5, "ffn2_mid_bwd", dep)
    du2 = (dgate2, dup2)
    dep = emit("ffn2_w_in", _mm_tn(du2, h2b, 1.0, "d_ffn2_w_in"))
    dr2, dr2b, dg2, db2 = _mm_nn_res_lnbwd(du2, w2i, dr3, r2, sp["ln2_g"], "ffn2_in_bwd_ln2", dep)
    dep = emit("mix_w_out", _mm_tn(z, dr2b, 1.0, "d_mix_w_out"))
    dya, dyb, dga, dgb = _dz_gate_bwd(dr2b, wo, proj, ya, yb, d_, "branch_gate_bwd", dep)
    dep = emit("w_branch_a", _mm_tn(ya_in, dya, 1.0, "d_w_branch_a"))
    dya_in = _mm_nt(dya, wa, F32, "branch_a_bwd", dep)
    dep = emit("w_branch_b", _mm_tn(dyb, ob, 1.0, "d_w_branch_b"))
    dob = _mm_nn(dyb, wb, F32, "branch_b_bwd", dep)
    dhq, dhff, dhfb, dhi, dhog, dng, dlbf, dlbb = _hgrn_bwd(
        proj, sp["hgrn_lb_fwd"], sp["hgrn_lb_bwd"], sp["hgrn_norm_g"], o_sum, dya_in, b_, s_, d_, "hgrn_bwd")
    dogs, cgs = _combine_bwd(dob, outs, lses, "attn_combine_bwd")
    dqkv = []
    for gi, (_, dil) in enumerate(ATTN_GROUPS):
        seg_rows = t_ // dil
        dq, dk, dv = _attn_bwd(qkvs[gi], gtabs[gi], dogs[gi].reshape(seg_rows, dil * ATTN_OUT),
                               cgs[gi].reshape(seg_rows, dil * ATTN_OUT), lses[gi].reshape(seg_rows, dil * ATTN_OUT),
                               b_, s_, dil, f"attn_bwd_{gi}")
        dqkv += [t.reshape(t_, ATTN_OUT) for t in (dq, dk, dv)]
    dproj = jnp.concatenate([dhq, dhff, dhfb, dhi, dhog] + dqkv + [dga, dgb], axis=1)
    dep = emit("mix_w_in", _mm_tn(dproj, h1b, 1.0, "d_mix_w_in"))
    dr1, dr1b, dg1, db1 = _mm_nn_res_lnbwd(dproj, wmx, dr2, r1, sp["ln1_g"], "mix_in_bwd_ln1", dep)
    dep = emit("ffn1_w_out", _mm_tn(a1, dr1b, 0.5, "d_ffn1_w_out"))
    dgate1, dup1 = _ffn_mid_bwd(dr1b, w1o, g1, u1, 0.5, "ffn1_mid_bwd", dep)
    du1 = (dgate1, dup1)
    dep = emit("ffn1_w_in", _mm_tn(du1, xb, 1.0, "d_ffn1_w_in"))
    grad_x = _mm_nn_res(du1, w1i, dr1, "ffn1_in_bwd", dep)
    small = {"ln1_g": dg1, "ln1_b": db1, "ln2_g": dg2, "ln2_b": db2, "ln3_g": dg3, "ln3_b": db3,
             "hgrn_norm_g": dng, "hgrn_lb_fwd": dlbf, "hgrn_lb_bwd": dlbb}
    return loss, grad_x.reshape(b_, s_, d_), small


def _pack_small(vals):
    rows = jnp.concatenate([vals[n] for n in SMALL], axis=0)
    return jnp.pad(rows, ((0, SMALL_ROWS - rows.shape[0]), (0, 0)))


def _unpack_small(packed):
    out, r = {}, 0
    for n in SMALL:
        k = 2 if n.startswith("hgrn_lb") else 1
        out[n] = packed[r:r + k]
        r += k
    return out


def kernel(x, ffn1_w_in, ffn1_w_out, ln1_g, ln1_b, mix_w_in, hgrn_lb_fwd, hgrn_lb_bwd, hgrn_norm_g, w_branch_a, w_branch_b, mix_w_out, ln2_g, ln2_b, ffn2_w_in, ffn2_w_out, ln3_g, ln3_b, loss_target, m_ffn1_w_in, m_ffn1_w_out, m_ln1_g, m_ln1_b, m_mix_w_in, m_hgrn_lb_fwd, m_hgrn_lb_bwd, m_hgrn_norm_g, m_w_branch_a, m_w_branch_b, m_mix_w_out, m_ln2_g, m_ln2_b, m_ffn2_w_in, m_ffn2_w_out, m_ln3_g, m_ln3_b, v_ffn1_w_in, v_ffn1_w_out, v_ln1_g, v_ln1_b, v_mix_w_in, v_hgrn_lb_fwd, v_hgrn_lb_bwd, v_hgrn_norm_g, v_w_branch_a, v_w_branch_b, v_mix_w_out, v_ln2_g, v_ln2_b, v_ffn2_w_in, v_ffn2_w_out, v_ln3_g, v_ln3_b):
    args = dict(locals())
    big_w = {n: args[n][0] for n in BIG}
    sp = {n: args[n] for n in SMALL}
    def rows_bf16(n, zero=0.0):
        w = big_w[n] + zero
        return (w.T if n in TRANSPOSED else w).astype(BF16)

    first, rest = BIG[:2], BIG[2:]
    pending, token = _gather_start([rows_bf16(n) for n in first], "gather_start_ffn1")
    gathering = dict(zip(first, pending))
    pending, all_started = _gather_start([rows_bf16(n, token[0, 0]) for n in rest], "gather_start_rest")
    gathering.update(zip(rest, pending))
    scattering = {}

    def weight(n, after):
        if n == first[0]:
            after = all_started
        return _gather_wait(_gather_forward(gathering[n], after, f"gather_forward_{n}"), f"gather_wait_{n}")

    def emit(n, grad):
        scattering[n], token = _scatter_start(grad, f"scatter_start_{n}")
        return token

    loss_part, grad_x, small = _local_step(x, loss_target, weight, emit, sp)
    loss = lax.psum(loss_part[0, 0], ("x", "y", "c"))
    out_g, out_d, out_m, out_v = {}, {}, {}, {}
    for n in BIG:
        g = _sum_partials(_scatter_wait(scattering[n], grad_x, f"scatter_wait_{n}"), f"sum_{n}")
        if n in TRANSPOSED:
            g = g.T
        d_w, m_w, v_w = _adam(big_w[n], g, args["m_" + n][0], args["v_" + n][0], f"adam_{n}")
        out_g[n], out_d[n], out_m[n], out_v[n] = g[None], d_w[None], m_w[None], v_w[None]
    (parts,) = _all_gather([_pack_small(small)], "gather_small_grads")
    res = _small_sum_adam(parts.reshape(N_DEV, SMALL_ROWS, parts.shape[1]), _pack_small(sp),
                          _pack_small({n: args["m_" + n] for n in SMALL}),
                          _pack_small({n: args["v_" + n] for n in SMALL}), "small_adam")
    sg, sd, sm, sv = (_unpack_small(r) for r in res)
    out_g.update(sg), out_d.update(sd), out_m.update(sm), out_v.update(sv)
    order = ("ffn1_w_in", "ffn1_w_out", "ln1_g", "ln1_b", "mix_w_in", "hgrn_lb_fwd", "hgrn_lb_bwd", "hgrn_norm_g",
             "w_branch_a", "w_branch_b", "mix_w_out", "ln2_g", "ln2_b", "ffn2_w_in", "ffn2_w_out", "ln3_g", "ln3_b")
    return (loss, grad_x, *[out_g[n] for n in order], *[out_d[n] for n in order],
            *[out_m[n] for n in order], *[out_v[n] for n in order])
```

```python
import functools

import jax
import jax.numpy as jnp
from jax import lax
from jax.experimental import pallas as pl
from jax.experimental.pallas import tpu as pltpu

F32 = jnp.float32
BF16 = jnp.bfloat16

N_DEV = 8
HEAD = 128
CHUNK = 32
ATTN_GROUPS = ((128, 1), (512, 4), (2048, 16))
ATTN_HEADS = 4
ATTN_HALF = 64
QKV_GROUP = 3 * ATTN_HEADS * HEAD
QKV_WIDTH = len(ATTN_GROUPS) * QKV_GROUP
ATTN_OUT = ATTN_HEADS * HEAD
ROPE_THETA = 500000.0
ROPE_DIM = HEAD // 4
ALPHA = 2.0 ** 0.25
LN_EPS = 1e-5
NEG_INF = -1e30
ADAM_LR, ADAM_B1, ADAM_B2, ADAM_EPS, ADAM_WD, ADAM_STEP = 0.001, 0.9, 0.999, 1e-08, 0.01, 10
VMEM_LIMIT = 56 * 1024 * 1024

NT = (((1,), (1,)), ((), ()))
NN = (((1,), (0,)), ((), ()))
TN = (((0,), (0,)), ((), ()))
MESH = pl.DeviceIdType.MESH


def _dot(a, b, dims):
    return lax.dot_general(a, b, dims, preferred_element_type=F32)


def _tile(n, pref, mult=128):
    if n <= pref:
        return n
    t = (pref // mult) * mult
    while t >= mult:
        if n % t == 0:
            return t
        t -= mult
    return n


def _tile_multi(ns, pref, mult=128):
    t = (pref // mult) * mult
    while t >= mult:
        if all(n % t == 0 for n in ns):
            return t
        t -= mult
    raise ValueError(f"no common tile for {ns}")


def _params(**kw):
    return pltpu.CompilerParams(vmem_limit_bytes=VMEM_LIMIT, **kw)


def _after(body, n_in, dep):
    if dep is None:
        return body, [], []

    def wrapped(*refs):
        body(*refs[:n_in], *refs[n_in + 1:])

    return wrapped, [pl.BlockSpec(dep.shape, lambda *_: (0,) * dep.ndim)], [dep]


def _pieces(a):
    pieces = tuple(a) if isinstance(a, (tuple, list)) else (a,)
    assert all(p.shape == pieces[0].shape for p in pieces)
    return pieces, pieces[0].shape[0], pieces[0].shape[1], len(pieces)


def _for_piece(step, p, per, npc, fn):
    if npc == 1:
        fn()
    else:
        pl.when((step >= p * per) & (step < (p + 1) * per))(fn)


def _sigmoid(x):
    return jax.nn.sigmoid(x)


def _dsilu(x, s):
    return s * (1.0 + x * (1.0 - s))


def _ln_stats(r):
    mu = jnp.mean(r, axis=-1, keepdims=True)
    xc = r - mu
    var = jnp.mean(xc * xc, axis=-1, keepdims=True)
    rstd = lax.rsqrt(var + LN_EPS)
    return xc * rstd, rstd


def _ln_bwd(dy, xhat, rstd, g):
    dyg = dy * g
    m1 = jnp.mean(dyg, axis=-1, keepdims=True)
    m2 = jnp.mean(dyg * xhat, axis=-1, keepdims=True)
    return rstd * (dyg - m1 - xhat * m2)


ROW_TILE = 1024
SUB_ROWS = 256


def _once(shape, index_map):
    return pl.BlockSpec(shape, index_map, pipeline_mode=pl.Buffered(1))


def _for_row_blocks(tm, fn):
    sub = SUB_ROWS if tm % SUB_ROWS == 0 else tm

    def step(s, carry):
        fn(pl.ds(pl.multiple_of(s * sub, sub), sub))
        return carry

    lax.fori_loop(0, tm // sub, step, 0)


def _row_runs(tm):
    sub = SUB_ROWS if tm % SUB_ROWS == 0 else tm
    return [slice(s, s + sub) for s in range(0, tm, sub)]


def _ffn_in_fwd(xb, w_t, name):
    t_, d_ = xb.shape
    f_ = w_t.shape[0] // 2
    tm, tn = _tile(t_, ROW_TILE, 8), _tile(f_, 512)
    nj = f_ // tn

    def body(x_ref, wg_ref, wu_ref, g_ref, u_ref, a_ref):
        wg, wu = wg_ref[...], wu_ref[...]
        for rows in _row_runs(tm):
            x = x_ref[rows, :]
            g = _dot(x, wg, NT)
            u = _dot(x, wu, NT)
            g_ref[rows, :] = g.astype(BF16)
            u_ref[rows, :] = u.astype(BF16)
            a_ref[rows, :] = (g * _sigmoid(g) * u).astype(BF16)

    return pl.pallas_call(
        body, name=name, grid=(t_ // tm, nj),
        in_specs=[pl.BlockSpec((tm, d_), lambda i, j: (i, 0)),
                  pl.BlockSpec((tn, d_), lambda i, j: (j, 0)),
                  pl.BlockSpec((tn, d_), lambda i, j: (j + nj, 0))],
        out_specs=[pl.BlockSpec((tm, tn), lambda i, j: (i, j))] * 3,
        out_shape=[jax.ShapeDtypeStruct((t_, f_), BF16)] * 3,
        compiler_params=_params(),
    )(xb, w_t, w_t)


def _mm_res_ln_fwd(a, w, res, g, b, scale, name):
    t_, k_ = a.shape
    d_ = w.shape[1]
    tm, tk = _tile(t_, ROW_TILE, 8), _tile(k_, 512)
    nk = k_ // tk

    def body(a_ref, w_ref, res_ref, g_ref, b_ref, r_ref, h_ref, hb_ref, acc):
        k = pl.program_id(1)

        @pl.when(k == 0)
        def _():
            acc[...] = jnp.zeros_like(acc)

        acc[...] += _dot(a_ref[...], w_ref[...], NN)

        @pl.when(k == nk - 1)
        def _():
            def rows_out(rows):
                r = ALPHA * res_ref[rows, :] + scale * acc[rows, :]
                xhat, _ = _ln_stats(r)
                h = xhat * g_ref[...] + b_ref[...]
                r_ref[rows, :] = r
                h_ref[rows, :] = h
                hb_ref[rows, :] = h.astype(BF16)

            _for_row_blocks(tm, rows_out)

    row = _once((tm, d_), lambda i, k: (i, 0))
    vec = pl.BlockSpec((1, d_), lambda i, k: (0, 0))
    return pl.pallas_call(
        body, name=name, grid=(t_ // tm, nk),
        in_specs=[pl.BlockSpec((tm, tk), lambda i, k: (i, k)),
                  pl.BlockSpec((tk, d_), lambda i, k: (k, 0)), row, vec, vec],
        out_specs=[row, row, row],
        out_shape=[jax.ShapeDtypeStruct((t_, d_), F32), jax.ShapeDtypeStruct((t_, d_), F32),
                   jax.ShapeDtypeStruct((t_, d_), BF16)],
        scratch_shapes=[pltpu.VMEM((tm, d_), F32)],
        compiler_params=_params(),
    )(a, w, res, g, b)


def _mm_nt(a, w_t, out_dtype, name, dep=None):
    t_, k_ = a.shape
    n_ = w_t.shape[0]
    tm, tn = _tile(t_, ROW_TILE, 8), _tile(n_, 512)

    def body(a_ref, w_ref, o_ref):
        o_ref[...] = _dot(a_ref[...], w_ref[...], NT).astype(out_dtype)

    body, dep_specs, deps = _after(body, 2, dep)
    return pl.pallas_call(
        body, name=name, grid=(t_ // tm, n_ // tn),
        in_specs=[pl.BlockSpec((tm, k_), lambda i, j: (i, 0)),
                  pl.BlockSpec((tn, k_), lambda i, j: (j, 0)), *dep_specs],
        out_specs=pl.BlockSpec((tm, tn), lambda i, j: (i, j)),
        out_shape=jax.ShapeDtypeStruct((t_, n_), out_dtype),
        compiler_params=_params(),
    )(a, w_t, *deps)


def _mm_nn(a, w, out_dtype, name, dep=None):
    t_, k_ = a.shape
    n_ = w.shape[1]
    tm, tn = _tile(t_, ROW_TILE, 8), _tile(n_, 512)

    def body(a_ref, w_ref, o_ref):
        o_ref[...] = _dot(a_ref[...], w_ref[...], NN).astype(out_dtype)

    body, dep_specs, deps = _after(body, 2, dep)
    return pl.pallas_call(
        body, name=name, grid=(t_ // tm, n_ // tn),
        in_specs=[pl.BlockSpec((tm, k_), lambda i, j: (i, 0)),
                  pl.BlockSpec((k_, tn), lambda i, j: (0, j)), *dep_specs],
        out_specs=pl.BlockSpec((tm, tn), lambda i, j: (i, j)),
        out_shape=jax.ShapeDtypeStruct((t_, n_), out_dtype),
        compiler_params=_params(),
    )(a, w, *deps)


def _mm_tn(a, b, scale, name):
    pieces, t_, mp, npc = _pieces(a)
    n_ = b.shape[1]
    tm = _tile(mp, 512)
    per = mp // tm

    def body(*refs):
        b_ref, o_ref = refs[npc], refs[npc + 1]
        for p in range(npc):
            def piece_out(p=p):
                o_ref[...] = (scale * _dot(refs[p][...], b_ref[...], TN)).astype(BF16)

            _for_piece(pl.program_id(0), p, per, npc, piece_out)

    return pl.pallas_call(
        body, name=name, grid=(npc * per,),
        in_specs=[pl.BlockSpec((t_, tm), lambda i, p=p: (0, jnp.clip(i - p * per, 0, per - 1))) for p in range(npc)]
        + [_once((t_, n_), lambda i: (0, 0))],
        out_specs=pl.BlockSpec((tm, n_), lambda i: (i, 0)),
        out_shape=jax.ShapeDtypeStruct((npc * mp, n_), BF16),
        compiler_params=_params(),
    )(*pieces, b)


def _gate_out_fwd(ya_in, ob, wa, wb_t, proj, d_, name):
    t_ = ya_in.shape[0]
    goff = 5 * d_ + QKV_WIDTH
    tm, tn = _tile(t_, ROW_TILE, 8), _tile_multi([d_, goff], 512)
    ja, jb = goff // tn, (goff + d_) // tn

    def body(ya_ref, ob_ref, wa_ref, wb_ref, ga_ref, gb_ref, yao_ref, ybo_ref, z_ref):
        wa, wb = wa_ref[...], wb_ref[...]
        for rows in _row_runs(tm):
            y_a = _dot(ya_ref[rows, :], wa, NN)
            y_b = _dot(ob_ref[rows, :], wb, NT)
            yao_ref[rows, :] = y_a.astype(BF16)
            ybo_ref[rows, :] = y_b.astype(BF16)
            z_ref[rows, :] = (_sigmoid(ga_ref[rows, :]) * y_a + _sigmoid(gb_ref[rows, :]) * y_b).astype(BF16)

    tile = pl.BlockSpec((tm, tn), lambda i, j: (i, j))
    return pl.pallas_call(
        body, name=name, grid=(t_ // tm, d_ // tn),
        in_specs=[pl.BlockSpec((tm, d_), lambda i, j: (i, 0)),
                  pl.BlockSpec((tm, ATTN_OUT), lambda i, j: (i, 0)),
                  pl.BlockSpec((d_, tn), lambda i, j: (0, j)),
                  pl.BlockSpec((tn, ATTN_OUT), lambda i, j: (j, 0)),
                  pl.BlockSpec((tm, tn), lambda i, j: (i, ja + j)),
                  pl.BlockSpec((tm, tn), lambda i, j: (i, jb + j))],
        out_specs=[tile, tile, tile],
        out_shape=[jax.ShapeDtypeStruct((t_, d_), BF16)] * 3,
        compiler_params=_params(),
    )(ya_in, ob, wa, wb_t, proj, proj)


def _ffn_mid_bwd(drb, w_out, gate, up, scale, name, dep=None):
    t_, d_ = drb.shape
    f_ = w_out.shape[0]
    tm, tn = _tile(t_, ROW_TILE, 8), _tile(f_, 512)

    def body(dr_ref, w_ref, g_ref, u_ref, dg_ref, du_ref):
        w = w_ref[...]
        for rows in _row_runs(tm):
            da = scale * _dot(dr_ref[rows, :], w, NT)
            g = g_ref[rows, :].astype(F32)
            s = _sigmoid(g)
            dg_ref[rows, :] = (da * u_ref[rows, :].astype(F32) * _dsilu(g, s)).astype(BF16)
            du_ref[rows, :] = (da * g * s).astype(BF16)

    body, dep_specs, deps = _after(body, 4, dep)
    tile = pl.BlockSpec((tm, tn), lambda i, j: (i, j))
    return pl.pallas_call(
        body, name=name, grid=(t_ // tm, f_ // tn),
        in_specs=[pl.BlockSpec((tm, d_), lambda i, j: (i, 0)),
                  pl.BlockSpec((tn, d_), lambda i, j: (j, 0)), tile, tile, *dep_specs],
        out_specs=[tile, tile],
        out_shape=[jax.ShapeDtypeStruct((t_, f_), BF16)] * 2,
        compiler_params=_params(),
    )(drb, w_out, gate, up, *deps)


def _mm_nn_res_lnbwd(a, w, dres, r, g, name, dep=None):
    pieces, t_, kp, npc = _pieces(a)
    d_ = w.shape[1]
    tm, tk = _tile(t_, ROW_TILE, 8), _tile(kp, 512)
    per = kp // tk
    nk = npc * per

    def body(*refs):
        w_ref, dres_ref, r_ref, g_ref, dr_ref, drb_ref, dg_ref, db_ref, acc = refs[npc:]
        i, k = pl.program_id(0), pl.program_id(1)

        @pl.when(k == 0)
        def _():
            acc[...] = jnp.zeros_like(acc)

        @pl.when((i == 0) & (k == 0))
        def _():
            dg_ref[...] = jnp.zeros_like(dg_ref)
            db_ref[...] = jnp.zeros_like(db_ref)

        for p in range(npc):
            def piece_in(p=p):
                acc[...] += _dot(refs[p][...], w_ref[...], NN)

            _for_piece(k, p, per, npc, piece_in)

        @pl.when(k == nk - 1)
        def _():
            def rows_out(rows):
                dy = acc[rows, :] + ALPHA * dres_ref[rows, :]
                xhat, rstd = _ln_stats(r_ref[rows, :])
                dr = _ln_bwd(dy, xhat, rstd, g_ref[...])
                dr_ref[rows, :] = dr
                drb_ref[rows, :] = dr.astype(BF16)
                dg_ref[...] += jnp.sum(dy * xhat, axis=0, keepdims=True)
                db_ref[...] += jnp.sum(dy, axis=0, keepdims=True)

            _for_row_blocks(tm, rows_out)

    body, dep_specs, deps = _after(body, npc + 4, dep)
    row = _once((tm, d_), lambda i, k: (i, 0))
    vec = pl.BlockSpec((1, d_), lambda i, k: (0, 0))
    return pl.pallas_call(
        body, name=name, grid=(t_ // tm, nk),
        in_specs=[pl.BlockSpec((tm, tk), lambda i, k, p=p: (i, jnp.clip(k - p * per, 0, per - 1))) for p in range(npc)]
        + [pl.BlockSpec((tk, d_), lambda i, k: (k, 0)), row, row, vec, *dep_specs],
        out_specs=[row, row, vec, vec],
        out_shape=[jax.ShapeDtypeStruct((t_, d_), F32), jax.ShapeDtypeStruct((t_, d_), BF16),
                   jax.ShapeDtypeStruct((1, d_), F32), jax.ShapeDtypeStruct((1, d_), F32)],
        scratch_shapes=[pltpu.VMEM((tm, d_), F32)],
        compiler_params=_params(),
    )(*pieces, w, dres, r, g, *deps)


def _mm_nn_res(a, w, dres, name, dep=None):
    pieces, t_, kp, npc = _pieces(a)
    d_ = w.shape[1]
    tm, tk = _tile(t_, ROW_TILE, 8), _tile(kp, 512)
    per = kp // tk
    nk = npc * per

    def body(*refs):
        w_ref, dres_ref, o_ref, acc = refs[npc:]
        k = pl.program_id(1)

        @pl.when(k == 0)
        def _():
            acc[...] = jnp.zeros_like(acc)

        for p in range(npc):
            def piece_in(p=p):
                acc[...] += _dot(refs[p][...], w_ref[...], NN)

            _for_piece(k, p, per, npc, piece_in)

        @pl.when(k == nk - 1)
        def _():
            def rows_out(rows):
                o_ref[rows, :] = acc[rows, :] + ALPHA * dres_ref[rows, :]

            _for_row_blocks(tm, rows_out)

    body, dep_specs, deps = _after(body, npc + 2, dep)
    row = _once((tm, d_), lambda i, k: (i, 0))
    return pl.pallas_call(
        body, name=name, grid=(t_ // tm, nk),
        in_specs=[pl.BlockSpec((tm, tk), lambda i, k, p=p: (i, jnp.clip(k - p * per, 0, per - 1))) for p in range(npc)]
        + [pl.BlockSpec((tk, d_), lambda i, k: (k, 0)), row, *dep_specs],
        out_specs=row,
        out_shape=jax.ShapeDtypeStruct((t_, d_), F32),
        scratch_shapes=[pltpu.VMEM((tm, d_), F32)],
        compiler_params=_params(),
    )(*pieces, w, dres, *deps)


def _dz_gate_bwd(drb, w_out, proj, ya, yb, d_, name, dep=None):
    t_ = drb.shape[0]
    goff = 5 * d_ + QKV_WIDTH
    tm, tn = _tile(t_, ROW_TILE, 8), _tile_multi([d_, goff], 512)
    ja, jb = goff // tn, (goff + d_) // tn

    def body(dr_ref, w_ref, ga_ref, gb_ref, ya_ref, yb_ref, dya_ref, dyb_ref, dga_ref, dgb_ref):
        w = w_ref[...]
        for rows in _row_runs(tm):
            dz = _dot(dr_ref[rows, :], w, NT)
            sa, sb = _sigmoid(ga_ref[rows, :]), _sigmoid(gb_ref[rows, :])
            dya_ref[rows, :] = (dz * sa).astype(BF16)
            dyb_ref[rows, :] = (dz * sb).astype(BF16)
            dga_ref[rows, :] = (dz * ya_ref[rows, :].astype(F32) * sa * (1.0 - sa)).astype(BF16)
            dgb_ref[rows, :] = (dz * yb_ref[rows, :].astype(F32) * sb * (1.0 - sb)).astype(BF16)

    body, dep_specs, deps = _after(body, 6, dep)
    tile = pl.BlockSpec((tm, tn), lambda i, j: (i, j))
    return pl.pallas_call(
        body, name=name, grid=(t_ // tm, d_ // tn),
        in_specs=[pl.BlockSpec((tm, d_), lambda i, j: (i, 0)),
                  pl.BlockSpec((tn, d_), lambda i, j: (j, 0)),
                  pl.BlockSpec((tm, tn), lambda i, j: (i, ja + j)),
                  pl.BlockSpec((tm, tn), lambda i, j: (i, jb + j)), tile, tile, *dep_specs],
        out_specs=[tile] * 4,
        out_shape=[jax.ShapeDtypeStruct((t_, d_), BF16)] * 4,
        compiler_params=_params(),
    )(drb, w_out, proj, proj, ya, yb, *deps)


def _ln_loss_bwd(r, target, g, b, name):
    t_, d_ = r.shape
    tm = _tile(t_, 256, 8)

    def body(r_ref, t_ref, g_ref, b_ref, dr_ref, drb_ref, dg_ref, db_ref, loss_ref):
        i = pl.program_id(0)

        @pl.when(i == 0)
        def _():
            dg_ref[...] = jnp.zeros_like(dg_ref)
            db_ref[...] = jnp.zeros_like(db_ref)
            loss_ref[...] = jnp.zeros_like(loss_ref)

        xhat, rstd = _ln_stats(r_ref[...])
        gain = g_ref[...]
        err = xhat * gain + b_ref[...] - t_ref[...]
        loss_ref[...] += (0.5 / d_) * jnp.sum(err * err)
        dy = err * (1.0 / d_)
        dr = _ln_bwd(dy, xhat, rstd, gain)
        dr_ref[...] = dr
        drb_ref[...] = dr.astype(BF16)
        dg_ref[...] += jnp.sum(dy * xhat, axis=0, keepdims=True)
        db_ref[...] += jnp.sum(dy, axis=0, keepdims=True)

    row = pl.BlockSpec((tm, d_), lambda i: (i, 0))
    vec = pl.BlockSpec((1, d_), lambda i: (0, 0))
    return pl.pallas_call(
        body, name=name, grid=(t_ // tm,),
        in_specs=[row, row, vec, vec],
        out_specs=[row, row, vec, vec, pl.BlockSpec((1, HEAD), lambda i: (0, 0))],
        out_shape=[jax.ShapeDtypeStruct((t_, d_), F32), jax.ShapeDtypeStruct((t_, d_), BF16),
                   jax.ShapeDtypeStruct((1, d_), F32), jax.ShapeDtypeStruct((1, d_), F32),
                   jax.ShapeDtypeStruct((1, HEAD), F32)],
        compiler_params=_params(),
    )(r, target, g, b)


def _chunk_scan(x, row, reverse, size):
    s = 1
    while s < CHUNK:
        if reverse:
            x = x + jnp.where(row < CHUNK - s, pltpu.roll(x, size - s, 0), 0.0)
        else:
            x = x + jnp.where(row >= s, pltpu.roll(x, s, 0), 0.0)
        s *= 2
    return x


def _lower_bound(tab):
    return _sigmoid(tab[0:1, :] - tab[1:2, :])


def _tri_mask(reverse):
    r = lax.broadcasted_iota(jnp.int32, (CHUNK, CHUNK), 0)
    c = lax.broadcasted_iota(jnp.int32, (CHUNK, CHUNK), 1)
    return (c >= r) if reverse else (r >= c)


def _hgrn_fwd(proj, lbf, lbb, ng, b_, s_, d_, name):
    h_ = d_ // HEAD
    nc = s_ // CHUNK

    def body(hq_ref, hff_ref, hfb_ref, hi_ref, hog_ref, lbf_ref, lbb_ref, ng_ref, ya_ref, o_ref,
             q_s, k_s, cum_s, o_s):
        row = lax.broadcasted_iota(jnp.int32, (s_, HEAD), 0) % CHUNK
        hq = hq_ref[...]
        q_s[...] = hq * _sigmoid(hq)
        o_s[...] = jnp.zeros_like(o_s)
        for reverse, hf_ref, lb_ref in ((False, hff_ref, lbf_ref), (True, hfb_ref, lbb_ref)):
            lb = _lower_bound(lb_ref[...])
            f = lb + (1.0 - lb) * _sigmoid(hf_ref[...])
            k_s[...] = 1.0 - f
            cum_s[...] = _chunk_scan(jnp.log(f), row, reverse, s_)
            mask = _tri_mask(reverse)

            def step(n, st, reverse=reverse, mask=mask):
                idx = (nc - 1 - n) if reverse else n
                sl = pl.ds(pl.multiple_of(idx * CHUNK, CHUNK), CHUNK)
                cm = cum_s[sl, :]
                tot = cm[0:1, :] if reverse else cm[CHUNK - 1:CHUNK, :]
                qc, kc = q_s[sl, :], k_s[sl, :]
                vb = hi_ref[sl, :].astype(BF16)
                qd = (qc * jnp.exp(cm)).astype(BF16)
                kd = (kc * jnp.exp(-cm)).astype(BF16)
                ke = (kc * jnp.exp(tot - cm)).astype(BF16)
                a = jnp.where(mask, _dot(qd, kd, NT), 0.0)
                o_s[sl, :] += _dot(a.astype(BF16), vb, NN) + _dot(qd, st.astype(BF16), NT)
                return st * jnp.exp(tot) + _dot(vb, ke, TN)

            lax.fori_loop(0, nc, step, jnp.zeros((HEAD, HEAD), F32))
        o = o_s[...]
        o_ref[...] = o
        nrm = o * lax.rsqrt(jnp.mean(o * o, axis=-1, keepdims=True) + LN_EPS)
        hog = hog_ref[...]
        ya_ref[...] = (nrm * ng_ref[...] * hog * _sigmoid(hog)).astype(BF16)

    def col(part):
        return pl.BlockSpec((s_, HEAD), lambda h, b, part=part: (b, part * h_ + h))

    tab = pl.BlockSpec((2, HEAD), lambda h, b: (0, h))
    out = pl.BlockSpec((s_, HEAD), lambda h, b: (b, h))
    return pl.pallas_call(
        body, name=name, grid=(h_, b_),
        in_specs=[col(0), col(1), col(2), col(3), col(4), tab, tab,
                  pl.BlockSpec((1, HEAD), lambda h, b: (0, h))],
        out_specs=[out, out],
        out_shape=[jax.ShapeDtypeStruct((b_ * s_, d_), BF16), jax.ShapeDtypeStruct((b_ * s_, d_), F32)],
        scratch_shapes=[pltpu.VMEM((s_, HEAD), F32)] * 4,
        compiler_params=_params(),
    )(proj, proj, proj, proj, proj, lbf, lbb, ng)


def _hgrn_bwd(proj, lbf, lbb, ng, o_sum, dya, b_, s_, d_, name):
    h_ = d_ // HEAD
    nc = s_ // CHUNK

    def body(hq_ref, hff_ref, hfb_ref, hi_ref, hog_ref, lbf_ref, lbb_ref, ng_ref, o_ref, dya_ref,
             dhq_ref, dhff_ref, dhfb_ref, dhi_ref, dhog_ref, dng_ref, dlbf_ref, dlbb_ref,
             q_s, k_s, cum_s, do_s, dq_s, dv_s, db_s, dk_s, st_s):
        b = pl.program_id(1)

        @pl.when(b == 0)
        def _():
            dng_ref[...] = jnp.zeros_like(dng_ref)
            dlbf_ref[...] = jnp.zeros_like(dlbf_ref)
            dlbb_ref[...] = jnp.zeros_like(dlbb_ref)

        row = lax.broadcasted_iota(jnp.int32, (s_, HEAD), 0) % CHUNK
        crow = lax.broadcasted_iota(jnp.int32, (CHUNK, HEAD), 0)
        hq = hq_ref[...]
        sq = _sigmoid(hq)
        q_s[...] = hq * sq
        o = o_ref[...]
        rinv = lax.rsqrt(jnp.mean(o * o, axis=-1, keepdims=True) + LN_EPS)
        nrm = o * rinv
        hog = hog_ref[...]
        so = _sigmoid(hog)
        gain = ng_ref[...]
        dy = dya_ref[...]
        dhog_ref[...] = (dy * nrm * gain * _dsilu(hog, so)).astype(BF16)
        dng_ref[...] += jnp.sum(dy * nrm * hog * so, axis=0, keepdims=True)
        dn = dy * gain * hog * so
        do_s[...] = rinv * (dn - nrm * jnp.mean(dn * nrm, axis=-1, keepdims=True))
        dq_s[...] = jnp.zeros_like(dq_s)
        dv_s[...] = jnp.zeros_like(dv_s)

        for reverse, hf_ref, lb_ref, dhf_ref, dlb_ref in (
                (False, hff_ref, lbf_ref, dhff_ref, dlbf_ref), (True, hfb_ref, lbb_ref, dhfb_ref, dlbb_ref)):
            tab = lb_ref[...]
            lb = _lower_bound(tab)
            sf = _sigmoid(hf_ref[...])
            f = lb + (1.0 - lb) * sf
            k_s[...] = 1.0 - f
            cum_s[...] = _chunk_scan(jnp.log(f), row, reverse, s_)
            mask = _tri_mask(reverse)
            last = 0 if reverse else CHUNK - 1

            def chunk(idx, reverse=reverse):
                sl = pl.ds(pl.multiple_of(idx * CHUNK, CHUNK), CHUNK)
                cm = cum_s[sl, :]
                tot = cm[0:1, :] if reverse else cm[CHUNK - 1:CHUNK, :]
                return sl, cm, tot

            def fstep(n, st, reverse=reverse, chunk=chunk):
                idx = (nc - 1 - n) if reverse else n
                sl, cm, tot = chunk(idx)
                st_s[idx] = st
                ke = (k_s[sl, :] * jnp.exp(tot - cm)).astype(BF16)
                return st * jnp.exp(tot) + _dot(hi_ref[sl, :].astype(BF16), ke, TN)

            lax.fori_loop(0, nc, fstep, jnp.zeros((HEAD, HEAD), F32))

            def bstep(n, dst, reverse=reverse, chunk=chunk, mask=mask, last=last):
                idx = n if reverse else (nc - 1 - n)
                sl, cm, tot = chunk(idx)
                eb, enb, ee, dec = jnp.exp(cm), jnp.exp(-cm), jnp.exp(tot - cm), jnp.exp(tot)
                qc, kc = q_s[sl, :], k_s[sl, :]
                qd, kd, ke = qc * eb, kc * enb, kc * ee
                qdb, kdb, keb = qd.astype(BF16), kd.astype(BF16), ke.astype(BF16)
                vb = hi_ref[sl, :].astype(BF16)
                dob = do_s[sl, :].astype(BF16)
                st0 = st_s[idx]
                dstb = dst.astype(BF16)
                a = jnp.where(mask, _dot(qdb, kdb, NT), 0.0).astype(BF16)
                da = jnp.where(mask, _dot(dob, vb, NT), 0.0).astype(BF16)
                dqd = _dot(da, kdb, NN) + _dot(dob, st0.astype(BF16), NN)
                dkd = _dot(da, qdb, TN)
                dv = _dot(a, dob, TN) + _dot(keb, dstb, NT)
                dke = _dot(vb, dstb, NN)
                ddec = jnp.sum(dst * st0, axis=0, keepdims=True)
                dtot = jnp.sum(dke * ke, axis=0, keepdims=True) + ddec * dec
                db = dqd * qd - dkd * kd - dke * ke
                db_s[sl, :] = db + jnp.where(crow == last, dtot, 0.0)
                dk_s[sl, :] = dkd * enb + dke * ee
                dq_s[sl, :] += dqd * eb
                dv_s[sl, :] += dv
                return dst * dec + _dot(dob, qdb, TN)

            lax.fori_loop(0, nc, bstep, jnp.zeros((HEAD, HEAD), F32))
            dlogf = _chunk_scan(db_s[...], row, not reverse, s_)
            df = dlogf / f - dk_s[...]
            dhf_ref[...] = (df * (1.0 - lb) * sf * (1.0 - sf)).astype(BF16)
            dlb = jnp.sum(df * (1.0 - sf), axis=0, keepdims=True) * lb * (1.0 - lb)
            dlb_ref[0:1, :] += dlb
            dlb_ref[1:2, :] -= dlb

        dhq_ref[...] = (dq_s[...] * _dsilu(hq, sq)).astype(BF16)
        dhi_ref[...] = dv_s[...].astype(BF16)

    def col(part):
        return pl.BlockSpec((s_, HEAD), lambda h, b, part=part: (b, part * h_ + h))

    tab = pl.BlockSpec((2, HEAD), lambda h, b: (0, h))
    vec = pl.BlockSpec((1, HEAD), lambda h, b: (0, h))
    blk = pl.BlockSpec((s_, HEAD), lambda h, b: (b, h))
    act = jax.ShapeDtypeStruct((b_ * s_, d_), BF16)
    return pl.pallas_call(
        body, name=name, grid=(h_, b_),
        in_specs=[col(0), col(1), col(2), col(3), col(4), tab, tab, vec, blk, blk],
        out_specs=[blk] * 5 + [vec, tab, tab],
        out_shape=[act] * 5 + [jax.ShapeDtypeStruct((1, d_), F32), jax.ShapeDtypeStruct((2, d_), F32),
                               jax.ShapeDtypeStruct((2, d_), F32)],
        scratch_shapes=[pltpu.VMEM((s_, HEAD), F32)] * 8 + [pltpu.VMEM((nc, HEAD, HEAD), F32)],
        compiler_params=_params(),
    )(proj, proj, proj, proj, proj, lbf, lbb, ng, o_sum, dya)


FWD_BLOCK = 128
BWD_BLOCK = 128


def _block_scan(x, row, reverse, size, blk):
    s = 1
    while s < blk:
        if reverse:
            x = x + jnp.where(row < blk - s, pltpu.roll(x, size - s, 0), 0.0)
        else:
            x = x + jnp.where(row >= s, pltpu.roll(x, s, 0), 0.0)
        s *= 2
    return x


def _block_exps(l, reverse):
    blk = l.shape[0]
    half = blk // 2
    first = lax.broadcasted_iota(jnp.int32, (blk, HEAD), 0) < half
    q1, q3 = half // 2, half + half // 2
    if reverse:
        rho1, rho2, lh, ltot = l[q1:q1 + 1], l[q3:q3 + 1], l[half:half + 1], l[0:1]
    else:
        rho1, rho2, lh, ltot = l[q1 - 1:q1], l[q3 - 1:q3], l[half - 1:half], l[blk - 1:blk]
    ref = jnp.where(first, rho1, rho2)
    query_half = first if reverse else jnp.logical_not(first)
    e2 = jnp.where(query_half, jnp.exp(jnp.minimum(l - lh, 0.0)), 0.0)
    e1 = jnp.where(query_half, 0.0, jnp.exp(jnp.minimum(lh - l, 0.0)))
    return (jnp.exp(l - ref), jnp.exp(ref - l), e2, e1, jnp.exp(l), jnp.exp(ltot - l),
            jnp.exp(ltot), jnp.exp(lh), jnp.exp(ltot - lh))


def _half_mask(reverse, blk):
    r = lax.broadcasted_iota(jnp.int32, (blk, blk), 0)
    c = lax.broadcasted_iota(jnp.int32, (blk, blk), 1)
    same = (r < blk // 2) == (c < blk // 2)
    return same & ((c >= r) if reverse else (r >= c))


def _hgrn_fwd(proj, lbf, lbb, ng, b_, s_, d_, name):
    h_ = d_ // HEAD
    BLOCK = min(FWD_BLOCK, s_)
    nb = s_ // BLOCK

    def body(hq_ref, hff_ref, hfb_ref, hi_ref, hog_ref, lbf_ref, lbb_ref, ng_ref, ya_ref, o_ref,
             q_s, k_s, l_s, of_s, oi_s, qd_s, u_s, st_s, dec_s):
        row = lax.broadcasted_iota(jnp.int32, (s_, HEAD), 0) % BLOCK
        hq = hq_ref[...]
        q_s[...] = hq * _sigmoid(hq)
        for reverse, hf_ref, lb_ref in ((False, hff_ref, lbf_ref), (True, hfb_ref, lbb_ref)):
            lb = _lower_bound(lb_ref[...])
            f = lb + (1.0 - lb) * _sigmoid(hf_ref[...])
            k_s[...] = 1.0 - f
            l_s[...] = _block_scan(jnp.log(f), row, reverse, s_, BLOCK)
            mask = _half_mask(reverse, BLOCK)

            def inside(n, carry, reverse=reverse, mask=mask):
                sl = pl.ds(pl.multiple_of(n * BLOCK, BLOCK), BLOCK)
                eq, ek, e2, e1, el, ee, dec, _, _ = _block_exps(l_s[sl, :], reverse)
                qc, kc = q_s[sl, :], k_s[sl, :]
                vb = hi_ref[sl, :].astype(BF16)
                a = jnp.where(mask, _dot((qc * eq).astype(BF16), (kc * ek).astype(BF16), NT), 0.0)
                a = a + _dot((qc * e2).astype(BF16), (kc * e1).astype(BF16), NT)
                oi_s[sl, :] = _dot(a.astype(BF16), vb, NN)
                qd_s[sl, :] = (qc * el).astype(BF16)
                u_s[n] = _dot(vb, (kc * ee).astype(BF16), TN)
                dec_s[n] = jnp.broadcast_to(dec, (8, HEAD))
                return carry

            lax.fori_loop(0, nb, inside, 0, unroll=4)

            def carry_state(n, st, reverse=reverse):
                idx = (nb - 1 - n) if reverse else n
                st_s[idx] = st.astype(BF16)
                return st * dec_s[idx][0:1, :] + u_s[idx]

            lax.fori_loop(0, nb, carry_state, jnp.zeros((HEAD, HEAD), F32))

            def across(n, carry, reverse=reverse):
                sl = pl.ds(pl.multiple_of(n * BLOCK, BLOCK), BLOCK)
                o_dir = oi_s[sl, :] + _dot(qd_s[sl, :], st_s[n], NT)
                if not reverse:
                    of_s[sl, :] = o_dir
                else:
                    o = of_s[sl, :] + o_dir
                    o_ref[sl, :] = o
                    nrm = o * lax.rsqrt(jnp.mean(o * o, axis=-1, keepdims=True) + LN_EPS)
                    hog = hog_ref[sl, :]
                    ya_ref[sl, :] = (nrm * ng_ref[...] * hog * _sigmoid(hog)).astype(BF16)
                return carry

            lax.fori_loop(0, nb, across, 0, unroll=8)

    def col(part):
        return pl.BlockSpec((s_, HEAD), lambda h, b, part=part: (b, part * h_ + h))

    tab = pl.BlockSpec((2, HEAD), lambda h, b: (0, h))
    out = pl.BlockSpec((s_, HEAD), lambda h, b: (b, h))
    return pl.pallas_call(
        body, name=name, grid=(h_, b_),
        in_specs=[col(0), col(1), col(2), col(3), col(4), tab, tab,
                  pl.BlockSpec((1, HEAD), lambda h, b: (0, h))],
        out_specs=[out, out],
        out_shape=[jax.ShapeDtypeStruct((b_ * s_, d_), BF16), jax.ShapeDtypeStruct((b_ * s_, d_), F32)],
        scratch_shapes=[pltpu.VMEM((s_, HEAD), F32)] * 5 + [
            pltpu.VMEM((s_, HEAD), BF16), pltpu.VMEM((nb, HEAD, HEAD), F32), pltpu.VMEM((nb, HEAD, HEAD), BF16),
            pltpu.VMEM((nb, 8, HEAD), F32)],
        compiler_params=_params(),
    )(proj, proj, proj, proj, proj, lbf, lbb, ng)


def _hgrn_bwd(proj, lbf, lbb, ng, o_sum, dya, b_, s_, d_, name):
    h_ = d_ // HEAD
    BLOCK = min(BWD_BLOCK, s_)
    nb = s_ // BLOCK

    def body(hq_ref, hff_ref, hfb_ref, hi_ref, hog_ref, lbf_ref, lbb_ref, ng_ref, o_ref, dya_ref,
             dhq_ref, dhff_ref, dhfb_ref, dhi_ref, dhog_ref, dng_ref, dlbf_ref, dlbb_ref,
             q_s, k_s, l_s, do_s, dq_s, dv_s, dl_s, dk_s, u_s, w_s, st_s, dst_s, dec_s):
        b = pl.program_id(1)

        @pl.when(b == 0)
        def _():
            dng_ref[...] = jnp.zeros_like(dng_ref)
            dlbf_ref[...] = jnp.zeros_like(dlbf_ref)
            dlbb_ref[...] = jnp.zeros_like(dlbb_ref)

        row = lax.broadcasted_iota(jnp.int32, (s_, HEAD), 0) % BLOCK
        brow = lax.broadcasted_iota(jnp.int32, (BLOCK, HEAD), 0)
        hq = hq_ref[...]
        sq = _sigmoid(hq)
        q_s[...] = hq * sq
        o = o_ref[...]
        rinv = lax.rsqrt(jnp.mean(o * o, axis=-1, keepdims=True) + LN_EPS)
        nrm = o * rinv
        hog = hog_ref[...]
        so = _sigmoid(hog)
        gain = ng_ref[...]
        dy = dya_ref[...]
        dhog_ref[...] = (dy * nrm * gain * _dsilu(hog, so)).astype(BF16)
        dng_ref[...] += jnp.sum(dy * nrm * hog * so, axis=0, keepdims=True)
        dn = dy * gain * hog * so
        do_s[...] = rinv * (dn - nrm * jnp.mean(dn * nrm, axis=-1, keepdims=True))

        for reverse, hf_ref, lb_ref, dhf_ref, dlb_ref in (
                (False, hff_ref, lbf_ref, dhff_ref, dlbf_ref), (True, hfb_ref, lbb_ref, dhfb_ref, dlbb_ref)):
            lb = _lower_bound(lb_ref[...])
            sf = _sigmoid(hf_ref[...])
            f = lb + (1.0 - lb) * sf
            k_s[...] = 1.0 - f
            l_s[...] = _block_scan(jnp.log(f), row, reverse, s_, BLOCK)
            mask = _half_mask(reverse, BLOCK)
            total_row = 0 if reverse else BLOCK - 1
            key_end = BLOCK // 2 if reverse else BLOCK // 2 - 1

            def prepare(n, carry, reverse=reverse):
                sl = pl.ds(pl.multiple_of(n * BLOCK, BLOCK), BLOCK)
                _, _, _, _, el, ee, dec, _, _ = _block_exps(l_s[sl, :], reverse)
                vb = hi_ref[sl, :].astype(BF16)
                u_s[n] = _dot(vb, (k_s[sl, :] * ee).astype(BF16), TN)
                w_s[n] = _dot(do_s[sl, :].astype(BF16), (q_s[sl, :] * el).astype(BF16), TN)
                dec_s[n] = jnp.broadcast_to(dec, (8, HEAD))
                return carry

            lax.fori_loop(0, nb, prepare, 0, unroll=4)

            def carry_state(n, st, reverse=reverse):
                idx = (nb - 1 - n) if reverse else n
                st_s[idx] = st
                return st * dec_s[idx][0:1, :] + u_s[idx]

            lax.fori_loop(0, nb, carry_state, jnp.zeros((HEAD, HEAD), F32))

            def carry_grad(n, dst, reverse=reverse):
                idx = n if reverse else (nb - 1 - n)
                dst_s[idx] = dst
                return dst * dec_s[idx][0:1, :] + w_s[idx]

            lax.fori_loop(0, nb, carry_grad, jnp.zeros((HEAD, HEAD), F32))

            def inside(n, carry, reverse=reverse, mask=mask, total_row=total_row, key_end=key_end):
                sl = pl.ds(pl.multiple_of(n * BLOCK, BLOCK), BLOCK)
                eq, ek, e2, e1, el, ee, dec, dec_key, dec_query = _block_exps(l_s[sl, :], reverse)
                qc, kc = q_s[sl, :], k_s[sl, :]
                vb = hi_ref[sl, :].astype(BF16)
                dob = do_s[sl, :].astype(BF16)
                qt, kt, q2, k1 = ((qc * eq).astype(BF16), (kc * ek).astype(BF16),
                                  (qc * e2).astype(BF16), (kc * e1).astype(BF16))
                kend = kc * ee
                st0, dst1 = st_s[n], dst_s[n]
                dstb = dst1.astype(BF16)
                a = jnp.where(mask, _dot(qt, kt, NT), 0.0) + _dot(q2, k1, NT)
                da = _dot(dob, vb, NT)
                dab = da.astype(BF16)
                dad = jnp.where(mask, da, 0.0).astype(BF16)
                dqt, dkt = _dot(dad, kt, NN), _dot(dad, qt, TN)
                dq2, dk1 = _dot(dab, k1, NN), _dot(dab, q2, TN)
                dqd = _dot(dob, st0.astype(BF16), NN)
                dke = _dot(vb, dstb, NN)
                dv = _dot(a.astype(BF16), dob, TN) + _dot(kend.astype(BF16), dstb, NT)
                dq = dqt * eq + dq2 * e2 + dqd * el
                dk = dkt * ek + dk1 * e1 + dke * ee
                dtot = jnp.sum(dke * kend, axis=0, keepdims=True) + jnp.sum(dst1 * st0, axis=0, keepdims=True) * dec
                st_mid = st0 * dec_key + _dot(vb, k1, TN)
                dst_mid = dst1 * dec_query + _dot(dob, q2, TN)
                dmid = jnp.sum(dst_mid * st_mid, axis=0, keepdims=True)
                dl_s[sl, :] = (qc * dq - kc * dk + jnp.where(brow == total_row, dtot, 0.0)
                               + jnp.where(brow == key_end, dmid, 0.0))
                dk_s[sl, :] = dk
                if not reverse:
                    dq_s[sl, :] = dq
                    dv_s[sl, :] = dv
                else:
                    hqc = hq_ref[sl, :]
                    dhq_ref[sl, :] = ((dq_s[sl, :] + dq) * _dsilu(hqc, _sigmoid(hqc))).astype(BF16)
                    dhi_ref[sl, :] = (dv_s[sl, :] + dv).astype(BF16)
                return carry

            lax.fori_loop(0, nb, inside, 0, unroll=2)
            dlogf = _block_scan(dl_s[...], row % (BLOCK // 2), not reverse, s_, BLOCK // 2)
            df = dlogf / f - dk_s[...]
            dhf_ref[...] = (df * (1.0 - lb) * sf * (1.0 - sf)).astype(BF16)
            dlb = jnp.sum(df * (1.0 - sf), axis=0, keepdims=True) * lb * (1.0 - lb)
            dlb_ref[0:1, :] += dlb
            dlb_ref[1:2, :] -= dlb

    def col(part):
        return pl.BlockSpec((s_, HEAD), lambda h, b, part=part: (b, part * h_ + h))

    tab = pl.BlockSpec((2, HEAD), lambda h, b: (0, h))
    vec = pl.BlockSpec((1, HEAD), lambda h, b: (0, h))
    blk = pl.BlockSpec((s_, HEAD), lambda h, b: (b, h))
    act = jax.ShapeDtypeStruct((b_ * s_, d_), BF16)
    state = pltpu.VMEM((nb, HEAD, HEAD), F32)
    return pl.pallas_call(
        body, name=name, grid=(h_, b_),
        in_specs=[col(0), col(1), col(2), col(3), col(4), tab, tab, vec, blk, blk],
        out_specs=[blk] * 5 + [vec, tab, tab],
        out_shape=[act] * 5 + [jax.ShapeDtypeStruct((1, d_), F32), jax.ShapeDtypeStruct((2, d_), F32),
                               jax.ShapeDtypeStruct((2, d_), F32)],
        scratch_shapes=[pltpu.VMEM((s_, HEAD), F32)] * 8 + [state] * 4 + [pltpu.VMEM((nb, 8, HEAD), F32)],
        compiler_params=_params(),
    )(proj, proj, proj, proj, proj, lbf, lbb, ng, o_sum, dya)


def _rope_tables(s_):
    half = ROPE_DIM // 2
    inv_freq = ROPE_THETA ** (-jnp.arange(0, ROPE_DIM, 2, dtype=F32) / ROPE_DIM)
    ang = jnp.arange(s_, dtype=F32)[:, None] * inv_freq
    cos, sin = jnp.cos(ang), jnp.sin(ang)
    zeros = jnp.zeros((s_, HEAD - ROPE_DIM), F32)
    zh = jnp.zeros((s_, half), F32)
    c = jnp.concatenate([cos, cos, jnp.ones((s_, HEAD - ROPE_DIM), F32)], axis=1)
    s1 = jnp.concatenate([-sin, zh, zeros], axis=1)
    s2 = jnp.concatenate([zh, sin, zeros], axis=1)
    return c, s1, s2


def _rope(t, c, s1, s2):
    half = ROPE_DIM // 2
    return t * c + pltpu.roll(t, HEAD - half, 1) * s1 + pltpu.roll(t, half, 1) * s2


def _rope_bwd(dt, c, s1, s2):
    half = ROPE_DIM // 2
    return dt * c + pltpu.roll(dt * s1, half, 1) + pltpu.roll(dt * s2, HEAD - half, 1)


def _window_mask(r0, qb, wk, seg):
    row = lax.broadcasted_iota(jnp.int32, (qb, wk), 0)
    col = lax.broadcasted_iota(jnp.int32, (qb, wk), 1)
    kj = r0 - ATTN_HALF + col
    return (col - row >= 0) & (col - row <= 2 * ATTN_HALF) & (kj >= 0) & (kj < seg)


def _attn_fwd(proj, tabs, b_, s_, col0, dil, name):
    seg = s_ // dil
    qb = min(128, seg)
    nq, wk = seg // qb, qb + 2 * ATTN_HALF
    scale = HEAD ** -0.5

    def body(q_ref, k_ref, v_ref, c_ref, s1_ref, s2_ref, o_ref, lse_ref, q_s, k_s, v_s):
        k_s[...] = jnp.zeros_like(k_s)
        v_s[...] = jnp.zeros_like(v_s)

        def residue(r, carry):
            cls = pl.ds(r, seg, stride=dil)
            c, s1, s2 = c_ref[cls, :], s1_ref[cls, :], s2_ref[cls, :]
            q_s[...] = _rope(q_ref[cls, :], c, s1, s2).astype(BF16)
            k_s[ATTN_HALF:ATTN_HALF + seg, :] = _rope(k_ref[cls, :], c, s1, s2).astype(BF16)
            v_s[ATTN_HALF:ATTN_HALF + seg, :] = v_ref[cls, :].astype(BF16)

            def step(i, carry):
                r0 = pl.multiple_of(i * qb, qb)
                sc = _dot(q_s[pl.ds(r0, qb), :], k_s[pl.ds(r0, wk), :], NT) * scale
                sc = jnp.where(_window_mask(r0, qb, wk, seg), sc, NEG_INF)
                m = jnp.max(sc, axis=-1, keepdims=True)
                p = jnp.exp(sc - m)
                den = jnp.sum(p, axis=-1, keepdims=True)
                rows = pl.ds(r + r0 * dil, qb, stride=dil)
                o_ref[rows, :] = _dot(p.astype(BF16), v_s[pl.ds(r0, wk), :], NN) / den
                lse_ref[rows, :] = jnp.broadcast_to(m + jnp.log(den), (qb, HEAD))
                return carry

            return lax.fori_loop(0, nq, step, carry)

        lax.fori_loop(0, dil, residue, 0)

    def col(part):
        return pl.BlockSpec((s_, HEAD), lambda b, h, part=part: (b, col0 + part * ATTN_HEADS + h))

    tab = pl.BlockSpec((s_, HEAD), lambda b, h: (0, 0))
    out = pl.BlockSpec((s_, HEAD), lambda b, h: (b, h))
    shape = jax.ShapeDtypeStruct((b_ * s_, ATTN_OUT), F32)
    return pl.pallas_call(
        body, name=name, grid=(b_, ATTN_HEADS),
        in_specs=[col(0), col(1), col(2), tab, tab, tab],
        out_specs=[out, out],
        out_shape=[shape, shape],
        scratch_shapes=[pltpu.VMEM((seg, HEAD), BF16), pltpu.VMEM((seg + 2 * ATTN_HALF, HEAD), BF16),
                        pltpu.VMEM((seg + 2 * ATTN_HALF, HEAD), BF16)],
        compiler_params=_params(),
    )(proj, proj, proj, *tabs)


def _attn_bwd(proj, tabs, dog, cg, lse, b_, s_, col0, dil, name):
    seg = s_ // dil
    qb = min(128, seg)
    nq, wk = seg // qb, qb + 2 * ATTN_HALF
    scale = HEAD ** -0.5

    def body(q_ref, k_ref, v_ref, c_ref, s1_ref, s2_ref, do_ref, cg_ref, lse_ref, dq_ref, dk_ref, dv_ref,
             q_s, k_s, v_s, do_s, cl_s, dk_s, dv_s):
        k_s[...] = jnp.zeros_like(k_s)
        v_s[...] = jnp.zeros_like(v_s)

        def residue(r, carry):
            cls = pl.ds(r, seg, stride=dil)
            c, s1, s2 = c_ref[cls, :], s1_ref[cls, :], s2_ref[cls, :]
            q_s[...] = _rope(q_ref[cls, :], c, s1, s2).astype(BF16)
            k_s[ATTN_HALF:ATTN_HALF + seg, :] = _rope(k_ref[cls, :], c, s1, s2).astype(BF16)
            v_s[ATTN_HALF:ATTN_HALF + seg, :] = v_ref[cls, :].astype(BF16)
            do_s[...] = do_ref[cls, :].astype(BF16)
            cl_s[0] = cg_ref[cls, :]
            cl_s[1] = lse_ref[cls, :]
            dk_s[...] = jnp.zeros_like(dk_s)
            dv_s[...] = jnp.zeros_like(dv_s)

            def step(i, carry):
                r0 = pl.multiple_of(i * qb, qb)
                rows, win = pl.ds(r0, qb), pl.ds(r0, wk)
                qc, kw, vw = q_s[rows, :], k_s[win, :], v_s[win, :]
                sc = _dot(qc, kw, NT) * scale
                p = jnp.where(_window_mask(r0, qb, wk, seg), jnp.exp(sc - cl_s[1, rows, 0:1]), 0.0)
                dob = do_s[rows, :]
                dp = _dot(dob, vw, NT)
                ds = (p * (dp + cl_s[0, rows, 0:1]) * scale).astype(BF16)
                out = pl.ds(r + r0 * dil, qb, stride=dil)
                dq_ref[out, :] = _rope_bwd(_dot(ds, kw, NN), c_ref[out, :], s1_ref[out, :], s2_ref[out, :])
                dk_s[win, :] += _dot(ds, qc, TN)
                dv_s[win, :] += _dot(p.astype(BF16), dob, TN)
                return carry

            carry = lax.fori_loop(0, nq, step, carry)
            dk_ref[cls, :] = _rope_bwd(dk_s[ATTN_HALF:ATTN_HALF + seg, :], c, s1, s2)
            dv_ref[cls, :] = dv_s[ATTN_HALF:ATTN_HALF + seg, :]
            return carry

        lax.fori_loop(0, dil, residue, 0)

    def col(part):
        return pl.BlockSpec((s_, HEAD), lambda b, h, part=part: (b, col0 + part * ATTN_HEADS + h))

    tab = pl.BlockSpec((s_, HEAD), lambda b, h: (0, 0))
    out = pl.BlockSpec((s_, HEAD), lambda b, h: (b, h))
    shape = jax.ShapeDtypeStruct((b_ * s_, ATTN_OUT), F32)
    pad = seg + 2 * ATTN_HALF
    return pl.pallas_call(
        body, name=name, grid=(b_, ATTN_HEADS),
        in_specs=[col(0), col(1), col(2), tab, tab, tab, out, out, out],
        out_specs=[out, out, out],
        out_shape=[shape, shape, shape],
        scratch_shapes=[pltpu.VMEM((seg, HEAD), BF16), pltpu.VMEM((pad, HEAD), BF16), pltpu.VMEM((pad, HEAD), BF16),
                        pltpu.VMEM((seg, HEAD), BF16), pltpu.VMEM((2, seg, HEAD), F32),
                        pltpu.VMEM((pad, HEAD), F32), pltpu.VMEM((pad, HEAD), F32)],
        compiler_params=_params(),
    )(proj, proj, proj, *tabs, dog, cg, lse)


def _group_weights(lses):
    m = jnp.maximum(jnp.maximum(lses[0], lses[1]), lses[2])
    es = [jnp.exp(l - m) for l in lses]
    den = es[0] + es[1] + es[2]
    return [e / den for e in es]


def _combine_fwd(outs, lses, name):
    t_, w_ = outs[0].shape
    tm = _tile(t_, 512, 8)
    ng = len(outs)

    def body(*refs):
        ws = _group_weights([r[...] for r in refs[ng:2 * ng]])
        acc = ws[0] * refs[0][...]
        for g in range(1, ng):
            acc = acc + ws[g] * refs[g][...]
        refs[2 * ng][...] = acc.astype(BF16)

    row = pl.BlockSpec((tm, w_), lambda i: (i, 0))
    return pl.pallas_call(
        body, name=name, grid=(t_ // tm,), in_specs=[row] * (2 * ng), out_specs=row,
        out_shape=jax.ShapeDtypeStruct((t_, w_), BF16), compiler_params=_params(),
    )(*outs, *lses)


def _combine_bwd(dob, outs, lses, name):
    t_, w_ = outs[0].shape
    tm = _tile(t_, 512, 8)
    ng = len(outs)

    def body(*refs):
        do = refs[0][...]
        os_ = [r[...] for r in refs[1:1 + ng]]
        ws = _group_weights([r[...] for r in refs[1 + ng:1 + 2 * ng]])
        o = ws[0] * os_[0]
        for g in range(1, ng):
            o = o + ws[g] * os_[g]
        prod = do * o
        heads = [jnp.broadcast_to(jnp.sum(prod[:, h * HEAD:(h + 1) * HEAD], axis=-1, keepdims=True), (tm, HEAD))
                 for h in range(w_ // HEAD)]
        tot = jnp.concatenate(heads, axis=1)
        for g in range(ng):
            refs[1 + 2 * ng + g][...] = ws[g] * do
            refs[1 + 3 * ng + g][...] = -ws[g] * tot

    row = pl.BlockSpec((tm, w_), lambda i: (i, 0))
    shape = jax.ShapeDtypeStruct((t_, w_), F32)
    res = pl.pallas_call(
        body, name=name, grid=(t_ // tm,), in_specs=[row] * (1 + 2 * ng), out_specs=[row] * (2 * ng),
        out_shape=[shape] * (2 * ng), compiler_params=_params(),
    )(dob, *outs, *lses)
    return res[:ng], res[ng:]


def _adam_update(w, g, m, v):
    m = ADAM_B1 * m + (1.0 - ADAM_B1) * g
    v = ADAM_B2 * v + (1.0 - ADAM_B2) * (g * g)
    m_hat = m / (1.0 - ADAM_B1 ** ADAM_STEP)
    v_hat = v / (1.0 - ADAM_B2 ** ADAM_STEP)
    return -ADAM_LR * (m_hat / (jnp.sqrt(v_hat) + ADAM_EPS) + ADAM_WD * w), m, v


def _adam(w, g, m, v, name):
    r_, c_ = w.shape
    tr = _tile(r_, 256, 8)

    def body(w_ref, g_ref, m_ref, v_ref, d_ref, mo_ref, vo_ref):
        d_ref[...], mo_ref[...], vo_ref[...] = _adam_update(w_ref[...], g_ref[...], m_ref[...], v_ref[...])

    blk = pl.BlockSpec((tr, c_), lambda i: (i, 0))
    shape = jax.ShapeDtypeStruct((r_, c_), F32)
    return pl.pallas_call(
        body, name=name, grid=(r_ // tr,), in_specs=[blk] * 4, out_specs=[blk] * 3,
        out_shape=[shape] * 3, compiler_params=_params(),
    )(w, g, m, v)


def _sum_partials(recv, name):
    n_, r_, c_ = recv.shape
    tr = _tile(r_, 128, 16)

    def body(p_ref, o_ref):
        acc = p_ref[0].astype(F32)
        for i in range(1, n_):
            acc = acc + p_ref[i].astype(F32)
        o_ref[...] = acc

    return pl.pallas_call(
        body, name=name, grid=(r_ // tr,),
        in_specs=[pl.BlockSpec((n_, tr, c_), lambda i: (0, i, 0))],
        out_specs=pl.BlockSpec((tr, c_), lambda i: (i, 0)),
        out_shape=jax.ShapeDtypeStruct((r_, c_), F32), compiler_params=_params(),
    )(recv)


def _small_sum_adam(parts, w, m, v, name):
    n_, r_, c_ = parts.shape

    def body(p_ref, w_ref, m_ref, v_ref, g_ref, d_ref, mo_ref, vo_ref):
        g = p_ref[0]
        for i in range(1, n_):
            g = g + p_ref[i]
        g_ref[...] = g
        d_ref[...], mo_ref[...], vo_ref[...] = _adam_update(w_ref[...], g, m_ref[...], v_ref[...])

    shape = jax.ShapeDtypeStruct((r_, c_), F32)
    return pl.pallas_call(body, name=name, out_shape=[shape] * 4, compiler_params=_params())(parts, w, m, v)


def _my_place():
    x, y, c = lax.axis_index("x"), lax.axis_index("y"), lax.axis_index("c")
    return x, y, c


def _peer(x, y, c, d):
    px = 1 - x if d & 4 else x
    py = 1 - y if d & 2 else y
    pc = 1 - c if d & 1 else c
    return (px, py, pc), 4 * px + 2 * py + pc


def _all_gather(shards, name):
    nw = len(shards)

    def body(*refs):
        ins, outs = refs[:nw], refs[nw:2 * nw]
        send_sems, recv_sems, local_sems = refs[2 * nw:]
        x, y, c = _my_place()
        me = 4 * x + 2 * y + c
        copies = []
        for k in range(nw):
            rows = shards[k].shape[0]
            mine = outs[k].at[pl.ds(pl.multiple_of(me * rows, 16), rows), :]
            local = pltpu.make_async_copy(ins[k], mine, local_sems.at[k])
            local.start()
            copies.append(local)
            for d in range(1, N_DEV):
                place, _ = _peer(x, y, c, d)
                remote = pltpu.make_async_remote_copy(
                    src_ref=ins[k], dst_ref=mine, send_sem=send_sems.at[d - 1, k], recv_sem=recv_sems.at[d - 1, k],
                    device_id=place, device_id_type=MESH)
                remote.start()
                copies.append(remote)
        for cp in copies:
            cp.wait()

    hbm = pl.BlockSpec(memory_space=pl.ANY)
    return pl.pallas_call(
        body, name=name, in_specs=[hbm] * nw, out_specs=[hbm] * nw,
        out_shape=[jax.ShapeDtypeStruct((N_DEV * s.shape[0], s.shape[1]), s.dtype) for s in shards],
        scratch_shapes=[pltpu.SemaphoreType.DMA((N_DEV - 1, nw)), pltpu.SemaphoreType.DMA((N_DEV - 1, nw)),
                        pltpu.SemaphoreType.DMA((nw,))],
    )(*shards)


HBM_SPEC = pl.BlockSpec(memory_space=pltpu.HBM)
SEM_SPEC = pl.BlockSpec(memory_space=pltpu.SEMAPHORE)
EFFECT = pltpu.SideEffectType.DATAFLOW_SIDE_EFFECTING


def _in_hbm(a):
    return pltpu.with_memory_space_constraint(a, pltpu.HBM)


def _token_shape():
    return jax.ShapeDtypeStruct((8, HEAD), F32)


SIBLING = 1
OTHER_CHIPS = (4, 2, 6)


def _rows_of(ref, num, rows):
    return ref.at[pl.ds(pl.multiple_of(num * rows, 16), rows), :]


def _gather_start(shards, name):
    nw = len(shards)
    lands = [lax.empty((N_DEV * s.shape[0], s.shape[1]), s.dtype) for s in shards]
    n_to = 1 + len(OTHER_CHIPS)

    def body(*refs):
        ins, lnd = refs[:nw], refs[nw:2 * nw]
        send, from_sib, from_chips, own = (refs[(2 + i) * nw:(3 + i) * nw] for i in range(4))
        token = refs[8 * nw]
        x, y, c = _my_place()
        me = 4 * x + 2 * y + c
        for k in range(nw):
            mine = _rows_of(lnd[k], me, shards[k].shape[0])
            pltpu.make_async_copy(ins[k], mine, own[k]).start()
            for i, d in enumerate((SIBLING,) + OTHER_CHIPS):
                place, _ = _peer(x, y, c, d)
                pltpu.make_async_remote_copy(
                    src_ref=ins[k], dst_ref=mine, send_sem=send[k].at[i],
                    recv_sem=from_sib[k] if i == 0 else from_chips[k].at[i - 1],
                    device_id=place, device_id_type=MESH).start()
        token[...] = jnp.zeros_like(token)

    dma = pltpu.SemaphoreType.DMA
    sems = [dma((n_to,))] * nw + [dma(())] * nw + [dma((len(OTHER_CHIPS),))] * nw + [dma(())] * nw
    thru = [pltpu.HBM(a.shape, a.dtype) for a in list(shards) + lands]
    res = pl.pallas_call(
        body, name=name, out_shape=(*sems, *thru, _token_shape()),
        in_specs=[HBM_SPEC] * (2 * nw),
        out_specs=(*([SEM_SPEC] * (4 * nw)), *([HBM_SPEC] * (2 * nw)), pl.BlockSpec(memory_space=pltpu.VMEM)),
        input_output_aliases={i: 4 * nw + i for i in range(2 * nw)},
        compiler_params=pltpu.CompilerParams(has_side_effects=EFFECT),
    )(*[_in_hbm(s) for s in shards], *[_in_hbm(l) for l in lands])
    return [dict(send=res[k], from_sib=res[nw + k], from_chips=res[2 * nw + k], own=res[3 * nw + k],
                 src=res[4 * nw + k], land=res[5 * nw + k]) for k in range(nw)], res[6 * nw]


def _gather_forward(pending, after, name):
    rows = pending["src"].shape[0]
    n_fw = len(OTHER_CHIPS)

    def body(land_ref, from_chips, after_ref, fw_send, fw_recv, land_thru):
        x, y, c = _my_place()
        sibling, _ = _peer(x, y, c, SIBLING)
        for j, d in enumerate(OTHER_CHIPS):
            _, num = _peer(x, y, c, d)
            block = _rows_of(land_ref, num, rows)
            pltpu.make_async_remote_copy(
                src_ref=block, dst_ref=block, send_sem=fw_send.at[j], recv_sem=from_chips.at[j],
                device_id=sibling, device_id_type=MESH).wait_recv()
            pltpu.make_async_remote_copy(
                src_ref=block, dst_ref=block, send_sem=fw_send.at[j], recv_sem=fw_recv.at[j],
                device_id=sibling, device_id_type=MESH).start()

    land = pending["land"]
    dma = pltpu.SemaphoreType.DMA
    fw_send, fw_recv, land = pl.pallas_call(
        body, name=name, out_shape=(dma((n_fw,)), dma((n_fw,)), pltpu.HBM(land.shape, land.dtype)),
        in_specs=(HBM_SPEC, SEM_SPEC, pl.BlockSpec(memory_space=pl.ANY)),
        out_specs=(SEM_SPEC, SEM_SPEC, HBM_SPEC), input_output_aliases={0: 2},
        compiler_params=pltpu.CompilerParams(has_side_effects=EFFECT),
    )(land, pending["from_chips"], after)
    return dict(pending, land=land, fw_send=fw_send, fw_recv=fw_recv)


def _gather_wait(pending, name):
    rows = pending["src"].shape[0]

    def body(src_ref, land_ref, send, from_sib, own, fw_send, fw_recv, src_dead, got):
        x, y, c = _my_place()
        me = 4 * x + 2 * y + c
        sibling, sib_num = _peer(x, y, c, SIBLING)
        mine = _rows_of(land_ref, me, rows)
        pltpu.make_async_copy(src_ref, mine, own).wait()
        for i in range(1 + len(OTHER_CHIPS)):
            pltpu.make_async_remote_copy(
                src_ref=src_ref, dst_ref=mine, send_sem=send.at[i], recv_sem=from_sib,
                device_id=sibling, device_id_type=MESH).wait_send()
        theirs = _rows_of(land_ref, sib_num, rows)
        pltpu.make_async_remote_copy(
            src_ref=src_ref, dst_ref=theirs, send_sem=send.at[0], recv_sem=from_sib,
            device_id=sibling, device_id_type=MESH).wait_recv()
        for j, d in enumerate(OTHER_CHIPS):
            _, num = _peer(x, y, c, d)
            sent = _rows_of(land_ref, num, rows)
            _, got_num = _peer(x, y, c, d | SIBLING)
            arrived = _rows_of(land_ref, got_num, rows)
            cp = pltpu.make_async_remote_copy(
                src_ref=sent, dst_ref=arrived, send_sem=fw_send.at[j], recv_sem=fw_recv.at[j],
                device_id=sibling, device_id_type=MESH)
            cp.wait_send()
            cp.wait_recv()

    src, land = pending["src"], pending["land"]
    return pl.pallas_call(
        body, name=name, out_shape=(pltpu.HBM(src.shape, src.dtype), pltpu.HBM(land.shape, land.dtype)),
        in_specs=(HBM_SPEC, HBM_SPEC) + (SEM_SPEC,) * 5,
        out_specs=(HBM_SPEC, HBM_SPEC), input_output_aliases={0: 0, 1: 1},
        compiler_params=pltpu.CompilerParams(has_side_effects=EFFECT),
    )(src, land, pending["send"], pending["from_sib"], pending["own"], pending["fw_send"], pending["fw_recv"])[1]


def _scatter_start(full, name):
    rows, cols = full.shape[0] // N_DEV, full.shape[1]
    land = lax.empty((N_DEV, rows, cols), full.dtype)

    def body(full_ref, land_ref, send, recv, own, full_thru, land_thru, token):
        x, y, c = _my_place()
        me = 4 * x + 2 * y + c
        slab = land_ref.at[me]
        pltpu.make_async_copy(full_ref.at[pl.ds(pl.multiple_of(me * rows, 16), rows), :], slab, own).start()
        for d in range(1, N_DEV):
            place, num = _peer(x, y, c, d)
            pltpu.make_async_remote_copy(
                src_ref=full_ref.at[pl.ds(pl.multiple_of(num * rows, 16), rows), :], dst_ref=slab,
                send_sem=send.at[d - 1], recv_sem=recv.at[d - 1], device_id=place, device_id_type=MESH).start()
        token[...] = jnp.zeros_like(token)

    res = pl.pallas_call(
        body, name=name,
        out_shape=(pltpu.SemaphoreType.DMA((N_DEV - 1,)), pltpu.SemaphoreType.DMA((N_DEV - 1,)),
                   pltpu.SemaphoreType.DMA(()),
                   pltpu.HBM(full.shape, full.dtype), pltpu.HBM(land.shape, land.dtype), _token_shape()),
        in_specs=(HBM_SPEC, HBM_SPEC),
        out_specs=(SEM_SPEC, SEM_SPEC, SEM_SPEC, HBM_SPEC, HBM_SPEC, pl.BlockSpec(memory_space=pltpu.VMEM)),
        input_output_aliases={0: 3, 1: 4},
        compiler_params=pltpu.CompilerParams(has_side_effects=EFFECT),
    )(_in_hbm(full), _in_hbm(land))
    return dict(send=res[0], recv=res[1], own=res[2], src=res[3], land=res[4]), res[5]


def _scatter_wait(pending, after, name):
    rows = pending["land"].shape[1]

    def body(src_ref, land_ref, send, recv, own, after_ref, src_dead, got):
        x, y, c = _my_place()
        me = 4 * x + 2 * y + c
        pltpu.make_async_copy(src_ref.at[pl.ds(pl.multiple_of(me * rows, 16), rows), :], land_ref.at[me], own).wait()
        for d in range(1, N_DEV):
            place, num = _peer(x, y, c, d)
            cp = pltpu.make_async_remote_copy(
                src_ref=src_ref.at[pl.ds(pl.multiple_of(num * rows, 16), rows), :], dst_ref=land_ref.at[me],
                send_sem=send.at[d - 1], recv_sem=recv.at[d - 1], device_id=place, device_id_type=MESH)
            cp.wait_send()
            cp.wait_recv()

    src, land = pending["src"], pending["land"]
    return pl.pallas_call(
        body, name=name, out_shape=(pltpu.HBM(src.shape, src.dtype), pltpu.HBM(land.shape, land.dtype)),
        in_specs=(HBM_SPEC, HBM_SPEC, SEM_SPEC, SEM_SPEC, SEM_SPEC, pl.BlockSpec(memory_space=pl.ANY)),
        out_specs=(HBM_SPEC, HBM_SPEC), input_output_aliases={0: 0, 1: 1},
        compiler_params=pltpu.CompilerParams(has_side_effects=EFFECT),
    )(src, land, pending["send"], pending["recv"], pending["own"], after)[1]


BIG = ("ffn1_w_in", "ffn1_w_out", "mix_w_in", "w_branch_a", "w_branch_b", "mix_w_out", "ffn2_w_in", "ffn2_w_out")
TRANSPOSED = ("ffn1_w_in", "mix_w_in", "w_branch_b", "ffn2_w_in")
SMALL = ("ln1_g", "ln1_b", "ln2_g", "ln2_b", "ln3_g", "ln3_b", "hgrn_norm_g", "hgrn_lb_fwd", "hgrn_lb_bwd")
SMALL_ROWS = 16


def _local_step(x, target, weight, emit, sp):
    b_, s_, d_ = x.shape
    t_ = b_ * s_
    x2, tgt = x.reshape(t_, d_), target.reshape(t_, d_)
    xb = x2.astype(BF16)
    w1i = weight("ffn1_w_in", xb)
    g1, u1, a1 = _ffn_in_fwd(xb, w1i, "ffn1_in")
    w1o = weight("ffn1_w_out", a1)
    r1, h1, h1b = _mm_res_ln_fwd(a1, w1o, x2, sp["ln1_g"], sp["ln1_b"], 0.5, "ffn1_out_ln1")
    wmx = weight("mix_w_in", h1b)
    proj = _mm_nt(h1b, wmx, F32, "mix_in")
    ya_in, o_sum = _hgrn_fwd(proj, sp["hgrn_lb_fwd"], sp["hgrn_lb_bwd"], sp["hgrn_norm_g"], b_, s_, d_, "hgrn_fwd")
    tabs = _rope_tables(s_)
    outs, lses = [], []
    group_col = [(5 * d_ + gi * QKV_GROUP) // HEAD for gi in range(len(ATTN_GROUPS))]
    for gi, (_, dil) in enumerate(ATTN_GROUPS):
        o_g, lse_g = _attn_fwd(proj, tabs, b_, s_, group_col[gi], dil, f"attn_fwd_{gi}")
        outs.append(o_g)
        lses.append(lse_g)
    ob = _combine_fwd(outs, lses, "attn_combine")
    wa, wb = weight("w_branch_a", ya_in), weight("w_branch_b", ob)
    ya, yb, z = _gate_out_fwd(ya_in, ob, wa, wb, proj, d_, "branch_gate")
    wo = weight("mix_w_out", z)
    r2, h2, h2b = _mm_res_ln_fwd(z, wo, h1, sp["ln2_g"], sp["ln2_b"], 1.0, "mix_out_ln2")
    w2i = weight("ffn2_w_in", h2b)
    g2, u2, a2 = _ffn_in_fwd(h2b, w2i, "ffn2_in")
    w2o = weight("ffn2_w_out", a2)
    r3, _, _ = _mm_res_ln_fwd(a2, w2o, h2, sp["ln3_g"], sp["ln3_b"], 0.5, "ffn2_out_ln3")
    dr3, dr3b, dg3, db3, loss = _ln_loss_bwd(r3, tgt, sp["ln3_g"], sp["ln3_b"], "loss_ln3_bwd")
    dep = emit("ffn2_w_out", _mm_tn(a2, dr3b, 0.5, "d_ffn2_w_out"))
    dgate2, dup2 = _ffn_mid_bwd(dr3b, w2o, g2, u2, 0.5, "ffn2_mid_bwd", dep)
    du2 = (dgate2, dup2)
    dep = emit("ffn2_w_in", _mm_tn(du2, h2b, 1.0, "d_ffn2_w_in"))
    dr2, dr2b, dg2, db2 = _mm_nn_res_lnbwd(du2, w2i, dr3, r2, sp["ln2_g"], "ffn2_in_bwd_ln2", dep)
    dep = emit("mix_w_out", _mm_tn(z, dr2b, 1.0, "d_mix_w_out"))
    dya, dyb, dga, dgb = _dz_gate_bwd(dr2b, wo, proj, ya, yb, d_, "branch_gate_bwd", dep)
    dep = emit("w_branch_a", _mm_tn(ya_in, dya, 1.0, "d_w_branch_a"))
    dya_in = _mm_nt(dya, wa, F32, "branch_a_bwd", dep)
    dep = emit("w_branch_b", _mm_tn(dyb, ob, 1.0, "d_w_branch_b"))
    dob = _mm_nn(dyb, wb, F32, "branch_b_bwd", dep)
    dhq, dhff, dhfb, dhi, dhog, dng, dlbf, dlbb = _hgrn_bwd(
        proj, sp["hgrn_lb_fwd"], sp["hgrn_lb_bwd"], sp["hgrn_norm_g"], o_sum, dya_in, b_, s_, d_, "hgrn_bwd")
    dogs, cgs = _combine_bwd(dob, outs, lses, "attn_combine_bwd")
    dqkv = []
    for gi, (_, dil) in enumerate(ATTN_GROUPS):
        dqkv += _attn_bwd(proj, tabs, dogs[gi], cgs[gi], lses[gi], b_, s_, group_col[gi], dil, f"attn_bwd_{gi}")
    dproj = jnp.concatenate([dhq, dhff, dhfb, dhi, dhog] + [t.astype(BF16) for t in dqkv] + [dga, dgb], axis=1)
    dep = emit("mix_w_in", _mm_tn(dproj, h1b, 1.0, "d_mix_w_in"))
    dr1, dr1b, dg1, db1 = _mm_nn_res_lnbwd(dproj, wmx, dr2, r1, sp["ln1_g"], "mix_in_bwd_ln1", dep)
    dep = emit("ffn1_w_out", _mm_tn(a1, dr1b, 0.5, "d_ffn1_w_out"))
    dgate1, dup1 = _ffn_mid_bwd(dr1b, w1o, g1, u1, 0.5, "ffn1_mid_bwd", dep)
    du1 = (dgate1, dup1)
    dep = emit("ffn1_w_in", _mm_tn(du1, xb, 1.0, "d_ffn1_w_in"))
    grad_x = _mm_nn_res(du1, w1i, dr1, "ffn1_in_bwd", dep)
    small = {"ln1_g": dg1, "ln1_b": db1, "ln2_g": dg2, "ln2_b": db2, "ln3_g": dg3, "ln3_b": db3,
             "hgrn_norm_g": dng, "hgrn_lb_fwd": dlbf, "hgrn_lb_bwd": dlbb}
    return loss, grad_x.reshape(b_, s_, d_), small


def _pack_small(vals):
    rows = jnp.concatenate([vals[n] for n in SMALL], axis=0)
    return jnp.pad(rows, ((0, SMALL_ROWS - rows.shape[0]), (0, 0)))


def _unpack_small(packed):
    out, r = {}, 0
    for n in SMALL:
        k = 2 if n.startswith("hgrn_lb") else 1
        out[n] = packed[r:r + k]
        r += k
    return out


def kernel(x, ffn1_w_in, ffn1_w_out, ln1_g, ln1_b, mix_w_in, hgrn_lb_fwd, hgrn_lb_bwd, hgrn_norm_g, w_branch_a, w_branch_b, mix_w_out, ln2_g, ln2_b, ffn2_w_in, ffn2_w_out, ln3_g, ln3_b, loss_target, m_ffn1_w_in, m_ffn1_w_out, m_ln1_g, m_ln1_b, m_mix_w_in, m_hgrn_lb_fwd, m_hgrn_lb_bwd, m_hgrn_norm_g, m_w_branch_a, m_w_branch_b, m_mix_w_out, m_ln2_g, m_ln2_b, m_ffn2_w_in, m_ffn2_w_out, m_ln3_g, m_ln3_b, v_ffn1_w_in, v_ffn1_w_out, v_ln1_g, v_ln1_b, v_mix_w_in, v_hgrn_lb_fwd, v_hgrn_lb_bwd, v_hgrn_norm_g, v_w_branch_a, v_w_branch_b, v_mix_w_out, v_ln2_g, v_ln2_b, v_ffn2_w_in, v_ffn2_w_out, v_ln3_g, v_ln3_b):
    args = dict(locals())
    big_w = {n: args[n][0] for n in BIG}
    sp = {n: args[n] for n in SMALL}
    def rows_bf16(n, zero=0.0):
        w = big_w[n] + zero
        return (w.T if n in TRANSPOSED else w).astype(BF16)

    first, rest = BIG[:2], BIG[2:]
    pending, token = _gather_start([rows_bf16(n) for n in first], "gather_start_ffn1")
    gathering = dict(zip(first, pending))
    pending, all_started = _gather_start([rows_bf16(n, token[0, 0]) for n in rest], "gather_start_rest")
    gathering.update(zip(rest, pending))
    scattering = {}

    def weight(n, after):
        if n == first[0]:
            after = all_started
        return _gather_wait(_gather_forward(gathering[n], after, f"gather_forward_{n}"), f"gather_wait_{n}")

    def emit(n, grad):
        scattering[n], token = _scatter_start(grad, f"scatter_start_{n}")
        return token

    loss_part, grad_x, small = _local_step(x, loss_target, weight, emit, sp)
    loss = lax.psum(loss_part[0, 0], ("x", "y", "c"))
    out_g, out_d, out_m, out_v = {}, {}, {}, {}
    for n in BIG:
        g = _sum_partials(_scatter_wait(scattering[n], grad_x, f"scatter_wait_{n}"), f"sum_{n}")
        if n in TRANSPOSED:
            g = g.T
        d_w, m_w, v_w = _adam(big_w[n], g, args["m_" + n][0], args["v_" + n][0], f"adam_{n}")
        out_g[n], out_d[n], out_m[n], out_v[n] = g[None], d_w[None], m_w[None], v_w[None]
    (parts,) = _all_gather([_pack_small(small)], "gather_small_grads")
    res = _small_sum_adam(parts.reshape(N_DEV, SMALL_ROWS, parts.shape[1]), _pack_small(sp),
                          _pack_small({n: args["m_" + n] for n in SMALL}),
                          _pack_small({n: args["v_" + n] for n in SMALL}), "small_adam")
    sg, sd, sm, sv = (_unpack_small(r) for r in res)
    out_g.update(sg), out_d.update(sd), out_m.update(sm), out_v.update(sv)
    order = ("ffn1_w_in", "ffn1_w_out", "ln1_g", "ln1_b", "mix_w_in", "hgrn_lb_fwd", "hgrn_lb_bwd", "hgrn_norm_g",
             "w_branch_a", "w_branch_b", "mix_w_out", "ln2_g", "ln2_b", "ffn2_w_in", "ffn2_w_out", "ln3_g", "ln3_b")
    return (loss, grad_x, *[out_g[n] for n in order], *[out_d[n] for n in order],
            *[out_m[n] for n in order], *[out_v[n] for n in order])
```

```python
import functools

import jax
import jax.numpy as jnp
from jax import lax
from jax.experimental import pallas as pl
from jax.experimental.pallas import tpu as pltpu

F32 = jnp.float32
BF16 = jnp.bfloat16

N_DEV = 8
HEAD = 128
CHUNK = 32
ATTN_GROUPS = ((128, 1), (512, 4), (2048, 16))
ATTN_HEADS = 4
ATTN_HALF = 64
QKV_GROUP = 3 * ATTN_HEADS * HEAD
QKV_WIDTH = len(ATTN_GROUPS) * QKV_GROUP
ATTN_OUT = ATTN_HEADS * HEAD
ROPE_THETA = 500000.0
ROPE_DIM = HEAD // 4
ALPHA = 2.0 ** 0.25
LN_EPS = 1e-5
NEG_INF = -1e30
ADAM_LR, ADAM_B1, ADAM_B2, ADAM_EPS, ADAM_WD, ADAM_STEP = 0.001, 0.9, 0.999, 1e-08, 0.01, 10
VMEM_LIMIT = 56 * 1024 * 1024

NT = (((1,), (1,)), ((), ()))
NN = (((1,), (0,)), ((), ()))
TN = (((0,), (0,)), ((), ()))
MESH = pl.DeviceIdType.MESH


def _dot(a, b, dims):
    return lax.dot_general(a, b, dims, preferred_element_type=F32)


def _tile(n, pref, mult=128):
    if n <= pref:
        return n
    t = (pref // mult) * mult
    while t >= mult:
        if n % t == 0:
            return t
        t -= mult
    return n


def _tile_multi(ns, pref, mult=128):
    t = (pref // mult) * mult
    while t >= mult:
        if all(n % t == 0 for n in ns):
            return t
        t -= mult
    raise ValueError(f"no common tile for {ns}")


def _params(**kw):
    return pltpu.CompilerParams(vmem_limit_bytes=VMEM_LIMIT, **kw)


def _after(body, n_in, dep):
    if dep is None:
        return body, [], []

    def wrapped(*refs):
        body(*refs[:n_in], *refs[n_in + 1:])

    return wrapped, [pl.BlockSpec(dep.shape, lambda *_: (0,) * dep.ndim)], [dep]


def _pieces(a):
    pieces = tuple(a) if isinstance(a, (tuple, list)) else (a,)
    assert all(p.shape == pieces[0].shape for p in pieces)
    return pieces, pieces[0].shape[0], pieces[0].shape[1], len(pieces)


def _for_piece(step, p, per, npc, fn):
    if npc == 1:
        fn()
    else:
        pl.when((step >= p * per) & (step < (p + 1) * per))(fn)


def _sigmoid(x):
    return jax.nn.sigmoid(x)


def _dsilu(x, s):
    return s * (1.0 + x * (1.0 - s))


def _ln_stats(r):
    mu = jnp.mean(r, axis=-1, keepdims=True)
    xc = r - mu
    var = jnp.mean(xc * xc, axis=-1, keepdims=True)
    rstd = lax.rsqrt(var + LN_EPS)
    return xc * rstd, rstd


def _ln_bwd(dy, xhat, rstd, g):
    dyg = dy * g
    m1 = jnp.mean(dyg, axis=-1, keepdims=True)
    m2 = jnp.mean(dyg * xhat, axis=-1, keepdims=True)
    return rstd * (dyg - m1 - xhat * m2)


ROW_TILE = 1024
SUB_ROWS = 256


def _once(shape, index_map):
    return pl.BlockSpec(shape, index_map, pipeline_mode=pl.Buffered(1))


def _for_row_blocks(tm, fn):
    sub = SUB_ROWS if tm % SUB_ROWS == 0 else tm

    def step(s, carry):
        fn(pl.ds(pl.multiple_of(s * sub, sub), sub))
        return carry

    lax.fori_loop(0, tm // sub, step, 0)


def _row_runs(tm):
    sub = SUB_ROWS if tm % SUB_ROWS == 0 else tm
    return [slice(s, s + sub) for s in range(0, tm, sub)]


def _ffn_in_fwd(xb, w_t, name):
    t_, d_ = xb.shape
    f_ = w_t.shape[0] // 2
    tm, tn = _tile(t_, ROW_TILE, 8), _tile(f_, 512)
    nj = f_ // tn

    def body(x_ref, wg_ref, wu_ref, g_ref, u_ref, a_ref):
        wg, wu = wg_ref[...], wu_ref[...]
        for rows in _row_runs(tm):
            x = x_ref[rows, :]
            g = _dot(x, wg, NT)
            u = _dot(x, wu, NT)
            g_ref[rows, :] = g.astype(BF16)
            u_ref[rows, :] = u.astype(BF16)
            a_ref[rows, :] = (g * _sigmoid(g) * u).astype(BF16)

    return pl.pallas_call(
        body, name=name, grid=(t_ // tm, nj),
        in_specs=[pl.BlockSpec((tm, d_), lambda i, j: (i, 0)),
                  pl.BlockSpec((tn, d_), lambda i, j: (j, 0)),
                  pl.BlockSpec((tn, d_), lambda i, j: (j + nj, 0))],
        out_specs=[pl.BlockSpec((tm, tn), lambda i, j: (i, j))] * 3,
        out_shape=[jax.ShapeDtypeStruct((t_, f_), BF16)] * 3,
        compiler_params=_params(),
    )(xb, w_t, w_t)


def _mm_res_ln_fwd(a, w, res, g, b, scale, name):
    t_, k_ = a.shape
    d_ = w.shape[1]
    whole = k_ <= 2048
    tm, tk = _tile(t_, ROW_TILE, 8), (k_ if whole else _tile(k_, 512))
    nk = k_ // tk

    def body(a_ref, w_ref, res_ref, g_ref, b_ref, r_ref, h_ref, hb_ref, *scratch):
        k = pl.program_id(1)
        if not whole:
            acc, = scratch

            @pl.when(k == 0)
            def _():
                acc[...] = jnp.zeros_like(acc)

            acc[...] += _dot(a_ref[...], w_ref[...], NN)

        @pl.when(k == nk - 1)
        def _():
            def rows_out(rows):
                prod = _dot(a_ref[rows, :], w_ref[...], NN) if whole else acc[rows, :]
                r = ALPHA * res_ref[rows, :] + scale * prod
                xhat, _ = _ln_stats(r)
                h = xhat * g_ref[...] + b_ref[...]
                r_ref[rows, :] = r
                h_ref[rows, :] = h
                hb_ref[rows, :] = h.astype(BF16)

            _for_row_blocks(tm, rows_out)

    row = _once((tm, d_), lambda i, k: (i, 0))
    vec = pl.BlockSpec((1, d_), lambda i, k: (0, 0))
    w_spec = _once((tk, d_), lambda i, k: (0, 0)) if whole else pl.BlockSpec((tk, d_), lambda i, k: (k, 0))
    return pl.pallas_call(
        body, name=name, grid=(t_ // tm, nk),
        in_specs=[pl.BlockSpec((tm, tk), lambda i, k: (i, k)), w_spec, row, vec, vec],
        out_specs=[row, row, row],
        out_shape=[jax.ShapeDtypeStruct((t_, d_), F32), jax.ShapeDtypeStruct((t_, d_), F32),
                   jax.ShapeDtypeStruct((t_, d_), BF16)],
        scratch_shapes=[] if whole else [pltpu.VMEM((tm, d_), F32)],
        compiler_params=_params(),
    )(a, w, res, g, b)


def _mm_res_loss_bwd(a, w, res, target, g, b, scale, name):
    t_, k_ = a.shape
    d_ = w.shape[1]
    tm, tk = _tile(t_, ROW_TILE, 8), _tile(k_, 512)
    nk = k_ // tk

    def body(a_ref, w_ref, res_ref, t_ref, g_ref, b_ref, dr_ref, drb_ref, dg_ref, db_ref, loss_ref, acc):
        i, k = pl.program_id(0), pl.program_id(1)

        @pl.when(k == 0)
        def _():
            acc[...] = jnp.zeros_like(acc)

        @pl.when((i == 0) & (k == 0))
        def _():
            dg_ref[...] = jnp.zeros_like(dg_ref)
            db_ref[...] = jnp.zeros_like(db_ref)
            loss_ref[...] = jnp.zeros_like(loss_ref)

        acc[...] += _dot(a_ref[...], w_ref[...], NN)

        @pl.when(k == nk - 1)
        def _():
            def rows_out(rows):
                r = ALPHA * res_ref[rows, :] + scale * acc[rows, :]
                xhat, rstd = _ln_stats(r)
                gain = g_ref[...]
                err = xhat * gain + b_ref[...] - t_ref[rows, :]
                loss_ref[...] += (0.5 / d_) * jnp.sum(err * err)
                dy = err * (1.0 / d_)
                dr = _ln_bwd(dy, xhat, rstd, gain)
                dr_ref[rows, :] = dr
                drb_ref[rows, :] = dr.astype(BF16)
                dg_ref[...] += jnp.sum(dy * xhat, axis=0, keepdims=True)
                db_ref[...] += jnp.sum(dy, axis=0, keepdims=True)

            _for_row_blocks(tm, rows_out)

    row = _once((tm, d_), lambda i, k: (i, 0))
    vec = pl.BlockSpec((1, d_), lambda i, k: (0, 0))
    return pl.pallas_call(
        body, name=name, grid=(t_ // tm, nk),
        in_specs=[pl.BlockSpec((tm, tk), lambda i, k: (i, k)),
                  pl.BlockSpec((tk, d_), lambda i, k: (k, 0)), row, row, vec, vec],
        out_specs=[row, row, vec, vec, pl.BlockSpec((1, HEAD), lambda i, k: (0, 0))],
        out_shape=[jax.ShapeDtypeStruct((t_, d_), F32), jax.ShapeDtypeStruct((t_, d_), BF16),
                   jax.ShapeDtypeStruct((1, d_), F32), jax.ShapeDtypeStruct((1, d_), F32),
                   jax.ShapeDtypeStruct((1, HEAD), F32)],
        scratch_shapes=[pltpu.VMEM((tm, d_), F32)],
        compiler_params=_params(),
    )(a, w, res, target, g, b)


def _mm_nt(a, w_t, out_dtype, name, dep=None):
    t_, k_ = a.shape
    n_ = w_t.shape[0]
    tm, tn = _tile(t_, ROW_TILE, 8), _tile(n_, 512)

    def body(a_ref, w_ref, o_ref):
        o_ref[...] = _dot(a_ref[...], w_ref[...], NT).astype(out_dtype)

    body, dep_specs, deps = _after(body, 2, dep)
    return pl.pallas_call(
        body, name=name, grid=(t_ // tm, n_ // tn),
        in_specs=[pl.BlockSpec((tm, k_), lambda i, j: (i, 0)),
                  pl.BlockSpec((tn, k_), lambda i, j: (j, 0)), *dep_specs],
        out_specs=pl.BlockSpec((tm, tn), lambda i, j: (i, j)),
        out_shape=jax.ShapeDtypeStruct((t_, n_), out_dtype),
        compiler_params=_params(),
    )(a, w_t, *deps)


def _mm_nn(a, w, out_dtype, name, dep=None):
    t_, k_ = a.shape
    n_ = w.shape[1]
    tm, tn = _tile(t_, ROW_TILE, 8), _tile(n_, 512)

    def body(a_ref, w_ref, o_ref):
        o_ref[...] = _dot(a_ref[...], w_ref[...], NN).astype(out_dtype)

    body, dep_specs, deps = _after(body, 2, dep)
    return pl.pallas_call(
        body, name=name, grid=(t_ // tm, n_ // tn),
        in_specs=[pl.BlockSpec((tm, k_), lambda i, j: (i, 0)),
                  pl.BlockSpec((k_, tn), lambda i, j: (0, j)), *dep_specs],
        out_specs=pl.BlockSpec((tm, tn), lambda i, j: (i, j)),
        out_shape=jax.ShapeDtypeStruct((t_, n_), out_dtype),
        compiler_params=_params(),
    )(a, w, *deps)


def _mm_tn(a, b, scale, name):
    pieces, t_, mp, npc = _pieces(a)
    n_ = b.shape[1]
    tm = _tile(mp, 512)
    per = mp // tm

    def body(*refs):
        b_ref, o_ref = refs[npc], refs[npc + 1]
        for p in range(npc):
            def piece_out(p=p):
                o_ref[...] = (scale * _dot(refs[p][...], b_ref[...], TN)).astype(BF16)

            _for_piece(pl.program_id(0), p, per, npc, piece_out)

    return pl.pallas_call(
        body, name=name, grid=(npc * per,),
        in_specs=[pl.BlockSpec((t_, tm), lambda i, p=p: (0, jnp.clip(i - p * per, 0, per - 1))) for p in range(npc)]
        + [_once((t_, n_), lambda i: (0, 0))],
        out_specs=pl.BlockSpec((tm, n_), lambda i: (i, 0)),
        out_shape=jax.ShapeDtypeStruct((npc * mp, n_), BF16),
        compiler_params=_params(),
    )(*pieces, b)


def _gate_out_fwd(ya_in, ob, wa, wb_t, proj, d_, name):
    t_ = ya_in.shape[0]
    goff = 5 * d_ + QKV_WIDTH
    tm, tn = _tile(t_, ROW_TILE, 8), _tile_multi([d_, goff], 512)
    ja, jb = goff // tn, (goff + d_) // tn

    def body(ya_ref, ob_ref, wa_ref, wb_ref, ga_ref, gb_ref, yao_ref, ybo_ref, z_ref):
        wa, wb = wa_ref[...], wb_ref[...]
        for rows in _row_runs(tm):
            y_a = _dot(ya_ref[rows, :], wa, NN)
            y_b = _dot(ob_ref[rows, :], wb, NT)
            yao_ref[rows, :] = y_a.astype(BF16)
            ybo_ref[rows, :] = y_b.astype(BF16)
            z_ref[rows, :] = (_sigmoid(ga_ref[rows, :]) * y_a + _sigmoid(gb_ref[rows, :]) * y_b).astype(BF16)

    tile = pl.BlockSpec((tm, tn), lambda i, j: (i, j))
    return pl.pallas_call(
        body, name=name, grid=(t_ // tm, d_ // tn),
        in_specs=[pl.BlockSpec((tm, d_), lambda i, j: (i, 0)),
                  pl.BlockSpec((tm, ATTN_OUT), lambda i, j: (i, 0)),
                  pl.BlockSpec((d_, tn), lambda i, j: (0, j)),
                  pl.BlockSpec((tn, ATTN_OUT), lambda i, j: (j, 0)),
                  pl.BlockSpec((tm, tn), lambda i, j: (i, ja + j)),
                  pl.BlockSpec((tm, tn), lambda i, j: (i, jb + j))],
        out_specs=[tile, tile, tile],
        out_shape=[jax.ShapeDtypeStruct((t_, d_), BF16)] * 3,
        compiler_params=_params(),
    )(ya_in, ob, wa, wb_t, proj, proj)


def _ffn_mid_bwd(drb, w_out, gate, up, scale, name, dep=None):
    t_, d_ = drb.shape
    f_ = w_out.shape[0]
    tm, tn = _tile(t_, ROW_TILE, 8), _tile(f_, 512)

    def body(dr_ref, w_ref, g_ref, u_ref, dg_ref, du_ref):
        w = w_ref[...]
        for rows in _row_runs(tm):
            da = scale * _dot(dr_ref[rows, :], w, NT)
            g = g_ref[rows, :].astype(F32)
            s = _sigmoid(g)
            dg_ref[rows, :] = (da * u_ref[rows, :].astype(F32) * _dsilu(g, s)).astype(BF16)
            du_ref[rows, :] = (da * g * s).astype(BF16)

    body, dep_specs, deps = _after(body, 4, dep)
    tile = pl.BlockSpec((tm, tn), lambda i, j: (i, j))
    return pl.pallas_call(
        body, name=name, grid=(t_ // tm, f_ // tn),
        in_specs=[pl.BlockSpec((tm, d_), lambda i, j: (i, 0)),
                  pl.BlockSpec((tn, d_), lambda i, j: (j, 0)), tile, tile, *dep_specs],
        out_specs=[tile, tile],
        out_shape=[jax.ShapeDtypeStruct((t_, f_), BF16)] * 2,
        compiler_params=_params(),
    )(drb, w_out, gate, up, *deps)


def _mm_nn_res_lnbwd(a, w, dres, r, g, name, dep=None):
    pieces, t_, kp, npc = _pieces(a)
    d_ = w.shape[1]
    tm, tk = _tile(t_, ROW_TILE, 8), _tile(kp, 512)
    per = kp // tk
    nk = npc * per

    def body(*refs):
        w_ref, dres_ref, r_ref, g_ref, dr_ref, drb_ref, dg_ref, db_ref, acc = refs[npc:]
        i, k = pl.program_id(0), pl.program_id(1)

        @pl.when(k == 0)
        def _():
            acc[...] = jnp.zeros_like(acc)

        @pl.when((i == 0) & (k == 0))
        def _():
            dg_ref[...] = jnp.zeros_like(dg_ref)
            db_ref[...] = jnp.zeros_like(db_ref)

        for p in range(npc):
            def piece_in(p=p):
                acc[...] += _dot(refs[p][...], w_ref[...], NN)

            _for_piece(k, p, per, npc, piece_in)

        @pl.when(k == nk - 1)
        def _():
            def rows_out(rows):
                dy = acc[rows, :] + ALPHA * dres_ref[rows, :]
                xhat, rstd = _ln_stats(r_ref[rows, :])
                dr = _ln_bwd(dy, xhat, rstd, g_ref[...])
                dr_ref[rows, :] = dr
                drb_ref[rows, :] = dr.astype(BF16)
                dg_ref[...] += jnp.sum(dy * xhat, axis=0, keepdims=True)
                db_ref[...] += jnp.sum(dy, axis=0, keepdims=True)

            _for_row_blocks(tm, rows_out)

    body, dep_specs, deps = _after(body, npc + 4, dep)
    row = _once((tm, d_), lambda i, k: (i, 0))
    vec = pl.BlockSpec((1, d_), lambda i, k: (0, 0))
    return pl.pallas_call(
        body, name=name, grid=(t_ // tm, nk),
        in_specs=[pl.BlockSpec((tm, tk), lambda i, k, p=p: (i, jnp.clip(k - p * per, 0, per - 1))) for p in range(npc)]
        + [pl.BlockSpec((tk, d_), lambda i, k: (k, 0)), row, row, vec, *dep_specs],
        out_specs=[row, row, vec, vec],
        out_shape=[jax.ShapeDtypeStruct((t_, d_), F32), jax.ShapeDtypeStruct((t_, d_), BF16),
                   jax.ShapeDtypeStruct((1, d_), F32), jax.ShapeDtypeStruct((1, d_), F32)],
        scratch_shapes=[pltpu.VMEM((tm, d_), F32)],
        compiler_params=_params(),
    )(*pieces, w, dres, r, g, *deps)


def _mm_nn_res(a, w, dres, name, dep=None):
    pieces, t_, kp, npc = _pieces(a)
    d_ = w.shape[1]
    tm, tk = _tile(t_, ROW_TILE, 8), _tile(kp, 512)
    per = kp // tk
    nk = npc * per

    def body(*refs):
        w_ref, dres_ref, o_ref, acc = refs[npc:]
        k = pl.program_id(1)

        @pl.when(k == 0)
        def _():
            acc[...] = jnp.zeros_like(acc)

        for p in range(npc):
            def piece_in(p=p):
                acc[...] += _dot(refs[p][...], w_ref[...], NN)

            _for_piece(k, p, per, npc, piece_in)

        @pl.when(k == nk - 1)
        def _():
            def rows_out(rows):
                o_ref[rows, :] = acc[rows, :] + ALPHA * dres_ref[rows, :]

            _for_row_blocks(tm, rows_out)

    body, dep_specs, deps = _after(body, npc + 2, dep)
    row = _once((tm, d_), lambda i, k: (i, 0))
    return pl.pallas_call(
        body, name=name, grid=(t_ // tm, nk),
        in_specs=[pl.BlockSpec((tm, tk), lambda i, k, p=p: (i, jnp.clip(k - p * per, 0, per - 1))) for p in range(npc)]
        + [pl.BlockSpec((tk, d_), lambda i, k: (k, 0)), row, *dep_specs],
        out_specs=row,
        out_shape=jax.ShapeDtypeStruct((t_, d_), F32),
        scratch_shapes=[pltpu.VMEM((tm, d_), F32)],
        compiler_params=_params(),
    )(*pieces, w, dres, *deps)


def _dz_gate_bwd(drb, w_out, proj, ya, yb, d_, name, dep=None):
    t_ = drb.shape[0]
    goff = 5 * d_ + QKV_WIDTH
    tm, tn = _tile(t_, ROW_TILE, 8), _tile_multi([d_, goff], 512)
    ja, jb = goff // tn, (goff + d_) // tn

    def body(dr_ref, w_ref, ga_ref, gb_ref, ya_ref, yb_ref, dya_ref, dyb_ref, dga_ref, dgb_ref):
        w = w_ref[...]
        for rows in _row_runs(tm):
            dz = _dot(dr_ref[rows, :], w, NT)
            sa, sb = _sigmoid(ga_ref[rows, :]), _sigmoid(gb_ref[rows, :])
            dya_ref[rows, :] = (dz * sa).astype(BF16)
            dyb_ref[rows, :] = (dz * sb).astype(BF16)
            dga_ref[rows, :] = (dz * ya_ref[rows, :].astype(F32) * sa * (1.0 - sa)).astype(BF16)
            dgb_ref[rows, :] = (dz * yb_ref[rows, :].astype(F32) * sb * (1.0 - sb)).astype(BF16)

    body, dep_specs, deps = _after(body, 6, dep)
    tile = pl.BlockSpec((tm, tn), lambda i, j: (i, j))
    return pl.pallas_call(
        body, name=name, grid=(t_ // tm, d_ // tn),
        in_specs=[pl.BlockSpec((tm, d_), lambda i, j: (i, 0)),
                  pl.BlockSpec((tn, d_), lambda i, j: (j, 0)),
                  pl.BlockSpec((tm, tn), lambda i, j: (i, ja + j)),
                  pl.BlockSpec((tm, tn), lambda i, j: (i, jb + j)), tile, tile, *dep_specs],
        out_specs=[tile] * 4,
        out_shape=[jax.ShapeDtypeStruct((t_, d_), BF16)] * 4,
        compiler_params=_params(),
    )(drb, w_out, proj, proj, ya, yb, *deps)


def _ln_loss_bwd(r, target, g, b, name):
    t_, d_ = r.shape
    tm = _tile(t_, 256, 8)

    def body(r_ref, t_ref, g_ref, b_ref, dr_ref, drb_ref, dg_ref, db_ref, loss_ref):
        i = pl.program_id(0)

        @pl.when(i == 0)
        def _():
            dg_ref[...] = jnp.zeros_like(dg_ref)
            db_ref[...] = jnp.zeros_like(db_ref)
            loss_ref[...] = jnp.zeros_like(loss_ref)

        xhat, rstd = _ln_stats(r_ref[...])
        gain = g_ref[...]
        err = xhat * gain + b_ref[...] - t_ref[...]
        loss_ref[...] += (0.5 / d_) * jnp.sum(err * err)
        dy = err * (1.0 / d_)
        dr = _ln_bwd(dy, xhat, rstd, gain)
        dr_ref[...] = dr
        drb_ref[...] = dr.astype(BF16)
        dg_ref[...] += jnp.sum(dy * xhat, axis=0, keepdims=True)
        db_ref[...] += jnp.sum(dy, axis=0, keepdims=True)

    row = pl.BlockSpec((tm, d_), lambda i: (i, 0))
    vec = pl.BlockSpec((1, d_), lambda i: (0, 0))
    return pl.pallas_call(
        body, name=name, grid=(t_ // tm,),
        in_specs=[row, row, vec, vec],
        out_specs=[row, row, vec, vec, pl.BlockSpec((1, HEAD), lambda i: (0, 0))],
        out_shape=[jax.ShapeDtypeStruct((t_, d_), F32), jax.ShapeDtypeStruct((t_, d_), BF16),
                   jax.ShapeDtypeStruct((1, d_), F32), jax.ShapeDtypeStruct((1, d_), F32),
                   jax.ShapeDtypeStruct((1, HEAD), F32)],
        compiler_params=_params(),
    )(r, target, g, b)


def _chunk_scan(x, row, reverse, size):
    s = 1
    while s < CHUNK:
        if reverse:
            x = x + jnp.where(row < CHUNK - s, pltpu.roll(x, size - s, 0), 0.0)
        else:
            x = x + jnp.where(row >= s, pltpu.roll(x, s, 0), 0.0)
        s *= 2
    return x


def _lower_bound(tab):
    return _sigmoid(tab[0:1, :] - tab[1:2, :])


def _tri_mask(reverse):
    r = lax.broadcasted_iota(jnp.int32, (CHUNK, CHUNK), 0)
    c = lax.broadcasted_iota(jnp.int32, (CHUNK, CHUNK), 1)
    return (c >= r) if reverse else (r >= c)


def _hgrn_fwd(proj, lbf, lbb, ng, b_, s_, d_, name):
    h_ = d_ // HEAD
    nc = s_ // CHUNK

    def body(hq_ref, hff_ref, hfb_ref, hi_ref, hog_ref, lbf_ref, lbb_ref, ng_ref, ya_ref, o_ref,
             q_s, k_s, cum_s, o_s):
        row = lax.broadcasted_iota(jnp.int32, (s_, HEAD), 0) % CHUNK
        hq = hq_ref[...]
        q_s[...] = hq * _sigmoid(hq)
        o_s[...] = jnp.zeros_like(o_s)
        for reverse, hf_ref, lb_ref in ((False, hff_ref, lbf_ref), (True, hfb_ref, lbb_ref)):
            lb = _lower_bound(lb_ref[...])
            f = lb + (1.0 - lb) * _sigmoid(hf_ref[...])
            k_s[...] = 1.0 - f
            cum_s[...] = _chunk_scan(jnp.log(f), row, reverse, s_)
            mask = _tri_mask(reverse)

            def step(n, st, reverse=reverse, mask=mask):
                idx = (nc - 1 - n) if reverse else n
                sl = pl.ds(pl.multiple_of(idx * CHUNK, CHUNK), CHUNK)
                cm = cum_s[sl, :]
                tot = cm[0:1, :] if reverse else cm[CHUNK - 1:CHUNK, :]
                qc, kc = q_s[sl, :], k_s[sl, :]
                vb = hi_ref[sl, :].astype(BF16)
                qd = (qc * jnp.exp(cm)).astype(BF16)
                kd = (kc * jnp.exp(-cm)).astype(BF16)
                ke = (kc * jnp.exp(tot - cm)).astype(BF16)
                a = jnp.where(mask, _dot(qd, kd, NT), 0.0)
                o_s[sl, :] += _dot(a.astype(BF16), vb, NN) + _dot(qd, st.astype(BF16), NT)
                return st * jnp.exp(tot) + _dot(vb, ke, TN)

            lax.fori_loop(0, nc, step, jnp.zeros((HEAD, HEAD), F32))
        o = o_s[...]
        o_ref[...] = o
        nrm = o * lax.rsqrt(jnp.mean(o * o, axis=-1, keepdims=True) + LN_EPS)
        hog = hog_ref[...]
        ya_ref[...] = (nrm * ng_ref[...] * hog * _sigmoid(hog)).astype(BF16)

    def col(part):
        return pl.BlockSpec((s_, HEAD), lambda h, b, part=part: (b, part * h_ + h))

    tab = pl.BlockSpec((2, HEAD), lambda h, b: (0, h))
    out = pl.BlockSpec((s_, HEAD), lambda h, b: (b, h))
    return pl.pallas_call(
        body, name=name, grid=(h_, b_),
        in_specs=[col(0), col(1), col(2), col(3), col(4), tab, tab,
                  pl.BlockSpec((1, HEAD), lambda h, b: (0, h))],
        out_specs=[out, out],
        out_shape=[jax.ShapeDtypeStruct((b_ * s_, d_), BF16), jax.ShapeDtypeStruct((b_ * s_, d_), F32)],
        scratch_shapes=[pltpu.VMEM((s_, HEAD), F32)] * 4,
        compiler_params=_params(),
    )(proj, proj, proj, proj, proj, lbf, lbb, ng)


def _hgrn_bwd(proj, lbf, lbb, ng, o_sum, dya, b_, s_, d_, name):
    h_ = d_ // HEAD
    nc = s_ // CHUNK

    def body(hq_ref, hff_ref, hfb_ref, hi_ref, hog_ref, lbf_ref, lbb_ref, ng_ref, o_ref, dya_ref,
             dhq_ref, dhff_ref, dhfb_ref, dhi_ref, dhog_ref, dng_ref, dlbf_ref, dlbb_ref,
             q_s, k_s, cum_s, do_s, dq_s, dv_s, db_s, dk_s, st_s):
        b = pl.program_id(1)

        @pl.when(b == 0)
        def _():
            dng_ref[...] = jnp.zeros_like(dng_ref)
            dlbf_ref[...] = jnp.zeros_like(dlbf_ref)
            dlbb_ref[...] = jnp.zeros_like(dlbb_ref)

        row = lax.broadcasted_iota(jnp.int32, (s_, HEAD), 0) % CHUNK
        crow = lax.broadcasted_iota(jnp.int32, (CHUNK, HEAD), 0)
        hq = hq_ref[...]
        sq = _sigmoid(hq)
        q_s[...] = hq * sq
        o = o_ref[...]
        rinv = lax.rsqrt(jnp.mean(o * o, axis=-1, keepdims=True) + LN_EPS)
        nrm = o * rinv
        hog = hog_ref[...]
        so = _sigmoid(hog)
        gain = ng_ref[...]
        dy = dya_ref[...]
        dhog_ref[...] = (dy * nrm * gain * _dsilu(hog, so)).astype(BF16)
        dng_ref[...] += jnp.sum(dy * nrm * hog * so, axis=0, keepdims=True)
        dn = dy * gain * hog * so
        do_s[...] = rinv * (dn - nrm * jnp.mean(dn * nrm, axis=-1, keepdims=True))
        dq_s[...] = jnp.zeros_like(dq_s)
        dv_s[...] = jnp.zeros_like(dv_s)

        for reverse, hf_ref, lb_ref, dhf_ref, dlb_ref in (
                (False, hff_ref, lbf_ref, dhff_ref, dlbf_ref), (True, hfb_ref, lbb_ref, dhfb_ref, dlbb_ref)):
            tab = lb_ref[...]
            lb = _lower_bound(tab)
            sf = _sigmoid(hf_ref[...])
            f = lb + (1.0 - lb) * sf
            k_s[...] = 1.0 - f
            cum_s[...] = _chunk_scan(jnp.log(f), row, reverse, s_)
            mask = _tri_mask(reverse)
            last = 0 if reverse else CHUNK - 1

            def chunk(idx, reverse=reverse):
                sl = pl.ds(pl.multiple_of(idx * CHUNK, CHUNK), CHUNK)
                cm = cum_s[sl, :]
                tot = cm[0:1, :] if reverse else cm[CHUNK - 1:CHUNK, :]
                return sl, cm, tot

            def fstep(n, st, reverse=reverse, chunk=chunk):
                idx = (nc - 1 - n) if reverse else n
                sl, cm, tot = chunk(idx)
                st_s[idx] = st
                ke = (k_s[sl, :] * jnp.exp(tot - cm)).astype(BF16)
                return st * jnp.exp(tot) + _dot(hi_ref[sl, :].astype(BF16), ke, TN)

            lax.fori_loop(0, nc, fstep, jnp.zeros((HEAD, HEAD), F32))

            def bstep(n, dst, reverse=reverse, chunk=chunk, mask=mask, last=last):
                idx = n if reverse else (nc - 1 - n)
                sl, cm, tot = chunk(idx)
                eb, enb, ee, dec = jnp.exp(cm), jnp.exp(-cm), jnp.exp(tot - cm), jnp.exp(tot)
                qc, kc = q_s[sl, :], k_s[sl, :]
                qd, kd, ke = qc * eb, kc * enb, kc * ee
                qdb, kdb, keb = qd.astype(BF16), kd.astype(BF16), ke.astype(BF16)
                vb = hi_ref[sl, :].astype(BF16)
                dob = do_s[sl, :].astype(BF16)
                st0 = st_s[idx]
                dstb = dst.astype(BF16)
                a = jnp.where(mask, _dot(qdb, kdb, NT), 0.0).astype(BF16)
                da = jnp.where(mask, _dot(dob, vb, NT), 0.0).astype(BF16)
                dqd = _dot(da, kdb, NN) + _dot(dob, st0.astype(BF16), NN)
                dkd = _dot(da, qdb, TN)
                dv = _dot(a, dob, TN) + _dot(keb, dstb, NT)
                dke = _dot(vb, dstb, NN)
                ddec = jnp.sum(dst * st0, axis=0, keepdims=True)
                dtot = jnp.sum(dke * ke, axis=0, keepdims=True) + ddec * dec
                db = dqd * qd - dkd * kd - dke * ke
                db_s[sl, :] = db + jnp.where(crow == last, dtot, 0.0)
                dk_s[sl, :] = dkd * enb + dke * ee
                dq_s[sl, :] += dqd * eb
                dv_s[sl, :] += dv
                return dst * dec + _dot(dob, qdb, TN)

            lax.fori_loop(0, nc, bstep, jnp.zeros((HEAD, HEAD), F32))
            dlogf = _chunk_scan(db_s[...], row, not reverse, s_)
            df = dlogf / f - dk_s[...]
            dhf_ref[...] = (df * (1.0 - lb) * sf * (1.0 - sf)).astype(BF16)
            dlb = jnp.sum(df * (1.0 - sf), axis=0, keepdims=True) * lb * (1.0 - lb)
            dlb_ref[0:1, :] += dlb
            dlb_ref[1:2, :] -= dlb

        dhq_ref[...] = (dq_s[...] * _dsilu(hq, sq)).astype(BF16)
        dhi_ref[...] = dv_s[...].astype(BF16)

    def col(part):
        return pl.BlockSpec((s_, HEAD), lambda h, b, part=part: (b, part * h_ + h))

    tab = pl.BlockSpec((2, HEAD), lambda h, b: (0, h))
    vec = pl.BlockSpec((1, HEAD), lambda h, b: (0, h))
    blk = pl.BlockSpec((s_, HEAD), lambda h, b: (b, h))
    act = jax.ShapeDtypeStruct((b_ * s_, d_), BF16)
    return pl.pallas_call(
        body, name=name, grid=(h_, b_),
        in_specs=[col(0), col(1), col(2), col(3), col(4), tab, tab, vec, blk, blk],
        out_specs=[blk] * 5 + [vec, tab, tab],
        out_shape=[act] * 5 + [jax.ShapeDtypeStruct((1, d_), F32), jax.ShapeDtypeStruct((2, d_), F32),
                               jax.ShapeDtypeStruct((2, d_), F32)],
        scratch_shapes=[pltpu.VMEM((s_, HEAD), F32)] * 8 + [pltpu.VMEM((nc, HEAD, HEAD), F32)],
        compiler_params=_params(),
    )(proj, proj, proj, proj, proj, lbf, lbb, ng, o_sum, dya)


FWD_BLOCK = 128
BWD_BLOCK = 128


def _block_scan(x, row, reverse, size, blk):
    s = 1
    while s < blk:
        if reverse:
            x = x + jnp.where(row < blk - s, pltpu.roll(x, size - s, 0), 0.0)
        else:
            x = x + jnp.where(row >= s, pltpu.roll(x, s, 0), 0.0)
        s *= 2
    return x


def _block_exps(l, reverse):
    blk = l.shape[0]
    half = blk // 2
    first = lax.broadcasted_iota(jnp.int32, (blk, HEAD), 0) < half
    q1, q3 = half // 2, half + half // 2
    if reverse:
        rho1, rho2, lh, ltot = l[q1:q1 + 1], l[q3:q3 + 1], l[half:half + 1], l[0:1]
    else:
        rho1, rho2, lh, ltot = l[q1 - 1:q1], l[q3 - 1:q3], l[half - 1:half], l[blk - 1:blk]
    ref = jnp.where(first, rho1, rho2)
    query_half = first if reverse else jnp.logical_not(first)
    e2 = jnp.where(query_half, jnp.exp(jnp.minimum(l - lh, 0.0)), 0.0)
    e1 = jnp.where(query_half, 0.0, jnp.exp(jnp.minimum(lh - l, 0.0)))
    return (jnp.exp(l - ref), jnp.exp(ref - l), e2, e1, jnp.exp(l), jnp.exp(ltot - l),
            jnp.exp(ltot), jnp.exp(lh), jnp.exp(ltot - lh))


def _half_mask(reverse, blk):
    r = lax.broadcasted_iota(jnp.int32, (blk, blk), 0)
    c = lax.broadcasted_iota(jnp.int32, (blk, blk), 1)
    same = (r < blk // 2) == (c < blk // 2)
    return same & ((c >= r) if reverse else (r >= c))


def _hgrn_fwd(proj, lbf, lbb, ng, b_, s_, d_, name):
    h_ = d_ // HEAD
    BLOCK = min(FWD_BLOCK, s_)
    nb = s_ // BLOCK

    def body(hq_ref, hff_ref, hfb_ref, hi_ref, hog_ref, lbf_ref, lbb_ref, ng_ref, ya_ref, o_ref,
             q_s, k_s, l_s, of_s, oi_s, qd_s, u_s, st_s, dec_s):
        row = lax.broadcasted_iota(jnp.int32, (s_, HEAD), 0) % BLOCK
        hq = hq_ref[...]
        q_s[...] = hq * _sigmoid(hq)
        for reverse, hf_ref, lb_ref in ((False, hff_ref, lbf_ref), (True, hfb_ref, lbb_ref)):
            lb = _lower_bound(lb_ref[...])
            f = lb + (1.0 - lb) * _sigmoid(hf_ref[...])
            k_s[...] = 1.0 - f
            l_s[...] = _block_scan(jnp.log(f), row, reverse, s_, BLOCK)
            mask = _half_mask(reverse, BLOCK)

            def inside(n, carry, reverse=reverse, mask=mask):
                sl = pl.ds(pl.multiple_of(n * BLOCK, BLOCK), BLOCK)
                eq, ek, e2, e1, el, ee, dec, _, _ = _block_exps(l_s[sl, :], reverse)
                qc, kc = q_s[sl, :], k_s[sl, :]
                vb = hi_ref[sl, :].astype(BF16)
                a = jnp.where(mask, _dot((qc * eq).astype(BF16), (kc * ek).astype(BF16), NT), 0.0)
                a = a + _dot((qc * e2).astype(BF16), (kc * e1).astype(BF16), NT)
                oi_s[sl, :] = _dot(a.astype(BF16), vb, NN)
                qd_s[sl, :] = (qc * el).astype(BF16)
                u_s[n] = _dot(vb, (kc * ee).astype(BF16), TN)
                dec_s[n] = jnp.broadcast_to(dec, (8, HEAD))
                return carry

            lax.fori_loop(0, nb, inside, 0, unroll=4)

            def carry_state(n, st, reverse=reverse):
                idx = (nb - 1 - n) if reverse else n
                st_s[idx] = st.astype(BF16)
                return st * dec_s[idx][0:1, :] + u_s[idx]

            lax.fori_loop(0, nb, carry_state, jnp.zeros((HEAD, HEAD), F32))

            def across(n, carry, reverse=reverse):
                sl = pl.ds(pl.multiple_of(n * BLOCK, BLOCK), BLOCK)
                o_dir = oi_s[sl, :] + _dot(qd_s[sl, :], st_s[n], NT)
                if not reverse:
                    of_s[sl, :] = o_dir
                else:
                    o = of_s[sl, :] + o_dir
                    o_ref[sl, :] = o
                    nrm = o * lax.rsqrt(jnp.mean(o * o, axis=-1, keepdims=True) + LN_EPS)
                    hog = hog_ref[sl, :]
                    ya_ref[sl, :] = (nrm * ng_ref[...] * hog * _sigmoid(hog)).astype(BF16)
                return carry

            lax.fori_loop(0, nb, across, 0, unroll=8)

    def col(part):
        return pl.BlockSpec((s_, HEAD), lambda h, b, part=part: (b, part * h_ + h))

    tab = pl.BlockSpec((2, HEAD), lambda h, b: (0, h))
    out = pl.BlockSpec((s_, HEAD), lambda h, b: (b, h))
    return pl.pallas_call(
        body, name=name, grid=(h_, b_),
        in_specs=[col(0), col(1), col(2), col(3), col(4), tab, tab,
                  pl.BlockSpec((1, HEAD), lambda h, b: (0, h))],
        out_specs=[out, out],
        out_shape=[jax.ShapeDtypeStruct((b_ * s_, d_), BF16), jax.ShapeDtypeStruct((b_ * s_, d_), F32)],
        scratch_shapes=[pltpu.VMEM((s_, HEAD), F32)] * 5 + [
            pltpu.VMEM((s_, HEAD), BF16), pltpu.VMEM((nb, HEAD, HEAD), F32), pltpu.VMEM((nb, HEAD, HEAD), BF16),
            pltpu.VMEM((nb, 8, HEAD), F32)],
        compiler_params=_params(),
    )(proj, proj, proj, proj, proj, lbf, lbb, ng)


def _hgrn_bwd(proj, lbf, lbb, ng, o_sum, dya, b_, s_, d_, name):
    h_ = d_ // HEAD
    BLOCK = min(BWD_BLOCK, s_)
    nb = s_ // BLOCK

    def body(hq_ref, hff_ref, hfb_ref, hi_ref, hog_ref, lbf_ref, lbb_ref, ng_ref, o_ref, dya_ref,
             dhq_ref, dhff_ref, dhfb_ref, dhi_ref, dhog_ref, dng_ref, dlbf_ref, dlbb_ref,
             q_s, k_s, l_s, do_s, dq_s, dv_s, dl_s, dk_s, u_s, w_s, st_s, dst_s, dec_s):
        b = pl.program_id(1)

        @pl.when(b == 0)
        def _():
            dng_ref[...] = jnp.zeros_like(dng_ref)
            dlbf_ref[...] = jnp.zeros_like(dlbf_ref)
            dlbb_ref[...] = jnp.zeros_like(dlbb_ref)

        row = lax.broadcasted_iota(jnp.int32, (s_, HEAD), 0) % BLOCK
        brow = lax.broadcasted_iota(jnp.int32, (BLOCK, HEAD), 0)
        hq = hq_ref[...]
        sq = _sigmoid(hq)
        q_s[...] = hq * sq
        o = o_ref[...]
        rinv = lax.rsqrt(jnp.mean(o * o, axis=-1, keepdims=True) + LN_EPS)
        nrm = o * rinv
        hog = hog_ref[...]
        so = _sigmoid(hog)
        gain = ng_ref[...]
        dy = dya_ref[...]
        dhog_ref[...] = (dy * nrm * gain * _dsilu(hog, so)).astype(BF16)
        dng_ref[...] += jnp.sum(dy * nrm * hog * so, axis=0, keepdims=True)
        dn = dy * gain * hog * so
        do_s[...] = rinv * (dn - nrm * jnp.mean(dn * nrm, axis=-1, keepdims=True))

        for reverse, hf_ref, lb_ref, dhf_ref, dlb_ref in (
                (False, hff_ref, lbf_ref, dhff_ref, dlbf_ref), (True, hfb_ref, lbb_ref, dhfb_ref, dlbb_ref)):
            lb = _lower_bound(lb_ref[...])
            sf = _sigmoid(hf_ref[...])
            f = lb + (1.0 - lb) * sf
            k_s[...] = 1.0 - f
            l_s[...] = _block_scan(jnp.log(f), row, reverse, s_, BLOCK)
            mask = _half_mask(reverse, BLOCK)
            total_row = 0 if reverse else BLOCK - 1
            key_end = BLOCK // 2 if reverse else BLOCK // 2 - 1

            def prepare(n, carry, reverse=reverse):
                sl = pl.ds(pl.multiple_of(n * BLOCK, BLOCK), BLOCK)
                _, _, _, _, el, ee, dec, _, _ = _block_exps(l_s[sl, :], reverse)
                vb = hi_ref[sl, :].astype(BF16)
                u_s[n] = _dot(vb, (k_s[sl, :] * ee).astype(BF16), TN)
                w_s[n] = _dot(do_s[sl, :].astype(BF16), (q_s[sl, :] * el).astype(BF16), TN)
                dec_s[n] = jnp.broadcast_to(dec, (8, HEAD))
                return carry

            lax.fori_loop(0, nb, prepare, 0, unroll=4)

            def carry_state(n, st, reverse=reverse):
                idx = (nb - 1 - n) if reverse else n
                st_s[idx] = st
                return st * dec_s[idx][0:1, :] + u_s[idx]

            lax.fori_loop(0, nb, carry_state, jnp.zeros((HEAD, HEAD), F32))

            def carry_grad(n, dst, reverse=reverse):
                idx = n if reverse else (nb - 1 - n)
                dst_s[idx] = dst
                return dst * dec_s[idx][0:1, :] + w_s[idx]

            lax.fori_loop(0, nb, carry_grad, jnp.zeros((HEAD, HEAD), F32))

            def inside(n, carry, reverse=reverse, mask=mask, total_row=total_row, key_end=key_end):
                sl = pl.ds(pl.multiple_of(n * BLOCK, BLOCK), BLOCK)
                eq, ek, e2, e1, el, ee, dec, dec_key, dec_query = _block_exps(l_s[sl, :], reverse)
                qc, kc = q_s[sl, :], k_s[sl, :]
                vb = hi_ref[sl, :].astype(BF16)
                dob = do_s[sl, :].astype(BF16)
                qt, kt, q2, k1 = ((qc * eq).astype(BF16), (kc * ek).astype(BF16),
                                  (qc * e2).astype(BF16), (kc * e1).astype(BF16))
                kend = kc * ee
                st0, dst1 = st_s[n], dst_s[n]
                dstb = dst1.astype(BF16)
                a = jnp.where(mask, _dot(qt, kt, NT), 0.0) + _dot(q2, k1, NT)
                da = _dot(dob, vb, NT)
                dab = da.astype(BF16)
                dad = jnp.where(mask, da, 0.0).astype(BF16)
                dqt, dkt = _dot(dad, kt, NN), _dot(dad, qt, TN)
                dq2, dk1 = _dot(dab, k1, NN), _dot(dab, q2, TN)
                dqd = _dot(dob, st0.astype(BF16), NN)
                dke = _dot(vb, dstb, NN)
                dv = _dot(a.astype(BF16), dob, TN) + _dot(kend.astype(BF16), dstb, NT)
                dq = dqt * eq + dq2 * e2 + dqd * el
                dk = dkt * ek + dk1 * e1 + dke * ee
                dtot = jnp.sum(dke * kend, axis=0, keepdims=True) + jnp.sum(dst1 * st0, axis=0, keepdims=True) * dec
                st_mid = st0 * dec_key + _dot(vb, k1, TN)
                dst_mid = dst1 * dec_query + _dot(dob, q2, TN)
                dmid = jnp.sum(dst_mid * st_mid, axis=0, keepdims=True)
                dl_s[sl, :] = (qc * dq - kc * dk + jnp.where(brow == total_row, dtot, 0.0)
                               + jnp.where(brow == key_end, dmid, 0.0))
                dk_s[sl, :] = dk
                if not reverse:
                    dq_s[sl, :] = dq
                    dv_s[sl, :] = dv
                else:
                    hqc = hq_ref[sl, :]
                    dhq_ref[sl, :] = ((dq_s[sl, :] + dq) * _dsilu(hqc, _sigmoid(hqc))).astype(BF16)
                    dhi_ref[sl, :] = (dv_s[sl, :] + dv).astype(BF16)
                return carry

            lax.fori_loop(0, nb, inside, 0, unroll=4)
            dlogf = _block_scan(dl_s[...], row % (BLOCK // 2), not reverse, s_, BLOCK // 2)
            df = dlogf / f - dk_s[...]
            dhf_ref[...] = (df * (1.0 - lb) * sf * (1.0 - sf)).astype(BF16)
            dlb = jnp.sum(df * (1.0 - sf), axis=0, keepdims=True) * lb * (1.0 - lb)
            dlb_ref[0:1, :] += dlb
            dlb_ref[1:2, :] -= dlb

    def col(part):
        return pl.BlockSpec((s_, HEAD), lambda h, b, part=part: (b, part * h_ + h))

    tab = pl.BlockSpec((2, HEAD), lambda h, b: (0, h))
    vec = pl.BlockSpec((1, HEAD), lambda h, b: (0, h))
    blk = pl.BlockSpec((s_, HEAD), lambda h, b: (b, h))
    act = jax.ShapeDtypeStruct((b_ * s_, d_), BF16)
    state = pltpu.VMEM((nb, HEAD, HEAD), F32)
    return pl.pallas_call(
        body, name=name, grid=(h_, b_),
        in_specs=[col(0), col(1), col(2), col(3), col(4), tab, tab, vec, blk, blk],
        out_specs=[blk] * 5 + [vec, tab, tab],
        out_shape=[act] * 5 + [jax.ShapeDtypeStruct((1, d_), F32), jax.ShapeDtypeStruct((2, d_), F32),
                               jax.ShapeDtypeStruct((2, d_), F32)],
        scratch_shapes=[pltpu.VMEM((s_, HEAD), F32)] * 8 + [state] * 4 + [pltpu.VMEM((nb, 8, HEAD), F32)],
        compiler_params=_params(),
    )(proj, proj, proj, proj, proj, lbf, lbb, ng, o_sum, dya)


def _rope_tables(s_):
    half = ROPE_DIM // 2
    inv_freq = ROPE_THETA ** (-jnp.arange(0, ROPE_DIM, 2, dtype=F32) / ROPE_DIM)
    ang = jnp.arange(s_, dtype=F32)[:, None] * inv_freq
    cos, sin = jnp.cos(ang), jnp.sin(ang)
    zeros = jnp.zeros((s_, HEAD - ROPE_DIM), F32)
    zh = jnp.zeros((s_, half), F32)
    c = jnp.concatenate([cos, cos, jnp.ones((s_, HEAD - ROPE_DIM), F32)], axis=1)
    s1 = jnp.concatenate([-sin, zh, zeros], axis=1)
    s2 = jnp.concatenate([zh, sin, zeros], axis=1)
    return c, s1, s2


def _rope(t, c, s1, s2):
    half = ROPE_DIM // 2
    return t * c + pltpu.roll(t, HEAD - half, 1) * s1 + pltpu.roll(t, half, 1) * s2


def _rope_bwd(dt, c, s1, s2):
    half = ROPE_DIM // 2
    return dt * c + pltpu.roll(dt * s1, half, 1) + pltpu.roll(dt * s2, HEAD - half, 1)


def _window_mask(r0, qb, wk, seg):
    row = lax.broadcasted_iota(jnp.int32, (qb, wk), 0)
    col = lax.broadcasted_iota(jnp.int32, (qb, wk), 1)
    kj = r0 - ATTN_HALF + col
    return (col - row >= 0) & (col - row <= 2 * ATTN_HALF) & (kj >= 0) & (kj < seg)


def _attn_fwd(proj, tabs, b_, s_, col0, dil, name):
    seg = s_ // dil
    qb = min(128, seg)
    nq, wk = seg // qb, qb + 2 * ATTN_HALF
    scale = HEAD ** -0.5

    def body(q_ref, k_ref, v_ref, c_ref, s1_ref, s2_ref, o_ref, lse_ref, q_s, k_s, v_s):
        k_s[...] = jnp.zeros_like(k_s)
        v_s[...] = jnp.zeros_like(v_s)

        def residue(r, carry):
            cls = pl.ds(r, seg, stride=dil)
            c, s1, s2 = c_ref[cls, :], s1_ref[cls, :], s2_ref[cls, :]
            q_s[...] = _rope(q_ref[cls, :], c, s1, s2).astype(BF16)
            k_s[ATTN_HALF:ATTN_HALF + seg, :] = _rope(k_ref[cls, :], c, s1, s2).astype(BF16)
            v_s[ATTN_HALF:ATTN_HALF + seg, :] = v_ref[cls, :].astype(BF16)

            def step(i, carry):
                r0 = pl.multiple_of(i * qb, qb)
                sc = _dot(q_s[pl.ds(r0, qb), :], k_s[pl.ds(r0, wk), :], NT) * scale
                sc = jnp.where(_window_mask(r0, qb, wk, seg), sc, NEG_INF)
                m = jnp.max(sc, axis=-1, keepdims=True)
                p = jnp.exp(sc - m)
                den = jnp.sum(p, axis=-1, keepdims=True)
                rows = pl.ds(r + r0 * dil, qb, stride=dil)
                o_ref[rows, :] = _dot(p.astype(BF16), v_s[pl.ds(r0, wk), :], NN) / den
                lse_ref[rows, :] = jnp.broadcast_to(m + jnp.log(den), (qb, HEAD))
                return carry

            return lax.fori_loop(0, nq, step, carry)

        lax.fori_loop(0, dil, residue, 0)

    def col(part):
        return pl.BlockSpec((s_, HEAD), lambda b, h, part=part: (b, col0 + part * ATTN_HEADS + h))

    tab = pl.BlockSpec((s_, HEAD), lambda b, h: (0, 0))
    out = pl.BlockSpec((s_, HEAD), lambda b, h: (b, h))
    shape = jax.ShapeDtypeStruct((b_ * s_, ATTN_OUT), F32)
    return pl.pallas_call(
        body, name=name, grid=(b_, ATTN_HEADS),
        in_specs=[col(0), col(1), col(2), tab, tab, tab],
        out_specs=[out, out],
        out_shape=[shape, shape],
        scratch_shapes=[pltpu.VMEM((seg, HEAD), BF16), pltpu.VMEM((seg + 2 * ATTN_HALF, HEAD), BF16),
                        pltpu.VMEM((seg + 2 * ATTN_HALF, HEAD), BF16)],
        compiler_params=_params(),
    )(proj, proj, proj, *tabs)


def _attn_bwd(proj, tabs, dog, cg, lse, b_, s_, col0, dil, name):
    seg = s_ // dil
    qb = min(128, seg)
    nq, wk = seg // qb, qb + 2 * ATTN_HALF
    scale = HEAD ** -0.5

    def body(q_ref, k_ref, v_ref, c_ref, s1_ref, s2_ref, do_ref, cg_ref, lse_ref, dq_ref, dk_ref, dv_ref,
             q_s, k_s, v_s, do_s, cl_s, dk_s, dv_s):
        k_s[...] = jnp.zeros_like(k_s)
        v_s[...] = jnp.zeros_like(v_s)

        def residue(r, carry):
            cls = pl.ds(r, seg, stride=dil)
            c, s1, s2 = c_ref[cls, :], s1_ref[cls, :], s2_ref[cls, :]
            q_s[...] = _rope(q_ref[cls, :], c, s1, s2).astype(BF16)
            k_s[ATTN_HALF:ATTN_HALF + seg, :] = _rope(k_ref[cls, :], c, s1, s2).astype(BF16)
            v_s[ATTN_HALF:ATTN_HALF + seg, :] = v_ref[cls, :].astype(BF16)
            do_s[...] = do_ref[cls, :].astype(BF16)
            cl_s[0] = cg_ref[cls, :]
            cl_s[1] = lse_ref[cls, :]
            dk_s[...] = jnp.zeros_like(dk_s)
            dv_s[...] = jnp.zeros_like(dv_s)

            def step(i, carry):
                r0 = pl.multiple_of(i * qb, qb)
                rows, win = pl.ds(r0, qb), pl.ds(r0, wk)
                qc, kw, vw = q_s[rows, :], k_s[win, :], v_s[win, :]
                sc = _dot(qc, kw, NT) * scale
                p = jnp.where(_window_mask(r0, qb, wk, seg), jnp.exp(sc - cl_s[1, rows, 0:1]), 0.0)
                dob = do_s[rows, :]
                dp = _dot(dob, vw, NT)
                ds = (p * (dp + cl_s[0, rows, 0:1]) * scale).astype(BF16)
                out = pl.ds(r + r0 * dil, qb, stride=dil)
                dq_ref[out, :] = _rope_bwd(_dot(ds, kw, NN), c_ref[out, :], s1_ref[out, :], s2_ref[out, :])
                dk_s[win, :] += _dot(ds, qc, TN)
                dv_s[win, :] += _dot(p.astype(BF16), dob, TN)
                return carry

            carry = lax.fori_loop(0, nq, step, carry)
            dk_ref[cls, :] = _rope_bwd(dk_s[ATTN_HALF:ATTN_HALF + seg, :], c, s1, s2)
            dv_ref[cls, :] = dv_s[ATTN_HALF:ATTN_HALF + seg, :]
            return carry

        lax.fori_loop(0, dil, residue, 0)

    def col(part):
        return pl.BlockSpec((s_, HEAD), lambda b, h, part=part: (b, col0 + part * ATTN_HEADS + h))

    tab = pl.BlockSpec((s_, HEAD), lambda b, h: (0, 0))
    out = pl.BlockSpec((s_, HEAD), lambda b, h: (b, h))
    shape = jax.ShapeDtypeStruct((b_ * s_, ATTN_OUT), F32)
    pad = seg + 2 * ATTN_HALF
    return pl.pallas_call(
        body, name=name, grid=(b_, ATTN_HEADS),
        in_specs=[col(0), col(1), col(2), tab, tab, tab, out, out, out],
        out_specs=[out, out, out],
        out_shape=[shape, shape, shape],
        scratch_shapes=[pltpu.VMEM((seg, HEAD), BF16), pltpu.VMEM((pad, HEAD), BF16), pltpu.VMEM((pad, HEAD), BF16),
                        pltpu.VMEM((seg, HEAD), BF16), pltpu.VMEM((2, seg, HEAD), F32),
                        pltpu.VMEM((pad, HEAD), F32), pltpu.VMEM((pad, HEAD), F32)],
        compiler_params=_params(),
    )(proj, proj, proj, *tabs, dog, cg, lse)


def _group_weights(lses):
    m = jnp.maximum(jnp.maximum(lses[0], lses[1]), lses[2])
    es = [jnp.exp(l - m) for l in lses]
    den = es[0] + es[1] + es[2]
    return [e / den for e in es]


def _combine_fwd(outs, lses, name):
    t_, w_ = outs[0].shape
    tm = _tile(t_, 512, 8)
    ng = len(outs)

    def body(*refs):
        ws = _group_weights([r[...] for r in refs[ng:2 * ng]])
        acc = ws[0] * refs[0][...]
        for g in range(1, ng):
            acc = acc + ws[g] * refs[g][...]
        refs[2 * ng][...] = acc.astype(BF16)

    row = pl.BlockSpec((tm, w_), lambda i: (i, 0))
    return pl.pallas_call(
        body, name=name, grid=(t_ // tm,), in_specs=[row] * (2 * ng), out_specs=row,
        out_shape=jax.ShapeDtypeStruct((t_, w_), BF16), compiler_params=_params(),
    )(*outs, *lses)


def _combine_bwd(dob, outs, lses, name):
    t_, w_ = outs[0].shape
    tm = _tile(t_, 512, 8)
    ng = len(outs)

    def body(*refs):
        do = refs[0][...]
        os_ = [r[...] for r in refs[1:1 + ng]]
        ws = _group_weights([r[...] for r in refs[1 + ng:1 + 2 * ng]])
        o = ws[0] * os_[0]
        for g in range(1, ng):
            o = o + ws[g] * os_[g]
        prod = do * o
        heads = [jnp.broadcast_to(jnp.sum(prod[:, h * HEAD:(h + 1) * HEAD], axis=-1, keepdims=True), (tm, HEAD))
                 for h in range(w_ // HEAD)]
        tot = jnp.concatenate(heads, axis=1)
        for g in range(ng):
            refs[1 + 2 * ng + g][...] = ws[g] * do
            refs[1 + 3 * ng + g][...] = -ws[g] * tot

    row = pl.BlockSpec((tm, w_), lambda i: (i, 0))
    shape = jax.ShapeDtypeStruct((t_, w_), F32)
    res = pl.pallas_call(
        body, name=name, grid=(t_ // tm,), in_specs=[row] * (1 + 2 * ng), out_specs=[row] * (2 * ng),
        out_shape=[shape] * (2 * ng), compiler_params=_params(),
    )(dob, *outs, *lses)
    return res[:ng], res[ng:]


def _adam_update(w, g, m, v):
    m = ADAM_B1 * m + (1.0 - ADAM_B1) * g
    v = ADAM_B2 * v + (1.0 - ADAM_B2) * (g * g)
    m_hat = m / (1.0 - ADAM_B1 ** ADAM_STEP)
    v_hat = v / (1.0 - ADAM_B2 ** ADAM_STEP)
    return -ADAM_LR * (m_hat / (jnp.sqrt(v_hat) + ADAM_EPS) + ADAM_WD * w), m, v


def _adam(w, g, m, v, name):
    r_, c_ = w.shape
    tr = _tile(r_, 256, 8)

    def body(w_ref, g_ref, m_ref, v_ref, d_ref, mo_ref, vo_ref):
        d_ref[...], mo_ref[...], vo_ref[...] = _adam_update(w_ref[...], g_ref[...], m_ref[...], v_ref[...])

    blk = pl.BlockSpec((tr, c_), lambda i: (i, 0))
    shape = jax.ShapeDtypeStruct((r_, c_), F32)
    return pl.pallas_call(
        body, name=name, grid=(r_ // tr,), in_specs=[blk] * 4, out_specs=[blk] * 3,
        out_shape=[shape] * 3, compiler_params=_params(),
    )(w, g, m, v)


def _sum_partials(recv, name):
    n_, r_, c_ = recv.shape
    tr = _tile(r_, 128, 16)

    def body(p_ref, o_ref):
        acc = p_ref[0].astype(F32)
        for i in range(1, n_):
            acc = acc + p_ref[i].astype(F32)
        o_ref[...] = acc

    return pl.pallas_call(
        body, name=name, grid=(r_ // tr,),
        in_specs=[pl.BlockSpec((n_, tr, c_), lambda i: (0, i, 0))],
        out_specs=pl.BlockSpec((tr, c_), lambda i: (i, 0)),
        out_shape=jax.ShapeDtypeStruct((r_, c_), F32), compiler_params=_params(),
    )(recv)


def _small_sum_adam(parts, w, m, v, name):
    n_, r_, c_ = parts.shape

    def body(p_ref, w_ref, m_ref, v_ref, g_ref, d_ref, mo_ref, vo_ref):
        g = p_ref[0]
        for i in range(1, n_):
            g = g + p_ref[i]
        g_ref[...] = g
        d_ref[...], mo_ref[...], vo_ref[...] = _adam_update(w_ref[...], g, m_ref[...], v_ref[...])

    shape = jax.ShapeDtypeStruct((r_, c_), F32)
    return pl.pallas_call(body, name=name, out_shape=[shape] * 4, compiler_params=_params())(parts, w, m, v)


def _my_place():
    x, y, c = lax.axis_index("x"), lax.axis_index("y"), lax.axis_index("c")
    return x, y, c


def _peer(x, y, c, d):
    px = 1 - x if d & 4 else x
    py = 1 - y if d & 2 else y
    pc = 1 - c if d & 1 else c
    return (px, py, pc), 4 * px + 2 * py + pc


def _all_gather(shards, name):
    nw = len(shards)

    def body(*refs):
        ins, outs = refs[:nw], refs[nw:2 * nw]
        send_sems, recv_sems, local_sems = refs[2 * nw:]
        x, y, c = _my_place()
        me = 4 * x + 2 * y + c
        copies = []
        for k in range(nw):
            rows = shards[k].shape[0]
            mine = outs[k].at[pl.ds(pl.multiple_of(me * rows, 16), rows), :]
            local = pltpu.make_async_copy(ins[k], mine, local_sems.at[k])
            local.start()
            copies.append(local)
            for d in range(1, N_DEV):
                place, _ = _peer(x, y, c, d)
                remote = pltpu.make_async_remote_copy(
                    src_ref=ins[k], dst_ref=mine, send_sem=send_sems.at[d - 1, k], recv_sem=recv_sems.at[d - 1, k],
                    device_id=place, device_id_type=MESH)
                remote.start()
                copies.append(remote)
        for cp in copies:
            cp.wait()

    hbm = pl.BlockSpec(memory_space=pl.ANY)
    return pl.pallas_call(
        body, name=name, in_specs=[hbm] * nw, out_specs=[hbm] * nw,
        out_shape=[jax.ShapeDtypeStruct((N_DEV * s.shape[0], s.shape[1]), s.dtype) for s in shards],
        scratch_shapes=[pltpu.SemaphoreType.DMA((N_DEV - 1, nw)), pltpu.SemaphoreType.DMA((N_DEV - 1, nw)),
                        pltpu.SemaphoreType.DMA((nw,))],
    )(*shards)


HBM_SPEC = pl.BlockSpec(memory_space=pltpu.HBM)
SEM_SPEC = pl.BlockSpec(memory_space=pltpu.SEMAPHORE)
EFFECT = pltpu.SideEffectType.DATAFLOW_SIDE_EFFECTING


def _in_hbm(a):
    return pltpu.with_memory_space_constraint(a, pltpu.HBM)


def _token_shape():
    return jax.ShapeDtypeStruct((8, HEAD), F32)


SIBLING = 1
OTHER_CHIPS = (4, 2, 6)


def _rows_of(ref, num, rows):
    return ref.at[pl.ds(pl.multiple_of(num * rows, 16), rows), :]


def _gather_start(shards, name):
    nw = len(shards)
    lands = [lax.empty((N_DEV * s.shape[0], s.shape[1]), s.dtype) for s in shards]
    n_to = 1 + len(OTHER_CHIPS)

    def body(*refs):
        ins, lnd = refs[:nw], refs[nw:2 * nw]
        send, from_sib, from_chips, own = (refs[(2 + i) * nw:(3 + i) * nw] for i in range(4))
        token = refs[8 * nw]
        x, y, c = _my_place()
        me = 4 * x + 2 * y + c
        for k in range(nw):
            mine = _rows_of(lnd[k], me, shards[k].shape[0])
            pltpu.make_async_copy(ins[k], mine, own[k]).start()
            for i, d in enumerate((SIBLING,) + OTHER_CHIPS):
                place, _ = _peer(x, y, c, d)
                pltpu.make_async_remote_copy(
                    src_ref=ins[k], dst_ref=mine, send_sem=send[k].at[i],
                    recv_sem=from_sib[k] if i == 0 else from_chips[k].at[i - 1],
                    device_id=place, device_id_type=MESH).start()
        token[...] = jnp.zeros_like(token)

    dma = pltpu.SemaphoreType.DMA
    sems = [dma((n_to,))] * nw + [dma(())] * nw + [dma((len(OTHER_CHIPS),))] * nw + [dma(())] * nw
    thru = [pltpu.HBM(a.shape, a.dtype) for a in list(shards) + lands]
    res = pl.pallas_call(
        body, name=name, out_shape=(*sems, *thru, _token_shape()),
        in_specs=[HBM_SPEC] * (2 * nw),
        out_specs=(*([SEM_SPEC] * (4 * nw)), *([HBM_SPEC] * (2 * nw)), pl.BlockSpec(memory_space=pltpu.VMEM)),
        input_output_aliases={i: 4 * nw + i for i in range(2 * nw)},
        compiler_params=pltpu.CompilerParams(has_side_effects=EFFECT),
    )(*[_in_hbm(s) for s in shards], *[_in_hbm(l) for l in lands])
    return [dict(send=res[k], from_sib=res[nw + k], from_chips=res[2 * nw + k], own=res[3 * nw + k],
                 src=res[4 * nw + k], land=res[5 * nw + k]) for k in range(nw)], res[6 * nw]


def _gather_forward(pending, after, name):
    rows = pending["src"].shape[0]
    n_fw = len(OTHER_CHIPS)

    def body(land_ref, from_chips, after_ref, fw_send, fw_recv, land_thru):
        x, y, c = _my_place()
        sibling, _ = _peer(x, y, c, SIBLING)
        for j, d in enumerate(OTHER_CHIPS):
            _, num = _peer(x, y, c, d)
            block = _rows_of(land_ref, num, rows)
            pltpu.make_async_remote_copy(
                src_ref=block, dst_ref=block, send_sem=fw_send.at[j], recv_sem=from_chips.at[j],
                device_id=sibling, device_id_type=MESH).wait_recv()
            pltpu.make_async_remote_copy(
                src_ref=block, dst_ref=block, send_sem=fw_send.at[j], recv_sem=fw_recv.at[j],
                device_id=sibling, device_id_type=MESH).start()

    land = pending["land"]
    dma = pltpu.SemaphoreType.DMA
    fw_send, fw_recv, land = pl.pallas_call(
        body, name=name, out_shape=(dma((n_fw,)), dma((n_fw,)), pltpu.HBM(land.shape, land.dtype)),
        in_specs=(HBM_SPEC, SEM_SPEC, pl.BlockSpec(memory_space=pl.ANY)),
        out_specs=(SEM_SPEC, SEM_SPEC, HBM_SPEC), input_output_aliases={0: 2},
        compiler_params=pltpu.CompilerParams(has_side_effects=EFFECT),
    )(land, pending["from_chips"], after)
    return dict(pending, land=land, fw_send=fw_send, fw_recv=fw_recv)


def _gather_wait(pending, name):
    rows = pending["src"].shape[0]

    def body(src_ref, land_ref, send, from_sib, own, fw_send, fw_recv, src_dead, got):
        x, y, c = _my_place()
        me = 4 * x + 2 * y + c
        sibling, sib_num = _peer(x, y, c, SIBLING)
        mine = _rows_of(land_ref, me, rows)
        pltpu.make_async_copy(src_ref, mine, own).wait()
        for i in range(1 + len(OTHER_CHIPS)):
            pltpu.make_async_remote_copy(
                src_ref=src_ref, dst_ref=mine, send_sem=send.at[i], recv_sem=from_sib,
                device_id=sibling, device_id_type=MESH).wait_send()
        theirs = _rows_of(land_ref, sib_num, rows)
        pltpu.make_async_remote_copy(
            src_ref=src_ref, dst_ref=theirs, send_sem=send.at[0], recv_sem=from_sib,
            device_id=sibling, device_id_type=MESH).wait_recv()
        for j, d in enumerate(OTHER_CHIPS):
            _, num = _peer(x, y, c, d)
            sent = _rows_of(land_ref, num, rows)
            _, got_num = _peer(x, y, c, d | SIBLING)
            arrived = _rows_of(land_ref, got_num, rows)
            cp = pltpu.make_async_remote_copy(
                src_ref=sent, dst_ref=arrived, send_sem=fw_send.at[j], recv_sem=fw_recv.at[j],
                device_id=sibling, device_id_type=MESH)
            cp.wait_send()
            cp.wait_recv()

    src, land = pending["src"], pending["land"]
    return pl.pallas_call(
        body, name=name, out_shape=(pltpu.HBM(src.shape, src.dtype), pltpu.HBM(land.shape, land.dtype)),
        in_specs=(HBM_SPEC, HBM_SPEC) + (SEM_SPEC,) * 5,
        out_specs=(HBM_SPEC, HBM_SPEC), input_output_aliases={0: 0, 1: 1},
        compiler_params=pltpu.CompilerParams(has_side_effects=EFFECT),
    )(src, land, pending["send"], pending["from_sib"], pending["own"], pending["fw_send"], pending["fw_recv"])[1]


def _scatter_start(full, name):
    rows, cols = full.shape[0] // N_DEV, full.shape[1]
    land = lax.empty((N_DEV, rows, cols), full.dtype)

    def body(full_ref, land_ref, send, recv, own, full_thru, land_thru, token):
        x, y, c = _my_place()
        me = 4 * x + 2 * y + c
        slab = land_ref.at[me]
        pltpu.make_async_copy(full_ref.at[pl.ds(pl.multiple_of(me * rows, 16), rows), :], slab, own).start()
        for d in range(1, N_DEV):
            place, num = _peer(x, y, c, d)
            pltpu.make_async_remote_copy(
                src_ref=full_ref.at[pl.ds(pl.multiple_of(num * rows, 16), rows), :], dst_ref=slab,
                send_sem=send.at[d - 1], recv_sem=recv.at[d - 1], device_id=place, device_id_type=MESH).start()
        token[...] = jnp.zeros_like(token)

    res = pl.pallas_call(
        body, name=name,
        out_shape=(pltpu.SemaphoreType.DMA((N_DEV - 1,)), pltpu.SemaphoreType.DMA((N_DEV - 1,)),
                   pltpu.SemaphoreType.DMA(()),
                   pltpu.HBM(full.shape, full.dtype), pltpu.HBM(land.shape, land.dtype), _token_shape()),
        in_specs=(HBM_SPEC, HBM_SPEC),
        out_specs=(SEM_SPEC, SEM_SPEC, SEM_SPEC, HBM_SPEC, HBM_SPEC, pl.BlockSpec(memory_space=pltpu.VMEM)),
        input_output_aliases={0: 3, 1: 4},
        compiler_params=pltpu.CompilerParams(has_side_effects=EFFECT),
    )(_in_hbm(full), _in_hbm(land))
    return dict(send=res[0], recv=res[1], own=res[2], src=res[3], land=res[4]), res[5]


def _scatter_wait(pending, after, name):
    rows = pending["land"].shape[1]

    def body(src_ref, land_ref, send, recv, own, after_ref, src_dead, got):
        x, y, c = _my_place()
        me = 4 * x + 2 * y + c
        pltpu.make_async_copy(src_ref.at[pl.ds(pl.multiple_of(me * rows, 16), rows), :], land_ref.at[me], own).wait()
        for d in range(1, N_DEV):
            place, num = _peer(x, y, c, d)
            cp = pltpu.make_async_remote_copy(
                src_ref=src_ref.at[pl.ds(pl.multiple_of(num * rows, 16), rows), :], dst_ref=land_ref.at[me],
                send_sem=send.at[d - 1], recv_sem=recv.at[d - 1], device_id=place, device_id_type=MESH)
            cp.wait_send()
            cp.wait_recv()

    src, land = pending["src"], pending["land"]
    return pl.pallas_call(
        body, name=name, out_shape=(pltpu.HBM(src.shape, src.dtype), pltpu.HBM(land.shape, land.dtype)),
        in_specs=(HBM_SPEC, HBM_SPEC, SEM_SPEC, SEM_SPEC, SEM_SPEC, pl.BlockSpec(memory_space=pl.ANY)),
        out_specs=(HBM_SPEC, HBM_SPEC), input_output_aliases={0: 0, 1: 1},
        compiler_params=pltpu.CompilerParams(has_side_effects=EFFECT),
    )(src, land, pending["send"], pending["recv"], pending["own"], after)[1]


BIG = ("ffn1_w_in", "ffn1_w_out", "mix_w_in", "w_branch_a", "w_branch_b", "mix_w_out", "ffn2_w_in", "ffn2_w_out")
TRANSPOSED = ("ffn1_w_in", "mix_w_in", "w_branch_b", "ffn2_w_in")
SMALL = ("ln1_g", "ln1_b", "ln2_g", "ln2_b", "ln3_g", "ln3_b", "hgrn_norm_g", "hgrn_lb_fwd", "hgrn_lb_bwd")
SMALL_ROWS = 16


def _local_step(x, target, weight, emit, emit_small, sp):
    b_, s_, d_ = x.shape
    t_ = b_ * s_
    x2, tgt = x.reshape(t_, d_), target.reshape(t_, d_)
    xb = x2.astype(BF16)
    w1i = weight("ffn1_w_in", xb)
    g1, u1, a1 = _ffn_in_fwd(xb, w1i, "ffn1_in")
    w1o = weight("ffn1_w_out", a1)
    r1, h1, h1b = _mm_res_ln_fwd(a1, w1o, x2, sp["ln1_g"], sp["ln1_b"], 0.5, "ffn1_out_ln1")
    wmx = weight("mix_w_in", h1b)
    proj = _mm_nt(h1b, wmx, F32, "mix_in")
    ya_in, o_sum = _hgrn_fwd(proj, sp["hgrn_lb_fwd"], sp["hgrn_lb_bwd"], sp["hgrn_norm_g"], b_, s_, d_, "hgrn_fwd")
    tabs = _rope_tables(s_)
    outs, lses = [], []
    group_col = [(5 * d_ + gi * QKV_GROUP) // HEAD for gi in range(len(ATTN_GROUPS))]
    for gi, (_, dil) in enumerate(ATTN_GROUPS):
        o_g, lse_g = _attn_fwd(proj, tabs, b_, s_, group_col[gi], dil, f"attn_fwd_{gi}")
        outs.append(o_g)
        lses.append(lse_g)
    ob = _combine_fwd(outs, lses, "attn_combine")
    wa, wb = weight("w_branch_a", ya_in), weight("w_branch_b", ob)
    ya, yb, z = _gate_out_fwd(ya_in, ob, wa, wb, proj, d_, "branch_gate")
    wo = weight("mix_w_out", z)
    r2, h2, h2b = _mm_res_ln_fwd(z, wo, h1, sp["ln2_g"], sp["ln2_b"], 1.0, "mix_out_ln2")
    w2i = weight("ffn2_w_in", h2b)
    g2, u2, a2 = _ffn_in_fwd(h2b, w2i, "ffn2_in")
    w2o = weight("ffn2_w_out", a2)
    dr3, dr3b, dg3, db3, loss = _mm_res_loss_bwd(a2, w2o, h2, tgt, sp["ln3_g"], sp["ln3_b"], 0.5, "ffn2_out_loss")
    dep = emit("ffn2_w_out", _mm_tn(a2, dr3b, 0.5, "d_ffn2_w_out"))
    dgate2, dup2 = _ffn_mid_bwd(dr3b, w2o, g2, u2, 0.5, "ffn2_mid_bwd", dep)
    du2 = (dgate2, dup2)
    dep = emit("ffn2_w_in", _mm_tn(du2, h2b, 1.0, "d_ffn2_w_in"))
    dr2, dr2b, dg2, db2 = _mm_nn_res_lnbwd(du2, w2i, dr3, r2, sp["ln2_g"], "ffn2_in_bwd_ln2", dep)
    dep = emit("mix_w_out", _mm_tn(z, dr2b, 1.0, "d_mix_w_out"))
    dya, dyb, dga, dgb = _dz_gate_bwd(dr2b, wo, proj, ya, yb, d_, "branch_gate_bwd", dep)
    dep = emit("w_branch_a", _mm_tn(ya_in, dya, 1.0, "d_w_branch_a"))
    dya_in = _mm_nt(dya, wa, F32, "branch_a_bwd", dep)
    dep = emit("w_branch_b", _mm_tn(dyb, ob, 1.0, "d_w_branch_b"))
    dob = _mm_nn(dyb, wb, F32, "branch_b_bwd", dep)
    dhq, dhff, dhfb, dhi, dhog, dng, dlbf, dlbb = _hgrn_bwd(
        proj, sp["hgrn_lb_fwd"], sp["hgrn_lb_bwd"], sp["hgrn_norm_g"], o_sum, dya_in, b_, s_, d_, "hgrn_bwd")
    dogs, cgs = _combine_bwd(dob, outs, lses, "attn_combine_bwd")
    dqkv = []
    for gi, (_, dil) in enumerate(ATTN_GROUPS):
        dqkv += _attn_bwd(proj, tabs, dogs[gi], cgs[gi], lses[gi], b_, s_, group_col[gi], dil, f"attn_bwd_{gi}")
    dproj = jnp.concatenate([dhq, dhff, dhfb, dhi, dhog] + [t.astype(BF16) for t in dqkv] + [dga, dgb], axis=1)
    dep = emit("mix_w_in", _mm_tn(dproj, h1b, 1.0, "d_mix_w_in"))
    dr1, dr1b, dg1, db1 = _mm_nn_res_lnbwd(dproj, wmx, dr2, r1, sp["ln1_g"], "mix_in_bwd_ln1", dep)
    dep_small = emit_small({"ln1_g": dg1, "ln1_b": db1, "ln2_g": dg2, "ln2_b": db2, "ln3_g": dg3, "ln3_b": db3,
                            "hgrn_norm_g": dng, "hgrn_lb_fwd": dlbf, "hgrn_lb_bwd": dlbb})
    dep = emit("ffn1_w_out", _mm_tn(a1, dr1b, 0.5, "d_ffn1_w_out")) + dep_small
    dgate1, dup1 = _ffn_mid_bwd(dr1b, w1o, g1, u1, 0.5, "ffn1_mid_bwd", dep)
    du1 = (dgate1, dup1)
    dep = emit("ffn1_w_in", _mm_tn(du1, xb, 1.0, "d_ffn1_w_in"))
    grad_x = _mm_nn_res(du1, w1i, dr1, "ffn1_in_bwd", dep)
    return loss, grad_x.reshape(b_, s_, d_)


def _pack_small(vals):
    rows = jnp.concatenate([vals[n] for n in SMALL], axis=0)
    return jnp.pad(rows, ((0, SMALL_ROWS - rows.shape[0]), (0, 0)))


def _unpack_small(packed):
    out, r = {}, 0
    for n in SMALL:
        k = 2 if n.startswith("hgrn_lb") else 1
        out[n] = packed[r:r + k]
        r += k
    return out


def kernel(x, ffn1_w_in, ffn1_w_out, ln1_g, ln1_b, mix_w_in, hgrn_lb_fwd, hgrn_lb_bwd, hgrn_norm_g, w_branch_a, w_branch_b, mix_w_out, ln2_g, ln2_b, ffn2_w_in, ffn2_w_out, ln3_g, ln3_b, loss_target, m_ffn1_w_in, m_ffn1_w_out, m_ln1_g, m_ln1_b, m_mix_w_in, m_hgrn_lb_fwd, m_hgrn_lb_bwd, m_hgrn_norm_g, m_w_branch_a, m_w_branch_b, m_mix_w_out, m_ln2_g, m_ln2_b, m_ffn2_w_in, m_ffn2_w_out, m_ln3_g, m_ln3_b, v_ffn1_w_in, v_ffn1_w_out, v_ln1_g, v_ln1_b, v_mix_w_in, v_hgrn_lb_fwd, v_hgrn_lb_bwd, v_hgrn_norm_g, v_w_branch_a, v_w_branch_b, v_mix_w_out, v_ln2_g, v_ln2_b, v_ffn2_w_in, v_ffn2_w_out, v_ln3_g, v_ln3_b):
    args = dict(locals())
    big_w = {n: args[n][0] for n in BIG}
    sp = {n: args[n] for n in SMALL}
    def rows_bf16(n, zero=0.0):
        w = big_w[n] + zero
        return (w.T if n in TRANSPOSED else w).astype(BF16)

    first, rest = BIG[:2], BIG[2:]
    pending, token = _gather_start([rows_bf16(n) for n in first], "gather_start_ffn1")
    gathering = dict(zip(first, pending))
    pending, all_started = _gather_start([rows_bf16(n, token[0, 0]) for n in rest], "gather_start_rest")
    gathering.update(zip(rest, pending))
    scattering = {}

    def weight(n, after):
        if n == first[0]:
            after = all_started
        return _gather_wait(_gather_forward(gathering[n], after, f"gather_forward_{n}"), f"gather_wait_{n}")

    def emit(n, grad):
        scattering[n], token = _scatter_start(grad, f"scatter_start_{n}")
        return token

    def emit_small(grads):
        pending, token = _gather_start([_pack_small(grads)], "gather_start_small")
        scattering["small"] = pending[0]
        return token

    loss_part, grad_x = _local_step(x, loss_target, weight, emit, emit_small, sp)
    loss = lax.psum(loss_part[0, 0], ("x", "y", "c"))
    out_g, out_d, out_m, out_v = {}, {}, {}, {}
    done = grad_x
    for n in ("ffn2_w_out", "ffn2_w_in", "mix_w_out", "w_branch_a", "w_branch_b", "mix_w_in", "small",
              "ffn1_w_out", "ffn1_w_in"):
        if n == "small":
            parts = _gather_wait(_gather_forward(scattering[n], done, "gather_forward_small"), "gather_wait_small")
            res = _small_sum_adam(parts.reshape(N_DEV, SMALL_ROWS, parts.shape[1]), _pack_small(sp),
                                  _pack_small({n: args["m_" + n] for n in SMALL}),
                                  _pack_small({n: args["v_" + n] for n in SMALL}), "small_adam")
            sg, sd, sm, sv = (_unpack_small(r) for r in res)
            out_g.update(sg), out_d.update(sd), out_m.update(sm), out_v.update(sv)
            done = res[3]
            continue
        g = _sum_partials(_scatter_wait(scattering[n], done, f"scatter_wait_{n}"), f"sum_{n}")
        if n in TRANSPOSED:
            g = g.T
        d_w, m_w, v_w = _adam(big_w[n], g, args["m_" + n][0], args["v_" + n][0], f"adam_{n}")
        out_g[n], out_d[n], out_m[n], out_v[n] = g[None], d_w[None], m_w[None], v_w[None]
        done = v_w
    order = ("ffn1_w_in", "ffn1_w_out", "ln1_g", "ln1_b", "mix_w_in", "hgrn_lb_fwd", "hgrn_lb_bwd", "hgrn_norm_g",
             "w_branch_a", "w_branch_b", "mix_w_out", "ln2_g", "ln2_b", "ffn2_w_in", "ffn2_w_out", "ln3_g", "ln3_b")
    return (loss, grad_x, *[out_g[n] for n in order], *[out_d[n] for n in order],
            *[out_m[n] for n in order], *[out_v[n] for n in order])
```

```python
import functools

import jax
import jax.numpy as jnp
from jax import lax
from jax.experimental import pallas as pl
from jax.experimental.pallas import tpu as pltpu

F32 = jnp.float32
BF16 = jnp.bfloat16

N_DEV = 8
HEAD = 128
CHUNK = 32
ATTN_GROUPS = ((128, 1), (512, 4), (2048, 16))
ATTN_HEADS = 4
ATTN_HALF = 64
QKV_GROUP = 3 * ATTN_HEADS * HEAD
QKV_WIDTH = len(ATTN_GROUPS) * QKV_GROUP
ATTN_OUT = ATTN_HEADS * HEAD
ROPE_THETA = 500000.0
ROPE_DIM = HEAD // 4
ALPHA = 2.0 ** 0.25
LN_EPS = 1e-5
NEG_INF = -1e30
ADAM_LR, ADAM_B1, ADAM_B2, ADAM_EPS, ADAM_WD, ADAM_STEP = 0.001, 0.9, 0.999, 1e-08, 0.01, 10
VMEM_LIMIT = 56 * 1024 * 1024

NT = (((1,), (1,)), ((), ()))
NN = (((1,), (0,)), ((), ()))
TN = (((0,), (0,)), ((), ()))
MESH = pl.DeviceIdType.MESH


def _dot(a, b, dims):
    return lax.dot_general(a, b, dims, preferred_element_type=F32)


def _tile(n, pref, mult=128):
    if n <= pref:
        return n
    t = (pref // mult) * mult
    while t >= mult:
        if n % t == 0:
            return t
        t -= mult
    return n


def _tile_multi(ns, pref, mult=128):
    t = (pref // mult) * mult
    while t >= mult:
        if all(n % t == 0 for n in ns):
            return t
        t -= mult
    raise ValueError(f"no common tile for {ns}")


def _params(**kw):
    return pltpu.CompilerParams(vmem_limit_bytes=VMEM_LIMIT, **kw)


def _after(body, n_in, dep):
    if dep is None:
        return body, [], []

    def wrapped(*refs):
        body(*refs[:n_in], *refs[n_in + 1:])

    return wrapped, [pl.BlockSpec(dep.shape, lambda *_: (0,) * dep.ndim)], [dep]


def _pieces(a):
    pieces = tuple(a) if isinstance(a, (tuple, list)) else (a,)
    assert all(p.shape == pieces[0].shape for p in pieces)
    return pieces, pieces[0].shape[0], pieces[0].shape[1], len(pieces)


def _for_piece(step, p, per, npc, fn):
    if npc == 1:
        fn()
    else:
        pl.when((step >= p * per) & (step < (p + 1) * per))(fn)


def _sigmoid(x):
    return jax.nn.sigmoid(x)


def _dsilu(x, s):
    return s * (1.0 + x * (1.0 - s))


def _ln_stats(r):
    mu = jnp.mean(r, axis=-1, keepdims=True)
    xc = r - mu
    var = jnp.mean(xc * xc, axis=-1, keepdims=True)
    rstd = lax.rsqrt(var + LN_EPS)
    return xc * rstd, rstd


def _ln_bwd(dy, xhat, rstd, g):
    dyg = dy * g
    m1 = jnp.mean(dyg, axis=-1, keepdims=True)
    m2 = jnp.mean(dyg * xhat, axis=-1, keepdims=True)
    return rstd * (dyg - m1 - xhat * m2)


ROW_TILE = 1024
SUB_ROWS = 256


def _once(shape, index_map):
    return pl.BlockSpec(shape, index_map, pipeline_mode=pl.Buffered(1))


def _for_row_blocks(tm, fn):
    sub = SUB_ROWS if tm % SUB_ROWS == 0 else tm

    def step(s, carry):
        fn(pl.ds(pl.multiple_of(s * sub, sub), sub))
        return carry

    lax.fori_loop(0, tm // sub, step, 0)


def _row_runs(tm):
    sub = SUB_ROWS if tm % SUB_ROWS == 0 else tm
    return [slice(s, s + sub) for s in range(0, tm, sub)]


def _ffn_in_fwd(xb, w_t, name):
    t_, d_ = xb.shape
    f_ = w_t.shape[0] // 2
    tm, tn = _tile(t_, ROW_TILE, 8), _tile(f_, 512)
    nj = f_ // tn

    def body(x_ref, wg_ref, wu_ref, g_ref, u_ref, a_ref):
        wg, wu = wg_ref[...], wu_ref[...]
        for rows in _row_runs(tm):
            x = x_ref[rows, :]
            g = _dot(x, wg, NT)
            u = _dot(x, wu, NT)
            g_ref[rows, :] = g.astype(BF16)
            u_ref[rows, :] = u.astype(BF16)
            a_ref[rows, :] = (g * _sigmoid(g) * u).astype(BF16)

    return pl.pallas_call(
        body, name=name, grid=(t_ // tm, nj),
        in_specs=[pl.BlockSpec((tm, d_), lambda i, j: (i, 0)),
                  pl.BlockSpec((tn, d_), lambda i, j: (j, 0)),
                  pl.BlockSpec((tn, d_), lambda i, j: (j + nj, 0))],
        out_specs=[pl.BlockSpec((tm, tn), lambda i, j: (i, j))] * 3,
        out_shape=[jax.ShapeDtypeStruct((t_, f_), BF16)] * 3,
        compiler_params=_params(),
    )(xb, w_t, w_t)


WHOLE_WEIGHT_BYTES = 24 * 1024 * 1024


def _k_plan(t_, k_, d_):
    if k_ * d_ * 2 > WHOLE_WEIGHT_BYTES:
        return False, _tile(t_, ROW_TILE, 8), _tile(k_, 512), _once
    if k_ <= 2048:
        return True, _tile(t_, ROW_TILE, 8), k_, _once
    return True, _tile(t_, SUB_ROWS, 8), k_, pl.BlockSpec


def _mm_res_ln_fwd(a, w, res, g, b, scale, name):
    t_, k_ = a.shape
    d_ = w.shape[1]
    whole, tm, tk, row_spec = _k_plan(t_, k_, d_)
    nk = k_ // tk

    def body(a_ref, w_ref, res_ref, g_ref, b_ref, r_ref, h_ref, hb_ref, *scratch):
        k = pl.program_id(1)
        if not whole:
            acc, = scratch

            @pl.when(k == 0)
            def _():
                acc[...] = jnp.zeros_like(acc)

            acc[...] += _dot(a_ref[...], w_ref[...], NN)

        @pl.when(k == nk - 1)
        def _():
            def rows_out(rows):
                prod = _dot(a_ref[rows, :], w_ref[...], NN) if whole else acc[rows, :]
                r = ALPHA * res_ref[rows, :] + scale * prod
                xhat, _ = _ln_stats(r)
                h = xhat * g_ref[...] + b_ref[...]
                r_ref[rows, :] = r
                h_ref[rows, :] = h
                hb_ref[rows, :] = h.astype(BF16)

            _for_row_blocks(tm, rows_out)

    row = row_spec((tm, d_), lambda i, k: (i, 0))
    vec = pl.BlockSpec((1, d_), lambda i, k: (0, 0))
    w_spec = _once((tk, d_), lambda i, k: (0, 0)) if whole else pl.BlockSpec((tk, d_), lambda i, k: (k, 0))
    return pl.pallas_call(
        body, name=name, grid=(t_ // tm, nk),
        in_specs=[pl.BlockSpec((tm, tk), lambda i, k: (i, k)), w_spec, row, vec, vec],
        out_specs=[row, row, row],
        out_shape=[jax.ShapeDtypeStruct((t_, d_), F32), jax.ShapeDtypeStruct((t_, d_), F32),
                   jax.ShapeDtypeStruct((t_, d_), BF16)],
        scratch_shapes=[] if whole else [pltpu.VMEM((tm, d_), F32)],
        compiler_params=_params(),
    )(a, w, res, g, b)


def _mm_res_loss_bwd(a, w, res, target, g, b, scale, name):
    t_, k_ = a.shape
    d_ = w.shape[1]
    whole, tm, tk, row_spec = _k_plan(t_, k_, d_)
    nk = k_ // tk

    def body(a_ref, w_ref, res_ref, t_ref, g_ref, b_ref, dr_ref, drb_ref, dg_ref, db_ref, loss_ref, *scratch):
        i, k = pl.program_id(0), pl.program_id(1)

        @pl.when((i == 0) & (k == 0))
        def _():
            dg_ref[...] = jnp.zeros_like(dg_ref)
            db_ref[...] = jnp.zeros_like(db_ref)
            loss_ref[...] = jnp.zeros_like(loss_ref)

        if not whole:
            acc, = scratch

            @pl.when(k == 0)
            def _():
                acc[...] = jnp.zeros_like(acc)

            acc[...] += _dot(a_ref[...], w_ref[...], NN)

        @pl.when(k == nk - 1)
        def _():
            def rows_out(rows):
                prod = _dot(a_ref[rows, :], w_ref[...], NN) if whole else acc[rows, :]
                r = ALPHA * res_ref[rows, :] + scale * prod
                xhat, rstd = _ln_stats(r)
                gain = g_ref[...]
                err = xhat * gain + b_ref[...] - t_ref[rows, :]
                loss_ref[...] += (0.5 / d_) * jnp.sum(err * err)
                dy = err * (1.0 / d_)
                dr = _ln_bwd(dy, xhat, rstd, gain)
                dr_ref[rows, :] = dr
                drb_ref[rows, :] = dr.astype(BF16)
                dg_ref[...] += jnp.sum(dy * xhat, axis=0, keepdims=True)
                db_ref[...] += jnp.sum(dy, axis=0, keepdims=True)

            _for_row_blocks(tm, rows_out)

    row = row_spec((tm, d_), lambda i, k: (i, 0))
    vec = pl.BlockSpec((1, d_), lambda i, k: (0, 0))
    w_spec = _once((tk, d_), lambda i, k: (0, 0)) if whole else pl.BlockSpec((tk, d_), lambda i, k: (k, 0))
    return pl.pallas_call(
        body, name=name, grid=(t_ // tm, nk),
        in_specs=[pl.BlockSpec((tm, tk), lambda i, k: (i, k)), w_spec, row, row, vec, vec],
        out_specs=[row, row, vec, vec, pl.BlockSpec((1, HEAD), lambda i, k: (0, 0))],
        out_shape=[jax.ShapeDtypeStruct((t_, d_), F32), jax.ShapeDtypeStruct((t_, d_), BF16),
                   jax.ShapeDtypeStruct((1, d_), F32), jax.ShapeDtypeStruct((1, d_), F32),
                   jax.ShapeDtypeStruct((1, HEAD), F32)],
        scratch_shapes=[] if whole else [pltpu.VMEM((tm, d_), F32)],
        compiler_params=_params(),
    )(a, w, res, target, g, b)


def _mm_nt(a, w_t, out_dtype, name, dep=None):
    t_, k_ = a.shape
    n_ = w_t.shape[0]
    tm, tn = _tile(t_, ROW_TILE, 8), _tile(n_, 512)

    def body(a_ref, w_ref, o_ref):
        o_ref[...] = _dot(a_ref[...], w_ref[...], NT).astype(out_dtype)

    body, dep_specs, deps = _after(body, 2, dep)
    return pl.pallas_call(
        body, name=name, grid=(t_ // tm, n_ // tn),
        in_specs=[pl.BlockSpec((tm, k_), lambda i, j: (i, 0)),
                  pl.BlockSpec((tn, k_), lambda i, j: (j, 0)), *dep_specs],
        out_specs=pl.BlockSpec((tm, tn), lambda i, j: (i, j)),
        out_shape=jax.ShapeDtypeStruct((t_, n_), out_dtype),
        compiler_params=_params(),
    )(a, w_t, *deps)


def _mm_nn(a, w, out_dtype, name, dep=None):
    t_, k_ = a.shape
    n_ = w.shape[1]
    tm, tn = _tile(t_, ROW_TILE, 8), _tile(n_, 512)

    def body(a_ref, w_ref, o_ref):
        o_ref[...] = _dot(a_ref[...], w_ref[...], NN).astype(out_dtype)

    body, dep_specs, deps = _after(body, 2, dep)
    return pl.pallas_call(
        body, name=name, grid=(t_ // tm, n_ // tn),
        in_specs=[pl.BlockSpec((tm, k_), lambda i, j: (i, 0)),
                  pl.BlockSpec((k_, tn), lambda i, j: (0, j)), *dep_specs],
        out_specs=pl.BlockSpec((tm, tn), lambda i, j: (i, j)),
        out_shape=jax.ShapeDtypeStruct((t_, n_), out_dtype),
        compiler_params=_params(),
    )(a, w, *deps)


def _mm_tn(a, b, scale, name):
    pieces, t_, mp, npc = _pieces(a)
    n_ = b.shape[1]
    tm = _tile(mp, 512)
    per = mp // tm

    def body(*refs):
        b_ref, o_ref = refs[npc], refs[npc + 1]
        for p in range(npc):
            def piece_out(p=p):
                o_ref[...] = (scale * _dot(refs[p][...], b_ref[...], TN)).astype(BF16)

            _for_piece(pl.program_id(0), p, per, npc, piece_out)

    return pl.pallas_call(
        body, name=name, grid=(npc * per,),
        in_specs=[pl.BlockSpec((t_, tm), lambda i, p=p: (0, jnp.clip(i - p * per, 0, per - 1))) for p in range(npc)]
        + [_once((t_, n_), lambda i: (0, 0))],
        out_specs=pl.BlockSpec((tm, n_), lambda i: (i, 0)),
        out_shape=jax.ShapeDtypeStruct((npc * mp, n_), BF16),
        compiler_params=_params(),
    )(*pieces, b)


def _gate_out_fwd(ya_in, ob, wa, wb_t, proj, d_, name):
    t_ = ya_in.shape[0]
    goff = 5 * d_ + QKV_WIDTH
    tm, tn = _tile(t_, ROW_TILE, 8), _tile_multi([d_, goff], 512)
    ja, jb = goff // tn, (goff + d_) // tn

    def body(ya_ref, ob_ref, wa_ref, wb_ref, ga_ref, gb_ref, yao_ref, ybo_ref, z_ref):
        wa, wb = wa_ref[...], wb_ref[...]
        for rows in _row_runs(tm):
            y_a = _dot(ya_ref[rows, :], wa, NN)
            y_b = _dot(ob_ref[rows, :], wb, NT)
            yao_ref[rows, :] = y_a.astype(BF16)
            ybo_ref[rows, :] = y_b.astype(BF16)
            z_ref[rows, :] = (_sigmoid(ga_ref[rows, :]) * y_a + _sigmoid(gb_ref[rows, :]) * y_b).astype(BF16)

    tile = pl.BlockSpec((tm, tn), lambda i, j: (i, j))
    return pl.pallas_call(
        body, name=name, grid=(t_ // tm, d_ // tn),
        in_specs=[pl.BlockSpec((tm, d_), lambda i, j: (i, 0)),
                  pl.BlockSpec((tm, ATTN_OUT), lambda i, j: (i, 0)),
                  pl.BlockSpec((d_, tn), lambda i, j: (0, j)),
                  pl.BlockSpec((tn, ATTN_OUT), lambda i, j: (j, 0)),
                  pl.BlockSpec((tm, tn), lambda i, j: (i, ja + j)),
                  pl.BlockSpec((tm, tn), lambda i, j: (i, jb + j))],
        out_specs=[tile, tile, tile],
        out_shape=[jax.ShapeDtypeStruct((t_, d_), BF16)] * 3,
        compiler_params=_params(),
    )(ya_in, ob, wa, wb_t, proj, proj)


def _ffn_mid_bwd(drb, w_out, gate, up, scale, name, dep=None):
    t_, d_ = drb.shape
    f_ = w_out.shape[0]
    tm, tn = _tile(t_, ROW_TILE, 8), _tile(f_, 512)

    def body(dr_ref, w_ref, g_ref, u_ref, dg_ref, du_ref):
        w = w_ref[...]
        for rows in _row_runs(tm):
            da = scale * _dot(dr_ref[rows, :], w, NT)
            g = g_ref[rows, :].astype(F32)
            s = _sigmoid(g)
            dg_ref[rows, :] = (da * u_ref[rows, :].astype(F32) * _dsilu(g, s)).astype(BF16)
            du_ref[rows, :] = (da * g * s).astype(BF16)

    body, dep_specs, deps = _after(body, 4, dep)
    tile = pl.BlockSpec((tm, tn), lambda i, j: (i, j))
    return pl.pallas_call(
        body, name=name, grid=(t_ // tm, f_ // tn),
        in_specs=[pl.BlockSpec((tm, d_), lambda i, j: (i, 0)),
                  pl.BlockSpec((tn, d_), lambda i, j: (j, 0)), tile, tile, *dep_specs],
        out_specs=[tile, tile],
        out_shape=[jax.ShapeDtypeStruct((t_, f_), BF16)] * 2,
        compiler_params=_params(),
    )(drb, w_out, gate, up, *deps)


def _mm_nn_res_lnbwd(a, w, dres, r, g, name, dep=None):
    pieces, t_, kp, npc = _pieces(a)
    d_ = w.shape[1]
    tm, tk = _tile(t_, ROW_TILE, 8), _tile(kp, 512)
    per = kp // tk
    nk = npc * per

    def body(*refs):
        w_ref, dres_ref, r_ref, g_ref, dr_ref, drb_ref, dg_ref, db_ref, acc = refs[npc:]
        i, k = pl.program_id(0), pl.program_id(1)

        @pl.when(k == 0)
        def _():
            acc[...] = jnp.zeros_like(acc)

        @pl.when((i == 0) & (k == 0))
        def _():
            dg_ref[...] = jnp.zeros_like(dg_ref)
            db_ref[...] = jnp.zeros_like(db_ref)

        for p in range(npc):
            def piece_in(p=p):
                acc[...] += _dot(refs[p][...], w_ref[...], NN)

            _for_piece(k, p, per, npc, piece_in)

        @pl.when(k == nk - 1)
        def _():
            def rows_out(rows):
                dy = acc[rows, :] + ALPHA * dres_ref[rows, :]
                xhat, rstd = _ln_stats(r_ref[rows, :])
                dr = _ln_bwd(dy, xhat, rstd, g_ref[...])
                dr_ref[rows, :] = dr
                drb_ref[rows, :] = dr.astype(BF16)
                dg_ref[...] += jnp.sum(dy * xhat, axis=0, keepdims=True)
                db_ref[...] += jnp.sum(dy, axis=0, keepdims=True)

            _for_row_blocks(tm, rows_out)

    body, dep_specs, deps = _after(body, npc + 4, dep)
    row = _once((tm, d_), lambda i, k: (i, 0))
    vec = pl.BlockSpec((1, d_), lambda i, k: (0, 0))
    return pl.pallas_call(
        body, name=name, grid=(t_ // tm, nk),
        in_specs=[pl.BlockSpec((tm, tk), lambda i, k, p=p: (i, jnp.clip(k - p * per, 0, per - 1))) for p in range(npc)]
        + [pl.BlockSpec((tk, d_), lambda i, k: (k, 0)), row, row, vec, *dep_specs],
        out_specs=[row, row, vec, vec],
        out_shape=[jax.ShapeDtypeStruct((t_, d_), F32), jax.ShapeDtypeStruct((t_, d_), BF16),
                   jax.ShapeDtypeStruct((1, d_), F32), jax.ShapeDtypeStruct((1, d_), F32)],
        scratch_shapes=[pltpu.VMEM((tm, d_), F32)],
        compiler_params=_params(),
    )(*pieces, w, dres, r, g, *deps)


def _mm_nn_res(a, w, dres, name, dep=None):
    pieces, t_, kp, npc = _pieces(a)
    d_ = w.shape[1]
    tm, tk = _tile(t_, ROW_TILE, 8), _tile(kp, 512)
    per = kp // tk
    nk = npc * per

    def body(*refs):
        w_ref, dres_ref, o_ref, acc = refs[npc:]
        k = pl.program_id(1)

        @pl.when(k == 0)
        def _():
            acc[...] = jnp.zeros_like(acc)

        for p in range(npc):
            def piece_in(p=p):
                acc[...] += _dot(refs[p][...], w_ref[...], NN)

            _for_piece(k, p, per, npc, piece_in)

        @pl.when(k == nk - 1)
        def _():
            def rows_out(rows):
                o_ref[rows, :] = acc[rows, :] + ALPHA * dres_ref[rows, :]

            _for_row_blocks(tm, rows_out)

    body, dep_specs, deps = _after(body, npc + 2, dep)
    row = _once((tm, d_), lambda i, k: (i, 0))
    return pl.pallas_call(
        body, name=name, grid=(t_ // tm, nk),
        in_specs=[pl.BlockSpec((tm, tk), lambda i, k, p=p: (i, jnp.clip(k - p * per, 0, per - 1))) for p in range(npc)]
        + [pl.BlockSpec((tk, d_), lambda i, k: (k, 0)), row, *dep_specs],
        out_specs=row,
        out_shape=jax.ShapeDtypeStruct((t_, d_), F32),
        scratch_shapes=[pltpu.VMEM((tm, d_), F32)],
        compiler_params=_params(),
    )(*pieces, w, dres, *deps)


def _dz_gate_bwd(drb, w_out, proj, ya, yb, d_, name, dep=None):
    t_ = drb.shape[0]
    goff = 5 * d_ + QKV_WIDTH
    tm, tn = _tile(t_, ROW_TILE, 8), _tile_multi([d_, goff], 512)
    ja, jb = goff // tn, (goff + d_) // tn

    def body(dr_ref, w_ref, ga_ref, gb_ref, ya_ref, yb_ref, dya_ref, dyb_ref, dga_ref, dgb_ref):
        w = w_ref[...]
        for rows in _row_runs(tm):
            dz = _dot(dr_ref[rows, :], w, NT)
            sa, sb = _sigmoid(ga_ref[rows, :]), _sigmoid(gb_ref[rows, :])
            dya_ref[rows, :] = (dz * sa).astype(BF16)
            dyb_ref[rows, :] = (dz * sb).astype(BF16)
            dga_ref[rows, :] = (dz * ya_ref[rows, :].astype(F32) * sa * (1.0 - sa)).astype(BF16)
            dgb_ref[rows, :] = (dz * yb_ref[rows, :].astype(F32) * sb * (1.0 - sb)).astype(BF16)

    body, dep_specs, deps = _after(body, 6, dep)
    tile = pl.BlockSpec((tm, tn), lambda i, j: (i, j))
    return pl.pallas_call(
        body, name=name, grid=(t_ // tm, d_ // tn),
        in_specs=[pl.BlockSpec((tm, d_), lambda i, j: (i, 0)),
                  pl.BlockSpec((tn, d_), lambda i, j: (j, 0)),
                  pl.BlockSpec((tm, tn), lambda i, j: (i, ja + j)),
                  pl.BlockSpec((tm, tn), lambda i, j: (i, jb + j)), tile, tile, *dep_specs],
        out_specs=[tile] * 4,
        out_shape=[jax.ShapeDtypeStruct((t_, d_), BF16)] * 4,
        compiler_params=_params(),
    )(drb, w_out, proj, proj, ya, yb, *deps)


def _ln_loss_bwd(r, target, g, b, name):
    t_, d_ = r.shape
    tm = _tile(t_, 256, 8)

    def body(r_ref, t_ref, g_ref, b_ref, dr_ref, drb_ref, dg_ref, db_ref, loss_ref):
        i = pl.program_id(0)

        @pl.when(i == 0)
        def _():
            dg_ref[...] = jnp.zeros_like(dg_ref)
            db_ref[...] = jnp.zeros_like(db_ref)
            loss_ref[...] = jnp.zeros_like(loss_ref)

        xhat, rstd = _ln_stats(r_ref[...])
        gain = g_ref[...]
        err = xhat * gain + b_ref[...] - t_ref[...]
        loss_ref[...] += (0.5 / d_) * jnp.sum(err * err)
        dy = err * (1.0 / d_)
        dr = _ln_bwd(dy, xhat, rstd, gain)
        dr_ref[...] = dr
        drb_ref[...] = dr.astype(BF16)
        dg_ref[...] += jnp.sum(dy * xhat, axis=0, keepdims=True)
        db_ref[...] += jnp.sum(dy, axis=0, keepdims=True)

    row = pl.BlockSpec((tm, d_), lambda i: (i, 0))
    vec = pl.BlockSpec((1, d_), lambda i: (0, 0))
    return pl.pallas_call(
        body, name=name, grid=(t_ // tm,),
        in_specs=[row, row, vec, vec],
        out_specs=[row, row, vec, vec, pl.BlockSpec((1, HEAD), lambda i: (0, 0))],
        out_shape=[jax.ShapeDtypeStruct((t_, d_), F32), jax.ShapeDtypeStruct((t_, d_), BF16),
                   jax.ShapeDtypeStruct((1, d_), F32), jax.ShapeDtypeStruct((1, d_), F32),
                   jax.ShapeDtypeStruct((1, HEAD), F32)],
        compiler_params=_params(),
    )(r, target, g, b)


def _chunk_scan(x, row, reverse, size):
    s = 1
    while s < CHUNK:
        if reverse:
            x = x + jnp.where(row < CHUNK - s, pltpu.roll(x, size - s, 0), 0.0)
        else:
            x = x + jnp.where(row >= s, pltpu.roll(x, s, 0), 0.0)
        s *= 2
    return x


def _lower_bound(tab):
    return _sigmoid(tab[0:1, :] - tab[1:2, :])


def _tri_mask(reverse):
    r = lax.broadcasted_iota(jnp.int32, (CHUNK, CHUNK), 0)
    c = lax.broadcasted_iota(jnp.int32, (CHUNK, CHUNK), 1)
    return (c >= r) if reverse else (r >= c)


def _hgrn_fwd(proj, lbf, lbb, ng, b_, s_, d_, name):
    h_ = d_ // HEAD
    nc = s_ // CHUNK

    def body(hq_ref, hff_ref, hfb_ref, hi_ref, hog_ref, lbf_ref, lbb_ref, ng_ref, ya_ref, o_ref,
             q_s, k_s, cum_s, o_s):
        row = lax.broadcasted_iota(jnp.int32, (s_, HEAD), 0) % CHUNK
        hq = hq_ref[...]
        q_s[...] = hq * _sigmoid(hq)
        o_s[...] = jnp.zeros_like(o_s)
        for reverse, hf_ref, lb_ref in ((False, hff_ref, lbf_ref), (True, hfb_ref, lbb_ref)):
            lb = _lower_bound(lb_ref[...])
            f = lb + (1.0 - lb) * _sigmoid(hf_ref[...])
            k_s[...] = 1.0 - f
            cum_s[...] = _chunk_scan(jnp.log(f), row, reverse, s_)
            mask = _tri_mask(reverse)

            def step(n, st, reverse=reverse, mask=mask):
                idx = (nc - 1 - n) if reverse else n
                sl = pl.ds(pl.multiple_of(idx * CHUNK, CHUNK), CHUNK)
                cm = cum_s[sl, :]
                tot = cm[0:1, :] if reverse else cm[CHUNK - 1:CHUNK, :]
                qc, kc = q_s[sl, :], k_s[sl, :]
                vb = hi_ref[sl, :].astype(BF16)
                qd = (qc * jnp.exp(cm)).astype(BF16)
                kd = (kc * jnp.exp(-cm)).astype(BF16)
                ke = (kc * jnp.exp(tot - cm)).astype(BF16)
                a = jnp.where(mask, _dot(qd, kd, NT), 0.0)
                o_s[sl, :] += _dot(a.astype(BF16), vb, NN) + _dot(qd, st.astype(BF16), NT)
                return st * jnp.exp(tot) + _dot(vb, ke, TN)

            lax.fori_loop(0, nc, step, jnp.zeros((HEAD, HEAD), F32))
        o = o_s[...]
        o_ref[...] = o
        nrm = o * lax.rsqrt(jnp.mean(o * o, axis=-1, keepdims=True) + LN_EPS)
        hog = hog_ref[...]
        ya_ref[...] = (nrm * ng_ref[...] * hog * _sigmoid(hog)).astype(BF16)

    def col(part):
        return pl.BlockSpec((s_, HEAD), lambda h, b, part=part: (b, part * h_ + h))

    tab = pl.BlockSpec((2, HEAD), lambda h, b: (0, h))
    out = pl.BlockSpec((s_, HEAD), lambda h, b: (b, h))
    return pl.pallas_call(
        body, name=name, grid=(h_, b_),
        in_specs=[col(0), col(1), col(2), col(3), col(4), tab, tab,
                  pl.BlockSpec((1, HEAD), lambda h, b: (0, h))],
        out_specs=[out, out],
        out_shape=[jax.ShapeDtypeStruct((b_ * s_, d_), BF16), jax.ShapeDtypeStruct((b_ * s_, d_), F32)],
        scratch_shapes=[pltpu.VMEM((s_, HEAD), F32)] * 4,
        compiler_params=_params(),
    )(proj, proj, proj, proj, proj, lbf, lbb, ng)


def _hgrn_bwd(proj, lbf, lbb, ng, o_sum, dya, b_, s_, d_, name):
    h_ = d_ // HEAD
    nc = s_ // CHUNK

    def body(hq_ref, hff_ref, hfb_ref, hi_ref, hog_ref, lbf_ref, lbb_ref, ng_ref, o_ref, dya_ref,
             dhq_ref, dhff_ref, dhfb_ref, dhi_ref, dhog_ref, dng_ref, dlbf_ref, dlbb_ref,
             q_s, k_s, cum_s, do_s, dq_s, dv_s, db_s, dk_s, st_s):
        b = pl.program_id(1)

        @pl.when(b == 0)
        def _():
            dng_ref[...] = jnp.zeros_like(dng_ref)
            dlbf_ref[...] = jnp.zeros_like(dlbf_ref)
            dlbb_ref[...] = jnp.zeros_like(dlbb_ref)

        row = lax.broadcasted_iota(jnp.int32, (s_, HEAD), 0) % CHUNK
        crow = lax.broadcasted_iota(jnp.int32, (CHUNK, HEAD), 0)
        hq = hq_ref[...]
        sq = _sigmoid(hq)
        q_s[...] = hq * sq
        o = o_ref[...]
        rinv = lax.rsqrt(jnp.mean(o * o, axis=-1, keepdims=True) + LN_EPS)
        nrm = o * rinv
        hog = hog_ref[...]
        so = _sigmoid(hog)
        gain = ng_ref[...]
        dy = dya_ref[...]
        dhog_ref[...] = (dy * nrm * gain * _dsilu(hog, so)).astype(BF16)
        dng_ref[...] += jnp.sum(dy * nrm * hog * so, axis=0, keepdims=True)
        dn = dy * gain * hog * so
        do_s[...] = rinv * (dn - nrm * jnp.mean(dn * nrm, axis=-1, keepdims=True))
        dq_s[...] = jnp.zeros_like(dq_s)
        dv_s[...] = jnp.zeros_like(dv_s)

        for reverse, hf_ref, lb_ref, dhf_ref, dlb_ref in (
                (False, hff_ref, lbf_ref, dhff_ref, dlbf_ref), (True, hfb_ref, lbb_ref, dhfb_ref, dlbb_ref)):
            tab = lb_ref[...]
            lb = _lower_bound(tab)
            sf = _sigmoid(hf_ref[...])
            f = lb + (1.0 - lb) * sf
            k_s[...] = 1.0 - f
            cum_s[...] = _chunk_scan(jnp.log(f), row, reverse, s_)
            mask = _tri_mask(reverse)
            last = 0 if reverse else CHUNK - 1

            def chunk(idx, reverse=reverse):
                sl = pl.ds(pl.multiple_of(idx * CHUNK, CHUNK), CHUNK)
                cm = cum_s[sl, :]
                tot = cm[0:1, :] if reverse else cm[CHUNK - 1:CHUNK, :]
                return sl, cm, tot

            def fstep(n, st, reverse=reverse, chunk=chunk):
                idx = (nc - 1 - n) if reverse else n
                sl, cm, tot = chunk(idx)
                st_s[idx] = st
                ke = (k_s[sl, :] * jnp.exp(tot - cm)).astype(BF16)
                return st * jnp.exp(tot) + _dot(hi_ref[sl, :].astype(BF16), ke, TN)

            lax.fori_loop(0, nc, fstep, jnp.zeros((HEAD, HEAD), F32))

            def bstep(n, dst, reverse=reverse, chunk=chunk, mask=mask, last=last):
                idx = n if reverse else (nc - 1 - n)
                sl, cm, tot = chunk(idx)
                eb, enb, ee, dec = jnp.exp(cm), jnp.exp(-cm), jnp.exp(tot - cm), jnp.exp(tot)
                qc, kc = q_s[sl, :], k_s[sl, :]
                qd, kd, ke = qc * eb, kc * enb, kc * ee
                qdb, kdb, keb = qd.astype(BF16), kd.astype(BF16), ke.astype(BF16)
                vb = hi_ref[sl, :].astype(BF16)
                dob = do_s[sl, :].astype(BF16)
                st0 = st_s[idx]
                dstb = dst.astype(BF16)
                a = jnp.where(mask, _dot(qdb, kdb, NT), 0.0).astype(BF16)
                da = jnp.where(mask, _dot(dob, vb, NT), 0.0).astype(BF16)
                dqd = _dot(da, kdb, NN) + _dot(dob, st0.astype(BF16), NN)
                dkd = _dot(da, qdb, TN)
                dv = _dot(a, dob, TN) + _dot(keb, dstb, NT)
                dke = _dot(vb, dstb, NN)
                ddec = jnp.sum(dst * st0, axis=0, keepdims=True)
                dtot = jnp.sum(dke * ke, axis=0, keepdims=True) + ddec * dec
                db = dqd * qd - dkd * kd - dke * ke
                db_s[sl, :] = db + jnp.where(crow == last, dtot, 0.0)
                dk_s[sl, :] = dkd * enb + dke * ee
                dq_s[sl, :] += dqd * eb
                dv_s[sl, :] += dv
                return dst * dec + _dot(dob, qdb, TN)

            lax.fori_loop(0, nc, bstep, jnp.zeros((HEAD, HEAD), F32))
            dlogf = _chunk_scan(db_s[...], row, not reverse, s_)
            df = dlogf / f - dk_s[...]
            dhf_ref[...] = (df * (1.0 - lb) * sf * (1.0 - sf)).astype(BF16)
            dlb = jnp.sum(df * (1.0 - sf), axis=0, keepdims=True) * lb * (1.0 - lb)
            dlb_ref[0:1, :] += dlb
            dlb_ref[1:2, :] -= dlb

        dhq_ref[...] = (dq_s[...] * _dsilu(hq, sq)).astype(BF16)
        dhi_ref[...] = dv_s[...].astype(BF16)

    def col(part):
        return pl.BlockSpec((s_, HEAD), lambda h, b, part=part: (b, part * h_ + h))

    tab = pl.BlockSpec((2, HEAD), lambda h, b: (0, h))
    vec = pl.BlockSpec((1, HEAD), lambda h, b: (0, h))
    blk = pl.BlockSpec((s_, HEAD), lambda h, b: (b, h))
    act = jax.ShapeDtypeStruct((b_ * s_, d_), BF16)
    return pl.pallas_call(
        body, name=name, grid=(h_, b_),
        in_specs=[col(0), col(1), col(2), col(3), col(4), tab, tab, vec, blk, blk],
        out_specs=[blk] * 5 + [vec, tab, tab],
        out_shape=[act] * 5 + [jax.ShapeDtypeStruct((1, d_), F32), jax.ShapeDtypeStruct((2, d_), F32),
                               jax.ShapeDtypeStruct((2, d_), F32)],
        scratch_shapes=[pltpu.VMEM((s_, HEAD), F32)] * 8 + [pltpu.VMEM((nc, HEAD, HEAD), F32)],
        compiler_params=_params(),
    )(proj, proj, proj, proj, proj, lbf, lbb, ng, o_sum, dya)


FWD_BLOCK = 128
BWD_BLOCK = 128


def _block_scan(x, row, reverse, size, blk):
    s = 1
    while s < blk:
        if reverse:
            x = x + jnp.where(row < blk - s, pltpu.roll(x, size - s, 0), 0.0)
        else:
            x = x + jnp.where(row >= s, pltpu.roll(x, s, 0), 0.0)
        s *= 2
    return x


def _block_exps(l, reverse):
    blk = l.shape[0]
    half = blk // 2
    first = lax.broadcasted_iota(jnp.int32, (blk, HEAD), 0) < half
    q1, q3 = half // 2, half + half // 2
    if reverse:
        rho1, rho2, lh, ltot = l[q1:q1 + 1], l[q3:q3 + 1], l[half:half + 1], l[0:1]
    else:
        rho1, rho2, lh, ltot = l[q1 - 1:q1], l[q3 - 1:q3], l[half - 1:half], l[blk - 1:blk]
    ref = jnp.where(first, rho1, rho2)
    query_half = first if reverse else jnp.logical_not(first)
    e2 = jnp.where(query_half, jnp.exp(jnp.minimum(l - lh, 0.0)), 0.0)
    e1 = jnp.where(query_half, 0.0, jnp.exp(jnp.minimum(lh - l, 0.0)))
    return (jnp.exp(l - ref), jnp.exp(ref - l), e2, e1, jnp.exp(l), jnp.exp(ltot - l),
            jnp.exp(ltot), jnp.exp(lh), jnp.exp(ltot - lh))


def _half_mask(reverse, blk):
    r = lax.broadcasted_iota(jnp.int32, (blk, blk), 0)
    c = lax.broadcasted_iota(jnp.int32, (blk, blk), 1)
    same = (r < blk // 2) == (c < blk // 2)
    return same & ((c >= r) if reverse else (r >= c))


def _hgrn_fwd(proj, lbf, lbb, ng, b_, s_, d_, name):
    h_ = d_ // HEAD
    BLOCK = min(FWD_BLOCK, s_)
    nb = s_ // BLOCK

    def body(hq_ref, hff_ref, hfb_ref, hi_ref, hog_ref, lbf_ref, lbb_ref, ng_ref, ya_ref, o_ref,
             q_s, k_s, l_s, of_s, oi_s, qd_s, u_s, st_s, dec_s):
        row = lax.broadcasted_iota(jnp.int32, (s_, HEAD), 0) % BLOCK
        hq = hq_ref[...]
        q_s[...] = hq * _sigmoid(hq)
        for reverse, hf_ref, lb_ref in ((False, hff_ref, lbf_ref), (True, hfb_ref, lbb_ref)):
            lb = _lower_bound(lb_ref[...])
            f = lb + (1.0 - lb) * _sigmoid(hf_ref[...])
            k_s[...] = 1.0 - f
            l_s[...] = _block_scan(jnp.log(f), row, reverse, s_, BLOCK)
            mask = _half_mask(reverse, BLOCK)

            def inside(n, carry, reverse=reverse, mask=mask):
                sl = pl.ds(pl.multiple_of(n * BLOCK, BLOCK), BLOCK)
                eq, ek, e2, e1, el, ee, dec, _, _ = _block_exps(l_s[sl, :], reverse)
                qc, kc = q_s[sl, :], k_s[sl, :]
                vb = hi_ref[sl, :].astype(BF16)
                a = jnp.where(mask, _dot((qc * eq).astype(BF16), (kc * ek).astype(BF16), NT), 0.0)
                a = a + _dot((qc * e2).astype(BF16), (kc * e1).astype(BF16), NT)
                oi_s[sl, :] = _dot(a.astype(BF16), vb, NN)
                qd_s[sl, :] = (qc * el).astype(BF16)
                u_s[n] = _dot(vb, (kc * ee).astype(BF16), TN)
                dec_s[n] = jnp.broadcast_to(dec, (8, HEAD))
                return carry

            lax.fori_loop(0, nb, inside, 0, unroll=4)

            def carry_state(n, st, reverse=reverse):
                idx = (nb - 1 - n) if reverse else n
                st_s[idx] = st.astype(BF16)
                return st * dec_s[idx][0:1, :] + u_s[idx]

            lax.fori_loop(0, nb, carry_state, jnp.zeros((HEAD, HEAD), F32))

            def across(n, carry, reverse=reverse):
                sl = pl.ds(pl.multiple_of(n * BLOCK, BLOCK), BLOCK)
                o_dir = oi_s[sl, :] + _dot(qd_s[sl, :], st_s[n], NT)
                if not reverse:
                    of_s[sl, :] = o_dir
                else:
                    o = of_s[sl, :] + o_dir
                    o_ref[sl, :] = o
                    nrm = o * lax.rsqrt(jnp.mean(o * o, axis=-1, keepdims=True) + LN_EPS)
                    hog = hog_ref[sl, :]
                    ya_ref[sl, :] = (nrm * ng_ref[...] * hog * _sigmoid(hog)).astype(BF16)
                return carry

            lax.fori_loop(0, nb, across, 0, unroll=8)

    def col(part):
        return pl.BlockSpec((s_, HEAD), lambda h, b, part=part: (b, part * h_ + h))

    tab = pl.BlockSpec((2, HEAD), lambda h, b: (0, h))
    out = pl.BlockSpec((s_, HEAD), lambda h, b: (b, h))
    return pl.pallas_call(
        body, name=name, grid=(h_, b_),
        in_specs=[col(0), col(1), col(2), col(3), col(4), tab, tab,
                  pl.BlockSpec((1, HEAD), lambda h, b: (0, h))],
        out_specs=[out, out],
        out_shape=[jax.ShapeDtypeStruct((b_ * s_, d_), BF16), jax.ShapeDtypeStruct((b_ * s_, d_), F32)],
        scratch_shapes=[pltpu.VMEM((s_, HEAD), F32)] * 5 + [
            pltpu.VMEM((s_, HEAD), BF16), pltpu.VMEM((nb, HEAD, HEAD), F32), pltpu.VMEM((nb, HEAD, HEAD), BF16),
            pltpu.VMEM((nb, 8, HEAD), F32)],
        compiler_params=_params(),
    )(proj, proj, proj, proj, proj, lbf, lbb, ng)


def _hgrn_bwd(proj, lbf, lbb, ng, o_sum, dya, b_, s_, d_, name):
    h_ = d_ // HEAD
    BLOCK = min(BWD_BLOCK, s_)
    nb = s_ // BLOCK

    def body(hq_ref, hff_ref, hfb_ref, hi_ref, hog_ref, lbf_ref, lbb_ref, ng_ref, o_ref, dya_ref,
             dhq_ref, dhff_ref, dhfb_ref, dhi_ref, dhog_ref, dng_ref, dlbf_ref, dlbb_ref,
             q_s, k_s, l_s, do_s, dq_s, dv_s, dl_s, dk_s, u_s, w_s, st_s, dst_s, dec_s):
        b = pl.program_id(1)

        @pl.when(b == 0)
        def _():
            dng_ref[...] = jnp.zeros_like(dng_ref)
            dlbf_ref[...] = jnp.zeros_like(dlbf_ref)
            dlbb_ref[...] = jnp.zeros_like(dlbb_ref)

        row = lax.broadcasted_iota(jnp.int32, (s_, HEAD), 0) % BLOCK
        brow = lax.broadcasted_iota(jnp.int32, (BLOCK, HEAD), 0)
        hq = hq_ref[...]
        sq = _sigmoid(hq)
        q_s[...] = hq * sq
        o = o_ref[...]
        rinv = lax.rsqrt(jnp.mean(o * o, axis=-1, keepdims=True) + LN_EPS)
        nrm = o * rinv
        hog = hog_ref[...]
        so = _sigmoid(hog)
        gain = ng_ref[...]
        dy = dya_ref[...]
        dhog_ref[...] = (dy * nrm * gain * _dsilu(hog, so)).astype(BF16)
        dng_ref[...] += jnp.sum(dy * nrm * hog * so, axis=0, keepdims=True)
        dn = dy * gain * hog * so
        do_s[...] = rinv * (dn - nrm * jnp.mean(dn * nrm, axis=-1, keepdims=True))

        for reverse, hf_ref, lb_ref, dhf_ref, dlb_ref in (
                (False, hff_ref, lbf_ref, dhff_ref, dlbf_ref), (True, hfb_ref, lbb_ref, dhfb_ref, dlbb_ref)):
            lb = _lower_bound(lb_ref[...])
            sf = _sigmoid(hf_ref[...])
            f = lb + (1.0 - lb) * sf
            k_s[...] = 1.0 - f
            l_s[...] = _block_scan(jnp.log(f), row, reverse, s_, BLOCK)
            mask = _half_mask(reverse, BLOCK)
            total_row = 0 if reverse else BLOCK - 1
            key_end = BLOCK // 2 if reverse else BLOCK // 2 - 1

            def prepare(n, carry, reverse=reverse):
                sl = pl.ds(pl.multiple_of(n * BLOCK, BLOCK), BLOCK)
                _, _, _, _, el, ee, dec, _, _ = _block_exps(l_s[sl, :], reverse)
                vb = hi_ref[sl, :].astype(BF16)
                u_s[n] = _dot(vb, (k_s[sl, :] * ee).astype(BF16), TN)
                w_s[n] = _dot(do_s[sl, :].astype(BF16), (q_s[sl, :] * el).astype(BF16), TN)
                dec_s[n] = jnp.broadcast_to(dec, (8, HEAD))
                return carry

            lax.fori_loop(0, nb, prepare, 0, unroll=4)

            def carry_state(n, st, reverse=reverse):
                idx = (nb - 1 - n) if reverse else n
                st_s[idx] = st
                return st * dec_s[idx][0:1, :] + u_s[idx]

            lax.fori_loop(0, nb, carry_state, jnp.zeros((HEAD, HEAD), F32))

            def carry_grad(n, dst, reverse=reverse):
                idx = n if reverse else (nb - 1 - n)
                dst_s[idx] = dst
                return dst * dec_s[idx][0:1, :] + w_s[idx]

            lax.fori_loop(0, nb, carry_grad, jnp.zeros((HEAD, HEAD), F32))

            def inside(n, carry, reverse=reverse, mask=mask, total_row=total_row, key_end=key_end):
                sl = pl.ds(pl.multiple_of(n * BLOCK, BLOCK), BLOCK)
                eq, ek, e2, e1, el, ee, dec, dec_key, dec_query = _block_exps(l_s[sl, :], reverse)
                qc, kc = q_s[sl, :], k_s[sl, :]
                vb = hi_ref[sl, :].astype(BF16)
                dob = do_s[sl, :].astype(BF16)
                qt, kt, q2, k1 = ((qc * eq).astype(BF16), (kc * ek).astype(BF16),
                                  (qc * e2).astype(BF16), (kc * e1).astype(BF16))
                kend = kc * ee
                st0, dst1 = st_s[n], dst_s[n]
                dstb = dst1.astype(BF16)
                a = jnp.where(mask, _dot(qt, kt, NT), 0.0) + _dot(q2, k1, NT)
                da = _dot(dob, vb, NT)
                dab = da.astype(BF16)
                dad = jnp.where(mask, da, 0.0).astype(BF16)
                dqt, dkt = _dot(dad, kt, NN), _dot(dad, qt, TN)
                dq2, dk1 = _dot(dab, k1, NN), _dot(dab, q2, TN)
                dqd = _dot(dob, st0.astype(BF16), NN)
                dke = _dot(vb, dstb, NN)
                dv = _dot(a.astype(BF16), dob, TN) + _dot(kend.astype(BF16), dstb, NT)
                dq = dqt * eq + dq2 * e2 + dqd * el
                dk = dkt * ek + dk1 * e1 + dke * ee
                dtot = jnp.sum(dke * kend, axis=0, keepdims=True) + jnp.sum(dst1 * st0, axis=0, keepdims=True) * dec
                st_mid = st0 * dec_key + _dot(vb, k1, TN)
                dst_mid = dst1 * dec_query + _dot(dob, q2, TN)
                dmid = jnp.sum(dst_mid * st_mid, axis=0, keepdims=True)
                dl_s[sl, :] = (qc * dq - kc * dk + jnp.where(brow == total_row, dtot, 0.0)
                               + jnp.where(brow == key_end, dmid, 0.0))
                dk_s[sl, :] = dk
                if not reverse:
                    dq_s[sl, :] = dq
                    dv_s[sl, :] = dv
                else:
                    hqc = hq_ref[sl, :]
                    dhq_ref[sl, :] = ((dq_s[sl, :] + dq) * _dsilu(hqc, _sigmoid(hqc))).astype(BF16)
                    dhi_ref[sl, :] = (dv_s[sl, :] + dv).astype(BF16)
                return carry

            lax.fori_loop(0, nb, inside, 0, unroll=4)
            dlogf = _block_scan(dl_s[...], row % (BLOCK // 2), not reverse, s_, BLOCK // 2)
            df = dlogf / f - dk_s[...]
            dhf_ref[...] = (df * (1.0 - lb) * sf * (1.0 - sf)).astype(BF16)
            dlb = jnp.sum(df * (1.0 - sf), axis=0, keepdims=True) * lb * (1.0 - lb)
            dlb_ref[0:1, :] += dlb
            dlb_ref[1:2, :] -= dlb

    def col(part):
        return pl.BlockSpec((s_, HEAD), lambda h, b, part=part: (b, part * h_ + h))

    tab = pl.BlockSpec((2, HEAD), lambda h, b: (0, h))
    vec = pl.BlockSpec((1, HEAD), lambda h, b: (0, h))
    blk = pl.BlockSpec((s_, HEAD), lambda h, b: (b, h))
    act = jax.ShapeDtypeStruct((b_ * s_, d_), BF16)
    state = pltpu.VMEM((nb, HEAD, HEAD), F32)
    return pl.pallas_call(
        body, name=name, grid=(h_, b_),
        in_specs=[col(0), col(1), col(2), col(3), col(4), tab, tab, vec, blk, blk],
        out_specs=[blk] * 5 + [vec, tab, tab],
        out_shape=[act] * 5 + [jax.ShapeDtypeStruct((1, d_), F32), jax.ShapeDtypeStruct((2, d_), F32),
                               jax.ShapeDtypeStruct((2, d_), F32)],
        scratch_shapes=[pltpu.VMEM((s_, HEAD), F32)] * 8 + [state] * 4 + [pltpu.VMEM((nb, 8, HEAD), F32)],
        compiler_params=_params(),
    )(proj, proj, proj, proj, proj, lbf, lbb, ng, o_sum, dya)


def _rope_tables(s_):
    half = ROPE_DIM // 2
    inv_freq = ROPE_THETA ** (-jnp.arange(0, ROPE_DIM, 2, dtype=F32) / ROPE_DIM)
    ang = jnp.arange(s_, dtype=F32)[:, None] * inv_freq
    cos, sin = jnp.cos(ang), jnp.sin(ang)
    zeros = jnp.zeros((s_, HEAD - ROPE_DIM), F32)
    zh = jnp.zeros((s_, half), F32)
    c = jnp.concatenate([cos, cos, jnp.ones((s_, HEAD - ROPE_DIM), F32)], axis=1)
    s1 = jnp.concatenate([-sin, zh, zeros], axis=1)
    s2 = jnp.concatenate([zh, sin, zeros], axis=1)
    return c, s1, s2


def _rope(t, c, s1, s2):
    half = ROPE_DIM // 2
    return t * c + pltpu.roll(t, HEAD - half, 1) * s1 + pltpu.roll(t, half, 1) * s2


def _rope_bwd(dt, c, s1, s2):
    half = ROPE_DIM // 2
    return dt * c + pltpu.roll(dt * s1, half, 1) + pltpu.roll(dt * s2, HEAD - half, 1)


def _window_mask(r0, qb, wk, seg):
    row = lax.broadcasted_iota(jnp.int32, (qb, wk), 0)
    col = lax.broadcasted_iota(jnp.int32, (qb, wk), 1)
    kj = r0 - ATTN_HALF + col
    return (col - row >= 0) & (col - row <= 2 * ATTN_HALF) & (kj >= 0) & (kj < seg)


def _attn_fwd(proj, tabs, b_, s_, col0, dil, name):
    seg = s_ // dil
    qb = min(128, seg)
    nq, wk = seg // qb, qb + 2 * ATTN_HALF
    scale = HEAD ** -0.5

    def body(q_ref, k_ref, v_ref, c_ref, s1_ref, s2_ref, o_ref, lse_ref, q_s, k_s, v_s):
        k_s[...] = jnp.zeros_like(k_s)
        v_s[...] = jnp.zeros_like(v_s)

        def residue(r, carry):
            cls = pl.ds(r, seg, stride=dil)
            c, s1, s2 = c_ref[cls, :], s1_ref[cls, :], s2_ref[cls, :]
            q_s[...] = _rope(q_ref[cls, :], c, s1, s2).astype(BF16)
            k_s[ATTN_HALF:ATTN_HALF + seg, :] = _rope(k_ref[cls, :], c, s1, s2).astype(BF16)
            v_s[ATTN_HALF:ATTN_HALF + seg, :] = v_ref[cls, :].astype(BF16)

            def step(i, carry):
                r0 = pl.multiple_of(i * qb, qb)
                sc = _dot(q_s[pl.ds(r0, qb), :], k_s[pl.ds(r0, wk), :], NT) * scale
                sc = jnp.where(_window_mask(r0, qb, wk, seg), sc, NEG_INF)
                m = jnp.max(sc, axis=-1, keepdims=True)
                p = jnp.exp(sc - m)
                den = jnp.sum(p, axis=-1, keepdims=True)
                rows = pl.ds(r + r0 * dil, qb, stride=dil)
                o_ref[rows, :] = _dot(p.astype(BF16), v_s[pl.ds(r0, wk), :], NN) / den
                lse_ref[rows, :] = jnp.broadcast_to(m + jnp.log(den), (qb, HEAD))
                return carry

            return lax.fori_loop(0, nq, step, carry)

        lax.fori_loop(0, dil, residue, 0)

    def col(part):
        return pl.BlockSpec((s_, HEAD), lambda b, h, part=part: (b, col0 + part * ATTN_HEADS + h))

    tab = pl.BlockSpec((s_, HEAD), lambda b, h: (0, 0))
    out = pl.BlockSpec((s_, HEAD), lambda b, h: (b, h))
    shape = jax.ShapeDtypeStruct((b_ * s_, ATTN_OUT), F32)
    return pl.pallas_call(
        body, name=name, grid=(b_, ATTN_HEADS),
        in_specs=[col(0), col(1), col(2), tab, tab, tab],
        out_specs=[out, out],
        out_shape=[shape, shape],
        scratch_shapes=[pltpu.VMEM((seg, HEAD), BF16), pltpu.VMEM((seg + 2 * ATTN_HALF, HEAD), BF16),
                        pltpu.VMEM((seg + 2 * ATTN_HALF, HEAD), BF16)],
        compiler_params=_params(),
    )(proj, proj, proj, *tabs)


def _attn_bwd(proj, tabs, dog, cg, lse, b_, s_, col0, dil, name):
    seg = s_ // dil
    qb = min(128, seg)
    nq, wk = seg // qb, qb + 2 * ATTN_HALF
    scale = HEAD ** -0.5

    def body(q_ref, k_ref, v_ref, c_ref, s1_ref, s2_ref, do_ref, cg_ref, lse_ref, dq_ref, dk_ref, dv_ref,
             q_s, k_s, v_s, do_s, cl_s, dk_s, dv_s):
        k_s[...] = jnp.zeros_like(k_s)
        v_s[...] = jnp.zeros_like(v_s)

        def residue(r, carry):
            cls = pl.ds(r, seg, stride=dil)
            c, s1, s2 = c_ref[cls, :], s1_ref[cls, :], s2_ref[cls, :]
            q_s[...] = _rope(q_ref[cls, :], c, s1, s2).astype(BF16)
            k_s[ATTN_HALF:ATTN_HALF + seg, :] = _rope(k_ref[cls, :], c, s1, s2).astype(BF16)
            v_s[ATTN_HALF:ATTN_HALF + seg, :] = v_ref[cls, :].astype(BF16)
            do_s[...] = do_ref[cls, :].astype(BF16)
            cl_s[0] = cg_ref[cls, :]
            cl_s[1] = lse_ref[cls, :]
            dk_s[...] = jnp.zeros_like(dk_s)
            dv_s[...] = jnp.zeros_like(dv_s)

            def step(i, carry):
                r0 = pl.multiple_of(i * qb, qb)
                rows, win = pl.ds(r0, qb), pl.ds(r0, wk)
                qc, kw, vw = q_s[rows, :], k_s[win, :], v_s[win, :]
                sc = _dot(qc, kw, NT) * scale
                p = jnp.where(_window_mask(r0, qb, wk, seg), jnp.exp(sc - cl_s[1, rows, 0:1]), 0.0)
                dob = do_s[rows, :]
                dp = _dot(dob, vw, NT)
                ds = (p * (dp + cl_s[0, rows, 0:1]) * scale).astype(BF16)
                out = pl.ds(r + r0 * dil, qb, stride=dil)
                dq_ref[out, :] = _rope_bwd(_dot(ds, kw, NN), c_ref[out, :], s1_ref[out, :], s2_ref[out, :])
                dk_s[win, :] += _dot(ds, qc, TN)
                dv_s[win, :] += _dot(p.astype(BF16), dob, TN)
                return carry

            carry = lax.fori_loop(0, nq, step, carry)
            dk_ref[cls, :] = _rope_bwd(dk_s[ATTN_HALF:ATTN_HALF + seg, :], c, s1, s2)
            dv_ref[cls, :] = dv_s[ATTN_HALF:ATTN_HALF + seg, :]
            return carry

        lax.fori_loop(0, dil, residue, 0)

    def col(part):
        return pl.BlockSpec((s_, HEAD), lambda b, h, part=part: (b, col0 + part * ATTN_HEADS + h))

    tab = pl.BlockSpec((s_, HEAD), lambda b, h: (0, 0))
    out = pl.BlockSpec((s_, HEAD), lambda b, h: (b, h))
    shape = jax.ShapeDtypeStruct((b_ * s_, ATTN_OUT), F32)
    pad = seg + 2 * ATTN_HALF
    return pl.pallas_call(
        body, name=name, grid=(b_, ATTN_HEADS),
        in_specs=[col(0), col(1), col(2), tab, tab, tab, out, out, out],
        out_specs=[out, out, out],
        out_shape=[shape, shape, shape],
        scratch_shapes=[pltpu.VMEM((seg, HEAD), BF16), pltpu.VMEM((pad, HEAD), BF16), pltpu.VMEM((pad, HEAD), BF16),
                        pltpu.VMEM((seg, HEAD), BF16), pltpu.VMEM((2, seg, HEAD), F32),
                        pltpu.VMEM((pad, HEAD), F32), pltpu.VMEM((pad, HEAD), F32)],
        compiler_params=_params(),
    )(proj, proj, proj, *tabs, dog, cg, lse)


def _group_weights(lses):
    m = jnp.maximum(jnp.maximum(lses[0], lses[1]), lses[2])
    es = [jnp.exp(l - m) for l in lses]
    den = es[0] + es[1] + es[2]
    return [e / den for e in es]


def _combine_fwd(outs, lses, name):
    t_, w_ = outs[0].shape
    tm = _tile(t_, 512, 8)
    ng = len(outs)

    def body(*refs):
        ws = _group_weights([r[...] for r in refs[ng:2 * ng]])
        acc = ws[0] * refs[0][...]
        for g in range(1, ng):
            acc = acc + ws[g] * refs[g][...]
        refs[2 * ng][...] = acc.astype(BF16)

    row = pl.BlockSpec((tm, w_), lambda i: (i, 0))
    return pl.pallas_call(
        body, name=name, grid=(t_ // tm,), in_specs=[row] * (2 * ng), out_specs=row,
        out_shape=jax.ShapeDtypeStruct((t_, w_), BF16), compiler_params=_params(),
    )(*outs, *lses)


def _combine_bwd(dob, outs, lses, name):
    t_, w_ = outs[0].shape
    tm = _tile(t_, 512, 8)
    ng = len(outs)

    def body(*refs):
        do = refs[0][...]
        os_ = [r[...] for r in refs[1:1 + ng]]
        ws = _group_weights([r[...] for r in refs[1 + ng:1 + 2 * ng]])
        o = ws[0] * os_[0]
        for g in range(1, ng):
            o = o + ws[g] * os_[g]
        prod = do * o
        heads = [jnp.broadcast_to(jnp.sum(prod[:, h * HEAD:(h + 1) * HEAD], axis=-1, keepdims=True), (tm, HEAD))
                 for h in range(w_ // HEAD)]
        tot = jnp.concatenate(heads, axis=1)
        for g in range(ng):
            refs[1 + 2 * ng + g][...] = ws[g] * do
            refs[1 + 3 * ng + g][...] = -ws[g] * tot

    row = pl.BlockSpec((tm, w_), lambda i: (i, 0))
    shape = jax.ShapeDtypeStruct((t_, w_), F32)
    res = pl.pallas_call(
        body, name=name, grid=(t_ // tm,), in_specs=[row] * (1 + 2 * ng), out_specs=[row] * (2 * ng),
        out_shape=[shape] * (2 * ng), compiler_params=_params(),
    )(dob, *outs, *lses)
    return res[:ng], res[ng:]


def _adam_update(w, g, m, v):
    m = ADAM_B1 * m + (1.0 - ADAM_B1) * g
    v = ADAM_B2 * v + (1.0 - ADAM_B2) * (g * g)
    m_hat = m / (1.0 - ADAM_B1 ** ADAM_STEP)
    v_hat = v / (1.0 - ADAM_B2 ** ADAM_STEP)
    return -ADAM_LR * (m_hat / (jnp.sqrt(v_hat) + ADAM_EPS) + ADAM_WD * w), m, v


def _adam(w, g, m, v, name):
    r_, c_ = w.shape
    tr = _tile(r_, 256, 8)

    def body(w_ref, g_ref, m_ref, v_ref, d_ref, mo_ref, vo_ref):
        d_ref[...], mo_ref[...], vo_ref[...] = _adam_update(w_ref[...], g_ref[...], m_ref[...], v_ref[...])

    blk = pl.BlockSpec((tr, c_), lambda i: (i, 0))
    shape = jax.ShapeDtypeStruct((r_, c_), F32)
    return pl.pallas_call(
        body, name=name, grid=(r_ // tr,), in_specs=[blk] * 4, out_specs=[blk] * 3,
        out_shape=[shape] * 3, compiler_params=_params(),
    )(w, g, m, v)


def _sum_partials(recv, name):
    n_, r_, c_ = recv.shape
    tr = _tile(r_, 128, 16)

    def body(p_ref, o_ref):
        acc = p_ref[0].astype(F32)
        for i in range(1, n_):
            acc = acc + p_ref[i].astype(F32)
        o_ref[...] = acc

    return pl.pallas_call(
        body, name=name, grid=(r_ // tr,),
        in_specs=[pl.BlockSpec((n_, tr, c_), lambda i: (0, i, 0))],
        out_specs=pl.BlockSpec((tr, c_), lambda i: (i, 0)),
        out_shape=jax.ShapeDtypeStruct((r_, c_), F32), compiler_params=_params(),
    )(recv)


def _small_sum_adam(parts, w, m, v, name):
    n_, r_, c_ = parts.shape

    def body(p_ref, w_ref, m_ref, v_ref, g_ref, d_ref, mo_ref, vo_ref):
        g = p_ref[0]
        for i in range(1, n_):
            g = g + p_ref[i]
        g_ref[...] = g
        d_ref[...], mo_ref[...], vo_ref[...] = _adam_update(w_ref[...], g, m_ref[...], v_ref[...])

    shape = jax.ShapeDtypeStruct((r_, c_), F32)
    return pl.pallas_call(body, name=name, out_shape=[shape] * 4, compiler_params=_params())(parts, w, m, v)


def _my_place():
    x, y, c = lax.axis_index("x"), lax.axis_index("y"), lax.axis_index("c")
    return x, y, c


def _peer(x, y, c, d):
    px = 1 - x if d & 4 else x
    py = 1 - y if d & 2 else y
    pc = 1 - c if d & 1 else c
    return (px, py, pc), 4 * px + 2 * py + pc


def _all_gather(shards, name):
    nw = len(shards)

    def body(*refs):
        ins, outs = refs[:nw], refs[nw:2 * nw]
        send_sems, recv_sems, local_sems = refs[2 * nw:]
        x, y, c = _my_place()
        me = 4 * x + 2 * y + c
        copies = []
        for k in range(nw):
            rows = shards[k].shape[0]
            mine = outs[k].at[pl.ds(pl.multiple_of(me * rows, 16), rows), :]
            local = pltpu.make_async_copy(ins[k], mine, local_sems.at[k])
            local.start()
            copies.append(local)
            for d in range(1, N_DEV):
                place, _ = _peer(x, y, c, d)
                remote = pltpu.make_async_remote_copy(
                    src_ref=ins[k], dst_ref=mine, send_sem=send_sems.at[d - 1, k], recv_sem=recv_sems.at[d - 1, k],
                    device_id=place, device_id_type=MESH)
                remote.start()
                copies.append(remote)
        for cp in copies:
            cp.wait()

    hbm = pl.BlockSpec(memory_space=pl.ANY)
    return pl.pallas_call(
        body, name=name, in_specs=[hbm] * nw, out_specs=[hbm] * nw,
        out_shape=[jax.ShapeDtypeStruct((N_DEV * s.shape[0], s.shape[1]), s.dtype) for s in shards],
        scratch_shapes=[pltpu.SemaphoreType.DMA((N_DEV - 1, nw)), pltpu.SemaphoreType.DMA((N_DEV - 1, nw)),
                        pltpu.SemaphoreType.DMA((nw,))],
    )(*shards)


HBM_SPEC = pl.BlockSpec(memory_space=pltpu.HBM)
SEM_SPEC = pl.BlockSpec(memory_space=pltpu.SEMAPHORE)
EFFECT = pltpu.SideEffectType.DATAFLOW_SIDE_EFFECTING


def _in_hbm(a):
    return pltpu.with_memory_space_constraint(a, pltpu.HBM)


def _token_shape():
    return jax.ShapeDtypeStruct((8, HEAD), F32)


SIBLING = 1
OTHER_CHIPS = (4, 2, 6)


def _rows_of(ref, num, rows):
    return ref.at[pl.ds(pl.multiple_of(num * rows, 16), rows), :]


def _gather_start(shards, name):
    nw = len(shards)
    lands = [lax.empty((N_DEV * s.shape[0], s.shape[1]), s.dtype) for s in shards]
    n_to = 1 + len(OTHER_CHIPS)

    def body(*refs):
        ins, lnd = refs[:nw], refs[nw:2 * nw]
        send, from_sib, from_chips, own = (refs[(2 + i) * nw:(3 + i) * nw] for i in range(4))
        token = refs[8 * nw]
        x, y, c = _my_place()
        me = 4 * x + 2 * y + c
        for k in range(nw):
            mine = _rows_of(lnd[k], me, shards[k].shape[0])
            pltpu.make_async_copy(ins[k], mine, own[k]).start()
            for i, d in enumerate((SIBLING,) + OTHER_CHIPS):
                place, _ = _peer(x, y, c, d)
                pltpu.make_async_remote_copy(
                    src_ref=ins[k], dst_ref=mine, send_sem=send[k].at[i],
                    recv_sem=from_sib[k] if i == 0 else from_chips[k].at[i - 1],
                    device_id=place, device_id_type=MESH).start()
        token[...] = jnp.zeros_like(token)

    dma = pltpu.SemaphoreType.DMA
    sems = [dma((n_to,))] * nw + [dma(())] * nw + [dma((len(OTHER_CHIPS),))] * nw + [dma(())] * nw
    thru = [pltpu.HBM(a.shape, a.dtype) for a in list(shards) + lands]
    res = pl.pallas_call(
        body, name=name, out_shape=(*sems, *thru, _token_shape()),
        in_specs=[HBM_SPEC] * (2 * nw),
        out_specs=(*([SEM_SPEC] * (4 * nw)), *([HBM_SPEC] * (2 * nw)), pl.BlockSpec(memory_space=pltpu.VMEM)),
        input_output_aliases={i: 4 * nw + i for i in range(2 * nw)},
        compiler_params=pltpu.CompilerParams(has_side_effects=EFFECT),
    )(*[_in_hbm(s) for s in shards], *[_in_hbm(l) for l in lands])
    return [dict(send=res[k], from_sib=res[nw + k], from_chips=res[2 * nw + k], own=res[3 * nw + k],
                 src=res[4 * nw + k], land=res[5 * nw + k]) for k in range(nw)], res[6 * nw]


def _gather_forward(pending, after, name):
    rows = pending["src"].shape[0]
    n_fw = len(OTHER_CHIPS)

    def body(land_ref, from_chips, after_ref, fw_send, fw_recv, land_thru):
        x, y, c = _my_place()
        sibling, _ = _peer(x, y, c, SIBLING)
        for j, d in enumerate(OTHER_CHIPS):
            _, num = _peer(x, y, c, d)
            block = _rows_of(land_ref, num, rows)
            pltpu.make_async_remote_copy(
                src_ref=block, dst_ref=block, send_sem=fw_send.at[j], recv_sem=from_chips.at[j],
                device_id=sibling, device_id_type=MESH).wait_recv()
            pltpu.make_async_remote_copy(
                src_ref=block, dst_ref=block, send_sem=fw_send.at[j], recv_sem=fw_recv.at[j],
                device_id=sibling, device_id_type=MESH).start()

    land = pending["land"]
    dma = pltpu.SemaphoreType.DMA
    fw_send, fw_recv, land = pl.pallas_call(
        body, name=name, out_shape=(dma((n_fw,)), dma((n_fw,)), pltpu.HBM(land.shape, land.dtype)),
        in_specs=(HBM_SPEC, SEM_SPEC, pl.BlockSpec(memory_space=pl.ANY)),
        out_specs=(SEM_SPEC, SEM_SPEC, HBM_SPEC), input_output_aliases={0: 2},
        compiler_params=pltpu.CompilerParams(has_side_effects=EFFECT),
    )(land, pending["from_chips"], after)
    return dict(pending, land=land, fw_send=fw_send, fw_recv=fw_recv)


def _gather_wait(pending, name):
    rows = pending["src"].shape[0]

    def body(src_ref, land_ref, send, from_sib, own, fw_send, fw_recv, src_dead, got):
        x, y, c = _my_place()
        me = 4 * x + 2 * y + c
        sibling, sib_num = _peer(x, y, c, SIBLING)
        mine = _rows_of(land_ref, me, rows)
        pltpu.make_async_copy(src_ref, mine, own).wait()
        for i in range(1 + len(OTHER_CHIPS)):
            pltpu.make_async_remote_copy(
                src_ref=src_ref, dst_ref=mine, send_sem=send.at[i], recv_sem=from_sib,
                device_id=sibling, device_id_type=MESH).wait_send()
        theirs = _rows_of(land_ref, sib_num, rows)
        pltpu.make_async_remote_copy(
            src_ref=src_ref, dst_ref=theirs, send_sem=send.at[0], recv_sem=from_sib,
            device_id=sibling, device_id_type=MESH).wait_recv()
        for j, d in enumerate(OTHER_CHIPS):
            _, num = _peer(x, y, c, d)
            sent = _rows_of(land_ref, num, rows)
            _, got_num = _peer(x, y, c, d | SIBLING)
            arrived = _rows_of(land_ref, got_num, rows)
            cp = pltpu.make_async_remote_copy(
                src_ref=sent, dst_ref=arrived, send_sem=fw_send.at[j], recv_sem=fw_recv.at[j],
                device_id=sibling, device_id_type=MESH)
            cp.wait_send()
            cp.wait_recv()

    src, land = pending["src"], pending["land"]
    return pl.pallas_call(
        body, name=name, out_shape=(pltpu.HBM(src.shape, src.dtype), pltpu.HBM(land.shape, land.dtype)),
        in_specs=(HBM_SPEC, HBM_SPEC) + (SEM_SPEC,) * 5,
        out_specs=(HBM_SPEC, HBM_SPEC), input_output_aliases={0: 0, 1: 1},
        compiler_params=pltpu.CompilerParams(has_side_effects=EFFECT),
    )(src, land, pending["send"], pending["from_sib"], pending["own"], pending["fw_send"], pending["fw_recv"])[1]


def _scatter_start(full, name):
    rows, cols = full.shape[0] // N_DEV, full.shape[1]
    land = lax.empty((N_DEV, rows, cols), full.dtype)

    def body(full_ref, land_ref, send, recv, own, full_thru, land_thru, token):
        x, y, c = _my_place()
        me = 4 * x + 2 * y + c
        slab = land_ref.at[me]
        pltpu.make_async_copy(full_ref.at[pl.ds(pl.multiple_of(me * rows, 16), rows), :], slab, own).start()
        for d in range(1, N_DEV):
            place, num = _peer(x, y, c, d)
            pltpu.make_async_remote_copy(
                src_ref=full_ref.at[pl.ds(pl.multiple_of(num * rows, 16), rows), :], dst_ref=slab,
                send_sem=send.at[d - 1], recv_sem=recv.at[d - 1], device_id=place, device_id_type=MESH).start()
        token[...] = jnp.zeros_like(token)

    res = pl.pallas_call(
        body, name=name,
        out_shape=(pltpu.SemaphoreType.DMA((N_DEV - 1,)), pltpu.SemaphoreType.DMA((N_DEV - 1,)),
                   pltpu.SemaphoreType.DMA(()),
                   pltpu.HBM(full.shape, full.dtype), pltpu.HBM(land.shape, land.dtype), _token_shape()),
        in_specs=(HBM_SPEC, HBM_SPEC),
        out_specs=(SEM_SPEC, SEM_SPEC, SEM_SPEC, HBM_SPEC, HBM_SPEC, pl.BlockSpec(memory_space=pltpu.VMEM)),
        input_output_aliases={0: 3, 1: 4},
        compiler_params=pltpu.CompilerParams(has_side_effects=EFFECT),
    )(_in_hbm(full), _in_hbm(land))
    return dict(send=res[0], recv=res[1], own=res[2], src=res[3], land=res[4]), res[5]


def _scatter_wait(pending, after, name):
    rows = pending["land"].shape[1]

    def body(src_ref, land_ref, send, recv, own, after_ref, src_dead, got):
        x, y, c = _my_place()
        me = 4 * x + 2 * y + c
        pltpu.make_async_copy(src_ref.at[pl.ds(pl.multiple_of(me * rows, 16), rows), :], land_ref.at[me], own).wait()
        for d in range(1, N_DEV):
            place, num = _peer(x, y, c, d)
            cp = pltpu.make_async_remote_copy(
                src_ref=src_ref.at[pl.ds(pl.multiple_of(num * rows, 16), rows), :], dst_ref=land_ref.at[me],
                send_sem=send.at[d - 1], recv_sem=recv.at[d - 1], device_id=place, device_id_type=MESH)
            cp.wait_send()
            cp.wait_recv()

    src, land = pending["src"], pending["land"]
    return pl.pallas_call(
        body, name=name, out_shape=(pltpu.HBM(src.shape, src.dtype), pltpu.HBM(land.shape, land.dtype)),
        in_specs=(HBM_SPEC, HBM_SPEC, SEM_SPEC, SEM_SPEC, SEM_SPEC, pl.BlockSpec(memory_space=pl.ANY)),
        out_specs=(HBM_SPEC, HBM_SPEC), input_output_aliases={0: 0, 1: 1},
        compiler_params=pltpu.CompilerParams(has_side_effects=EFFECT),
    )(src, land, pending["send"], pending["recv"], pending["own"], after)[1]


BIG = ("ffn1_w_in", "ffn1_w_out", "mix_w_in", "w_branch_a", "w_branch_b", "mix_w_out", "ffn2_w_in", "ffn2_w_out")
TRANSPOSED = ("ffn1_w_in", "mix_w_in", "w_branch_b", "ffn2_w_in")
SMALL = ("ln1_g", "ln1_b", "ln2_g", "ln2_b", "ln3_g", "ln3_b", "hgrn_norm_g", "hgrn_lb_fwd", "hgrn_lb_bwd")
SMALL_ROWS = 16


def _local_step(x, target, weight, emit, emit_small, sp):
    b_, s_, d_ = x.shape
    t_ = b_ * s_
    x2, tgt = x.reshape(t_, d_), target.reshape(t_, d_)
    xb = x2.astype(BF16)
    w1i = weight("ffn1_w_in", xb)
    g1, u1, a1 = _ffn_in_fwd(xb, w1i, "ffn1_in")
    w1o = weight("ffn1_w_out", a1)
    r1, h1, h1b = _mm_res_ln_fwd(a1, w1o, x2, sp["ln1_g"], sp["ln1_b"], 0.5, "ffn1_out_ln1")
    wmx = weight("mix_w_in", h1b)
    proj = _mm_nt(h1b, wmx, F32, "mix_in")
    ya_in, o_sum = _hgrn_fwd(proj, sp["hgrn_lb_fwd"], sp["hgrn_lb_bwd"], sp["hgrn_norm_g"], b_, s_, d_, "hgrn_fwd")
    tabs = _rope_tables(s_)
    outs, lses = [], []
    group_col = [(5 * d_ + gi * QKV_GROUP) // HEAD for gi in range(len(ATTN_GROUPS))]
    for gi, (_, dil) in enumerate(ATTN_GROUPS):
        o_g, lse_g = _attn_fwd(proj, tabs, b_, s_, group_col[gi], dil, f"attn_fwd_{gi}")
        outs.append(o_g)
        lses.append(lse_g)
    ob = _combine_fwd(outs, lses, "attn_combine")
    wa, wb = weight("w_branch_a", ya_in), weight("w_branch_b", ob)
    ya, yb, z = _gate_out_fwd(ya_in, ob, wa, wb, proj, d_, "branch_gate")
    wo = weight("mix_w_out", z)
    r2, h2, h2b = _mm_res_ln_fwd(z, wo, h1, sp["ln2_g"], sp["ln2_b"], 1.0, "mix_out_ln2")
    w2i = weight("ffn2_w_in", h2b)
    g2, u2, a2 = _ffn_in_fwd(h2b, w2i, "ffn2_in")
    w2o = weight("ffn2_w_out", a2)
    dr3, dr3b, dg3, db3, loss = _mm_res_loss_bwd(a2, w2o, h2, tgt, sp["ln3_g"], sp["ln3_b"], 0.5, "ffn2_out_loss")
    dep = emit("ffn2_w_out", _mm_tn(a2, dr3b, 0.5, "d_ffn2_w_out"))
    dgate2, dup2 = _ffn_mid_bwd(dr3b, w2o, g2, u2, 0.5, "ffn2_mid_bwd", dep)
    du2 = (dgate2, dup2)
    dep = emit("ffn2_w_in", _mm_tn(du2, h2b, 1.0, "d_ffn2_w_in"))
    dr2, dr2b, dg2, db2 = _mm_nn_res_lnbwd(du2, w2i, dr3, r2, sp["ln2_g"], "ffn2_in_bwd_ln2", dep)
    dep = emit("mix_w_out", _mm_tn(z, dr2b, 1.0, "d_mix_w_out"))
    dya, dyb, dga, dgb = _dz_gate_bwd(dr2b, wo, proj, ya, yb, d_, "branch_gate_bwd", dep)
    dep = emit("w_branch_a", _mm_tn(ya_in, dya, 1.0, "d_w_branch_a"))
    dya_in = _mm_nt(dya, wa, F32, "branch_a_bwd", dep)
    dep = emit("w_branch_b", _mm_tn(dyb, ob, 1.0, "d_w_branch_b"))
    dob = _mm_nn(dyb, wb, F32, "branch_b_bwd", dep)
    dhq, dhff, dhfb, dhi, dhog, dng, dlbf, dlbb = _hgrn_bwd(
        proj, sp["hgrn_lb_fwd"], sp["hgrn_lb_bwd"], sp["hgrn_norm_g"], o_sum, dya_in, b_, s_, d_, "hgrn_bwd")
    dogs, cgs = _combine_bwd(dob, outs, lses, "attn_combine_bwd")
    dqkv = []
    for gi, (_, dil) in enumerate(ATTN_GROUPS):
        dqkv += _attn_bwd(proj, tabs, dogs[gi], cgs[gi], lses[gi], b_, s_, group_col[gi], dil, f"attn_bwd_{gi}")
    dproj = jnp.concatenate([dhq, dhff, dhfb, dhi, dhog] + [t.astype(BF16) for t in dqkv] + [dga, dgb], axis=1)
    dep = emit("mix_w_in", _mm_tn(dproj, h1b, 1.0, "d_mix_w_in"))
    dr1, dr1b, dg1, db1 = _mm_nn_res_lnbwd(dproj, wmx, dr2, r1, sp["ln1_g"], "mix_in_bwd_ln1", dep)
    dep_small = emit_small({"ln1_g": dg1, "ln1_b": db1, "ln2_g": dg2, "ln2_b": db2, "ln3_g": dg3, "ln3_b": db3,
                            "hgrn_norm_g": dng, "hgrn_lb_fwd": dlbf, "hgrn_lb_bwd": dlbb})
    dep = emit("ffn1_w_out", _mm_tn(a1, dr1b, 0.5, "d_ffn1_w_out")) + dep_small
    dgate1, dup1 = _ffn_mid_bwd(dr1b, w1o, g1, u1, 0.5, "ffn1_mid_bwd", dep)
    du1 = (dgate1, dup1)
    dep = emit("ffn1_w_in", _mm_tn(du1, xb, 1.0, "d_ffn1_w_in"))
    grad_x = _mm_nn_res(du1, w1i, dr1, "ffn1_in_bwd", dep)
    return loss, grad_x.reshape(b_, s_, d_)


def _pack_small(vals):
    rows = jnp.concatenate([vals[n] for n in SMALL], axis=0)
    return jnp.pad(rows, ((0, SMALL_ROWS - rows.shape[0]), (0, 0)))


def _unpack_small(packed):
    out, r = {}, 0
    for n in SMALL:
        k = 2 if n.startswith("hgrn_lb") else 1
        out[n] = packed[r:r + k]
        r += k
    return out


def kernel(x, ffn1_w_in, ffn1_w_out, ln1_g, ln1_b, mix_w_in, hgrn_lb_fwd, hgrn_lb_bwd, hgrn_norm_g, w_branch_a, w_branch_b, mix_w_out, ln2_g, ln2_b, ffn2_w_in, ffn2_w_out, ln3_g, ln3_b, loss_target, m_ffn1_w_in, m_ffn1_w_out, m_ln1_g, m_ln1_b, m_mix_w_in, m_hgrn_lb_fwd, m_hgrn_lb_bwd, m_hgrn_norm_g, m_w_branch_a, m_w_branch_b, m_mix_w_out, m_ln2_g, m_ln2_b, m_ffn2_w_in, m_ffn2_w_out, m_ln3_g, m_ln3_b, v_ffn1_w_in, v_ffn1_w_out, v_ln1_g, v_ln1_b, v_mix_w_in, v_hgrn_lb_fwd, v_hgrn_lb_bwd, v_hgrn_norm_g, v_w_branch_a, v_w_branch_b, v_mix_w_out, v_ln2_g, v_ln2_b, v_ffn2_w_in, v_ffn2_w_out, v_ln3_g, v_ln3_b):
    args = dict(locals())
    big_w = {n: args[n][0] for n in BIG}
    sp = {n: args[n] for n in SMALL}
    def rows_bf16(n, zero=0.0):
        w = big_w[n] + zero
        return (w.T if n in TRANSPOSED else w).astype(BF16)

    first, rest = BIG[:2], BIG[2:]
    pending, token = _gather_start([rows_bf16(n) for n in first], "gather_start_ffn1")
    gathering = dict(zip(first, pending))
    pending, all_started = _gather_start([rows_bf16(n, token[0, 0]) for n in rest], "gather_start_rest")
    gathering.update(zip(rest, pending))
    scattering = {}

    def weight(n, after):
        if n == first[0]:
            after = all_started
        return _gather_wait(_gather_forward(gathering[n], after, f"gather_forward_{n}"), f"gather_wait_{n}")

    def emit(n, grad):
        scattering[n], token = _scatter_start(grad, f"scatter_start_{n}")
        return token

    def emit_small(grads):
        pending, token = _gather_start([_pack_small(grads)], "gather_start_small")
        scattering["small"] = pending[0]
        return token

    loss_part, grad_x = _local_step(x, loss_target, weight, emit, emit_small, sp)
    loss = lax.psum(loss_part[0, 0], ("x", "y", "c"))
    out_g, out_d, out_m, out_v = {}, {}, {}, {}
    done = grad_x
    for n in ("ffn2_w_out", "ffn2_w_in", "mix_w_out", "w_branch_a", "w_branch_b", "mix_w_in", "small",
              "ffn1_w_out", "ffn1_w_in"):
        if n == "small":
            parts = _gather_wait(_gather_forward(scattering[n], done, "gather_forward_small"), "gather_wait_small")
            res = _small_sum_adam(parts.reshape(N_DEV, SMALL_ROWS, parts.shape[1]), _pack_small(sp),
                                  _pack_small({n: args["m_" + n] for n in SMALL}),
                                  _pack_small({n: args["v_" + n] for n in SMALL}), "small_adam")
            sg, sd, sm, sv = (_unpack_small(r) for r in res)
            out_g.update(sg), out_d.update(sd), out_m.update(sm), out_v.update(sv)
            done = res[3]
            continue
        g = _sum_partials(_scatter_wait(scattering[n], done, f"scatter_wait_{n}"), f"sum_{n}")
        if n in TRANSPOSED:
            g = g.T
        d_w, m_w, v_w = _adam(big_w[n], g, args["m_" + n][0], args["v_" + n][0], f"adam_{n}")
        out_g[n], out_d[n], out_m[n], out_v[n] = g[None], d_w[None], m_w[None], v_w[None]
        done = v_w
    order = ("ffn1_w_in", "ffn1_w_out", "ln1_g", "ln1_b", "mix_w_in", "hgrn_lb_fwd", "hgrn_lb_bwd", "hgrn_norm_g",
             "w_branch_a", "w_branch_b", "mix_w_out", "ln2_g", "ln2_b", "ffn2_w_in", "ffn2_w_out", "ln3_g", "ln3_b")
    return (loss, grad_x, *[out_g[n] for n in order], *[out_d[n] for n in order],
            *[out_m[n] for n in order], *[out_v[n] for n in order])
```

```python
import jax
import jax.numpy as jnp
from jax import lax
from jax.experimental import pallas as pl
from jax.experimental.pallas import tpu as pltpu

F32 = jnp.float32
BF16 = jnp.bfloat16

N_DEV = 8
HEAD = 128
ATTN_GROUPS = ((128, 1), (512, 4), (2048, 16))
ATTN_HEADS = 4
ATTN_HALF = 64
QKV_GROUP = 3 * ATTN_HEADS * HEAD
QKV_WIDTH = len(ATTN_GROUPS) * QKV_GROUP
ATTN_OUT = ATTN_HEADS * HEAD
ROPE_THETA = 500000.0
ROPE_DIM = HEAD // 4
ALPHA = 2.0 ** 0.25
LN_EPS = 1e-5
NEG_INF = -1e30
ADAM_LR, ADAM_B1, ADAM_B2, ADAM_EPS, ADAM_WD, ADAM_STEP = 0.001, 0.9, 0.999, 1e-08, 0.01, 10
VMEM_LIMIT = 56 * 1024 * 1024

NT = (((1,), (1,)), ((), ()))
NN = (((1,), (0,)), ((), ()))
TN = (((0,), (0,)), ((), ()))
MESH = pl.DeviceIdType.MESH


def _dot(a, b, dims):
    return lax.dot_general(a, b, dims, preferred_element_type=F32)


def _tile(n, pref, mult=128):
    if n <= pref:
        return n
    t = (pref // mult) * mult
    while t >= mult:
        if n % t == 0:
            return t
        t -= mult
    return n


def _tile_multi(ns, pref, mult=128):
    t = (pref // mult) * mult
    while t >= mult:
        if all(n % t == 0 for n in ns):
            return t
        t -= mult
    raise ValueError(f"no common tile for {ns}")


def _params(**kw):
    return pltpu.CompilerParams(vmem_limit_bytes=VMEM_LIMIT, **kw)


def _after(body, n_in, dep):
    if dep is None:
        return body, [], []

    def wrapped(*refs):
        body(*refs[:n_in], *refs[n_in + 1:])

    return wrapped, [pl.BlockSpec(dep.shape, lambda *_: (0,) * dep.ndim)], [dep]


def _pieces(a):
    pieces = tuple(a) if isinstance(a, (tuple, list)) else (a,)
    assert all(p.shape == pieces[0].shape for p in pieces)
    return pieces, pieces[0].shape[0], pieces[0].shape[1], len(pieces)


def _for_piece(step, p, per, npc, fn):
    if npc == 1:
        fn()
    else:
        pl.when((step >= p * per) & (step < (p + 1) * per))(fn)


def _sigmoid(x):
    return jax.nn.sigmoid(x)


def _dsilu(x, s):
    return s * (1.0 + x * (1.0 - s))


def _ln_stats(r):
    mu = jnp.mean(r, axis=-1, keepdims=True)
    xc = r - mu
    var = jnp.mean(xc * xc, axis=-1, keepdims=True)
    rstd = lax.rsqrt(var + LN_EPS)
    return xc * rstd, rstd


def _ln_bwd(dy, xhat, rstd, g):
    dyg = dy * g
    m1 = jnp.mean(dyg, axis=-1, keepdims=True)
    m2 = jnp.mean(dyg * xhat, axis=-1, keepdims=True)
    return rstd * (dyg - m1 - xhat * m2)


ROW_TILE = 1024
SUB_ROWS = 256


def _once(shape, index_map):
    return pl.BlockSpec(shape, index_map, pipeline_mode=pl.Buffered(1))


def _for_row_blocks(tm, fn):
    sub = SUB_ROWS if tm % SUB_ROWS == 0 else tm

    def step(s, carry):
        fn(pl.ds(pl.multiple_of(s * sub, sub), sub))
        return carry

    lax.fori_loop(0, tm // sub, step, 0)


def _row_runs(tm):
    sub = SUB_ROWS if tm % SUB_ROWS == 0 else tm
    return [slice(s, s + sub) for s in range(0, tm, sub)]


def _ffn_in_fwd(xb, w_t, name):
    t_, d_ = xb.shape
    f_ = w_t.shape[0] // 2
    tm, tn = _tile(t_, ROW_TILE, 8), _tile(f_, 512)
    nj = f_ // tn

    def body(x_ref, wg_ref, wu_ref, g_ref, u_ref, a_ref):
        wg, wu = wg_ref[...], wu_ref[...]
        for rows in _row_runs(tm):
            x = x_ref[rows, :]
            g = _dot(x, wg, NT)
            u = _dot(x, wu, NT)
            g_ref[rows, :] = g.astype(BF16)
            u_ref[rows, :] = u.astype(BF16)
            a_ref[rows, :] = (g * _sigmoid(g) * u).astype(BF16)

    return pl.pallas_call(
        body, name=name, grid=(t_ // tm, nj),
        in_specs=[pl.BlockSpec((tm, d_), lambda i, j: (i, 0)),
                  pl.BlockSpec((tn, d_), lambda i, j: (j, 0)),
                  pl.BlockSpec((tn, d_), lambda i, j: (j + nj, 0))],
        out_specs=[pl.BlockSpec((tm, tn), lambda i, j: (i, j))] * 3,
        out_shape=[jax.ShapeDtypeStruct((t_, f_), BF16)] * 3,
        compiler_params=_params(),
    )(xb, w_t, w_t)


WHOLE_WEIGHT_BYTES = 24 * 1024 * 1024


def _k_plan(t_, k_, d_):
    if k_ * d_ * 2 > WHOLE_WEIGHT_BYTES:
        return False, _tile(t_, ROW_TILE, 8), _tile(k_, 512), _once
    if k_ <= 2048:
        return True, _tile(t_, ROW_TILE, 8), k_, _once
    return True, _tile(t_, SUB_ROWS, 8), k_, pl.BlockSpec


def _mm_res_ln_fwd(a, w, res, g, b, scale, name):
    t_, k_ = a.shape
    d_ = w.shape[1]
    whole, tm, tk, row_spec = _k_plan(t_, k_, d_)
    nk = k_ // tk

    def body(a_ref, w_ref, res_ref, g_ref, b_ref, r_ref, h_ref, hb_ref, *scratch):
        k = pl.program_id(1)
        if not whole:
            acc, = scratch

            @pl.when(k == 0)
            def _():
                acc[...] = jnp.zeros_like(acc)

            acc[...] += _dot(a_ref[...], w_ref[...], NN)

        @pl.when(k == nk - 1)
        def _():
            def rows_out(rows):
                prod = _dot(a_ref[rows, :], w_ref[...], NN) if whole else acc[rows, :]
                r = ALPHA * res_ref[rows, :] + scale * prod
                xhat, _ = _ln_stats(r)
                h = xhat * g_ref[...] + b_ref[...]
                r_ref[rows, :] = r
                h_ref[rows, :] = h
                hb_ref[rows, :] = h.astype(BF16)

            _for_row_blocks(tm, rows_out)

    row = row_spec((tm, d_), lambda i, k: (i, 0))
    vec = pl.BlockSpec((1, d_), lambda i, k: (0, 0))
    w_spec = _once((tk, d_), lambda i, k: (0, 0)) if whole else pl.BlockSpec((tk, d_), lambda i, k: (k, 0))
    return pl.pallas_call(
        body, name=name, grid=(t_ // tm, nk),
        in_specs=[pl.BlockSpec((tm, tk), lambda i, k: (i, k)), w_spec, row, vec, vec],
        out_specs=[row, row, row],
        out_shape=[jax.ShapeDtypeStruct((t_, d_), F32), jax.ShapeDtypeStruct((t_, d_), F32),
                   jax.ShapeDtypeStruct((t_, d_), BF16)],
        scratch_shapes=[] if whole else [pltpu.VMEM((tm, d_), F32)],
        compiler_params=_params(),
    )(a, w, res, g, b)


def _mm_res_loss_bwd(a, w, res, target, g, b, scale, name):
    t_, k_ = a.shape
    d_ = w.shape[1]
    whole, tm, tk, row_spec = _k_plan(t_, k_, d_)
    nk = k_ // tk

    def body(a_ref, w_ref, res_ref, t_ref, g_ref, b_ref, dr_ref, drb_ref, dg_ref, db_ref, loss_ref, *scratch):
        i, k = pl.program_id(0), pl.program_id(1)

        @pl.when((i == 0) & (k == 0))
        def _():
            dg_ref[...] = jnp.zeros_like(dg_ref)
            db_ref[...] = jnp.zeros_like(db_ref)
            loss_ref[...] = jnp.zeros_like(loss_ref)

        if not whole:
            acc, = scratch

            @pl.when(k == 0)
            def _():
                acc[...] = jnp.zeros_like(acc)

            acc[...] += _dot(a_ref[...], w_ref[...], NN)

        @pl.when(k == nk - 1)
        def _():
            def rows_out(rows):
                prod = _dot(a_ref[rows, :], w_ref[...], NN) if whole else acc[rows, :]
                r = ALPHA * res_ref[rows, :] + scale * prod
                xhat, rstd = _ln_stats(r)
                gain = g_ref[...]
                err = xhat * gain + b_ref[...] - t_ref[rows, :]
                loss_ref[...] += (0.5 / d_) * jnp.sum(err * err)
                dy = err * (1.0 / d_)
                dr = _ln_bwd(dy, xhat, rstd, gain)
                dr_ref[rows, :] = dr
                drb_ref[rows, :] = dr.astype(BF16)
                dg_ref[...] += jnp.sum(dy * xhat, axis=0, keepdims=True)
                db_ref[...] += jnp.sum(dy, axis=0, keepdims=True)

            _for_row_blocks(tm, rows_out)

    row = row_spec((tm, d_), lambda i, k: (i, 0))
    vec = pl.BlockSpec((1, d_), lambda i, k: (0, 0))
    w_spec = _once((tk, d_), lambda i, k: (0, 0)) if whole else pl.BlockSpec((tk, d_), lambda i, k: (k, 0))
    return pl.pallas_call(
        body, name=name, grid=(t_ // tm, nk),
        in_specs=[pl.BlockSpec((tm, tk), lambda i, k: (i, k)), w_spec, row, row, vec, vec],
        out_specs=[row, row, vec, vec, pl.BlockSpec((1, HEAD), lambda i, k: (0, 0))],
        out_shape=[jax.ShapeDtypeStruct((t_, d_), F32), jax.ShapeDtypeStruct((t_, d_), BF16),
                   jax.ShapeDtypeStruct((1, d_), F32), jax.ShapeDtypeStruct((1, d_), F32),
                   jax.ShapeDtypeStruct((1, HEAD), F32)],
        scratch_shapes=[] if whole else [pltpu.VMEM((tm, d_), F32)],
        compiler_params=_params(),
    )(a, w, res, target, g, b)


def _mm_nt(a, w_t, out_dtype, name, dep=None):
    t_, k_ = a.shape
    n_ = w_t.shape[0]
    tm, tn = _tile(t_, ROW_TILE, 8), _tile(n_, 512)

    def body(a_ref, w_ref, o_ref):
        o_ref[...] = _dot(a_ref[...], w_ref[...], NT).astype(out_dtype)

    body, dep_specs, deps = _after(body, 2, dep)
    return pl.pallas_call(
        body, name=name, grid=(t_ // tm, n_ // tn),
        in_specs=[pl.BlockSpec((tm, k_), lambda i, j: (i, 0)),
                  pl.BlockSpec((tn, k_), lambda i, j: (j, 0)), *dep_specs],
        out_specs=pl.BlockSpec((tm, tn), lambda i, j: (i, j)),
        out_shape=jax.ShapeDtypeStruct((t_, n_), out_dtype),
        compiler_params=_params(),
    )(a, w_t, *deps)


def _mm_nn(a, w, out_dtype, name, dep=None):
    t_, k_ = a.shape
    n_ = w.shape[1]
    tm, tn = _tile(t_, ROW_TILE, 8), _tile(n_, 512)

    def body(a_ref, w_ref, o_ref):
        o_ref[...] = _dot(a_ref[...], w_ref[...], NN).astype(out_dtype)

    body, dep_specs, deps = _after(body, 2, dep)
    return pl.pallas_call(
        body, name=name, grid=(t_ // tm, n_ // tn),
        in_specs=[pl.BlockSpec((tm, k_), lambda i, j: (i, 0)),
                  pl.BlockSpec((k_, tn), lambda i, j: (0, j)), *dep_specs],
        out_specs=pl.BlockSpec((tm, tn), lambda i, j: (i, j)),
        out_shape=jax.ShapeDtypeStruct((t_, n_), out_dtype),
        compiler_params=_params(),
    )(a, w, *deps)


def _mm_tn(a, b, scale, name):
    pieces, t_, mp, npc = _pieces(a)
    n_ = b.shape[1]
    tm = _tile(mp, 512)
    per = mp // tm

    def body(*refs):
        b_ref, o_ref = refs[npc], refs[npc + 1]
        for p in range(npc):
            def piece_out(p=p):
                o_ref[...] = (scale * _dot(refs[p][...], b_ref[...], TN)).astype(BF16)

            _for_piece(pl.program_id(0), p, per, npc, piece_out)

    return pl.pallas_call(
        body, name=name, grid=(npc * per,),
        in_specs=[pl.BlockSpec((t_, tm), lambda i, p=p: (0, jnp.clip(i - p * per, 0, per - 1))) for p in range(npc)]
        + [_once((t_, n_), lambda i: (0, 0))],
        out_specs=pl.BlockSpec((tm, n_), lambda i: (i, 0)),
        out_shape=jax.ShapeDtypeStruct((npc * mp, n_), BF16),
        compiler_params=_params(),
    )(*pieces, b)


def _gate_out_fwd(ya_in, ob, wa, wb_t, proj, d_, name):
    t_ = ya_in.shape[0]
    goff = 5 * d_ + QKV_WIDTH
    tm, tn = _tile(t_, ROW_TILE, 8), _tile_multi([d_, goff], 512)
    ja, jb = goff // tn, (goff + d_) // tn

    def body(ya_ref, ob_ref, wa_ref, wb_ref, ga_ref, gb_ref, yao_ref, ybo_ref, z_ref):
        wa, wb = wa_ref[...], wb_ref[...]
        for rows in _row_runs(tm):
            y_a = _dot(ya_ref[rows, :], wa, NN)
            y_b = _dot(ob_ref[rows, :], wb, NT)
            yao_ref[rows, :] = y_a.astype(BF16)
            ybo_ref[rows, :] = y_b.astype(BF16)
            z_ref[rows, :] = (_sigmoid(ga_ref[rows, :]) * y_a + _sigmoid(gb_ref[rows, :]) * y_b).astype(BF16)

    tile = pl.BlockSpec((tm, tn), lambda i, j: (i, j))
    return pl.pallas_call(
        body, name=name, grid=(t_ // tm, d_ // tn),
        in_specs=[pl.BlockSpec((tm, d_), lambda i, j: (i, 0)),
                  pl.BlockSpec((tm, ATTN_OUT), lambda i, j: (i, 0)),
                  pl.BlockSpec((d_, tn), lambda i, j: (0, j)),
                  pl.BlockSpec((tn, ATTN_OUT), lambda i, j: (j, 0)),
                  pl.BlockSpec((tm, tn), lambda i, j: (i, ja + j)),
                  pl.BlockSpec((tm, tn), lambda i, j: (i, jb + j))],
        out_specs=[tile, tile, tile],
        out_shape=[jax.ShapeDtypeStruct((t_, d_), BF16)] * 3,
        compiler_params=_params(),
    )(ya_in, ob, wa, wb_t, proj, proj)


def _ffn_mid_bwd(drb, w_out, gate, up, scale, name, dep=None):
    t_, d_ = drb.shape
    f_ = w_out.shape[0]
    tm, tn = _tile(t_, ROW_TILE, 8), _tile(f_, 512)

    def body(dr_ref, w_ref, g_ref, u_ref, dg_ref, du_ref):
        w = w_ref[...]
        for rows in _row_runs(tm):
            da = scale * _dot(dr_ref[rows, :], w, NT)
            g = g_ref[rows, :].astype(F32)
            s = _sigmoid(g)
            dg_ref[rows, :] = (da * u_ref[rows, :].astype(F32) * _dsilu(g, s)).astype(BF16)
            du_ref[rows, :] = (da * g * s).astype(BF16)

    body, dep_specs, deps = _after(body, 4, dep)
    tile = pl.BlockSpec((tm, tn), lambda i, j: (i, j))
    return pl.pallas_call(
        body, name=name, grid=(t_ // tm, f_ // tn),
        in_specs=[pl.BlockSpec((tm, d_), lambda i, j: (i, 0)),
                  pl.BlockSpec((tn, d_), lambda i, j: (j, 0)), tile, tile, *dep_specs],
        out_specs=[tile, tile],
        out_shape=[jax.ShapeDtypeStruct((t_, f_), BF16)] * 2,
        compiler_params=_params(),
    )(drb, w_out, gate, up, *deps)


def _mm_nn_res_lnbwd(a, w, dres, r, g, name, dep=None):
    pieces, t_, kp, npc = _pieces(a)
    d_ = w.shape[1]
    tm, tk = _tile(t_, ROW_TILE, 8), _tile(kp, 512)
    per = kp // tk
    nk = npc * per

    def body(*refs):
        w_ref, dres_ref, r_ref, g_ref, dr_ref, drb_ref, dg_ref, db_ref, acc = refs[npc:]
        i, k = pl.program_id(0), pl.program_id(1)

        @pl.when(k == 0)
        def _():
            acc[...] = jnp.zeros_like(acc)

        @pl.when((i == 0) & (k == 0))
        def _():
            dg_ref[...] = jnp.zeros_like(dg_ref)
            db_ref[...] = jnp.zeros_like(db_ref)

        for p in range(npc):
            def piece_in(p=p):
                acc[...] += _dot(refs[p][...], w_ref[...], NN)

            _for_piece(k, p, per, npc, piece_in)

        @pl.when(k == nk - 1)
        def _():
            def rows_out(rows):
                dy = acc[rows, :] + ALPHA * dres_ref[rows, :]
                xhat, rstd = _ln_stats(r_ref[rows, :])
                dr = _ln_bwd(dy, xhat, rstd, g_ref[...])
                dr_ref[rows, :] = dr
                drb_ref[rows, :] = dr.astype(BF16)
                dg_ref[...] += jnp.sum(dy * xhat, axis=0, keepdims=True)
                db_ref[...] += jnp.sum(dy, axis=0, keepdims=True)

            _for_row_blocks(tm, rows_out)

    body, dep_specs, deps = _after(body, npc + 4, dep)
    row = _once((tm, d_), lambda i, k: (i, 0))
    vec = pl.BlockSpec((1, d_), lambda i, k: (0, 0))
    return pl.pallas_call(
        body, name=name, grid=(t_ // tm, nk),
        in_specs=[pl.BlockSpec((tm, tk), lambda i, k, p=p: (i, jnp.clip(k - p * per, 0, per - 1))) for p in range(npc)]
        + [pl.BlockSpec((tk, d_), lambda i, k: (k, 0)), row, row, vec, *dep_specs],
        out_specs=[row, row, vec, vec],
        out_shape=[jax.ShapeDtypeStruct((t_, d_), F32), jax.ShapeDtypeStruct((t_, d_), BF16),
                   jax.ShapeDtypeStruct((1, d_), F32), jax.ShapeDtypeStruct((1, d_), F32)],
        scratch_shapes=[pltpu.VMEM((tm, d_), F32)],
        compiler_params=_params(),
    )(*pieces, w, dres, r, g, *deps)


def _mm_nn_res(a, w, dres, name, dep=None):
    pieces, t_, kp, npc = _pieces(a)
    d_ = w.shape[1]
    tm, tk = _tile(t_, ROW_TILE, 8), _tile(kp, 512)
    per = kp // tk
    nk = npc * per

    def body(*refs):
        w_ref, dres_ref, o_ref, acc = refs[npc:]
        k = pl.program_id(1)

        @pl.when(k == 0)
        def _():
            acc[...] = jnp.zeros_like(acc)

        for p in range(npc):
            def piece_in(p=p):
                acc[...] += _dot(refs[p][...], w_ref[...], NN)

            _for_piece(k, p, per, npc, piece_in)

        @pl.when(k == nk - 1)
        def _():
            def rows_out(rows):
                o_ref[rows, :] = acc[rows, :] + ALPHA * dres_ref[rows, :]

            _for_row_blocks(tm, rows_out)

    body, dep_specs, deps = _after(body, npc + 2, dep)
    row = _once((tm, d_), lambda i, k: (i, 0))
    return pl.pallas_call(
        body, name=name, grid=(t_ // tm, nk),
        in_specs=[pl.BlockSpec((tm, tk), lambda i, k, p=p: (i, jnp.clip(k - p * per, 0, per - 1))) for p in range(npc)]
        + [pl.BlockSpec((tk, d_), lambda i, k: (k, 0)), row, *dep_specs],
        out_specs=row,
        out_shape=jax.ShapeDtypeStruct((t_, d_), F32),
        scratch_shapes=[pltpu.VMEM((tm, d_), F32)],
        compiler_params=_params(),
    )(*pieces, w, dres, *deps)


def _dz_gate_bwd(drb, w_out, proj, ya, yb, d_, name, dep=None):
    t_ = drb.shape[0]
    goff = 5 * d_ + QKV_WIDTH
    tm, tn = _tile(t_, ROW_TILE, 8), _tile_multi([d_, goff], 512)
    ja, jb = goff // tn, (goff + d_) // tn

    def body(dr_ref, w_ref, ga_ref, gb_ref, ya_ref, yb_ref, dya_ref, dyb_ref, dga_ref, dgb_ref):
        w = w_ref[...]
        for rows in _row_runs(tm):
            dz = _dot(dr_ref[rows, :], w, NT)
            sa, sb = _sigmoid(ga_ref[rows, :]), _sigmoid(gb_ref[rows, :])
            dya_ref[rows, :] = (dz * sa).astype(BF16)
            dyb_ref[rows, :] = (dz * sb).astype(BF16)
            dga_ref[rows, :] = (dz * ya_ref[rows, :].astype(F32) * sa * (1.0 - sa)).astype(BF16)
            dgb_ref[rows, :] = (dz * yb_ref[rows, :].astype(F32) * sb * (1.0 - sb)).astype(BF16)

    body, dep_specs, deps = _after(body, 6, dep)
    tile = pl.BlockSpec((tm, tn), lambda i, j: (i, j))
    return pl.pallas_call(
        body, name=name, grid=(t_ // tm, d_ // tn),
        in_specs=[pl.BlockSpec((tm, d_), lambda i, j: (i, 0)),
                  pl.BlockSpec((tn, d_), lambda i, j: (j, 0)),
                  pl.BlockSpec((tm, tn), lambda i, j: (i, ja + j)),
                  pl.BlockSpec((tm, tn), lambda i, j: (i, jb + j)), tile, tile, *dep_specs],
        out_specs=[tile] * 4,
        out_shape=[jax.ShapeDtypeStruct((t_, d_), BF16)] * 4,
        compiler_params=_params(),
    )(drb, w_out, proj, proj, ya, yb, *deps)


def _lower_bound(tab):
    return _sigmoid(tab[0:1, :] - tab[1:2, :])


FWD_BLOCK = 128
BWD_BLOCK = 128


def _block_scan(x, row, reverse, size, blk):
    s = 1
    while s < blk:
        if reverse:
            x = x + jnp.where(row < blk - s, pltpu.roll(x, size - s, 0), 0.0)
        else:
            x = x + jnp.where(row >= s, pltpu.roll(x, s, 0), 0.0)
        s *= 2
    return x


def _block_exps(l, reverse):
    blk = l.shape[0]
    half = blk // 2
    first = lax.broadcasted_iota(jnp.int32, (blk, HEAD), 0) < half
    q1, q3 = half // 2, half + half // 2
    if reverse:
        rho1, rho2, lh, ltot = l[q1:q1 + 1], l[q3:q3 + 1], l[half:half + 1], l[0:1]
    else:
        rho1, rho2, lh, ltot = l[q1 - 1:q1], l[q3 - 1:q3], l[half - 1:half], l[blk - 1:blk]
    ref = jnp.where(first, rho1, rho2)
    query_half = first if reverse else jnp.logical_not(first)
    e2 = jnp.where(query_half, jnp.exp(jnp.minimum(l - lh, 0.0)), 0.0)
    e1 = jnp.where(query_half, 0.0, jnp.exp(jnp.minimum(lh - l, 0.0)))
    return (jnp.exp(l - ref), jnp.exp(ref - l), e2, e1, jnp.exp(l), jnp.exp(ltot - l),
            jnp.exp(ltot), jnp.exp(lh), jnp.exp(ltot - lh))


def _half_mask(reverse, blk):
    r = lax.broadcasted_iota(jnp.int32, (blk, blk), 0)
    c = lax.broadcasted_iota(jnp.int32, (blk, blk), 1)
    same = (r < blk // 2) == (c < blk // 2)
    return same & ((c >= r) if reverse else (r >= c))


def _hgrn_fwd(proj, lbf, lbb, ng, b_, s_, d_, name):
    h_ = d_ // HEAD
    BLOCK = min(FWD_BLOCK, s_)
    nb = s_ // BLOCK

    def body(hq_ref, hff_ref, hfb_ref, hi_ref, hog_ref, lbf_ref, lbb_ref, ng_ref, ya_ref, o_ref,
             q_s, k_s, l_s, of_s, oi_s, qd_s, u_s, st_s, dec_s):
        row = lax.broadcasted_iota(jnp.int32, (s_, HEAD), 0) % BLOCK
        hq = hq_ref[...]
        q_s[...] = hq * _sigmoid(hq)
        for reverse, hf_ref, lb_ref in ((False, hff_ref, lbf_ref), (True, hfb_ref, lbb_ref)):
            lb = _lower_bound(lb_ref[...])
            f = lb + (1.0 - lb) * _sigmoid(hf_ref[...])
            k_s[...] = 1.0 - f
            l_s[...] = _block_scan(jnp.log(f), row, reverse, s_, BLOCK)
            mask = _half_mask(reverse, BLOCK)

            def inside(n, carry, reverse=reverse, mask=mask):
                sl = pl.ds(pl.multiple_of(n * BLOCK, BLOCK), BLOCK)
                eq, ek, e2, e1, el, ee, dec, _, _ = _block_exps(l_s[sl, :], reverse)
                qc, kc = q_s[sl, :], k_s[sl, :]
                vb = hi_ref[sl, :].astype(BF16)
                a = jnp.where(mask, _dot((qc * eq).astype(BF16), (kc * ek).astype(BF16), NT), 0.0)
                a = a + _dot((qc * e2).astype(BF16), (kc * e1).astype(BF16), NT)
                oi_s[sl, :] = _dot(a.astype(BF16), vb, NN)
                qd_s[sl, :] = (qc * el).astype(BF16)
                u_s[n] = _dot(vb, (kc * ee).astype(BF16), TN)
                dec_s[n] = jnp.broadcast_to(dec, (8, HEAD))
                return carry

            lax.fori_loop(0, nb, inside, 0, unroll=8)

            def carry_state(n, st, reverse=reverse):
                idx = (nb - 1 - n) if reverse else n
                st_s[idx] = st.astype(BF16)
                return st * dec_s[idx][0:1, :] + u_s[idx]

            lax.fori_loop(0, nb, carry_state, jnp.zeros((HEAD, HEAD), F32))

            def across(n, carry, reverse=reverse):
                sl = pl.ds(pl.multiple_of(n * BLOCK, BLOCK), BLOCK)
                o_dir = oi_s[sl, :] + _dot(qd_s[sl, :], st_s[n], NT)
                if not reverse:
                    of_s[sl, :] = o_dir
                else:
                    o = of_s[sl, :] + o_dir
                    o_ref[sl, :] = o
                    nrm = o * lax.rsqrt(jnp.mean(o * o, axis=-1, keepdims=True) + LN_EPS)
                    hog = hog_ref[sl, :]
                    ya_ref[sl, :] = (nrm * ng_ref[...] * hog * _sigmoid(hog)).astype(BF16)
                return carry

            lax.fori_loop(0, nb, across, 0, unroll=8)

    def col(part):
        return pl.BlockSpec((s_, HEAD), lambda h, b, part=part: (b, part * h_ + h))

    tab = pl.BlockSpec((2, HEAD), lambda h, b: (0, h))
    out = pl.BlockSpec((s_, HEAD), lambda h, b: (b, h))
    return pl.pallas_call(
        body, name=name, grid=(h_, b_),
        in_specs=[col(0), col(1), col(2), col(3), col(4), tab, tab,
                  pl.BlockSpec((1, HEAD), lambda h, b: (0, h))],
        out_specs=[out, out],
        out_shape=[jax.ShapeDtypeStruct((b_ * s_, d_), BF16), jax.ShapeDtypeStruct((b_ * s_, d_), F32)],
        scratch_shapes=[pltpu.VMEM((s_, HEAD), F32)] * 5 + [
            pltpu.VMEM((s_, HEAD), BF16), pltpu.VMEM((nb, HEAD, HEAD), F32), pltpu.VMEM((nb, HEAD, HEAD), BF16),
            pltpu.VMEM((nb, 8, HEAD), F32)],
        compiler_params=_params(),
    )(proj, proj, proj, proj, proj, lbf, lbb, ng)


def _hgrn_bwd(proj, lbf, lbb, ng, o_sum, dya, b_, s_, d_, name):
    h_ = d_ // HEAD
    BLOCK = min(BWD_BLOCK, s_)
    nb = s_ // BLOCK

    def body(hq_ref, hff_ref, hfb_ref, hi_ref, hog_ref, lbf_ref, lbb_ref, ng_ref, o_ref, dya_ref,
             dhq_ref, dhff_ref, dhfb_ref, dhi_ref, dhog_ref, dng_ref, dlbf_ref, dlbb_ref,
             q_s, k_s, l_s, do_s, dq_s, dv_s, dl_s, dk_s, u_s, w_s, st_s, dst_s, dec_s):
        b = pl.program_id(1)

        @pl.when(b == 0)
        def _():
            dng_ref[...] = jnp.zeros_like(dng_ref)
            dlbf_ref[...] = jnp.zeros_like(dlbf_ref)
            dlbb_ref[...] = jnp.zeros_like(dlbb_ref)

        row = lax.broadcasted_iota(jnp.int32, (s_, HEAD), 0) % BLOCK
        brow = lax.broadcasted_iota(jnp.int32, (BLOCK, HEAD), 0)
        hq = hq_ref[...]
        q_s[...] = hq * _sigmoid(hq)
        o = o_ref[...]
        rinv = lax.rsqrt(jnp.mean(o * o, axis=-1, keepdims=True) + LN_EPS)
        nrm = o * rinv
        hog = hog_ref[...]
        so = _sigmoid(hog)
        gain = ng_ref[...]
        dy = dya_ref[...]
        dhog_ref[...] = (dy * nrm * gain * _dsilu(hog, so)).astype(BF16)
        dng_ref[...] += jnp.sum(dy * nrm * hog * so, axis=0, keepdims=True)
        dn = dy * gain * hog * so
        do_s[...] = rinv * (dn - nrm * jnp.mean(dn * nrm, axis=-1, keepdims=True))

        for reverse, hf_ref, lb_ref, dhf_ref, dlb_ref in (
                (False, hff_ref, lbf_ref, dhff_ref, dlbf_ref), (True, hfb_ref, lbb_ref, dhfb_ref, dlbb_ref)):
            lb = _lower_bound(lb_ref[...])
            sf = _sigmoid(hf_ref[...])
            f = lb + (1.0 - lb) * sf
            k_s[...] = 1.0 - f
            l_s[...] = _block_scan(jnp.log(f), row, reverse, s_, BLOCK)
            mask = _half_mask(reverse, BLOCK)
            total_row = 0 if reverse else BLOCK - 1
            key_end = BLOCK // 2 if reverse else BLOCK // 2 - 1

            def prepare(n, carry, reverse=reverse):
                sl = pl.ds(pl.multiple_of(n * BLOCK, BLOCK), BLOCK)
                _, _, _, _, el, ee, dec, _, _ = _block_exps(l_s[sl, :], reverse)
                vb = hi_ref[sl, :].astype(BF16)
                u_s[n] = _dot(vb, (k_s[sl, :] * ee).astype(BF16), TN)
                w_s[n] = _dot(do_s[sl, :].astype(BF16), (q_s[sl, :] * el).astype(BF16), TN)
                dec_s[n] = jnp.broadcast_to(dec, (8, HEAD))
                return carry

            lax.fori_loop(0, nb, prepare, 0, unroll=8)

            def carry_state(n, st, reverse=reverse):
                idx = (nb - 1 - n) if reverse else n
                st_s[idx] = st
                return st * dec_s[idx][0:1, :] + u_s[idx]

            lax.fori_loop(0, nb, carry_state, jnp.zeros((HEAD, HEAD), F32))

            def carry_grad(n, dst, reverse=reverse):
                idx = n if reverse else (nb - 1 - n)
                dst_s[idx] = dst
                return dst * dec_s[idx][0:1, :] + w_s[idx]

            lax.fori_loop(0, nb, carry_grad, jnp.zeros((HEAD, HEAD), F32))

            def inside(n, carry, reverse=reverse, mask=mask, total_row=total_row, key_end=key_end):
                sl = pl.ds(pl.multiple_of(n * BLOCK, BLOCK), BLOCK)
                eq, ek, e2, e1, el, ee, dec, dec_key, dec_query = _block_exps(l_s[sl, :], reverse)
                qc, kc = q_s[sl, :], k_s[sl, :]
                vb = hi_ref[sl, :].astype(BF16)
                dob = do_s[sl, :].astype(BF16)
                qt, kt, q2, k1 = ((qc * eq).astype(BF16), (kc * ek).astype(BF16),
                                  (qc * e2).astype(BF16), (kc * e1).astype(BF16))
                kend = kc * ee
                st0, dst1 = st_s[n], dst_s[n]
                dstb = dst1.astype(BF16)
                a = jnp.where(mask, _dot(qt, kt, NT), 0.0) + _dot(q2, k1, NT)
                da = _dot(dob, vb, NT)
                dab = da.astype(BF16)
                dad = jnp.where(mask, da, 0.0).astype(BF16)
                dqt, dkt = _dot(dad, kt, NN), _dot(dad, qt, TN)
                dq2, dk1 = _dot(dab, k1, NN), _dot(dab, q2, TN)
                dqd = _dot(dob, st0.astype(BF16), NN)
                dke = _dot(vb, dstb, NN)
                dv = _dot(a.astype(BF16), dob, TN) + _dot(kend.astype(BF16), dstb, NT)
                dq = dqt * eq + dq2 * e2 + dqd * el
                dk = dkt * ek + dk1 * e1 + dke * ee
                dtot = jnp.sum(dke * kend, axis=0, keepdims=True) + jnp.sum(dst1 * st0, axis=0, keepdims=True) * dec
                st_mid = st0 * dec_key + _dot(vb, k1, TN)
                dst_mid = dst1 * dec_query + _dot(dob, q2, TN)
                dmid = jnp.sum(dst_mid * st_mid, axis=0, keepdims=True)
                dl_s[sl, :] = (qc * dq - kc * dk + jnp.where(brow == total_row, dtot, 0.0)
                               + jnp.where(brow == key_end, dmid, 0.0))
                dk_s[sl, :] = dk
                if not reverse:
                    dq_s[sl, :] = dq
                    dv_s[sl, :] = dv
                else:
                    hqc = hq_ref[sl, :]
                    dhq_ref[sl, :] = ((dq_s[sl, :] + dq) * _dsilu(hqc, _sigmoid(hqc))).astype(BF16)
                    dhi_ref[sl, :] = (dv_s[sl, :] + dv).astype(BF16)
                return carry

            lax.fori_loop(0, nb, inside, 0, unroll=4)
            dlogf = _block_scan(dl_s[...], row % (BLOCK // 2), not reverse, s_, BLOCK // 2)
            df = dlogf / f - dk_s[...]
            dhf_ref[...] = (df * (1.0 - lb) * sf * (1.0 - sf)).astype(BF16)
            dlb = jnp.sum(df * (1.0 - sf), axis=0, keepdims=True) * lb * (1.0 - lb)
            dlb_ref[0:1, :] += dlb
            dlb_ref[1:2, :] -= dlb

    def col(part):
        return pl.BlockSpec((s_, HEAD), lambda h, b, part=part: (b, part * h_ + h))

    tab = pl.BlockSpec((2, HEAD), lambda h, b: (0, h))
    vec = pl.BlockSpec((1, HEAD), lambda h, b: (0, h))
    blk = pl.BlockSpec((s_, HEAD), lambda h, b: (b, h))
    act = jax.ShapeDtypeStruct((b_ * s_, d_), BF16)
    state = pltpu.VMEM((nb, HEAD, HEAD), F32)
    return pl.pallas_call(
        body, name=name, grid=(h_, b_),
        in_specs=[col(0), col(1), col(2), col(3), col(4), tab, tab, vec, blk, blk],
        out_specs=[blk] * 5 + [vec, tab, tab],
        out_shape=[act] * 5 + [jax.ShapeDtypeStruct((1, d_), F32), jax.ShapeDtypeStruct((2, d_), F32),
                               jax.ShapeDtypeStruct((2, d_), F32)],
        scratch_shapes=[pltpu.VMEM((s_, HEAD), F32)] * 8 + [state] * 4 + [pltpu.VMEM((nb, 8, HEAD), F32)],
        compiler_params=_params(),
    )(proj, proj, proj, proj, proj, lbf, lbb, ng, o_sum, dya)


def _rope_tables(s_):
    half = ROPE_DIM // 2
    inv_freq = ROPE_THETA ** (-jnp.arange(0, ROPE_DIM, 2, dtype=F32) / ROPE_DIM)
    ang = jnp.arange(s_, dtype=F32)[:, None] * inv_freq
    cos, sin = jnp.cos(ang), jnp.sin(ang)
    zeros = jnp.zeros((s_, HEAD - ROPE_DIM), F32)
    zh = jnp.zeros((s_, half), F32)
    c = jnp.concatenate([cos, cos, jnp.ones((s_, HEAD - ROPE_DIM), F32)], axis=1)
    s1 = jnp.concatenate([-sin, zh, zeros], axis=1)
    s2 = jnp.concatenate([zh, sin, zeros], axis=1)
    return c, s1, s2


def _rope(t, c, s1, s2):
    half = ROPE_DIM // 2
    return t * c + pltpu.roll(t, HEAD - half, 1) * s1 + pltpu.roll(t, half, 1) * s2


def _rope_bwd(dt, c, s1, s2):
    half = ROPE_DIM // 2
    return dt * c + pltpu.roll(dt * s1, half, 1) + pltpu.roll(dt * s2, HEAD - half, 1)


def _window_mask(r0, qb, wk, seg):
    row = lax.broadcasted_iota(jnp.int32, (qb, wk), 0)
    col = lax.broadcasted_iota(jnp.int32, (qb, wk), 1)
    kj = r0 - ATTN_HALF + col
    return (col - row >= 0) & (col - row <= 2 * ATTN_HALF) & (kj >= 0) & (kj < seg)


def _attn_fwd(proj, tabs, b_, s_, col0, dil, name):
    seg = s_ // dil
    qb = min(128, seg)
    nq, wk = seg // qb, qb + 2 * ATTN_HALF
    scale = HEAD ** -0.5

    def body(q_ref, k_ref, v_ref, c_ref, s1_ref, s2_ref, o_ref, lse_ref, q_s, k_s, v_s):
        k_s[...] = jnp.zeros_like(k_s)
        v_s[...] = jnp.zeros_like(v_s)

        def residue(r, carry):
            cls = pl.ds(r, seg, stride=dil)
            c, s1, s2 = c_ref[cls, :], s1_ref[cls, :], s2_ref[cls, :]
            q_s[...] = _rope(q_ref[cls, :], c, s1, s2).astype(BF16)
            k_s[ATTN_HALF:ATTN_HALF + seg, :] = _rope(k_ref[cls, :], c, s1, s2).astype(BF16)
            v_s[ATTN_HALF:ATTN_HALF + seg, :] = v_ref[cls, :].astype(BF16)

            def step(i, carry):
                r0 = pl.multiple_of(i * qb, qb)
                sc = _dot(q_s[pl.ds(r0, qb), :], k_s[pl.ds(r0, wk), :], NT) * scale
                sc = jnp.where(_window_mask(r0, qb, wk, seg), sc, NEG_INF)
                m = jnp.max(sc, axis=-1, keepdims=True)
                p = jnp.exp(sc - m)
                den = jnp.sum(p, axis=-1, keepdims=True)
                rows = pl.ds(r + r0 * dil, qb, stride=dil)
                o_ref[rows, :] = _dot(p.astype(BF16), v_s[pl.ds(r0, wk), :], NN) / den
                lse_ref[rows, :] = jnp.broadcast_to(m + jnp.log(den), (qb, HEAD))
                return carry

            return lax.fori_loop(0, nq, step, carry)

        lax.fori_loop(0, dil, residue, 0)

    def col(part):
        return pl.BlockSpec((s_, HEAD), lambda b, h, part=part: (b, col0 + part * ATTN_HEADS + h))

    tab = pl.BlockSpec((s_, HEAD), lambda b, h: (0, 0))
    out = pl.BlockSpec((s_, HEAD), lambda b, h: (b, h))
    shape = jax.ShapeDtypeStruct((b_ * s_, ATTN_OUT), F32)
    return pl.pallas_call(
        body, name=name, grid=(b_, ATTN_HEADS),
        in_specs=[col(0), col(1), col(2), tab, tab, tab],
        out_specs=[out, out],
        out_shape=[shape, shape],
        scratch_shapes=[pltpu.VMEM((seg, HEAD), BF16), pltpu.VMEM((seg + 2 * ATTN_HALF, HEAD), BF16),
                        pltpu.VMEM((seg + 2 * ATTN_HALF, HEAD), BF16)],
        compiler_params=_params(),
    )(proj, proj, proj, *tabs)


def _attn_bwd(proj, tabs, dog, cg, lse, b_, s_, col0, dil, name):
    seg = s_ // dil
    qb = min(128, seg)
    nq, wk = seg // qb, qb + 2 * ATTN_HALF
    scale = HEAD ** -0.5

    def body(q_ref, k_ref, v_ref, c_ref, s1_ref, s2_ref, do_ref, cg_ref, lse_ref, dq_ref, dk_ref, dv_ref,
             q_s, k_s, v_s, do_s, cl_s, dk_s, dv_s):
        k_s[...] = jnp.zeros_like(k_s)
        v_s[...] = jnp.zeros_like(v_s)

        def residue(r, carry):
            cls = pl.ds(r, seg, stride=dil)
            c, s1, s2 = c_ref[cls, :], s1_ref[cls, :], s2_ref[cls, :]
            q_s[...] = _rope(q_ref[cls, :], c, s1, s2).astype(BF16)
            k_s[ATTN_HALF:ATTN_HALF + seg, :] = _rope(k_ref[cls, :], c, s1, s2).astype(BF16)
            v_s[ATTN_HALF:ATTN_HALF + seg, :] = v_ref[cls, :].astype(BF16)
            do_s[...] = do_ref[cls, :].astype(BF16)
            cl_s[0] = cg_ref[cls, :]
            cl_s[1] = lse_ref[cls, :]
            dk_s[...] = jnp.zeros_like(dk_s)
            dv_s[...] = jnp.zeros_like(dv_s)

            def step(i, carry):
                r0 = pl.multiple_of(i * qb, qb)
                rows, win = pl.ds(r0, qb), pl.ds(r0, wk)
                qc, kw, vw = q_s[rows, :], k_s[win, :], v_s[win, :]
                sc = _dot(qc, kw, NT) * scale
                p = jnp.where(_window_mask(r0, qb, wk, seg), jnp.exp(sc - cl_s[1, rows, 0:1]), 0.0)
                dob = do_s[rows, :]
                dp = _dot(dob, vw, NT)
                ds = (p * (dp + cl_s[0, rows, 0:1]) * scale).astype(BF16)
                out = pl.ds(r + r0 * dil, qb, stride=dil)
                dq_ref[out, :] = _rope_bwd(_dot(ds, kw, NN), c_ref[out, :], s1_ref[out, :], s2_ref[out, :])
                dk_s[win, :] += _dot(ds, qc, TN)
                dv_s[win, :] += _dot(p.astype(BF16), dob, TN)
                return carry

            carry = lax.fori_loop(0, nq, step, carry)
            dk_ref[cls, :] = _rope_bwd(dk_s[ATTN_HALF:ATTN_HALF + seg, :], c, s1, s2)
            dv_ref[cls, :] = dv_s[ATTN_HALF:ATTN_HALF + seg, :]
            return carry

        lax.fori_loop(0, dil, residue, 0)

    def col(part):
        return pl.BlockSpec((s_, HEAD), lambda b, h, part=part: (b, col0 + part * ATTN_HEADS + h))

    tab = pl.BlockSpec((s_, HEAD), lambda b, h: (0, 0))
    out = pl.BlockSpec((s_, HEAD), lambda b, h: (b, h))
    shape = jax.ShapeDtypeStruct((b_ * s_, ATTN_OUT), F32)
    pad = seg + 2 * ATTN_HALF
    return pl.pallas_call(
        body, name=name, grid=(b_, ATTN_HEADS),
        in_specs=[col(0), col(1), col(2), tab, tab, tab, out, out, out],
        out_specs=[out, out, out],
        out_shape=[shape, shape, shape],
        scratch_shapes=[pltpu.VMEM((seg, HEAD), BF16), pltpu.VMEM((pad, HEAD), BF16), pltpu.VMEM((pad, HEAD), BF16),
                        pltpu.VMEM((seg, HEAD), BF16), pltpu.VMEM((2, seg, HEAD), F32),
                        pltpu.VMEM((pad, HEAD), F32), pltpu.VMEM((pad, HEAD), F32)],
        compiler_params=_params(),
    )(proj, proj, proj, *tabs, dog, cg, lse)


def _group_weights(lses):
    m = jnp.maximum(jnp.maximum(lses[0], lses[1]), lses[2])
    es = [jnp.exp(l - m) for l in lses]
    den = es[0] + es[1] + es[2]
    return [e / den for e in es]


def _combine_fwd(outs, lses, name):
    t_, w_ = outs[0].shape
    tm = _tile(t_, 512, 8)
    ng = len(outs)

    def body(*refs):
        ws = _group_weights([r[...] for r in refs[ng:2 * ng]])
        acc = ws[0] * refs[0][...]
        for g in range(1, ng):
            acc = acc + ws[g] * refs[g][...]
        refs[2 * ng][...] = acc.astype(BF16)

    row = pl.BlockSpec((tm, w_), lambda i: (i, 0))
    return pl.pallas_call(
        body, name=name, grid=(t_ // tm,), in_specs=[row] * (2 * ng), out_specs=row,
        out_shape=jax.ShapeDtypeStruct((t_, w_), BF16), compiler_params=_params(),
    )(*outs, *lses)


def _combine_bwd(dob, outs, lses, name):
    t_, w_ = outs[0].shape
    tm = _tile(t_, 512, 8)
    ng = len(outs)

    def body(*refs):
        do = refs[0][...]
        os_ = [r[...] for r in refs[1:1 + ng]]
        ws = _group_weights([r[...] for r in refs[1 + ng:1 + 2 * ng]])
        o = ws[0] * os_[0]
        for g in range(1, ng):
            o = o + ws[g] * os_[g]
        prod = do * o
        heads = [jnp.broadcast_to(jnp.sum(prod[:, h * HEAD:(h + 1) * HEAD], axis=-1, keepdims=True), (tm, HEAD))
                 for h in range(w_ // HEAD)]
        tot = jnp.concatenate(heads, axis=1)
        for g in range(ng):
            refs[1 + 2 * ng + g][...] = ws[g] * do
            refs[1 + 3 * ng + g][...] = -ws[g] * tot

    row = pl.BlockSpec((tm, w_), lambda i: (i, 0))
    shape = jax.ShapeDtypeStruct((t_, w_), F32)
    res = pl.pallas_call(
        body, name=name, grid=(t_ // tm,), in_specs=[row] * (1 + 2 * ng), out_specs=[row] * (2 * ng),
        out_shape=[shape] * (2 * ng), compiler_params=_params(),
    )(dob, *outs, *lses)
    return res[:ng], res[ng:]


def _adam_update(w, g, m, v):
    m = ADAM_B1 * m + (1.0 - ADAM_B1) * g
    v = ADAM_B2 * v + (1.0 - ADAM_B2) * (g * g)
    m_hat = m / (1.0 - ADAM_B1 ** ADAM_STEP)
    v_hat = v / (1.0 - ADAM_B2 ** ADAM_STEP)
    return -ADAM_LR * (m_hat / (jnp.sqrt(v_hat) + ADAM_EPS) + ADAM_WD * w), m, v


def _adam(w, g, m, v, name):
    r_, c_ = w.shape
    tr = _tile(r_, 256, 8)

    def body(w_ref, g_ref, m_ref, v_ref, d_ref, mo_ref, vo_ref):
        d_ref[...], mo_ref[...], vo_ref[...] = _adam_update(w_ref[...], g_ref[...], m_ref[...], v_ref[...])

    blk = pl.BlockSpec((tr, c_), lambda i: (i, 0))
    shape = jax.ShapeDtypeStruct((r_, c_), F32)
    return pl.pallas_call(
        body, name=name, grid=(r_ // tr,), in_specs=[blk] * 4, out_specs=[blk] * 3,
        out_shape=[shape] * 3, compiler_params=_params(),
    )(w, g, m, v)


def _sum_partials(recv, name):
    n_, r_, c_ = recv.shape
    tr = _tile(r_, 128, 16)

    def body(p_ref, o_ref):
        acc = p_ref[0].astype(F32)
        for i in range(1, n_):
            acc = acc + p_ref[i].astype(F32)
        o_ref[...] = acc

    return pl.pallas_call(
        body, name=name, grid=(r_ // tr,),
        in_specs=[pl.BlockSpec((n_, tr, c_), lambda i: (0, i, 0))],
        out_specs=pl.BlockSpec((tr, c_), lambda i: (i, 0)),
        out_shape=jax.ShapeDtypeStruct((r_, c_), F32), compiler_params=_params(),
    )(recv)


def _small_sum_adam(parts, w, m, v, name):
    n_, r_, c_ = parts.shape

    def body(p_ref, w_ref, m_ref, v_ref, g_ref, d_ref, mo_ref, vo_ref):
        g = p_ref[0]
        for i in range(1, n_):
            g = g + p_ref[i]
        g_ref[...] = g
        d_ref[...], mo_ref[...], vo_ref[...] = _adam_update(w_ref[...], g, m_ref[...], v_ref[...])

    shape = jax.ShapeDtypeStruct((r_, c_), F32)
    return pl.pallas_call(body, name=name, out_shape=[shape] * 4, compiler_params=_params())(parts, w, m, v)


def _my_place():
    x, y, c = lax.axis_index("x"), lax.axis_index("y"), lax.axis_index("c")
    return x, y, c


def _peer(x, y, c, d):
    px = 1 - x if d & 4 else x
    py = 1 - y if d & 2 else y
    pc = 1 - c if d & 1 else c
    return (px, py, pc), 4 * px + 2 * py + pc


HBM_SPEC =pl.BlockSpec(memory_space=pltpu.HBM)
SEM_SPEC = pl.BlockSpec(memory_space=pltpu.SEMAPHORE)
EFFECT = pltpu.SideEffectType.DATAFLOW_SIDE_EFFECTING


def _in_hbm(a):
    return pltpu.with_memory_space_constraint(a, pltpu.HBM)


def _token_shape():
    return jax.ShapeDtypeStruct((8, HEAD), F32)


SIBLING = 1
OTHER_CHIPS = (4, 2, 6)


def _rows_of(ref, num, rows):
    return ref.at[pl.ds(pl.multiple_of(num * rows, 16), rows), :]


def _gather_start(shards, name):
    nw = len(shards)
    lands = [lax.empty((N_DEV * s.shape[0], s.shape[1]), s.dtype) for s in shards]
    n_to = 1 + len(OTHER_CHIPS)

    def body(*refs):
        ins, lnd = refs[:nw], refs[nw:2 * nw]
        send, from_sib, from_chips, own = (refs[(2 + i) * nw:(3 + i) * nw] for i in range(4))
        token = refs[8 * nw]
        x, y, c = _my_place()
        me = 4 * x + 2 * y + c
        for k in range(nw):
            mine = _rows_of(lnd[k], me, shards[k].shape[0])
            pltpu.make_async_copy(ins[k], mine, own[k]).start()
            for i, d in enumerate((SIBLING,) + OTHER_CHIPS):
                place, _ = _peer(x, y, c, d)
                pltpu.make_async_remote_copy(
                    src_ref=ins[k], dst_ref=mine, send_sem=send[k].at[i],
                    recv_sem=from_sib[k] if i == 0 else from_chips[k].at[i - 1],
                    device_id=place, device_id_type=MESH).start()
        token[...] = jnp.zeros_like(token)

    dma = pltpu.SemaphoreType.DMA
    sems = [dma((n_to,))] * nw + [dma(())] * nw + [dma((len(OTHER_CHIPS),))] * nw + [dma(())] * nw
    thru = [pltpu.HBM(a.shape, a.dtype) for a in list(shards) + lands]
    res = pl.pallas_call(
        body, name=name, out_shape=(*sems, *thru, _token_shape()),
        in_specs=[HBM_SPEC] * (2 * nw),
        out_specs=(*([SEM_SPEC] * (4 * nw)), *([HBM_SPEC] * (2 * nw)), pl.BlockSpec(memory_space=pltpu.VMEM)),
        input_output_aliases={i: 4 * nw + i for i in range(2 * nw)},
        compiler_params=pltpu.CompilerParams(has_side_effects=EFFECT),
    )(*[_in_hbm(s) for s in shards], *[_in_hbm(l) for l in lands])
    return [dict(send=res[k], from_sib=res[nw + k], from_chips=res[2 * nw + k], own=res[3 * nw + k],
                 src=res[4 * nw + k], land=res[5 * nw + k]) for k in range(nw)], res[6 * nw]


def _gather_forward(pending, after, name):
    rows = pending["src"].shape[0]
    n_fw = len(OTHER_CHIPS)

    def body(land_ref, from_chips, after_ref, fw_send, fw_recv, land_thru):
        x, y, c = _my_place()
        sibling, _ = _peer(x, y, c, SIBLING)
        for j, d in enumerate(OTHER_CHIPS):
            _, num = _peer(x, y, c, d)
            block = _rows_of(land_ref, num, rows)
            pltpu.make_async_remote_copy(
                src_ref=block, dst_ref=block, send_sem=fw_send.at[j], recv_sem=from_chips.at[j],
                device_id=sibling, device_id_type=MESH).wait_recv()
            pltpu.make_async_remote_copy(
                src_ref=block, dst_ref=block, send_sem=fw_send.at[j], recv_sem=fw_recv.at[j],
                device_id=sibling, device_id_type=MESH).start()

    land = pending["land"]
    dma = pltpu.SemaphoreType.DMA
    fw_send, fw_recv, land = pl.pallas_call(
        body, name=name, out_shape=(dma((n_fw,)), dma((n_fw,)), pltpu.HBM(land.shape, land.dtype)),
        in_specs=(HBM_SPEC, SEM_SPEC, pl.BlockSpec(memory_space=pl.ANY)),
        out_specs=(SEM_SPEC, SEM_SPEC, HBM_SPEC), input_output_aliases={0: 2},
        compiler_params=pltpu.CompilerParams(has_side_effects=EFFECT),
    )(land, pending["from_chips"], after)
    return dict(pending, land=land, fw_send=fw_send, fw_recv=fw_recv)


def _gather_wait(pending, name):
    rows = pending["src"].shape[0]

    def body(src_ref, land_ref, send, from_sib, own, fw_send, fw_recv, src_dead, got):
        x, y, c = _my_place()
        me = 4 * x + 2 * y + c
        sibling, sib_num = _peer(x, y, c, SIBLING)
        mine = _rows_of(land_ref, me, rows)
        pltpu.make_async_copy(src_ref, mine, own).wait()
        for i in range(1 + len(OTHER_CHIPS)):
            pltpu.make_async_remote_copy(
                src_ref=src_ref, dst_ref=mine, send_sem=send.at[i], recv_sem=from_sib,
                device_id=sibling, device_id_type=MESH).wait_send()
        theirs = _rows_of(land_ref, sib_num, rows)
        pltpu.make_async_remote_copy(
            src_ref=src_ref, dst_ref=theirs, send_sem=send.at[0], recv_sem=from_sib,
            device_id=sibling, device_id_type=MESH).wait_recv()
        for j, d in enumerate(OTHER_CHIPS):
            _, num = _peer(x, y, c, d)
            sent = _rows_of(land_ref, num, rows)
            _, got_num = _peer(x, y, c, d | SIBLING)
            arrived = _rows_of(land_ref, got_num, rows)
            cp = pltpu.make_async_remote_copy(
                src_ref=sent, dst_ref=arrived, send_sem=fw_send.at[j], recv_sem=fw_recv.at[j],
                device_id=sibling, device_id_type=MESH)
            cp.wait_send()
            cp.wait_recv()

    src, land = pending["src"], pending["land"]
    return pl.pallas_call(
        body, name=name, out_shape=(pltpu.HBM(src.shape, src.dtype), pltpu.HBM(land.shape, land.dtype)),
        in_specs=(HBM_SPEC, HBM_SPEC) + (SEM_SPEC,) * 5,
        out_specs=(HBM_SPEC, HBM_SPEC), input_output_aliases={0: 0, 1: 1},
        compiler_params=pltpu.CompilerParams(has_side_effects=EFFECT),
    )(src, land, pending["send"], pending["from_sib"], pending["own"], pending["fw_send"], pending["fw_recv"])[1]


def _gather_end(pending, after, n):
    return _gather_wait(_gather_forward(pending, after, f"gather_forward_{n}"), f"gather_wait_{n}")


def _scatter_start(full, name):
    rows, cols = full.shape[0] // N_DEV, full.shape[1]
    land = lax.empty((N_DEV, rows, cols), full.dtype)

    def body(full_ref, land_ref, send, recv, own, full_thru, land_thru, token):
        x, y, c = _my_place()
        me = 4 * x + 2 * y + c
        slab = land_ref.at[me]
        pltpu.make_async_copy(full_ref.at[pl.ds(pl.multiple_of(me * rows, 16), rows), :], slab, own).start()
        for d in range(1, N_DEV):
            place, num = _peer(x, y, c, d)
            pltpu.make_async_remote_copy(
                src_ref=full_ref.at[pl.ds(pl.multiple_of(num * rows, 16), rows), :], dst_ref=slab,
                send_sem=send.at[d - 1], recv_sem=recv.at[d - 1], device_id=place, device_id_type=MESH).start()
        token[...] = jnp.zeros_like(token)

    res = pl.pallas_call(
        body, name=name,
        out_shape=(pltpu.SemaphoreType.DMA((N_DEV - 1,)), pltpu.SemaphoreType.DMA((N_DEV - 1,)),
                   pltpu.SemaphoreType.DMA(()),
                   pltpu.HBM(full.shape, full.dtype), pltpu.HBM(land.shape, land.dtype), _token_shape()),
        in_specs=(HBM_SPEC, HBM_SPEC),
        out_specs=(SEM_SPEC, SEM_SPEC, SEM_SPEC, HBM_SPEC, HBM_SPEC, pl.BlockSpec(memory_space=pltpu.VMEM)),
        input_output_aliases={0: 3, 1: 4},
        compiler_params=pltpu.CompilerParams(has_side_effects=EFFECT),
    )(_in_hbm(full), _in_hbm(land))
    return dict(send=res[0], recv=res[1], own=res[2], src=res[3], land=res[4]), res[5]


def _scatter_wait(pending, after, name):
    rows = pending["land"].shape[1]

    def body(src_ref, land_ref, send, recv, own, after_ref, src_dead, got):
        x, y, c = _my_place()
        me = 4 * x + 2 * y + c
        pltpu.make_async_copy(src_ref.at[pl.ds(pl.multiple_of(me * rows, 16), rows), :], land_ref.at[me], own).wait()
        for d in range(1, N_DEV):
            place, num = _peer(x, y, c, d)
            cp = pltpu.make_async_remote_copy(
                src_ref=src_ref.at[pl.ds(pl.multiple_of(num * rows, 16), rows), :], dst_ref=land_ref.at[me],
                send_sem=send.at[d - 1], recv_sem=recv.at[d - 1], device_id=place, device_id_type=MESH)
            cp.wait_send()
            cp.wait_recv()

    src, land = pending["src"], pending["land"]
    return pl.pallas_call(
        body, name=name, out_shape=(pltpu.HBM(src.shape, src.dtype), pltpu.HBM(land.shape, land.dtype)),
        in_specs=(HBM_SPEC, HBM_SPEC, SEM_SPEC, SEM_SPEC, SEM_SPEC, pl.BlockSpec(memory_space=pl.ANY)),
        out_specs=(HBM_SPEC, HBM_SPEC), input_output_aliases={0: 0, 1: 1},
        compiler_params=pltpu.CompilerParams(has_side_effects=EFFECT),
    )(src, land, pending["send"], pending["recv"], pending["own"], after)[1]


BIG = ("ffn1_w_in", "ffn1_w_out", "mix_w_in", "w_branch_a", "w_branch_b", "mix_w_out", "ffn2_w_in", "ffn2_w_out")
TRANSPOSED = ("ffn1_w_in", "mix_w_in", "w_branch_b", "ffn2_w_in")
SMALL = ("ln1_g", "ln1_b", "ln2_g", "ln2_b", "ln3_g", "ln3_b", "hgrn_norm_g", "hgrn_lb_fwd", "hgrn_lb_bwd")
SMALL_ROWS = 16


def _local_step(x, target, weight, emit, emit_small, sp):
    b_, s_, d_ = x.shape
    t_ = b_ * s_
    x2, tgt = x.reshape(t_, d_), target.reshape(t_, d_)
    xb = x2.astype(BF16)
    w1i = weight("ffn1_w_in", xb)
    g1, u1, a1 = _ffn_in_fwd(xb, w1i, "ffn1_in")
    w1o = weight("ffn1_w_out", a1)
    r1, h1, h1b = _mm_res_ln_fwd(a1, w1o, x2, sp["ln1_g"], sp["ln1_b"], 0.5, "ffn1_out_ln1")
    wmx = weight("mix_w_in", h1b)
    proj = _mm_nt(h1b, wmx, F32, "mix_in")
    ya_in, o_sum = _hgrn_fwd(proj, sp["hgrn_lb_fwd"], sp["hgrn_lb_bwd"], sp["hgrn_norm_g"], b_, s_, d_, "hgrn_fwd")
    tabs = _rope_tables(s_)
    outs, lses = [], []
    group_col = [(5 * d_ + gi * QKV_GROUP) // HEAD for gi in range(len(ATTN_GROUPS))]
    for gi, (_, dil) in enumerate(ATTN_GROUPS):
        o_g, lse_g = _attn_fwd(proj, tabs, b_, s_, group_col[gi], dil, f"attn_fwd_{gi}")
        outs.append(o_g)
        lses.append(lse_g)
    ob = _combine_fwd(outs, lses, "attn_combine")
    wa, wb = weight("w_branch_a", ya_in), weight("w_branch_b", ob)
    ya, yb, z = _gate_out_fwd(ya_in, ob, wa, wb, proj, d_, "branch_gate")
    wo = weight("mix_w_out", z)
    r2, h2, h2b = _mm_res_ln_fwd(z, wo, h1, sp["ln2_g"], sp["ln2_b"], 1.0, "mix_out_ln2")
    w2i = weight("ffn2_w_in", h2b)
    g2, u2, a2 = _ffn_in_fwd(h2b, w2i, "ffn2_in")
    w2o = weight("ffn2_w_out", a2)
    dr3, dr3b, dg3, db3, loss = _mm_res_loss_bwd(a2, w2o, h2, tgt, sp["ln3_g"], sp["ln3_b"], 0.5, "ffn2_out_loss")
    dep = emit("ffn2_w_out", _mm_tn(a2, dr3b, 0.5, "d_ffn2_w_out"))
    dgate2, dup2 = _ffn_mid_bwd(dr3b, w2o, g2, u2, 0.5, "ffn2_mid_bwd", dep)
    du2 = (dgate2, dup2)
    dep = emit("ffn2_w_in", _mm_tn(du2, h2b, 1.0, "d_ffn2_w_in"))
    dr2, dr2b, dg2, db2 = _mm_nn_res_lnbwd(du2, w2i, dr3, r2, sp["ln2_g"], "ffn2_in_bwd_ln2", dep)
    dep = emit("mix_w_out", _mm_tn(z, dr2b, 1.0, "d_mix_w_out"))
    dya, dyb, dga, dgb = _dz_gate_bwd(dr2b, wo, proj, ya, yb, d_, "branch_gate_bwd", dep)
    dep = emit("w_branch_a", _mm_tn(ya_in, dya, 1.0, "d_w_branch_a"))
    dya_in = _mm_nt(dya, wa, F32, "branch_a_bwd", dep)
    dep = emit("w_branch_b", _mm_tn(dyb, ob, 1.0, "d_w_branch_b"))
    dob = _mm_nn(dyb, wb, F32, "branch_b_bwd", dep)
    dhq, dhff, dhfb, dhi, dhog, dng, dlbf, dlbb = _hgrn_bwd(
        proj, sp["hgrn_lb_fwd"], sp["hgrn_lb_bwd"], sp["hgrn_norm_g"], o_sum, dya_in, b_, s_, d_, "hgrn_bwd")
    dogs, cgs = _combine_bwd(dob, outs, lses, "attn_combine_bwd")
    dqkv = []
    for gi, (_, dil) in enumerate(ATTN_GROUPS):
        dqkv += _attn_bwd(proj, tabs, dogs[gi], cgs[gi], lses[gi], b_, s_, group_col[gi], dil, f"attn_bwd_{gi}")
    dproj = jnp.concatenate([dhq, dhff, dhfb, dhi, dhog] + [t.astype(BF16) for t in dqkv] + [dga, dgb], axis=1)
    dep = emit("mix_w_in", _mm_tn(dproj, h1b, 1.0, "d_mix_w_in"))
    dr1, dr1b, dg1, db1 = _mm_nn_res_lnbwd(dproj, wmx, dr2, r1, sp["ln1_g"], "mix_in_bwd_ln1", dep)
    dep_small = emit_small({"ln1_g": dg1, "ln1_b": db1, "ln2_g": dg2, "ln2_b": db2, "ln3_g": dg3, "ln3_b": db3,
                            "hgrn_norm_g": dng, "hgrn_lb_fwd": dlbf, "hgrn_lb_bwd": dlbb})
    dep = emit("ffn1_w_out", _mm_tn(a1, dr1b, 0.5, "d_ffn1_w_out")) + dep_small
    dgate1, dup1 = _ffn_mid_bwd(dr1b, w1o, g1, u1, 0.5, "ffn1_mid_bwd", dep)
    du1 = (dgate1, dup1)
    dep = emit("ffn1_w_in", _mm_tn(du1, xb, 1.0, "d_ffn1_w_in"))
    grad_x = _mm_nn_res(du1, w1i, dr1, "ffn1_in_bwd", dep)
    return loss, grad_x.reshape(b_, s_, d_)


def _pack_small(vals):
    rows = jnp.concatenate([vals[n] for n in SMALL], axis=0)
    return jnp.pad(rows, ((0, SMALL_ROWS - rows.shape[0]), (0, 0)))


def _unpack_small(packed):
    out, r = {}, 0
    for n in SMALL:
        k = 2 if n.startswith("hgrn_lb") else 1
        out[n] = packed[r:r + k]
        r += k
    return out


def kernel(x, ffn1_w_in, ffn1_w_out, ln1_g, ln1_b, mix_w_in, hgrn_lb_fwd, hgrn_lb_bwd, hgrn_norm_g, w_branch_a, w_branch_b, mix_w_out, ln2_g, ln2_b, ffn2_w_in, ffn2_w_out, ln3_g, ln3_b, loss_target, m_ffn1_w_in, m_ffn1_w_out, m_ln1_g, m_ln1_b, m_mix_w_in, m_hgrn_lb_fwd, m_hgrn_lb_bwd, m_hgrn_norm_g, m_w_branch_a, m_w_branch_b, m_mix_w_out, m_ln2_g, m_ln2_b, m_ffn2_w_in, m_ffn2_w_out, m_ln3_g, m_ln3_b, v_ffn1_w_in, v_ffn1_w_out, v_ln1_g, v_ln1_b, v_mix_w_in, v_hgrn_lb_fwd, v_hgrn_lb_bwd, v_hgrn_norm_g, v_w_branch_a, v_w_branch_b, v_mix_w_out, v_ln2_g, v_ln2_b, v_ffn2_w_in, v_ffn2_w_out, v_ln3_g, v_ln3_b):
    args = dict(locals())
    big_w = {n: args[n][0] for n in BIG}
    sp = {n: args[n] for n in SMALL}
    def rows_bf16(n, zero=0.0):
        w = big_w[n] + zero
        return (w.T if n in TRANSPOSED else w).astype(BF16)

    first, rest = BIG[:2], BIG[2:]
    pending, token = _gather_start([rows_bf16(n) for n in first], "gather_start_ffn1")
    gathering = dict(zip(first, pending))
    pending, all_started = _gather_start([rows_bf16(n, token[0, 0]) for n in rest], "gather_start_rest")
    gathering.update(zip(rest, pending))
    scattering = {}

    def weight(n, after):
        if n == first[0]:
            after = all_started
        return _gather_end(gathering[n], after, n)

    def emit(n, grad):
        scattering[n], token = _scatter_start(grad, f"scatter_start_{n}")
        return token

    def emit_small(grads):
        pending, token = _gather_start([_pack_small(grads)], "gather_start_small")
        scattering["small"] = pending[0]
        return token

    loss_part, grad_x = _local_step(x, loss_target, weight, emit, emit_small, sp)
    loss = lax.psum(loss_part[0, 0], ("x", "y", "c"))
    out_g, out_d, out_m, out_v = {}, {}, {}, {}
    done = grad_x
    for n in ("ffn2_w_out", "ffn2_w_in", "mix_w_out", "w_branch_a", "w_branch_b", "mix_w_in", "small",
              "ffn1_w_out", "ffn1_w_in"):
        if n == "small":
            parts = _gather_end(scattering[n], done, n)
            res = _small_sum_adam(parts.reshape(N_DEV, SMALL_ROWS, parts.shape[1]), _pack_small(sp),
                                  _pack_small({n: args["m_" + n] for n in SMALL}),
                                  _pack_small({n: args["v_" + n] for n in SMALL}), "small_adam")
            sg, sd, sm, sv = (_unpack_small(r) for r in res)
            out_g.update(sg), out_d.update(sd), out_m.update(sm), out_v.update(sv)
            done = res[3]
            continue
        g = _sum_partials(_scatter_wait(scattering[n], done, f"scatter_wait_{n}"), f"sum_{n}")
        if n in TRANSPOSED:
            g = g.T
        d_w, m_w, v_w = _adam(big_w[n], g, args["m_" + n][0], args["v_" + n][0], f"adam_{n}")
        out_g[n], out_d[n], out_m[n], out_v[n] = g[None], d_w[None], m_w[None], v_w[None]
        done = v_w
    order = ("ffn1_w_in", "ffn1_w_out", "ln1_g", "ln1_b", "mix_w_in", "hgrn_lb_fwd", "hgrn_lb_bwd", "hgrn_norm_g",
             "w_branch_a", "w_branch_b", "mix_w_out", "ln2_g", "ln2_b", "ffn2_w_in", "ffn2_w_out", "ln3_g", "ln3_b")
    return (loss, grad_x, *[out_g[n] for n in order], *[out_d[n] for n in order],
            *[out_m[n] for n in order], *[out_v[n] for n in order])
```

```python
import jax
import jax.numpy as jnp
from jax import lax
from jax.experimental import pallas as pl
from jax.experimental.pallas import tpu as pltpu

F32 = jnp.float32
BF16 = jnp.bfloat16

N_DEV = 8
HEAD = 128
ATTN_GROUPS = ((128, 1), (512, 4), (2048, 16))
ATTN_HEADS = 4
ATTN_HALF = 64
QKV_GROUP = 3 * ATTN_HEADS * HEAD
QKV_WIDTH = len(ATTN_GROUPS) * QKV_GROUP
ATTN_OUT = ATTN_HEADS * HEAD
ROPE_THETA = 500000.0
ROPE_DIM = HEAD // 4
ALPHA = 2.0 ** 0.25
LN_EPS = 1e-5
NEG_INF = -1e30
ADAM_LR, ADAM_B1, ADAM_B2, ADAM_EPS, ADAM_WD, ADAM_STEP = 0.001, 0.9, 0.999, 1e-08, 0.01, 10
VMEM_LIMIT = 56 * 1024 * 1024

NT = (((1,), (1,)), ((), ()))
NN = (((1,), (0,)), ((), ()))
TN = (((0,), (0,)), ((), ()))
MESH = pl.DeviceIdType.MESH


def _dot(a, b, dims):
    return lax.dot_general(a, b, dims, preferred_element_type=F32)


def _tile(n, pref, mult=128):
    if n <= pref:
        return n
    t = (pref // mult) * mult
    while t >= mult:
        if n % t == 0:
            return t
        t -= mult
    return n


def _tile_multi(ns, pref, mult=128):
    t = (pref // mult) * mult
    while t >= mult:
        if all(n % t == 0 for n in ns):
            return t
        t -= mult
    raise ValueError(f"no common tile for {ns}")


def _params(**kw):
    return pltpu.CompilerParams(vmem_limit_bytes=VMEM_LIMIT, **kw)


def _after(body, n_in, dep):
    if dep is None:
        return body, [], []

    def wrapped(*refs):
        body(*refs[:n_in], *refs[n_in + 1:])

    return wrapped, [pl.BlockSpec(dep.shape, lambda *_: (0,) * dep.ndim)], [dep]


def _pieces(a):
    pieces = tuple(a) if isinstance(a, (tuple, list)) else (a,)
    assert all(p.shape == pieces[0].shape for p in pieces)
    return pieces, pieces[0].shape[0], pieces[0].shape[1], len(pieces)


def _for_piece(step, p, per, npc, fn):
    if npc == 1:
        fn()
    else:
        pl.when((step >= p * per) & (step < (p + 1) * per))(fn)


def _sigmoid(x):
    return jax.nn.sigmoid(x)


def _dsilu(x, s):
    return s * (1.0 + x * (1.0 - s))


def _ln_stats(r):
    mu = jnp.mean(r, axis=-1, keepdims=True)
    xc = r - mu
    var = jnp.mean(xc * xc, axis=-1, keepdims=True)
    rstd = lax.rsqrt(var + LN_EPS)
    return xc * rstd, rstd


def _ln_bwd(dy, xhat, rstd, g):
    dyg = dy * g
    m1 = jnp.mean(dyg, axis=-1, keepdims=True)
    m2 = jnp.mean(dyg * xhat, axis=-1, keepdims=True)
    return rstd * (dyg - m1 - xhat * m2)


ROW_TILE = 1024
SUB_ROWS = 256


def _once(shape, index_map):
    return pl.BlockSpec(shape, index_map, pipeline_mode=pl.Buffered(1))


def _for_row_blocks(tm, fn):
    sub = SUB_ROWS if tm % SUB_ROWS == 0 else tm

    def step(s, carry):
        fn(pl.ds(pl.multiple_of(s * sub, sub), sub))
        return carry

    lax.fori_loop(0, tm // sub, step, 0)


def _row_runs(tm):
    sub = SUB_ROWS if tm % SUB_ROWS == 0 else tm
    return [slice(s, s + sub) for s in range(0, tm, sub)]


def _ffn_in_fwd(xb, w_t, name):
    t_, d_ = xb.shape
    f_ = w_t.shape[0] // 2
    tm, tn = _tile(t_, ROW_TILE, 8), _tile(f_, 512)
    nj = f_ // tn

    def body(x_ref, wg_ref, wu_ref, g_ref, u_ref, a_ref):
        wg, wu = wg_ref[...], wu_ref[...]
        for rows in _row_runs(tm):
            x = x_ref[rows, :]
            g = _dot(x, wg, NT)
            u = _dot(x, wu, NT)
            g_ref[rows, :] = g.astype(BF16)
            u_ref[rows, :] = u.astype(BF16)
            a_ref[rows, :] = (g * _sigmoid(g) * u).astype(BF16)

    return pl.pallas_call(
        body, name=name, grid=(t_ // tm, nj),
        in_specs=[pl.BlockSpec((tm, d_), lambda i, j: (i, 0)),
                  pl.BlockSpec((tn, d_), lambda i, j: (j, 0)),
                  pl.BlockSpec((tn, d_), lambda i, j: (j + nj, 0))],
        out_specs=[pl.BlockSpec((tm, tn), lambda i, j: (i, j))] * 3,
        out_shape=[jax.ShapeDtypeStruct((t_, f_), BF16)] * 3,
        compiler_params=_params(),
    )(xb, w_t, w_t)


WHOLE_WEIGHT_BYTES = 24 * 1024 * 1024


def _k_plan(t_, k_, d_):
    if k_ * d_ * 2 > WHOLE_WEIGHT_BYTES:
        return False, _tile(t_, ROW_TILE, 8), _tile(k_, 512), _once
    if k_ <= 2048:
        return True, _tile(t_, ROW_TILE, 8), k_, _once
    return True, _tile(t_, SUB_ROWS, 8), k_, pl.BlockSpec


def _mm_res_ln_fwd(a, w, res, g, b, scale, name):
    t_, k_ = a.shape
    d_ = w.shape[1]
    whole, tm, tk, row_spec = _k_plan(t_, k_, d_)
    nk = k_ // tk

    def body(a_ref, w_ref, res_ref, g_ref, b_ref, r_ref, h_ref, hb_ref, *scratch):
        k = pl.program_id(1)
        if not whole:
            acc, = scratch

            @pl.when(k == 0)
            def _():
                acc[...] = jnp.zeros_like(acc)

            acc[...] += _dot(a_ref[...], w_ref[...], NN)

        @pl.when(k == nk - 1)
        def _():
            def rows_out(rows):
                prod = _dot(a_ref[rows, :], w_ref[...], NN) if whole else acc[rows, :]
                r = ALPHA * res_ref[rows, :] + scale * prod
                xhat, _ = _ln_stats(r)
                h = xhat * g_ref[...] + b_ref[...]
                r_ref[rows, :] = r
                h_ref[rows, :] = h
                hb_ref[rows, :] = h.astype(BF16)

            _for_row_blocks(tm, rows_out)

    row = row_spec((tm, d_), lambda i, k: (i, 0))
    vec = pl.BlockSpec((1, d_), lambda i, k: (0, 0))
    w_spec = _once((tk, d_), lambda i, k: (0, 0)) if whole else pl.BlockSpec((tk, d_), lambda i, k: (k, 0))
    return pl.pallas_call(
        body, name=name, grid=(t_ // tm, nk),
        in_specs=[pl.BlockSpec((tm, tk), lambda i, k: (i, k)), w_spec, row, vec, vec],
        out_specs=[row, row, row],
        out_shape=[jax.ShapeDtypeStruct((t_, d_), F32), jax.ShapeDtypeStruct((t_, d_), F32),
                   jax.ShapeDtypeStruct((t_, d_), BF16)],
        scratch_shapes=[] if whole else [pltpu.VMEM((tm, d_), F32)],
        compiler_params=_params(),
    )(a, w, res, g, b)


def _mm_res_loss_bwd(a, w, res, target, g, b, scale, name):
    t_, k_ = a.shape
    d_ = w.shape[1]
    whole, tm, tk, row_spec = _k_plan(t_, k_, d_)
    nk = k_ // tk

    def body(a_ref, w_ref, res_ref, t_ref, g_ref, b_ref, dr_ref, drb_ref, dg_ref, db_ref, loss_ref, *scratch):
        i, k = pl.program_id(0), pl.program_id(1)

        @pl.when((i == 0) & (k == 0))
        def _():
            dg_ref[...] = jnp.zeros_like(dg_ref)
            db_ref[...] = jnp.zeros_like(db_ref)
            loss_ref[...] = jnp.zeros_like(loss_ref)

        if not whole:
            acc, = scratch

            @pl.when(k == 0)
            def _():
                acc[...] = jnp.zeros_like(acc)

            acc[...] += _dot(a_ref[...], w_ref[...], NN)

        @pl.when(k == nk - 1)
        def _():
            def rows_out(rows):
                prod = _dot(a_ref[rows, :], w_ref[...], NN) if whole else acc[rows, :]
                r = ALPHA * res_ref[rows, :] + scale * prod
                xhat, rstd = _ln_stats(r)
                gain = g_ref[...]
                err = xhat * gain + b_ref[...] - t_ref[rows, :]
                loss_ref[...] += (0.5 / d_) * jnp.sum(err * err)
                dy = err * (1.0 / d_)
                dr = _ln_bwd(dy, xhat, rstd, gain)
                dr_ref[rows, :] = dr
                drb_ref[rows, :] = dr.astype(BF16)
                dg_ref[...] += jnp.sum(dy * xhat, axis=0, keepdims=True)
                db_ref[...] += jnp.sum(dy, axis=0, keepdims=True)

            _for_row_blocks(tm, rows_out)

    row = row_spec((tm, d_), lambda i, k: (i, 0))
    vec = pl.BlockSpec((1, d_), lambda i, k: (0, 0))
    w_spec = _once((tk, d_), lambda i, k: (0, 0)) if whole else pl.BlockSpec((tk, d_), lambda i, k: (k, 0))
    return pl.pallas_call(
        body, name=name, grid=(t_ // tm, nk),
        in_specs=[pl.BlockSpec((tm, tk), lambda i, k: (i, k)), w_spec, row, row, vec, vec],
        out_specs=[row, row, vec, vec, pl.BlockSpec((1, HEAD), lambda i, k: (0, 0))],
        out_shape=[jax.ShapeDtypeStruct((t_, d_), F32), jax.ShapeDtypeStruct((t_, d_), BF16),
                   jax.ShapeDtypeStruct((1, d_), F32), jax.ShapeDtypeStruct((1, d_), F32),
                   jax.ShapeDtypeStruct((1, HEAD), F32)],
        scratch_shapes=[] if whole else [pltpu.VMEM((tm, d_), F32)],
        compiler_params=_params(),
    )(a, w, res, target, g, b)


def _mm_nt(a, w_t, out_dtype, name, dep=None):
    t_, k_ = a.shape
    n_ = w_t.shape[0]
    tm, tn = _tile(t_, ROW_TILE, 8), _tile(n_, 512)

    def body(a_ref, w_ref, o_ref):
        o_ref[...] = _dot(a_ref[...], w_ref[...], NT).astype(out_dtype)

    body, dep_specs, deps = _after(body, 2, dep)
    return pl.pallas_call(
        body, name=name, grid=(t_ // tm, n_ // tn),
        in_specs=[pl.BlockSpec((tm, k_), lambda i, j: (i, 0)),
                  pl.BlockSpec((tn, k_), lambda i, j: (j, 0)), *dep_specs],
        out_specs=pl.BlockSpec((tm, tn), lambda i, j: (i, j)),
        out_shape=jax.ShapeDtypeStruct((t_, n_), out_dtype),
        compiler_params=_params(),
    )(a, w_t, *deps)


def _mm_nn(a, w, out_dtype, name, dep=None):
    t_, k_ = a.shape
    n_ = w.shape[1]
    tm, tn = _tile(t_, ROW_TILE, 8), _tile(n_, 512)

    def body(a_ref, w_ref, o_ref):
        o_ref[...] = _dot(a_ref[...], w_ref[...], NN).astype(out_dtype)

    body, dep_specs, deps = _after(body, 2, dep)
    return pl.pallas_call(
        body, name=name, grid=(t_ // tm, n_ // tn),
        in_specs=[pl.BlockSpec((tm, k_), lambda i, j: (i, 0)),
                  pl.BlockSpec((k_, tn), lambda i, j: (0, j)), *dep_specs],
        out_specs=pl.BlockSpec((tm, tn), lambda i, j: (i, j)),
        out_shape=jax.ShapeDtypeStruct((t_, n_), out_dtype),
        compiler_params=_params(),
    )(a, w, *deps)


def _mm_tn(a, b, scale, name):
    pieces, t_, mp, npc = _pieces(a)
    n_ = b.shape[1]
    tm = _tile(mp, 512)
    per = mp // tm

    def body(*refs):
        b_ref, o_ref = refs[npc], refs[npc + 1]
        for p in range(npc):
            def piece_out(p=p):
                o_ref[...] = (scale * _dot(refs[p][...], b_ref[...], TN)).astype(BF16)

            _for_piece(pl.program_id(0), p, per, npc, piece_out)

    return pl.pallas_call(
        body, name=name, grid=(npc * per,),
        in_specs=[pl.BlockSpec((t_, tm), lambda i, p=p: (0, jnp.clip(i - p * per, 0, per - 1))) for p in range(npc)]
        + [_once((t_, n_), lambda i: (0, 0))],
        out_specs=pl.BlockSpec((tm, n_), lambda i: (i, 0)),
        out_shape=jax.ShapeDtypeStruct((npc * mp, n_), BF16),
        compiler_params=_params(),
    )(*pieces, b)


def _gate_out_fwd(ya_in, ob, wa, wb_t, proj, d_, name):
    t_ = ya_in.shape[0]
    goff = 5 * d_ + QKV_WIDTH
    tm, tn = _tile(t_, ROW_TILE, 8), _tile_multi([d_, goff], 512)
    ja, jb = goff // tn, (goff + d_) // tn

    def body(ya_ref, ob_ref, wa_ref, wb_ref, ga_ref, gb_ref, yao_ref, ybo_ref, z_ref):
        wa, wb = wa_ref[...], wb_ref[...]
        for rows in _row_runs(tm):
            y_a = _dot(ya_ref[rows, :], wa, NN)
            y_b = _dot(ob_ref[rows, :], wb, NT)
            yao_ref[rows, :] = y_a.astype(BF16)
            ybo_ref[rows, :] = y_b.astype(BF16)
            z_ref[rows, :] = (_sigmoid(ga_ref[rows, :]) * y_a + _sigmoid(gb_ref[rows, :]) * y_b).astype(BF16)

    tile = pl.BlockSpec((tm, tn), lambda i, j: (i, j))
    return pl.pallas_call(
        body, name=name, grid=(t_ // tm, d_ // tn),
        in_specs=[pl.BlockSpec((tm, d_), lambda i, j: (i, 0)),
                  pl.BlockSpec((tm, ATTN_OUT), lambda i, j: (i, 0)),
                  pl.BlockSpec((d_, tn), lambda i, j: (0, j)),
                  pl.BlockSpec((tn, ATTN_OUT), lambda i, j: (j, 0)),
                  pl.BlockSpec((tm, tn), lambda i, j: (i, ja + j)),
                  pl.BlockSpec((tm, tn), lambda i, j: (i, jb + j))],
        out_specs=[tile, tile, tile],
        out_shape=[jax.ShapeDtypeStruct((t_, d_), BF16)] * 3,
        compiler_params=_params(),
    )(ya_in, ob, wa, wb_t, proj, proj)


def _ffn_mid_bwd(drb, w_out, gate, up, scale, name, dep=None):
    t_, d_ = drb.shape
    f_ = w_out.shape[0]
    tm, tn = _tile(t_, ROW_TILE, 8), _tile(f_, 512)

    def body(dr_ref, w_ref, g_ref, u_ref, dg_ref, du_ref):
        w = w_ref[...]
        for rows in _row_runs(tm):
            da = scale * _dot(dr_ref[rows, :], w, NT)
            g = g_ref[rows, :].astype(F32)
            s = _sigmoid(g)
            dg_ref[rows, :] = (da * u_ref[rows, :].astype(F32) * _dsilu(g, s)).astype(BF16)
            du_ref[rows, :] = (da * g * s).astype(BF16)

    body, dep_specs, deps = _after(body, 4, dep)
    tile = pl.BlockSpec((tm, tn), lambda i, j: (i, j))
    return pl.pallas_call(
        body, name=name, grid=(t_ // tm, f_ // tn),
        in_specs=[pl.BlockSpec((tm, d_), lambda i, j: (i, 0)),
                  pl.BlockSpec((tn, d_), lambda i, j: (j, 0)), tile, tile, *dep_specs],
        out_specs=[tile, tile],
        out_shape=[jax.ShapeDtypeStruct((t_, f_), BF16)] * 2,
        compiler_params=_params(),
    )(drb, w_out, gate, up, *deps)


def _mm_nn_res_lnbwd(a, w, dres, r, g, name, dep=None):
    pieces, t_, kp, npc = _pieces(a)
    d_ = w.shape[1]
    tm, tk = _tile(t_, ROW_TILE, 8), _tile(kp, 512)
    per = kp // tk
    nk = npc * per

    def body(*refs):
        w_ref, dres_ref, r_ref, g_ref, dr_ref, drb_ref, dg_ref, db_ref, acc = refs[npc:]
        i, k = pl.program_id(0), pl.program_id(1)

        @pl.when(k == 0)
        def _():
            acc[...] = jnp.zeros_like(acc)

        @pl.when((i == 0) & (k == 0))
        def _():
            dg_ref[...] = jnp.zeros_like(dg_ref)
            db_ref[...] = jnp.zeros_like(db_ref)

        for p in range(npc):
            def piece_in(p=p):
                acc[...] += _dot(refs[p][...], w_ref[...], NN)

            _for_piece(k, p, per, npc, piece_in)

        @pl.when(k == nk - 1)
        def _():
            def rows_out(rows):
                dy = acc[rows, :] + ALPHA * dres_ref[rows, :]
                xhat, rstd = _ln_stats(r_ref[rows, :])
                dr = _ln_bwd(dy, xhat, rstd, g_ref[...])
                dr_ref[rows, :] = dr
                drb_ref[rows, :] = dr.astype(BF16)
                dg_ref[...] += jnp.sum(dy * xhat, axis=0, keepdims=True)
                db_ref[...] += jnp.sum(dy, axis=0, keepdims=True)

            _for_row_blocks(tm, rows_out)

    body, dep_specs, deps = _after(body, npc + 4, dep)
    row = _once((tm, d_), lambda i, k: (i, 0))
    vec = pl.BlockSpec((1, d_), lambda i, k: (0, 0))
    return pl.pallas_call(
        body, name=name, grid=(t_ // tm, nk),
        in_specs=[pl.BlockSpec((tm, tk), lambda i, k, p=p: (i, jnp.clip(k - p * per, 0, per - 1))) for p in range(npc)]
        + [pl.BlockSpec((tk, d_), lambda i, k: (k, 0)), row, row, vec, *dep_specs],
        out_specs=[row, row, vec, vec],
        out_shape=[jax.ShapeDtypeStruct((t_, d_), F32), jax.ShapeDtypeStruct((t_, d_), BF16),
                   jax.ShapeDtypeStruct((1, d_), F32), jax.ShapeDtypeStruct((1, d_), F32)],
        scratch_shapes=[pltpu.VMEM((tm, d_), F32)],
        compiler_params=_params(),
    )(*pieces, w, dres, r, g, *deps)


def _mm_nn_res(a, w, dres, name, dep=None):
    pieces, t_, kp, npc = _pieces(a)
    d_ = w.shape[1]
    tm, tk = _tile(t_, ROW_TILE, 8), _tile(kp, 512)
    per = kp // tk
    nk = npc * per

    def body(*refs):
        w_ref, dres_ref, o_ref, acc = refs[npc:]
        k = pl.program_id(1)

        @pl.when(k == 0)
        def _():
            acc[...] = jnp.zeros_like(acc)

        for p in range(npc):
            def piece_in(p=p):
                acc[...] += _dot(refs[p][...], w_ref[...], NN)

            _for_piece(k, p, per, npc, piece_in)

        @pl.when(k == nk - 1)
        def _():
            def rows_out(rows):
                o_ref[rows, :] = acc[rows, :] + ALPHA * dres_ref[rows, :]

            _for_row_blocks(tm, rows_out)

    body, dep_specs, deps = _after(body, npc + 2, dep)
    row = _once((tm, d_), lambda i, k: (i, 0))
    return pl.pallas_call(
        body, name=name, grid=(t_ // tm, nk),
        in_specs=[pl.BlockSpec((tm, tk), lambda i, k, p=p: (i, jnp.clip(k - p * per, 0, per - 1))) for p in range(npc)]
        + [pl.BlockSpec((tk, d_), lambda i, k: (k, 0)), row, *dep_specs],
        out_specs=row,
        out_shape=jax.ShapeDtypeStruct((t_, d_), F32),
        scratch_shapes=[pltpu.VMEM((tm, d_), F32)],
        compiler_params=_params(),
    )(*pieces, w, dres, *deps)


def _dz_gate_bwd(drb, w_out, proj, ya, yb, d_, name, dep=None):
    t_ = drb.shape[0]
    goff = 5 * d_ + QKV_WIDTH
    tm, tn = _tile(t_, ROW_TILE, 8), _tile_multi([d_, goff], 512)
    ja, jb = goff // tn, (goff + d_) // tn

    def body(dr_ref, w_ref, ga_ref, gb_ref, ya_ref, yb_ref, dya_ref, dyb_ref, dga_ref, dgb_ref):
        w = w_ref[...]
        for rows in _row_runs(tm):
            dz = _dot(dr_ref[rows, :], w, NT)
            sa, sb = _sigmoid(ga_ref[rows, :]), _sigmoid(gb_ref[rows, :])
            dya_ref[rows, :] = (dz * sa).astype(BF16)
            dyb_ref[rows, :] = (dz * sb).astype(BF16)
            dga_ref[rows, :] = (dz * ya_ref[rows, :].astype(F32) * sa * (1.0 - sa)).astype(BF16)
            dgb_ref[rows, :] = (dz * yb_ref[rows, :].astype(F32) * sb * (1.0 - sb)).astype(BF16)

    body, dep_specs, deps = _after(body, 6, dep)
    tile = pl.BlockSpec((tm, tn), lambda i, j: (i, j))
    return pl.pallas_call(
        body, name=name, grid=(t_ // tm, d_ // tn),
        in_specs=[pl.BlockSpec((tm, d_), lambda i, j: (i, 0)),
                  pl.BlockSpec((tn, d_), lambda i, j: (j, 0)),
                  pl.BlockSpec((tm, tn), lambda i, j: (i, ja + j)),
                  pl.BlockSpec((tm, tn), lambda i, j: (i, jb + j)), tile, tile, *dep_specs],
        out_specs=[tile] * 4,
        out_shape=[jax.ShapeDtypeStruct((t_, d_), BF16)] * 4,
        compiler_params=_params(),
    )(drb, w_out, proj, proj, ya, yb, *deps)


def _lower_bound(tab):
    return _sigmoid(tab[0:1, :] - tab[1:2, :])


FWD_BLOCK = 128
BWD_BLOCK = 128


def _block_scan(x, row, reverse, size, blk):
    s = 1
    while s < blk:
        if reverse:
            x = x + jnp.where(row < blk - s, pltpu.roll(x, size - s, 0), 0.0)
        else:
            x = x + jnp.where(row >= s, pltpu.roll(x, s, 0), 0.0)
        s *= 2
    return x


def _block_exps(l, reverse):
    blk = l.shape[0]
    half = blk // 2
    first = lax.broadcasted_iota(jnp.int32, (blk, HEAD), 0) < half
    q1, q3 = half // 2, half + half // 2
    if reverse:
        rho1, rho2, lh, ltot = l[q1:q1 + 1], l[q3:q3 + 1], l[half:half + 1], l[0:1]
    else:
        rho1, rho2, lh, ltot = l[q1 - 1:q1], l[q3 - 1:q3], l[half - 1:half], l[blk - 1:blk]
    ref = jnp.where(first, rho1, rho2)
    query_half = first if reverse else jnp.logical_not(first)
    e2 = jnp.where(query_half, jnp.exp(jnp.minimum(l - lh, 0.0)), 0.0)
    e1 = jnp.where(query_half, 0.0, jnp.exp(jnp.minimum(lh - l, 0.0)))
    return (jnp.exp(l - ref), jnp.exp(ref - l), e2, e1, jnp.exp(l), jnp.exp(ltot - l),
            jnp.exp(ltot), jnp.exp(lh), jnp.exp(ltot - lh))


def _half_mask(reverse, blk):
    r = lax.broadcasted_iota(jnp.int32, (blk, blk), 0)
    c = lax.broadcasted_iota(jnp.int32, (blk, blk), 1)
    same = (r < blk // 2) == (c < blk // 2)
    return same & ((c >= r) if reverse else (r >= c))


def _hgrn_fwd(proj, lbf, lbb, ng, b_, s_, d_, name):
    h_ = d_ // HEAD
    BLOCK = min(FWD_BLOCK, s_)
    nb = s_ // BLOCK

    def body(hq_ref, hff_ref, hfb_ref, hi_ref, hog_ref, lbf_ref, lbb_ref, ng_ref, ya_ref, o_ref,
             q_s, k_s, l_s, of_s, oi_s, qd_s, u_s, st_s, dec_s):
        row = lax.broadcasted_iota(jnp.int32, (s_, HEAD), 0) % BLOCK
        hq = hq_ref[...]
        q_s[...] = hq * _sigmoid(hq)
        for reverse, hf_ref, lb_ref in ((False, hff_ref, lbf_ref), (True, hfb_ref, lbb_ref)):
            lb = _lower_bound(lb_ref[...])
            f = lb + (1.0 - lb) * _sigmoid(hf_ref[...])
            k_s[...] = 1.0 - f
            l_s[...] = _block_scan(jnp.log(f), row, reverse, s_, BLOCK)
            mask = _half_mask(reverse, BLOCK)

            def inside(n, carry, reverse=reverse, mask=mask):
                sl = pl.ds(pl.multiple_of(n * BLOCK, BLOCK), BLOCK)
                eq, ek, e2, e1, el, ee, dec, _, _ = _block_exps(l_s[sl, :], reverse)
                qc, kc = q_s[sl, :], k_s[sl, :]
                vb = hi_ref[sl, :].astype(BF16)
                a = jnp.where(mask, _dot((qc * eq).astype(BF16), (kc * ek).astype(BF16), NT), 0.0)
                a = a + _dot((qc * e2).astype(BF16), (kc * e1).astype(BF16), NT)
                oi_s[sl, :] = _dot(a.astype(BF16), vb, NN)
                qd_s[sl, :] = (qc * el).astype(BF16)
                u_s[n] = _dot(vb, (kc * ee).astype(BF16), TN)
                dec_s[n] = jnp.broadcast_to(dec, (8, HEAD))
                return carry

            lax.fori_loop(0, nb, inside, 0, unroll=8)

            def carry_state(n, st, reverse=reverse):
                idx = (nb - 1 - n) if reverse else n
                st_s[idx] = st.astype(BF16)
                return st * dec_s[idx][0:1, :] + u_s[idx]

            lax.fori_loop(0, nb, carry_state, jnp.zeros((HEAD, HEAD), F32))

            def across(n, carry, reverse=reverse):
                sl = pl.ds(pl.multiple_of(n * BLOCK, BLOCK), BLOCK)
                o_dir = oi_s[sl, :] + _dot(qd_s[sl, :], st_s[n], NT)
                if not reverse:
                    of_s[sl, :] = o_dir
                else:
                    o = of_s[sl, :] + o_dir
                    o_ref[sl, :] = o
                    nrm = o * lax.rsqrt(jnp.mean(o * o, axis=-1, keepdims=True) + LN_EPS)
                    hog = hog_ref[sl, :]
                    ya_ref[sl, :] = (nrm * ng_ref[...] * hog * _sigmoid(hog)).astype(BF16)
                return carry

            lax.fori_loop(0, nb, across, 0, unroll=8)

    def col(part):
        return pl.BlockSpec((s_, HEAD), lambda h, b, part=part: (b, part * h_ + h))

    tab = pl.BlockSpec((2, HEAD), lambda h, b: (0, h))
    out = pl.BlockSpec((s_, HEAD), lambda h, b: (b, h))
    return pl.pallas_call(
        body, name=name, grid=(h_, b_),
        in_specs=[col(0), col(1), col(2), col(3), col(4), tab, tab,
                  pl.BlockSpec((1, HEAD), lambda h, b: (0, h))],
        out_specs=[out, out],
        out_shape=[jax.ShapeDtypeStruct((b_ * s_, d_), BF16), jax.ShapeDtypeStruct((b_ * s_, d_), F32)],
        scratch_shapes=[pltpu.VMEM((s_, HEAD), F32)] * 5 + [
            pltpu.VMEM((s_, HEAD), BF16), pltpu.VMEM((nb, HEAD, HEAD), F32), pltpu.VMEM((nb, HEAD, HEAD), BF16),
            pltpu.VMEM((nb, 8, HEAD), F32)],
        compiler_params=_params(),
    )(proj, proj, proj, proj, proj, lbf, lbb, ng)


def _hgrn_bwd(proj, lbf, lbb, ng, o_sum, dya, b_, s_, d_, name):
    h_ = d_ // HEAD
    BLOCK = min(BWD_BLOCK, s_)
    nb = s_ // BLOCK

    def body(hq_ref, hff_ref, hfb_ref, hi_ref, hog_ref, lbf_ref, lbb_ref, ng_ref, o_ref, dya_ref,
             dhq_ref, dhff_ref, dhfb_ref, dhi_ref, dhog_ref, dng_ref, dlbf_ref, dlbb_ref,
             q_s, k_s, l_s, do_s, dq_s, dv_s, dl_s, dk_s, u_s, w_s, st_s, dst_s, dec_s):
        b = pl.program_id(1)

        @pl.when(b == 0)
        def _():
            dng_ref[...] = jnp.zeros_like(dng_ref)
            dlbf_ref[...] = jnp.zeros_like(dlbf_ref)
            dlbb_ref[...] = jnp.zeros_like(dlbb_ref)

        row = lax.broadcasted_iota(jnp.int32, (s_, HEAD), 0) % BLOCK
        brow = lax.broadcasted_iota(jnp.int32, (BLOCK, HEAD), 0)
        hq = hq_ref[...]
        q_s[...] = hq * _sigmoid(hq)
        o = o_ref[...]
        rinv = lax.rsqrt(jnp.mean(o * o, axis=-1, keepdims=True) + LN_EPS)
        nrm = o * rinv
        hog = hog_ref[...]
        so = _sigmoid(hog)
        gain = ng_ref[...]
        dy = dya_ref[...]
        dhog_ref[...] = (dy * nrm * gain * _dsilu(hog, so)).astype(BF16)
        dng_ref[...] += jnp.sum(dy * nrm * hog * so, axis=0, keepdims=True)
        dn = dy * gain * hog * so
        do_s[...] = rinv * (dn - nrm * jnp.mean(dn * nrm, axis=-1, keepdims=True))

        for reverse, hf_ref, lb_ref, dhf_ref, dlb_ref in (
                (False, hff_ref, lbf_ref, dhff_ref, dlbf_ref), (True, hfb_ref, lbb_ref, dhfb_ref, dlbb_ref)):
            lb = _lower_bound(lb_ref[...])
            sf = _sigmoid(hf_ref[...])
            f = lb + (1.0 - lb) * sf
            k_s[...] = 1.0 - f
            l_s[...] = _block_scan(jnp.log(f), row, reverse, s_, BLOCK)
            mask = _half_mask(reverse, BLOCK)
            total_row = 0 if reverse else BLOCK - 1
            key_end = BLOCK // 2 if reverse else BLOCK // 2 - 1

            def prepare(n, carry, reverse=reverse):
                sl = pl.ds(pl.multiple_of(n * BLOCK, BLOCK), BLOCK)
                _, _, _, _, el, ee, dec, _, _ = _block_exps(l_s[sl, :], reverse)
                vb = hi_ref[sl, :].astype(BF16)
                u_s[n] = _dot(vb, (k_s[sl, :] * ee).astype(BF16), TN)
                w_s[n] = _dot(do_s[sl, :].astype(BF16), (q_s[sl, :] * el).astype(BF16), TN)
                dec_s[n] = jnp.broadcast_to(dec, (8, HEAD))
                return carry

            lax.fori_loop(0, nb, prepare, 0, unroll=8)

            def carry_state(n, st, reverse=reverse):
                idx = (nb - 1 - n) if reverse else n
                st_s[idx] = st
                return st * dec_s[idx][0:1, :] + u_s[idx]

            lax.fori_loop(0, nb, carry_state, jnp.zeros((HEAD, HEAD), F32))

            def carry_grad(n, dst, reverse=reverse):
                idx = n if reverse else (nb - 1 - n)
                dst_s[idx] = dst
                return dst * dec_s[idx][0:1, :] + w_s[idx]

            lax.fori_loop(0, nb, carry_grad, jnp.zeros((HEAD, HEAD), F32))

            def inside(n, carry, reverse=reverse, mask=mask, total_row=total_row, key_end=key_end):
                sl = pl.ds(pl.multiple_of(n * BLOCK, BLOCK), BLOCK)
                eq, ek, e2, e1, el, ee, dec, dec_key, dec_query = _block_exps(l_s[sl, :], reverse)
                qc, kc = q_s[sl, :], k_s[sl, :]
                vb = hi_ref[sl, :].astype(BF16)
                dob = do_s[sl, :].astype(BF16)
                qt, kt, q2, k1 = ((qc * eq).astype(BF16), (kc * ek).astype(BF16),
                                  (qc * e2).astype(BF16), (kc * e1).astype(BF16))
                kend = kc * ee
                st0, dst1 = st_s[n], dst_s[n]
                dstb = dst1.astype(BF16)
                a = jnp.where(mask, _dot(qt, kt, NT), 0.0) + _dot(q2, k1, NT)
                da = _dot(dob, vb, NT)
                dab = da.astype(BF16)
                dad = jnp.where(mask, da, 0.0).astype(BF16)
                dqt, dkt = _dot(dad, kt, NN), _dot(dad, qt, TN)
                dq2, dk1 = _dot(dab, k1, NN), _dot(dab, q2, TN)
                dqd = _dot(dob, st0.astype(BF16), NN)
                dke = _dot(vb, dstb, NN)
                dv = _dot(a.astype(BF16), dob, TN) + _dot(kend.astype(BF16), dstb, NT)
                dq = dqt * eq + dq2 * e2 + dqd * el
                dk = dkt * ek + dk1 * e1 + dke * ee
                dtot = jnp.sum(dke * kend, axis=0, keepdims=True) + jnp.sum(dst1 * st0, axis=0, keepdims=True) * dec
                st_mid = st0 * dec_key + _dot(vb, k1, TN)
                dst_mid = dst1 * dec_query + _dot(dob, q2, TN)
                dmid = jnp.sum(dst_mid * st_mid, axis=0, keepdims=True)
                dl_s[sl, :] = (qc * dq - kc * dk + jnp.where(brow == total_row, dtot, 0.0)
                               + jnp.where(brow == key_end, dmid, 0.0))
                dk_s[sl, :] = dk
                if not reverse:
                    dq_s[sl, :] = dq
                    dv_s[sl, :] = dv
                else:
                    hqc = hq_ref[sl, :]
                    dhq_ref[sl, :] = ((dq_s[sl, :] + dq) * _dsilu(hqc, _sigmoid(hqc))).astype(BF16)
                    dhi_ref[sl, :] = (dv_s[sl, :] + dv).astype(BF16)
                return carry

            lax.fori_loop(0, nb, inside, 0, unroll=4)
            dlogf = _block_scan(dl_s[...], row % (BLOCK // 2), not reverse, s_, BLOCK // 2)
            df = dlogf / f - dk_s[...]
            dhf_ref[...] = (df * (1.0 - lb) * sf * (1.0 - sf)).astype(BF16)
            dlb = jnp.sum(df * (1.0 - sf), axis=0, keepdims=True) * lb * (1.0 - lb)
            dlb_ref[0:1, :] += dlb
            dlb_ref[1:2, :] -= dlb

    def col(part):
        return pl.BlockSpec((s_, HEAD), lambda h, b, part=part: (b, part * h_ + h))

    tab = pl.BlockSpec((2, HEAD), lambda h, b: (0, h))
    vec = pl.BlockSpec((1, HEAD), lambda h, b: (0, h))
    blk = pl.BlockSpec((s_, HEAD), lambda h, b: (b, h))
    act = jax.ShapeDtypeStruct((b_ * s_, d_), BF16)
    state = pltpu.VMEM((nb, HEAD, HEAD), F32)
    return pl.pallas_call(
        body, name=name, grid=(h_, b_),
        in_specs=[col(0), col(1), col(2), col(3), col(4), tab, tab, vec, blk, blk],
        out_specs=[blk] * 5 + [vec, tab, tab],
        out_shape=[act] * 5 + [jax.ShapeDtypeStruct((1, d_), F32), jax.ShapeDtypeStruct((2, d_), F32),
                               jax.ShapeDtypeStruct((2, d_), F32)],
        scratch_shapes=[pltpu.VMEM((s_, HEAD), F32)] * 8 + [state] * 4 + [pltpu.VMEM((nb, 8, HEAD), F32)],
        compiler_params=_params(),
    )(proj, proj, proj, proj, proj, lbf, lbb, ng, o_sum, dya)


def _rope_tables(s_):
    half = ROPE_DIM // 2
    inv_freq = ROPE_THETA ** (-jnp.arange(0, ROPE_DIM, 2, dtype=F32) / ROPE_DIM)
    ang = jnp.arange(s_, dtype=F32)[:, None] * inv_freq
    cos, sin = jnp.cos(ang), jnp.sin(ang)
    zeros = jnp.zeros((s_, HEAD - ROPE_DIM), F32)
    zh = jnp.zeros((s_, half), F32)
    c = jnp.concatenate([cos, cos, jnp.ones((s_, HEAD - ROPE_DIM), F32)], axis=1)
    s1 = jnp.concatenate([-sin, zh, zeros], axis=1)
    s2 = jnp.concatenate([zh, sin, zeros], axis=1)
    return c, s1, s2


def _rope(t, c, s1, s2):
    half = ROPE_DIM // 2
    return t * c + pltpu.roll(t, HEAD - half, 1) * s1 + pltpu.roll(t, half, 1) * s2


def _rope_bwd(dt, c, s1, s2):
    half = ROPE_DIM // 2
    return dt * c + pltpu.roll(dt * s1, half, 1) + pltpu.roll(dt * s2, HEAD - half, 1)


def _window_mask(r0, qb, wk, seg):
    row = lax.broadcasted_iota(jnp.int32, (qb, wk), 0)
    col = lax.broadcasted_iota(jnp.int32, (qb, wk), 1)
    kj = r0 - ATTN_HALF + col
    return (col - row >= 0) & (col - row <= 2 * ATTN_HALF) & (kj >= 0) & (kj < seg)


UNROLLED_BLOCKS = 4


def _classes_in_flight(dil):
    return 2 if dil % 2 == 0 else 1


def _block_start(i, qb):
    return i * qb if isinstance(i, int) else pl.multiple_of(i * qb, qb)


def _for_blocks(nq, step):
    if nq <= UNROLLED_BLOCKS:
        for i in range(nq):
            step(i, 0)
    else:
        lax.fori_loop(0, nq, step, 0, unroll=UNROLLED_BLOCKS)


def _attn_fwd(proj, tabs, b_, s_, col0, dil, name):
    seg = s_ // dil
    qb = min(128, seg)
    nq, wk = seg // qb, qb + 2 * ATTN_HALF
    scale = HEAD ** -0.5

    par = _classes_in_flight(dil)

    def body(q_ref, k_ref, v_ref, c_ref, s1_ref, s2_ref, o_ref, lse_ref, q_all, k_all, v_all):
        k_all[...] = jnp.zeros_like(k_all)
        v_all[...] = jnp.zeros_like(v_all)

        def residue(r, q_s, k_s, v_s):
            cls = pl.ds(r, seg, stride=dil)
            c, s1, s2 = c_ref[cls, :], s1_ref[cls, :], s2_ref[cls, :]
            q_s[...] = _rope(q_ref[cls, :], c, s1, s2).astype(BF16)
            k_s[ATTN_HALF:ATTN_HALF + seg, :] = _rope(k_ref[cls, :], c, s1, s2).astype(BF16)
            v_s[ATTN_HALF:ATTN_HALF + seg, :] = v_ref[cls, :].astype(BF16)

            def step(i, carry):
                r0 = _block_start(i, qb)
                sc = _dot(q_s[pl.ds(r0, qb), :], k_s[pl.ds(r0, wk), :], NT) * scale
                sc = jnp.where(_window_mask(r0, qb, wk, seg), sc, NEG_INF)
                m = jnp.max(sc, axis=-1, keepdims=True)
                p = jnp.exp(sc - m)
                den = jnp.sum(p, axis=-1, keepdims=True)
                rows = pl.ds(r + r0 * dil, qb, stride=dil)
                o_ref[rows, :] = _dot(p.astype(BF16), v_s[pl.ds(r0, wk), :], NN) / den
                lse_ref[rows, :] = jnp.broadcast_to(m + jnp.log(den), (qb, HEAD))
                return carry

            _for_blocks(nq, step)

        def classes(j, carry):
            for slot in range(par):
                residue(j * par + slot, q_all.at[slot], k_all.at[slot], v_all.at[slot])
            return carry

        lax.fori_loop(0, dil // par, classes, 0)

    def col(part):
        return pl.BlockSpec((s_, HEAD), lambda b, h, part=part: (b, col0 + part * ATTN_HEADS + h))

    tab = pl.BlockSpec((s_, HEAD), lambda b, h: (0, 0))
    out = pl.BlockSpec((s_, HEAD), lambda b, h: (b, h))
    shape = jax.ShapeDtypeStruct((b_ * s_, ATTN_OUT), F32)
    return pl.pallas_call(
        body, name=name, grid=(b_, ATTN_HEADS),
        in_specs=[col(0), col(1), col(2), tab, tab, tab],
        out_specs=[out, out],
        out_shape=[shape, shape],
        scratch_shapes=[pltpu.VMEM((par, seg, HEAD), BF16), pltpu.VMEM((par, seg + 2 * ATTN_HALF, HEAD), BF16),
                        pltpu.VMEM((par, seg + 2 * ATTN_HALF, HEAD), BF16)],
        compiler_params=_params(),
    )(proj, proj, proj, *tabs)


def _attn_bwd(proj, tabs, dog, cg, lse, b_, s_, col0, dil, name):
    seg = s_ // dil
    qb = min(128, seg)
    nq, wk = seg // qb, qb + 2 * ATTN_HALF
    scale = HEAD ** -0.5

    par = _classes_in_flight(dil)

    def body(q_ref, k_ref, v_ref, c_ref, s1_ref, s2_ref, do_ref, cg_ref, lse_ref, dq_ref, dk_ref, dv_ref,
             q_all, k_all, v_all, do_all, cg_all, lse_all, dk_all, dv_all):
        k_all[...] = jnp.zeros_like(k_all)
        v_all[...] = jnp.zeros_like(v_all)

        def residue(r, q_s, k_s, v_s, do_s, cg_s, lse_s, dk_s, dv_s):
            cls = pl.ds(r, seg, stride=dil)
            c, s1, s2 = c_ref[cls, :], s1_ref[cls, :], s2_ref[cls, :]
            q_s[...] = _rope(q_ref[cls, :], c, s1, s2).astype(BF16)
            k_s[ATTN_HALF:ATTN_HALF + seg, :] = _rope(k_ref[cls, :], c, s1, s2).astype(BF16)
            v_s[ATTN_HALF:ATTN_HALF + seg, :] = v_ref[cls, :].astype(BF16)
            do_s[...] = do_ref[cls, :].astype(BF16)
            cg_s[...] = cg_ref[cls, :]
            lse_s[...] = lse_ref[cls, :]
            dk_s[...] = jnp.zeros_like(dk_s)
            dv_s[...] = jnp.zeros_like(dv_s)

            def step(i, carry):
                r0 = _block_start(i, qb)
                rows, win = pl.ds(r0, qb), pl.ds(r0, wk)
                qc, kw, vw = q_s[rows, :], k_s[win, :], v_s[win, :]
                sc = _dot(qc, kw, NT) * scale
                p = jnp.where(_window_mask(r0, qb, wk, seg), jnp.exp(sc - lse_s[rows, 0:1]), 0.0)
                dob = do_s[rows, :]
                dp = _dot(dob, vw, NT)
                ds = (p * (dp + cg_s[rows, 0:1]) * scale).astype(BF16)
                out = pl.ds(r + r0 * dil, qb, stride=dil)
                dq_ref[out, :] = _rope_bwd(_dot(ds, kw, NN), c_ref[out, :], s1_ref[out, :], s2_ref[out, :])
                dk_s[win, :] += _dot(ds, qc, TN)
                dv_s[win, :] += _dot(p.astype(BF16), dob, TN)
                return carry

            _for_blocks(nq, step)
            dk_ref[cls, :] = _rope_bwd(dk_s[ATTN_HALF:ATTN_HALF + seg, :], c, s1, s2)
            dv_ref[cls, :] = dv_s[ATTN_HALF:ATTN_HALF + seg, :]

        scratch = (q_all, k_all, v_all, do_all, cg_all, lse_all, dk_all, dv_all)

        def classes(j, carry):
            for slot in range(par):
                residue(j * par + slot, *[s.at[slot] for s in scratch])
            return carry

        lax.fori_loop(0, dil // par, classes, 0)

    def col(part):
        return pl.BlockSpec((s_, HEAD), lambda b, h, part=part: (b, col0 + part * ATTN_HEADS + h))

    tab = pl.BlockSpec((s_, HEAD), lambda b, h: (0, 0))
    out = pl.BlockSpec((s_, HEAD), lambda b, h: (b, h))
    shape = jax.ShapeDtypeStruct((b_ * s_, ATTN_OUT), F32)
    pad = seg + 2 * ATTN_HALF
    return pl.pallas_call(
        body, name=name, grid=(b_, ATTN_HEADS),
        in_specs=[col(0), col(1), col(2), tab, tab, tab, out, out, out],
        out_specs=[out, out, out],
        out_shape=[shape, shape, shape],
        scratch_shapes=[pltpu.VMEM((par, seg, HEAD), BF16), pltpu.VMEM((par, pad, HEAD), BF16),
                        pltpu.VMEM((par, pad, HEAD), BF16), pltpu.VMEM((par, seg, HEAD), BF16),
                        pltpu.VMEM((par, seg, HEAD), F32), pltpu.VMEM((par, seg, HEAD), F32),
                        pltpu.VMEM((par, pad, HEAD), F32), pltpu.VMEM((par, pad, HEAD), F32)],
        compiler_params=_params(),
    )(proj, proj, proj, *tabs, dog, cg, lse)


def _group_weights(lses):
    m = jnp.maximum(jnp.maximum(lses[0], lses[1]), lses[2])
    es = [jnp.exp(l - m) for l in lses]
    den = es[0] + es[1] + es[2]
    return [e / den for e in es]


def _combine_fwd(outs, lses, name):
    t_, w_ = outs[0].shape
    tm = _tile(t_, 512, 8)
    ng = len(outs)

    def body(*refs):
        ws = _group_weights([r[...] for r in refs[ng:2 * ng]])
        acc = ws[0] * refs[0][...]
        for g in range(1, ng):
            acc = acc + ws[g] * refs[g][...]
        refs[2 * ng][...] = acc.astype(BF16)

    row = pl.BlockSpec((tm, w_), lambda i: (i, 0))
    return pl.pallas_call(
        body, name=name, grid=(t_ // tm,), in_specs=[row] * (2 * ng), out_specs=row,
        out_shape=jax.ShapeDtypeStruct((t_, w_), BF16), compiler_params=_params(),
    )(*outs, *lses)


def _combine_bwd(dob, outs, lses, name):
    t_, w_ = outs[0].shape
    tm = _tile(t_, 512, 8)
    ng = len(outs)

    def body(*refs):
        do = refs[0][...]
        os_ = [r[...] for r in refs[1:1 + ng]]
        ws = _group_weights([r[...] for r in refs[1 + ng:1 + 2 * ng]])
        o = ws[0] * os_[0]
        for g in range(1, ng):
            o = o + ws[g] * os_[g]
        prod = do * o
        heads = [jnp.broadcast_to(jnp.sum(prod[:, h * HEAD:(h + 1) * HEAD], axis=-1, keepdims=True), (tm, HEAD))
                 for h in range(w_ // HEAD)]
        tot = jnp.concatenate(heads, axis=1)
        for g in range(ng):
            refs[1 + 2 * ng + g][...] = ws[g] * do
            refs[1 + 3 * ng + g][...] = -ws[g] * tot

    row = pl.BlockSpec((tm, w_), lambda i: (i, 0))
    shape = jax.ShapeDtypeStruct((t_, w_), F32)
    res = pl.pallas_call(
        body, name=name, grid=(t_ // tm,), in_specs=[row] * (1 + 2 * ng), out_specs=[row] * (2 * ng),
        out_shape=[shape] * (2 * ng), compiler_params=_params(),
    )(dob, *outs, *lses)
    return res[:ng], res[ng:]


def _adam_update(w, g, m, v):
    m = ADAM_B1 * m + (1.0 - ADAM_B1) * g
    v = ADAM_B2 * v + (1.0 - ADAM_B2) * (g * g)
    m_hat = m / (1.0 - ADAM_B1 ** ADAM_STEP)
    v_hat = v / (1.0 - ADAM_B2 ** ADAM_STEP)
    return -ADAM_LR * (m_hat / (jnp.sqrt(v_hat) + ADAM_EPS) + ADAM_WD * w), m, v


def _adam(w, g, m, v, name):
    r_, c_ = w.shape
    tr = _tile(r_, 256, 8)

    def body(w_ref, g_ref, m_ref, v_ref, d_ref, mo_ref, vo_ref):
        d_ref[...], mo_ref[...], vo_ref[...] = _adam_update(w_ref[...], g_ref[...], m_ref[...], v_ref[...])

    blk = pl.BlockSpec((tr, c_), lambda i: (i, 0))
    shape = jax.ShapeDtypeStruct((r_, c_), F32)
    return pl.pallas_call(
        body, name=name, grid=(r_ // tr,), in_specs=[blk] * 4, out_specs=[blk] * 3,
        out_shape=[shape] * 3, compiler_params=_params(),
    )(w, g, m, v)


def _sum_partials(recv, name):
    n_, r_, c_ = recv.shape
    tr = _tile(r_, 128, 16)

    def body(p_ref, o_ref):
        acc = p_ref[0].astype(F32)
        for i in range(1, n_):
            acc = acc + p_ref[i].astype(F32)
        o_ref[...] = acc

    return pl.pallas_call(
        body, name=name, grid=(r_ // tr,),
        in_specs=[pl.BlockSpec((n_, tr, c_), lambda i: (0, i, 0))],
        out_specs=pl.BlockSpec((tr, c_), lambda i: (i, 0)),
        out_shape=jax.ShapeDtypeStruct((r_, c_), F32), compiler_params=_params(),
    )(recv)


def _small_sum_adam(parts, w, m, v, name):
    n_, r_, c_ = parts.shape

    def body(p_ref, w_ref, m_ref, v_ref, g_ref, d_ref, mo_ref, vo_ref):
        g = p_ref[0]
        for i in range(1, n_):
            g = g + p_ref[i]
        g_ref[...] = g
        d_ref[...], mo_ref[...], vo_ref[...] = _adam_update(w_ref[...], g, m_ref[...], v_ref[...])

    shape = jax.ShapeDtypeStruct((r_, c_), F32)
    return pl.pallas_call(body, name=name, out_shape=[shape] * 4, compiler_params=_params())(parts, w, m, v)


def _my_place():
    x, y, c = lax.axis_index("x"), lax.axis_index("y"), lax.axis_index("c")
    return x, y, c


def _peer(x, y, c, d):
    px = 1 - x if d & 4 else x
    py = 1 - y if d & 2 else y
    pc = 1 - c if d & 1 else c
    return (px, py, pc), 4 * px + 2 * py + pc


HBM_SPEC =pl.BlockSpec(memory_space=pltpu.HBM)
SEM_SPEC = pl.BlockSpec(memory_space=pltpu.SEMAPHORE)
EFFECT = pltpu.SideEffectType.DATAFLOW_SIDE_EFFECTING


def _in_hbm(a):
    return pltpu.with_memory_space_constraint(a, pltpu.HBM)


def _token_shape():
    return jax.ShapeDtypeStruct((8, HEAD), F32)


SIBLING = 1
OTHER_CHIPS = (4, 2, 6)


def _rows_of(ref, num, rows):
    return ref.at[pl.ds(pl.multiple_of(num * rows, 16), rows), :]


def _gather_start(shards, name):
    nw = len(shards)
    lands = [lax.empty((N_DEV * s.shape[0], s.shape[1]), s.dtype) for s in shards]
    n_to = 1 + len(OTHER_CHIPS)

    def body(*refs):
        ins, lnd = refs[:nw], refs[nw:2 * nw]
        send, from_sib, from_chips, own = (refs[(2 + i) * nw:(3 + i) * nw] for i in range(4))
        token = refs[8 * nw]
        x, y, c = _my_place()
        me = 4 * x + 2 * y + c
        for k in range(nw):
            mine = _rows_of(lnd[k], me, shards[k].shape[0])
            pltpu.make_async_copy(ins[k], mine, own[k]).start()
            for i, d in enumerate((SIBLING,) + OTHER_CHIPS):
                place, _ = _peer(x, y, c, d)
                pltpu.make_async_remote_copy(
                    src_ref=ins[k], dst_ref=mine, send_sem=send[k].at[i],
                    recv_sem=from_sib[k] if i == 0 else from_chips[k].at[i - 1],
                    device_id=place, device_id_type=MESH).start()
        token[...] = jnp.zeros_like(token)

    dma = pltpu.SemaphoreType.DMA
    sems = [dma((n_to,))] * nw + [dma(())] * nw + [dma((len(OTHER_CHIPS),))] * nw + [dma(())] * nw
    thru = [pltpu.HBM(a.shape, a.dtype) for a in list(shards) + lands]
    res = pl.pallas_call(
        body, name=name, out_shape=(*sems, *thru, _token_shape()),
        in_specs=[HBM_SPEC] * (2 * nw),
        out_specs=(*([SEM_SPEC] * (4 * nw)), *([HBM_SPEC] * (2 * nw)), pl.BlockSpec(memory_space=pltpu.VMEM)),
        input_output_aliases={i: 4 * nw + i for i in range(2 * nw)},
        compiler_params=pltpu.CompilerParams(has_side_effects=EFFECT),
    )(*[_in_hbm(s) for s in shards], *[_in_hbm(l) for l in lands])
    return [dict(send=res[k], from_sib=res[nw + k], from_chips=res[2 * nw + k], own=res[3 * nw + k],
                 src=res[4 * nw + k], land=res[5 * nw + k]) for k in range(nw)], res[6 * nw]


def _gather_forward(pending, after, name):
    rows = pending["src"].shape[0]
    n_fw = len(OTHER_CHIPS)

    def body(land_ref, from_chips, after_ref, fw_send, fw_recv, land_thru):
        x, y, c = _my_place()
        sibling, _ = _peer(x, y, c, SIBLING)
        for j, d in enumerate(OTHER_CHIPS):
            _, num = _peer(x, y, c, d)
            block = _rows_of(land_ref, num, rows)
            pltpu.make_async_remote_copy(
                src_ref=block, dst_ref=block, send_sem=fw_send.at[j], recv_sem=from_chips.at[j],
                device_id=sibling, device_id_type=MESH).wait_recv()
            pltpu.make_async_remote_copy(
                src_ref=block, dst_ref=block, send_sem=fw_send.at[j], recv_sem=fw_recv.at[j],
                device_id=sibling, device_id_type=MESH).start()

    land = pending["land"]
    dma = pltpu.SemaphoreType.DMA
    fw_send, fw_recv, land = pl.pallas_call(
        body, name=name, out_shape=(dma((n_fw,)), dma((n_fw,)), pltpu.HBM(land.shape, land.dtype)),
        in_specs=(HBM_SPEC, SEM_SPEC, pl.BlockSpec(memory_space=pl.ANY)),
        out_specs=(SEM_SPEC, SEM_SPEC, HBM_SPEC), input_output_aliases={0: 2},
        compiler_params=pltpu.CompilerParams(has_side_effects=EFFECT),
    )(land, pending["from_chips"], after)
    return dict(pending, land=land, fw_send=fw_send, fw_recv=fw_recv)


def _gather_wait(pending, name):
    rows = pending["src"].shape[0]

    def body(src_ref, land_ref, send, from_sib, own, fw_send, fw_recv, src_dead, got):
        x, y, c = _my_place()
        me = 4 * x + 2 * y + c
        sibling, sib_num = _peer(x, y, c, SIBLING)
        mine = _rows_of(land_ref, me, rows)
        pltpu.make_async_copy(src_ref, mine, own).wait()
        for i in range(1 + len(OTHER_CHIPS)):
            pltpu.make_async_remote_copy(
                src_ref=src_ref, dst_ref=mine, send_sem=send.at[i], recv_sem=from_sib,
                device_id=sibling, device_id_type=MESH).wait_send()
        theirs = _rows_of(land_ref, sib_num, rows)
        pltpu.make_async_remote_copy(
            src_ref=src_ref, dst_ref=theirs, send_sem=send.at[0], recv_sem=from_sib,
            device_id=sibling, device_id_type=MESH).wait_recv()
        for j, d in enumerate(OTHER_CHIPS):
            _, num = _peer(x, y, c, d)
            sent = _rows_of(land_ref, num, rows)
            _, got_num = _peer(x, y, c, d | SIBLING)
            arrived = _rows_of(land_ref, got_num, rows)
            cp = pltpu.make_async_remote_copy(
                src_ref=sent, dst_ref=arrived, send_sem=fw_send.at[j], recv_sem=fw_recv.at[j],
                device_id=sibling, device_id_type=MESH)
            cp.wait_send()
            cp.wait_recv()

    src, land = pending["src"], pending["land"]
    return pl.pallas_call(
        body, name=name, out_shape=(pltpu.HBM(src.shape, src.dtype), pltpu.HBM(land.shape, land.dtype)),
        in_specs=(HBM_SPEC, HBM_SPEC) + (SEM_SPEC,) * 5,
        out_specs=(HBM_SPEC, HBM_SPEC), input_output_aliases={0: 0, 1: 1},
        compiler_params=pltpu.CompilerParams(has_side_effects=EFFECT),
    )(src, land, pending["send"], pending["from_sib"], pending["own"], pending["fw_send"], pending["fw_recv"])[1]


def _gather_end(pending, after, n):
    return _gather_wait(_gather_forward(pending, after, f"gather_forward_{n}"), f"gather_wait_{n}")


def _scatter_start(full, name):
    rows, cols = full.shape[0] // N_DEV, full.shape[1]
    land = lax.empty((N_DEV, rows, cols), full.dtype)

    def body(full_ref, land_ref, send, recv, own, full_thru, land_thru, token):
        x, y, c = _my_place()
        me = 4 * x + 2 * y + c
        slab = land_ref.at[me]
        pltpu.make_async_copy(full_ref.at[pl.ds(pl.multiple_of(me * rows, 16), rows), :], slab, own).start()
        for d in range(1, N_DEV):
            place, num = _peer(x, y, c, d)
            pltpu.make_async_remote_copy(
                src_ref=full_ref.at[pl.ds(pl.multiple_of(num * rows, 16), rows), :], dst_ref=slab,
                send_sem=send.at[d - 1], recv_sem=recv.at[d - 1], device_id=place, device_id_type=MESH).start()
        token[...] = jnp.zeros_like(token)

    res = pl.pallas_call(
        body, name=name,
        out_shape=(pltpu.SemaphoreType.DMA((N_DEV - 1,)), pltpu.SemaphoreType.DMA((N_DEV - 1,)),
                   pltpu.SemaphoreType.DMA(()),
                   pltpu.HBM(full.shape, full.dtype), pltpu.HBM(land.shape, land.dtype), _token_shape()),
        in_specs=(HBM_SPEC, HBM_SPEC),
        out_specs=(SEM_SPEC, SEM_SPEC, SEM_SPEC, HBM_SPEC, HBM_SPEC, pl.BlockSpec(memory_space=pltpu.VMEM)),
        input_output_aliases={0: 3, 1: 4},
        compiler_params=pltpu.CompilerParams(has_side_effects=EFFECT),
    )(_in_hbm(full), _in_hbm(land))
    return dict(send=res[0], recv=res[1], own=res[2], src=res[3], land=res[4]), res[5]


def _scatter_wait(pending, after, name):
    rows = pending["land"].shape[1]

    def body(src_ref, land_ref, send, recv, own, after_ref, src_dead, got):
        x, y, c = _my_place()
        me = 4 * x + 2 * y + c
        pltpu.make_async_copy(src_ref.at[pl.ds(pl.multiple_of(me * rows, 16), rows), :], land_ref.at[me], own).wait()
        for d in range(1, N_DEV):
            place, num = _peer(x, y, c, d)
            cp = pltpu.make_async_remote_copy(
                src_ref=src_ref.at[pl.ds(pl.multiple_of(num * rows, 16), rows), :], dst_ref=land_ref.at[me],
                send_sem=send.at[d - 1], recv_sem=recv.at[d - 1], device_id=place, device_id_type=MESH)
            cp.wait_send()
            cp.wait_recv()

    src, land = pending["src"], pending["land"]
    return pl.pallas_call(
        body, name=name, out_shape=(pltpu.HBM(src.shape, src.dtype), pltpu.HBM(land.shape, land.dtype)),
        in_specs=(HBM_SPEC, HBM_SPEC, SEM_SPEC, SEM_SPEC, SEM_SPEC, pl.BlockSpec(memory_space=pl.ANY)),
        out_specs=(HBM_SPEC, HBM_SPEC), input_output_aliases={0: 0, 1: 1},
        compiler_params=pltpu.CompilerParams(has_side_effects=EFFECT),
    )(src, land, pending["send"], pending["recv"], pending["own"], after)[1]


BIG = ("ffn1_w_in", "ffn1_w_out", "mix_w_in", "w_branch_a", "w_branch_b", "mix_w_out", "ffn2_w_in", "ffn2_w_out")
TRANSPOSED = ("ffn1_w_in", "mix_w_in", "w_branch_b", "ffn2_w_in")
SMALL = ("ln1_g", "ln1_b", "ln2_g", "ln2_b", "ln3_g", "ln3_b", "hgrn_norm_g", "hgrn_lb_fwd", "hgrn_lb_bwd")
SMALL_ROWS = 16


def _local_step(x, target, weight, emit, emit_small, sp):
    b_, s_, d_ = x.shape
    t_ = b_ * s_
    x2, tgt = x.reshape(t_, d_), target.reshape(t_, d_)
    xb = x2.astype(BF16)
    w1i = weight("ffn1_w_in", xb)
    g1, u1, a1 = _ffn_in_fwd(xb, w1i, "ffn1_in")
    w1o = weight("ffn1_w_out", a1)
    r1, h1, h1b = _mm_res_ln_fwd(a1, w1o, x2, sp["ln1_g"], sp["ln1_b"], 0.5, "ffn1_out_ln1")
    wmx = weight("mix_w_in", h1b)
    proj = _mm_nt(h1b, wmx, F32, "mix_in")
    ya_in, o_sum = _hgrn_fwd(proj, sp["hgrn_lb_fwd"], sp["hgrn_lb_bwd"], sp["hgrn_norm_g"], b_, s_, d_, "hgrn_fwd")
    tabs = _rope_tables(s_)
    outs, lses = [], []
    group_col = [(5 * d_ + gi * QKV_GROUP) // HEAD for gi in range(len(ATTN_GROUPS))]
    for gi, (_, dil) in enumerate(ATTN_GROUPS):
        o_g, lse_g = _attn_fwd(proj, tabs, b_, s_, group_col[gi], dil, f"attn_fwd_{gi}")
        outs.append(o_g)
        lses.append(lse_g)
    ob = _combine_fwd(outs, lses, "attn_combine")
    wa, wb = weight("w_branch_a", ya_in), weight("w_branch_b", ob)
    ya, yb, z = _gate_out_fwd(ya_in, ob, wa, wb, proj, d_, "branch_gate")
    wo = weight("mix_w_out", z)
    r2, h2, h2b = _mm_res_ln_fwd(z, wo, h1, sp["ln2_g"], sp["ln2_b"], 1.0, "mix_out_ln2")
    w2i = weight("ffn2_w_in", h2b)
    g2, u2, a2 = _ffn_in_fwd(h2b, w2i, "ffn2_in")
    w2o = weight("ffn2_w_out", a2)
    dr3, dr3b, dg3, db3, loss = _mm_res_loss_bwd(a2, w2o, h2, tgt, sp["ln3_g"], sp["ln3_b"], 0.5, "ffn2_out_loss")
    dep = emit("ffn2_w_out", _mm_tn(a2, dr3b, 0.5, "d_ffn2_w_out"))
    dgate2, dup2 = _ffn_mid_bwd(dr3b, w2o, g2, u2, 0.5, "ffn2_mid_bwd", dep)
    du2 = (dgate2, dup2)
    dep = emit("ffn2_w_in", _mm_tn(du2, h2b, 1.0, "d_ffn2_w_in"))
    dr2, dr2b, dg2, db2 = _mm_nn_res_lnbwd(du2, w2i, dr3, r2, sp["ln2_g"], "ffn2_in_bwd_ln2", dep)
    dep = emit("mix_w_out", _mm_tn(z, dr2b, 1.0, "d_mix_w_out"))
    dya, dyb, dga, dgb = _dz_gate_bwd(dr2b, wo, proj, ya, yb, d_, "branch_gate_bwd", dep)
    dep = emit("w_branch_a", _mm_tn(ya_in, dya, 1.0, "d_w_branch_a"))
    dya_in = _mm_nt(dya, wa, F32, "branch_a_bwd", dep)
    dep = emit("w_branch_b", _mm_tn(dyb, ob, 1.0, "d_w_branch_b"))
    dob = _mm_nn(dyb, wb, F32, "branch_b_bwd", dep)
    dhq, dhff, dhfb, dhi, dhog, dng, dlbf, dlbb = _hgrn_bwd(
        proj, sp["hgrn_lb_fwd"], sp["hgrn_lb_bwd"], sp["hgrn_norm_g"], o_sum, dya_in, b_, s_, d_, "hgrn_bwd")
    dogs, cgs = _combine_bwd(dob, outs, lses, "attn_combine_bwd")
    dqkv = []
    for gi, (_, dil) in enumerate(ATTN_GROUPS):
        dqkv += _attn_bwd(proj, tabs, dogs[gi], cgs[gi], lses[gi], b_, s_, group_col[gi], dil, f"attn_bwd_{gi}")
    dproj = jnp.concatenate([dhq, dhff, dhfb, dhi, dhog] + [t.astype(BF16) for t in dqkv] + [dga, dgb], axis=1)
    dep = emit("mix_w_in", _mm_tn(dproj, h1b, 1.0, "d_mix_w_in"))
    dr1, dr1b, dg1, db1 = _mm_nn_res_lnbwd(dproj, wmx, dr2, r1, sp["ln1_g"], "mix_in_bwd_ln1", dep)
    dep_small = emit_small({"ln1_g": dg1, "ln1_b": db1, "ln2_g": dg2, "ln2_b": db2, "ln3_g": dg3, "ln3_b": db3,
                            "hgrn_norm_g": dng, "hgrn_lb_fwd": dlbf, "hgrn_lb_bwd": dlbb})
    dep = emit("ffn1_w_out", _mm_tn(a1, dr1b, 0.5, "d_ffn1_w_out")) + dep_small
    dgate1, dup1 = _ffn_mid_bwd(dr1b, w1o, g1, u1, 0.5, "ffn1_mid_bwd", dep)
    du1 = (dgate1, dup1)
    dep = emit("ffn1_w_in", _mm_tn(du1, xb, 1.0, "d_ffn1_w_in"))
    grad_x = _mm_nn_res(du1, w1i, dr1, "ffn1_in_bwd", dep)
    return loss, grad_x.reshape(b_, s_, d_)


def _pack_small(vals):
    rows = jnp.concatenate([vals[n] for n in SMALL], axis=0)
    return jnp.pad(rows, ((0, SMALL_ROWS - rows.shape[0]), (0, 0)))


def _unpack_small(packed):
    out, r = {}, 0
    for n in SMALL:
        k = 2 if n.startswith("hgrn_lb") else 1
        out[n] = packed[r:r + k]
        r += k
    return out


def kernel(x, ffn1_w_in, ffn1_w_out, ln1_g, ln1_b, mix_w_in, hgrn_lb_fwd, hgrn_lb_bwd, hgrn_norm_g, w_branch_a, w_branch_b, mix_w_out, ln2_g, ln2_b, ffn2_w_in, ffn2_w_out, ln3_g, ln3_b, loss_target, m_ffn1_w_in, m_ffn1_w_out, m_ln1_g, m_ln1_b, m_mix_w_in, m_hgrn_lb_fwd, m_hgrn_lb_bwd, m_hgrn_norm_g, m_w_branch_a, m_w_branch_b, m_mix_w_out, m_ln2_g, m_ln2_b, m_ffn2_w_in, m_ffn2_w_out, m_ln3_g, m_ln3_b, v_ffn1_w_in, v_ffn1_w_out, v_ln1_g, v_ln1_b, v_mix_w_in, v_hgrn_lb_fwd, v_hgrn_lb_bwd, v_hgrn_norm_g, v_w_branch_a, v_w_branch_b, v_mix_w_out, v_ln2_g, v_ln2_b, v_ffn2_w_in, v_ffn2_w_out, v_ln3_g, v_ln3_b):
    args = dict(locals())
    big_w = {n: args[n][0] for n in BIG}
    sp = {n: args[n] for n in SMALL}
    def rows_bf16(n, zero=0.0):
        w = big_w[n] + zero
        return (w.T if n in TRANSPOSED else w).astype(BF16)

    first, rest = BIG[:2], BIG[2:]
    pending, token = _gather_start([rows_bf16(n) for n in first], "gather_start_ffn1")
    gathering = dict(zip(first, pending))
    pending, all_started = _gather_start([rows_bf16(n, token[0, 0]) for n in rest], "gather_start_rest")
    gathering.update(zip(rest, pending))
    scattering = {}

    def weight(n, after):
        if n == first[0]:
            after = all_started
        return _gather_end(gathering[n], after, n)

    def emit(n, grad):
        scattering[n], token = _scatter_start(grad, f"scatter_start_{n}")
        return token

    def emit_small(grads):
        pending, token = _gather_start([_pack_small(grads)], "gather_start_small")
        scattering["small"] = pending[0]
        return token

    loss_part, grad_x = _local_step(x, loss_target, weight, emit, emit_small, sp)
    loss = lax.psum(loss_part[0, 0], ("x", "y", "c"))
    out_g, out_d, out_m, out_v = {}, {}, {}, {}
    done = grad_x
    for n in ("ffn2_w_out", "ffn2_w_in", "mix_w_out", "w_branch_a", "w_branch_b", "mix_w_in", "small",
              "ffn1_w_out", "ffn1_w_in"):
        if n == "small":
            parts = _gather_end(scattering[n], done, n)
            res = _small_sum_adam(parts.reshape(N_DEV, SMALL_ROWS, parts.shape[1]), _pack_small(sp),
                                  _pack_small({n: args["m_" + n] for n in SMALL}),
                                  _pack_small({n: args["v_" + n] for n in SMALL}), "small_adam")
            sg, sd, sm, sv = (_unpack_small(r) for r in res)
            out_g.update(sg), out_d.update(sd), out_m.update(sm), out_v.update(sv)
            done = res[3]
            continue
        g = _sum_partials(_scatter_wait(scattering[n], done, f"scatter_wait_{n}"), f"sum_{n}")
        if n in TRANSPOSED:
            g = g.T
        d_w, m_w, v_w = _adam(big_w[n], g, args["m_" + n][0], args["v_" + n][0], f"adam_{n}")
        out_g[n], out_d[n], out_m[n], out_v[n] = g[None], d_w[None], m_w[None], v_w[None]
        done = v_w
    order = ("ffn1_w_in", "ffn1_w_out", "ln1_g", "ln1_b", "mix_w_in", "hgrn_lb_fwd", "hgrn_lb_bwd", "hgrn_norm_g",
             "w_branch_a", "w_branch_b", "mix_w_out", "ln2_g", "ln2_b", "ffn2_w_in", "ffn2_w_out", "ln3_g", "ln3_b")
    return (loss, grad_x, *[out_g[n] for n in order], *[out_d[n] for n in order],
            *[out_m[n] for n in order], *[out_v[n] for n in order])
```

```python
import jax
import jax.numpy as jnp
from jax import lax
from jax.experimental import pallas as pl
from jax.experimental.pallas import tpu as pltpu

F32 = jnp.float32
BF16 = jnp.bfloat16

N_DEV = 8
HEAD = 128
ATTN_GROUPS = ((128, 1), (512, 4), (2048, 16))
ATTN_HEADS = 4
ATTN_HALF = 64
QKV_GROUP = 3 * ATTN_HEADS * HEAD
QKV_WIDTH = len(ATTN_GROUPS) * QKV_GROUP
ATTN_OUT = ATTN_HEADS * HEAD
ROPE_THETA = 500000.0
ROPE_DIM = HEAD // 4
ALPHA = 2.0 ** 0.25
LN_EPS = 1e-5
NEG_INF = -1e30
ADAM_LR, ADAM_B1, ADAM_B2, ADAM_EPS, ADAM_WD, ADAM_STEP = 0.001, 0.9, 0.999, 1e-08, 0.01, 10
VMEM_LIMIT = 56 * 1024 * 1024

NT = (((1,), (1,)), ((), ()))
NN = (((1,), (0,)), ((), ()))
TN = (((0,), (0,)), ((), ()))
MESH = pl.DeviceIdType.MESH


def _dot(a, b, dims):
    return lax.dot_general(a, b, dims, preferred_element_type=F32)


def _tile(n, pref, mult=128):
    if n <= pref:
        return n
    t = (pref // mult) * mult
    while t >= mult:
        if n % t == 0:
            return t
        t -= mult
    return n


def _tile_multi(ns, pref, mult=128):
    t = (pref // mult) * mult
    while t >= mult:
        if all(n % t == 0 for n in ns):
            return t
        t -= mult
    raise ValueError(f"no common tile for {ns}")


def _params(**kw):
    return pltpu.CompilerParams(vmem_limit_bytes=VMEM_LIMIT, **kw)


def _after(body, n_in, dep):
    if dep is None:
        return body, [], []

    def wrapped(*refs):
        body(*refs[:n_in], *refs[n_in + 1:])

    return wrapped, [pl.BlockSpec(dep.shape, lambda *_: (0,) * dep.ndim)], [dep]


def _pieces(a):
    pieces = tuple(a) if isinstance(a, (tuple, list)) else (a,)
    assert all(p.shape == pieces[0].shape for p in pieces)
    return pieces, pieces[0].shape[0], pieces[0].shape[1], len(pieces)


def _for_piece(step, p, per, npc, fn):
    if npc == 1:
        fn()
    else:
        pl.when((step >= p * per) & (step < (p + 1) * per))(fn)


def _sigmoid(x):
    return jax.nn.sigmoid(x)


def _dsilu(x, s):
    return s * (1.0 + x * (1.0 - s))


def _ln_stats(r):
    mu = jnp.mean(r, axis=-1, keepdims=True)
    xc = r - mu
    var = jnp.mean(xc * xc, axis=-1, keepdims=True)
    rstd = lax.rsqrt(var + LN_EPS)
    return xc * rstd, rstd


def _ln_bwd(dy, xhat, rstd, g):
    dyg = dy * g
    m1 = jnp.mean(dyg, axis=-1, keepdims=True)
    m2 = jnp.mean(dyg * xhat, axis=-1, keepdims=True)
    return rstd * (dyg - m1 - xhat * m2)


ROW_TILE = 1024
SUB_ROWS = 256


def _once(shape, index_map):
    return pl.BlockSpec(shape, index_map, pipeline_mode=pl.Buffered(1))


def _for_row_blocks(tm, fn):
    sub = SUB_ROWS if tm % SUB_ROWS == 0 else tm

    def step(s, carry):
        fn(pl.ds(pl.multiple_of(s * sub, sub), sub))
        return carry

    lax.fori_loop(0, tm // sub, step, 0)


def _row_runs(tm):
    sub = SUB_ROWS if tm % SUB_ROWS == 0 else tm
    return [slice(s, s + sub) for s in range(0, tm, sub)]


def _ffn_in_fwd(xb, w_t, name):
    t_, d_ = xb.shape
    f_ = w_t.shape[0] // 2
    tm, tn = _tile(t_, ROW_TILE, 8), _tile(f_, 512)
    nj = f_ // tn

    def body(x_ref, wg_ref, wu_ref, g_ref, u_ref, a_ref):
        wg, wu = wg_ref[...], wu_ref[...]
        for rows in _row_runs(tm):
            x = x_ref[rows, :]
            g = _dot(x, wg, NT)
            u = _dot(x, wu, NT)
            g_ref[rows, :] = g.astype(BF16)
            u_ref[rows, :] = u.astype(BF16)
            a_ref[rows, :] = (g * _sigmoid(g) * u).astype(BF16)

    return pl.pallas_call(
        body, name=name, grid=(t_ // tm, nj),
        in_specs=[pl.BlockSpec((tm, d_), lambda i, j: (i, 0)),
                  pl.BlockSpec((tn, d_), lambda i, j: (j, 0)),
                  pl.BlockSpec((tn, d_), lambda i, j: (j + nj, 0))],
        out_specs=[pl.BlockSpec((tm, tn), lambda i, j: (i, j))] * 3,
        out_shape=[jax.ShapeDtypeStruct((t_, f_), BF16)] * 3,
        compiler_params=_params(),
    )(xb, w_t, w_t)


WHOLE_WEIGHT_BYTES = 24 * 1024 * 1024


def _k_plan(t_, k_, d_):
    if k_ * d_ * 2 > WHOLE_WEIGHT_BYTES:
        return False, _tile(t_, ROW_TILE, 8), _tile(k_, 512), _once
    if k_ <= 2048:
        return True, _tile(t_, ROW_TILE, 8), k_, _once
    return True, _tile(t_, SUB_ROWS, 8), k_, pl.BlockSpec


def _mm_res_ln_fwd(a, w, res, g, b, scale, name):
    t_, k_ = a.shape
    d_ = w.shape[1]
    whole, tm, tk, row_spec = _k_plan(t_, k_, d_)
    nk = k_ // tk

    def body(a_ref, w_ref, res_ref, g_ref, b_ref, r_ref, h_ref, hb_ref, *scratch):
        k = pl.program_id(1)
        if not whole:
            acc, = scratch

            @pl.when(k == 0)
            def _():
                acc[...] = jnp.zeros_like(acc)

            acc[...] += _dot(a_ref[...], w_ref[...], NN)

        @pl.when(k == nk - 1)
        def _():
            def rows_out(rows):
                prod = _dot(a_ref[rows, :], w_ref[...], NN) if whole else acc[rows, :]
                r = ALPHA * res_ref[rows, :] + scale * prod
                xhat, _ = _ln_stats(r)
                h = xhat * g_ref[...] + b_ref[...]
                r_ref[rows, :] = r
                h_ref[rows, :] = h
                hb_ref[rows, :] = h.astype(BF16)

            _for_row_blocks(tm, rows_out)

    row = row_spec((tm, d_), lambda i, k: (i, 0))
    vec = pl.BlockSpec((1, d_), lambda i, k: (0, 0))
    w_spec = _once((tk, d_), lambda i, k: (0, 0)) if whole else pl.BlockSpec((tk, d_), lambda i, k: (k, 0))
    return pl.pallas_call(
        body, name=name, grid=(t_ // tm, nk),
        in_specs=[pl.BlockSpec((tm, tk), lambda i, k: (i, k)), w_spec, row, vec, vec],
        out_specs=[row, row, row],
        out_shape=[jax.ShapeDtypeStruct((t_, d_), F32), jax.ShapeDtypeStruct((t_, d_), F32),
                   jax.ShapeDtypeStruct((t_, d_), BF16)],
        scratch_shapes=[] if whole else [pltpu.VMEM((tm, d_), F32)],
        compiler_params=_params(),
    )(a, w, res, g, b)


def _mm_res_loss_bwd(a, w, res, target, g, b, scale, name):
    t_, k_ = a.shape
    d_ = w.shape[1]
    whole, tm, tk, row_spec = _k_plan(t_, k_, d_)
    nk = k_ // tk

    def body(a_ref, w_ref, res_ref, t_ref, g_ref, b_ref, dr_ref, drb_ref, dg_ref, db_ref, loss_ref, *scratch):
        i, k = pl.program_id(0), pl.program_id(1)

        @pl.when((i == 0) & (k == 0))
        def _():
            dg_ref[...] = jnp.zeros_like(dg_ref)
            db_ref[...] = jnp.zeros_like(db_ref)
            loss_ref[...] = jnp.zeros_like(loss_ref)

        if not whole:
            acc, = scratch

            @pl.when(k == 0)
            def _():
                acc[...] = jnp.zeros_like(acc)

            acc[...] += _dot(a_ref[...], w_ref[...], NN)

        @pl.when(k == nk - 1)
        def _():
            def rows_out(rows):
                prod = _dot(a_ref[rows, :], w_ref[...], NN) if whole else acc[rows, :]
                r = ALPHA * res_ref[rows, :] + scale * prod
                xhat, rstd = _ln_stats(r)
                gain = g_ref[...]
                err = xhat * gain + b_ref[...] - t_ref[rows, :]
                loss_ref[...] += (0.5 / d_) * jnp.sum(err * err)
                dy = err * (1.0 / d_)
                dr = _ln_bwd(dy, xhat, rstd, gain)
                dr_ref[rows, :] = dr
                drb_ref[rows, :] = dr.astype(BF16)
                dg_ref[...] += jnp.sum(dy * xhat, axis=0, keepdims=True)
                db_ref[...] += jnp.sum(dy, axis=0, keepdims=True)

            _for_row_blocks(tm, rows_out)

    row = row_spec((tm, d_), lambda i, k: (i, 0))
    vec = pl.BlockSpec((1, d_), lambda i, k: (0, 0))
    w_spec = _once((tk, d_), lambda i, k: (0, 0)) if whole else pl.BlockSpec((tk, d_), lambda i, k: (k, 0))
    return pl.pallas_call(
        body, name=name, grid=(t_ // tm, nk),
        in_specs=[pl.BlockSpec((tm, tk), lambda i, k: (i, k)), w_spec, row, row, vec, vec],
        out_specs=[row, row, vec, vec, pl.BlockSpec((1, HEAD), lambda i, k: (0, 0))],
        out_shape=[jax.ShapeDtypeStruct((t_, d_), F32), jax.ShapeDtypeStruct((t_, d_), BF16),
                   jax.ShapeDtypeStruct((1, d_), F32), jax.ShapeDtypeStruct((1, d_), F32),
                   jax.ShapeDtypeStruct((1, HEAD), F32)],
        scratch_shapes=[] if whole else [pltpu.VMEM((tm, d_), F32)],
        compiler_params=_params(),
    )(a, w, res, target, g, b)


def _mm_nt(a, w_t, out_dtype, name, dep=None):
    t_, k_ = a.shape
    n_ = w_t.shape[0]
    tm, tn = _tile(t_, ROW_TILE, 8), _tile(n_, 512)

    def body(a_ref, w_ref, o_ref):
        o_ref[...] = _dot(a_ref[...], w_ref[...], NT).astype(out_dtype)

    body, dep_specs, deps = _after(body, 2, dep)
    return pl.pallas_call(
        body, name=name, grid=(t_ // tm, n_ // tn),
        in_specs=[pl.BlockSpec((tm, k_), lambda i, j: (i, 0)),
                  pl.BlockSpec((tn, k_), lambda i, j: (j, 0)), *dep_specs],
        out_specs=pl.BlockSpec((tm, tn), lambda i, j: (i, j)),
        out_shape=jax.ShapeDtypeStruct((t_, n_), out_dtype),
        compiler_params=_params(),
    )(a, w_t, *deps)


def _mm_nn(a, w, out_dtype, name, dep=None):
    t_, k_ = a.shape
    n_ = w.shape[1]
    tm, tn = _tile(t_, ROW_TILE, 8), _tile(n_, 512)

    def body(a_ref, w_ref, o_ref):
        o_ref[...] = _dot(a_ref[...], w_ref[...], NN).astype(out_dtype)

    body, dep_specs, deps = _after(body, 2, dep)
    return pl.pallas_call(
        body, name=name, grid=(t_ // tm, n_ // tn),
        in_specs=[pl.BlockSpec((tm, k_), lambda i, j: (i, 0)),
                  pl.BlockSpec((k_, tn), lambda i, j: (0, j)), *dep_specs],
        out_specs=pl.BlockSpec((tm, tn), lambda i, j: (i, j)),
        out_shape=jax.ShapeDtypeStruct((t_, n_), out_dtype),
        compiler_params=_params(),
    )(a, w, *deps)


def _mm_tn(a, b, scale, name):
    pieces, t_, mp, npc = _pieces(a)
    n_ = b.shape[1]
    tm = _tile(mp, 512)
    per = mp // tm

    def body(*refs):
        b_ref, o_ref = refs[npc], refs[npc + 1]
        for p in range(npc):
            def piece_out(p=p):
                o_ref[...] = (scale * _dot(refs[p][...], b_ref[...], TN)).astype(BF16)

            _for_piece(pl.program_id(0), p, per, npc, piece_out)

    return pl.pallas_call(
        body, name=name, grid=(npc * per,),
        in_specs=[pl.BlockSpec((t_, tm), lambda i, p=p: (0, jnp.clip(i - p * per, 0, per - 1))) for p in range(npc)]
        + [_once((t_, n_), lambda i: (0, 0))],
        out_specs=pl.BlockSpec((tm, n_), lambda i: (i, 0)),
        out_shape=jax.ShapeDtypeStruct((npc * mp, n_), BF16),
        compiler_params=_params(),
    )(*pieces, b)


def _gate_out_fwd(ya_in, ob, wa, wb_t, proj, d_, name):
    t_ = ya_in.shape[0]
    goff = 5 * d_ + QKV_WIDTH
    tm, tn = _tile(t_, ROW_TILE, 8), _tile_multi([d_, goff], 512)
    ja, jb = goff // tn, (goff + d_) // tn

    def body(ya_ref, ob_ref, wa_ref, wb_ref, ga_ref, gb_ref, yao_ref, ybo_ref, z_ref):
        wa, wb = wa_ref[...], wb_ref[...]
        for rows in _row_runs(tm):
            y_a = _dot(ya_ref[rows, :], wa, NN)
            y_b = _dot(ob_ref[rows, :], wb, NT)
            yao_ref[rows, :] = y_a.astype(BF16)
            ybo_ref[rows, :] = y_b.astype(BF16)
            z_ref[rows, :] = (_sigmoid(ga_ref[rows, :]) * y_a + _sigmoid(gb_ref[rows, :]) * y_b).astype(BF16)

    tile = pl.BlockSpec((tm, tn), lambda i, j: (i, j))
    return pl.pallas_call(
        body, name=name, grid=(t_ // tm, d_ // tn),
        in_specs=[pl.BlockSpec((tm, d_), lambda i, j: (i, 0)),
                  pl.BlockSpec((tm, ATTN_OUT), lambda i, j: (i, 0)),
                  pl.BlockSpec((d_, tn), lambda i, j: (0, j)),
                  pl.BlockSpec((tn, ATTN_OUT), lambda i, j: (j, 0)),
                  pl.BlockSpec((tm, tn), lambda i, j: (i, ja + j)),
                  pl.BlockSpec((tm, tn), lambda i, j: (i, jb + j))],
        out_specs=[tile, tile, tile],
        out_shape=[jax.ShapeDtypeStruct((t_, d_), BF16)] * 3,
        compiler_params=_params(),
    )(ya_in, ob, wa, wb_t, proj, proj)


def _ffn_mid_bwd(drb, w_out, gate, up, scale, name, dep=None):
    t_, d_ = drb.shape
    f_ = w_out.shape[0]
    tm, tn = _tile(t_, ROW_TILE, 8), _tile(f_, 512)

    def body(dr_ref, w_ref, g_ref, u_ref, dg_ref, du_ref):
        w = w_ref[...]
        for rows in _row_runs(tm):
            da = scale * _dot(dr_ref[rows, :], w, NT)
            g = g_ref[rows, :].astype(F32)
            s = _sigmoid(g)
            dg_ref[rows, :] = (da * u_ref[rows, :].astype(F32) * _dsilu(g, s)).astype(BF16)
            du_ref[rows, :] = (da * g * s).astype(BF16)

    body, dep_specs, deps = _after(body, 4, dep)
    tile = pl.BlockSpec((tm, tn), lambda i, j: (i, j))
    return pl.pallas_call(
        body, name=name, grid=(t_ // tm, f_ // tn),
        in_specs=[pl.BlockSpec((tm, d_), lambda i, j: (i, 0)),
                  pl.BlockSpec((tn, d_), lambda i, j: (j, 0)), tile, tile, *dep_specs],
        out_specs=[tile, tile],
        out_shape=[jax.ShapeDtypeStruct((t_, f_), BF16)] * 2,
        compiler_params=_params(),
    )(drb, w_out, gate, up, *deps)


def _mm_nn_res_lnbwd(a, w, dres, r, g, name, dep=None):
    pieces, t_, kp, npc = _pieces(a)
    d_ = w.shape[1]
    tm, tk = _tile(t_, ROW_TILE, 8), _tile(kp, 512)
    per = kp // tk
    nk = npc * per

    def body(*refs):
        w_ref, dres_ref, r_ref, g_ref, dr_ref, drb_ref, dg_ref, db_ref, acc = refs[npc:]
        i, k = pl.program_id(0), pl.program_id(1)

        @pl.when(k == 0)
        def _():
            acc[...] = jnp.zeros_like(acc)

        @pl.when((i == 0) & (k == 0))
        def _():
            dg_ref[...] = jnp.zeros_like(dg_ref)
            db_ref[...] = jnp.zeros_like(db_ref)

        for p in range(npc):
            def piece_in(p=p):
                acc[...] += _dot(refs[p][...], w_ref[...], NN)

            _for_piece(k, p, per, npc, piece_in)

        @pl.when(k == nk - 1)
        def _():
            def rows_out(rows):
                dy = acc[rows, :] + ALPHA * dres_ref[rows, :]
                xhat, rstd = _ln_stats(r_ref[rows, :])
                dr = _ln_bwd(dy, xhat, rstd, g_ref[...])
                dr_ref[rows, :] = dr
                drb_ref[rows, :] = dr.astype(BF16)
                dg_ref[...] += jnp.sum(dy * xhat, axis=0, keepdims=True)
                db_ref[...] += jnp.sum(dy, axis=0, keepdims=True)

            _for_row_blocks(tm, rows_out)

    body, dep_specs, deps = _after(body, npc + 4, dep)
    row = _once((tm, d_), lambda i, k: (i, 0))
    vec = pl.BlockSpec((1, d_), lambda i, k: (0, 0))
    return pl.pallas_call(
        body, name=name, grid=(t_ // tm, nk),
        in_specs=[pl.BlockSpec((tm, tk), lambda i, k, p=p: (i, jnp.clip(k - p * per, 0, per - 1))) for p in range(npc)]
        + [pl.BlockSpec((tk, d_), lambda i, k: (k, 0)), row, row, vec, *dep_specs],
        out_specs=[row, row, vec, vec],
        out_shape=[jax.ShapeDtypeStruct((t_, d_), F32), jax.ShapeDtypeStruct((t_, d_), BF16),
                   jax.ShapeDtypeStruct((1, d_), F32), jax.ShapeDtypeStruct((1, d_), F32)],
        scratch_shapes=[pltpu.VMEM((tm, d_), F32)],
        compiler_params=_params(),
    )(*pieces, w, dres, r, g, *deps)


def _mm_nn_res(a, w, dres, name, dep=None):
    pieces, t_, kp, npc = _pieces(a)
    d_ = w.shape[1]
    tm, tk = _tile(t_, ROW_TILE, 8), _tile(kp, 512)
    per = kp // tk
    nk = npc * per

    def body(*refs):
        w_ref, dres_ref, o_ref, acc = refs[npc:]
        k = pl.program_id(1)

        @pl.when(k == 0)
        def _():
            acc[...] = jnp.zeros_like(acc)

        for p in range(npc):
            def piece_in(p=p):
                acc[...] += _dot(refs[p][...], w_ref[...], NN)

            _for_piece(k, p, per, npc, piece_in)

        @pl.when(k == nk - 1)
        def _():
            def rows_out(rows):
                o_ref[rows, :] = acc[rows, :] + ALPHA * dres_ref[rows, :]

            _for_row_blocks(tm, rows_out)

    body, dep_specs, deps = _after(body, npc + 2, dep)
    row = _once((tm, d_), lambda i, k: (i, 0))
    return pl.pallas_call(
        body, name=name, grid=(t_ // tm, nk),
        in_specs=[pl.BlockSpec((tm, tk), lambda i, k, p=p: (i, jnp.clip(k - p * per, 0, per - 1))) for p in range(npc)]
        + [pl.BlockSpec((tk, d_), lambda i, k: (k, 0)), row, *dep_specs],
        out_specs=row,
        out_shape=jax.ShapeDtypeStruct((t_, d_), F32),
        scratch_shapes=[pltpu.VMEM((tm, d_), F32)],
        compiler_params=_params(),
    )(*pieces, w, dres, *deps)


def _dz_gate_bwd(drb, w_out, proj, ya, yb, d_, name, dep=None):
    t_ = drb.shape[0]
    goff = 5 * d_ + QKV_WIDTH
    tm, tn = _tile(t_, ROW_TILE, 8), _tile_multi([d_, goff], 512)
    ja, jb = goff // tn, (goff + d_) // tn

    def body(dr_ref, w_ref, ga_ref, gb_ref, ya_ref, yb_ref, dya_ref, dyb_ref, dga_ref, dgb_ref):
        w = w_ref[...]
        for rows in _row_runs(tm):
            dz = _dot(dr_ref[rows, :], w, NT)
            sa, sb = _sigmoid(ga_ref[rows, :]), _sigmoid(gb_ref[rows, :])
            dya_ref[rows, :] = (dz * sa).astype(BF16)
            dyb_ref[rows, :] = (dz * sb).astype(BF16)
            dga_ref[rows, :] = (dz * ya_ref[rows, :].astype(F32) * sa * (1.0 - sa)).astype(BF16)
            dgb_ref[rows, :] = (dz * yb_ref[rows, :].astype(F32) * sb * (1.0 - sb)).astype(BF16)

    body, dep_specs, deps = _after(body, 6, dep)
    tile = pl.BlockSpec((tm, tn), lambda i, j: (i, j))
    return pl.pallas_call(
        body, name=name, grid=(t_ // tm, d_ // tn),
        in_specs=[pl.BlockSpec((tm, d_), lambda i, j: (i, 0)),
                  pl.BlockSpec((tn, d_), lambda i, j: (j, 0)),
                  pl.BlockSpec((tm, tn), lambda i, j: (i, ja + j)),
                  pl.BlockSpec((tm, tn), lambda i, j: (i, jb + j)), tile, tile, *dep_specs],
        out_specs=[tile] * 4,
        out_shape=[jax.ShapeDtypeStruct((t_, d_), BF16)] * 4,
        compiler_params=_params(),
    )(drb, w_out, proj, proj, ya, yb, *deps)


def _lower_bound(tab):
    return _sigmoid(tab[0:1, :] - tab[1:2, :])


FWD_BLOCK = 128
BWD_BLOCK = 128


def _block_scan(x, row, reverse, size, blk):
    s = 1
    while s < blk:
        if reverse:
            x = x + jnp.where(row < blk - s, pltpu.roll(x, size - s, 0), 0.0)
        else:
            x = x + jnp.where(row >= s, pltpu.roll(x, s, 0), 0.0)
        s *= 2
    return x


def _block_exps(l, reverse):
    blk = l.shape[0]
    half = blk // 2
    first = lax.broadcasted_iota(jnp.int32, (blk, HEAD), 0) < half
    q1, q3 = half // 2, half + half // 2
    if reverse:
        rho1, rho2, lh, ltot = l[q1:q1 + 1], l[q3:q3 + 1], l[half:half + 1], l[0:1]
    else:
        rho1, rho2, lh, ltot = l[q1 - 1:q1], l[q3 - 1:q3], l[half - 1:half], l[blk - 1:blk]
    ref = jnp.where(first, rho1, rho2)
    query_half = first if reverse else jnp.logical_not(first)
    e2 = jnp.where(query_half, jnp.exp(jnp.minimum(l - lh, 0.0)), 0.0)
    e1 = jnp.where(query_half, 0.0, jnp.exp(jnp.minimum(lh - l, 0.0)))
    return (jnp.exp(l - ref), jnp.exp(ref - l), e2, e1, jnp.exp(l), jnp.exp(ltot - l),
            jnp.exp(ltot), jnp.exp(lh), jnp.exp(ltot - lh))


def _half_mask(reverse, blk):
    r = lax.broadcasted_iota(jnp.int32, (blk, blk), 0)
    c = lax.broadcasted_iota(jnp.int32, (blk, blk), 1)
    same = (r < blk // 2) == (c < blk // 2)
    return same & ((c >= r) if reverse else (r >= c))


def _hgrn_fwd(proj, lbf, lbb, ng, b_, s_, d_, name):
    h_ = d_ // HEAD
    BLOCK = min(FWD_BLOCK, s_)
    nb = s_ // BLOCK

    def body(hq_ref, hff_ref, hfb_ref, hi_ref, hog_ref, lbf_ref, lbb_ref, ng_ref, ya_ref, o_ref,
             q_s, k_s, l_s, of_s, oi_s, qd_s, u_s, st_s, dec_s):
        row = lax.broadcasted_iota(jnp.int32, (s_, HEAD), 0) % BLOCK
        hq = hq_ref[...]
        q_s[...] = hq * _sigmoid(hq)
        for reverse, hf_ref, lb_ref in ((False, hff_ref, lbf_ref), (True, hfb_ref, lbb_ref)):
            lb = _lower_bound(lb_ref[...])
            f = lb + (1.0 - lb) * _sigmoid(hf_ref[...])
            k_s[...] = 1.0 - f
            l_s[...] = _block_scan(jnp.log(f), row, reverse, s_, BLOCK)
            mask = _half_mask(reverse, BLOCK)

            def inside(n, carry, reverse=reverse, mask=mask):
                sl = pl.ds(pl.multiple_of(n * BLOCK, BLOCK), BLOCK)
                eq, ek, e2, e1, el, ee, dec, _, _ = _block_exps(l_s[sl, :], reverse)
                qc, kc = q_s[sl, :], k_s[sl, :]
                vb = hi_ref[sl, :].astype(BF16)
                a = jnp.where(mask, _dot((qc * eq).astype(BF16), (kc * ek).astype(BF16), NT), 0.0)
                a = a + _dot((qc * e2).astype(BF16), (kc * e1).astype(BF16), NT)
                oi_s[sl, :] = _dot(a.astype(BF16), vb, NN)
                qd_s[sl, :] = (qc * el).astype(BF16)
                u_s[n] = _dot(vb, (kc * ee).astype(BF16), TN)
                dec_s[n] = jnp.broadcast_to(dec, (8, HEAD))
                return carry

            lax.fori_loop(0, nb, inside, 0, unroll=8)

            def carry_state(n, st, reverse=reverse):
                idx = (nb - 1 - n) if reverse else n
                st_s[idx] = st.astype(BF16)
                return st * dec_s[idx][0:1, :] + u_s[idx]

            lax.fori_loop(0, nb, carry_state, jnp.zeros((HEAD, HEAD), F32))

            def across(n, carry, reverse=reverse):
                sl = pl.ds(pl.multiple_of(n * BLOCK, BLOCK), BLOCK)
                o_dir = oi_s[sl, :] + _dot(qd_s[sl, :], st_s[n], NT)
                if not reverse:
                    of_s[sl, :] = o_dir
                else:
                    o = of_s[sl, :] + o_dir
                    o_ref[sl, :] = o
                    nrm = o * lax.rsqrt(jnp.mean(o * o, axis=-1, keepdims=True) + LN_EPS)
                    hog = hog_ref[sl, :]
                    ya_ref[sl, :] = (nrm * ng_ref[...] * hog * _sigmoid(hog)).astype(BF16)
                return carry

            lax.fori_loop(0, nb, across, 0, unroll=8)

    def col(part):
        return pl.BlockSpec((s_, HEAD), lambda h, b, part=part: (b, part * h_ + h))

    tab = pl.BlockSpec((2, HEAD), lambda h, b: (0, h))
    out = pl.BlockSpec((s_, HEAD), lambda h, b: (b, h))
    return pl.pallas_call(
        body, name=name, grid=(h_, b_),
        in_specs=[col(0), col(1), col(2), col(3), col(4), tab, tab,
                  pl.BlockSpec((1, HEAD), lambda h, b: (0, h))],
        out_specs=[out, out],
        out_shape=[jax.ShapeDtypeStruct((b_ * s_, d_), BF16), jax.ShapeDtypeStruct((b_ * s_, d_), F32)],
        scratch_shapes=[pltpu.VMEM((s_, HEAD), F32)] * 5 + [
            pltpu.VMEM((s_, HEAD), BF16), pltpu.VMEM((nb, HEAD, HEAD), F32), pltpu.VMEM((nb, HEAD, HEAD), BF16),
            pltpu.VMEM((nb, 8, HEAD), F32)],
        compiler_params=_params(),
    )(proj, proj, proj, proj, proj, lbf, lbb, ng)


def _hgrn_bwd(proj, lbf, lbb, ng, o_sum, dya, b_, s_, d_, name):
    h_ = d_ // HEAD
    BLOCK = min(BWD_BLOCK, s_)
    nb = s_ // BLOCK

    def body(hq_ref, hff_ref, hfb_ref, hi_ref, hog_ref, lbf_ref, lbb_ref, ng_ref, o_ref, dya_ref,
             dhq_ref, dhff_ref, dhfb_ref, dhi_ref, dhog_ref, dng_ref, dlbf_ref, dlbb_ref,
             q_s, k_s, l_s, do_s, dq_s, dv_s, dl_s, dk_s, u_s, w_s, st_s, dst_s, dec_s):
        b = pl.program_id(1)

        @pl.when(b == 0)
        def _():
            dng_ref[...] = jnp.zeros_like(dng_ref)
            dlbf_ref[...] = jnp.zeros_like(dlbf_ref)
            dlbb_ref[...] = jnp.zeros_like(dlbb_ref)

        row = lax.broadcasted_iota(jnp.int32, (s_, HEAD), 0) % BLOCK
        brow = lax.broadcasted_iota(jnp.int32, (BLOCK, HEAD), 0)
        hq = hq_ref[...]
        q_s[...] = hq * _sigmoid(hq)
        o = o_ref[...]
        rinv = lax.rsqrt(jnp.mean(o * o, axis=-1, keepdims=True) + LN_EPS)
        nrm = o * rinv
        hog = hog_ref[...]
        so = _sigmoid(hog)
        gain = ng_ref[...]
        dy = dya_ref[...]
        dhog_ref[...] = (dy * nrm * gain * _dsilu(hog, so)).astype(BF16)
        dng_ref[...] += jnp.sum(dy * nrm * hog * so, axis=0, keepdims=True)
        dn = dy * gain * hog * so
        do_s[...] = rinv * (dn - nrm * jnp.mean(dn * nrm, axis=-1, keepdims=True))

        for reverse, hf_ref, lb_ref, dhf_ref, dlb_ref in (
                (False, hff_ref, lbf_ref, dhff_ref, dlbf_ref), (True, hfb_ref, lbb_ref, dhfb_ref, dlbb_ref)):
            lb = _lower_bound(lb_ref[...])
            sf = _sigmoid(hf_ref[...])
            f = lb + (1.0 - lb) * sf
            k_s[...] = 1.0 - f
            l_s[...] = _block_scan(jnp.log(f), row, reverse, s_, BLOCK)
            mask = _half_mask(reverse, BLOCK)
            total_row = 0 if reverse else BLOCK - 1
            key_end = BLOCK // 2 if reverse else BLOCK // 2 - 1

            def prepare(n, carry, reverse=reverse):
                sl = pl.ds(pl.multiple_of(n * BLOCK, BLOCK), BLOCK)
                _, _, _, _, el, ee, dec, _, _ = _block_exps(l_s[sl, :], reverse)
                vb = hi_ref[sl, :].astype(BF16)
                u_s[n] = _dot(vb, (k_s[sl, :] * ee).astype(BF16), TN)
                w_s[n] = _dot(do_s[sl, :].astype(BF16), (q_s[sl, :] * el).astype(BF16), TN)
                dec_s[n] = jnp.broadcast_to(dec, (8, HEAD))
                return carry

            lax.fori_loop(0, nb, prepare, 0, unroll=8)

            def carry_state(n, st, reverse=reverse):
                idx = (nb - 1 - n) if reverse else n
                st_s[idx] = st
                return st * dec_s[idx][0:1, :] + u_s[idx]

            lax.fori_loop(0, nb, carry_state, jnp.zeros((HEAD, HEAD), F32))

            def carry_grad(n, dst, reverse=reverse):
                idx = n if reverse else (nb - 1 - n)
                dst_s[idx] = dst
                return dst * dec_s[idx][0:1, :] + w_s[idx]

            lax.fori_loop(0, nb, carry_grad, jnp.zeros((HEAD, HEAD), F32))

            def inside(n, carry, reverse=reverse, mask=mask, total_row=total_row, key_end=key_end):
                sl = pl.ds(pl.multiple_of(n * BLOCK, BLOCK), BLOCK)
                eq, ek, e2, e1, el, ee, dec, dec_key, dec_query = _block_exps(l_s[sl, :], reverse)
                qc, kc = q_s[sl, :], k_s[sl, :]
                vb = hi_ref[sl, :].astype(BF16)
                dob = do_s[sl, :].astype(BF16)
                qt, kt, q2, k1 = ((qc * eq).astype(BF16), (kc * ek).astype(BF16),
                                  (qc * e2).astype(BF16), (kc * e1).astype(BF16))
                kend = kc * ee
                st0, dst1 = st_s[n], dst_s[n]
                dstb = dst1.astype(BF16)
                a = jnp.where(mask, _dot(qt, kt, NT), 0.0) + _dot(q2, k1, NT)
                da = _dot(dob, vb, NT)
                dab = da.astype(BF16)
                dad = jnp.where(mask, da, 0.0).astype(BF16)
                dqt, dkt = _dot(dad, kt, NN), _dot(dad, qt, TN)
                dq2, dk1 = _dot(dab, k1, NN), _dot(dab, q2, TN)
                dqd = _dot(dob, st0.astype(BF16), NN)
                dke = _dot(vb, dstb, NN)
                dv = _dot(a.astype(BF16), dob, TN) + _dot(kend.astype(BF16), dstb, NT)
                dq = dqt * eq + dq2 * e2 + dqd * el
                dk = dkt * ek + dk1 * e1 + dke * ee
                dtot = jnp.sum(dke * kend, axis=0, keepdims=True) + jnp.sum(dst1 * st0, axis=0, keepdims=True) * dec
                st_mid = st0 * dec_key + _dot(vb, k1, TN)
                dst_mid = dst1 * dec_query + _dot(dob, q2, TN)
                dmid = jnp.sum(dst_mid * st_mid, axis=0, keepdims=True)
                dl_s[sl, :] = (qc * dq - kc * dk + jnp.where(brow == total_row, dtot, 0.0)
                               + jnp.where(brow == key_end, dmid, 0.0))
                dk_s[sl, :] = dk
                if not reverse:
                    dq_s[sl, :] = dq
                    dv_s[sl, :] = dv
                else:
                    hqc = hq_ref[sl, :]
                    dhq_ref[sl, :] = ((dq_s[sl, :] + dq) * _dsilu(hqc, _sigmoid(hqc))).astype(BF16)
                    dhi_ref[sl, :] = (dv_s[sl, :] + dv).astype(BF16)
                return carry

            lax.fori_loop(0, nb, inside, 0, unroll=8)
            dlogf = _block_scan(dl_s[...], row % (BLOCK // 2), not reverse, s_, BLOCK // 2)
            df = dlogf / f - dk_s[...]
            dhf_ref[...] = (df * (1.0 - lb) * sf * (1.0 - sf)).astype(BF16)
            dlb = jnp.sum(df * (1.0 - sf), axis=0, keepdims=True) * lb * (1.0 - lb)
            dlb_ref[0:1, :] += dlb
            dlb_ref[1:2, :] -= dlb

    def col(part):
        return pl.BlockSpec((s_, HEAD), lambda h, b, part=part: (b, part * h_ + h))

    tab = pl.BlockSpec((2, HEAD), lambda h, b: (0, h))
    vec = pl.BlockSpec((1, HEAD), lambda h, b: (0, h))
    blk = pl.BlockSpec((s_, HEAD), lambda h, b: (b, h))
    act = jax.ShapeDtypeStruct((b_ * s_, d_), BF16)
    state = pltpu.VMEM((nb, HEAD, HEAD), F32)
    return pl.pallas_call(
        body, name=name, grid=(h_, b_),
        in_specs=[col(0), col(1), col(2), col(3), col(4), tab, tab, vec, blk, blk],
        out_specs=[blk] * 5 + [vec, tab, tab],
        out_shape=[act] * 5 + [jax.ShapeDtypeStruct((1, d_), F32), jax.ShapeDtypeStruct((2, d_), F32),
                               jax.ShapeDtypeStruct((2, d_), F32)],
        scratch_shapes=[pltpu.VMEM((s_, HEAD), F32)] * 8 + [state] * 4 + [pltpu.VMEM((nb, 8, HEAD), F32)],
        compiler_params=_params(),
    )(proj, proj, proj, proj, proj, lbf, lbb, ng, o_sum, dya)


def _rope_tables(s_):
    half = ROPE_DIM // 2
    inv_freq = ROPE_THETA ** (-jnp.arange(0, ROPE_DIM, 2, dtype=F32) / ROPE_DIM)
    ang = jnp.arange(s_, dtype=F32)[:, None] * inv_freq
    cos, sin = jnp.cos(ang), jnp.sin(ang)
    zeros = jnp.zeros((s_, HEAD - ROPE_DIM), F32)
    zh = jnp.zeros((s_, half), F32)
    c = jnp.concatenate([cos, cos, jnp.ones((s_, HEAD - ROPE_DIM), F32)], axis=1)
    s1 = jnp.concatenate([-sin, zh, zeros], axis=1)
    s2 = jnp.concatenate([zh, sin, zeros], axis=1)
    return c, s1, s2


def _rope(t, c, s1, s2):
    half = ROPE_DIM // 2
    return t * c + pltpu.roll(t, HEAD - half, 1) * s1 + pltpu.roll(t, half, 1) * s2


def _rope_bwd(dt, c, s1, s2):
    half = ROPE_DIM // 2
    return dt * c + pltpu.roll(dt * s1, half, 1) + pltpu.roll(dt * s2, HEAD - half, 1)


def _window_mask(r0, qb, wk, seg):
    row = lax.broadcasted_iota(jnp.int32, (qb, wk), 0)
    col = lax.broadcasted_iota(jnp.int32, (qb, wk), 1)
    kj = r0 - ATTN_HALF + col
    return (col - row >= 0) & (col - row <= 2 * ATTN_HALF) & (kj >= 0) & (kj < seg)


UNROLLED_BLOCKS = 4


def _classes_in_flight(dil, nq):
    par = 1
    while par < 4 and dil % (2 * par) == 0 and 2 * par * nq <= 8:
        par *= 2
    return par


def _block_start(i, qb):
    return i * qb if isinstance(i, int) else pl.multiple_of(i * qb, qb)


def _for_blocks(nq, step):
    if nq <= UNROLLED_BLOCKS:
        for i in range(nq):
            step(i, 0)
    else:
        lax.fori_loop(0, nq, step, 0, unroll=UNROLLED_BLOCKS)


def _attn_fwd(proj, tabs, b_, s_, col0, dil, name):
    seg = s_ // dil
    qb = min(128, seg)
    nq, wk = seg // qb, qb + 2 * ATTN_HALF
    scale = HEAD ** -0.5

    par = _classes_in_flight(dil, nq)

    def body(q_ref, k_ref, v_ref, c_ref, s1_ref, s2_ref, o_ref, lse_ref, q_all, k_all, v_all):
        k_all[...] = jnp.zeros_like(k_all)
        v_all[...] = jnp.zeros_like(v_all)

        def residue(r, q_s, k_s, v_s):
            cls = pl.ds(r, seg, stride=dil)
            c, s1, s2 = c_ref[cls, :], s1_ref[cls, :], s2_ref[cls, :]
            q_s[...] = _rope(q_ref[cls, :], c, s1, s2).astype(BF16)
            k_s[ATTN_HALF:ATTN_HALF + seg, :] = _rope(k_ref[cls, :], c, s1, s2).astype(BF16)
            v_s[ATTN_HALF:ATTN_HALF + seg, :] = v_ref[cls, :].astype(BF16)

            def step(i, carry):
                r0 = _block_start(i, qb)
                sc = _dot(q_s[pl.ds(r0, qb), :], k_s[pl.ds(r0, wk), :], NT) * scale
                sc = jnp.where(_window_mask(r0, qb, wk, seg), sc, NEG_INF)
                m = jnp.max(sc, axis=-1, keepdims=True)
                p = jnp.exp(sc - m)
                den = jnp.sum(p, axis=-1, keepdims=True)
                rows = pl.ds(r + r0 * dil, qb, stride=dil)
                o_ref[rows, :] = _dot(p.astype(BF16), v_s[pl.ds(r0, wk), :], NN) / den
                lse_ref[rows, :] = jnp.broadcast_to(m + jnp.log(den), (qb, HEAD))
                return carry

            _for_blocks(nq, step)

        def classes(j, carry):
            for slot in range(par):
                residue(j * par + slot, q_all.at[slot], k_all.at[slot], v_all.at[slot])
            return carry

        lax.fori_loop(0, dil // par, classes, 0)

    def col(part):
        return pl.BlockSpec((s_, HEAD), lambda b, h, part=part: (b, col0 + part * ATTN_HEADS + h))

    tab = pl.BlockSpec((s_, HEAD), lambda b, h: (0, 0))
    out = pl.BlockSpec((s_, HEAD), lambda b, h: (b, h))
    shape = jax.ShapeDtypeStruct((b_ * s_, ATTN_OUT), F32)
    return pl.pallas_call(
        body, name=name, grid=(b_, ATTN_HEADS),
        in_specs=[col(0), col(1), col(2), tab, tab, tab],
        out_specs=[out, out],
        out_shape=[shape, shape],
        scratch_shapes=[pltpu.VMEM((par, seg, HEAD), BF16), pltpu.VMEM((par, seg + 2 * ATTN_HALF, HEAD), BF16),
                        pltpu.VMEM((par, seg + 2 * ATTN_HALF, HEAD), BF16)],
        compiler_params=_params(),
    )(proj, proj, proj, *tabs)


def _attn_bwd(proj, tabs, dog, cg, lse, b_, s_, col0, dil, name):
    seg = s_ // dil
    qb = min(128, seg)
    nq, wk = seg // qb, qb + 2 * ATTN_HALF
    scale = HEAD ** -0.5

    par = _classes_in_flight(dil, nq)

    def body(q_ref, k_ref, v_ref, c_ref, s1_ref, s2_ref, do_ref, cg_ref, lse_ref, dq_ref, dk_ref, dv_ref,
             q_all, k_all, v_all, do_all, cg_all, lse_all, dk_all, dv_all):
        k_all[...] = jnp.zeros_like(k_all)
        v_all[...] = jnp.zeros_like(v_all)

        def residue(r, q_s, k_s, v_s, do_s, cg_s, lse_s, dk_s, dv_s):
            cls = pl.ds(r, seg, stride=dil)
            c, s1, s2 = c_ref[cls, :], s1_ref[cls, :], s2_ref[cls, :]
            q_s[...] = _rope(q_ref[cls, :], c, s1, s2).astype(BF16)
            k_s[ATTN_HALF:ATTN_HALF + seg, :] = _rope(k_ref[cls, :], c, s1, s2).astype(BF16)
            v_s[ATTN_HALF:ATTN_HALF + seg, :] = v_ref[cls, :].astype(BF16)
            do_s[...] = do_ref[cls, :].astype(BF16)
            cg_s[...] = cg_ref[cls, :]
            lse_s[...] = lse_ref[cls, :]
            dk_s[...] = jnp.zeros_like(dk_s)
            dv_s[...] = jnp.zeros_like(dv_s)

            def step(i, carry):
                r0 = _block_start(i, qb)
                rows, win = pl.ds(r0, qb), pl.ds(r0, wk)
                qc, kw, vw = q_s[rows, :], k_s[win, :], v_s[win, :]
                sc = _dot(qc, kw, NT) * scale
                p = jnp.where(_window_mask(r0, qb, wk, seg), jnp.exp(sc - lse_s[rows, 0:1]), 0.0)
                dob = do_s[rows, :]
                dp = _dot(dob, vw, NT)
                ds = (p * (dp + cg_s[rows, 0:1]) * scale).astype(BF16)
                out = pl.ds(r + r0 * dil, qb, stride=dil)
                dq_ref[out, :] = _rope_bwd(_dot(ds, kw, NN), c_ref[out, :], s1_ref[out, :], s2_ref[out, :])
                dk_s[win, :] += _dot(ds, qc, TN)
                dv_s[win, :] += _dot(p.astype(BF16), dob, TN)
                return carry

            _for_blocks(nq, step)
            dk_ref[cls, :] = _rope_bwd(dk_s[ATTN_HALF:ATTN_HALF + seg, :], c, s1, s2)
            dv_ref[cls, :] = dv_s[ATTN_HALF:ATTN_HALF + seg, :]

        scratch = (q_all, k_all, v_all, do_all, cg_all, lse_all, dk_all, dv_all)

        def classes(j, carry):
            for slot in range(par):
                residue(j * par + slot, *[s.at[slot] for s in scratch])
            return carry

        lax.fori_loop(0, dil // par, classes, 0)

    def col(part):
        return pl.BlockSpec((s_, HEAD), lambda b, h, part=part: (b, col0 + part * ATTN_HEADS + h))

    tab = pl.BlockSpec((s_, HEAD), lambda b, h: (0, 0))
    out = pl.BlockSpec((s_, HEAD), lambda b, h: (b, h))
    shape = jax.ShapeDtypeStruct((b_ * s_, ATTN_OUT), F32)
    pad = seg + 2 * ATTN_HALF
    return pl.pallas_call(
        body, name=name, grid=(b_, ATTN_HEADS),
        in_specs=[col(0), col(1), col(2), tab, tab, tab, out, out, out],
        out_specs=[out, out, out],
        out_shape=[shape, shape, shape],
        scratch_shapes=[pltpu.VMEM((par, seg, HEAD), BF16), pltpu.VMEM((par, pad, HEAD), BF16),
                        pltpu.VMEM((par, pad, HEAD), BF16), pltpu.VMEM((par, seg, HEAD), BF16),
                        pltpu.VMEM((par, seg, HEAD), F32), pltpu.VMEM((par, seg, HEAD), F32),
                        pltpu.VMEM((par, pad, HEAD), F32), pltpu.VMEM((par, pad, HEAD), F32)],
        compiler_params=_params(),
    )(proj, proj, proj, *tabs, dog, cg, lse)


def _group_weights(lses):
    m = jnp.maximum(jnp.maximum(lses[0], lses[1]), lses[2])
    es = [jnp.exp(l - m) for l in lses]
    den = es[0] + es[1] + es[2]
    return [e / den for e in es]


def _combine_fwd(outs, lses, name):
    t_, w_ = outs[0].shape
    tm = _tile(t_, 512, 8)
    ng = len(outs)

    def body(*refs):
        ws = _group_weights([r[...] for r in refs[ng:2 * ng]])
        acc = ws[0] * refs[0][...]
        for g in range(1, ng):
            acc = acc + ws[g] * refs[g][...]
        refs[2 * ng][...] = acc.astype(BF16)

    row = pl.BlockSpec((tm, w_), lambda i: (i, 0))
    return pl.pallas_call(
        body, name=name, grid=(t_ // tm,), in_specs=[row] * (2 * ng), out_specs=row,
        out_shape=jax.ShapeDtypeStruct((t_, w_), BF16), compiler_params=_params(),
    )(*outs, *lses)


def _combine_bwd(dob, outs, lses, name):
    t_, w_ = outs[0].shape
    tm = _tile(t_, 512, 8)
    ng = len(outs)

    def body(*refs):
        do = refs[0][...]
        os_ = [r[...] for r in refs[1:1 + ng]]
        ws = _group_weights([r[...] for r in refs[1 + ng:1 + 2 * ng]])
        o = ws[0] * os_[0]
        for g in range(1, ng):
            o = o + ws[g] * os_[g]
        prod = do * o
        heads = [jnp.broadcast_to(jnp.sum(prod[:, h * HEAD:(h + 1) * HEAD], axis=-1, keepdims=True), (tm, HEAD))
                 for h in range(w_ // HEAD)]
        tot = jnp.concatenate(heads, axis=1)
        for g in range(ng):
            refs[1 + 2 * ng + g][...] = ws[g] * do
            refs[1 + 3 * ng + g][...] = -ws[g] * tot

    row = pl.BlockSpec((tm, w_), lambda i: (i, 0))
    shape = jax.ShapeDtypeStruct((t_, w_), F32)
    res = pl.pallas_call(
        body, name=name, grid=(t_ // tm,), in_specs=[row] * (1 + 2 * ng), out_specs=[row] * (2 * ng),
        out_shape=[shape] * (2 * ng), compiler_params=_params(),
    )(dob, *outs, *lses)
    return res[:ng], res[ng:]


def _adam_update(w, g, m, v):
    m = ADAM_B1 * m + (1.0 - ADAM_B1) * g
    v = ADAM_B2 * v + (1.0 - ADAM_B2) * (g * g)
    m_hat = m / (1.0 - ADAM_B1 ** ADAM_STEP)
    v_hat = v / (1.0 - ADAM_B2 ** ADAM_STEP)
    return -ADAM_LR * (m_hat / (jnp.sqrt(v_hat) + ADAM_EPS) + ADAM_WD * w), m, v


def _adam(w, g, m, v, name):
    r_, c_ = w.shape
    tr = _tile(r_, 256, 8)

    def body(w_ref, g_ref, m_ref, v_ref, d_ref, mo_ref, vo_ref):
        d_ref[...], mo_ref[...], vo_ref[...] = _adam_update(w_ref[...], g_ref[...], m_ref[...], v_ref[...])

    blk = pl.BlockSpec((tr, c_), lambda i: (i, 0))
    shape = jax.ShapeDtypeStruct((r_, c_), F32)
    return pl.pallas_call(
        body, name=name, grid=(r_ // tr,), in_specs=[blk] * 4, out_specs=[blk] * 3,
        out_shape=[shape] * 3, compiler_params=_params(),
    )(w, g, m, v)


def _sum_partials(recv, name):
    n_, r_, c_ = recv.shape
    tr = _tile(r_, 128, 16)

    def body(p_ref, o_ref):
        acc = p_ref[0].astype(F32)
        for i in range(1, n_):
            acc = acc + p_ref[i].astype(F32)
        o_ref[...] = acc

    return pl.pallas_call(
        body, name=name, grid=(r_ // tr,),
        in_specs=[pl.BlockSpec((n_, tr, c_), lambda i: (0, i, 0))],
        out_specs=pl.BlockSpec((tr, c_), lambda i: (i, 0)),
        out_shape=jax.ShapeDtypeStruct((r_, c_), F32), compiler_params=_params(),
    )(recv)


def _small_sum_adam(parts, w, m, v, name):
    n_, r_, c_ = parts.shape

    def body(p_ref, w_ref, m_ref, v_ref, g_ref, d_ref, mo_ref, vo_ref):
        g = p_ref[0]
        for i in range(1, n_):
            g = g + p_ref[i]
        g_ref[...] = g
        d_ref[...], mo_ref[...], vo_ref[...] = _adam_update(w_ref[...], g, m_ref[...], v_ref[...])

    shape = jax.ShapeDtypeStruct((r_, c_), F32)
    return pl.pallas_call(body, name=name, out_shape=[shape] * 4, compiler_params=_params())(parts, w, m, v)


def _my_place():
    x, y, c = lax.axis_index("x"), lax.axis_index("y"), lax.axis_index("c")
    return x, y, c


def _peer(x, y, c, d):
    px = 1 - x if d & 4 else x
    py = 1 - y if d & 2 else y
    pc = 1 - c if d & 1 else c
    return (px, py, pc), 4 * px + 2 * py + pc


HBM_SPEC =pl.BlockSpec(memory_space=pltpu.HBM)
SEM_SPEC = pl.BlockSpec(memory_space=pltpu.SEMAPHORE)
EFFECT = pltpu.SideEffectType.DATAFLOW_SIDE_EFFECTING


def _in_hbm(a):
    return pltpu.with_memory_space_constraint(a, pltpu.HBM)


def _token_shape():
    return jax.ShapeDtypeStruct((8, HEAD), F32)


SIBLING = 1
OTHER_CHIPS = (4, 2, 6)


def _rows_of(ref, num, rows):
    return ref.at[pl.ds(pl.multiple_of(num * rows, 16), rows), :]


def _gather_start(shards, name):
    nw = len(shards)
    lands = [lax.empty((N_DEV * s.shape[0], s.shape[1]), s.dtype) for s in shards]
    n_to = 1 + len(OTHER_CHIPS)

    def body(*refs):
        ins, lnd = refs[:nw], refs[nw:2 * nw]
        send, from_sib, from_chips, own = (refs[(2 + i) * nw:(3 + i) * nw] for i in range(4))
        token = refs[8 * nw]
        x, y, c = _my_place()
        me = 4 * x + 2 * y + c
        for k in range(nw):
            mine = _rows_of(lnd[k], me, shards[k].shape[0])
            pltpu.make_async_copy(ins[k], mine, own[k]).start()
            for i, d in enumerate((SIBLING,) + OTHER_CHIPS):
                place, _ = _peer(x, y, c, d)
                pltpu.make_async_remote_copy(
                    src_ref=ins[k], dst_ref=mine, send_sem=send[k].at[i],
                    recv_sem=from_sib[k] if i == 0 else from_chips[k].at[i - 1],
                    device_id=place, device_id_type=MESH).start()
        token[...] = jnp.zeros_like(token)

    dma = pltpu.SemaphoreType.DMA
    sems = [dma((n_to,))] * nw + [dma(())] * nw + [dma((len(OTHER_CHIPS),))] * nw + [dma(())] * nw
    thru = [pltpu.HBM(a.shape, a.dtype) for a in list(shards) + lands]
    res = pl.pallas_call(
        body, name=name, out_shape=(*sems, *thru, _token_shape()),
        in_specs=[HBM_SPEC] * (2 * nw),
        out_specs=(*([SEM_SPEC] * (4 * nw)), *([HBM_SPEC] * (2 * nw)), pl.BlockSpec(memory_space=pltpu.VMEM)),
        input_output_aliases={i: 4 * nw + i for i in range(2 * nw)},
        compiler_params=pltpu.CompilerParams(has_side_effects=EFFECT),
    )(*[_in_hbm(s) for s in shards], *[_in_hbm(l) for l in lands])
    return [dict(send=res[k], from_sib=res[nw + k], from_chips=res[2 * nw + k], own=res[3 * nw + k],
                 src=res[4 * nw + k], land=res[5 * nw + k]) for k in range(nw)], res[6 * nw]


def _gather_forward(pending, after, name):
    rows = pending["src"].shape[0]
    n_fw = len(OTHER_CHIPS)

    def body(land_ref, from_chips, after_ref, fw_send, fw_recv, land_thru):
        x, y, c = _my_place()
        sibling, _ = _peer(x, y, c, SIBLING)
        for j, d in enumerate(OTHER_CHIPS):
            _, num = _peer(x, y, c, d)
            block = _rows_of(land_ref, num, rows)
            pltpu.make_async_remote_copy(
                src_ref=block, dst_ref=block, send_sem=fw_send.at[j], recv_sem=from_chips.at[j],
                device_id=sibling, device_id_type=MESH).wait_recv()
            pltpu.make_async_remote_copy(
                src_ref=block, dst_ref=block, send_sem=fw_send.at[j], recv_sem=fw_recv.at[j],
                device_id=sibling, device_id_type=MESH).start()

    land = pending["land"]
    dma = pltpu.SemaphoreType.DMA
    fw_send, fw_recv, land = pl.pallas_call(
        body, name=name, out_shape=(dma((n_fw,)), dma((n_fw,)), pltpu.HBM(land.shape, land.dtype)),
        in_specs=(HBM_SPEC, SEM_SPEC, pl.BlockSpec(memory_space=pl.ANY)),
        out_specs=(SEM_SPEC, SEM_SPEC, HBM_SPEC), input_output_aliases={0: 2},
        compiler_params=pltpu.CompilerParams(has_side_effects=EFFECT),
    )(land, pending["from_chips"], after)
    return dict(pending, land=land, fw_send=fw_send, fw_recv=fw_recv)


def _gather_wait(pending, name):
    rows = pending["src"].shape[0]

    def body(src_ref, land_ref, send, from_sib, own, fw_send, fw_recv, src_dead, got):
        x, y, c = _my_place()
        me = 4 * x + 2 * y + c
        sibling, sib_num = _peer(x, y, c, SIBLING)
        mine = _rows_of(land_ref, me, rows)
        pltpu.make_async_copy(src_ref, mine, own).wait()
        for i in range(1 + len(OTHER_CHIPS)):
            pltpu.make_async_remote_copy(
                src_ref=src_ref, dst_ref=mine, send_sem=send.at[i], recv_sem=from_sib,
                device_id=sibling, device_id_type=MESH).wait_send()
        theirs = _rows_of(land_ref, sib_num, rows)
        pltpu.make_async_remote_copy(
            src_ref=src_ref, dst_ref=theirs, send_sem=send.at[0], recv_sem=from_sib,
            device_id=sibling, device_id_type=MESH).wait_recv()
        for j, d in enumerate(OTHER_CHIPS):
            _, num = _peer(x, y, c, d)
            sent = _rows_of(land_ref, num, rows)
            _, got_num = _peer(x, y, c, d | SIBLING)
            arrived = _rows_of(land_ref, got_num, rows)
            cp = pltpu.make_async_remote_copy(
                src_ref=sent, dst_ref=arrived, send_sem=fw_send.at[j], recv_sem=fw_recv.at[j],
                device_id=sibling, device_id_type=MESH)
            cp.wait_send()
            cp.wait_recv()

    src, land = pending["src"], pending["land"]
    return pl.pallas_call(
        body, name=name, out_shape=(pltpu.HBM(src.shape, src.dtype), pltpu.HBM(land.shape, land.dtype)),
        in_specs=(HBM_SPEC, HBM_SPEC) + (SEM_SPEC,) * 5,
        out_specs=(HBM_SPEC, HBM_SPEC), input_output_aliases={0: 0, 1: 1},
        compiler_params=pltpu.CompilerParams(has_side_effects=EFFECT),
    )(src, land, pending["send"], pending["from_sib"], pending["own"], pending["fw_send"], pending["fw_recv"])[1]


def _gather_end(pending, after, n):
    return _gather_wait(_gather_forward(pending, after, f"gather_forward_{n}"), f"gather_wait_{n}")


def _scatter_start(full, name):
    rows, cols = full.shape[0] // N_DEV, full.shape[1]
    land = lax.empty((N_DEV, rows, cols), full.dtype)

    def body(full_ref, land_ref, send, recv, own, full_thru, land_thru, token):
        x, y, c = _my_place()
        me = 4 * x + 2 * y + c
        slab = land_ref.at[me]
        pltpu.make_async_copy(full_ref.at[pl.ds(pl.multiple_of(me * rows, 16), rows), :], slab, own).start()
        for d in range(1, N_DEV):
            place, num = _peer(x, y, c, d)
            pltpu.make_async_remote_copy(
                src_ref=full_ref.at[pl.ds(pl.multiple_of(num * rows, 16), rows), :], dst_ref=slab,
                send_sem=send.at[d - 1], recv_sem=recv.at[d - 1], device_id=place, device_id_type=MESH).start()
        token[...] = jnp.zeros_like(token)

    res = pl.pallas_call(
        body, name=name,
        out_shape=(pltpu.SemaphoreType.DMA((N_DEV - 1,)), pltpu.SemaphoreType.DMA((N_DEV - 1,)),
                   pltpu.SemaphoreType.DMA(()),
                   pltpu.HBM(full.shape, full.dtype), pltpu.HBM(land.shape, land.dtype), _token_shape()),
        in_specs=(HBM_SPEC, HBM_SPEC),
        out_specs=(SEM_SPEC, SEM_SPEC, SEM_SPEC, HBM_SPEC, HBM_SPEC, pl.BlockSpec(memory_space=pltpu.VMEM)),
        input_output_aliases={0: 3, 1: 4},
        compiler_params=pltpu.CompilerParams(has_side_effects=EFFECT),
    )(_in_hbm(full), _in_hbm(land))
    return dict(send=res[0], recv=res[1], own=res[2], src=res[3], land=res[4]), res[5]


def _scatter_wait(pending, after, name):
    rows = pending["land"].shape[1]

    def body(src_ref, land_ref, send, recv, own, after_ref, src_dead, got):
        x, y, c = _my_place()
        me = 4 * x + 2 * y + c
        pltpu.make_async_copy(src_ref.at[pl.ds(pl.multiple_of(me * rows, 16), rows), :], land_ref.at[me], own).wait()
        for d in range(1, N_DEV):
            place, num = _peer(x, y, c, d)
            cp = pltpu.make_async_remote_copy(
                src_ref=src_ref.at[pl.ds(pl.multiple_of(num * rows, 16), rows), :], dst_ref=land_ref.at[me],
                send_sem=send.at[d - 1], recv_sem=recv.at[d - 1], device_id=place, device_id_type=MESH)
            cp.wait_send()
            cp.wait_recv()

    src, land = pending["src"], pending["land"]
    return pl.pallas_call(
        body, name=name, out_shape=(pltpu.HBM(src.shape, src.dtype), pltpu.HBM(land.shape, land.dtype)),
        in_specs=(HBM_SPEC, HBM_SPEC, SEM_SPEC, SEM_SPEC, SEM_SPEC, pl.BlockSpec(memory_space=pl.ANY)),
        out_specs=(HBM_SPEC, HBM_SPEC), input_output_aliases={0: 0, 1: 1},
        compiler_params=pltpu.CompilerParams(has_side_effects=EFFECT),
    )(src, land, pending["send"], pending["recv"], pending["own"], after)[1]


BIG = ("ffn1_w_in", "ffn1_w_out", "mix_w_in", "w_branch_a", "w_branch_b", "mix_w_out", "ffn2_w_in", "ffn2_w_out")
TRANSPOSED = ("ffn1_w_in", "mix_w_in", "w_branch_b", "ffn2_w_in")
SMALL = ("ln1_g", "ln1_b", "ln2_g", "ln2_b", "ln3_g", "ln3_b", "hgrn_norm_g", "hgrn_lb_fwd", "hgrn_lb_bwd")
SMALL_ROWS = 16


def _local_step(x, target, weight, emit, emit_small, sp):
    b_, s_, d_ = x.shape
    t_ = b_ * s_
    x2, tgt = x.reshape(t_, d_), target.reshape(t_, d_)
    xb = x2.astype(BF16)
    w1i = weight("ffn1_w_in", xb)
    g1, u1, a1 = _ffn_in_fwd(xb, w1i, "ffn1_in")
    w1o = weight("ffn1_w_out", a1)
    r1, h1, h1b = _mm_res_ln_fwd(a1, w1o, x2, sp["ln1_g"], sp["ln1_b"], 0.5, "ffn1_out_ln1")
    wmx = weight("mix_w_in", h1b)
    proj = _mm_nt(h1b, wmx, F32, "mix_in")
    ya_in, o_sum = _hgrn_fwd(proj, sp["hgrn_lb_fwd"], sp["hgrn_lb_bwd"], sp["hgrn_norm_g"], b_, s_, d_, "hgrn_fwd")
    tabs = _rope_tables(s_)
    outs, lses = [], []
    group_col = [(5 * d_ + gi * QKV_GROUP) // HEAD for gi in range(len(ATTN_GROUPS))]
    for gi, (_, dil) in enumerate(ATTN_GROUPS):
        o_g, lse_g = _attn_fwd(proj, tabs, b_, s_, group_col[gi], dil, f"attn_fwd_{gi}")
        outs.append(o_g)
        lses.append(lse_g)
    ob = _combine_fwd(outs, lses, "attn_combine")
    wa, wb = weight("w_branch_a", ya_in), weight("w_branch_b", ob)
    ya, yb, z = _gate_out_fwd(ya_in, ob, wa, wb, proj, d_, "branch_gate")
    wo = weight("mix_w_out", z)
    r2, h2, h2b = _mm_res_ln_fwd(z, wo, h1, sp["ln2_g"], sp["ln2_b"], 1.0, "mix_out_ln2")
    w2i = weight("ffn2_w_in", h2b)
    g2, u2, a2 = _ffn_in_fwd(h2b, w2i, "ffn2_in")
    w2o = weight("ffn2_w_out", a2)
    dr3, dr3b, dg3, db3, loss = _mm_res_loss_bwd(a2, w2o, h2, tgt, sp["ln3_g"], sp["ln3_b"], 0.5, "ffn2_out_loss")
    dep = emit("ffn2_w_out", _mm_tn(a2, dr3b, 0.5, "d_ffn2_w_out"))
    dgate2, dup2 = _ffn_mid_bwd(dr3b, w2o, g2, u2, 0.5, "ffn2_mid_bwd", dep)
    du2 = (dgate2, dup2)
    dep = emit("ffn2_w_in", _mm_tn(du2, h2b, 1.0, "d_ffn2_w_in"))
    dr2, dr2b, dg2, db2 = _mm_nn_res_lnbwd(du2, w2i, dr3, r2, sp["ln2_g"], "ffn2_in_bwd_ln2", dep)
    dep = emit("mix_w_out", _mm_tn(z, dr2b, 1.0, "d_mix_w_out"))
    dya, dyb, dga, dgb = _dz_gate_bwd(dr2b, wo, proj, ya, yb, d_, "branch_gate_bwd", dep)
    dep = emit("w_branch_a", _mm_tn(ya_in, dya, 1.0, "d_w_branch_a"))
    dya_in = _mm_nt(dya, wa, F32, "branch_a_bwd", dep)
    dep = emit("w_branch_b", _mm_tn(dyb, ob, 1.0, "d_w_branch_b"))
    dob = _mm_nn(dyb, wb, F32, "branch_b_bwd", dep)
    dhq, dhff, dhfb, dhi, dhog, dng, dlbf, dlbb = _hgrn_bwd(
        proj, sp["hgrn_lb_fwd"], sp["hgrn_lb_bwd"], sp["hgrn_norm_g"], o_sum, dya_in, b_, s_, d_, "hgrn_bwd")
    dogs, cgs = _combine_bwd(dob, outs, lses, "attn_combine_bwd")
    dqkv = []
    for gi, (_, dil) in enumerate(ATTN_GROUPS):
        dqkv += _attn_bwd(proj, tabs, dogs[gi], cgs[gi], lses[gi], b_, s_, group_col[gi], dil, f"attn_bwd_{gi}")
    dproj = jnp.concatenate([dhq, dhff, dhfb, dhi, dhog] + [t.astype(BF16) for t in dqkv] + [dga, dgb], axis=1)
    dep = emit("mix_w_in", _mm_tn(dproj, h1b, 1.0, "d_mix_w_in"))
    dr1, dr1b, dg1, db1 = _mm_nn_res_lnbwd(dproj, wmx, dr2, r1, sp["ln1_g"], "mix_in_bwd_ln1", dep)
    dep_small = emit_small({"ln1_g": dg1, "ln1_b": db1, "ln2_g": dg2, "ln2_b": db2, "ln3_g": dg3, "ln3_b": db3,
                            "hgrn_norm_g": dng, "hgrn_lb_fwd": dlbf, "hgrn_lb_bwd": dlbb})
    dep = emit("ffn1_w_out", _mm_tn(a1, dr1b, 0.5, "d_ffn1_w_out")) + dep_small
    dgate1, dup1 = _ffn_mid_bwd(dr1b, w1o, g1, u1, 0.5, "ffn1_mid_bwd", dep)
    du1 = (dgate1, dup1)
    dep = emit("ffn1_w_in", _mm_tn(du1, xb, 1.0, "d_ffn1_w_in"))
    grad_x = _mm_nn_res(du1, w1i, dr1, "ffn1_in_bwd", dep)
    return loss, grad_x.reshape(b_, s_, d_)


def _pack_small(vals):
    rows = jnp.concatenate([vals[n] for n in SMALL], axis=0)
    return jnp.pad(rows, ((0, SMALL_ROWS - rows.shape[0]), (0, 0)))


def _unpack_small(packed):
    out, r = {}, 0
    for n in SMALL:
        k = 2 if n.startswith("hgrn_lb") else 1
        out[n] = packed[r:r + k]
        r += k
    return out


def kernel(x, ffn1_w_in, ffn1_w_out, ln1_g, ln1_b, mix_w_in, hgrn_lb_fwd, hgrn_lb_bwd, hgrn_norm_g, w_branch_a, w_branch_b, mix_w_out, ln2_g, ln2_b, ffn2_w_in, ffn2_w_out, ln3_g, ln3_b, loss_target, m_ffn1_w_in, m_ffn1_w_out, m_ln1_g, m_ln1_b, m_mix_w_in, m_hgrn_lb_fwd, m_hgrn_lb_bwd, m_hgrn_norm_g, m_w_branch_a, m_w_branch_b, m_mix_w_out, m_ln2_g, m_ln2_b, m_ffn2_w_in, m_ffn2_w_out, m_ln3_g, m_ln3_b, v_ffn1_w_in, v_ffn1_w_out, v_ln1_g, v_ln1_b, v_mix_w_in, v_hgrn_lb_fwd, v_hgrn_lb_bwd, v_hgrn_norm_g, v_w_branch_a, v_w_branch_b, v_mix_w_out, v_ln2_g, v_ln2_b, v_ffn2_w_in, v_ffn2_w_out, v_ln3_g, v_ln3_b):
    args = dict(locals())
    big_w = {n: args[n][0] for n in BIG}
    sp = {n: args[n] for n in SMALL}
    def rows_bf16(n, zero=0.0):
        w = big_w[n] + zero
        return (w.T if n in TRANSPOSED else w).astype(BF16)

    first, rest = BIG[:2], BIG[2:]
    pending, token = _gather_start([rows_bf16(n) for n in first], "gather_start_ffn1")
    gathering = dict(zip(first, pending))
    pending, all_started = _gather_start([rows_bf16(n, token[0, 0]) for n in rest], "gather_start_rest")
    gathering.update(zip(rest, pending))
    scattering = {}

    def weight(n, after):
        if n == first[0]:
            after = all_started
        return _gather_end(gathering[n], after, n)

    def emit(n, grad):
        scattering[n], token = _scatter_start(grad, f"scatter_start_{n}")
        return token

    def emit_small(grads):
        pending, token = _gather_start([_pack_small(grads)], "gather_start_small")
        scattering["small"] = pending[0]
        return token

    loss_part, grad_x = _local_step(x, loss_target, weight, emit, emit_small, sp)
    loss = lax.psum(loss_part[0, 0], ("x", "y", "c"))
    out_g, out_d, out_m, out_v = {}, {}, {}, {}
    done = grad_x
    for n in ("ffn2_w_out", "ffn2_w_in", "mix_w_out", "w_branch_a", "w_branch_b", "mix_w_in", "small",
              "ffn1_w_out", "ffn1_w_in"):
        if n == "small":
            parts = _gather_end(scattering[n], done, n)
            res = _small_sum_adam(parts.reshape(N_DEV, SMALL_ROWS, parts.shape[1]), _pack_small(sp),
                                  _pack_small({n: args["m_" + n] for n in SMALL}),
                                  _pack_small({n: args["v_" + n] for n in SMALL}), "small_adam")
            sg, sd, sm, sv = (_unpack_small(r) for r in res)
            out_g.update(sg), out_d.update(sd), out_m.update(sm), out_v.update(sv)
            done = res[3]
            continue
        g = _sum_partials(_scatter_wait(scattering[n], done, f"scatter_wait_{n}"), f"sum_{n}")
        if n in TRANSPOSED:
            g = g.T
        d_w, m_w, v_w = _adam(big_w[n], g, args["m_" + n][0], args["v_" + n][0], f"adam_{n}")
        out_g[n], out_d[n], out_m[n], out_v[n] = g[None], d_w[None], m_w[None], v_w[None]
        done = v_w
    order = ("ffn1_w_in", "ffn1_w_out", "ln1_g", "ln1_b", "mix_w_in", "hgrn_lb_fwd", "hgrn_lb_bwd", "hgrn_norm_g",
             "w_branch_a", "w_branch_b", "mix_w_out", "ln2_g", "ln2_b", "ffn2_w_in", "ffn2_w_out", "ln3_g", "ln3_b")
    return (loss, grad_x, *[out_g[n] for n in order], *[out_d[n] for n in order],
            *[out_m[n] for n in order], *[out_v[n] for n in order])
```

```python
import jax
import jax.numpy as jnp
from jax import lax
from jax.experimental import pallas as pl
from jax.experimental.pallas import tpu as pltpu

F32 = jnp.float32
BF16 = jnp.bfloat16

N_DEV = 8
HEAD = 128
ATTN_GROUPS = ((128, 1), (512, 4), (2048, 16))
ATTN_HEADS = 4
ATTN_HALF = 64
QKV_GROUP = 3 * ATTN_HEADS * HEAD
QKV_WIDTH = len(ATTN_GROUPS) * QKV_GROUP
ATTN_OUT = ATTN_HEADS * HEAD
ROPE_THETA = 500000.0
ROPE_DIM = HEAD // 4
ALPHA = 2.0 ** 0.25
LN_EPS = 1e-5
NEG_INF = -1e30
ADAM_LR, ADAM_B1, ADAM_B2, ADAM_EPS, ADAM_WD, ADAM_STEP = 0.001, 0.9, 0.999, 1e-08, 0.01, 10
VMEM_LIMIT = 56 * 1024 * 1024

NT = (((1,), (1,)), ((), ()))
NN = (((1,), (0,)), ((), ()))
TN = (((0,), (0,)), ((), ()))
MESH = pl.DeviceIdType.MESH


def _dot(a, b, dims):
    return lax.dot_general(a, b, dims, preferred_element_type=F32)


def _tile(n, pref, mult=128):
    if n <= pref:
        return n
    t = (pref // mult) * mult
    while t >= mult:
        if n % t == 0:
            return t
        t -= mult
    return n


def _tile_multi(ns, pref, mult=128):
    t = (pref // mult) * mult
    while t >= mult:
        if all(n % t == 0 for n in ns):
            return t
        t -= mult
    raise ValueError(f"no common tile for {ns}")


def _params(**kw):
    return pltpu.CompilerParams(vmem_limit_bytes=VMEM_LIMIT, **kw)


def _after(body, n_in, dep):
    if dep is None:
        return body, [], []

    def wrapped(*refs):
        body(*refs[:n_in], *refs[n_in + 1:])

    return wrapped, [pl.BlockSpec(dep.shape, lambda *_: (0,) * dep.ndim)], [dep]


def _pieces(a):
    pieces = tuple(a) if isinstance(a, (tuple, list)) else (a,)
    assert all(p.shape == pieces[0].shape for p in pieces)
    return pieces, pieces[0].shape[0], pieces[0].shape[1], len(pieces)


def _for_piece(step, p, per, npc, fn):
    if npc == 1:
        fn()
    else:
        pl.when((step >= p * per) & (step < (p + 1) * per))(fn)


def _sigmoid(x):
    return jax.nn.sigmoid(x)


def _dsilu(x, s):
    return s * (1.0 + x * (1.0 - s))


def _ln_stats(r):
    mu = jnp.mean(r, axis=-1, keepdims=True)
    xc = r - mu
    var = jnp.mean(xc * xc, axis=-1, keepdims=True)
    rstd = lax.rsqrt(var + LN_EPS)
    return xc * rstd, rstd


def _ln_bwd(dy, xhat, rstd, g):
    dyg = dy * g
    m1 = jnp.mean(dyg, axis=-1, keepdims=True)
    m2 = jnp.mean(dyg * xhat, axis=-1, keepdims=True)
    return rstd * (dyg - m1 - xhat * m2)


ROW_TILE = 1024
SUB_ROWS = 256


def _once(shape, index_map):
    return pl.BlockSpec(shape, index_map, pipeline_mode=pl.Buffered(1))


def _for_row_blocks(tm, fn):
    sub = SUB_ROWS if tm % SUB_ROWS == 0 else tm

    def step(s, carry):
        fn(pl.ds(pl.multiple_of(s * sub, sub), sub))
        return carry

    lax.fori_loop(0, tm // sub, step, 0)


def _row_runs(tm):
    sub = SUB_ROWS if tm % SUB_ROWS == 0 else tm
    return [slice(s, s + sub) for s in range(0, tm, sub)]


def _ffn_in_fwd(xb, w_t, name):
    t_, d_ = xb.shape
    f_ = w_t.shape[0] // 2
    tm, tn = _tile(t_, ROW_TILE, 8), _tile(f_, 512)
    nj = f_ // tn

    def body(x_ref, wg_ref, wu_ref, g_ref, u_ref, a_ref):
        wg, wu = wg_ref[...], wu_ref[...]
        for rows in _row_runs(tm):
            x = x_ref[rows, :]
            g = _dot(x, wg, NT)
            u = _dot(x, wu, NT)
            g_ref[rows, :] = g.astype(BF16)
            u_ref[rows, :] = u.astype(BF16)
            a_ref[rows, :] = (g * _sigmoid(g) * u).astype(BF16)

    return pl.pallas_call(
        body, name=name, grid=(t_ // tm, nj),
        in_specs=[pl.BlockSpec((tm, d_), lambda i, j: (i, 0)),
                  pl.BlockSpec((tn, d_), lambda i, j: (j, 0)),
                  pl.BlockSpec((tn, d_), lambda i, j: (j + nj, 0))],
        out_specs=[pl.BlockSpec((tm, tn), lambda i, j: (i, j))] * 3,
        out_shape=[jax.ShapeDtypeStruct((t_, f_), BF16)] * 3,
        compiler_params=_params(),
    )(xb, w_t, w_t)


WHOLE_WEIGHT_BYTES = 24 * 1024 * 1024


def _k_plan(t_, k_, d_):
    if k_ * d_ * 2 > WHOLE_WEIGHT_BYTES:
        return False, _tile(t_, ROW_TILE, 8), _tile(k_, 512), _once
    if k_ <= 2048:
        return True, _tile(t_, ROW_TILE, 8), k_, _once
    return True, _tile(t_, SUB_ROWS, 8), k_, pl.BlockSpec


def _mm_res_ln_fwd(a, w, res, g, b, scale, name):
    t_, k_ = a.shape
    d_ = w.shape[1]
    whole, tm, tk, row_spec = _k_plan(t_, k_, d_)
    nk = k_ // tk

    def body(a_ref, w_ref, res_ref, g_ref, b_ref, r_ref, h_ref, hb_ref, *scratch):
        k = pl.program_id(1)
        if not whole:
            acc, = scratch

            @pl.when(k == 0)
            def _():
                acc[...] = jnp.zeros_like(acc)

            acc[...] += _dot(a_ref[...], w_ref[...], NN)

        @pl.when(k == nk - 1)
        def _():
            def rows_out(rows):
                prod = _dot(a_ref[rows, :], w_ref[...], NN) if whole else acc[rows, :]
                r = ALPHA * res_ref[rows, :] + scale * prod
                xhat, _ = _ln_stats(r)
                h = xhat * g_ref[...] + b_ref[...]
                r_ref[rows, :] = r
                h_ref[rows, :] = h
                hb_ref[rows, :] = h.astype(BF16)

            _for_row_blocks(tm, rows_out)

    row = row_spec((tm, d_), lambda i, k: (i, 0))
    vec = pl.BlockSpec((1, d_), lambda i, k: (0, 0))
    w_spec = _once((tk, d_), lambda i, k: (0, 0)) if whole else pl.BlockSpec((tk, d_), lambda i, k: (k, 0))
    return pl.pallas_call(
        body, name=name, grid=(t_ // tm, nk),
        in_specs=[pl.BlockSpec((tm, tk), lambda i, k: (i, k)), w_spec, row, vec, vec],
        out_specs=[row, row, row],
        out_shape=[jax.ShapeDtypeStruct((t_, d_), F32), jax.ShapeDtypeStruct((t_, d_), F32),
                   jax.ShapeDtypeStruct((t_, d_), BF16)],
        scratch_shapes=[] if whole else [pltpu.VMEM((tm, d_), F32)],
        compiler_params=_params(),
    )(a, w, res, g, b)


def _mm_res_loss_bwd(a, w, res, target, g, b, scale, name):
    t_, k_ = a.shape
    d_ = w.shape[1]
    whole, tm, tk, row_spec = _k_plan(t_, k_, d_)
    nk = k_ // tk

    def body(a_ref, w_ref, res_ref, t_ref, g_ref, b_ref, dr_ref, drb_ref, dg_ref, db_ref, loss_ref, *scratch):
        i, k = pl.program_id(0), pl.program_id(1)

        @pl.when((i == 0) & (k == 0))
        def _():
            dg_ref[...] = jnp.zeros_like(dg_ref)
            db_ref[...] = jnp.zeros_like(db_ref)
            loss_ref[...] = jnp.zeros_like(loss_ref)

        if not whole:
            acc, = scratch

            @pl.when(k == 0)
            def _():
                acc[...] = jnp.zeros_like(acc)

            acc[...] += _dot(a_ref[...], w_ref[...], NN)

        @pl.when(k == nk - 1)
        def _():
            def rows_out(rows):
                prod = _dot(a_ref[rows, :], w_ref[...], NN) if whole else acc[rows, :]
                r = ALPHA * res_ref[rows, :] + scale * prod
                xhat, rstd = _ln_stats(r)
                gain = g_ref[...]
                err = xhat * gain + b_ref[...] - t_ref[rows, :]
                loss_ref[...] += (0.5 / d_) * jnp.sum(err * err)
                dy = err * (1.0 / d_)
                dr = _ln_bwd(dy, xhat, rstd, gain)
                dr_ref[rows, :] = dr
                drb_ref[rows, :] = dr.astype(BF16)
                dg_ref[...] += jnp.sum(dy * xhat, axis=0, keepdims=True)
                db_ref[...] += jnp.sum(dy, axis=0, keepdims=True)

            _for_row_blocks(tm, rows_out)

    row = row_spec((tm, d_), lambda i, k: (i, 0))
    vec = pl.BlockSpec((1, d_), lambda i, k: (0, 0))
    w_spec = _once((tk, d_), lambda i, k: (0, 0)) if whole else pl.BlockSpec((tk, d_), lambda i, k: (k, 0))
    return pl.pallas_call(
        body, name=name, grid=(t_ // tm, nk),
        in_specs=[pl.BlockSpec((tm, tk), lambda i, k: (i, k)), w_spec, row, row, vec, vec],
        out_specs=[row, row, vec, vec, pl.BlockSpec((1, HEAD), lambda i, k: (0, 0))],
        out_shape=[jax.ShapeDtypeStruct((t_, d_), F32), jax.ShapeDtypeStruct((t_, d_), BF16),
                   jax.ShapeDtypeStruct((1, d_), F32), jax.ShapeDtypeStruct((1, d_), F32),
                   jax.ShapeDtypeStruct((1, HEAD), F32)],
        scratch_shapes=[] if whole else [pltpu.VMEM((tm, d_), F32)],
        compiler_params=_params(),
    )(a, w, res, target, g, b)


def _mm_nt(a, w_t, out_dtype, name, dep=None):
    t_, k_ = a.shape
    n_ = w_t.shape[0]
    tm, tn = _tile(t_, ROW_TILE, 8), _tile(n_, 512)

    def body(a_ref, w_ref, o_ref):
        o_ref[...] = _dot(a_ref[...], w_ref[...], NT).astype(out_dtype)

    body, dep_specs, deps = _after(body, 2, dep)
    return pl.pallas_call(
        body, name=name, grid=(t_ // tm, n_ // tn),
        in_specs=[pl.BlockSpec((tm, k_), lambda i, j: (i, 0)),
                  pl.BlockSpec((tn, k_), lambda i, j: (j, 0)), *dep_specs],
        out_specs=pl.BlockSpec((tm, tn), lambda i, j: (i, j)),
        out_shape=jax.ShapeDtypeStruct((t_, n_), out_dtype),
        compiler_params=_params(),
    )(a, w_t, *deps)


def _mm_nn(a, w, out_dtype, name, dep=None):
    t_, k_ = a.shape
    n_ = w.shape[1]
    tm, tn = _tile(t_, ROW_TILE, 8), _tile(n_, 512)

    def body(a_ref, w_ref, o_ref):
        o_ref[...] = _dot(a_ref[...], w_ref[...], NN).astype(out_dtype)

    body, dep_specs, deps = _after(body, 2, dep)
    return pl.pallas_call(
        body, name=name, grid=(t_ // tm, n_ // tn),
        in_specs=[pl.BlockSpec((tm, k_), lambda i, j: (i, 0)),
                  pl.BlockSpec((k_, tn), lambda i, j: (0, j)), *dep_specs],
        out_specs=pl.BlockSpec((tm, tn), lambda i, j: (i, j)),
        out_shape=jax.ShapeDtypeStruct((t_, n_), out_dtype),
        compiler_params=_params(),
    )(a, w, *deps)


def _mm_tn(a, b, scale, name):
    pieces, t_, mp, npc = _pieces(a)
    n_ = b.shape[1]
    tm = _tile(mp, 512)
    per = mp // tm

    def body(*refs):
        b_ref, o_ref = refs[npc], refs[npc + 1]
        for p in range(npc):
            def piece_out(p=p):
                o_ref[...] = (scale * _dot(refs[p][...], b_ref[...], TN)).astype(BF16)

            _for_piece(pl.program_id(0), p, per, npc, piece_out)

    return pl.pallas_call(
        body, name=name, grid=(npc * per,),
        in_specs=[pl.BlockSpec((t_, tm), lambda i, p=p: (0, jnp.clip(i - p * per, 0, per - 1))) for p in range(npc)]
        + [_once((t_, n_), lambda i: (0, 0))],
        out_specs=pl.BlockSpec((tm, n_), lambda i: (i, 0)),
        out_shape=jax.ShapeDtypeStruct((npc * mp, n_), BF16),
        compiler_params=_params(),
    )(*pieces, b)


def _gate_out_fwd(ya_in, ob, wa, wb_t, proj, d_, name):
    t_ = ya_in.shape[0]
    goff = 5 * d_ + QKV_WIDTH
    tm, tn = _tile(t_, ROW_TILE, 8), _tile_multi([d_, goff], 512)
    ja, jb = goff // tn, (goff + d_) // tn

    def body(ya_ref, ob_ref, wa_ref, wb_ref, ga_ref, gb_ref, yao_ref, ybo_ref, z_ref):
        wa, wb = wa_ref[...], wb_ref[...]
        for rows in _row_runs(tm):
            y_a = _dot(ya_ref[rows, :], wa, NN)
            y_b = _dot(ob_ref[rows, :], wb, NT)
            yao_ref[rows, :] = y_a.astype(BF16)
            ybo_ref[rows, :] = y_b.astype(BF16)
            z_ref[rows, :] = (_sigmoid(ga_ref[rows, :]) * y_a + _sigmoid(gb_ref[rows, :]) * y_b).astype(BF16)

    tile = pl.BlockSpec((tm, tn), lambda i, j: (i, j))
    return pl.pallas_call(
        body, name=name, grid=(t_ // tm, d_ // tn),
        in_specs=[pl.BlockSpec((tm, d_), lambda i, j: (i, 0)),
                  pl.BlockSpec((tm, ATTN_OUT), lambda i, j: (i, 0)),
                  pl.BlockSpec((d_, tn), lambda i, j: (0, j)),
                  pl.BlockSpec((tn, ATTN_OUT), lambda i, j: (j, 0)),
                  pl.BlockSpec((tm, tn), lambda i, j: (i, ja + j)),
                  pl.BlockSpec((tm, tn), lambda i, j: (i, jb + j))],
        out_specs=[tile, tile, tile],
        out_shape=[jax.ShapeDtypeStruct((t_, d_), BF16)] * 3,
        compiler_params=_params(),
    )(ya_in, ob, wa, wb_t, proj, proj)


def _ffn_mid_bwd(drb, w_out, gate, up, scale, name, dep=None):
    t_, d_ = drb.shape
    f_ = w_out.shape[0]
    tm, tn = _tile(t_, ROW_TILE, 8), _tile(f_, 512)

    def body(dr_ref, w_ref, g_ref, u_ref, dg_ref, du_ref):
        w = w_ref[...]
        for rows in _row_runs(tm):
            da = scale * _dot(dr_ref[rows, :], w, NT)
            g = g_ref[rows, :].astype(F32)
            s = _sigmoid(g)
            dg_ref[rows, :] = (da * u_ref[rows, :].astype(F32) * _dsilu(g, s)).astype(BF16)
            du_ref[rows, :] = (da * g * s).astype(BF16)

    body, dep_specs, deps = _after(body, 4, dep)
    tile = pl.BlockSpec((tm, tn), lambda i, j: (i, j))
    return pl.pallas_call(
        body, name=name, grid=(t_ // tm, f_ // tn),
        in_specs=[pl.BlockSpec((tm, d_), lambda i, j: (i, 0)),
                  pl.BlockSpec((tn, d_), lambda i, j: (j, 0)), tile, tile, *dep_specs],
        out_specs=[tile, tile],
        out_shape=[jax.ShapeDtypeStruct((t_, f_), BF16)] * 2,
        compiler_params=_params(),
    )(drb, w_out, gate, up, *deps)


def _mm_nn_res_lnbwd(a, w, dres, r, g, name, dep=None):
    pieces, t_, kp, npc = _pieces(a)
    d_ = w.shape[1]
    tm, tk = _tile(t_, ROW_TILE, 8), _tile(kp, 512)
    per = kp // tk
    nk = npc * per

    def body(*refs):
        w_ref, dres_ref, r_ref, g_ref, dr_ref, drb_ref, dg_ref, db_ref, acc = refs[npc:]
        i, k = pl.program_id(0), pl.program_id(1)

        @pl.when(k == 0)
        def _():
            acc[...] = jnp.zeros_like(acc)

        @pl.when((i == 0) & (k == 0))
        def _():
            dg_ref[...] = jnp.zeros_like(dg_ref)
            db_ref[...] = jnp.zeros_like(db_ref)

        for p in range(npc):
            def piece_in(p=p):
                acc[...] += _dot(refs[p][...], w_ref[...], NN)

            _for_piece(k, p, per, npc, piece_in)

        @pl.when(k == nk - 1)
        def _():
            def rows_out(rows):
                dy = acc[rows, :] + ALPHA * dres_ref[rows, :]
                xhat, rstd = _ln_stats(r_ref[rows, :])
                dr = _ln_bwd(dy, xhat, rstd, g_ref[...])
                dr_ref[rows, :] = dr
                drb_ref[rows, :] = dr.astype(BF16)
                dg_ref[...] += jnp.sum(dy * xhat, axis=0, keepdims=True)
                db_ref[...] += jnp.sum(dy, axis=0, keepdims=True)

            _for_row_blocks(tm, rows_out)

    body, dep_specs, deps = _after(body, npc + 4, dep)
    row = _once((tm, d_), lambda i, k: (i, 0))
    vec = pl.BlockSpec((1, d_), lambda i, k: (0, 0))
    return pl.pallas_call(
        body, name=name, grid=(t_ // tm, nk),
        in_specs=[pl.BlockSpec((tm, tk), lambda i, k, p=p: (i, jnp.clip(k - p * per, 0, per - 1))) for p in range(npc)]
        + [pl.BlockSpec((tk, d_), lambda i, k: (k, 0)), row, row, vec, *dep_specs],
        out_specs=[row, row, vec, vec],
        out_shape=[jax.ShapeDtypeStruct((t_, d_), F32), jax.ShapeDtypeStruct((t_, d_), BF16),
                   jax.ShapeDtypeStruct((1, d_), F32), jax.ShapeDtypeStruct((1, d_), F32)],
        scratch_shapes=[pltpu.VMEM((tm, d_), F32)],
        compiler_params=_params(),
    )(*pieces, w, dres, r, g, *deps)


def _mm_nn_res(a, w, dres, name, dep=None):
    pieces, t_, kp, npc = _pieces(a)
    d_ = w.shape[1]
    tm, tk = _tile(t_, ROW_TILE, 8), _tile(kp, 512)
    per = kp // tk
    nk = npc * per

    def body(*refs):
        w_ref, dres_ref, o_ref, acc = refs[npc:]
        k = pl.program_id(1)

        @pl.when(k == 0)
        def _():
            acc[...] = jnp.zeros_like(acc)

        for p in range(npc):
            def piece_in(p=p):
                acc[...] += _dot(refs[p][...], w_ref[...], NN)

            _for_piece(k, p, per, npc, piece_in)

        @pl.when(k == nk - 1)
        def _():
            def rows_out(rows):
                o_ref[rows, :] = acc[rows, :] + ALPHA * dres_ref[rows, :]

            _for_row_blocks(tm, rows_out)

    body, dep_specs, deps = _after(body, npc + 2, dep)
    row = _once((tm, d_), lambda i, k: (i, 0))
    return pl.pallas_call(
        body, name=name, grid=(t_ // tm, nk),
        in_specs=[pl.BlockSpec((tm, tk), lambda i, k, p=p: (i, jnp.clip(k - p * per, 0, per - 1))) for p in range(npc)]
        + [pl.BlockSpec((tk, d_), lambda i, k: (k, 0)), row, *dep_specs],
        out_specs=row,
        out_shape=jax.ShapeDtypeStruct((t_, d_), F32),
        scratch_shapes=[pltpu.VMEM((tm, d_), F32)],
        compiler_params=_params(),
    )(*pieces, w, dres, *deps)


def _dz_gate_bwd(drb, w_out, proj, ya, yb, d_, name, dep=None):
    t_ = drb.shape[0]
    goff = 5 * d_ + QKV_WIDTH
    tm, tn = _tile(t_, ROW_TILE, 8), _tile_multi([d_, goff], 512)
    ja, jb = goff // tn, (goff + d_) // tn

    def body(dr_ref, w_ref, ga_ref, gb_ref, ya_ref, yb_ref, dya_ref, dyb_ref, dga_ref, dgb_ref):
        w = w_ref[...]
        for rows in _row_runs(tm):
            dz = _dot(dr_ref[rows, :], w, NT)
            sa, sb = _sigmoid(ga_ref[rows, :]), _sigmoid(gb_ref[rows, :])
            dya_ref[rows, :] = (dz * sa).astype(BF16)
            dyb_ref[rows, :] = (dz * sb).astype(BF16)
            dga_ref[rows, :] = (dz * ya_ref[rows, :].astype(F32) * sa * (1.0 - sa)).astype(BF16)
            dgb_ref[rows, :] = (dz * yb_ref[rows, :].astype(F32) * sb * (1.0 - sb)).astype(BF16)

    body, dep_specs, deps = _after(body, 6, dep)
    tile = pl.BlockSpec((tm, tn), lambda i, j: (i, j))
    return pl.pallas_call(
        body, name=name, grid=(t_ // tm, d_ // tn),
        in_specs=[pl.BlockSpec((tm, d_), lambda i, j: (i, 0)),
                  pl.BlockSpec((tn, d_), lambda i, j: (j, 0)),
                  pl.BlockSpec((tm, tn), lambda i, j: (i, ja + j)),
                  pl.BlockSpec((tm, tn), lambda i, j: (i, jb + j)), tile, tile, *dep_specs],
        out_specs=[tile] * 4,
        out_shape=[jax.ShapeDtypeStruct((t_, d_), BF16)] * 4,
        compiler_params=_params(),
    )(drb, w_out, proj, proj, ya, yb, *deps)


def _lower_bound(tab):
    return _sigmoid(tab[0:1, :] - tab[1:2, :])


FWD_BLOCK = 128
BWD_BLOCK = 128


def _block_scan(x, row, reverse, size, blk):
    s = 1
    while s < blk:
        if reverse:
            x = x + jnp.where(row < blk - s, pltpu.roll(x, size - s, 0), 0.0)
        else:
            x = x + jnp.where(row >= s, pltpu.roll(x, s, 0), 0.0)
        s *= 2
    return x


def _block_exps(l, reverse):
    blk = l.shape[0]
    half = blk // 2
    first = lax.broadcasted_iota(jnp.int32, (blk, HEAD), 0) < half
    q1, q3 = half // 2, half + half // 2
    if reverse:
        rho1, rho2, lh, ltot = l[q1:q1 + 1], l[q3:q3 + 1], l[half:half + 1], l[0:1]
    else:
        rho1, rho2, lh, ltot = l[q1 - 1:q1], l[q3 - 1:q3], l[half - 1:half], l[blk - 1:blk]
    ref = jnp.where(first, rho1, rho2)
    query_half = first if reverse else jnp.logical_not(first)
    e2 = jnp.where(query_half, jnp.exp(jnp.minimum(l - lh, 0.0)), 0.0)
    e1 = jnp.where(query_half, 0.0, jnp.exp(jnp.minimum(lh - l, 0.0)))
    return (jnp.exp(l - ref), jnp.exp(ref - l), e2, e1, jnp.exp(l), jnp.exp(ltot - l),
            jnp.exp(ltot), jnp.exp(lh), jnp.exp(ltot - lh))


def _half_mask(reverse, blk):
    r = lax.broadcasted_iota(jnp.int32, (blk, blk), 0)
    c = lax.broadcasted_iota(jnp.int32, (blk, blk), 1)
    same = (r < blk // 2) == (c < blk // 2)
    return same & ((c >= r) if reverse else (r >= c))


def _hgrn_fwd(proj, lbf, lbb, ng, b_, s_, d_, name):
    h_ = d_ // HEAD
    BLOCK = min(FWD_BLOCK, s_)
    nb = s_ // BLOCK

    def body(hq_ref, hff_ref, hfb_ref, hi_ref, hog_ref, lbf_ref, lbb_ref, ng_ref, ya_ref, o_ref,
             q_s, k_s, l_s, of_s, oi_s, qd_s, u_s, st_s, dec_s):
        row = lax.broadcasted_iota(jnp.int32, (s_, HEAD), 0) % BLOCK
        hq = hq_ref[...]
        q_s[...] = hq * _sigmoid(hq)
        for reverse, hf_ref, lb_ref in ((False, hff_ref, lbf_ref), (True, hfb_ref, lbb_ref)):
            lb = _lower_bound(lb_ref[...])
            f = lb + (1.0 - lb) * _sigmoid(hf_ref[...])
            k_s[...] = 1.0 - f
            l_s[...] = _block_scan(jnp.log(f), row, reverse, s_, BLOCK)
            mask = _half_mask(reverse, BLOCK)

            def inside(n, carry, reverse=reverse, mask=mask):
                sl = pl.ds(pl.multiple_of(n * BLOCK, BLOCK), BLOCK)
                eq, ek, e2, e1, el, ee, dec, _, _ = _block_exps(l_s[sl, :], reverse)
                qc, kc = q_s[sl, :], k_s[sl, :]
                vb = hi_ref[sl, :].astype(BF16)
                a = jnp.where(mask, _dot((qc * eq).astype(BF16), (kc * ek).astype(BF16), NT), 0.0)
                a = a + _dot((qc * e2).astype(BF16), (kc * e1).astype(BF16), NT)
                oi_s[sl, :] = _dot(a.astype(BF16), vb, NN)
                qd_s[sl, :] = (qc * el).astype(BF16)
                u_s[n] = _dot(vb, (kc * ee).astype(BF16), TN)
                dec_s[n] = jnp.broadcast_to(dec, (8, HEAD))
                return carry

            lax.fori_loop(0, nb, inside, 0, unroll=8)

            def carry_state(n, st, reverse=reverse):
                idx = (nb - 1 - n) if reverse else n
                st_s[idx] = st.astype(BF16)
                return st * dec_s[idx][0:1, :] + u_s[idx]

            lax.fori_loop(0, nb, carry_state, jnp.zeros((HEAD, HEAD), F32))

            def across(n, carry, reverse=reverse):
                sl = pl.ds(pl.multiple_of(n * BLOCK, BLOCK), BLOCK)
                o_dir = oi_s[sl, :] + _dot(qd_s[sl, :], st_s[n], NT)
                if not reverse:
                    of_s[sl, :] = o_dir
                else:
                    o = of_s[sl, :] + o_dir
                    o_ref[sl, :] = o
                    nrm = o * lax.rsqrt(jnp.mean(o * o, axis=-1, keepdims=True) + LN_EPS)
                    hog = hog_ref[sl, :]
                    ya_ref[sl, :] = (nrm * ng_ref[...] * hog * _sigmoid(hog)).astype(BF16)
                return carry

            lax.fori_loop(0, nb, across, 0, unroll=8)

    def col(part):
        return pl.BlockSpec((s_, HEAD), lambda h, b, part=part: (b, part * h_ + h))

    tab = pl.BlockSpec((2, HEAD), lambda h, b: (0, h))
    out = pl.BlockSpec((s_, HEAD), lambda h, b: (b, h))
    return pl.pallas_call(
        body, name=name, grid=(h_, b_),
        in_specs=[col(0), col(1), col(2), col(3), col(4), tab, tab,
                  pl.BlockSpec((1, HEAD), lambda h, b: (0, h))],
        out_specs=[out, out],
        out_shape=[jax.ShapeDtypeStruct((b_ * s_, d_), BF16), jax.ShapeDtypeStruct((b_ * s_, d_), F32)],
        scratch_shapes=[pltpu.VMEM((s_, HEAD), F32)] * 5 + [
            pltpu.VMEM((s_, HEAD), BF16), pltpu.VMEM((nb, HEAD, HEAD), F32), pltpu.VMEM((nb, HEAD, HEAD), BF16),
            pltpu.VMEM((nb, 8, HEAD), F32)],
        compiler_params=_params(),
    )(proj, proj, proj, proj, proj, lbf, lbb, ng)


def _hgrn_bwd(proj, lbf, lbb, ng, o_sum, dya, b_, s_, d_, name):
    h_ = d_ // HEAD
    BLOCK = min(BWD_BLOCK, s_)
    nb = s_ // BLOCK

    def body(hq_ref, hff_ref, hfb_ref, hi_ref, hog_ref, lbf_ref, lbb_ref, ng_ref, o_ref, dya_ref,
             dhq_ref, dhff_ref, dhfb_ref, dhi_ref, dhog_ref, dng_ref, dlbf_ref, dlbb_ref,
             q_s, k_s, l_s, do_s, dq_s, dv_s, dl_s, dk_s, u_s, w_s, st_s, dst_s, dec_s):
        b = pl.program_id(1)

        @pl.when(b == 0)
        def _():
            dng_ref[...] = jnp.zeros_like(dng_ref)
            dlbf_ref[...] = jnp.zeros_like(dlbf_ref)
            dlbb_ref[...] = jnp.zeros_like(dlbb_ref)

        row = lax.broadcasted_iota(jnp.int32, (s_, HEAD), 0) % BLOCK
        brow = lax.broadcasted_iota(jnp.int32, (BLOCK, HEAD), 0)
        hq = hq_ref[...]
        q_s[...] = hq * _sigmoid(hq)
        o = o_ref[...]
        rinv = lax.rsqrt(jnp.mean(o * o, axis=-1, keepdims=True) + LN_EPS)
        nrm = o * rinv
        hog = hog_ref[...]
        so = _sigmoid(hog)
        gain = ng_ref[...]
        dy = dya_ref[...]
        dhog_ref[...] = (dy * nrm * gain * _dsilu(hog, so)).astype(BF16)
        dng_ref[...] += jnp.sum(dy * nrm * hog * so, axis=0, keepdims=True)
        dn = dy * gain * hog * so
        do_s[...] = rinv * (dn - nrm * jnp.mean(dn * nrm, axis=-1, keepdims=True))

        for reverse, hf_ref, lb_ref, dhf_ref, dlb_ref in (
                (False, hff_ref, lbf_ref, dhff_ref, dlbf_ref), (True, hfb_ref, lbb_ref, dhfb_ref, dlbb_ref)):
            lb = _lower_bound(lb_ref[...])
            sf = _sigmoid(hf_ref[...])
            f = lb + (1.0 - lb) * sf
            k_s[...] = 1.0 - f
            l_s[...] = _block_scan(jnp.log(f), row, reverse, s_, BLOCK)
            mask = _half_mask(reverse, BLOCK)
            total_row = 0 if reverse else BLOCK - 1
            key_end = BLOCK // 2 if reverse else BLOCK // 2 - 1

            def prepare(n, carry, reverse=reverse):
                sl = pl.ds(pl.multiple_of(n * BLOCK, BLOCK), BLOCK)
                _, _, _, _, el, ee, dec, _, _ = _block_exps(l_s[sl, :], reverse)
                vb = hi_ref[sl, :].astype(BF16)
                u_s[n] = _dot(vb, (k_s[sl, :] * ee).astype(BF16), TN)
                w_s[n] = _dot(do_s[sl, :].astype(BF16), (q_s[sl, :] * el).astype(BF16), TN)
                dec_s[n] = jnp.broadcast_to(dec, (8, HEAD))
                return carry

            lax.fori_loop(0, nb, prepare, 0, unroll=8)

            def carry_state(n, st, reverse=reverse):
                idx = (nb - 1 - n) if reverse else n
                st_s[idx] = st
                return st * dec_s[idx][0:1, :] + u_s[idx]

            lax.fori_loop(0, nb, carry_state, jnp.zeros((HEAD, HEAD), F32))

            def carry_grad(n, dst, reverse=reverse):
                idx = n if reverse else (nb - 1 - n)
                dst_s[idx] = dst
                return dst * dec_s[idx][0:1, :] + w_s[idx]

            lax.fori_loop(0, nb, carry_grad, jnp.zeros((HEAD, HEAD), F32))

            def inside(n, carry, reverse=reverse, mask=mask, total_row=total_row, key_end=key_end):
                sl = pl.ds(pl.multiple_of(n * BLOCK, BLOCK), BLOCK)
                eq, ek, e2, e1, el, ee, dec, dec_key, dec_query = _block_exps(l_s[sl, :], reverse)
                qc, kc = q_s[sl, :], k_s[sl, :]
                vb = hi_ref[sl, :].astype(BF16)
                dob = do_s[sl, :].astype(BF16)
                qt, kt, q2, k1 = ((qc * eq).astype(BF16), (kc * ek).astype(BF16),
                                  (qc * e2).astype(BF16), (kc * e1).astype(BF16))
                kend = kc * ee
                st0, dst1 = st_s[n], dst_s[n]
                dstb = dst1.astype(BF16)
                a = jnp.where(mask, _dot(qt, kt, NT), 0.0) + _dot(q2, k1, NT)
                da = _dot(dob, vb, NT)
                dab = da.astype(BF16)
                dad = jnp.where(mask, da, 0.0).astype(BF16)
                dqt, dkt = _dot(dad, kt, NN), _dot(dad, qt, TN)
                dq2, dk1 = _dot(dab, k1, NN), _dot(dab, q2, TN)
                dqd = _dot(dob, st0.astype(BF16), NN)
                dke = _dot(vb, dstb, NN)
                dv = _dot(a.astype(BF16), dob, TN) + _dot(kend.astype(BF16), dstb, NT)
                dq = dqt * eq + dq2 * e2 + dqd * el
                dk = dkt * ek + dk1 * e1 + dke * ee
                dtot = jnp.sum(dke * kend, axis=0, keepdims=True) + jnp.sum(dst1 * st0, axis=0, keepdims=True) * dec
                st_mid = st0 * dec_key + _dot(vb, k1, TN)
                dst_mid = dst1 * dec_query + _dot(dob, q2, TN)
                dmid = jnp.sum(dst_mid * st_mid, axis=0, keepdims=True)
                dl_s[sl, :] = (qc * dq - kc * dk + jnp.where(brow == total_row, dtot, 0.0)
                               + jnp.where(brow == key_end, dmid, 0.0))
                dk_s[sl, :] = dk
                if not reverse:
                    dq_s[sl, :] = dq
                    dv_s[sl, :] = dv
                else:
                    hqc = hq_ref[sl, :]
                    dhq_ref[sl, :] = ((dq_s[sl, :] + dq) * _dsilu(hqc, _sigmoid(hqc))).astype(BF16)
                    dhi_ref[sl, :] = (dv_s[sl, :] + dv).astype(BF16)
                return carry

            lax.fori_loop(0, nb, inside, 0, unroll=8)
            dlogf = _block_scan(dl_s[...], row % (BLOCK // 2), not reverse, s_, BLOCK // 2)
            df = dlogf / f - dk_s[...]
            dhf_ref[...] = (df * (1.0 - lb) * sf * (1.0 - sf)).astype(BF16)
            dlb = jnp.sum(df * (1.0 - sf), axis=0, keepdims=True) * lb * (1.0 - lb)
            dlb_ref[0:1, :] += dlb
            dlb_ref[1:2, :] -= dlb

    def col(part):
        return pl.BlockSpec((s_, HEAD), lambda h, b, part=part: (b, part * h_ + h))

    tab = pl.BlockSpec((2, HEAD), lambda h, b: (0, h))
    vec = pl.BlockSpec((1, HEAD), lambda h, b: (0, h))
    blk = pl.BlockSpec((s_, HEAD), lambda h, b: (b, h))
    act = jax.ShapeDtypeStruct((b_ * s_, d_), BF16)
    state = pltpu.VMEM((nb, HEAD, HEAD), F32)
    return pl.pallas_call(
        body, name=name, grid=(h_, b_),
        in_specs=[col(0), col(1), col(2), col(3), col(4), tab, tab, vec, blk, blk],
        out_specs=[blk] * 5 + [vec, tab, tab],
        out_shape=[act] * 5 + [jax.ShapeDtypeStruct((1, d_), F32), jax.ShapeDtypeStruct((2, d_), F32),
                               jax.ShapeDtypeStruct((2, d_), F32)],
        scratch_shapes=[pltpu.VMEM((s_, HEAD), F32)] * 8 + [state] * 4 + [pltpu.VMEM((nb, 8, HEAD), F32)],
        compiler_params=_params(),
    )(proj, proj, proj, proj, proj, lbf, lbb, ng, o_sum, dya)


def _rope_tables(s_):
    half = ROPE_DIM // 2
    inv_freq = ROPE_THETA ** (-jnp.arange(0, ROPE_DIM, 2, dtype=F32) / ROPE_DIM)
    ang = jnp.arange(s_, dtype=F32)[:, None] * inv_freq
    cos, sin = jnp.cos(ang), jnp.sin(ang)
    zeros = jnp.zeros((s_, HEAD - ROPE_DIM), F32)
    zh = jnp.zeros((s_, half), F32)
    c = jnp.concatenate([cos, cos, jnp.ones((s_, HEAD - ROPE_DIM), F32)], axis=1)
    s1 = jnp.concatenate([-sin, zh, zeros], axis=1)
    s2 = jnp.concatenate([zh, sin, zeros], axis=1)
    return c, s1, s2


def _rope(t, c, s1, s2):
    half = ROPE_DIM // 2
    return t * c + pltpu.roll(t, HEAD - half, 1) * s1 + pltpu.roll(t, half, 1) * s2


def _rope_bwd(dt, c, s1, s2):
    half = ROPE_DIM // 2
    return dt * c + pltpu.roll(dt * s1, half, 1) + pltpu.roll(dt * s2, HEAD - half, 1)


def _window_mask(r0, qb, wk, seg):
    row = lax.broadcasted_iota(jnp.int32, (qb, wk), 0)
    col = lax.broadcasted_iota(jnp.int32, (qb, wk), 1)
    kj = r0 - ATTN_HALF + col
    return (col - row >= 0) & (col - row <= 2 * ATTN_HALF) & (kj >= 0) & (kj < seg)


UNROLLED_BLOCKS = 4


def _classes_in_flight(dil, nq):
    par = 1
    while par < 4 and dil % (2 * par) == 0 and 2 * par * nq <= 8:
        par *= 2
    return par


def _block_start(i, qb):
    return i * qb if isinstance(i, int) else pl.multiple_of(i * qb, qb)


def _for_blocks(nq, step):
    if nq <= UNROLLED_BLOCKS:
        for i in range(nq):
            step(i, 0)
    else:
        lax.fori_loop(0, nq, step, 0, unroll=UNROLLED_BLOCKS)


def _attn_fwd(proj, tabs, b_, s_, col0, dil, name):
    seg = s_ // dil
    qb = min(128, seg)
    nq, wk = seg // qb, qb + 2 * ATTN_HALF
    scale = HEAD ** -0.5

    par = _classes_in_flight(dil, nq)

    def body(q_ref, k_ref, v_ref, c_ref, s1_ref, s2_ref, o_ref, lse_ref, q_all, k_all, v_all):
        k_all[...] = jnp.zeros_like(k_all)
        v_all[...] = jnp.zeros_like(v_all)

        def residue(r, q_s, k_s, v_s):
            cls = pl.ds(r, seg, stride=dil)
            c, s1, s2 = c_ref[cls, :], s1_ref[cls, :], s2_ref[cls, :]
            q_s[...] = _rope(q_ref[cls, :], c, s1, s2).astype(BF16)
            k_s[ATTN_HALF:ATTN_HALF + seg, :] = _rope(k_ref[cls, :], c, s1, s2).astype(BF16)
            v_s[ATTN_HALF:ATTN_HALF + seg, :] = v_ref[cls, :].astype(BF16)

            def step(i, carry):
                r0 = _block_start(i, qb)
                sc = _dot(q_s[pl.ds(r0, qb), :], k_s[pl.ds(r0, wk), :], NT) * scale
                sc = jnp.where(_window_mask(r0, qb, wk, seg), sc, NEG_INF)
                m = jnp.max(sc, axis=-1, keepdims=True)
                p = jnp.exp(sc - m)
                den = jnp.sum(p, axis=-1, keepdims=True)
                rows = pl.ds(r + r0 * dil, qb, stride=dil)
                o_ref[rows, :] = _dot(p.astype(BF16), v_s[pl.ds(r0, wk), :], NN) / den
                lse_ref[rows, :] = jnp.broadcast_to(m + jnp.log(den), (qb, HEAD))
                return carry

            _for_blocks(nq, step)

        def classes(j, carry):
            for slot in range(par):
                residue(j * par + slot, q_all.at[slot], k_all.at[slot], v_all.at[slot])
            return carry

        lax.fori_loop(0, dil // par, classes, 0)

    def col(part):
        return pl.BlockSpec((s_, HEAD), lambda b, h, part=part: (b, col0 + part * ATTN_HEADS + h))

    tab = pl.BlockSpec((s_, HEAD), lambda b, h: (0, 0))
    out = pl.BlockSpec((s_, HEAD), lambda b, h: (b, h))
    shape = jax.ShapeDtypeStruct((b_ * s_, ATTN_OUT), F32)
    return pl.pallas_call(
        body, name=name, grid=(b_, ATTN_HEADS),
        in_specs=[col(0), col(1), col(2), tab, tab, tab],
        out_specs=[out, out],
        out_shape=[shape, shape],
        scratch_shapes=[pltpu.VMEM((par, seg, HEAD), BF16), pltpu.VMEM((par, seg + 2 * ATTN_HALF, HEAD), BF16),
                        pltpu.VMEM((par, seg + 2 * ATTN_HALF, HEAD), BF16)],
        compiler_params=_params(),
    )(proj, proj, proj, *tabs)


def _attn_bwd(proj, tabs, dog, cg, lse, b_, s_, col0, dil, name):
    seg = s_ // dil
    qb = min(128, seg)
    nq, wk = seg // qb, qb + 2 * ATTN_HALF
    scale = HEAD ** -0.5

    par = _classes_in_flight(dil, nq)

    def body(q_ref, k_ref, v_ref, c_ref, s1_ref, s2_ref, do_ref, cg_ref, lse_ref, dq_ref, dk_ref, dv_ref,
             q_all, k_all, v_all, do_all, cg_all, lse_all, dk_all, dv_all):
        k_all[...] = jnp.zeros_like(k_all)
        v_all[...] = jnp.zeros_like(v_all)

        def residue(r, q_s, k_s, v_s, do_s, cg_s, lse_s, dk_s, dv_s):
            cls = pl.ds(r, seg, stride=dil)
            c, s1, s2 = c_ref[cls, :], s1_ref[cls, :], s2_ref[cls, :]
            q_s[...] = _rope(q_ref[cls, :], c, s1, s2).astype(BF16)
            k_s[ATTN_HALF:ATTN_HALF + seg, :] = _rope(k_ref[cls, :], c, s1, s2).astype(BF16)
            v_s[ATTN_HALF:ATTN_HALF + seg, :] = v_ref[cls, :].astype(BF16)
            do_s[...] = do_ref[cls, :].astype(BF16)
            cg_s[...] = cg_ref[cls, :]
            lse_s[...] = lse_ref[cls, :]
            dk_s[...] = jnp.zeros_like(dk_s)
            dv_s[...] = jnp.zeros_like(dv_s)

            def step(i, carry):
                r0 = _block_start(i, qb)
                rows, win = pl.ds(r0, qb), pl.ds(r0, wk)
                qc, kw, vw = q_s[rows, :], k_s[win, :], v_s[win, :]
                sc = _dot(qc, kw, NT) * scale
                p = jnp.where(_window_mask(r0, qb, wk, seg), jnp.exp(sc - lse_s[rows, 0:1]), 0.0)
                dob = do_s[rows, :]
                dp = _dot(dob, vw, NT)
                ds = (p * (dp + cg_s[rows, 0:1]) * scale).astype(BF16)
                out = pl.ds(r + r0 * dil, qb, stride=dil)
                dq_ref[out, :] = _rope_bwd(_dot(ds, kw, NN), c_ref[out, :], s1_ref[out, :], s2_ref[out, :])
                dk_s[win, :] += _dot(ds, qc, TN)
                dv_s[win, :] += _dot(p.astype(BF16), dob, TN)
                return carry

            _for_blocks(nq, step)
            dk_ref[cls, :] = _rope_bwd(dk_s[ATTN_HALF:ATTN_HALF + seg, :], c, s1, s2)
            dv_ref[cls, :] = dv_s[ATTN_HALF:ATTN_HALF + seg, :]

        scratch = (q_all, k_all, v_all, do_all, cg_all, lse_all, dk_all, dv_all)

        def classes(j, carry):
            for slot in range(par):
                residue(j * par + slot, *[s.at[slot] for s in scratch])
            return carry

        lax.fori_loop(0, dil // par, classes, 0)

    def col(part):
        return pl.BlockSpec((s_, HEAD), lambda b, h, part=part: (b, col0 + part * ATTN_HEADS + h))

    tab = pl.BlockSpec((s_, HEAD), lambda b, h: (0, 0))
    out = pl.BlockSpec((s_, HEAD), lambda b, h: (b, h))
    shape = jax.ShapeDtypeStruct((b_ * s_, ATTN_OUT), F32)
    pad = seg + 2 * ATTN_HALF
    return pl.pallas_call(
        body, name=name, grid=(b_, ATTN_HEADS),
        in_specs=[col(0), col(1), col(2), tab, tab, tab, out, out, out],
        out_specs=[out, out, out],
        out_shape=[shape, shape, shape],
        scratch_shapes=[pltpu.VMEM((par, seg, HEAD), BF16), pltpu.VMEM((par, pad, HEAD), BF16),
                        pltpu.VMEM((par, pad, HEAD), BF16), pltpu.VMEM((par, seg, HEAD), BF16),
                        pltpu.VMEM((par, seg, HEAD), F32), pltpu.VMEM((par, seg, HEAD), F32),
                        pltpu.VMEM((par, pad, HEAD), F32), pltpu.VMEM((par, pad, HEAD), F32)],
        compiler_params=_params(),
    )(proj, proj, proj, *tabs, dog, cg, lse)


def _group_weights(lses):
    m = jnp.maximum(jnp.maximum(lses[0], lses[1]), lses[2])
    es = [jnp.exp(l - m) for l in lses]
    den = es[0] + es[1] + es[2]
    return [e / den for e in es]


def _combine_fwd(outs, lses, name):
    t_, w_ = outs[0].shape
    tm = _tile(t_, 512, 8)
    ng = len(outs)

    def body(*refs):
        ws = _group_weights([r[...] for r in refs[ng:2 * ng]])
        acc = ws[0] * refs[0][...]
        for g in range(1, ng):
            acc = acc + ws[g] * refs[g][...]
        refs[2 * ng][...] = acc.astype(BF16)

    row = pl.BlockSpec((tm, w_), lambda i: (i, 0))
    return pl.pallas_call(
        body, name=name, grid=(t_ // tm,), in_specs=[row] * (2 * ng), out_specs=row,
        out_shape=jax.ShapeDtypeStruct((t_, w_), BF16), compiler_params=_params(),
    )(*outs, *lses)


def _combine_bwd(dob, outs, lses, name):
    t_, w_ = outs[0].shape
    tm = _tile(t_, 512, 8)
    ng = len(outs)

    def body(*refs):
        do = refs[0][...]
        os_ = [r[...] for r in refs[1:1 + ng]]
        ws = _group_weights([r[...] for r in refs[1 + ng:1 + 2 * ng]])
        o = ws[0] * os_[0]
        for g in range(1, ng):
            o = o + ws[g] * os_[g]
        prod = do * o
        heads = [jnp.broadcast_to(jnp.sum(prod[:, h * HEAD:(h + 1) * HEAD], axis=-1, keepdims=True), (tm, HEAD))
                 for h in range(w_ // HEAD)]
        tot = jnp.concatenate(heads, axis=1)
        for g in range(ng):
            refs[1 + 2 * ng + g][...] = ws[g] * do
            refs[1 + 3 * ng + g][...] = -ws[g] * tot

    row = pl.BlockSpec((tm, w_), lambda i: (i, 0))
    shape = jax.ShapeDtypeStruct((t_, w_), F32)
    res = pl.pallas_call(
        body, name=name, grid=(t_ // tm,), in_specs=[row] * (1 + 2 * ng), out_specs=[row] * (2 * ng),
        out_shape=[shape] * (2 * ng), compiler_params=_params(),
    )(dob, *outs, *lses)
    return res[:ng], res[ng:]


def _adam_update(w, g, m, v):
    m = ADAM_B1 * m + (1.0 - ADAM_B1) * g
    v = ADAM_B2 * v + (1.0 - ADAM_B2) * (g * g)
    m_hat = m / (1.0 - ADAM_B1 ** ADAM_STEP)
    v_hat = v / (1.0 - ADAM_B2 ** ADAM_STEP)
    return -ADAM_LR * (m_hat / (jnp.sqrt(v_hat) + ADAM_EPS) + ADAM_WD * w), m, v


def _adam(w, g, m, v, name):
    r_, c_ = w.shape
    tr = _tile(r_, 256, 8)

    def body(w_ref, g_ref, m_ref, v_ref, d_ref, mo_ref, vo_ref):
        d_ref[...], mo_ref[...], vo_ref[...] = _adam_update(w_ref[...], g_ref[...], m_ref[...], v_ref[...])

    blk = pl.BlockSpec((tr, c_), lambda i: (i, 0))
    shape = jax.ShapeDtypeStruct((r_, c_), F32)
    return pl.pallas_call(
        body, name=name, grid=(r_ // tr,), in_specs=[blk] * 4, out_specs=[blk] * 3,
        out_shape=[shape] * 3, compiler_params=_params(),
    )(w, g, m, v)


def _sum_partials(recv, name):
    n_, r_, c_ = recv.shape
    tr = _tile(r_, 128, 16)

    def body(p_ref, o_ref):
        acc = p_ref[0].astype(F32)
        for i in range(1, n_):
            acc = acc + p_ref[i].astype(F32)
        o_ref[...] = acc

    return pl.pallas_call(
        body, name=name, grid=(r_ // tr,),
        in_specs=[pl.BlockSpec((n_, tr, c_), lambda i: (0, i, 0))],
        out_specs=pl.BlockSpec((tr, c_), lambda i: (i, 0)),
        out_shape=jax.ShapeDtypeStruct((r_, c_), F32), compiler_params=_params(),
    )(recv)


def _small_sum_adam(parts, w, m, v, name):
    n_, r_, c_ = parts.shape

    def body(p_ref, w_ref, m_ref, v_ref, g_ref, d_ref, mo_ref, vo_ref):
        g = p_ref[0]
        for i in range(1, n_):
            g = g + p_ref[i]
        g_ref[...] = g
        d_ref[...], mo_ref[...], vo_ref[...] = _adam_update(w_ref[...], g, m_ref[...], v_ref[...])

    shape = jax.ShapeDtypeStruct((r_, c_), F32)
    return pl.pallas_call(body, name=name, out_shape=[shape] * 4, compiler_params=_params())(parts, w, m, v)


def _my_place():
    x, y, c = lax.axis_index("x"), lax.axis_index("y"), lax.axis_index("c")
    return x, y, c


def _peer(x, y, c, d):
    px = 1 - x if d & 4 else x
    py = 1 - y if d & 2 else y
    pc = 1 - c if d & 1 else c
    return (px, py, pc), 4 * px + 2 * py + pc


HBM_SPEC =pl.BlockSpec(memory_space=pltpu.HBM)
SEM_SPEC = pl.BlockSpec(memory_space=pltpu.SEMAPHORE)
EFFECT = pltpu.SideEffectType.DATAFLOW_SIDE_EFFECTING


def _in_hbm(a):
    return pltpu.with_memory_space_constraint(a, pltpu.HBM)


def _token_shape():
    return jax.ShapeDtypeStruct((8, HEAD), F32)


SIBLING = 1
OTHER_CHIPS = (4, 2, 6)


def _rows_of(ref, num, rows):
    return ref.at[pl.ds(pl.multiple_of(num * rows, 16), rows), :]


def _gather_start(shards, name):
    nw = len(shards)
    lands = [lax.empty((N_DEV * s.shape[0], s.shape[1]), s.dtype) for s in shards]
    n_to = 1 + len(OTHER_CHIPS)

    def body(*refs):
        ins, lnd = refs[:nw], refs[nw:2 * nw]
        send, from_sib, from_chips, own = (refs[(2 + i) * nw:(3 + i) * nw] for i in range(4))
        token = refs[8 * nw]
        x, y, c = _my_place()
        me = 4 * x + 2 * y + c
        for k in range(nw):
            mine = _rows_of(lnd[k], me, shards[k].shape[0])
            pltpu.make_async_copy(ins[k], mine, own[k]).start()
            for i, d in enumerate((SIBLING,) + OTHER_CHIPS):
                place, _ = _peer(x, y, c, d)
                pltpu.make_async_remote_copy(
                    src_ref=ins[k], dst_ref=mine, send_sem=send[k].at[i],
                    recv_sem=from_sib[k] if i == 0 else from_chips[k].at[i - 1],
                    device_id=place, device_id_type=MESH).start()
        token[...] = jnp.zeros_like(token)

    dma = pltpu.SemaphoreType.DMA
    sems = [dma((n_to,))] * nw + [dma(())] * nw + [dma((len(OTHER_CHIPS),))] * nw + [dma(())] * nw
    thru = [pltpu.HBM(a.shape, a.dtype) for a in list(shards) + lands]
    res = pl.pallas_call(
        body, name=name, out_shape=(*sems, *thru, _token_shape()),
        in_specs=[HBM_SPEC] * (2 * nw),
        out_specs=(*([SEM_SPEC] * (4 * nw)), *([HBM_SPEC] * (2 * nw)), pl.BlockSpec(memory_space=pltpu.VMEM)),
        input_output_aliases={i: 4 * nw + i for i in range(2 * nw)},
        compiler_params=pltpu.CompilerParams(has_side_effects=EFFECT),
    )(*[_in_hbm(s) for s in shards], *[_in_hbm(l) for l in lands])
    return [dict(send=res[k], from_sib=res[nw + k], from_chips=res[2 * nw + k], own=res[3 * nw + k],
                 src=res[4 * nw + k], land=res[5 * nw + k]) for k in range(nw)], res[6 * nw]


def _gather_forward(pending, after, name):
    rows = pending["src"].shape[0]
    n_fw = len(OTHER_CHIPS)

    def body(land_ref, from_chips, after_ref, fw_send, fw_recv, land_thru):
        x, y, c = _my_place()
        sibling, _ = _peer(x, y, c, SIBLING)
        for j, d in enumerate(OTHER_CHIPS):
            _, num = _peer(x, y, c, d)
            block = _rows_of(land_ref, num, rows)
            pltpu.make_async_remote_copy(
                src_ref=block, dst_ref=block, send_sem=fw_send.at[j], recv_sem=from_chips.at[j],
                device_id=sibling, device_id_type=MESH).wait_recv()
            pltpu.make_async_remote_copy(
                src_ref=block, dst_ref=block, send_sem=fw_send.at[j], recv_sem=fw_recv.at[j],
                device_id=sibling, device_id_type=MESH).start()

    land = pending["land"]
    dma = pltpu.SemaphoreType.DMA
    fw_send, fw_recv, land = pl.pallas_call(
        body, name=name, out_shape=(dma((n_fw,)), dma((n_fw,)), pltpu.HBM(land.shape, land.dtype)),
        in_specs=(HBM_SPEC, SEM_SPEC, pl.BlockSpec(memory_space=pl.ANY)),
        out_specs=(SEM_SPEC, SEM_SPEC, HBM_SPEC), input_output_aliases={0: 2},
        compiler_params=pltpu.CompilerParams(has_side_effects=EFFECT),
    )(land, pending["from_chips"], after)
    return dict(pending, land=land, fw_send=fw_send, fw_recv=fw_recv)


def _gather_wait(pending, name):
    rows = pending["src"].shape[0]

    def body(src_ref, land_ref, send, from_sib, own, fw_send, fw_recv, src_dead, got):
        x, y, c = _my_place()
        me = 4 * x + 2 * y + c
        sibling, sib_num = _peer(x, y, c, SIBLING)
        mine = _rows_of(land_ref, me, rows)
        pltpu.make_async_copy(src_ref, mine, own).wait()
        for i in range(1 + len(OTHER_CHIPS)):
            pltpu.make_async_remote_copy(
                src_ref=src_ref, dst_ref=mine, send_sem=send.at[i], recv_sem=from_sib,
                device_id=sibling, device_id_type=MESH).wait_send()
        theirs = _rows_of(land_ref, sib_num, rows)
        pltpu.make_async_remote_copy(
            src_ref=src_ref, dst_ref=theirs, send_sem=send.at[0], recv_sem=from_sib,
            device_id=sibling, device_id_type=MESH).wait_recv()
        for j, d in enumerate(OTHER_CHIPS):
            _, num = _peer(x, y, c, d)
            sent = _rows_of(land_ref, num, rows)
            _, got_num = _peer(x, y, c, d | SIBLING)
            arrived = _rows_of(land_ref, got_num, rows)
            cp = pltpu.make_async_remote_copy(
                src_ref=sent, dst_ref=arrived, send_sem=fw_send.at[j], recv_sem=fw_recv.at[j],
                device_id=sibling, device_id_type=MESH)
            cp.wait_send()
            cp.wait_recv()

    src, land = pending["src"], pending["land"]
    return pl.pallas_call(
        body, name=name, out_shape=(pltpu.HBM(src.shape, src.dtype), pltpu.HBM(land.shape, land.dtype)),
        in_specs=(HBM_SPEC, HBM_SPEC) + (SEM_SPEC,) * 5,
        out_specs=(HBM_SPEC, HBM_SPEC), input_output_aliases={0: 0, 1: 1},
        compiler_params=pltpu.CompilerParams(has_side_effects=EFFECT),
    )(src, land, pending["send"], pending["from_sib"], pending["own"], pending["fw_send"], pending["fw_recv"])[1]


def _gather_end(pending, after, n):
    return _gather_wait(_gather_forward(pending, after, f"gather_forward_{n}"), f"gather_wait_{n}")


def _scatter_start(full, name):
    rows, cols = full.shape[0] // N_DEV, full.shape[1]
    land = lax.empty((N_DEV, rows, cols), full.dtype)

    def body(full_ref, land_ref, send, recv, own, full_thru, land_thru, token):
        x, y, c = _my_place()
        me = 4 * x + 2 * y + c
        slab = land_ref.at[me]
        pltpu.make_async_copy(full_ref.at[pl.ds(pl.multiple_of(me * rows, 16), rows), :], slab, own).start()
        for d in range(1, N_DEV):
            place, num = _peer(x, y, c, d)
            pltpu.make_async_remote_copy(
                src_ref=full_ref.at[pl.ds(pl.multiple_of(num * rows, 16), rows), :], dst_ref=slab,
                send_sem=send.at[d - 1], recv_sem=recv.at[d - 1], device_id=place, device_id_type=MESH).start()
        token[...] = jnp.zeros_like(token)

    res = pl.pallas_call(
        body, name=name,
        out_shape=(pltpu.SemaphoreType.DMA((N_DEV - 1,)), pltpu.SemaphoreType.DMA((N_DEV - 1,)),
                   pltpu.SemaphoreType.DMA(()),
                   pltpu.HBM(full.shape, full.dtype), pltpu.HBM(land.shape, land.dtype), _token_shape()),
        in_specs=(HBM_SPEC, HBM_SPEC),
        out_specs=(SEM_SPEC, SEM_SPEC, SEM_SPEC, HBM_SPEC, HBM_SPEC, pl.BlockSpec(memory_space=pltpu.VMEM)),
        input_output_aliases={0: 3, 1: 4},
        compiler_params=pltpu.CompilerParams(has_side_effects=EFFECT),
    )(_in_hbm(full), _in_hbm(land))
    return dict(send=res[0], recv=res[1], own=res[2], src=res[3], land=res[4]), res[5]


def _scatter_wait(pending, after, name):
    rows = pending["land"].shape[1]

    def body(src_ref, land_ref, send, recv, own, after_ref, src_dead, got):
        x, y, c = _my_place()
        me = 4 * x + 2 * y + c
        pltpu.make_async_copy(src_ref.at[pl.ds(pl.multiple_of(me * rows, 16), rows), :], land_ref.at[me], own).wait()
        for d in range(1, N_DEV):
            place, num = _peer(x, y, c, d)
            cp = pltpu.make_async_remote_copy(
                src_ref=src_ref.at[pl.ds(pl.multiple_of(num * rows, 16), rows), :], dst_ref=land_ref.at[me],
                send_sem=send.at[d - 1], recv_sem=recv.at[d - 1], device_id=place, device_id_type=MESH)
            cp.wait_send()
            cp.wait_recv()

    src, land = pending["src"], pending["land"]
    return pl.pallas_call(
        body, name=name, out_shape=(pltpu.HBM(src.shape, src.dtype), pltpu.HBM(land.shape, land.dtype)),
        in_specs=(HBM_SPEC, HBM_SPEC, SEM_SPEC, SEM_SPEC, SEM_SPEC, pl.BlockSpec(memory_space=pl.ANY)),
        out_specs=(HBM_SPEC, HBM_SPEC), input_output_aliases={0: 0, 1: 1},
        compiler_params=pltpu.CompilerParams(has_side_effects=EFFECT),
    )(src, land, pending["send"], pending["recv"], pending["own"], after)[1]


BIG = ("ffn1_w_in", "ffn1_w_out", "mix_w_in", "w_branch_a", "w_branch_b", "mix_w_out", "ffn2_w_in", "ffn2_w_out")
TRANSPOSED = ("ffn1_w_in", "mix_w_in", "w_branch_b", "ffn2_w_in")
SMALL = ("ln1_g", "ln1_b", "ln2_g", "ln2_b", "ln3_g", "ln3_b", "hgrn_norm_g", "hgrn_lb_fwd", "hgrn_lb_bwd")
SMALL_ROWS = 16


def _local_step(x, target, weight, emit, emit_small, sp):
    b_, s_, d_ = x.shape
    t_ = b_ * s_
    x2, tgt = x.reshape(t_, d_), target.reshape(t_, d_)
    xb = x2.astype(BF16)
    w1i = weight("ffn1_w_in", xb)
    g1, u1, a1 = _ffn_in_fwd(xb, w1i, "ffn1_in")
    w1o = weight("ffn1_w_out", a1)
    r1, h1, h1b = _mm_res_ln_fwd(a1, w1o, x2, sp["ln1_g"], sp["ln1_b"], 0.5, "ffn1_out_ln1")
    wmx = weight("mix_w_in", h1b)
    proj = _mm_nt(h1b, wmx, F32, "mix_in")
    ya_in, o_sum = _hgrn_fwd(proj, sp["hgrn_lb_fwd"], sp["hgrn_lb_bwd"], sp["hgrn_norm_g"], b_, s_, d_, "hgrn_fwd")
    tabs = _rope_tables(s_)
    outs, lses = [], []
    group_col = [(5 * d_ + gi * QKV_GROUP) // HEAD for gi in range(len(ATTN_GROUPS))]
    for gi, (_, dil) in enumerate(ATTN_GROUPS):
        o_g, lse_g = _attn_fwd(proj, tabs, b_, s_, group_col[gi], dil, f"attn_fwd_{gi}")
        outs.append(o_g)
        lses.append(lse_g)
    ob = _combine_fwd(outs, lses, "attn_combine")
    wa, wb = weight("w_branch_a", ya_in), weight("w_branch_b", ob)
    ya, yb, z = _gate_out_fwd(ya_in, ob, wa, wb, proj, d_, "branch_gate")
    wo = weight("mix_w_out", z)
    r2, h2, h2b = _mm_res_ln_fwd(z, wo, h1, sp["ln2_g"], sp["ln2_b"], 1.0, "mix_out_ln2")
    w2i = weight("ffn2_w_in", h2b)
    g2, u2, a2 = _ffn_in_fwd(h2b, w2i, "ffn2_in")
    w2o = weight("ffn2_w_out", a2)
    dr3, dr3b, dg3, db3, loss = _mm_res_loss_bwd(a2, w2o, h2, tgt, sp["ln3_g"], sp["ln3_b"], 0.5, "ffn2_out_loss")
    loss = lax.psum(loss[0, 0], ("x", "y", "c"))
    dep = emit("ffn2_w_out", _mm_tn(a2, dr3b, 0.5, "d_ffn2_w_out")) + loss
    dgate2, dup2 = _ffn_mid_bwd(dr3b, w2o, g2, u2, 0.5, "ffn2_mid_bwd", dep)
    du2 = (dgate2, dup2)
    dep = emit("ffn2_w_in", _mm_tn(du2, h2b, 1.0, "d_ffn2_w_in"))
    dr2, dr2b, dg2, db2 = _mm_nn_res_lnbwd(du2, w2i, dr3, r2, sp["ln2_g"], "ffn2_in_bwd_ln2", dep)
    dep = emit("mix_w_out", _mm_tn(z, dr2b, 1.0, "d_mix_w_out"))
    dya, dyb, dga, dgb = _dz_gate_bwd(dr2b, wo, proj, ya, yb, d_, "branch_gate_bwd", dep)
    dep = emit("w_branch_a", _mm_tn(ya_in, dya, 1.0, "d_w_branch_a"))
    dya_in = _mm_nt(dya, wa, F32, "branch_a_bwd", dep)
    dep = emit("w_branch_b", _mm_tn(dyb, ob, 1.0, "d_w_branch_b"))
    dob = _mm_nn(dyb, wb, F32, "branch_b_bwd", dep)
    dhq, dhff, dhfb, dhi, dhog, dng, dlbf, dlbb = _hgrn_bwd(
        proj, sp["hgrn_lb_fwd"], sp["hgrn_lb_bwd"], sp["hgrn_norm_g"], o_sum, dya_in, b_, s_, d_, "hgrn_bwd")
    dogs, cgs = _combine_bwd(dob, outs, lses, "attn_combine_bwd")
    dqkv = []
    for gi, (_, dil) in enumerate(ATTN_GROUPS):
        dqkv += _attn_bwd(proj, tabs, dogs[gi], cgs[gi], lses[gi], b_, s_, group_col[gi], dil, f"attn_bwd_{gi}")
    dproj = jnp.concatenate([dhq, dhff, dhfb, dhi, dhog] + [t.astype(BF16) for t in dqkv] + [dga, dgb], axis=1)
    dep = emit("mix_w_in", _mm_tn(dproj, h1b, 1.0, "d_mix_w_in"))
    dr1, dr1b, dg1, db1 = _mm_nn_res_lnbwd(dproj, wmx, dr2, r1, sp["ln1_g"], "mix_in_bwd_ln1", dep)
    dep_small = emit_small({"ln1_g": dg1, "ln1_b": db1, "ln2_g": dg2, "ln2_b": db2, "ln3_g": dg3, "ln3_b": db3,
                            "hgrn_norm_g": dng, "hgrn_lb_fwd": dlbf, "hgrn_lb_bwd": dlbb})
    dep = emit("ffn1_w_out", _mm_tn(a1, dr1b, 0.5, "d_ffn1_w_out")) + dep_small
    dgate1, dup1 = _ffn_mid_bwd(dr1b, w1o, g1, u1, 0.5, "ffn1_mid_bwd", dep)
    du1 = (dgate1, dup1)
    dep = emit("ffn1_w_in", _mm_tn(du1, xb, 1.0, "d_ffn1_w_in"))
    grad_x = _mm_nn_res(du1, w1i, dr1, "ffn1_in_bwd", dep)
    return loss, grad_x.reshape(b_, s_, d_)


def _pack_small(vals):
    rows = jnp.concatenate([vals[n] for n in SMALL], axis=0)
    return jnp.pad(rows, ((0, SMALL_ROWS - rows.shape[0]), (0, 0)))


def _unpack_small(packed):
    out, r = {}, 0
    for n in SMALL:
        k = 2 if n.startswith("hgrn_lb") else 1
        out[n] = packed[r:r + k]
        r += k
    return out


def kernel(x, ffn1_w_in, ffn1_w_out, ln1_g, ln1_b, mix_w_in, hgrn_lb_fwd, hgrn_lb_bwd, hgrn_norm_g, w_branch_a, w_branch_b, mix_w_out, ln2_g, ln2_b, ffn2_w_in, ffn2_w_out, ln3_g, ln3_b, loss_target, m_ffn1_w_in, m_ffn1_w_out, m_ln1_g, m_ln1_b, m_mix_w_in, m_hgrn_lb_fwd, m_hgrn_lb_bwd, m_hgrn_norm_g, m_w_branch_a, m_w_branch_b, m_mix_w_out, m_ln2_g, m_ln2_b, m_ffn2_w_in, m_ffn2_w_out, m_ln3_g, m_ln3_b, v_ffn1_w_in, v_ffn1_w_out, v_ln1_g, v_ln1_b, v_mix_w_in, v_hgrn_lb_fwd, v_hgrn_lb_bwd, v_hgrn_norm_g, v_w_branch_a, v_w_branch_b, v_mix_w_out, v_ln2_g, v_ln2_b, v_ffn2_w_in, v_ffn2_w_out, v_ln3_g, v_ln3_b):
    args = dict(locals())
    big_w = {n: args[n][0] for n in BIG}
    sp = {n: args[n] for n in SMALL}
    def rows_bf16(n, zero=0.0):
        w = big_w[n] + zero
        return (w.T if n in TRANSPOSED else w).astype(BF16)

    first, rest = BIG[:2], BIG[2:]
    pending, token = _gather_start([rows_bf16(n) for n in first], "gather_start_ffn1")
    gathering = dict(zip(first, pending))
    pending, all_started = _gather_start([rows_bf16(n, token[0, 0]) for n in rest], "gather_start_rest")
    gathering.update(zip(rest, pending))
    scattering = {}

    def weight(n, after):
        if n == first[0]:
            after = all_started
        return _gather_end(gathering[n], after, n)

    def emit(n, grad):
        scattering[n], token = _scatter_start(grad, f"scatter_start_{n}")
        return token

    def emit_small(grads):
        pending, token = _gather_start([_pack_small(grads)], "gather_start_small")
        scattering["small"] = pending[0]
        return token

    loss, grad_x = _local_step(x, loss_target, weight, emit, emit_small, sp)
    out_g, out_d, out_m, out_v = {}, {}, {}, {}
    done = grad_x
    for n in ("ffn2_w_out", "ffn2_w_in", "mix_w_out", "w_branch_a", "w_branch_b", "mix_w_in", "small",
              "ffn1_w_out", "ffn1_w_in"):
        if n == "small":
            parts = _gather_end(scattering[n], done, n)
            res = _small_sum_adam(parts.reshape(N_DEV, SMALL_ROWS, parts.shape[1]), _pack_small(sp),
                                  _pack_small({n: args["m_" + n] for n in SMALL}),
                                  _pack_small({n: args["v_" + n] for n in SMALL}), "small_adam")
            sg, sd, sm, sv = (_unpack_small(r) for r in res)
            out_g.update(sg), out_d.update(sd), out_m.update(sm), out_v.update(sv)
            done = res[3]
            continue
        g = _sum_partials(_scatter_wait(scattering[n], done, f"scatter_wait_{n}"), f"sum_{n}")
        if n in TRANSPOSED:
            g = g.T
        d_w, m_w, v_w = _adam(big_w[n], g, args["m_" + n][0], args["v_" + n][0], f"adam_{n}")
        out_g[n], out_d[n], out_m[n], out_v[n] = g[None], d_w[None], m_w[None], v_w[None]
        done = v_w
    order = ("ffn1_w_in", "ffn1_w_out", "ln1_g", "ln1_b", "mix_w_in", "hgrn_lb_fwd", "hgrn_lb_bwd", "hgrn_norm_g",
             "w_branch_a", "w_branch_b", "mix_w_out", "ln2_g", "ln2_b", "ffn2_w_in", "ffn2_w_out", "ln3_g", "ln3_b")
    return (loss, grad_x, *[out_g[n] for n in order], *[out_d[n] for n in order],
            *[out_m[n] for n in order], *[out_v[n] for n in order])
```

```python
import jax
import jax.numpy as jnp
from jax import lax
from jax.experimental import pallas as pl
from jax.experimental.pallas import tpu as pltpu

F32 = jnp.float32
BF16 = jnp.bfloat16

N_DEV = 8
HEAD = 128
ATTN_GROUPS = ((128, 1), (512, 4), (2048, 16))
ATTN_HEADS = 4
ATTN_HALF = 64
QKV_GROUP = 3 * ATTN_HEADS * HEAD
QKV_WIDTH = len(ATTN_GROUPS) * QKV_GROUP
ATTN_OUT = ATTN_HEADS * HEAD
ROPE_THETA = 500000.0
ROPE_DIM = HEAD // 4
ALPHA = 2.0 ** 0.25
LN_EPS = 1e-5
NEG_INF = -1e30
ADAM_LR, ADAM_B1, ADAM_B2, ADAM_EPS, ADAM_WD, ADAM_STEP = 0.001, 0.9, 0.999, 1e-08, 0.01, 10
VMEM_LIMIT = 56 * 1024 * 1024

NT = (((1,), (1,)), ((), ()))
NN = (((1,), (0,)), ((), ()))
TN = (((0,), (0,)), ((), ()))
MESH = pl.DeviceIdType.MESH


def _dot(a, b, dims):
    return lax.dot_general(a, b, dims, preferred_element_type=F32)


def _tile(n, pref, mult=128):
    if n <= pref:
        return n
    t = (pref // mult) * mult
    while t >= mult:
        if n % t == 0:
            return t
        t -= mult
    return n


def _tile_multi(ns, pref, mult=128):
    t = (pref // mult) * mult
    while t >= mult:
        if all(n % t == 0 for n in ns):
            return t
        t -= mult
    raise ValueError(f"no common tile for {ns}")


def _params(**kw):
    return pltpu.CompilerParams(vmem_limit_bytes=VMEM_LIMIT, **kw)


def _after(body, n_in, dep):
    if dep is None:
        return body, [], []

    def wrapped(*refs):
        body(*refs[:n_in], *refs[n_in + 1:])

    return wrapped, [pl.BlockSpec(dep.shape, lambda *_: (0,) * dep.ndim)], [dep]


def _pieces(a):
    pieces = tuple(a) if isinstance(a, (tuple, list)) else (a,)
    assert all(p.shape == pieces[0].shape for p in pieces)
    return pieces, pieces[0].shape[0], pieces[0].shape[1], len(pieces)


def _for_piece(step, p, per, npc, fn):
    if npc == 1:
        fn()
    else:
        pl.when((step >= p * per) & (step < (p + 1) * per))(fn)


def _sigmoid(x):
    return jax.nn.sigmoid(x)


def _dsilu(x, s):
    return s * (1.0 + x * (1.0 - s))


def _ln_stats(r):
    mu = jnp.mean(r, axis=-1, keepdims=True)
    xc = r - mu
    var = jnp.mean(xc * xc, axis=-1, keepdims=True)
    rstd = lax.rsqrt(var + LN_EPS)
    return xc * rstd, rstd


def _ln_bwd(dy, xhat, rstd, g):
    dyg = dy * g
    m1 = jnp.mean(dyg, axis=-1, keepdims=True)
    m2 = jnp.mean(dyg * xhat, axis=-1, keepdims=True)
    return rstd * (dyg - m1 - xhat * m2)


ROW_TILE = 1024
SUB_ROWS = 256


def _once(shape, index_map):
    return pl.BlockSpec(shape, index_map, pipeline_mode=pl.Buffered(1))


def _for_row_blocks(tm, fn):
    sub = SUB_ROWS if tm % SUB_ROWS == 0 else tm

    def step(s, carry):
        fn(pl.ds(pl.multiple_of(s * sub, sub), sub))
        return carry

    lax.fori_loop(0, tm // sub, step, 0)


def _row_runs(tm):
    sub = SUB_ROWS if tm % SUB_ROWS == 0 else tm
    return [slice(s, s + sub) for s in range(0, tm, sub)]


def _ffn_in_fwd(xb, w_t, name):
    t_, d_ = xb.shape
    f_ = w_t.shape[0] // 2
    tm, tn = _tile(t_, ROW_TILE, 8), _tile(f_, 512)
    nj = f_ // tn

    def body(x_ref, wg_ref, wu_ref, g_ref, u_ref, a_ref):
        wg, wu = wg_ref[...], wu_ref[...]
        for rows in _row_runs(tm):
            x = x_ref[rows, :]
            g = _dot(x, wg, NT)
            u = _dot(x, wu, NT)
            g_ref[rows, :] = g.astype(BF16)
            u_ref[rows, :] = u.astype(BF16)
            a_ref[rows, :] = (g * _sigmoid(g) * u).astype(BF16)

    return pl.pallas_call(
        body, name=name, grid=(t_ // tm, nj),
        in_specs=[pl.BlockSpec((tm, d_), lambda i, j: (i, 0)),
                  pl.BlockSpec((tn, d_), lambda i, j: (j, 0)),
                  pl.BlockSpec((tn, d_), lambda i, j: (j + nj, 0))],
        out_specs=[pl.BlockSpec((tm, tn), lambda i, j: (i, j))] * 3,
        out_shape=[jax.ShapeDtypeStruct((t_, f_), BF16)] * 3,
        compiler_params=_params(),
    )(xb, w_t, w_t)


WHOLE_WEIGHT_BYTES = 24 * 1024 * 1024


def _k_plan(t_, k_, d_):
    if k_ * d_ * 2 > WHOLE_WEIGHT_BYTES:
        return False, _tile(t_, ROW_TILE, 8), _tile(k_, 512), _once
    if k_ <= 2048:
        return True, _tile(t_, ROW_TILE, 8), k_, _once
    return True, _tile(t_, SUB_ROWS, 8), k_, pl.BlockSpec


def _mm_res_ln_fwd(a, w, res, g, b, scale, name):
    t_, k_ = a.shape
    d_ = w.shape[1]
    whole, tm, tk, row_spec = _k_plan(t_, k_, d_)
    nk = k_ // tk

    def body(a_ref, w_ref, res_ref, g_ref, b_ref, r_ref, h_ref, hb_ref, *scratch):
        k = pl.program_id(1)
        if not whole:
            acc, = scratch

            @pl.when(k == 0)
            def _():
                acc[...] = jnp.zeros_like(acc)

            acc[...] += _dot(a_ref[...], w_ref[...], NN)

        @pl.when(k == nk - 1)
        def _():
            def rows_out(rows):
                prod = _dot(a_ref[rows, :], w_ref[...], NN) if whole else acc[rows, :]
                r = ALPHA * res_ref[rows, :] + scale * prod
                xhat, _ = _ln_stats(r)
                h = xhat * g_ref[...] + b_ref[...]
                r_ref[rows, :] = r
                h_ref[rows, :] = h
                hb_ref[rows, :] = h.astype(BF16)

            _for_row_blocks(tm, rows_out)

    row = row_spec((tm, d_), lambda i, k: (i, 0))
    vec = pl.BlockSpec((1, d_), lambda i, k: (0, 0))
    w_spec = _once((tk, d_), lambda i, k: (0, 0)) if whole else pl.BlockSpec((tk, d_), lambda i, k: (k, 0))
    return pl.pallas_call(
        body, name=name, grid=(t_ // tm, nk),
        in_specs=[pl.BlockSpec((tm, tk), lambda i, k: (i, k)), w_spec, row, vec, vec],
        out_specs=[row, row, row],
        out_shape=[jax.ShapeDtypeStruct((t_, d_), F32), jax.ShapeDtypeStruct((t_, d_), F32),
                   jax.ShapeDtypeStruct((t_, d_), BF16)],
        scratch_shapes=[] if whole else [pltpu.VMEM((tm, d_), F32)],
        compiler_params=_params(),
    )(a, w, res, g, b)


def _mm_res_loss_bwd(a, w, res, target, g, b, scale, name):
    t_, k_ = a.shape
    d_ = w.shape[1]
    whole, tm, tk, row_spec = _k_plan(t_, k_, d_)
    nk = k_ // tk

    def body(a_ref, w_ref, res_ref, t_ref, g_ref, b_ref, dr_ref, drb_ref, dg_ref, db_ref, loss_ref, *scratch):
        i, k = pl.program_id(0), pl.program_id(1)

        @pl.when((i == 0) & (k == 0))
        def _():
            dg_ref[...] = jnp.zeros_like(dg_ref)
            db_ref[...] = jnp.zeros_like(db_ref)
            loss_ref[...] = jnp.zeros_like(loss_ref)

        if not whole:
            acc, = scratch

            @pl.when(k == 0)
            def _():
                acc[...] = jnp.zeros_like(acc)

            acc[...] += _dot(a_ref[...], w_ref[...], NN)

        @pl.when(k == nk - 1)
        def _():
            def rows_out(rows):
                prod = _dot(a_ref[rows, :], w_ref[...], NN) if whole else acc[rows, :]
                r = ALPHA * res_ref[rows, :] + scale * prod
                xhat, rstd = _ln_stats(r)
                gain = g_ref[...]
                err = xhat * gain + b_ref[...] - t_ref[rows, :]
                loss_ref[...] += (0.5 / d_) * jnp.sum(err * err)
                dy = err * (1.0 / d_)
                dr = _ln_bwd(dy, xhat, rstd, gain)
                dr_ref[rows, :] = dr
                drb_ref[rows, :] = dr.astype(BF16)
                dg_ref[...] += jnp.sum(dy * xhat, axis=0, keepdims=True)
                db_ref[...] += jnp.sum(dy, axis=0, keepdims=True)

            _for_row_blocks(tm, rows_out)

    row = row_spec((tm, d_), lambda i, k: (i, 0))
    vec = pl.BlockSpec((1, d_), lambda i, k: (0, 0))
    w_spec = _once((tk, d_), lambda i, k: (0, 0)) if whole else pl.BlockSpec((tk, d_), lambda i, k: (k, 0))
    return pl.pallas_call(
        body, name=name, grid=(t_ // tm, nk),
        in_specs=[pl.BlockSpec((tm, tk), lambda i, k: (i, k)), w_spec, row, row, vec, vec],
        out_specs=[row, row, vec, vec, pl.BlockSpec((1, HEAD), lambda i, k: (0, 0))],
        out_shape=[jax.ShapeDtypeStruct((t_, d_), F32), jax.ShapeDtypeStruct((t_, d_), BF16),
                   jax.ShapeDtypeStruct((1, d_), F32), jax.ShapeDtypeStruct((1, d_), F32),
                   jax.ShapeDtypeStruct((1, HEAD), F32)],
        scratch_shapes=[] if whole else [pltpu.VMEM((tm, d_), F32)],
        compiler_params=_params(),
    )(a, w, res, target, g, b)


def _mm_nt(a, w_t, out_dtype, name, dep=None):
    t_, k_ = a.shape
    n_ = w_t.shape[0]
    tm, tn = _tile(t_, ROW_TILE, 8), _tile(n_, 512)

    def body(a_ref, w_ref, o_ref):
        o_ref[...] = _dot(a_ref[...], w_ref[...], NT).astype(out_dtype)

    body, dep_specs, deps = _after(body, 2, dep)
    return pl.pallas_call(
        body, name=name, grid=(t_ // tm, n_ // tn),
        in_specs=[pl.BlockSpec((tm, k_), lambda i, j: (i, 0)),
                  pl.BlockSpec((tn, k_), lambda i, j: (j, 0)), *dep_specs],
        out_specs=pl.BlockSpec((tm, tn), lambda i, j: (i, j)),
        out_shape=jax.ShapeDtypeStruct((t_, n_), out_dtype),
        compiler_params=_params(),
    )(a, w_t, *deps)


def _mm_nn(a, w, out_dtype, name, dep=None):
    t_, k_ = a.shape
    n_ = w.shape[1]
    tm, tn = _tile(t_, ROW_TILE, 8), _tile(n_, 512)

    def body(a_ref, w_ref, o_ref):
        o_ref[...] = _dot(a_ref[...], w_ref[...], NN).astype(out_dtype)

    body, dep_specs, deps = _after(body, 2, dep)
    return pl.pallas_call(
        body, name=name, grid=(t_ // tm, n_ // tn),
        in_specs=[pl.BlockSpec((tm, k_), lambda i, j: (i, 0)),
                  pl.BlockSpec((k_, tn), lambda i, j: (0, j)), *dep_specs],
        out_specs=pl.BlockSpec((tm, tn), lambda i, j: (i, j)),
        out_shape=jax.ShapeDtypeStruct((t_, n_), out_dtype),
        compiler_params=_params(),
    )(a, w, *deps)


def _mm_tn(a, b, scale, name):
    pieces, t_, mp, npc = _pieces(a)
    n_ = b.shape[1]
    tm = _tile(mp, 512)
    per = mp // tm

    def body(*refs):
        b_ref, o_ref = refs[npc], refs[npc + 1]
        for p in range(npc):
            def piece_out(p=p):
                o_ref[...] = (scale * _dot(refs[p][...], b_ref[...], TN)).astype(BF16)

            _for_piece(pl.program_id(0), p, per, npc, piece_out)

    return pl.pallas_call(
        body, name=name, grid=(npc * per,),
        in_specs=[pl.BlockSpec((t_, tm), lambda i, p=p: (0, jnp.clip(i - p * per, 0, per - 1))) for p in range(npc)]
        + [_once((t_, n_), lambda i: (0, 0))],
        out_specs=pl.BlockSpec((tm, n_), lambda i: (i, 0)),
        out_shape=jax.ShapeDtypeStruct((npc * mp, n_), BF16),
        compiler_params=_params(),
    )(*pieces, b)


def _gate_out_fwd(ya_in, ob, wa, wb_t, proj, d_, name):
    t_ = ya_in.shape[0]
    goff = 5 * d_ + QKV_WIDTH
    tm, tn = _tile(t_, ROW_TILE, 8), _tile_multi([d_, goff], 512)
    ja, jb = goff // tn, (goff + d_) // tn

    def body(ya_ref, ob_ref, wa_ref, wb_ref, ga_ref, gb_ref, yao_ref, ybo_ref, z_ref):
        wa, wb = wa_ref[...], wb_ref[...]
        for rows in _row_runs(tm):
            y_a = _dot(ya_ref[rows, :], wa, NN)
            y_b = _dot(ob_ref[rows, :], wb, NT)
            yao_ref[rows, :] = y_a.astype(BF16)
            ybo_ref[rows, :] = y_b.astype(BF16)
            z_ref[rows, :] = (_sigmoid(ga_ref[rows, :]) * y_a + _sigmoid(gb_ref[rows, :]) * y_b).astype(BF16)

    tile = pl.BlockSpec((tm, tn), lambda i, j: (i, j))
    return pl.pallas_call(
        body, name=name, grid=(t_ // tm, d_ // tn),
        in_specs=[pl.BlockSpec((tm, d_), lambda i, j: (i, 0)),
                  pl.BlockSpec((tm, ATTN_OUT), lambda i, j: (i, 0)),
                  pl.BlockSpec((d_, tn), lambda i, j: (0, j)),
                  pl.BlockSpec((tn, ATTN_OUT), lambda i, j: (j, 0)),
                  pl.BlockSpec((tm, tn), lambda i, j: (i, ja + j)),
                  pl.BlockSpec((tm, tn), lambda i, j: (i, jb + j))],
        out_specs=[tile, tile, tile],
        out_shape=[jax.ShapeDtypeStruct((t_, d_), BF16)] * 3,
        compiler_params=_params(),
    )(ya_in, ob, wa, wb_t, proj, proj)


def _ffn_mid_bwd(drb, w_out, gate, up, scale, name, dep=None):
    t_, d_ = drb.shape
    f_ = w_out.shape[0]
    tm, tn = _tile(t_, ROW_TILE, 8), _tile(f_, 512)

    def body(dr_ref, w_ref, g_ref, u_ref, dg_ref, du_ref):
        w = w_ref[...]
        for rows in _row_runs(tm):
            da = scale * _dot(dr_ref[rows, :], w, NT)
            g = g_ref[rows, :].astype(F32)
            s = _sigmoid(g)
            dg_ref[rows, :] = (da * u_ref[rows, :].astype(F32) * _dsilu(g, s)).astype(BF16)
            du_ref[rows, :] = (da * g * s).astype(BF16)

    body, dep_specs, deps = _after(body, 4, dep)
    tile = pl.BlockSpec((tm, tn), lambda i, j: (i, j))
    return pl.pallas_call(
        body, name=name, grid=(t_ // tm, f_ // tn),
        in_specs=[pl.BlockSpec((tm, d_), lambda i, j: (i, 0)),
                  pl.BlockSpec((tn, d_), lambda i, j: (j, 0)), tile, tile, *dep_specs],
        out_specs=[tile, tile],
        out_shape=[jax.ShapeDtypeStruct((t_, f_), BF16)] * 2,
        compiler_params=_params(),
    )(drb, w_out, gate, up, *deps)


def _mm_nn_res_lnbwd(a, w, dres, r, g, name, dep=None):
    pieces, t_, kp, npc = _pieces(a)
    d_ = w.shape[1]
    tm, tk = _tile(t_, ROW_TILE, 8), _tile(kp, 512)
    per = kp // tk
    nk = npc * per

    def body(*refs):
        w_ref, dres_ref, r_ref, g_ref, dr_ref, drb_ref, dg_ref, db_ref, acc = refs[npc:]
        i, k = pl.program_id(0), pl.program_id(1)

        @pl.when(k == 0)
        def _():
            acc[...] = jnp.zeros_like(acc)

        @pl.when((i == 0) & (k == 0))
        def _():
            dg_ref[...] = jnp.zeros_like(dg_ref)
            db_ref[...] = jnp.zeros_like(db_ref)

        for p in range(npc):
            def piece_in(p=p):
                acc[...] += _dot(refs[p][...], w_ref[...], NN)

            _for_piece(k, p, per, npc, piece_in)

        @pl.when(k == nk - 1)
        def _():
            def rows_out(rows):
                dy = acc[rows, :] + ALPHA * dres_ref[rows, :]
                xhat, rstd = _ln_stats(r_ref[rows, :])
                dr = _ln_bwd(dy, xhat, rstd, g_ref[...])
                dr_ref[rows, :] = dr
                drb_ref[rows, :] = dr.astype(BF16)
                dg_ref[...] += jnp.sum(dy * xhat, axis=0, keepdims=True)
                db_ref[...] += jnp.sum(dy, axis=0, keepdims=True)

            _for_row_blocks(tm, rows_out)

    body, dep_specs, deps = _after(body, npc + 4, dep)
    row = _once((tm, d_), lambda i, k: (i, 0))
    vec = pl.BlockSpec((1, d_), lambda i, k: (0, 0))
    return pl.pallas_call(
        body, name=name, grid=(t_ // tm, nk),
        in_specs=[pl.BlockSpec((tm, tk), lambda i, k, p=p: (i, jnp.clip(k - p * per, 0, per - 1))) for p in range(npc)]
        + [pl.BlockSpec((tk, d_), lambda i, k: (k, 0)), row, row, vec, *dep_specs],
        out_specs=[row, row, vec, vec],
        out_shape=[jax.ShapeDtypeStruct((t_, d_), F32), jax.ShapeDtypeStruct((t_, d_), BF16),
                   jax.ShapeDtypeStruct((1, d_), F32), jax.ShapeDtypeStruct((1, d_), F32)],
        scratch_shapes=[pltpu.VMEM((tm, d_), F32)],
        compiler_params=_params(),
    )(*pieces, w, dres, r, g, *deps)


def _mm_nn_res(a, w, dres, name, dep=None):
    pieces, t_, kp, npc = _pieces(a)
    d_ = w.shape[1]
    tm, tk = _tile(t_, ROW_TILE, 8), _tile(kp, 512)
    per = kp // tk
    nk = npc * per

    def body(*refs):
        w_ref, dres_ref, o_ref, acc = refs[npc:]
        k = pl.program_id(1)

        @pl.when(k == 0)
        def _():
            acc[...] = jnp.zeros_like(acc)

        for p in range(npc):
            def piece_in(p=p):
                acc[...] += _dot(refs[p][...], w_ref[...], NN)

            _for_piece(k, p, per, npc, piece_in)

        @pl.when(k == nk - 1)
        def _():
            def rows_out(rows):
                o_ref[rows, :] = acc[rows, :] + ALPHA * dres_ref[rows, :]

            _for_row_blocks(tm, rows_out)

    body, dep_specs, deps = _after(body, npc + 2, dep)
    row = _once((tm, d_), lambda i, k: (i, 0))
    return pl.pallas_call(
        body, name=name, grid=(t_ // tm, nk),
        in_specs=[pl.BlockSpec((tm, tk), lambda i, k, p=p: (i, jnp.clip(k - p * per, 0, per - 1))) for p in range(npc)]
        + [pl.BlockSpec((tk, d_), lambda i, k: (k, 0)), row, *dep_specs],
        out_specs=row,
        out_shape=jax.ShapeDtypeStruct((t_, d_), F32),
        scratch_shapes=[pltpu.VMEM((tm, d_), F32)],
        compiler_params=_params(),
    )(*pieces, w, dres, *deps)


def _dz_gate_bwd(drb, w_out, proj, ya, yb, d_, name, dep=None):
    t_ = drb.shape[0]
    goff = 5 * d_ + QKV_WIDTH
    tm, tn = _tile(t_, ROW_TILE, 8), _tile_multi([d_, goff], 512)
    ja, jb = goff // tn, (goff + d_) // tn

    def body(dr_ref, w_ref, ga_ref, gb_ref, ya_ref, yb_ref, dya_ref, dyb_ref, dga_ref, dgb_ref):
        w = w_ref[...]
        for rows in _row_runs(tm):
            dz = _dot(dr_ref[rows, :], w, NT)
            sa, sb = _sigmoid(ga_ref[rows, :]), _sigmoid(gb_ref[rows, :])
            dya_ref[rows, :] = (dz * sa).astype(BF16)
            dyb_ref[rows, :] = (dz * sb).astype(BF16)
            dga_ref[rows, :] = (dz * ya_ref[rows, :].astype(F32) * sa * (1.0 - sa)).astype(BF16)
            dgb_ref[rows, :] = (dz * yb_ref[rows, :].astype(F32) * sb * (1.0 - sb)).astype(BF16)

    body, dep_specs, deps = _after(body, 6, dep)
    tile = pl.BlockSpec((tm, tn), lambda i, j: (i, j))
    return pl.pallas_call(
        body, name=name, grid=(t_ // tm, d_ // tn),
        in_specs=[pl.BlockSpec((tm, d_), lambda i, j: (i, 0)),
                  pl.BlockSpec((tn, d_), lambda i, j: (j, 0)),
                  pl.BlockSpec((tm, tn), lambda i, j: (i, ja + j)),
                  pl.BlockSpec((tm, tn), lambda i, j: (i, jb + j)), tile, tile, *dep_specs],
        out_specs=[tile] * 4,
        out_shape=[jax.ShapeDtypeStruct((t_, d_), BF16)] * 4,
        compiler_params=_params(),
    )(drb, w_out, proj, proj, ya, yb, *deps)


def _lower_bound(tab):
    return _sigmoid(tab[0:1, :] - tab[1:2, :])


FWD_BLOCK = 128
BWD_BLOCK = 128


def _block_scan(x, row, reverse, size, blk):
    s = 1
    while s < blk:
        if reverse:
            x = x + jnp.where(row < blk - s, pltpu.roll(x, size - s, 0), 0.0)
        else:
            x = x + jnp.where(row >= s, pltpu.roll(x, s, 0), 0.0)
        s *= 2
    return x


def _block_exps(l, reverse):
    blk = l.shape[0]
    half = blk // 2
    first = lax.broadcasted_iota(jnp.int32, (blk, HEAD), 0) < half
    q1, q3 = half // 2, half + half // 2
    if reverse:
        rho1, rho2, lh, ltot = l[q1:q1 + 1], l[q3:q3 + 1], l[half:half + 1], l[0:1]
    else:
        rho1, rho2, lh, ltot = l[q1 - 1:q1], l[q3 - 1:q3], l[half - 1:half], l[blk - 1:blk]
    ref = jnp.where(first, rho1, rho2)
    query_half = first if reverse else jnp.logical_not(first)
    e2 = jnp.where(query_half, jnp.exp(jnp.minimum(l - lh, 0.0)), 0.0)
    e1 = jnp.where(query_half, 0.0, jnp.exp(jnp.minimum(lh - l, 0.0)))
    return (jnp.exp(l - ref), jnp.exp(ref - l), e2, e1, jnp.exp(l), jnp.exp(ltot - l),
            jnp.exp(ltot), jnp.exp(lh), jnp.exp(ltot - lh))


def _half_mask(reverse, blk):
    r = lax.broadcasted_iota(jnp.int32, (blk, blk), 0)
    c = lax.broadcasted_iota(jnp.int32, (blk, blk), 1)
    same = (r < blk // 2) == (c < blk // 2)
    return same & ((c >= r) if reverse else (r >= c))


def _hgrn_fwd(proj, lbf, lbb, ng, b_, s_, d_, name):
    h_ = d_ // HEAD
    BLOCK = min(FWD_BLOCK, s_)
    nb = s_ // BLOCK

    def body(hq_ref, hff_ref, hfb_ref, hi_ref, hog_ref, lbf_ref, lbb_ref, ng_ref, ya_ref, o_ref,
             q_s, k_s, l_s, of_s, oi_s, qd_s, u_s, st_s, dec_s):
        row = lax.broadcasted_iota(jnp.int32, (s_, HEAD), 0) % BLOCK
        hq = hq_ref[...]
        q_s[...] = hq * _sigmoid(hq)
        for reverse, hf_ref, lb_ref in ((False, hff_ref, lbf_ref), (True, hfb_ref, lbb_ref)):
            lb = _lower_bound(lb_ref[...])
            f = lb + (1.0 - lb) * _sigmoid(hf_ref[...])
            k_s[...] = 1.0 - f
            l_s[...] = _block_scan(jnp.log(f), row, reverse, s_, BLOCK)
            mask = _half_mask(reverse, BLOCK)

            def inside(n, carry, reverse=reverse, mask=mask):
                sl = pl.ds(pl.multiple_of(n * BLOCK, BLOCK), BLOCK)
                eq, ek, e2, e1, el, ee, dec, _, _ = _block_exps(l_s[sl, :], reverse)
                qc, kc = q_s[sl, :], k_s[sl, :]
                vb = hi_ref[sl, :].astype(BF16)
                a = jnp.where(mask, _dot((qc * eq).astype(BF16), (kc * ek).astype(BF16), NT), 0.0)
                a = a + _dot((qc * e2).astype(BF16), (kc * e1).astype(BF16), NT)
                oi_s[sl, :] = _dot(a.astype(BF16), vb, NN)
                qd_s[sl, :] = (qc * el).astype(BF16)
                u_s[n] = _dot(vb, (kc * ee).astype(BF16), TN)
                dec_s[n] = jnp.broadcast_to(dec, (8, HEAD))
                return carry

            lax.fori_loop(0, nb, inside, 0, unroll=8)

            def carry_state(n, st, reverse=reverse):
                idx = (nb - 1 - n) if reverse else n
                st_s[idx] = st.astype(BF16)
                return st * dec_s[idx][0:1, :] + u_s[idx]

            lax.fori_loop(0, nb, carry_state, jnp.zeros((HEAD, HEAD), F32))

            def across(n, carry, reverse=reverse):
                sl = pl.ds(pl.multiple_of(n * BLOCK, BLOCK), BLOCK)
                o_dir = oi_s[sl, :] + _dot(qd_s[sl, :], st_s[n], NT)
                if not reverse:
                    of_s[sl, :] = o_dir
                else:
                    o = of_s[sl, :] + o_dir
                    o_ref[sl, :] = o
                    nrm = o * lax.rsqrt(jnp.mean(o * o, axis=-1, keepdims=True) + LN_EPS)
                    hog = hog_ref[sl, :]
                    ya_ref[sl, :] = (nrm * ng_ref[...] * hog * _sigmoid(hog)).astype(BF16)
                return carry

            lax.fori_loop(0, nb, across, 0, unroll=8)

    def col(part):
        return pl.BlockSpec((s_, HEAD), lambda h, b, part=part: (b, part * h_ + h))

    tab = pl.BlockSpec((2, HEAD), lambda h, b: (0, h))
    out = pl.BlockSpec((s_, HEAD), lambda h, b: (b, h))
    return pl.pallas_call(
        body, name=name, grid=(h_, b_),
        in_specs=[col(0), col(1), col(2), col(3), col(4), tab, tab,
                  pl.BlockSpec((1, HEAD), lambda h, b: (0, h))],
        out_specs=[out, out],
        out_shape=[jax.ShapeDtypeStruct((b_ * s_, d_), BF16), jax.ShapeDtypeStruct((b_ * s_, d_), F32)],
        scratch_shapes=[pltpu.VMEM((s_, HEAD), F32)] * 5 + [
            pltpu.VMEM((s_, HEAD), BF16), pltpu.VMEM((nb, HEAD, HEAD), F32), pltpu.VMEM((nb, HEAD, HEAD), BF16),
            pltpu.VMEM((nb, 8, HEAD), F32)],
        compiler_params=_params(),
    )(proj, proj, proj, proj, proj, lbf, lbb, ng)


def _hgrn_bwd(proj, lbf, lbb, ng, o_sum, dya, b_, s_, d_, name):
    h_ = d_ // HEAD
    BLOCK = min(BWD_BLOCK, s_)
    nb = s_ // BLOCK

    def body(hq_ref, hff_ref, hfb_ref, hi_ref, hog_ref, lbf_ref, lbb_ref, ng_ref, o_ref, dya_ref,
             dhq_ref, dhff_ref, dhfb_ref, dhi_ref, dhog_ref, dng_ref, dlbf_ref, dlbb_ref,
             q_s, k_s, l_s, do_s, dq_s, dv_s, dl_s, dk_s, u_s, w_s, st_s, dst_s, dec_s):
        b = pl.program_id(1)

        @pl.when(b == 0)
        def _():
            dng_ref[...] = jnp.zeros_like(dng_ref)
            dlbf_ref[...] = jnp.zeros_like(dlbf_ref)
            dlbb_ref[...] = jnp.zeros_like(dlbb_ref)

        row = lax.broadcasted_iota(jnp.int32, (s_, HEAD), 0) % BLOCK
        brow = lax.broadcasted_iota(jnp.int32, (BLOCK, HEAD), 0)
        hq = hq_ref[...]
        q_s[...] = hq * _sigmoid(hq)
        o = o_ref[...]
        rinv = lax.rsqrt(jnp.mean(o * o, axis=-1, keepdims=True) + LN_EPS)
        nrm = o * rinv
        hog = hog_ref[...]
        so = _sigmoid(hog)
        gain = ng_ref[...]
        dy = dya_ref[...]
        dhog_ref[...] = (dy * nrm * gain * _dsilu(hog, so)).astype(BF16)
        dng_ref[...] += jnp.sum(dy * nrm * hog * so, axis=0, keepdims=True)
        dn = dy * gain * hog * so
        do_s[...] = rinv * (dn - nrm * jnp.mean(dn * nrm, axis=-1, keepdims=True))

        for reverse, hf_ref, lb_ref, dhf_ref, dlb_ref in (
                (False, hff_ref, lbf_ref, dhff_ref, dlbf_ref), (True, hfb_ref, lbb_ref, dhfb_ref, dlbb_ref)):
            lb = _lower_bound(lb_ref[...])
            sf = _sigmoid(hf_ref[...])
            f = lb + (1.0 - lb) * sf
            k_s[...] = 1.0 - f
            l_s[...] = _block_scan(jnp.log(f), row, reverse, s_, BLOCK)
            mask = _half_mask(reverse, BLOCK)
            total_row = 0 if reverse else BLOCK - 1
            key_end = BLOCK // 2 if reverse else BLOCK // 2 - 1

            def prepare(n, carry, reverse=reverse):
                sl = pl.ds(pl.multiple_of(n * BLOCK, BLOCK), BLOCK)
                _, _, _, _, el, ee, dec, _, _ = _block_exps(l_s[sl, :], reverse)
                vb = hi_ref[sl, :].astype(BF16)
                u_s[n] = _dot(vb, (k_s[sl, :] * ee).astype(BF16), TN)
                w_s[n] = _dot(do_s[sl, :].astype(BF16), (q_s[sl, :] * el).astype(BF16), TN)
                dec_s[n] = jnp.broadcast_to(dec, (8, HEAD))
                return carry

            lax.fori_loop(0, nb, prepare, 0, unroll=8)

            def carry_state(n, st, reverse=reverse):
                idx = (nb - 1 - n) if reverse else n
                st_s[idx] = st
                return st * dec_s[idx][0:1, :] + u_s[idx]

            lax.fori_loop(0, nb, carry_state, jnp.zeros((HEAD, HEAD), F32))

            def carry_grad(n, dst, reverse=reverse):
                idx = n if reverse else (nb - 1 - n)
                dst_s[idx] = dst
                return dst * dec_s[idx][0:1, :] + w_s[idx]

            lax.fori_loop(0, nb, carry_grad, jnp.zeros((HEAD, HEAD), F32))

            def inside(n, carry, reverse=reverse, mask=mask, total_row=total_row, key_end=key_end):
                sl = pl.ds(pl.multiple_of(n * BLOCK, BLOCK), BLOCK)
                eq, ek, e2, e1, el, ee, dec, dec_key, dec_query = _block_exps(l_s[sl, :], reverse)
                qc, kc = q_s[sl, :], k_s[sl, :]
                vb = hi_ref[sl, :].astype(BF16)
                dob = do_s[sl, :].astype(BF16)
                qt, kt, q2, k1 = ((qc * eq).astype(BF16), (kc * ek).astype(BF16),
                                  (qc * e2).astype(BF16), (kc * e1).astype(BF16))
                kend = kc * ee
                st0, dst1 = st_s[n], dst_s[n]
                dstb = dst1.astype(BF16)
                a = jnp.where(mask, _dot(qt, kt, NT), 0.0) + _dot(q2, k1, NT)
                da = _dot(dob, vb, NT)
                dab = da.astype(BF16)
                dad = jnp.where(mask, da, 0.0).astype(BF16)
                dqt, dkt = _dot(dad, kt, NN), _dot(dad, qt, TN)
                dq2, dk1 = _dot(dab, k1, NN), _dot(dab, q2, TN)
                dqd = _dot(dob, st0.astype(BF16), NN)
                dke = _dot(vb, dstb, NN)
                dv = _dot(a.astype(BF16), dob, TN) + _dot(kend.astype(BF16), dstb, NT)
                dq = dqt * eq + dq2 * e2 + dqd * el
                dk = dkt * ek + dk1 * e1 + dke * ee
                dtot = jnp.sum(dke * kend, axis=0, keepdims=True) + jnp.sum(dst1 * st0, axis=0, keepdims=True) * dec
                st_mid = st0 * dec_key + _dot(vb, k1, TN)
                dst_mid = dst1 * dec_query + _dot(dob, q2, TN)
                dmid = jnp.sum(dst_mid * st_mid, axis=0, keepdims=True)
                dl_s[sl, :] = (qc * dq - kc * dk + jnp.where(brow == total_row, dtot, 0.0)
                               + jnp.where(brow == key_end, dmid, 0.0))
                dk_s[sl, :] = dk
                if not reverse:
                    dq_s[sl, :] = dq
                    dv_s[sl, :] = dv
                else:
                    hqc = hq_ref[sl, :]
                    dhq_ref[sl, :] = ((dq_s[sl, :] + dq) * _dsilu(hqc, _sigmoid(hqc))).astype(BF16)
                    dhi_ref[sl, :] = (dv_s[sl, :] + dv).astype(BF16)
                return carry

            lax.fori_loop(0, nb, inside, 0, unroll=8)
            dlogf = _block_scan(dl_s[...], row % (BLOCK // 2), not reverse, s_, BLOCK // 2)
            df = dlogf / f - dk_s[...]
            dhf_ref[...] = (df * (1.0 - lb) * sf * (1.0 - sf)).astype(BF16)
            dlb = jnp.sum(df * (1.0 - sf), axis=0, keepdims=True) * lb * (1.0 - lb)
            dlb_ref[0:1, :] += dlb
            dlb_ref[1:2, :] -= dlb

    def col(part):
        return pl.BlockSpec((s_, HEAD), lambda h, b, part=part: (b, part * h_ + h))

    tab = pl.BlockSpec((2, HEAD), lambda h, b: (0, h))
    vec = pl.BlockSpec((1, HEAD), lambda h, b: (0, h))
    blk = pl.BlockSpec((s_, HEAD), lambda h, b: (b, h))
    act = jax.ShapeDtypeStruct((b_ * s_, d_), BF16)
    state = pltpu.VMEM((nb, HEAD, HEAD), F32)
    return pl.pallas_call(
        body, name=name, grid=(h_, b_),
        in_specs=[col(0), col(1), col(2), col(3), col(4), tab, tab, vec, blk, blk],
        out_specs=[blk] * 5 + [vec, tab, tab],
        out_shape=[act] * 5 + [jax.ShapeDtypeStruct((1, d_), F32), jax.ShapeDtypeStruct((2, d_), F32),
                               jax.ShapeDtypeStruct((2, d_), F32)],
        scratch_shapes=[pltpu.VMEM((s_, HEAD), F32)] * 8 + [state] * 4 + [pltpu.VMEM((nb, 8, HEAD), F32)],
        compiler_params=_params(),
    )(proj, proj, proj, proj, proj, lbf, lbb, ng, o_sum, dya)


def _rope_tables(s_):
    half = ROPE_DIM // 2
    inv_freq = ROPE_THETA ** (-jnp.arange(0, ROPE_DIM, 2, dtype=F32) / ROPE_DIM)
    ang = jnp.arange(s_, dtype=F32)[:, None] * inv_freq
    cos, sin = jnp.cos(ang), jnp.sin(ang)
    zeros = jnp.zeros((s_, HEAD - ROPE_DIM), F32)
    zh = jnp.zeros((s_, half), F32)
    c = jnp.concatenate([cos, cos, jnp.ones((s_, HEAD - ROPE_DIM), F32)], axis=1)
    s1 = jnp.concatenate([-sin, zh, zeros], axis=1)
    s2 = jnp.concatenate([zh, sin, zeros], axis=1)
    return c, s1, s2


def _rope(t, c, s1, s2):
    half = ROPE_DIM // 2
    return t * c + pltpu.roll(t, HEAD - half, 1) * s1 + pltpu.roll(t, half, 1) * s2


def _rope_bwd(dt, c, s1, s2):
    half = ROPE_DIM // 2
    return dt * c + pltpu.roll(dt * s1, half, 1) + pltpu.roll(dt * s2, HEAD - half, 1)


def _window_mask(r0, qb, wk, seg):
    row = lax.broadcasted_iota(jnp.int32, (qb, wk), 0)
    col = lax.broadcasted_iota(jnp.int32, (qb, wk), 1)
    kj = r0 - ATTN_HALF + col
    return (col - row >= 0) & (col - row <= 2 * ATTN_HALF) & (kj >= 0) & (kj < seg)


UNROLLED_BLOCKS = 8


def _classes_in_flight(dil, nq):
    par = 1
    while par < 4 and dil % (2 * par) == 0 and 2 * par * nq <= 8:
        par *= 2
    return par


def _block_start(i, qb):
    return i * qb if isinstance(i, int) else pl.multiple_of(i * qb, qb)


def _for_blocks(nq, step):
    if nq <= UNROLLED_BLOCKS:
        for i in range(nq):
            step(i, 0)
    else:
        lax.fori_loop(0, nq, step, 0, unroll=UNROLLED_BLOCKS)


def _attn_fwd(proj, tabs, b_, s_, col0, dil, name):
    seg = s_ // dil
    qb = min(128, seg)
    nq, wk = seg // qb, qb + 2 * ATTN_HALF
    scale = HEAD ** -0.5

    par = _classes_in_flight(dil, nq)

    def body(q_ref, k_ref, v_ref, c_ref, s1_ref, s2_ref, o_ref, lse_ref, q_all, k_all, v_all):
        k_all[...] = jnp.zeros_like(k_all)
        v_all[...] = jnp.zeros_like(v_all)

        def residue(r, q_s, k_s, v_s):
            cls = pl.ds(r, seg, stride=dil)
            c, s1, s2 = c_ref[cls, :], s1_ref[cls, :], s2_ref[cls, :]
            q_s[...] = _rope(q_ref[cls, :], c, s1, s2).astype(BF16)
            k_s[ATTN_HALF:ATTN_HALF + seg, :] = _rope(k_ref[cls, :], c, s1, s2).astype(BF16)
            v_s[ATTN_HALF:ATTN_HALF + seg, :] = v_ref[cls, :].astype(BF16)

            def step(i, carry):
                r0 = _block_start(i, qb)
                sc = _dot(q_s[pl.ds(r0, qb), :], k_s[pl.ds(r0, wk), :], NT) * scale
                sc = jnp.where(_window_mask(r0, qb, wk, seg), sc, NEG_INF)
                m = jnp.max(sc, axis=-1, keepdims=True)
                p = jnp.exp(sc - m)
                den = jnp.sum(p, axis=-1, keepdims=True)
                rows = pl.ds(r + r0 * dil, qb, stride=dil)
                o_ref[rows, :] = _dot(p.astype(BF16), v_s[pl.ds(r0, wk), :], NN) / den
                lse_ref[rows, :] = jnp.broadcast_to(m + jnp.log(den), (qb, HEAD))
                return carry

            _for_blocks(nq, step)

        def classes(j, carry):
            for slot in range(par):
                residue(j * par + slot, q_all.at[slot], k_all.at[slot], v_all.at[slot])
            return carry

        lax.fori_loop(0, dil // par, classes, 0)

    def col(part):
        return pl.BlockSpec((s_, HEAD), lambda b, h, part=part: (b, col0 + part * ATTN_HEADS + h))

    tab = pl.BlockSpec((s_, HEAD), lambda b, h: (0, 0))
    out = pl.BlockSpec((s_, HEAD), lambda b, h: (b, h))
    shape = jax.ShapeDtypeStruct((b_ * s_, ATTN_OUT), F32)
    return pl.pallas_call(
        body, name=name, grid=(b_, ATTN_HEADS),
        in_specs=[col(0), col(1), col(2), tab, tab, tab],
        out_specs=[out, out],
        out_shape=[shape, shape],
        scratch_shapes=[pltpu.VMEM((par, seg, HEAD), BF16), pltpu.VMEM((par, seg + 2 * ATTN_HALF, HEAD), BF16),
                        pltpu.VMEM((par, seg + 2 * ATTN_HALF, HEAD), BF16)],
        compiler_params=_params(),
    )(proj, proj, proj, *tabs)


def _attn_bwd(proj, tabs, dog, cg, lse, b_, s_, col0, dil, name):
    seg = s_ // dil
    qb = min(128, seg)
    nq, wk = seg // qb, qb + 2 * ATTN_HALF
    scale = HEAD ** -0.5

    par = _classes_in_flight(dil, nq)

    def body(q_ref, k_ref, v_ref, c_ref, s1_ref, s2_ref, do_ref, cg_ref, lse_ref, dq_ref, dk_ref, dv_ref,
             q_all, k_all, v_all, do_all, cg_all, lse_all, dk_all, dv_all):
        k_all[...] = jnp.zeros_like(k_all)
        v_all[...] = jnp.zeros_like(v_all)

        def residue(r, q_s, k_s, v_s, do_s, cg_s, lse_s, dk_s, dv_s):
            cls = pl.ds(r, seg, stride=dil)
            c, s1, s2 = c_ref[cls, :], s1_ref[cls, :], s2_ref[cls, :]
            q_s[...] = _rope(q_ref[cls, :], c, s1, s2).astype(BF16)
            k_s[ATTN_HALF:ATTN_HALF + seg, :] = _rope(k_ref[cls, :], c, s1, s2).astype(BF16)
            v_s[ATTN_HALF:ATTN_HALF + seg, :] = v_ref[cls, :].astype(BF16)
            do_s[...] = do_ref[cls, :].astype(BF16)
            cg_s[...] = cg_ref[cls, :]
            lse_s[...] = lse_ref[cls, :]
            dk_s[...] = jnp.zeros_like(dk_s)
            dv_s[...] = jnp.zeros_like(dv_s)

            def step(i, carry):
                r0 = _block_start(i, qb)
                rows, win = pl.ds(r0, qb), pl.ds(r0, wk)
                qc, kw, vw = q_s[rows, :], k_s[win, :], v_s[win, :]
                sc = _dot(qc, kw, NT) * scale
                p = jnp.where(_window_mask(r0, qb, wk, seg), jnp.exp(sc - lse_s[rows, 0:1]), 0.0)
                dob = do_s[rows, :]
                dp = _dot(dob, vw, NT)
                ds = (p * (dp + cg_s[rows, 0:1]) * scale).astype(BF16)
                out = pl.ds(r + r0 * dil, qb, stride=dil)
                dq_ref[out, :] = _rope_bwd(_dot(ds, kw, NN), c_ref[out, :], s1_ref[out, :], s2_ref[out, :])
                dk_s[win, :] += _dot(ds, qc, TN)
                dv_s[win, :] += _dot(p.astype(BF16), dob, TN)
                return carry

            _for_blocks(nq, step)
            dk_ref[cls, :] = _rope_bwd(dk_s[ATTN_HALF:ATTN_HALF + seg, :], c, s1, s2)
            dv_ref[cls, :] = dv_s[ATTN_HALF:ATTN_HALF + seg, :]

        scratch = (q_all, k_all, v_all, do_all, cg_all, lse_all, dk_all, dv_all)

        def classes(j, carry):
            for slot in range(par):
                residue(j * par + slot, *[s.at[slot] for s in scratch])
            return carry

        lax.fori_loop(0, dil // par, classes, 0)

    def col(part):
        return pl.BlockSpec((s_, HEAD), lambda b, h, part=part: (b, col0 + part * ATTN_HEADS + h))

    tab = pl.BlockSpec((s_, HEAD), lambda b, h: (0, 0))
    out = pl.BlockSpec((s_, HEAD), lambda b, h: (b, h))
    shape = jax.ShapeDtypeStruct((b_ * s_, ATTN_OUT), F32)
    pad = seg + 2 * ATTN_HALF
    return pl.pallas_call(
        body, name=name, grid=(b_, ATTN_HEADS),
        in_specs=[col(0), col(1), col(2), tab, tab, tab, out, out, out],
        out_specs=[out, out, out],
        out_shape=[shape, shape, shape],
        scratch_shapes=[pltpu.VMEM((par, seg, HEAD), BF16), pltpu.VMEM((par, pad, HEAD), BF16),
                        pltpu.VMEM((par, pad, HEAD), BF16), pltpu.VMEM((par, seg, HEAD), BF16),
                        pltpu.VMEM((par, seg, HEAD), F32), pltpu.VMEM((par, seg, HEAD), F32),
                        pltpu.VMEM((par, pad, HEAD), F32), pltpu.VMEM((par, pad, HEAD), F32)],
        compiler_params=_params(),
    )(proj, proj, proj, *tabs, dog, cg, lse)


def _group_weights(lses):
    m = jnp.maximum(jnp.maximum(lses[0], lses[1]), lses[2])
    es = [jnp.exp(l - m) for l in lses]
    den = es[0] + es[1] + es[2]
    return [e / den for e in es]


def _combine_fwd(outs, lses, name):
    t_, w_ = outs[0].shape
    tm = _tile(t_, 512, 8)
    ng = len(outs)

    def body(*refs):
        ws = _group_weights([r[...] for r in refs[ng:2 * ng]])
        acc = ws[0] * refs[0][...]
        for g in range(1, ng):
            acc = acc + ws[g] * refs[g][...]
        refs[2 * ng][...] = acc.astype(BF16)

    row = pl.BlockSpec((tm, w_), lambda i: (i, 0))
    return pl.pallas_call(
        body, name=name, grid=(t_ // tm,), in_specs=[row] * (2 * ng), out_specs=row,
        out_shape=jax.ShapeDtypeStruct((t_, w_), BF16), compiler_params=_params(),
    )(*outs, *lses)


def _combine_bwd(dob, outs, lses, name):
    t_, w_ = outs[0].shape
    tm = _tile(t_, 512, 8)
    ng = len(outs)

    def body(*refs):
        do = refs[0][...]
        os_ = [r[...] for r in refs[1:1 + ng]]
        ws = _group_weights([r[...] for r in refs[1 + ng:1 + 2 * ng]])
        o = ws[0] * os_[0]
        for g in range(1, ng):
            o = o + ws[g] * os_[g]
        prod = do * o
        heads = [jnp.broadcast_to(jnp.sum(prod[:, h * HEAD:(h + 1) * HEAD], axis=-1, keepdims=True), (tm, HEAD))
                 for h in range(w_ // HEAD)]
        tot = jnp.concatenate(heads, axis=1)
        for g in range(ng):
            refs[1 + 2 * ng + g][...] = ws[g] * do
            refs[1 + 3 * ng + g][...] = -ws[g] * tot

    row = pl.BlockSpec((tm, w_), lambda i: (i, 0))
    shape = jax.ShapeDtypeStruct((t_, w_), F32)
    res = pl.pallas_call(
        body, name=name, grid=(t_ // tm,), in_specs=[row] * (1 + 2 * ng), out_specs=[row] * (2 * ng),
        out_shape=[shape] * (2 * ng), compiler_params=_params(),
    )(dob, *outs, *lses)
    return res[:ng], res[ng:]


def _adam_update(w, g, m, v):
    m = ADAM_B1 * m + (1.0 - ADAM_B1) * g
    v = ADAM_B2 * v + (1.0 - ADAM_B2) * (g * g)
    m_hat = m / (1.0 - ADAM_B1 ** ADAM_STEP)
    v_hat = v / (1.0 - ADAM_B2 ** ADAM_STEP)
    return -ADAM_LR * (m_hat / (jnp.sqrt(v_hat) + ADAM_EPS) + ADAM_WD * w), m, v


def _adam(w, g, m, v, name):
    r_, c_ = w.shape
    tr = _tile(r_, 256, 8)

    def body(w_ref, g_ref, m_ref, v_ref, d_ref, mo_ref, vo_ref):
        d_ref[...], mo_ref[...], vo_ref[...] = _adam_update(w_ref[...], g_ref[...], m_ref[...], v_ref[...])

    blk = pl.BlockSpec((tr, c_), lambda i: (i, 0))
    shape = jax.ShapeDtypeStruct((r_, c_), F32)
    return pl.pallas_call(
        body, name=name, grid=(r_ // tr,), in_specs=[blk] * 4, out_specs=[blk] * 3,
        out_shape=[shape] * 3, compiler_params=_params(),
    )(w, g, m, v)


def _sum_partials(recv, name):
    n_, r_, c_ = recv.shape
    tr = _tile(r_, 128, 16)

    def body(p_ref, o_ref):
        acc = p_ref[0].astype(F32)
        for i in range(1, n_):
            acc = acc + p_ref[i].astype(F32)
        o_ref[...] = acc

    return pl.pallas_call(
        body, name=name, grid=(r_ // tr,),
        in_specs=[pl.BlockSpec((n_, tr, c_), lambda i: (0, i, 0))],
        out_specs=pl.BlockSpec((tr, c_), lambda i: (i, 0)),
        out_shape=jax.ShapeDtypeStruct((r_, c_), F32), compiler_params=_params(),
    )(recv)


def _small_sum_adam(parts, w, m, v, name):
    n_, r_, c_ = parts.shape

    def body(p_ref, w_ref, m_ref, v_ref, g_ref, d_ref, mo_ref, vo_ref):
        g = p_ref[0]
        for i in range(1, n_):
            g = g + p_ref[i]
        g_ref[...] = g
        d_ref[...], mo_ref[...], vo_ref[...] = _adam_update(w_ref[...], g, m_ref[...], v_ref[...])

    shape = jax.ShapeDtypeStruct((r_, c_), F32)
    return pl.pallas_call(body, name=name, out_shape=[shape] * 4, compiler_params=_params())(parts, w, m, v)


def _my_place():
    x, y, c = lax.axis_index("x"), lax.axis_index("y"), lax.axis_index("c")
    return x, y, c


def _peer(x, y, c, d):
    px = 1 - x if d & 4 else x
    py = 1 - y if d & 2 else y
    pc = 1 - c if d & 1 else c
    return (px, py, pc), 4 * px + 2 * py + pc


HBM_SPEC =pl.BlockSpec(memory_space=pltpu.HBM)
SEM_SPEC = pl.BlockSpec(memory_space=pltpu.SEMAPHORE)
EFFECT = pltpu.SideEffectType.DATAFLOW_SIDE_EFFECTING


def _in_hbm(a):
    return pltpu.with_memory_space_constraint(a, pltpu.HBM)


def _token_shape():
    return jax.ShapeDtypeStruct((8, HEAD), F32)


SIBLING = 1
OTHER_CHIPS = (4, 2, 6)


def _rows_of(ref, num, rows):
    return ref.at[pl.ds(pl.multiple_of(num * rows, 16), rows), :]


def _gather_start(shards, name):
    nw = len(shards)
    lands = [lax.empty((N_DEV * s.shape[0], s.shape[1]), s.dtype) for s in shards]
    n_to = 1 + len(OTHER_CHIPS)

    def body(*refs):
        ins, lnd = refs[:nw], refs[nw:2 * nw]
        send, from_sib, from_chips, own = (refs[(2 + i) * nw:(3 + i) * nw] for i in range(4))
        token = refs[8 * nw]
        x, y, c = _my_place()
        me = 4 * x + 2 * y + c
        for k in range(nw):
            mine = _rows_of(lnd[k], me, shards[k].shape[0])
            pltpu.make_async_copy(ins[k], mine, own[k]).start()
            for i, d in enumerate((SIBLING,) + OTHER_CHIPS):
                place, _ = _peer(x, y, c, d)
                pltpu.make_async_remote_copy(
                    src_ref=ins[k], dst_ref=mine, send_sem=send[k].at[i],
                    recv_sem=from_sib[k] if i == 0 else from_chips[k].at[i - 1],
                    device_id=place, device_id_type=MESH).start()
        token[...] = jnp.zeros_like(token)

    dma = pltpu.SemaphoreType.DMA
    sems = [dma((n_to,))] * nw + [dma(())] * nw + [dma((len(OTHER_CHIPS),))] * nw + [dma(())] * nw
    thru = [pltpu.HBM(a.shape, a.dtype) for a in list(shards) + lands]
    res = pl.pallas_call(
        body, name=name, out_shape=(*sems, *thru, _token_shape()),
        in_specs=[HBM_SPEC] * (2 * nw),
        out_specs=(*([SEM_SPEC] * (4 * nw)), *([HBM_SPEC] * (2 * nw)), pl.BlockSpec(memory_space=pltpu.VMEM)),
        input_output_aliases={i: 4 * nw + i for i in range(2 * nw)},
        compiler_params=pltpu.CompilerParams(has_side_effects=EFFECT),
    )(*[_in_hbm(s) for s in shards], *[_in_hbm(l) for l in lands])
    return [dict(send=res[k], from_sib=res[nw + k], from_chips=res[2 * nw + k], own=res[3 * nw + k],
                 src=res[4 * nw + k], land=res[5 * nw + k]) for k in range(nw)], res[6 * nw]


def _gather_forward(pending, after, name):
    rows = pending["src"].shape[0]
    n_fw = len(OTHER_CHIPS)

    def body(land_ref, from_chips, after_ref, fw_send, fw_recv, land_thru):
        x, y, c = _my_place()
        sibling, _ = _peer(x, y, c, SIBLING)
        for j, d in enumerate(OTHER_CHIPS):
            _, num = _peer(x, y, c, d)
            block = _rows_of(land_ref, num, rows)
            pltpu.make_async_remote_copy(
                src_ref=block, dst_ref=block, send_sem=fw_send.at[j], recv_sem=from_chips.at[j],
                device_id=sibling, device_id_type=MESH).wait_recv()
            pltpu.make_async_remote_copy(
                src_ref=block, dst_ref=block, send_sem=fw_send.at[j], recv_sem=fw_recv.at[j],
                device_id=sibling, device_id_type=MESH).start()

    land = pending["land"]
    dma = pltpu.SemaphoreType.DMA
    fw_send, fw_recv, land = pl.pallas_call(
        body, name=name, out_shape=(dma((n_fw,)), dma((n_fw,)), pltpu.HBM(land.shape, land.dtype)),
        in_specs=(HBM_SPEC, SEM_SPEC, pl.BlockSpec(memory_space=pl.ANY)),
        out_specs=(SEM_SPEC, SEM_SPEC, HBM_SPEC), input_output_aliases={0: 2},
        compiler_params=pltpu.CompilerParams(has_side_effects=EFFECT),
    )(land, pending["from_chips"], after)
    return dict(pending, land=land, fw_send=fw_send, fw_recv=fw_recv)


def _gather_wait(pending, name):
    rows = pending["src"].shape[0]

    def body(src_ref, land_ref, send, from_sib, own, fw_send, fw_recv, src_dead, got):
        x, y, c = _my_place()
        me = 4 * x + 2 * y + c
        sibling, sib_num = _peer(x, y, c, SIBLING)
        mine = _rows_of(land_ref, me, rows)
        pltpu.make_async_copy(src_ref, mine, own).wait()
        for i in range(1 + len(OTHER_CHIPS)):
            pltpu.make_async_remote_copy(
                src_ref=src_ref, dst_ref=mine, send_sem=send.at[i], recv_sem=from_sib,
                device_id=sibling, device_id_type=MESH).wait_send()
        theirs = _rows_of(land_ref, sib_num, rows)
        pltpu.make_async_remote_copy(
            src_ref=src_ref, dst_ref=theirs, send_sem=send.at[0], recv_sem=from_sib,
            device_id=sibling, device_id_type=MESH).wait_recv()
        for j, d in enumerate(OTHER_CHIPS):
            _, num = _peer(x, y, c, d)
            sent = _rows_of(land_ref, num, rows)
            _, got_num = _peer(x, y, c, d | SIBLING)
            arrived = _rows_of(land_ref, got_num, rows)
            cp = pltpu.make_async_remote_copy(
                src_ref=sent, dst_ref=arrived, send_sem=fw_send.at[j], recv_sem=fw_recv.at[j],
                device_id=sibling, device_id_type=MESH)
            cp.wait_send()
            cp.wait_recv()

    src, land = pending["src"], pending["land"]
    return pl.pallas_call(
        body, name=name, out_shape=(pltpu.HBM(src.shape, src.dtype), pltpu.HBM(land.shape, land.dtype)),
        in_specs=(HBM_SPEC, HBM_SPEC) + (SEM_SPEC,) * 5,
        out_specs=(HBM_SPEC, HBM_SPEC), input_output_aliases={0: 0, 1: 1},
        compiler_params=pltpu.CompilerParams(has_side_effects=EFFECT),
    )(src, land, pending["send"], pending["from_sib"], pending["own"], pending["fw_send"], pending["fw_recv"])[1]


def _gather_end(pending, after, n):
    return _gather_wait(_gather_forward(pending, after, f"gather_forward_{n}"), f"gather_wait_{n}")


def _scatter_start(full, name):
    rows, cols = full.shape[0] // N_DEV, full.shape[1]
    land = lax.empty((N_DEV, rows, cols), full.dtype)

    def body(full_ref, land_ref, send, recv, own, full_thru, land_thru, token):
        x, y, c = _my_place()
        me = 4 * x + 2 * y + c
        slab = land_ref.at[me]
        pltpu.make_async_copy(full_ref.at[pl.ds(pl.multiple_of(me * rows, 16), rows), :], slab, own).start()
        for d in range(1, N_DEV):
            place, num = _peer(x, y, c, d)
            pltpu.make_async_remote_copy(
                src_ref=full_ref.at[pl.ds(pl.multiple_of(num * rows, 16), rows), :], dst_ref=slab,
                send_sem=send.at[d - 1], recv_sem=recv.at[d - 1], device_id=place, device_id_type=MESH).start()
        token[...] = jnp.zeros_like(token)

    res = pl.pallas_call(
        body, name=name,
        out_shape=(pltpu.SemaphoreType.DMA((N_DEV - 1,)), pltpu.SemaphoreType.DMA((N_DEV - 1,)),
                   pltpu.SemaphoreType.DMA(()),
                   pltpu.HBM(full.shape, full.dtype), pltpu.HBM(land.shape, land.dtype), _token_shape()),
        in_specs=(HBM_SPEC, HBM_SPEC),
        out_specs=(SEM_SPEC, SEM_SPEC, SEM_SPEC, HBM_SPEC, HBM_SPEC, pl.BlockSpec(memory_space=pltpu.VMEM)),
        input_output_aliases={0: 3, 1: 4},
        compiler_params=pltpu.CompilerParams(has_side_effects=EFFECT),
    )(_in_hbm(full), _in_hbm(land))
    return dict(send=res[0], recv=res[1], own=res[2], src=res[3], land=res[4]), res[5]


def _scatter_wait(pending, after, name):
    rows = pending["land"].shape[1]

    def body(src_ref, land_ref, send, recv, own, after_ref, src_dead, got):
        x, y, c = _my_place()
        me = 4 * x + 2 * y + c
        pltpu.make_async_copy(src_ref.at[pl.ds(pl.multiple_of(me * rows, 16), rows), :], land_ref.at[me], own).wait()
        for d in range(1, N_DEV):
            place, num = _peer(x, y, c, d)
            cp = pltpu.make_async_remote_copy(
                src_ref=src_ref.at[pl.ds(pl.multiple_of(num * rows, 16), rows), :], dst_ref=land_ref.at[me],
                send_sem=send.at[d - 1], recv_sem=recv.at[d - 1], device_id=place, device_id_type=MESH)
            cp.wait_send()
            cp.wait_recv()

    src, land = pending["src"], pending["land"]
    return pl.pallas_call(
        body, name=name, out_shape=(pltpu.HBM(src.shape, src.dtype), pltpu.HBM(land.shape, land.dtype)),
        in_specs=(HBM_SPEC, HBM_SPEC, SEM_SPEC, SEM_SPEC, SEM_SPEC, pl.BlockSpec(memory_space=pl.ANY)),
        out_specs=(HBM_SPEC, HBM_SPEC), input_output_aliases={0: 0, 1: 1},
        compiler_params=pltpu.CompilerParams(has_side_effects=EFFECT),
    )(src, land, pending["send"], pending["recv"], pending["own"], after)[1]


BIG = ("ffn1_w_in", "ffn1_w_out", "mix_w_in", "w_branch_a", "w_branch_b", "mix_w_out", "ffn2_w_in", "ffn2_w_out")
TRANSPOSED = ("ffn1_w_in", "mix_w_in", "w_branch_b", "ffn2_w_in")
SMALL = ("ln1_g", "ln1_b", "ln2_g", "ln2_b", "ln3_g", "ln3_b", "hgrn_norm_g", "hgrn_lb_fwd", "hgrn_lb_bwd")
SMALL_ROWS = 16


def _local_step(x, target, weight, emit, emit_small, sp):
    b_, s_, d_ = x.shape
    t_ = b_ * s_
    x2, tgt = x.reshape(t_, d_), target.reshape(t_, d_)
    xb = x2.astype(BF16)
    w1i = weight("ffn1_w_in", xb)
    g1, u1, a1 = _ffn_in_fwd(xb, w1i, "ffn1_in")
    w1o = weight("ffn1_w_out", a1)
    r1, h1, h1b = _mm_res_ln_fwd(a1, w1o, x2, sp["ln1_g"], sp["ln1_b"], 0.5, "ffn1_out_ln1")
    wmx = weight("mix_w_in", h1b)
    proj = _mm_nt(h1b, wmx, F32, "mix_in")
    ya_in, o_sum = _hgrn_fwd(proj, sp["hgrn_lb_fwd"], sp["hgrn_lb_bwd"], sp["hgrn_norm_g"], b_, s_, d_, "hgrn_fwd")
    tabs = _rope_tables(s_)
    outs, lses = [], []
    group_col = [(5 * d_ + gi * QKV_GROUP) // HEAD for gi in range(len(ATTN_GROUPS))]
    for gi, (_, dil) in enumerate(ATTN_GROUPS):
        o_g, lse_g = _attn_fwd(proj, tabs, b_, s_, group_col[gi], dil, f"attn_fwd_{gi}")
        outs.append(o_g)
        lses.append(lse_g)
    ob = _combine_fwd(outs, lses, "attn_combine")
    wa, wb = weight("w_branch_a", ya_in), weight("w_branch_b", ob)
    ya, yb, z = _gate_out_fwd(ya_in, ob, wa, wb, proj, d_, "branch_gate")
    wo = weight("mix_w_out", z)
    r2, h2, h2b = _mm_res_ln_fwd(z, wo, h1, sp["ln2_g"], sp["ln2_b"], 1.0, "mix_out_ln2")
    w2i = weight("ffn2_w_in", h2b)
    g2, u2, a2 = _ffn_in_fwd(h2b, w2i, "ffn2_in")
    w2o = weight("ffn2_w_out", a2)
    dr3, dr3b, dg3, db3, loss = _mm_res_loss_bwd(a2, w2o, h2, tgt, sp["ln3_g"], sp["ln3_b"], 0.5, "ffn2_out_loss")
    loss = lax.psum(loss[0, 0], ("x", "y", "c"))
    dep = emit("ffn2_w_out", _mm_tn(a2, dr3b, 0.5, "d_ffn2_w_out")) + loss
    dgate2, dup2 = _ffn_mid_bwd(dr3b, w2o, g2, u2, 0.5, "ffn2_mid_bwd", dep)
    du2 = (dgate2, dup2)
    dep = emit("ffn2_w_in", _mm_tn(du2, h2b, 1.0, "d_ffn2_w_in"))
    dr2, dr2b, dg2, db2 = _mm_nn_res_lnbwd(du2, w2i, dr3, r2, sp["ln2_g"], "ffn2_in_bwd_ln2", dep)
    dep = emit("mix_w_out", _mm_tn(z, dr2b, 1.0, "d_mix_w_out"))
    dya, dyb, dga, dgb = _dz_gate_bwd(dr2b, wo, proj, ya, yb, d_, "branch_gate_bwd", dep)
    dep = emit("w_branch_a", _mm_tn(ya_in, dya, 1.0, "d_w_branch_a"))
    dya_in = _mm_nt(dya, wa, F32, "branch_a_bwd", dep)
    dep = emit("w_branch_b", _mm_tn(dyb, ob, 1.0, "d_w_branch_b"))
    dob = _mm_nn(dyb, wb, F32, "branch_b_bwd", dep)
    dhq, dhff, dhfb, dhi, dhog, dng, dlbf, dlbb = _hgrn_bwd(
        proj, sp["hgrn_lb_fwd"], sp["hgrn_lb_bwd"], sp["hgrn_norm_g"], o_sum, dya_in, b_, s_, d_, "hgrn_bwd")
    dogs, cgs = _combine_bwd(dob, outs, lses, "attn_combine_bwd")
    dqkv = []
    for gi, (_, dil) in enumerate(ATTN_GROUPS):
        dqkv += _attn_bwd(proj, tabs, dogs[gi], cgs[gi], lses[gi], b_, s_, group_col[gi], dil, f"attn_bwd_{gi}")
    dproj = jnp.concatenate([dhq, dhff, dhfb, dhi, dhog] + [t.astype(BF16) for t in dqkv] + [dga, dgb], axis=1)
    dep = emit("mix_w_in", _mm_tn(dproj, h1b, 1.0, "d_mix_w_in"))
    dr1, dr1b, dg1, db1 = _mm_nn_res_lnbwd(dproj, wmx, dr2, r1, sp["ln1_g"], "mix_in_bwd_ln1", dep)
    dep_small = emit_small({"ln1_g": dg1, "ln1_b": db1, "ln2_g": dg2, "ln2_b": db2, "ln3_g": dg3, "ln3_b": db3,
                            "hgrn_norm_g": dng, "hgrn_lb_fwd": dlbf, "hgrn_lb_bwd": dlbb})
    dep = emit("ffn1_w_out", _mm_tn(a1, dr1b, 0.5, "d_ffn1_w_out")) + dep_small
    dgate1, dup1 = _ffn_mid_bwd(dr1b, w1o, g1, u1, 0.5, "ffn1_mid_bwd", dep)
    du1 = (dgate1, dup1)
    dep = emit("ffn1_w_in", _mm_tn(du1, xb, 1.0, "d_ffn1_w_in"))
    grad_x = _mm_nn_res(du1, w1i, dr1, "ffn1_in_bwd", dep)
    return loss, grad_x.reshape(b_, s_, d_)


def _pack_small(vals):
    rows = jnp.concatenate([vals[n] for n in SMALL], axis=0)
    return jnp.pad(rows, ((0, SMALL_ROWS - rows.shape[0]), (0, 0)))


def _unpack_small(packed):
    out, r = {}, 0
    for n in SMALL:
        k = 2 if n.startswith("hgrn_lb") else 1
        out[n] = packed[r:r + k]
        r += k
    return out


def kernel(x, ffn1_w_in, ffn1_w_out, ln1_g, ln1_b, mix_w_in, hgrn_lb_fwd, hgrn_lb_bwd, hgrn_norm_g, w_branch_a, w_branch_b, mix_w_out, ln2_g, ln2_b, ffn2_w_in, ffn2_w_out, ln3_g, ln3_b, loss_target, m_ffn1_w_in, m_ffn1_w_out, m_ln1_g, m_ln1_b, m_mix_w_in, m_hgrn_lb_fwd, m_hgrn_lb_bwd, m_hgrn_norm_g, m_w_branch_a, m_w_branch_b, m_mix_w_out, m_ln2_g, m_ln2_b, m_ffn2_w_in, m_ffn2_w_out, m_ln3_g, m_ln3_b, v_ffn1_w_in, v_ffn1_w_out, v_ln1_g, v_ln1_b, v_mix_w_in, v_hgrn_lb_fwd, v_hgrn_lb_bwd, v_hgrn_norm_g, v_w_branch_a, v_w_branch_b, v_mix_w_out, v_ln2_g, v_ln2_b, v_ffn2_w_in, v_ffn2_w_out, v_ln3_g, v_ln3_b):
    args = dict(locals())
    big_w = {n: args[n][0] for n in BIG}
    sp = {n: args[n] for n in SMALL}
    def rows_bf16(n, zero=0.0):
        w = big_w[n] + zero
        return (w.T if n in TRANSPOSED else w).astype(BF16)

    first, rest = BIG[:2], BIG[2:]
    pending, token = _gather_start([rows_bf16(n) for n in first], "gather_start_ffn1")
    gathering = dict(zip(first, pending))
    pending, all_started = _gather_start([rows_bf16(n, token[0, 0]) for n in rest], "gather_start_rest")
    gathering.update(zip(rest, pending))
    scattering = {}

    def weight(n, after):
        if n == first[0]:
            after = all_started
        return _gather_end(gathering[n], after, n)

    def emit(n, grad):
        scattering[n], token = _scatter_start(grad, f"scatter_start_{n}")
        return token

    def emit_small(grads):
        pending, token = _gather_start([_pack_small(grads)], "gather_start_small")
        scattering["small"] = pending[0]
        return token

    loss, grad_x = _local_step(x, loss_target, weight, emit, emit_small, sp)
    out_g, out_d, out_m, out_v = {}, {}, {}, {}
    done = grad_x
    for n in ("ffn2_w_out", "ffn2_w_in", "mix_w_out", "w_branch_a", "w_branch_b", "mix_w_in", "small",
              "ffn1_w_out", "ffn1_w_in"):
        if n == "small":
            parts = _gather_end(scattering[n], done, n)
            res = _small_sum_adam(parts.reshape(N_DEV, SMALL_ROWS, parts.shape[1]), _pack_small(sp),
                                  _pack_small({n: args["m_" + n] for n in SMALL}),
                                  _pack_small({n: args["v_" + n] for n in SMALL}), "small_adam")
            sg, sd, sm, sv = (_unpack_small(r) for r in res)
            out_g.update(sg), out_d.update(sd), out_m.update(sm), out_v.update(sv)
            done = res[3]
            continue
        g = _sum_partials(_scatter_wait(scattering[n], done, f"scatter_wait_{n}"), f"sum_{n}")
        if n in TRANSPOSED:
            g = g.T
        d_w, m_w, v_w = _adam(big_w[n], g, args["m_" + n][0], args["v_" + n][0], f"adam_{n}")
        out_g[n], out_d[n], out_m[n], out_v[n] = g[None], d_w[None], m_w[None], v_w[None]
        done = v_w
    order = ("ffn1_w_in", "ffn1_w_out", "ln1_g", "ln1_b", "mix_w_in", "hgrn_lb_fwd", "hgrn_lb_bwd", "hgrn_norm_g",
             "w_branch_a", "w_branch_b", "mix_w_out", "ln2_g", "ln2_b", "ffn2_w_in", "ffn2_w_out", "ln3_g", "ln3_b")
    return (loss, grad_x, *[out_g[n] for n in order], *[out_d[n] for n in order],
            *[out_m[n] for n in order], *[out_v[n] for n in order])
```
